```python
import jax, jax.numpy as jnp
from jax import lax
import numpy as np

D_MODEL = 1024
BATCH = 2
SEQ = 8192
DEPTH = 1

D_MIX = D_MODEL
GLA_HEADS = 4
GLA_DV = (D_MIX // 2) // GLA_HEADS
GLA_DK = GLA_DV // 2
GLA_RANK = 16
GLA_GATE_NORMALIZER = 16.0
GLA_CHUNK = 64
ATT_Q_HEADS = 8
ATT_KV_HEADS = 2
ATT_HEAD_DIM = (D_MIX // 2) // ATT_Q_HEADS
ATT_WINDOW = 128
ATT_BLOCK = 128
ROT_DIM = ATT_HEAD_DIM // 4
ROPE_THETA = 500000.0
N_EXPERTS = 32
TOP_K = 4
D_FF = D_MODEL
SWIGLU_LIMIT = 7.0
SWIGLU_ALPHA = 1.702
MOE_BLOCK = 128
NORM_EPS = 1e-6
NEG_INF = -1e30
IN_SPLITS = (GLA_HEADS * GLA_DK, GLA_HEADS * GLA_DK, GLA_HEADS * GLA_DV, GLA_HEADS * GLA_DV, 2 * GLA_RANK, ATT_Q_HEADS * ATT_HEAD_DIM, ATT_KV_HEADS * ATT_HEAD_DIM, ATT_KV_HEADS * ATT_HEAD_DIM)
D_IN = sum(IN_SPLITS)
D_MIX_OUT = GLA_HEADS * GLA_DV + ATT_Q_HEADS * ATT_HEAD_DIM

kernel_name = 'hybrid_gla_swa_moe_block'


def rms_norm(x, g):
    xf = x.astype(jnp.float32)
    y = xf * lax.rsqrt(jnp.mean(xf * xf, axis=-1, keepdims=True) + NORM_EPS)
    return (y * g.astype(jnp.float32)).astype(x.dtype)


def partial_rotary(x, positions):
    half = ROT_DIM // 2
    inv_freq = ROPE_THETA ** (-2.0 * jnp.arange(half, dtype=jnp.float32) / ROT_DIM)
    ang = positions.astype(jnp.float32)[:, None] * inv_freq[None, :]
    cos = jnp.cos(ang)[None, :, None, :]
    sin = jnp.sin(ang)[None, :, None, :]
    x1 = x[..., :half].astype(jnp.float32)
    x2 = x[..., half:ROT_DIM].astype(jnp.float32)
    rot = jnp.concatenate([x1 * cos - x2 * sin, x2 * cos + x1 * sin], axis=-1).astype(x.dtype)
    return jnp.concatenate([rot, x[..., ROT_DIM:]], axis=-1)


def gla_one_direction(q, k, v, log_a):
    B, S, H, DK = q.shape
    DV = v.shape[-1]
    C = GLA_CHUNK
    N = S // C

    def chunks(t):
        return t.astype(jnp.float32).reshape(B, N, C, H, t.shape[-1]).transpose(1, 0, 3, 2, 4)

    qc, kc, vc, ac = chunks(q), chunks(k), chunks(v), chunks(log_a)
    b = jnp.cumsum(ac, axis=3)
    b_last = b[:, :, :, -1:, :]
    b_ref = b[:, :, :, C // 2 - 1:C // 2, :]
    scores = jnp.einsum('nbhid,nbhjd->nbhij', qc * jnp.exp(b - b_ref), kc * jnp.exp(b_ref - b))
    causal_in_chunk = jnp.tril(jnp.ones((C, C), dtype=bool))
    scores = jnp.where(causal_in_chunk, scores, 0.0)
    o_intra = jnp.einsum('nbhij,nbhjv->nbhiv', scores, vc)
    q_inter = qc * jnp.exp(b)
    k_state = kc * jnp.exp(b_last - b)
    chunk_decay = jnp.exp(b_last[:, :, :, 0, :])

    def step(state, inp):
        q_n, k_n, v_n, d_n = inp
        o_n = jnp.einsum('bhid,bhdv->bhiv', q_n, state)
        state = state * d_n[..., None] + jnp.einsum('bhjd,bhjv->bhdv', k_n, v_n)
        return state, o_n

    state0 = jnp.zeros((B, H, DK, DV), jnp.float32)
    _, o_inter = lax.scan(step, state0, (q_inter, k_state, vc, chunk_decay))
    o = o_intra + o_inter
    return o.transpose(1, 0, 3, 2, 4).reshape(B, S, H, DV)


def window_attention(q, k, v, sink):
    B, S, HQ, HD = q.shape
    HKV = k.shape[2]
    G = HQ // HKV
    QB = ATT_BLOCK
    NB = S // QB
    pad = ((0, 0), (QB, QB), (0, 0), (0, 0))

    def band(t):
        tp = jnp.pad(t, pad).reshape(B, NB + 2, QB, HKV, HD)
        return jnp.concatenate([tp[:, :-2], tp[:, 1:-1], tp[:, 2:]], axis=2)

    kb, vb = band(k), band(v)
    qb = q.reshape(B, NB, QB, HKV, G, HD)
    s = jnp.einsum('bnikgd,bnjkd->bnkgij', qb, kb).astype(jnp.float32) * (HD ** -0.5)
    blk = jnp.arange(NB)[:, None, None] * QB
    qpos = blk + jnp.arange(QB)[None, :, None]
    kpos = blk - QB + jnp.arange(3 * QB)[None, None, :]
    valid = (jnp.abs(kpos - qpos) <= ATT_WINDOW) & (kpos >= 0) & (kpos < S)
    s = jnp.where(valid[None, :, None, None], s, NEG_INF)
    sink_logit = jnp.broadcast_to(sink.astype(jnp.float32).reshape(HKV, G)[None, None, :, :, None, None], s.shape[:-1] + (1,))
    p = jax.nn.softmax(jnp.concatenate([s, sink_logit], axis=-1), axis=-1)[..., :-1]
    o = jnp.einsum('bnkgij,bnjkd->bnikgd', p.astype(v.dtype), vb)
    return o.reshape(B, S, HQ, HD)


def hybrid_mixer(h, w_in, w_gk_fwd, b_gk_fwd, w_gk_bwd, b_gk_bwd, g_gla_out, attn_sink, w_out):
    B, S, _ = h.shape
    proj = h @ w_in
    split_at = np.cumsum(IN_SPLITS)[:-1].tolist()
    gq, gk, gv, gg, glr, aq, ak, av = jnp.split(proj, split_at, axis=-1)
    q = (gq * (GLA_DK ** -0.5)).reshape(B, S, GLA_HEADS, GLA_DK)
    k = gk.reshape(B, S, GLA_HEADS, GLA_DK)
    v = gv.reshape(B, S, GLA_HEADS, GLA_DV)
    lr_f, lr_b = jnp.split(glr, 2, axis=-1)
    la_f = (jax.nn.log_sigmoid((lr_f @ w_gk_fwd + b_gk_fwd).astype(jnp.float32)) / GLA_GATE_NORMALIZER).reshape(B, S, GLA_HEADS, GLA_DK)
    la_b = (jax.nn.log_sigmoid((lr_b @ w_gk_bwd + b_gk_bwd).astype(jnp.float32)) / GLA_GATE_NORMALIZER).reshape(B, S, GLA_HEADS, GLA_DK)
    o_f = gla_one_direction(q, k, v, la_f)
    o_b = jnp.flip(gla_one_direction(jnp.flip(q, 1), jnp.flip(k, 1), jnp.flip(v, 1), jnp.flip(la_b, 1)), 1)
    o_gla = (o_f + o_b).astype(h.dtype)
    o_gla = rms_norm(o_gla, g_gla_out) * jax.nn.silu(gg.reshape(B, S, GLA_HEADS, GLA_DV))
    positions = jnp.arange(S)
    qa = partial_rotary(aq.reshape(B, S, ATT_Q_HEADS, ATT_HEAD_DIM), positions)
    ka = partial_rotary(ak.reshape(B, S, ATT_KV_HEADS, ATT_HEAD_DIM), positions)
    va = av.reshape(B, S, ATT_KV_HEADS, ATT_HEAD_DIM)
    o_att = window_attention(qa, ka, va, attn_sink)
    o = jnp.concatenate([o_gla.reshape(B, S, GLA_HEADS * GLA_DV), o_att.reshape(B, S, ATT_Q_HEADS * ATT_HEAD_DIM)], axis=-1)
    return o @ w_out


def moe_ffn(h, w_router, b_router, w_gate_up, b_gate_up, w_down, b_down):
    B, S, D = h.shape
    T = B * S
    A = T * TOP_K
    xf = h.reshape(T, D)
    logits = (xf @ w_router + b_router).astype(jnp.float32)
    top_v, top_i = lax.top_k(logits, TOP_K)
    gates = jax.nn.softmax(top_v, axis=-1)
    e_flat = top_i.reshape(A)
    tok_flat = jnp.arange(A, dtype=jnp.int32) // TOP_K
    order = jnp.argsort(e_flat)
    e_sorted = e_flat[order]
    counts = jnp.bincount(e_flat, length=N_EXPERTS)
    padded = (counts + MOE_BLOCK - 1) // MOE_BLOCK * MOE_BLOCK
    pad_end = jnp.cumsum(padded)
    pad_start = pad_end - padded
    start = jnp.cumsum(counts) - counts
    dest = pad_start[e_sorted] + jnp.arange(A, dtype=jnp.int32) - start[e_sorted]
    R = A + N_EXPERTS * MOE_BLOCK
    n_blocks = R // MOE_BLOCK
    row_tok = jnp.zeros((R,), jnp.int32).at[dest].set(tok_flat[order])
    row_gate = jnp.zeros((R,), jnp.float32).at[dest].set(gates.reshape(A)[order])
    block_expert = jnp.minimum(jnp.searchsorted(pad_end, jnp.arange(n_blocks) * MOE_BLOCK, side='right'), N_EXPERTS - 1)
    xs = xf[row_tok].reshape(n_blocks, MOE_BLOCK, D)

    def expert_block(args):
        xb, e = args
        gu = xb @ w_gate_up[e] + b_gate_up[e]
        gate, up = jnp.split(gu, 2, axis=-1)
        gate = jnp.minimum(gate, SWIGLU_LIMIT)
        up = jnp.clip(up, -SWIGLU_LIMIT, SWIGLU_LIMIT)
        act = (up + 1.0) * gate * jax.nn.sigmoid(SWIGLU_ALPHA * gate)
        return act @ w_down[e] + b_down[e]

    ys = lax.map(expert_block, (xs, block_expert)).reshape(R, D)
    out = jnp.zeros((T, D), h.dtype).at[row_tok].add(ys * row_gate[:, None].astype(ys.dtype))
    return out.reshape(B, S, D)


def setup_inputs(seed: int = 0) -> dict:
    key = jax.random.key(seed)
    ks = jax.random.split(key, 24)
    f32 = jnp.float32
    L, D = DEPTH, D_MODEL
    gk_w = GLA_HEADS * GLA_DK

    def nrm(k, shape, scale):
        return jax.random.normal(k, shape, f32) * scale

    return {
        'x': nrm(ks[0], (BATCH, SEQ, D), 1.0),
        'c': nrm(ks[1], (BATCH, D), 1.0),
        'w_ada': nrm(ks[2], (L, D, 6 * D), D ** -0.5),
        'b_ada': nrm(ks[3], (L, 6 * D), 0.02),
        'g_pre_mix': 1.0 + nrm(ks[4], (L, D), 0.02),
        'g_post_mix': 1.0 + nrm(ks[5], (L, D), 0.02),
        'w_in': nrm(ks[6], (L, D, D_IN), D ** -0.5),
        'w_gk_fwd': nrm(ks[7], (L, GLA_RANK, gk_w), GLA_RANK ** -0.5),
        'b_gk_fwd': nrm(ks[8], (L, gk_w), 0.1),
        'w_gk_bwd': nrm(ks[9], (L, GLA_RANK, gk_w), GLA_RANK ** -0.5),
        'b_gk_bwd': nrm(ks[10], (L, gk_w), 0.1),
        'g_gla_out': 1.0 + nrm(ks[11], (L, GLA_DV), 0.02),
        'attn_sink': nrm(ks[12], (L, ATT_Q_HEADS), 1.0),
        'w_out': nrm(ks[13], (L, D_MIX_OUT, D), D_MIX_OUT ** -0.5),
        'g_pre_ffn': 1.0 + nrm(ks[14], (L, D), 0.02),
        'g_post_ffn': 1.0 + nrm(ks[15], (L, D), 0.02),
        'w_router': nrm(ks[16], (L, D, N_EXPERTS), D ** -0.5),
        'b_router': nrm(ks[17], (L, N_EXPERTS), 0.01),
        'w_gate_up': nrm(ks[18], (L, N_EXPERTS, D, 2 * D_FF), D ** -0.5),
        'b_gate_up': nrm(ks[19], (L, N_EXPERTS, 2 * D_FF), 0.01),
        'w_down': nrm(ks[20], (L, N_EXPERTS, D_FF, D), D_FF ** -0.5),
        'b_down': nrm(ks[21], (L, N_EXPERTS, D), 0.01),
    }


def reference(x, c, w_ada, b_ada, g_pre_mix, g_post_mix, w_in, w_gk_fwd, b_gk_fwd, w_gk_bwd, b_gk_bwd, g_gla_out, attn_sink, w_out, g_pre_ffn, g_post_ffn, w_router, b_router, w_gate_up, b_gate_up, w_down, b_down):
    B = x.shape[0]
    D = x.shape[-1]
    c_act = jax.nn.silu(c)
    for l in range(DEPTH):
        mod = (c_act @ w_ada[l] + b_ada[l]).reshape(B, 6, D)[:, :, None, :]
        shift1, scale1, gate1, shift2, scale2, gate2 = (mod[:, i] for i in range(6))
        h = rms_norm(x, g_pre_mix[l]) * (1.0 + scale1) + shift1
        y = hybrid_mixer(h, w_in[l], w_gk_fwd[l], b_gk_fwd[l], w_gk_bwd[l], b_gk_bwd[l], g_gla_out[l], attn_sink[l], w_out[l])
        x = x + gate1 * rms_norm(y, g_post_mix[l])
        h = rms_norm(x, g_pre_ffn[l]) * (1.0 + scale2) + shift2
        y = moe_ffn(h, w_router[l], b_router[l], w_gate_up[l], b_gate_up[l], w_down[l], b_down[l])
        x = x + gate2 * rms_norm(y, g_post_ffn[l])
    return x
```

```python
import functools

import jax
import jax.numpy as jnp
import numpy as np
from jax import lax
from jax.experimental import pallas as pl
from jax.experimental.pallas import tpu as pltpu

F32 = jnp.float32
BF16 = jnp.bfloat16
I32 = jnp.int32

D_MODEL = 1024
BATCH = 2
SEQ = 8192
TOKENS = BATCH * SEQ
GLA_HEADS = 4
GLA_DV = 128
GLA_DK = 64
GLA_RANK = 16
GLA_GATE_NORMALIZER = 16.0
GLA_CHUNK = 64
ATT_Q_HEADS = 8
ATT_KV_HEADS = 2
ATT_HEAD_DIM = 64
ATT_WINDOW = 128
ATT_BLOCK = 128
ROT_DIM = 16
ROPE_THETA = 500000.0
N_EXPERTS = 32
TOP_K = 4
D_FF = 1024
SWIGLU_LIMIT = 7.0
SWIGLU_ALPHA = 1.702
NORM_EPS = 1e-6
NEG_INF = -1e30

LANES = 128
SUBLANES = 8
ROW_TILES = D_MODEL // LANES

TM_IN = 512
GLA_GROUP = 4
TM_POST = 256
MOE_BM = 256
MOE_ROWS = TOKENS * TOP_K + N_EXPERTS * MOE_BM
MOE_NB = MOE_ROWS // MOE_BM
MOE_NB_PAD = ((MOE_NB + LANES - 1) // LANES) * LANES
TM_DISP = 256
TM_COMB = 128

NT_DIMS = (((1,), (1,)), ((), ()))
TN_DIMS = (((0,), (0,)), ((), ()))


def _params(semantics, vmem_mib):
    return pltpu.CompilerParams(dimension_semantics=semantics, vmem_limit_bytes=vmem_mib * 1024 * 1024)


def _rms(x, g):
    return x * lax.rsqrt(jnp.mean(x * x, axis=-1, keepdims=True) + NORM_EPS) * g


def _silu(x):
    return x * jax.nn.sigmoid(x)


def _ada_body(c_ref, w_ref, b_ref, o_ref):
    ca = _silu(c_ref[...]).astype(BF16)
    o_ref[...] = jnp.dot(ca, w_ref[...].astype(BF16), preferred_element_type=F32) + b_ref[...]


def _ada(c_pad, w_ada, b_ada):
    d = D_MODEL
    return pl.pallas_call(
        _ada_body,
        grid=(6,),
        in_specs=[
            pl.BlockSpec((SUBLANES, d), lambda j: (0, 0)),
            pl.BlockSpec((d, d), lambda j: (0, j)),
            pl.BlockSpec((1, d), lambda j: (0, j)),
        ],
        out_specs=pl.BlockSpec((SUBLANES, d), lambda j: (0, j)),
        out_shape=jax.ShapeDtypeStruct((SUBLANES, 6 * d), F32),
        compiler_params=_params(("arbitrary",), 32),
        name="ada",
    )(c_pad, w_ada, b_ada)


def _rotary(x, cos_t, msin_t, psin_t):
    width = x.shape[1]
    reps = width // LANES
    c = jnp.concatenate([cos_t] * reps, axis=1)
    m = jnp.concatenate([msin_t] * reps, axis=1)
    p = jnp.concatenate([psin_t] * reps, axis=1)
    half = ROT_DIM // 2
    return x * c + pltpu.roll(x, width - half, 1) * m + pltpu.roll(x, half, 1) * p


def _inproj_body(x_ref, mod_ref, g_ref, wa_ref, wlr_ref, wgk_ref, bgk_ref, wb_ref, rc_ref, rm_ref, rp_ref,
                 q_ref, k_ref, v_ref, gg_ref, laf_ref, lab_ref, aq_ref, ak_ref, av_ref):
    shift = mod_ref[0:1, :]
    scale = mod_ref[1:2, :]
    h = (_rms(x_ref[...], g_ref[...]) * (1.0 + scale) + shift).astype(BF16)

    hk = GLA_HEADS * GLA_DK
    hv = GLA_HEADS * GLA_DV
    pa = jnp.dot(h, wa_ref[...], preferred_element_type=F32)
    q_ref[...] = pa[:, 0:hk] * (GLA_DK ** -0.5)
    k_ref[...] = pa[:, hk:2 * hk]
    v_ref[...] = pa[:, 2 * hk:2 * hk + hv].astype(BF16)
    gg_ref[...] = pa[:, 2 * hk + hv:2 * hk + 2 * hv]

    plr = jnp.dot(h, wlr_ref[...], preferred_element_type=F32)
    gk = jnp.dot(plr.astype(BF16), wgk_ref[...], preferred_element_type=F32) + bgk_ref[...]
    la = (jnp.minimum(gk, 0.0) - jnp.log1p(jnp.exp(-jnp.abs(gk)))) * (1.0 / GLA_GATE_NORMALIZER)
    laf_ref[...] = la[:, 0:hk]
    lab_ref[...] = la[:, hk:2 * hk]

    pb = jnp.dot(h, wb_ref[...], preferred_element_type=F32)
    nq = ATT_Q_HEADS * ATT_HEAD_DIM
    nk = 2 * ATT_KV_HEADS * ATT_HEAD_DIM
    rc, rm, rp = rc_ref[...], rm_ref[...], rp_ref[...]
    aq_ref[...] = (_rotary(pb[:, 0:nq], rc, rm, rp) * (ATT_HEAD_DIM ** -0.5)).astype(BF16)
    ak_ref[...] = _rotary(pb[:, nq:nq + nk], rc, rm, rp).astype(BF16)
    av_ref[...] = pb[:, nq + nk:nq + 2 * nk].astype(BF16)


def _inproj(x2, mod, g_pre, wa, wlr, wgk, bgk, wb, rc, rm, rp):
    t, d = x2.shape
    tm = TM_IN
    tiles_per_seq = SEQ // tm
    hk = GLA_HEADS * GLA_DK
    hv = GLA_HEADS * GLA_DV
    nq = ATT_Q_HEADS * ATT_HEAD_DIM
    nk = 2 * ATT_KV_HEADS * ATT_HEAD_DIM

    def full(a):
        return pl.BlockSpec(a.shape, lambda i: (0,) * a.ndim)

    def rows(w):
        return pl.BlockSpec((tm, w), lambda i: (i, 0))

    def table():
        return pl.BlockSpec((tm, LANES), lambda i: (i % tiles_per_seq, 0))

    out_widths = [(hk, F32), (hk, F32), (hv, BF16), (hv, F32), (hk, F32), (hk, F32), (nq, BF16), (nk, BF16), (nk, BF16)]
    return pl.pallas_call(
        _inproj_body,
        grid=(t // tm,),
        in_specs=[
            rows(d),
            pl.BlockSpec((None, 6, d), lambda i: (i // tiles_per_seq, 0, 0)),
            full(g_pre), full(wa), full(wlr), full(wgk), full(bgk), full(wb),
            table(), table(), table(),
        ],
        out_specs=[rows(w) for w, _ in out_widths],
        out_shape=[jax.ShapeDtypeStruct((t, w), dt) for w, dt in out_widths],
        compiler_params=_params(("arbitrary",), 56),
        name="inproj",
    )(x2, mod, g_pre, wa, wlr, wgk, bgk, wb, rc, rm, rp)


def _gla_chunk(q, k, la, v_ref, rows, o_ref, states, cum, tri, i_last, i_mid, head_masks):
    hi = la.astype(BF16)
    lo = (la - hi.astype(F32)).astype(BF16)
    b = jnp.dot(cum, hi, preferred_element_type=F32) + jnp.dot(cum, lo, preferred_element_type=F32)
    b_last = b[i_last:i_last + 1, :]
    b_mid = b[i_mid:i_mid + 1, :]
    qs = q * jnp.exp(b - b_mid)
    ks = k * jnp.exp(b_mid - b)
    qi = q * jnp.exp(b)
    kst = k * jnp.exp(b_last - b)
    decay = jnp.exp(b_last)
    new_states = []
    for h in range(GLA_HEADS):
        pair = slice((h // 2) * LANES, (h // 2 + 1) * LANES)
        mask = head_masks[h % 2]
        vcols = slice(h * GLA_DV, (h + 1) * GLA_DV)
        qs_h = jnp.where(mask, qs[:, pair], 0.0).astype(BF16)
        sc = lax.dot_general(qs_h, ks[:, pair].astype(BF16), NT_DIMS, preferred_element_type=F32)
        sc = jnp.where(tri, sc, 0.0)
        v_h = v_ref[rows, vcols]
        st = states[h]
        qi_h = jnp.where(mask, qi[:, pair], 0.0).astype(BF16)
        o = jnp.dot(sc.astype(BF16), v_h, preferred_element_type=F32)
        o = o + lax.dot_general(qi_h, st.astype(BF16), NT_DIMS, preferred_element_type=F32)
        o_ref[rows, vcols] = o
        kv = lax.dot_general(v_h, kst[:, pair].astype(BF16), TN_DIMS, preferred_element_type=F32)
        new_states.append(st * decay[:, pair] + kv)
    return new_states


def _gla_body(qf_ref, kf_ref, vf_ref, laf_ref, qb_ref, kb_ref, vb_ref, lab_ref, of_ref, ob_ref, sf_ref, sb_ref):
    @pl.when(pl.program_id(1) == 0)
    def _():
        sf_ref[...] = jnp.zeros_like(sf_ref)
        sb_ref[...] = jnp.zeros_like(sb_ref)

    c = GLA_CHUNK
    r_i = lax.broadcasted_iota(I32, (c, c), 0)
    c_i = lax.broadcasted_iota(I32, (c, c), 1)
    lower = c_i <= r_i
    upper = c_i >= r_i
    cum_f = jnp.where(lower, 1.0, 0.0).astype(BF16)
    cum_b = jnp.where(upper, 1.0, 0.0).astype(BF16)
    lane = lax.broadcasted_iota(I32, (1, LANES), 1)
    head_masks = (lane < GLA_DK, lane >= GLA_DK)

    st_f = [sf_ref[h] for h in range(GLA_HEADS)]
    st_b = [sb_ref[h] for h in range(GLA_HEADS)]
    for g in range(GLA_GROUP):
        rows_f = slice(g * c, (g + 1) * c)
        gb = GLA_GROUP - 1 - g
        rows_b = slice(gb * c, (gb + 1) * c)
        st_f = _gla_chunk(qf_ref[rows_f, :], kf_ref[rows_f, :], laf_ref[rows_f, :], vf_ref, rows_f, of_ref,
                          st_f, cum_f, lower, c - 1, c // 2 - 1, head_masks)
        st_b = _gla_chunk(qb_ref[rows_b, :], kb_ref[rows_b, :], lab_ref[rows_b, :], vb_ref, rows_b, ob_ref,
                          st_b, cum_b, upper, 0, c // 2, head_masks)
    for h in range(GLA_HEADS):
        sf_ref[h] = st_f[h]
        sb_ref[h] = st_b[h]


def _gla(q, k, v, laf, lab):
    t = q.shape[0]
    rows = GLA_GROUP * GLA_CHUNK
    ng = SEQ // rows
    hk = GLA_HEADS * GLA_DK
    hv = GLA_HEADS * GLA_DV

    def fwd(w):
        return pl.BlockSpec((rows, w), lambda b, n: (b * ng + n, 0))

    def bwd(w):
        return pl.BlockSpec((rows, w), lambda b, n: (b * ng + ng - 1 - n, 0))

    return pl.pallas_call(
        _gla_body,
        grid=(BATCH, ng),
        in_specs=[fwd(hk), fwd(hk), fwd(hv), fwd(hk), bwd(hk), bwd(hk), bwd(hv), bwd(hk)],
        out_specs=[fwd(hv), bwd(hv)],
        out_shape=[jax.ShapeDtypeStruct((t, hv), F32)] * 2,
        scratch_shapes=[pltpu.VMEM((GLA_HEADS, GLA_DV, 2 * GLA_DK), F32)] * 2,
        compiler_params=_params(("arbitrary", "arbitrary"), 32),
        name="gla",
    )(q, k, v, laf, q, k, v, lab)


def _attn_body(sink_ref, q_ref, kp_ref, kc_ref, kn_ref, vp_ref, vc_ref, vn_ref, o_ref):
    n = pl.program_id(1)
    nb = pl.num_programs(1)
    qb = ATT_BLOCK
    k_all = jnp.concatenate([kp_ref[...], kc_ref[...], kn_ref[...]], axis=0)
    v_all = jnp.concatenate([vp_ref[...], vc_ref[...], vn_ref[...]], axis=0)
    lane = lax.broadcasted_iota(I32, (1, LANES), 1)
    lo = lane < ATT_HEAD_DIM
    i_q = lax.broadcasted_iota(I32, (qb, 3 * qb), 0)
    j_k = lax.broadcasted_iota(I32, (qb, 3 * qb), 1)
    rel = j_k - qb - i_q
    valid = (jnp.abs(rel) <= ATT_WINDOW) & ((j_k >= qb) | (n > 0)) & ((j_k < 2 * qb) | (n < nb - 1))
    valid4 = jnp.concatenate([valid] * 4, axis=0)
    for g in range(ATT_KV_HEADS):
        kg = k_all[:, g * LANES:(g + 1) * LANES]
        vg = v_all[:, g * LANES:(g + 1) * LANES]
        qa = q_ref[:, (2 * g) * LANES:(2 * g + 1) * LANES]
        qc = q_ref[:, (2 * g + 1) * LANES:(2 * g + 2) * LANES]
        zero = jnp.zeros_like(qa)
        lhs = jnp.concatenate([jnp.where(lo, qa, zero), jnp.where(lo, zero, qa),
                               jnp.where(lo, qc, zero), jnp.where(lo, zero, qc)], axis=0)
        s = lax.dot_general(lhs, kg, NT_DIMS, preferred_element_type=F32)
        s = jnp.where(valid4, s, NEG_INF)
        sink = jnp.concatenate([jnp.full((qb, 1), sink_ref[4 * g + r], F32) for r in range(4)], axis=0)
        m = jnp.maximum(jnp.max(s, axis=-1, keepdims=True), sink)
        p = jnp.exp(s - m)
        denom = jnp.sum(p, axis=-1, keepdims=True) + jnp.exp(sink - m)
        o = jnp.dot(p.astype(BF16), vg, preferred_element_type=F32) / denom
        o_ref[:, (2 * g) * LANES:(2 * g + 1) * LANES] = jnp.where(lo, o[0:qb], o[qb:2 * qb]).astype(o_ref.dtype)
        o_ref[:, (2 * g + 1) * LANES:(2 * g + 2) * LANES] = jnp.where(lo, o[2 * qb:3 * qb], o[3 * qb:4 * qb]).astype(o_ref.dtype)


def _attn(sink, aq, ak2, av2):
    t = aq.shape[0]
    qb = ATT_BLOCK
    nb = SEQ // qb
    nq = ATT_Q_HEADS * ATT_HEAD_DIM
    nk = 2 * ATT_KV_HEADS * ATT_HEAD_DIM

    def kv(shift):
        return pl.BlockSpec((qb, nk), lambda b, n: (b * nb + jnp.clip(n + shift, 0, nb - 1), 0))

    return pl.pallas_call(
        _attn_body,
        grid=(BATCH, nb),
        in_specs=[
            pl.BlockSpec(memory_space=pltpu.SMEM),
            pl.BlockSpec((qb, nq), lambda b, n: (b * nb + n, 0)),
            kv(-1), kv(0), kv(1), kv(-1), kv(0), kv(1),
        ],
        out_specs=pl.BlockSpec((qb, nq), lambda b, n: (b * nb + n, 0)),
        out_shape=jax.ShapeDtypeStruct((t, nq), BF16),
        compiler_params=_params(("arbitrary", "arbitrary"), 32),
        name="attn",
    )(sink, aq, ak2, ak2, ak2, av2, av2, av2)


def _post_body(of_ref, ob_ref, gg_ref, oa_ref, x_ref, mod_ref, ggla_ref, gpm_ref, gpf_ref, wout_ref,
               wrh_ref, wrl_ref, br_ref,
               x1_ref, h2_ref, ti_ref, gt_ref, rk_ref, cnt_ref, base_ref):
    tm = TM_POST

    @pl.when(pl.program_id(0) == 0)
    def _():
        base_ref[...] = jnp.zeros_like(base_ref)

    og = of_ref[...] + ob_ref[...]
    gg = gg_ref[...]
    parts = []
    for h in range(GLA_HEADS):
        cols = slice(h * GLA_DV, (h + 1) * GLA_DV)
        parts.append((_rms(og[:, cols], ggla_ref[...]) * _silu(gg[:, cols])).astype(BF16))
    o = jnp.concatenate(parts + [oa_ref[...]], axis=1)
    y = jnp.dot(o, wout_ref[...], preferred_element_type=F32)

    gate1 = mod_ref[2:3, :]
    shift2 = mod_ref[3:4, :]
    scale2 = mod_ref[4:5, :]
    x1 = x_ref[...] + gate1 * _rms(y, gpm_ref[...])
    x1_ref[...] = x1
    h2 = _rms(x1, gpf_ref[...]) * (1.0 + scale2) + shift2
    h2_hi = h2.astype(BF16)
    h2_hi32 = h2_hi.astype(F32)
    h2_lo = (h2 - h2_hi32).astype(BF16)
    for s in range(ROW_TILES):
        h2_ref[pl.ds(s, tm, stride=SUBLANES), :] = h2_hi32[:, s * LANES:(s + 1) * LANES]

    wrh = wrh_ref[...]
    logits = (lax.dot_general(wrh, h2_hi, NT_DIMS, preferred_element_type=F32)
              + lax.dot_general(wrh, h2_lo, NT_DIMS, preferred_element_type=F32)
              + lax.dot_general(wrl_ref[...], h2_hi, NT_DIMS, preferred_element_type=F32)
              + br_ref[...])
    e_iota = lax.broadcasted_iota(I32, (N_EXPERTS, tm), 0)
    idxs, vals = [], []
    work = logits
    for _ in range(TOP_K):
        m = jnp.max(work, axis=0, keepdims=True)
        idx = jnp.min(jnp.where(work == m, e_iota, N_EXPERTS), axis=0, keepdims=True)
        idxs.append(idx)
        vals.append(m)
        work = jnp.where(e_iota == idx, -jnp.inf, work)
    exps = [jnp.exp(v - vals[0]) for v in vals]
    inv = 1.0 / (exps[0] + exps[1] + exps[2] + exps[3])
    gt_ref[...] = jnp.concatenate([e * inv for e in exps], axis=0)
    ti_ref[...] = jnp.concatenate(idxs, axis=0)

    onehots = [e_iota == idx for idx in idxs]
    member = jnp.where(onehots[0] | onehots[1] | onehots[2] | onehots[3], 1.0, 0.0)
    t_row = lax.broadcasted_iota(I32, (tm, tm), 0)
    t_col = lax.broadcasted_iota(I32, (tm, tm), 1)
    strict = jnp.where(t_row < t_col, 1.0, 0.0).astype(BF16)
    before = base_ref[...] + jnp.dot(member.astype(BF16), strict, preferred_element_type=F32)
    rk_ref[...] = jnp.concatenate(
        [jnp.sum(jnp.where(oh, before, 0.0), axis=0, keepdims=True) for oh in onehots], axis=0).astype(I32)
    new_base = base_ref[...] + jnp.sum(member, axis=1, keepdims=True)
    base_ref[...] = new_base
    cnt_ref[...] = jnp.broadcast_to(new_base, cnt_ref.shape)


def _post(o_f, o_b, gg, o_att, x2, mod, g_gla, g_pm, g_pf, wout, wrh, wrl, br):
    t, d = x2.shape
    tm = TM_POST
    tiles_per_seq = SEQ // tm
    hv = GLA_HEADS * GLA_DV

    def full(a):
        return pl.BlockSpec(a.shape, lambda i: (0,) * a.ndim)

    def rows(w):
        return pl.BlockSpec((tm, w), lambda i: (i, 0))

    def lanes():
        return pl.BlockSpec((TOP_K, tm), lambda i: (0, i))

    return pl.pallas_call(
        _post_body,
        grid=(t // tm,),
        in_specs=[
            rows(hv), rows(hv), rows(hv), rows(hv), rows(d),
            pl.BlockSpec((None, 6, d), lambda i: (i // tiles_per_seq, 0, 0)),
            full(g_gla), full(g_pm), full(g_pf), full(wout), full(wrh), full(wrl), full(br),
        ],
        out_specs=[
            rows(d),
            pl.BlockSpec((tm * SUBLANES, LANES), lambda i: (i, 0)),
            lanes(), lanes(), lanes(),
            pl.BlockSpec((N_EXPERTS, LANES), lambda i: (0, 0)),
        ],
        out_shape=[
            jax.ShapeDtypeStruct((t, d), F32),
            jax.ShapeDtypeStruct((t * SUBLANES, LANES), F32),
            jax.ShapeDtypeStruct((TOP_K, t), I32),
            jax.ShapeDtypeStruct((TOP_K, t), F32),
            jax.ShapeDtypeStruct((TOP_K, t), I32),
            jax.ShapeDtypeStruct((N_EXPERTS, LANES), F32),
        ],
        scratch_shapes=[pltpu.VMEM((N_EXPERTS, 1), F32)],
        compiler_params=_params(("arbitrary",), 48),
        name="post",
    )(o_f, o_b, gg, o_att, x2, mod, g_gla, g_pm, g_pf, wout, wrh, wrl, br)


def _route_body(ti_ref, rk_ref, cnt_ref, pos_ref, be_ref, meta_ref):
    cnt = cnt_ref[...]
    padded = jnp.floor((cnt + (MOE_BM - 1)) * (1.0 / MOE_BM)) * MOE_BM
    starts, ends = [], []
    acc = jnp.zeros((1, LANES), F32)
    for e in range(N_EXPERTS):
        starts.append(acc)
        acc = acc + padded[e:e + 1, :]
        ends.append(acc)
    ti = ti_ref[...]
    off = jnp.zeros(ti.shape, F32)
    for e in range(N_EXPERTS):
        off = jnp.where(ti == e, starts[e][:, 0:1], off)
    pos_ref[...] = rk_ref[...] + off.astype(I32)

    block_start = lax.broadcasted_iota(I32, (1, MOE_NB_PAD), 1).astype(F32) * MOE_BM
    owner = jnp.zeros((1, MOE_NB_PAD), I32)
    for e in range(N_EXPERTS):
        owner = owner + jnp.where(ends[e][:, 0:1] <= block_start, 1, 0)
    be_ref[...] = jnp.minimum(owner, N_EXPERTS - 1)
    meta_ref[...] = jnp.concatenate(
        [starts[e] + cnt[e:e + 1, :] for e in range(N_EXPERTS)] + ends + [acc * (1.0 / MOE_BM)]
        + [jnp.zeros((SUBLANES - 1, LANES), F32)], axis=0).astype(I32)


def _route(top_i, rank, counts):
    return pl.pallas_call(
        _route_body,
        out_shape=[
            jax.ShapeDtypeStruct(top_i.shape, I32),
            jax.ShapeDtypeStruct((1, MOE_NB_PAD), I32),
            jax.ShapeDtypeStruct((2 * N_EXPERTS + SUBLANES, LANES), I32),
        ],
        compiler_params=pltpu.CompilerParams(vmem_limit_bytes=32 * 1024 * 1024),
        name="route",
    )(top_i, rank, counts)


def _row(ref, r):
    return ref.at[pl.ds(pl.multiple_of(r * SUBLANES, SUBLANES), SUBLANES), :]


def _dispatch_body(pos_ref, meta_ref, h2_ref, xs_ref, zero_ref, sem, zsem):
    i = pl.program_id(0)
    tm = TM_DISP
    base = i * tm

    def issue(tl, carry):
        src = _row(h2_ref, tl)
        for k in range(TOP_K):
            p = pos_ref[k * TOKENS + base + tl]
            pltpu.make_async_copy(src, _row(xs_ref, p), sem).start()
        return carry

    lax.fori_loop(0, tm, issue, 0)

    @pl.when(i == 0)
    def _():
        zero_ref[...] = jnp.zeros_like(zero_ref)
        zrow = _row(zero_ref, 0)

        def per_expert(e, carry):
            lo = meta_ref[e]
            hi = meta_ref[N_EXPERTS + e]

            def fill(r, c):
                pltpu.make_async_copy(zrow, _row(xs_ref, r), zsem).start()
                return c

            lax.fori_loop(lo, hi, fill, 0)

            def drain(r, c):
                pltpu.make_async_copy(zrow, _row(xs_ref, r), zsem).wait()
                return c

            lax.fori_loop(lo, hi, drain, 0)
            return carry

        lax.fori_loop(0, N_EXPERTS, per_expert, 0)

        def block(b):
            return xs_ref.at[pl.ds(pl.multiple_of(b * (MOE_BM * SUBLANES), SUBLANES), MOE_BM * SUBLANES), :]

        def fill_tail(b, c):
            pltpu.make_async_copy(zero_ref, block(b), zsem).start()
            return c

        def drain_tail(b, c):
            pltpu.make_async_copy(zero_ref, block(b), zsem).wait()
            return c

        lax.fori_loop(meta_ref[2 * N_EXPERTS], MOE_NB, fill_tail, 0)
        lax.fori_loop(meta_ref[2 * N_EXPERTS], MOE_NB, drain_tail, 0)

    for k in range(TOP_K):
        pltpu.make_async_copy(h2_ref, xs_ref.at[pl.ds(0, tm * SUBLANES), :], sem).wait()


def _dispatch(pos_flat, meta_flat, h2_tiles):
    tm = TM_DISP
    return pl.pallas_call(
        _dispatch_body,
        grid=(TOKENS // tm,),
        in_specs=[
            pl.BlockSpec(memory_space=pltpu.SMEM),
            pl.BlockSpec(memory_space=pltpu.SMEM),
            pl.BlockSpec((tm * SUBLANES, LANES), lambda i: (i, 0)),
        ],
        out_specs=pl.BlockSpec(memory_space=pl.ANY),
        out_shape=jax.ShapeDtypeStruct((MOE_ROWS * SUBLANES, LANES), F32),
        scratch_shapes=[pltpu.VMEM((MOE_BM * SUBLANES, LANES), F32), pltpu.SemaphoreType.DMA,
                        pltpu.SemaphoreType.DMA],
        compiler_params=_params(("arbitrary",), 32),
        name="dispatch",
    )(pos_flat, meta_flat, h2_tiles)


def _experts_body(be_ref, nu_ref, xs_ref, wgu_ref, bgu_ref, wd_ref, bd_ref, ys_ref, wgu_bf, wd_bf):
    i = pl.program_id(0)
    bm = MOE_BM
    used = i < nu_ref[0]
    prev = be_ref[jnp.maximum(i - 1, 0)]
    fresh = (i == 0) | (be_ref[i] != prev)

    @pl.when(used & fresh)
    def _():
        wgu_bf[...] = wgu_ref[...].astype(BF16)
        wd_bf[...] = wd_ref[...].astype(BF16)

    @pl.when(used)
    def _():
        x = jnp.concatenate([xs_ref[pl.ds(s, bm, stride=SUBLANES), :] for s in range(ROW_TILES)], axis=1).astype(BF16)
        gu = jnp.dot(x, wgu_bf[...], preferred_element_type=F32) + bgu_ref[...]
        gate = jnp.minimum(gu[:, 0:D_FF], SWIGLU_LIMIT)
        up = jnp.clip(gu[:, D_FF:2 * D_FF], -SWIGLU_LIMIT, SWIGLU_LIMIT)
        act = ((up + 1.0) * gate * jax.nn.sigmoid(SWIGLU_ALPHA * gate)).astype(BF16)
        y = jnp.dot(act, wd_bf[...], preferred_element_type=F32) + bd_ref[...]
        for s in range(ROW_TILES):
            ys_ref[pl.ds(s, bm, stride=SUBLANES), :] = y[:, s * LANES:(s + 1) * LANES]

    @pl.when(jnp.logical_not(used))
    def _():
        ys_ref[...] = jnp.zeros_like(ys_ref)


def _experts(block_expert, n_used, xs, w_gate_up, b_gate_up, w_down, b_down):
    bm = MOE_BM
    d = D_MODEL

    def blk(i, be, nu):
        return jnp.minimum(i, nu[0] - 1)

    grid_spec = pltpu.PrefetchScalarGridSpec(
        num_scalar_prefetch=2,
        grid=(MOE_NB,),
        in_specs=[
            pl.BlockSpec((bm * SUBLANES, LANES), lambda i, be, nu: (blk(i, be, nu), 0)),
            pl.BlockSpec((None, d, 2 * D_FF), lambda i, be, nu: (be[blk(i, be, nu)], 0, 0)),
            pl.BlockSpec((None, 1, 2 * D_FF), lambda i, be, nu: (be[blk(i, be, nu)], 0, 0)),
            pl.BlockSpec((None, D_FF, d), lambda i, be, nu: (be[blk(i, be, nu)], 0, 0)),
            pl.BlockSpec((None, 1, d), lambda i, be, nu: (be[blk(i, be, nu)], 0, 0)),
        ],
        out_specs=pl.BlockSpec((bm * SUBLANES, LANES), lambda i, be, nu: (i, 0)),
        scratch_shapes=[pltpu.VMEM((d, 2 * D_FF), BF16), pltpu.VMEM((D_FF, d), BF16)],
    )
    return pl.pallas_call(
        _experts_body,
        grid_spec=grid_spec,
        out_shape=jax.ShapeDtypeStruct((MOE_ROWS * SUBLANES, LANES), F32),
        compiler_params=_params(("arbitrary",), 56),
        name="experts",
    )(block_expert, n_used, xs, w_gate_up, b_gate_up, w_down, b_down)


def _combine_body(pos_ref, gates_ref, x1_ref, mod_ref, gpost_ref, ys_ref, o_ref, ybuf, sem):
    i = pl.program_id(0)
    tm = TM_COMB
    base = i * tm

    def issue(tl, carry):
        for k in range(TOP_K):
            p = pos_ref[k * TOKENS + base + tl]
            pltpu.make_async_copy(_row(ys_ref, p), _row(ybuf.at[k], tl), sem).start()
        return carry

    lax.fori_loop(0, tm, issue, 0)
    for k in range(TOP_K):
        pltpu.make_async_copy(ys_ref.at[pl.ds(0, tm * SUBLANES), :], ybuf.at[k], sem).wait()

    gates = gates_ref[...]
    y = jnp.zeros((tm, D_MODEL), F32)
    for k in range(TOP_K):
        yk = jnp.concatenate([ybuf[k, pl.ds(s, tm, stride=SUBLANES), :] for s in range(ROW_TILES)], axis=1)
        y = y + yk * gates[:, k:k + 1]
    gate2 = mod_ref[5:6, :]
    o_ref[...] = x1_ref[...] + gate2 * _rms(y, gpost_ref[...])


def _combine(pos_flat, gates_t, x1, mod, g_post, ys):
    t, d = x1.shape
    tm = TM_COMB
    tiles_per_seq = SEQ // tm
    return pl.pallas_call(
        _combine_body,
        grid=(t // tm,),
        in_specs=[
            pl.BlockSpec(memory_space=pltpu.SMEM),
            pl.BlockSpec((tm, TOP_K), lambda i: (i, 0)),
            pl.BlockSpec((tm, d), lambda i: (i, 0)),
            pl.BlockSpec((None, 6, d), lambda i: (i // tiles_per_seq, 0, 0)),
            pl.BlockSpec(g_post.shape, lambda i: (0, 0)),
            pl.BlockSpec(memory_space=pl.ANY),
        ],
        out_specs=pl.BlockSpec((tm, d), lambda i: (i, 0)),
        out_shape=jax.ShapeDtypeStruct((t, d), F32),
        scratch_shapes=[pltpu.VMEM((TOP_K, tm * SUBLANES, LANES), F32), pltpu.SemaphoreType.DMA],
        compiler_params=_params(("arbitrary",), 32),
        name="combine",
    )(pos_flat, gates_t, x1, mod, g_post, ys)


def _rotary_tables():
    half = ROT_DIM // 2
    inv_freq = ROPE_THETA ** (-2.0 * jnp.arange(half, dtype=F32) / ROT_DIM)
    ang = jnp.arange(SEQ).astype(F32)[:, None] * inv_freq[None, :]
    cos, sin = jnp.cos(ang), jnp.sin(ang)
    ones = jnp.ones((SEQ, ATT_HEAD_DIM - ROT_DIM), F32)
    zeros = jnp.zeros((SEQ, ATT_HEAD_DIM - ROT_DIM), F32)
    zh = jnp.zeros((SEQ, half), F32)
    reps = LANES // ATT_HEAD_DIM
    rc = jnp.tile(jnp.concatenate([cos, cos, ones], axis=1), (1, reps))
    rm = jnp.tile(jnp.concatenate([-sin, zh, zeros], axis=1), (1, reps))
    rp = jnp.tile(jnp.concatenate([zh, sin, zeros], axis=1), (1, reps))
    return rc, rm, rp


def kernel(x, c, w_ada, b_ada, g_pre_mix, g_post_mix, w_in, w_gk_fwd, b_gk_fwd, w_gk_bwd, b_gk_bwd, g_gla_out,
           attn_sink, w_out, g_pre_ffn, g_post_ffn, w_router, b_router, w_gate_up, b_gate_up, w_down, b_down):
    assert x.shape == (BATCH, SEQ, D_MODEL) and w_ada.shape[0] == 1
    d = D_MODEL
    x2 = x.reshape(TOKENS, d)

    c_pad = jnp.pad(c, ((0, SUBLANES - BATCH), (0, 0)))
    mod = _ada(c_pad, w_ada[0], b_ada)[:BATCH].reshape(BATCH, 6, d)

    hk = GLA_HEADS * GLA_DK
    hv = GLA_HEADS * GLA_DV
    w = w_in[0]
    o_lr = 2 * hk + 2 * hv
    o_aq = o_lr + 2 * GLA_RANK
    o_ak = o_aq + ATT_Q_HEADS * ATT_HEAD_DIM
    o_av = o_ak + ATT_KV_HEADS * ATT_HEAD_DIM
    hd = ATT_HEAD_DIM
    wa = w[:, :o_lr].astype(BF16)
    wlr = w[:, o_lr:o_aq].astype(BF16)
    dup = lambda m: jnp.concatenate([m[:, g * hd:(g + 1) * hd] for g in range(ATT_KV_HEADS) for _ in range(2)], axis=1)
    wb = jnp.concatenate([w[:, o_aq:o_ak], dup(w[:, o_ak:o_av]), dup(w[:, o_av:o_av + ATT_KV_HEADS * hd])], axis=1).astype(BF16)
    zr = jnp.zeros((GLA_RANK, hk), F32)
    wgk = jnp.concatenate([jnp.concatenate([w_gk_fwd[0], zr], axis=1),
                           jnp.concatenate([zr, w_gk_bwd[0]], axis=1)], axis=0).astype(BF16)
    bgk = jnp.concatenate([b_gk_fwd[0], b_gk_bwd[0]])[None, :]
    rc, rm, rp = _rotary_tables()

    q, k, v, gg, laf, lab, aq, ak2, av2 = _inproj(x2, mod, g_pre_mix, wa, wlr, wgk, bgk, wb, rc, rm, rp)
    o_f, o_b = _gla(q, k, v, laf, lab)
    o_att = _attn(attn_sink[0], aq, ak2, av2)

    wr_t = w_router[0].T
    wrh = wr_t.astype(BF16)
    wrl = (wr_t - wrh.astype(F32)).astype(BF16)
    x1, h2_tiles, top_i, gates, rank, counts = _post(
        o_f, o_b, gg, o_att, x2, mod, g_gla_out, g_post_mix, g_pre_ffn, w_out[0].astype(BF16), wrh, wrl,
        b_router[0][:, None])

    pos, block_expert, meta = _route(top_i, rank, counts)
    pos_flat = pos.reshape(TOP_K * TOKENS)
    meta_flat = meta[:, 0]
    xs = _dispatch(pos_flat, meta_flat, h2_tiles)
    ys = _experts(block_expert[0, :MOE_NB], meta_flat[2 * N_EXPERTS:2 * N_EXPERTS + 1], xs,
                  w_gate_up[0], b_gate_up[0][:, None, :], w_down[0], b_down[0][:, None, :])
    out = _combine(pos_flat, gates.T, x1, mod, g_post_ffn, ys)
    return out.reshape(BATCH, SEQ, d)
```

```python
import functools

import jax
import jax.numpy as jnp
import numpy as np
from jax import lax
from jax.experimental import pallas as pl
from jax.experimental.pallas import tpu as pltpu
from jax.experimental.pallas import tpu_sc as plsc

F32 = jnp.float32
BF16 = jnp.bfloat16
I32 = jnp.int32

D_MODEL = 1024
BATCH = 2
SEQ = 8192
TOKENS = BATCH * SEQ
GLA_HEADS = 4
GLA_DV = 128
GLA_DK = 64
GLA_RANK = 16
GLA_GATE_NORMALIZER = 16.0
GLA_CHUNK = 64
ATT_Q_HEADS = 8
ATT_KV_HEADS = 2
ATT_HEAD_DIM = 64
ATT_WINDOW = 128
ATT_BLOCK = 128
ROT_DIM = 16
ROPE_THETA = 500000.0
N_EXPERTS = 32
TOP_K = 4
D_FF = 1024
SWIGLU_LIMIT = 7.0
SWIGLU_ALPHA = 1.702
NORM_EPS = 1e-6
NEG_INF = -1e30

LANES = 128
SUBLANES = 8
ROW_TILES = D_MODEL // LANES

TM_IN = 512
GLA_GROUP = 4
TM_POST = 256
MOE_BM = 256
MOE_ROWS = TOKENS * TOP_K + N_EXPERTS * MOE_BM
MOE_NB = MOE_ROWS // MOE_BM
MOE_NB_PAD = ((MOE_NB + LANES - 1) // LANES) * LANES
TM_DISP = 256
TM_COMB = 256
SC_GATHER_WINDOW = 32

NT_DIMS = (((1,), (1,)), ((), ()))
TN_DIMS = (((0,), (0,)), ((), ()))


def _params(semantics, vmem_mib):
    return pltpu.CompilerParams(dimension_semantics=semantics, vmem_limit_bytes=vmem_mib * 1024 * 1024)


def _rms(x, g):
    return x * lax.rsqrt(jnp.mean(x * x, axis=-1, keepdims=True) + NORM_EPS) * g


def _silu(x):
    return x * jax.nn.sigmoid(x)


def _ada_body(c_ref, w_ref, b_ref, o_ref):
    ca = _silu(c_ref[...]).astype(BF16)
    o_ref[...] = jnp.dot(ca, w_ref[...].astype(BF16), preferred_element_type=F32) + b_ref[...]


def _ada(c_pad, w_ada, b_ada):
    d = D_MODEL
    return pl.pallas_call(
        _ada_body,
        grid=(6,),
        in_specs=[
            pl.BlockSpec((SUBLANES, d), lambda j: (0, 0)),
            pl.BlockSpec((d, d), lambda j: (0, j)),
            pl.BlockSpec((1, d), lambda j: (0, j)),
        ],
        out_specs=pl.BlockSpec((SUBLANES, d), lambda j: (0, j)),
        out_shape=jax.ShapeDtypeStruct((SUBLANES, 6 * d), F32),
        compiler_params=_params(("arbitrary",), 32),
        name="ada",
    )(c_pad, w_ada, b_ada)


def _rotary(x, cos_t, msin_t, psin_t):
    width = x.shape[1]
    reps = width // LANES
    c = jnp.concatenate([cos_t] * reps, axis=1)
    m = jnp.concatenate([msin_t] * reps, axis=1)
    p = jnp.concatenate([psin_t] * reps, axis=1)
    half = ROT_DIM // 2
    return x * c + pltpu.roll(x, width - half, 1) * m + pltpu.roll(x, half, 1) * p


def _inproj_body(x_ref, mod_ref, g_ref, wa_ref, wlr_ref, wgk_ref, bgk_ref, wb_ref, rc_ref, rm_ref, rp_ref,
                 q_ref, k_ref, v_ref, gg_ref, laf_ref, lab_ref, aq_ref, ak_ref, av_ref):
    shift = mod_ref[0:1, :]
    scale = mod_ref[1:2, :]
    h = (_rms(x_ref[...], g_ref[...]) * (1.0 + scale) + shift).astype(BF16)

    hk = GLA_HEADS * GLA_DK
    hv = GLA_HEADS * GLA_DV
    pa = jnp.dot(h, wa_ref[...], preferred_element_type=F32)
    q_ref[...] = pa[:, 0:hk] * (GLA_DK ** -0.5)
    k_ref[...] = pa[:, hk:2 * hk]
    v_ref[...] = pa[:, 2 * hk:2 * hk + hv].astype(BF16)
    gg_ref[...] = pa[:, 2 * hk + hv:2 * hk + 2 * hv]

    plr = jnp.dot(h, wlr_ref[...], preferred_element_type=F32)
    gk = jnp.dot(plr.astype(BF16), wgk_ref[...], preferred_element_type=F32) + bgk_ref[...]
    la = (jnp.minimum(gk, 0.0) - jnp.log1p(jnp.exp(-jnp.abs(gk)))) * (1.0 / GLA_GATE_NORMALIZER)
    laf_ref[...] = la[:, 0:hk]
    lab_ref[...] = la[:, hk:2 * hk]

    pb = jnp.dot(h, wb_ref[...], preferred_element_type=F32)
    nq = ATT_Q_HEADS * ATT_HEAD_DIM
    nk = 2 * ATT_KV_HEADS * ATT_HEAD_DIM
    rc, rm, rp = rc_ref[...], rm_ref[...], rp_ref[...]
    aq_ref[...] = (_rotary(pb[:, 0:nq], rc, rm, rp) * (ATT_HEAD_DIM ** -0.5)).astype(BF16)
    ak_ref[...] = _rotary(pb[:, nq:nq + nk], rc, rm, rp).astype(BF16)
    av_ref[...] = pb[:, nq + nk:nq + 2 * nk].astype(BF16)


def _inproj(x2, mod, g_pre, wa, wlr, wgk, bgk, wb, rc, rm, rp):
    t, d = x2.shape
    tm = TM_IN
    tiles_per_seq = SEQ // tm
    hk = GLA_HEADS * GLA_DK
    hv = GLA_HEADS * GLA_DV
    nq = ATT_Q_HEADS * ATT_HEAD_DIM
    nk = 2 * ATT_KV_HEADS * ATT_HEAD_DIM

    def full(a):
        return pl.BlockSpec(a.shape, lambda i: (0,) * a.ndim)

    def rows(w):
        return pl.BlockSpec((tm, w), lambda i: (i, 0))

    def table():
        return pl.BlockSpec((tm, LANES), lambda i: (i % tiles_per_seq, 0))

    out_widths = [(hk, F32), (hk, F32), (hv, BF16), (hv, F32), (hk, F32), (hk, F32), (nq, BF16), (nk, BF16), (nk, BF16)]
    return pl.pallas_call(
        _inproj_body,
        grid=(t // tm,),
        in_specs=[
            rows(d),
            pl.BlockSpec((None, 6, d), lambda i: (i // tiles_per_seq, 0, 0)),
            full(g_pre), full(wa), full(wlr), full(wgk), full(bgk), full(wb),
            table(), table(), table(),
        ],
        out_specs=[rows(w) for w, _ in out_widths],
        out_shape=[jax.ShapeDtypeStruct((t, w), dt) for w, dt in out_widths],
        compiler_params=_params(("arbitrary",), 56),
        name="inproj",
    )(x2, mod, g_pre, wa, wlr, wgk, bgk, wb, rc, rm, rp)


def _gla_chunk(q, k, la, v_ref, rows, o_ref, states, cum, tri, i_last, i_mid, head_masks):
    hi = la.astype(BF16)
    lo = (la - hi.astype(F32)).astype(BF16)
    b = jnp.dot(cum, hi, preferred_element_type=F32) + jnp.dot(cum, lo, preferred_element_type=F32)
    b_last = b[i_last:i_last + 1, :]
    b_mid = b[i_mid:i_mid + 1, :]
    qs = q * jnp.exp(b - b_mid)
    ks = k * jnp.exp(b_mid - b)
    qi = q * jnp.exp(b)
    kst = k * jnp.exp(b_last - b)
    decay = jnp.exp(b_last)
    new_states = []
    for h in range(GLA_HEADS):
        pair = slice((h // 2) * LANES, (h // 2 + 1) * LANES)
        mask = head_masks[h % 2]
        vcols = slice(h * GLA_DV, (h + 1) * GLA_DV)
        qs_h = jnp.where(mask, qs[:, pair], 0.0).astype(BF16)
        sc = lax.dot_general(qs_h, ks[:, pair].astype(BF16), NT_DIMS, preferred_element_type=F32)
        sc = jnp.where(tri, sc, 0.0)
        v_h = v_ref[rows, vcols]
        st = states[h]
        qi_h = jnp.where(mask, qi[:, pair], 0.0).astype(BF16)
        o = jnp.dot(sc.astype(BF16), v_h, preferred_element_type=F32)
        o = o + lax.dot_general(qi_h, st.astype(BF16), NT_DIMS, preferred_element_type=F32)
        o_ref[rows, vcols] = o
        kv = lax.dot_general(v_h, kst[:, pair].astype(BF16), TN_DIMS, preferred_element_type=F32)
        new_states.append(st * decay[:, pair] + kv)
    return new_states


def _gla_body(qf_ref, kf_ref, vf_ref, laf_ref, qb_ref, kb_ref, vb_ref, lab_ref, of_ref, ob_ref, sf_ref, sb_ref):
    @pl.when(pl.program_id(1) == 0)
    def _():
        sf_ref[...] = jnp.zeros_like(sf_ref)
        sb_ref[...] = jnp.zeros_like(sb_ref)

    c = GLA_CHUNK
    r_i = lax.broadcasted_iota(I32, (c, c), 0)
    c_i = lax.broadcasted_iota(I32, (c, c), 1)
    lower = c_i <= r_i
    upper = c_i >= r_i
    cum_f = jnp.where(lower, 1.0, 0.0).astype(BF16)
    cum_b = jnp.where(upper, 1.0, 0.0).astype(BF16)
    lane = lax.broadcasted_iota(I32, (1, LANES), 1)
    head_masks = (lane < GLA_DK, lane >= GLA_DK)

    st_f = [sf_ref[h] for h in range(GLA_HEADS)]
    st_b = [sb_ref[h] for h in range(GLA_HEADS)]
    for g in range(GLA_GROUP):
        rows_f = slice(g * c, (g + 1) * c)
        gb = GLA_GROUP - 1 - g
        rows_b = slice(gb * c, (gb + 1) * c)
        st_f = _gla_chunk(qf_ref[rows_f, :], kf_ref[rows_f, :], laf_ref[rows_f, :], vf_ref, rows_f, of_ref,
                          st_f, cum_f, lower, c - 1, c // 2 - 1, head_masks)
        st_b = _gla_chunk(qb_ref[rows_b, :], kb_ref[rows_b, :], lab_ref[rows_b, :], vb_ref, rows_b, ob_ref,
                          st_b, cum_b, upper, 0, c // 2, head_masks)
    for h in range(GLA_HEADS):
        sf_ref[h] = st_f[h]
        sb_ref[h] = st_b[h]


def _gla(q, k, v, laf, lab):
    t = q.shape[0]
    rows = GLA_GROUP * GLA_CHUNK
    ng = SEQ // rows
    hk = GLA_HEADS * GLA_DK
    hv = GLA_HEADS * GLA_DV

    def fwd(w):
        return pl.BlockSpec((rows, w), lambda b, n: (b * ng + n, 0))

    def bwd(w):
        return pl.BlockSpec((rows, w), lambda b, n: (b * ng + ng - 1 - n, 0))

    return pl.pallas_call(
        _gla_body,
        grid=(BATCH, ng),
        in_specs=[fwd(hk), fwd(hk), fwd(hv), fwd(hk), bwd(hk), bwd(hk), bwd(hv), bwd(hk)],
        out_specs=[fwd(hv), bwd(hv)],
        out_shape=[jax.ShapeDtypeStruct((t, hv), F32)] * 2,
        scratch_shapes=[pltpu.VMEM((GLA_HEADS, GLA_DV, 2 * GLA_DK), F32)] * 2,
        compiler_params=_params(("arbitrary", "arbitrary"), 32),
        name="gla",
    )(q, k, v, laf, q, k, v, lab)


def _attn_body(sink_ref, q_ref, kp_ref, kc_ref, kn_ref, vp_ref, vc_ref, vn_ref, o_ref):
    n = pl.program_id(1)
    nb = pl.num_programs(1)
    qb = ATT_BLOCK
    k_all = jnp.concatenate([kp_ref[...], kc_ref[...], kn_ref[...]], axis=0)
    v_all = jnp.concatenate([vp_ref[...], vc_ref[...], vn_ref[...]], axis=0)
    lane = lax.broadcasted_iota(I32, (1, LANES), 1)
    lo = lane < ATT_HEAD_DIM
    i_q = lax.broadcasted_iota(I32, (qb, 3 * qb), 0)
    j_k = lax.broadcasted_iota(I32, (qb, 3 * qb), 1)
    rel = j_k - qb - i_q
    valid = (jnp.abs(rel) <= ATT_WINDOW) & ((j_k >= qb) | (n > 0)) & ((j_k < 2 * qb) | (n < nb - 1))
    valid4 = jnp.concatenate([valid] * 4, axis=0)
    for g in range(ATT_KV_HEADS):
        kg = k_all[:, g * LANES:(g + 1) * LANES]
        vg = v_all[:, g * LANES:(g + 1) * LANES]
        qa = q_ref[:, (2 * g) * LANES:(2 * g + 1) * LANES]
        qc = q_ref[:, (2 * g + 1) * LANES:(2 * g + 2) * LANES]
        zero = jnp.zeros_like(qa)
        lhs = jnp.concatenate([jnp.where(lo, qa, zero), jnp.where(lo, zero, qa),
                               jnp.where(lo, qc, zero), jnp.where(lo, zero, qc)], axis=0)
        s = lax.dot_general(lhs, kg, NT_DIMS, preferred_element_type=F32)
        s = jnp.where(valid4, s, NEG_INF)
        sink = jnp.concatenate([jnp.full((qb, 1), sink_ref[4 * g + r], F32) for r in range(4)], axis=0)
        m = jnp.maximum(jnp.max(s, axis=-1, keepdims=True), sink)
        p = jnp.exp(s - m)
        denom = jnp.sum(p, axis=-1, keepdims=True) + jnp.exp(sink - m)
        o = jnp.dot(p.astype(BF16), vg, preferred_element_type=F32) / denom
        o_ref[:, (2 * g) * LANES:(2 * g + 1) * LANES] = jnp.where(lo, o[0:qb], o[qb:2 * qb]).astype(o_ref.dtype)
        o_ref[:, (2 * g + 1) * LANES:(2 * g + 2) * LANES] = jnp.where(lo, o[2 * qb:3 * qb], o[3 * qb:4 * qb]).astype(o_ref.dtype)


def _attn(sink, aq, ak2, av2):
    t = aq.shape[0]
    qb = ATT_BLOCK
    nb = SEQ // qb
    nq = ATT_Q_HEADS * ATT_HEAD_DIM
    nk = 2 * ATT_KV_HEADS * ATT_HEAD_DIM

    def kv(shift):
        return pl.BlockSpec((qb, nk), lambda b, n: (b * nb + jnp.clip(n + shift, 0, nb - 1), 0))

    return pl.pallas_call(
        _attn_body,
        grid=(BATCH, nb),
        in_specs=[
            pl.BlockSpec(memory_space=pltpu.SMEM),
            pl.BlockSpec((qb, nq), lambda b, n: (b * nb + n, 0)),
            kv(-1), kv(0), kv(1), kv(-1), kv(0), kv(1),
        ],
        out_specs=pl.BlockSpec((qb, nq), lambda b, n: (b * nb + n, 0)),
        out_shape=jax.ShapeDtypeStruct((t, nq), BF16),
        compiler_params=_params(("arbitrary", "arbitrary"), 32),
        name="attn",
    )(sink, aq, ak2, ak2, ak2, av2, av2, av2)


def _post_body(of_ref, ob_ref, gg_ref, oa_ref, x_ref, mod_ref, ggla_ref, gpm_ref, gpf_ref, wout_ref,
               wrh_ref, wrl_ref, br_ref,
               x1_ref, h2_ref, ti_ref, gt_ref, rk_ref, cnt_ref, base_ref):
    tm = TM_POST

    @pl.when(pl.program_id(0) == 0)
    def _():
        base_ref[...] = jnp.zeros_like(base_ref)

    og = of_ref[...] + ob_ref[...]
    gg = gg_ref[...]
    parts = []
    for h in range(GLA_HEADS):
        cols = slice(h * GLA_DV, (h + 1) * GLA_DV)
        parts.append((_rms(og[:, cols], ggla_ref[...]) * _silu(gg[:, cols])).astype(BF16))
    o = jnp.concatenate(parts + [oa_ref[...]], axis=1)
    y = jnp.dot(o, wout_ref[...], preferred_element_type=F32)

    gate1 = mod_ref[2:3, :]
    shift2 = mod_ref[3:4, :]
    scale2 = mod_ref[4:5, :]
    x1 = x_ref[...] + gate1 * _rms(y, gpm_ref[...])
    x1_ref[...] = x1
    h2 = _rms(x1, gpf_ref[...]) * (1.0 + scale2) + shift2
    h2_hi = h2.astype(BF16)
    h2_hi32 = h2_hi.astype(F32)
    h2_lo = (h2 - h2_hi32).astype(BF16)
    for s in range(ROW_TILES):
        h2_ref[pl.ds(s, tm, stride=SUBLANES), :] = h2_hi32[:, s * LANES:(s + 1) * LANES]

    wrh = wrh_ref[...]
    logits = (lax.dot_general(wrh, h2_hi, NT_DIMS, preferred_element_type=F32)
              + lax.dot_general(wrh, h2_lo, NT_DIMS, preferred_element_type=F32)
              + lax.dot_general(wrl_ref[...], h2_hi, NT_DIMS, preferred_element_type=F32)
              + br_ref[...])
    e_iota = lax.broadcasted_iota(I32, (N_EXPERTS, tm), 0)
    idxs, vals = [], []
    work = logits
    for _ in range(TOP_K):
        m = jnp.max(work, axis=0, keepdims=True)
        idx = jnp.min(jnp.where(work == m, e_iota, N_EXPERTS), axis=0, keepdims=True)
        idxs.append(idx)
        vals.append(m)
        work = jnp.where(e_iota == idx, -jnp.inf, work)
    exps = [jnp.exp(v - vals[0]) for v in vals]
    inv = 1.0 / (exps[0] + exps[1] + exps[2] + exps[3])
    gt_ref[...] = jnp.concatenate([e * inv for e in exps], axis=0)
    ti_ref[...] = jnp.concatenate(idxs, axis=0)

    onehots = [e_iota == idx for idx in idxs]
    member = jnp.where(onehots[0] | onehots[1] | onehots[2] | onehots[3], 1.0, 0.0)
    t_row = lax.broadcasted_iota(I32, (tm, tm), 0)
    t_col = lax.broadcasted_iota(I32, (tm, tm), 1)
    strict = jnp.where(t_row < t_col, 1.0, 0.0).astype(BF16)
    before = base_ref[...] + jnp.dot(member.astype(BF16), strict, preferred_element_type=F32)
    rk_ref[...] = jnp.concatenate(
        [jnp.sum(jnp.where(oh, before, 0.0), axis=0, keepdims=True) for oh in onehots], axis=0).astype(I32)
    new_base = base_ref[...] + jnp.sum(member, axis=1, keepdims=True)
    base_ref[...] = new_base
    cnt_ref[...] = jnp.broadcast_to(new_base, cnt_ref.shape)


def _post(o_f, o_b, gg, o_att, x2, mod, g_gla, g_pm, g_pf, wout, wrh, wrl, br):
    t, d = x2.shape
    tm = TM_POST
    tiles_per_seq = SEQ // tm
    hv = GLA_HEADS * GLA_DV

    def full(a):
        return pl.BlockSpec(a.shape, lambda i: (0,) * a.ndim)

    def rows(w):
        return pl.BlockSpec((tm, w), lambda i: (i, 0))

    def lanes():
        return pl.BlockSpec((TOP_K, tm), lambda i: (0, i))

    return pl.pallas_call(
        _post_body,
        grid=(t // tm,),
        in_specs=[
            rows(hv), rows(hv), rows(hv), rows(hv), rows(d),
            pl.BlockSpec((None, 6, d), lambda i: (i // tiles_per_seq, 0, 0)),
            full(g_gla), full(g_pm), full(g_pf), full(wout), full(wrh), full(wrl), full(br),
        ],
        out_specs=[
            rows(d),
            pl.BlockSpec((tm * SUBLANES, LANES), lambda i: (i, 0)),
            lanes(), lanes(), lanes(),
            pl.BlockSpec((N_EXPERTS, LANES), lambda i: (0, 0)),
        ],
        out_shape=[
            jax.ShapeDtypeStruct((t, d), F32),
            jax.ShapeDtypeStruct((t * SUBLANES, LANES), F32),
            jax.ShapeDtypeStruct((TOP_K, t), I32),
            jax.ShapeDtypeStruct((TOP_K, t), F32),
            jax.ShapeDtypeStruct((TOP_K, t), I32),
            jax.ShapeDtypeStruct((N_EXPERTS, LANES), F32),
        ],
        scratch_shapes=[pltpu.VMEM((N_EXPERTS, 1), F32)],
        compiler_params=_params(("arbitrary",), 48),
        name="post",
    )(o_f, o_b, gg, o_att, x2, mod, g_gla, g_pm, g_pf, wout, wrh, wrl, br)


def _route_body(ti_ref, rk_ref, cnt_ref, pos_ref, be_ref, meta_ref):
    cnt = cnt_ref[...]
    padded = jnp.floor((cnt + (MOE_BM - 1)) * (1.0 / MOE_BM)) * MOE_BM
    starts, ends = [], []
    acc = jnp.zeros((1, LANES), F32)
    for e in range(N_EXPERTS):
        starts.append(acc)
        acc = acc + padded[e:e + 1, :]
        ends.append(acc)
    ti = ti_ref[...]
    off = jnp.zeros(ti.shape, F32)
    for e in range(N_EXPERTS):
        off = jnp.where(ti == e, starts[e][:, 0:1], off)
    pos_ref[...] = rk_ref[...] + off.astype(I32)

    block_start = lax.broadcasted_iota(I32, (1, MOE_NB_PAD), 1).astype(F32) * MOE_BM
    owner = jnp.zeros((1, MOE_NB_PAD), I32)
    for e in range(N_EXPERTS):
        owner = owner + jnp.where(ends[e][:, 0:1] <= block_start, 1, 0)
    be_ref[...] = jnp.minimum(owner, N_EXPERTS - 1)
    meta_ref[...] = jnp.concatenate(
        [starts[e] + cnt[e:e + 1, :] for e in range(N_EXPERTS)] + ends + [acc * (1.0 / MOE_BM)]
        + [jnp.zeros((SUBLANES - 1, LANES), F32)], axis=0).astype(I32)


def _route(top_i, rank, counts):
    return pl.pallas_call(
        _route_body,
        out_shape=[
            jax.ShapeDtypeStruct(top_i.shape, I32),
            jax.ShapeDtypeStruct((1, MOE_NB_PAD), I32),
            jax.ShapeDtypeStruct((2 * N_EXPERTS + SUBLANES, LANES), I32),
        ],
        compiler_params=pltpu.CompilerParams(vmem_limit_bytes=32 * 1024 * 1024),
        name="route",
    )(top_i, rank, counts)


def _row(ref, r):
    return ref.at[pl.ds(pl.multiple_of(r * SUBLANES, SUBLANES), SUBLANES), :]


def _dispatch_body(pos_ref, meta_ref, h2_ref, xs_ref, zero_ref, sem, zsem):
    i = pl.program_id(0)
    tm = TM_DISP
    base = i * tm

    def issue(tl, carry):
        src = _row(h2_ref, tl)
        for k in range(TOP_K):
            p = pos_ref[k * TOKENS + base + tl]
            pltpu.make_async_copy(src, _row(xs_ref, p), sem).start()
        return carry

    lax.fori_loop(0, tm, issue, 0)

    @pl.when(i == 0)
    def _():
        zero_ref[...] = jnp.zeros_like(zero_ref)
        zrow = _row(zero_ref, 0)

        def per_expert(e, carry):
            lo = meta_ref[e]
            hi = meta_ref[N_EXPERTS + e]

            def fill(r, c):
                pltpu.make_async_copy(zrow, _row(xs_ref, r), zsem).start()
                return c

            lax.fori_loop(lo, hi, fill, 0)

            def drain(r, c):
                pltpu.make_async_copy(zrow, _row(xs_ref, r), zsem).wait()
                return c

            lax.fori_loop(lo, hi, drain, 0)
            return carry

        lax.fori_loop(0, N_EXPERTS, per_expert, 0)

        def block(b):
            return xs_ref.at[pl.ds(pl.multiple_of(b * (MOE_BM * SUBLANES), SUBLANES), MOE_BM * SUBLANES), :]

        def fill_tail(b, c):
            pltpu.make_async_copy(zero_ref, block(b), zsem).start()
            return c

        def drain_tail(b, c):
            pltpu.make_async_copy(zero_ref, block(b), zsem).wait()
            return c

        lax.fori_loop(meta_ref[2 * N_EXPERTS], MOE_NB, fill_tail, 0)
        lax.fori_loop(meta_ref[2 * N_EXPERTS], MOE_NB, drain_tail, 0)

    for k in range(TOP_K):
        pltpu.make_async_copy(h2_ref, xs_ref.at[pl.ds(0, tm * SUBLANES), :], sem).wait()


def _dispatch(pos_flat, meta_flat, h2_tiles):
    tm = TM_DISP
    return pl.pallas_call(
        _dispatch_body,
        grid=(TOKENS // tm,),
        in_specs=[
            pl.BlockSpec(memory_space=pltpu.SMEM),
            pl.BlockSpec(memory_space=pltpu.SMEM),
            pl.BlockSpec((tm * SUBLANES, LANES), lambda i: (i, 0)),
        ],
        out_specs=pl.BlockSpec(memory_space=pl.ANY),
        out_shape=jax.ShapeDtypeStruct((MOE_ROWS * SUBLANES, LANES), F32),
        scratch_shapes=[pltpu.VMEM((MOE_BM * SUBLANES, LANES), F32), pltpu.SemaphoreType.DMA,
                        pltpu.SemaphoreType.DMA],
        compiler_params=_params(("arbitrary",), 32),
        name="dispatch",
    )(pos_flat, meta_flat, h2_tiles)


def _experts_body(be_ref, nu_ref, xs_ref, wgu_ref, bgu_ref, wd_ref, bd_ref, ys_ref, wgu_bf, wd_bf):
    i = pl.program_id(0)
    bm = MOE_BM
    used = i < nu_ref[0]
    prev = be_ref[jnp.maximum(i - 1, 0)]
    fresh = (i == 0) | (be_ref[i] != prev)

    @pl.when(used & fresh)
    def _():
        wgu_bf[...] = wgu_ref[...].astype(BF16)
        wd_bf[...] = wd_ref[...].astype(BF16)

    @pl.when(used)
    def _():
        x = jnp.concatenate([xs_ref[pl.ds(s, bm, stride=SUBLANES), :] for s in range(ROW_TILES)], axis=1).astype(BF16)
        gu = jnp.dot(x, wgu_bf[...], preferred_element_type=F32) + bgu_ref[...]
        gate = jnp.minimum(gu[:, 0:D_FF], SWIGLU_LIMIT)
        up = jnp.clip(gu[:, D_FF:2 * D_FF], -SWIGLU_LIMIT, SWIGLU_LIMIT)
        act = ((up + 1.0) * gate * jax.nn.sigmoid(SWIGLU_ALPHA * gate)).astype(BF16)
        y = jnp.dot(act, wd_bf[...], preferred_element_type=F32) + bd_ref[...]
        for s in range(ROW_TILES):
            ys_ref[pl.ds(s, bm, stride=SUBLANES), :] = y[:, s * LANES:(s + 1) * LANES]

    @pl.when(jnp.logical_not(used))
    def _():
        ys_ref[...] = jnp.zeros_like(ys_ref)


def _experts(block_expert, n_used, xs, w_gate_up, b_gate_up, w_down, b_down):
    bm = MOE_BM
    d = D_MODEL

    def blk(i, be, nu):
        return jnp.minimum(i, nu[0] - 1)

    grid_spec = pltpu.PrefetchScalarGridSpec(
        num_scalar_prefetch=2,
        grid=(MOE_NB,),
        in_specs=[
            pl.BlockSpec((bm * SUBLANES, LANES), lambda i, be, nu: (blk(i, be, nu), 0)),
            pl.BlockSpec((None, d, 2 * D_FF), lambda i, be, nu: (be[blk(i, be, nu)], 0, 0)),
            pl.BlockSpec((None, 1, 2 * D_FF), lambda i, be, nu: (be[blk(i, be, nu)], 0, 0)),
            pl.BlockSpec((None, D_FF, d), lambda i, be, nu: (be[blk(i, be, nu)], 0, 0)),
            pl.BlockSpec((None, 1, d), lambda i, be, nu: (be[blk(i, be, nu)], 0, 0)),
        ],
        out_specs=pl.BlockSpec((bm * SUBLANES, LANES), lambda i, be, nu: (i, 0)),
        scratch_shapes=[pltpu.VMEM((d, 2 * D_FF), BF16), pltpu.VMEM((D_FF, d), BF16)],
    )
    return pl.pallas_call(
        _experts_body,
        grid_spec=grid_spec,
        out_shape=jax.ShapeDtypeStruct((MOE_ROWS * SUBLANES, LANES), F32),
        compiler_params=_params(("arbitrary",), 56),
        name="experts",
    )(block_expert, n_used, xs, w_gate_up, b_gate_up, w_down, b_down)


def _sc_gather_rows(table, idx):
    n = idx.shape[0]
    window = SC_GATHER_WINDOW
    info = plsc.get_sparse_core_info()
    workers = info.num_cores * info.num_subcores
    per_worker = n // workers
    chunks = per_worker // window
    assert per_worker * workers == n and chunks * window == per_worker and chunks % 2 == 0
    row_shape = table.shape[1:]
    mesh = plsc.VectorSubcoreMesh(core_axis_name="core", subcore_axis_name="subcore")

    @functools.partial(
        pl.kernel,
        out_type=jax.ShapeDtypeStruct((n,) + row_shape, table.dtype),
        mesh=mesh,
        scratch_types=[
            pltpu.VMEM((per_worker,), I32),
            pltpu.VMEM((window,) + row_shape, table.dtype),
            pltpu.VMEM((window,) + row_shape, table.dtype),
            pltpu.SemaphoreType.DMA,
            pltpu.SemaphoreType.DMA,
        ],
        name="sc_gather_rows",
    )
    def gather(table_hbm, idx_hbm, out_hbm, idx_v, buf0, buf1, sem0, sem1):
        wid = lax.axis_index("subcore") * info.num_cores + lax.axis_index("core")
        base = wid * per_worker
        pltpu.sync_copy(idx_hbm.at[pl.ds(base, per_worker)], idx_v)

        def fetch(c, buf, sem):
            return pltpu.make_async_copy(table_hbm.at[idx_v.at[pl.ds(c * window, window)]], buf, sem)

        def flush(c, buf):
            pltpu.sync_copy(buf, out_hbm.at[pl.ds(base + c * window, window)])

        fetch(0, buf0, sem0).start()

        @pl.loop(0, chunks, step=2)
        def _(c):
            fetch(c + 1, buf1, sem1).start()
            fetch(c, buf0, sem0).wait()
            flush(c, buf0)

            @pl.when(c + 2 < chunks)
            def _():
                fetch(c + 2, buf0, sem0).start()

            fetch(c + 1, buf1, sem1).wait()
            flush(c + 1, buf1)

    return gather(table, idx)


def _combine_body(gates_ref, x1_ref, mod_ref, gpost_ref, y0_ref, y1_ref, y2_ref, y3_ref, o_ref):
    tm = TM_COMB
    gates = gates_ref[...]
    y = jnp.zeros((tm, D_MODEL), F32)
    for k, yk_ref in enumerate((y0_ref, y1_ref, y2_ref, y3_ref)):
        yk = jnp.concatenate([yk_ref[pl.ds(s, tm, stride=SUBLANES), :] for s in range(ROW_TILES)], axis=1)
        y = y + yk * gates[:, k:k + 1]
    gate2 = mod_ref[5:6, :]
    o_ref[...] = x1_ref[...] + gate2 * _rms(y, gpost_ref[...])


def _combine(gates_t, x1, mod, g_post, y4):
    t, d = x1.shape
    tm = TM_COMB
    tiles = t // tm
    tiles_per_seq = SEQ // tm

    def slab(k):
        return pl.BlockSpec((tm * SUBLANES, LANES), lambda i: (k * tiles + i, 0))

    return pl.pallas_call(
        _combine_body,
        grid=(tiles,),
        in_specs=[
            pl.BlockSpec((tm, TOP_K), lambda i: (i, 0)),
            pl.BlockSpec((tm, d), lambda i: (i, 0)),
            pl.BlockSpec((None, 6, d), lambda i: (i // tiles_per_seq, 0, 0)),
            pl.BlockSpec(g_post.shape, lambda i: (0, 0)),
            slab(0), slab(1), slab(2), slab(3),
        ],
        out_specs=pl.BlockSpec((tm, d), lambda i: (i, 0)),
        out_shape=jax.ShapeDtypeStruct((t, d), F32),
        compiler_params=_params(("arbitrary",), 48),
        name="combine",
    )(gates_t, x1, mod, g_post, y4, y4, y4, y4)


def _rotary_tables():
    half = ROT_DIM // 2
    inv_freq = ROPE_THETA ** (-2.0 * jnp.arange(half, dtype=F32) / ROT_DIM)
    ang = jnp.arange(SEQ).astype(F32)[:, None] * inv_freq[None, :]
    cos, sin = jnp.cos(ang), jnp.sin(ang)
    ones = jnp.ones((SEQ, ATT_HEAD_DIM - ROT_DIM), F32)
    zeros = jnp.zeros((SEQ, ATT_HEAD_DIM - ROT_DIM), F32)
    zh = jnp.zeros((SEQ, half), F32)
    reps = LANES // ATT_HEAD_DIM
    rc = jnp.tile(jnp.concatenate([cos, cos, ones], axis=1), (1, reps))
    rm = jnp.tile(jnp.concatenate([-sin, zh, zeros], axis=1), (1, reps))
    rp = jnp.tile(jnp.concatenate([zh, sin, zeros], axis=1), (1, reps))
    return rc, rm, rp


def kernel(x, c, w_ada, b_ada, g_pre_mix, g_post_mix, w_in, w_gk_fwd, b_gk_fwd, w_gk_bwd, b_gk_bwd, g_gla_out,
           attn_sink, w_out, g_pre_ffn, g_post_ffn, w_router, b_router, w_gate_up, b_gate_up, w_down, b_down):
    assert x.shape == (BATCH, SEQ, D_MODEL) and w_ada.shape[0] == 1
    d = D_MODEL
    x2 = x.reshape(TOKENS, d)

    c_pad = jnp.pad(c, ((0, SUBLANES - BATCH), (0, 0)))
    mod = _ada(c_pad, w_ada[0], b_ada)[:BATCH].reshape(BATCH, 6, d)

    hk = GLA_HEADS * GLA_DK
    hv = GLA_HEADS * GLA_DV
    w = w_in[0]
    o_lr = 2 * hk + 2 * hv
    o_aq = o_lr + 2 * GLA_RANK
    o_ak = o_aq + ATT_Q_HEADS * ATT_HEAD_DIM
    o_av = o_ak + ATT_KV_HEADS * ATT_HEAD_DIM
    hd = ATT_HEAD_DIM
    wa = w[:, :o_lr].astype(BF16)
    wlr = w[:, o_lr:o_aq].astype(BF16)
    dup = lambda m: jnp.concatenate([m[:, g * hd:(g + 1) * hd] for g in range(ATT_KV_HEADS) for _ in range(2)], axis=1)
    wb = jnp.concatenate([w[:, o_aq:o_ak], dup(w[:, o_ak:o_av]), dup(w[:, o_av:o_av + ATT_KV_HEADS * hd])], axis=1).astype(BF16)
    zr = jnp.zeros((GLA_RANK, hk), F32)
    wgk = jnp.concatenate([jnp.concatenate([w_gk_fwd[0], zr], axis=1),
                           jnp.concatenate([zr, w_gk_bwd[0]], axis=1)], axis=0).astype(BF16)
    bgk = jnp.concatenate([b_gk_fwd[0], b_gk_bwd[0]])[None, :]
    rc, rm, rp = _rotary_tables()

    q, k, v, gg, laf, lab, aq, ak2, av2 = _inproj(x2, mod, g_pre_mix, wa, wlr, wgk, bgk, wb, rc, rm, rp)
    o_f, o_b = _gla(q, k, v, laf, lab)
    o_att = _attn(attn_sink[0], aq, ak2, av2)

    wr_t = w_router[0].T
    wrh = wr_t.astype(BF16)
    wrl = (wr_t - wrh.astype(F32)).astype(BF16)
    x1, h2_tiles, top_i, gates, rank, counts = _post(
        o_f, o_b, gg, o_att, x2, mod, g_gla_out, g_post_mix, g_pre_ffn, w_out[0].astype(BF16), wrh, wrl,
        b_router[0][:, None])

    pos, block_expert, meta = _route(top_i, rank, counts)
    pos_flat = pos.reshape(TOP_K * TOKENS)
    meta_flat = meta[:, 0]
    xs = _dispatch(pos_flat, meta_flat, h2_tiles)
    ys = _experts(block_expert[0, :MOE_NB], meta_flat[2 * N_EXPERTS:2 * N_EXPERTS + 1], xs,
                  w_gate_up[0], b_gate_up[0][:, None, :], w_down[0], b_down[0][:, None, :])
    y4 = _sc_gather_rows(ys.reshape(MOE_ROWS, SUBLANES, LANES), pos_flat)
    out = _combine(gates.T, x1, mod, g_post_ffn, y4.reshape(TOP_K * TOKENS * SUBLANES, LANES))
    return out.reshape(BATCH, SEQ, d)
```

```python
import functools

import jax
import jax.numpy as jnp
import numpy as np
from jax import lax
from jax.experimental import pallas as pl
from jax.experimental.pallas import tpu as pltpu
from jax.experimental.pallas import tpu_sc as plsc

F32 = jnp.float32
BF16 = jnp.bfloat16
I32 = jnp.int32

D_MODEL = 1024
BATCH = 2
SEQ = 8192
TOKENS = BATCH * SEQ
GLA_HEADS = 4
GLA_DV = 128
GLA_DK = 64
GLA_RANK = 16
GLA_GATE_NORMALIZER = 16.0
GLA_CHUNK = 64
ATT_Q_HEADS = 8
ATT_KV_HEADS = 2
ATT_HEAD_DIM = 64
ATT_WINDOW = 128
ATT_BLOCK = 128
ROT_DIM = 16
ROPE_THETA = 500000.0
N_EXPERTS = 32
TOP_K = 4
D_FF = 1024
SWIGLU_LIMIT = 7.0
SWIGLU_ALPHA = 1.702
NORM_EPS = 1e-6
NEG_INF = -1e30

LANES = 128
SUBLANES = 8
ROW_TILES = D_MODEL // LANES

TM_IN = 512
GLA_GROUP = 4
TM_POST = 256
MOE_BM = 256
MOE_ROWS = TOKENS * TOP_K + N_EXPERTS * MOE_BM
MOE_NB = MOE_ROWS // MOE_BM
MOE_NB_PAD = ((MOE_NB + LANES - 1) // LANES) * LANES
SC_SCAN_CHUNK = 4096
TM_COMB = 256
SC_GATHER_WINDOW = 32

NT_DIMS = (((1,), (1,)), ((), ()))
TN_DIMS = (((0,), (0,)), ((), ()))


def _params(semantics, vmem_mib):
    return pltpu.CompilerParams(dimension_semantics=semantics, vmem_limit_bytes=vmem_mib * 1024 * 1024)


def _rms(x, g):
    return x * lax.rsqrt(jnp.mean(x * x, axis=-1, keepdims=True) + NORM_EPS) * g


def _silu(x):
    return x * jax.nn.sigmoid(x)


def _ada_body(c_ref, w_ref, b_ref, o_ref):
    ca = _silu(c_ref[...]).astype(BF16)
    o_ref[...] = jnp.dot(ca, w_ref[...].astype(BF16), preferred_element_type=F32) + b_ref[...]


def _ada(c_pad, w_ada, b_ada):
    d = D_MODEL
    return pl.pallas_call(
        _ada_body,
        grid=(6,),
        in_specs=[
            pl.BlockSpec((SUBLANES, d), lambda j: (0, 0)),
            pl.BlockSpec((d, d), lambda j: (0, j)),
            pl.BlockSpec((1, d), lambda j: (0, j)),
        ],
        out_specs=pl.BlockSpec((SUBLANES, d), lambda j: (0, j)),
        out_shape=jax.ShapeDtypeStruct((SUBLANES, 6 * d), F32),
        compiler_params=_params(("arbitrary",), 32),
        name="ada",
    )(c_pad, w_ada, b_ada)


def _rotary(x, cos_t, msin_t, psin_t):
    width = x.shape[1]
    reps = width // LANES
    c = jnp.concatenate([cos_t] * reps, axis=1)
    m = jnp.concatenate([msin_t] * reps, axis=1)
    p = jnp.concatenate([psin_t] * reps, axis=1)
    half = ROT_DIM // 2
    return x * c + pltpu.roll(x, width - half, 1) * m + pltpu.roll(x, half, 1) * p


def _inproj_body(x_ref, mod_ref, g_ref, wa_ref, wlr_ref, wgk_ref, bgk_ref, wb_ref, rc_ref, rm_ref, rp_ref,
                 q_ref, k_ref, v_ref, gg_ref, laf_ref, lab_ref, aq_ref, ak_ref, av_ref):
    shift = mod_ref[0:1, :]
    scale = mod_ref[1:2, :]
    h = (_rms(x_ref[...], g_ref[...]) * (1.0 + scale) + shift).astype(BF16)

    hk = GLA_HEADS * GLA_DK
    hv = GLA_HEADS * GLA_DV
    pa = jnp.dot(h, wa_ref[...], preferred_element_type=F32)
    q_ref[...] = pa[:, 0:hk] * (GLA_DK ** -0.5)
    k_ref[...] = pa[:, hk:2 * hk]
    v_ref[...] = pa[:, 2 * hk:2 * hk + hv].astype(BF16)
    gg_ref[...] = pa[:, 2 * hk + hv:2 * hk + 2 * hv]

    plr = jnp.dot(h, wlr_ref[...], preferred_element_type=F32)
    gk = jnp.dot(plr.astype(BF16), wgk_ref[...], preferred_element_type=F32) + bgk_ref[...]
    la = (jnp.minimum(gk, 0.0) - jnp.log1p(jnp.exp(-jnp.abs(gk)))) * (1.0 / GLA_GATE_NORMALIZER)
    laf_ref[...] = la[:, 0:hk]
    lab_ref[...] = la[:, hk:2 * hk]

    pb = jnp.dot(h, wb_ref[...], preferred_element_type=F32)
    nq = ATT_Q_HEADS * ATT_HEAD_DIM
    nk = 2 * ATT_KV_HEADS * ATT_HEAD_DIM
    rc, rm, rp = rc_ref[...], rm_ref[...], rp_ref[...]
    aq_ref[...] = (_rotary(pb[:, 0:nq], rc, rm, rp) * (ATT_HEAD_DIM ** -0.5)).astype(BF16)
    ak_ref[...] = _rotary(pb[:, nq:nq + nk], rc, rm, rp).astype(BF16)
    av_ref[...] = pb[:, nq + nk:nq + 2 * nk].astype(BF16)


def _inproj(x2, mod, g_pre, wa, wlr, wgk, bgk, wb, rc, rm, rp):
    t, d = x2.shape
    tm = TM_IN
    tiles_per_seq = SEQ // tm
    hk = GLA_HEADS * GLA_DK
    hv = GLA_HEADS * GLA_DV
    nq = ATT_Q_HEADS * ATT_HEAD_DIM
    nk = 2 * ATT_KV_HEADS * ATT_HEAD_DIM

    def full(a):
        return pl.BlockSpec(a.shape, lambda i: (0,) * a.ndim)

    def rows(w):
        return pl.BlockSpec((tm, w), lambda i: (i, 0))

    def table():
        return pl.BlockSpec((tm, LANES), lambda i: (i % tiles_per_seq, 0))

    out_widths = [(hk, F32), (hk, F32), (hv, BF16), (hv, F32), (hk, F32), (hk, F32), (nq, BF16), (nk, BF16), (nk, BF16)]
    return pl.pallas_call(
        _inproj_body,
        grid=(t // tm,),
        in_specs=[
            rows(d),
            pl.BlockSpec((None, 6, d), lambda i: (i // tiles_per_seq, 0, 0)),
            full(g_pre), full(wa), full(wlr), full(wgk), full(bgk), full(wb),
            table(), table(), table(),
        ],
        out_specs=[rows(w) for w, _ in out_widths],
        out_shape=[jax.ShapeDtypeStruct((t, w), dt) for w, dt in out_widths],
        compiler_params=_params(("arbitrary",), 56),
        name="inproj",
    )(x2, mod, g_pre, wa, wlr, wgk, bgk, wb, rc, rm, rp)


def _gla_chunk(q, k, la, v_ref, rows, o_ref, states, cum, tri, i_last, i_mid, head_masks):
    hi = la.astype(BF16)
    lo = (la - hi.astype(F32)).astype(BF16)
    b = jnp.dot(cum, hi, preferred_element_type=F32) + jnp.dot(cum, lo, preferred_element_type=F32)
    b_last = b[i_last:i_last + 1, :]
    b_mid = b[i_mid:i_mid + 1, :]
    qs = q * jnp.exp(b - b_mid)
    ks = k * jnp.exp(b_mid - b)
    qi = q * jnp.exp(b)
    kst = k * jnp.exp(b_last - b)
    decay = jnp.exp(b_last)
    new_states = []
    for h in range(GLA_HEADS):
        pair = slice((h // 2) * LANES, (h // 2 + 1) * LANES)
        mask = head_masks[h % 2]
        vcols = slice(h * GLA_DV, (h + 1) * GLA_DV)
        qs_h = jnp.where(mask, qs[:, pair], 0.0).astype(BF16)
        sc = lax.dot_general(qs_h, ks[:, pair].astype(BF16), NT_DIMS, preferred_element_type=F32)
        sc = jnp.where(tri, sc, 0.0)
        v_h = v_ref[rows, vcols]
        st = states[h]
        qi_h = jnp.where(mask, qi[:, pair], 0.0).astype(BF16)
        o = jnp.dot(sc.astype(BF16), v_h, preferred_element_type=F32)
        o = o + lax.dot_general(qi_h, st.astype(BF16), NT_DIMS, preferred_element_type=F32)
        o_ref[rows, vcols] = o
        kv = lax.dot_general(v_h, kst[:, pair].astype(BF16), TN_DIMS, preferred_element_type=F32)
        new_states.append(st * decay[:, pair] + kv)
    return new_states


def _gla_body(qf_ref, kf_ref, vf_ref, laf_ref, qb_ref, kb_ref, vb_ref, lab_ref, of_ref, ob_ref, sf_ref, sb_ref):
    @pl.when(pl.program_id(1) == 0)
    def _():
        sf_ref[...] = jnp.zeros_like(sf_ref)
        sb_ref[...] = jnp.zeros_like(sb_ref)

    c = GLA_CHUNK
    r_i = lax.broadcasted_iota(I32, (c, c), 0)
    c_i = lax.broadcasted_iota(I32, (c, c), 1)
    lower = c_i <= r_i
    upper = c_i >= r_i
    cum_f = jnp.where(lower, 1.0, 0.0).astype(BF16)
    cum_b = jnp.where(upper, 1.0, 0.0).astype(BF16)
    lane = lax.broadcasted_iota(I32, (1, LANES), 1)
    head_masks = (lane < GLA_DK, lane >= GLA_DK)

    st_f = [sf_ref[h] for h in range(GLA_HEADS)]
    st_b = [sb_ref[h] for h in range(GLA_HEADS)]
    for g in range(GLA_GROUP):
        rows_f = slice(g * c, (g + 1) * c)
        gb = GLA_GROUP - 1 - g
        rows_b = slice(gb * c, (gb + 1) * c)
        st_f = _gla_chunk(qf_ref[rows_f, :], kf_ref[rows_f, :], laf_ref[rows_f, :], vf_ref, rows_f, of_ref,
                          st_f, cum_f, lower, c - 1, c // 2 - 1, head_masks)
        st_b = _gla_chunk(qb_ref[rows_b, :], kb_ref[rows_b, :], lab_ref[rows_b, :], vb_ref, rows_b, ob_ref,
                          st_b, cum_b, upper, 0, c // 2, head_masks)
    for h in range(GLA_HEADS):
        sf_ref[h] = st_f[h]
        sb_ref[h] = st_b[h]


def _gla(q, k, v, laf, lab):
    t = q.shape[0]
    rows = GLA_GROUP * GLA_CHUNK
    ng = SEQ // rows
    hk = GLA_HEADS * GLA_DK
    hv = GLA_HEADS * GLA_DV

    def fwd(w):
        return pl.BlockSpec((rows, w), lambda b, n: (b * ng + n, 0))

    def bwd(w):
        return pl.BlockSpec((rows, w), lambda b, n: (b * ng + ng - 1 - n, 0))

    return pl.pallas_call(
        _gla_body,
        grid=(BATCH, ng),
        in_specs=[fwd(hk), fwd(hk), fwd(hv), fwd(hk), bwd(hk), bwd(hk), bwd(hv), bwd(hk)],
        out_specs=[fwd(hv), bwd(hv)],
        out_shape=[jax.ShapeDtypeStruct((t, hv), F32)] * 2,
        scratch_shapes=[pltpu.VMEM((GLA_HEADS, GLA_DV, 2 * GLA_DK), F32)] * 2,
        compiler_params=_params(("arbitrary", "arbitrary"), 32),
        name="gla",
    )(q, k, v, laf, q, k, v, lab)


def _attn_body(sink_ref, q_ref, kp_ref, kc_ref, kn_ref, vp_ref, vc_ref, vn_ref, o_ref):
    n = pl.program_id(1)
    nb = pl.num_programs(1)
    qb = ATT_BLOCK
    k_all = jnp.concatenate([kp_ref[...], kc_ref[...], kn_ref[...]], axis=0)
    v_all = jnp.concatenate([vp_ref[...], vc_ref[...], vn_ref[...]], axis=0)
    lane = lax.broadcasted_iota(I32, (1, LANES), 1)
    lo = lane < ATT_HEAD_DIM
    i_q = lax.broadcasted_iota(I32, (qb, 3 * qb), 0)
    j_k = lax.broadcasted_iota(I32, (qb, 3 * qb), 1)
    rel = j_k - qb - i_q
    valid = (jnp.abs(rel) <= ATT_WINDOW) & ((j_k >= qb) | (n > 0)) & ((j_k < 2 * qb) | (n < nb - 1))
    valid4 = jnp.concatenate([valid] * 4, axis=0)
    for g in range(ATT_KV_HEADS):
        kg = k_all[:, g * LANES:(g + 1) * LANES]
        vg = v_all[:, g * LANES:(g + 1) * LANES]
        qa = q_ref[:, (2 * g) * LANES:(2 * g + 1) * LANES]
        qc = q_ref[:, (2 * g + 1) * LANES:(2 * g + 2) * LANES]
        zero = jnp.zeros_like(qa)
        lhs = jnp.concatenate([jnp.where(lo, qa, zero), jnp.where(lo, zero, qa),
                               jnp.where(lo, qc, zero), jnp.where(lo, zero, qc)], axis=0)
        s = lax.dot_general(lhs, kg, NT_DIMS, preferred_element_type=F32)
        s = jnp.where(valid4, s, NEG_INF)
        sink = jnp.concatenate([jnp.full((qb, 1), sink_ref[4 * g + r], F32) for r in range(4)], axis=0)
        m = jnp.maximum(jnp.max(s, axis=-1, keepdims=True), sink)
        p = jnp.exp(s - m)
        denom = jnp.sum(p, axis=-1, keepdims=True) + jnp.exp(sink - m)
        o = jnp.dot(p.astype(BF16), vg, preferred_element_type=F32) / denom
        o_ref[:, (2 * g) * LANES:(2 * g + 1) * LANES] = jnp.where(lo, o[0:qb], o[qb:2 * qb]).astype(o_ref.dtype)
        o_ref[:, (2 * g + 1) * LANES:(2 * g + 2) * LANES] = jnp.where(lo, o[2 * qb:3 * qb], o[3 * qb:4 * qb]).astype(o_ref.dtype)


def _attn(sink, aq, ak2, av2):
    t = aq.shape[0]
    qb = ATT_BLOCK
    nb = SEQ // qb
    nq = ATT_Q_HEADS * ATT_HEAD_DIM
    nk = 2 * ATT_KV_HEADS * ATT_HEAD_DIM

    def kv(shift):
        return pl.BlockSpec((qb, nk), lambda b, n: (b * nb + jnp.clip(n + shift, 0, nb - 1), 0))

    return pl.pallas_call(
        _attn_body,
        grid=(BATCH, nb),
        in_specs=[
            pl.BlockSpec(memory_space=pltpu.SMEM),
            pl.BlockSpec((qb, nq), lambda b, n: (b * nb + n, 0)),
            kv(-1), kv(0), kv(1), kv(-1), kv(0), kv(1),
        ],
        out_specs=pl.BlockSpec((qb, nq), lambda b, n: (b * nb + n, 0)),
        out_shape=jax.ShapeDtypeStruct((t, nq), BF16),
        compiler_params=_params(("arbitrary", "arbitrary"), 32),
        name="attn",
    )(sink, aq, ak2, ak2, ak2, av2, av2, av2)


def _post_body(of_ref, ob_ref, gg_ref, oa_ref, x_ref, mod_ref, ggla_ref, gpm_ref, gpf_ref, wout_ref,
               wrh_ref, wrl_ref, br_ref,
               x1_ref, h2_ref, ti_ref, gt_ref, rk_ref, cnt_ref, base_ref):
    tm = TM_POST

    @pl.when(pl.program_id(0) == 0)
    def _():
        base_ref[...] = jnp.zeros_like(base_ref)

    og = of_ref[...] + ob_ref[...]
    gg = gg_ref[...]
    parts = []
    for h in range(GLA_HEADS):
        cols = slice(h * GLA_DV, (h + 1) * GLA_DV)
        parts.append((_rms(og[:, cols], ggla_ref[...]) * _silu(gg[:, cols])).astype(BF16))
    o = jnp.concatenate(parts + [oa_ref[...]], axis=1)
    y = jnp.dot(o, wout_ref[...], preferred_element_type=F32)

    gate1 = mod_ref[2:3, :]
    shift2 = mod_ref[3:4, :]
    scale2 = mod_ref[4:5, :]
    x1 = x_ref[...] + gate1 * _rms(y, gpm_ref[...])
    x1_ref[...] = x1
    h2 = _rms(x1, gpf_ref[...]) * (1.0 + scale2) + shift2
    h2_hi = h2.astype(BF16)
    h2_hi32 = h2_hi.astype(F32)
    h2_lo = (h2 - h2_hi32).astype(BF16)
    for s in range(ROW_TILES):
        h2_ref[pl.ds(s, tm, stride=SUBLANES), :] = h2_hi32[:, s * LANES:(s + 1) * LANES]

    wrh = wrh_ref[...]
    logits = (lax.dot_general(wrh, h2_hi, NT_DIMS, preferred_element_type=F32)
              + lax.dot_general(wrh, h2_lo, NT_DIMS, preferred_element_type=F32)
              + lax.dot_general(wrl_ref[...], h2_hi, NT_DIMS, preferred_element_type=F32)
              + br_ref[...])
    e_iota = lax.broadcasted_iota(I32, (N_EXPERTS, tm), 0)
    idxs, vals = [], []
    work = logits
    for _ in range(TOP_K):
        m = jnp.max(work, axis=0, keepdims=True)
        idx = jnp.min(jnp.where(work == m, e_iota, N_EXPERTS), axis=0, keepdims=True)
        idxs.append(idx)
        vals.append(m)
        work = jnp.where(e_iota == idx, -jnp.inf, work)
    exps = [jnp.exp(v - vals[0]) for v in vals]
    inv = 1.0 / (exps[0] + exps[1] + exps[2] + exps[3])
    gt_ref[...] = jnp.concatenate([e * inv for e in exps], axis=0)
    ti_ref[...] = jnp.concatenate(idxs, axis=0)

    onehots = [e_iota == idx for idx in idxs]
    member = jnp.where(onehots[0] | onehots[1] | onehots[2] | onehots[3], 1.0, 0.0)
    t_row = lax.broadcasted_iota(I32, (tm, tm), 0)
    t_col = lax.broadcasted_iota(I32, (tm, tm), 1)
    strict = jnp.where(t_row < t_col, 1.0, 0.0).astype(BF16)
    before = base_ref[...] + jnp.dot(member.astype(BF16), strict, preferred_element_type=F32)
    rk_ref[...] = jnp.concatenate(
        [jnp.sum(jnp.where(oh, before, 0.0), axis=0, keepdims=True) for oh in onehots], axis=0).astype(I32)
    new_base = base_ref[...] + jnp.sum(member, axis=1, keepdims=True)
    base_ref[...] = new_base
    cnt_ref[...] = jnp.broadcast_to(new_base, cnt_ref.shape)


def _post(o_f, o_b, gg, o_att, x2, mod, g_gla, g_pm, g_pf, wout, wrh, wrl, br):
    t, d = x2.shape
    tm = TM_POST
    tiles_per_seq = SEQ // tm
    hv = GLA_HEADS * GLA_DV

    def full(a):
        return pl.BlockSpec(a.shape, lambda i: (0,) * a.ndim)

    def rows(w):
        return pl.BlockSpec((tm, w), lambda i: (i, 0))

    def lanes():
        return pl.BlockSpec((TOP_K, tm), lambda i: (0, i))

    return pl.pallas_call(
        _post_body,
        grid=(t // tm,),
        in_specs=[
            rows(hv), rows(hv), rows(hv), rows(hv), rows(d),
            pl.BlockSpec((None, 6, d), lambda i: (i // tiles_per_seq, 0, 0)),
            full(g_gla), full(g_pm), full(g_pf), full(wout), full(wrh), full(wrl), full(br),
        ],
        out_specs=[
            rows(d),
            pl.BlockSpec((tm * SUBLANES, LANES), lambda i: (i, 0)),
            lanes(), lanes(), lanes(),
            pl.BlockSpec((N_EXPERTS, LANES), lambda i: (0, 0)),
        ],
        out_shape=[
            jax.ShapeDtypeStruct((t, d), F32),
            jax.ShapeDtypeStruct((t * SUBLANES, LANES), F32),
            jax.ShapeDtypeStruct((TOP_K, t), I32),
            jax.ShapeDtypeStruct((TOP_K, t), F32),
            jax.ShapeDtypeStruct((TOP_K, t), I32),
            jax.ShapeDtypeStruct((N_EXPERTS, LANES), F32),
        ],
        scratch_shapes=[pltpu.VMEM((N_EXPERTS, 1), F32)],
        compiler_params=_params(("arbitrary",), 48),
        name="post",
    )(o_f, o_b, gg, o_att, x2, mod, g_gla, g_pm, g_pf, wout, wrh, wrl, br)


def _route_body(ti_ref, rk_ref, cnt_ref, pos_ref, be_ref, meta_ref):
    cnt = cnt_ref[...]
    padded = jnp.floor((cnt + (MOE_BM - 1)) * (1.0 / MOE_BM)) * MOE_BM
    starts, ends = [], []
    acc = jnp.zeros((1, LANES), F32)
    for e in range(N_EXPERTS):
        starts.append(acc)
        acc = acc + padded[e:e + 1, :]
        ends.append(acc)
    ti = ti_ref[...]
    off = jnp.zeros(ti.shape, F32)
    for e in range(N_EXPERTS):
        off = jnp.where(ti == e, starts[e][:, 0:1], off)
    pos_ref[...] = rk_ref[...] + off.astype(I32)

    block_start = lax.broadcasted_iota(I32, (1, MOE_NB_PAD), 1).astype(F32) * MOE_BM
    owner = jnp.zeros((1, MOE_NB_PAD), I32)
    for e in range(N_EXPERTS):
        owner = owner + jnp.where(ends[e][:, 0:1] <= block_start, 1, 0)
    be_ref[...] = jnp.minimum(owner, N_EXPERTS - 1)
    meta_ref[...] = jnp.concatenate(
        [starts[e] + cnt[e:e + 1, :] for e in range(N_EXPERTS)] + ends + [acc * (1.0 / MOE_BM)]
        + [jnp.zeros((SUBLANES - 1, LANES), F32)], axis=0).astype(I32)


def _route(top_i, rank, counts):
    return pl.pallas_call(
        _route_body,
        out_shape=[
            jax.ShapeDtypeStruct(top_i.shape, I32),
            jax.ShapeDtypeStruct((1, MOE_NB_PAD), I32),
            jax.ShapeDtypeStruct((2 * N_EXPERTS + SUBLANES, LANES), I32),
        ],
        compiler_params=pltpu.CompilerParams(vmem_limit_bytes=32 * 1024 * 1024),
        name="route",
    )(top_i, rank, counts)


def _sc_workers():
    info = plsc.get_sparse_core_info()
    return info.num_cores, info.num_subcores, info.num_lanes


def _sc_gather_loop(table_hbm, out_hbm, idx_v, base, chunks, buf0, buf1, sem0, sem1):
    window = SC_GATHER_WINDOW

    def fetch(c, buf, sem):
        return pltpu.make_async_copy(table_hbm.at[idx_v.at[pl.ds(c * window, window)]], buf, sem)

    def flush(c, buf):
        pltpu.sync_copy(buf, out_hbm.at[pl.ds(base + c * window, window)])

    fetch(0, buf0, sem0).start()

    @pl.loop(0, chunks, step=2)
    def _(c):
        fetch(c + 1, buf1, sem1).start()
        fetch(c, buf0, sem0).wait()
        flush(c, buf0)

        @pl.when(c + 2 < chunks)
        def _():
            fetch(c + 2, buf0, sem0).start()

        fetch(c + 1, buf1, sem1).wait()
        flush(c + 1, buf1)


def _sc_dispatch_rows(table, pos_flat, n_rows):
    cores, subcores, lanes = _sc_workers()
    workers = cores * subcores
    window = SC_GATHER_WINDOW
    per_worker = n_rows // workers
    chunks = per_worker // window
    n_assign = pos_flat.shape[0]
    scan = SC_SCAN_CHUNK
    assert per_worker * workers == n_rows and chunks * window == per_worker and chunks % 2 == 0
    assert n_assign % scan == 0 and scan % lanes == 0 and per_worker % lanes == 0
    row_shape = table.shape[1:]
    mesh = plsc.VectorSubcoreMesh(core_axis_name="core", subcore_axis_name="subcore")

    @functools.partial(
        pl.kernel,
        out_type=jax.ShapeDtypeStruct((n_rows,) + row_shape, table.dtype),
        mesh=mesh,
        scratch_types=[
            pltpu.VMEM((per_worker,), I32),
            pltpu.VMEM((scan,), I32),
            pltpu.VMEM((window,) + row_shape, table.dtype),
            pltpu.VMEM((window,) + row_shape, table.dtype),
            pltpu.SemaphoreType.DMA,
            pltpu.SemaphoreType.DMA,
        ],
        compiler_params=pltpu.CompilerParams(needs_layout_passes=False),
        name="sc_dispatch_rows",
    )
    def dispatch(table_hbm, pos_hbm, out_hbm, src_v, pos_v, buf0, buf1, sem0, sem1):
        wid = lax.axis_index("subcore") * cores + lax.axis_index("core")
        base = wid * per_worker
        lane = lax.iota(I32, lanes)

        @pl.loop(0, per_worker, step=lanes)
        def _(j):
            src_v[pl.ds(j, lanes)] = (base + j + lane) & (TOKENS - 1)

        @pl.loop(0, n_assign, step=scan)
        def _(a0):
            pltpu.sync_copy(pos_hbm.at[pl.ds(a0, scan)], pos_v)

            @pl.loop(0, scan, step=lanes)
            def _(j):
                rel = pos_v[pl.ds(j, lanes)] - base
                mine = (rel >= 0) & (rel < per_worker)
                tok = (a0 + j + lane) & (TOKENS - 1)
                plsc.store_scatter(src_v, [jnp.where(mine, rel, 0)], tok, mask=mine)

        _sc_gather_loop(table_hbm, out_hbm, src_v, base, chunks, buf0, buf1, sem0, sem1)

    return dispatch(table, pos_flat)


def _sc_gather_rows(table, idx):
    cores, subcores, _ = _sc_workers()
    workers = cores * subcores
    n = idx.shape[0]
    window = SC_GATHER_WINDOW
    per_worker = n // workers
    chunks = per_worker // window
    assert per_worker * workers == n and chunks * window == per_worker and chunks % 2 == 0
    row_shape = table.shape[1:]
    mesh = plsc.VectorSubcoreMesh(core_axis_name="core", subcore_axis_name="subcore")

    @functools.partial(
        pl.kernel,
        out_type=jax.ShapeDtypeStruct((n,) + row_shape, table.dtype),
        mesh=mesh,
        scratch_types=[
            pltpu.VMEM((per_worker,), I32),
            pltpu.VMEM((window,) + row_shape, table.dtype),
            pltpu.VMEM((window,) + row_shape, table.dtype),
            pltpu.SemaphoreType.DMA,
            pltpu.SemaphoreType.DMA,
        ],
        name="sc_gather_rows",
    )
    def gather(table_hbm, idx_hbm, out_hbm, idx_v, buf0, buf1, sem0, sem1):
        wid = lax.axis_index("subcore") * cores + lax.axis_index("core")
        base = wid * per_worker
        pltpu.sync_copy(idx_hbm.at[pl.ds(base, per_worker)], idx_v)
        _sc_gather_loop(table_hbm, out_hbm, idx_v, base, chunks, buf0, buf1, sem0, sem1)

    return gather(table, idx)


def _experts_body(be_ref, nu_ref, xs_ref, wgu_ref, bgu_ref, wd_ref, bd_ref, ys_ref, wgu_bf, wd_bf):
    i = pl.program_id(0)
    bm = MOE_BM
    used = i < nu_ref[0]
    prev = be_ref[jnp.maximum(i - 1, 0)]
    fresh = (i == 0) | (be_ref[i] != prev)

    @pl.when(used & fresh)
    def _():
        wgu_bf[...] = wgu_ref[...].astype(BF16)
        wd_bf[...] = wd_ref[...].astype(BF16)

    @pl.when(used)
    def _():
        x = jnp.concatenate([xs_ref[pl.ds(s, bm, stride=SUBLANES), :] for s in range(ROW_TILES)], axis=1).astype(BF16)
        gu = jnp.dot(x, wgu_bf[...], preferred_element_type=F32) + bgu_ref[...]
        gate = jnp.minimum(gu[:, 0:D_FF], SWIGLU_LIMIT)
        up = jnp.clip(gu[:, D_FF:2 * D_FF], -SWIGLU_LIMIT, SWIGLU_LIMIT)
        act = ((up + 1.0) * gate * jax.nn.sigmoid(SWIGLU_ALPHA * gate)).astype(BF16)
        y = jnp.dot(act, wd_bf[...], preferred_element_type=F32) + bd_ref[...]
        for s in range(ROW_TILES):
            ys_ref[pl.ds(s, bm, stride=SUBLANES), :] = y[:, s * LANES:(s + 1) * LANES]

    @pl.when(jnp.logical_not(used))
    def _():
        ys_ref[...] = jnp.zeros_like(ys_ref)


def _experts(block_expert, n_used, xs, w_gate_up, b_gate_up, w_down, b_down):
    bm = MOE_BM
    d = D_MODEL

    def blk(i, be, nu):
        return jnp.minimum(i, nu[0] - 1)

    grid_spec = pltpu.PrefetchScalarGridSpec(
        num_scalar_prefetch=2,
        grid=(MOE_NB,),
        in_specs=[
            pl.BlockSpec((bm * SUBLANES, LANES), lambda i, be, nu: (blk(i, be, nu), 0)),
            pl.BlockSpec((None, d, 2 * D_FF), lambda i, be, nu: (be[blk(i, be, nu)], 0, 0)),
            pl.BlockSpec((None, 1, 2 * D_FF), lambda i, be, nu: (be[blk(i, be, nu)], 0, 0)),
            pl.BlockSpec((None, D_FF, d), lambda i, be, nu: (be[blk(i, be, nu)], 0, 0)),
            pl.BlockSpec((None, 1, d), lambda i, be, nu: (be[blk(i, be, nu)], 0, 0)),
        ],
        out_specs=pl.BlockSpec((bm * SUBLANES, LANES), lambda i, be, nu: (i, 0)),
        scratch_shapes=[pltpu.VMEM((d, 2 * D_FF), BF16), pltpu.VMEM((D_FF, d), BF16)],
    )
    return pl.pallas_call(
        _experts_body,
        grid_spec=grid_spec,
        out_shape=jax.ShapeDtypeStruct((MOE_ROWS * SUBLANES, LANES), F32),
        compiler_params=_params(("arbitrary",), 56),
        name="experts",
    )(block_expert, n_used, xs, w_gate_up, b_gate_up, w_down, b_down)


def _combine_body(gates_ref, x1_ref, mod_ref, gpost_ref, y0_ref, y1_ref, y2_ref, y3_ref, o_ref):
    tm = TM_COMB
    gates = gates_ref[...]
    y = jnp.zeros((tm, D_MODEL), F32)
    for k, yk_ref in enumerate((y0_ref, y1_ref, y2_ref, y3_ref)):
        yk = jnp.concatenate([yk_ref[pl.ds(s, tm, stride=SUBLANES), :] for s in range(ROW_TILES)], axis=1)
        y = y + yk * gates[:, k:k + 1]
    gate2 = mod_ref[5:6, :]
    o_ref[...] = x1_ref[...] + gate2 * _rms(y, gpost_ref[...])


def _combine(gates_t, x1, mod, g_post, y4):
    t, d = x1.shape
    tm = TM_COMB
    tiles = t // tm
    tiles_per_seq = SEQ // tm

    def slab(k):
        return pl.BlockSpec((tm * SUBLANES, LANES), lambda i: (k * tiles + i, 0))

    return pl.pallas_call(
        _combine_body,
        grid=(tiles,),
        in_specs=[
            pl.BlockSpec((tm, TOP_K), lambda i: (i, 0)),
            pl.BlockSpec((tm, d), lambda i: (i, 0)),
            pl.BlockSpec((None, 6, d), lambda i: (i // tiles_per_seq, 0, 0)),
            pl.BlockSpec(g_post.shape, lambda i: (0, 0)),
            slab(0), slab(1), slab(2), slab(3),
        ],
        out_specs=pl.BlockSpec((tm, d), lambda i: (i, 0)),
        out_shape=jax.ShapeDtypeStruct((t, d), F32),
        compiler_params=_params(("arbitrary",), 48),
        name="combine",
    )(gates_t, x1, mod, g_post, y4, y4, y4, y4)


def _rotary_tables():
    half = ROT_DIM // 2
    inv_freq = ROPE_THETA ** (-2.0 * jnp.arange(half, dtype=F32) / ROT_DIM)
    ang = jnp.arange(SEQ).astype(F32)[:, None] * inv_freq[None, :]
    cos, sin = jnp.cos(ang), jnp.sin(ang)
    ones = jnp.ones((SEQ, ATT_HEAD_DIM - ROT_DIM), F32)
    zeros = jnp.zeros((SEQ, ATT_HEAD_DIM - ROT_DIM), F32)
    zh = jnp.zeros((SEQ, half), F32)
    reps = LANES // ATT_HEAD_DIM
    rc = jnp.tile(jnp.concatenate([cos, cos, ones], axis=1), (1, reps))
    rm = jnp.tile(jnp.concatenate([-sin, zh, zeros], axis=1), (1, reps))
    rp = jnp.tile(jnp.concatenate([zh, sin, zeros], axis=1), (1, reps))
    return rc, rm, rp


def kernel(x, c, w_ada, b_ada, g_pre_mix, g_post_mix, w_in, w_gk_fwd, b_gk_fwd, w_gk_bwd, b_gk_bwd, g_gla_out,
           attn_sink, w_out, g_pre_ffn, g_post_ffn, w_router, b_router, w_gate_up, b_gate_up, w_down, b_down):
    assert x.shape == (BATCH, SEQ, D_MODEL) and w_ada.shape[0] == 1
    d = D_MODEL
    x2 = x.reshape(TOKENS, d)

    c_pad = jnp.pad(c, ((0, SUBLANES - BATCH), (0, 0)))
    mod = _ada(c_pad, w_ada[0], b_ada)[:BATCH].reshape(BATCH, 6, d)

    hk = GLA_HEADS * GLA_DK
    hv = GLA_HEADS * GLA_DV
    w = w_in[0]
    o_lr = 2 * hk + 2 * hv
    o_aq = o_lr + 2 * GLA_RANK
    o_ak = o_aq + ATT_Q_HEADS * ATT_HEAD_DIM
    o_av = o_ak + ATT_KV_HEADS * ATT_HEAD_DIM
    hd = ATT_HEAD_DIM
    wa = w[:, :o_lr].astype(BF16)
    wlr = w[:, o_lr:o_aq].astype(BF16)
    dup = lambda m: jnp.concatenate([m[:, g * hd:(g + 1) * hd] for g in range(ATT_KV_HEADS) for _ in range(2)], axis=1)
    wb = jnp.concatenate([w[:, o_aq:o_ak], dup(w[:, o_ak:o_av]), dup(w[:, o_av:o_av + ATT_KV_HEADS * hd])], axis=1).astype(BF16)
    zr = jnp.zeros((GLA_RANK, hk), F32)
    wgk = jnp.concatenate([jnp.concatenate([w_gk_fwd[0], zr], axis=1),
                           jnp.concatenate([zr, w_gk_bwd[0]], axis=1)], axis=0).astype(BF16)
    bgk = jnp.concatenate([b_gk_fwd[0], b_gk_bwd[0]])[None, :]
    rc, rm, rp = _rotary_tables()

    q, k, v, gg, laf, lab, aq, ak2, av2 = _inproj(x2, mod, g_pre_mix, wa, wlr, wgk, bgk, wb, rc, rm, rp)
    o_f, o_b = _gla(q, k, v, laf, lab)
    o_att = _attn(attn_sink[0], aq, ak2, av2)

    wr_t = w_router[0].T
    wrh = wr_t.astype(BF16)
    wrl = (wr_t - wrh.astype(F32)).astype(BF16)
    x1, h2_tiles, top_i, gates, rank, counts = _post(
        o_f, o_b, gg, o_att, x2, mod, g_gla_out, g_post_mix, g_pre_ffn, w_out[0].astype(BF16), wrh, wrl,
        b_router[0][:, None])

    pos, block_expert, meta = _route(top_i, rank, counts)
    pos_flat = pos.reshape(TOP_K * TOKENS)
    meta_flat = meta[:, 0]
    xs = _sc_dispatch_rows(h2_tiles.reshape(TOKENS, SUBLANES, LANES), pos_flat, MOE_ROWS)
    xs = xs.reshape(MOE_ROWS * SUBLANES, LANES)
    ys = _experts(block_expert[0, :MOE_NB], meta_flat[2 * N_EXPERTS:2 * N_EXPERTS + 1], xs,
                  w_gate_up[0], b_gate_up[0][:, None, :], w_down[0], b_down[0][:, None, :])
    y4 = _sc_gather_rows(ys.reshape(MOE_ROWS, SUBLANES, LANES), pos_flat)
    out = _combine(gates.T, x1, mod, g_post_ffn, y4.reshape(TOP_K * TOKENS * SUBLANES, LANES))
    return out.reshape(BATCH, SEQ, d)
```

```python
import functools

import jax
import jax.numpy as jnp
import numpy as np
from jax import lax
from jax.experimental import pallas as pl
from jax.experimental.pallas import tpu as pltpu
from jax.experimental.pallas import tpu_sc as plsc

F32 = jnp.float32
BF16 = jnp.bfloat16
I32 = jnp.int32

D_MODEL = 1024
BATCH = 2
SEQ = 8192
TOKENS = BATCH * SEQ
GLA_HEADS = 4
GLA_DV = 128
GLA_DK = 64
GLA_RANK = 16
GLA_GATE_NORMALIZER = 16.0
GLA_CHUNK = 64
ATT_Q_HEADS = 8
ATT_KV_HEADS = 2
ATT_HEAD_DIM = 64
ATT_WINDOW = 128
ATT_BLOCK = 128
ROT_DIM = 16
ROPE_THETA = 500000.0
N_EXPERTS = 32
TOP_K = 4
D_FF = 1024
SWIGLU_LIMIT = 7.0
SWIGLU_ALPHA = 1.702
NORM_EPS = 1e-6
NEG_INF = -1e30

LANES = 128
SUBLANES = 8
PACK_COLS = D_MODEL // 2
PACK_SUB = PACK_COLS // LANES

TM_IN = 512
GLA_GROUP = 4
TM_POST = 256
MOE_BM = 256
MOE_ROWS = TOKENS * TOP_K + N_EXPERTS * MOE_BM
MOE_NB = MOE_ROWS // MOE_BM
MOE_NB_PAD = ((MOE_NB + LANES - 1) // LANES) * LANES
SC_SCAN_CHUNK = 4096
TM_COMB = 256
SC_GATHER_WINDOW = 64

NT_DIMS = (((1,), (1,)), ((), ()))
TN_DIMS = (((0,), (0,)), ((), ()))


def _params(semantics, vmem_mib):
    return pltpu.CompilerParams(dimension_semantics=semantics, vmem_limit_bytes=vmem_mib * 1024 * 1024)


def _rms(x, g):
    return x * lax.rsqrt(jnp.mean(x * x, axis=-1, keepdims=True) + NORM_EPS) * g


def _silu(x):
    return x * jax.nn.sigmoid(x)


def _pack_rows(ref, v):
    m = v.shape[0]
    bits = lax.bitcast_convert_type(v, jnp.uint32)
    word = lax.bitcast_convert_type(bits[:, :PACK_COLS] | (bits[:, PACK_COLS:] >> 16), I32)
    for s in range(PACK_SUB):
        ref[pl.ds(s, m, stride=PACK_SUB), :] = word[:, s * LANES:(s + 1) * LANES]


def _unpack_rows(ref, m):
    word = jnp.concatenate([ref[pl.ds(s, m, stride=PACK_SUB), :] for s in range(PACK_SUB)], axis=1)
    bits = lax.bitcast_convert_type(word, jnp.uint32)
    hi = lax.bitcast_convert_type(bits & jnp.uint32(0xFFFF0000), F32)
    lo = lax.bitcast_convert_type(bits << 16, F32)
    return hi, lo


def _ada_body(c_ref, w_ref, b_ref, o_ref):
    ca = _silu(c_ref[...]).astype(BF16)
    o_ref[...] = jnp.dot(ca, w_ref[...].astype(BF16), preferred_element_type=F32) + b_ref[...]


def _ada(c_pad, w_ada, b_ada):
    d = D_MODEL
    return pl.pallas_call(
        _ada_body,
        grid=(6,),
        in_specs=[
            pl.BlockSpec((SUBLANES, d), lambda j: (0, 0)),
            pl.BlockSpec((d, d), lambda j: (0, j)),
            pl.BlockSpec((1, d), lambda j: (0, j)),
        ],
        out_specs=pl.BlockSpec((SUBLANES, d), lambda j: (0, j)),
        out_shape=jax.ShapeDtypeStruct((SUBLANES, 6 * d), F32),
        compiler_params=_params(("arbitrary",), 32),
        name="ada",
    )(c_pad, w_ada, b_ada)


def _rotary(x, cos_t, msin_t, psin_t):
    width = x.shape[1]
    reps = width // LANES
    c = jnp.concatenate([cos_t] * reps, axis=1)
    m = jnp.concatenate([msin_t] * reps, axis=1)
    p = jnp.concatenate([psin_t] * reps, axis=1)
    half = ROT_DIM // 2
    return x * c + pltpu.roll(x, width - half, 1) * m + pltpu.roll(x, half, 1) * p


def _inproj_body(x_ref, mod_ref, g_ref, wa_ref, wlr_ref, wgk_ref, bgk_ref, wb_ref, rc_ref, rm_ref, rp_ref,
                 q_ref, k_ref, v_ref, gg_ref, laf_ref, lab_ref, aq_ref, ak_ref, av_ref):
    shift = mod_ref[0:1, :]
    scale = mod_ref[1:2, :]
    h = (_rms(x_ref[...], g_ref[...]) * (1.0 + scale) + shift).astype(BF16)

    hk = GLA_HEADS * GLA_DK
    hv = GLA_HEADS * GLA_DV
    pa = jnp.dot(h, wa_ref[...], preferred_element_type=F32)
    q_ref[...] = pa[:, 0:hk] * (GLA_DK ** -0.5)
    k_ref[...] = pa[:, hk:2 * hk]
    v_ref[...] = pa[:, 2 * hk:2 * hk + hv].astype(BF16)
    gg_ref[...] = pa[:, 2 * hk + hv:2 * hk + 2 * hv]

    plr = jnp.dot(h, wlr_ref[...], preferred_element_type=F32)
    gk = jnp.dot(plr.astype(BF16), wgk_ref[...], preferred_element_type=F32) + bgk_ref[...]
    la = (jnp.minimum(gk, 0.0) - jnp.log1p(jnp.exp(-jnp.abs(gk)))) * (1.0 / GLA_GATE_NORMALIZER)
    laf_ref[...] = la[:, 0:hk]
    lab_ref[...] = la[:, hk:2 * hk]

    pb = jnp.dot(h, wb_ref[...], preferred_element_type=F32)
    nq = ATT_Q_HEADS * ATT_HEAD_DIM
    nk = 2 * ATT_KV_HEADS * ATT_HEAD_DIM
    rc, rm, rp = rc_ref[...], rm_ref[...], rp_ref[...]
    aq_ref[...] = (_rotary(pb[:, 0:nq], rc, rm, rp) * (ATT_HEAD_DIM ** -0.5)).astype(BF16)
    ak_ref[...] = _rotary(pb[:, nq:nq + nk], rc, rm, rp).astype(BF16)
    av_ref[...] = pb[:, nq + nk:nq + 2 * nk].astype(BF16)


def _inproj(x2, mod, g_pre, wa, wlr, wgk, bgk, wb, rc, rm, rp):
    t, d = x2.shape
    tm = TM_IN
    tiles_per_seq = SEQ // tm
    hk = GLA_HEADS * GLA_DK
    hv = GLA_HEADS * GLA_DV
    nq = ATT_Q_HEADS * ATT_HEAD_DIM
    nk = 2 * ATT_KV_HEADS * ATT_HEAD_DIM

    def full(a):
        return pl.BlockSpec(a.shape, lambda i: (0,) * a.ndim)

    def rows(w):
        return pl.BlockSpec((tm, w), lambda i: (i, 0))

    def table():
        return pl.BlockSpec((tm, LANES), lambda i: (i % tiles_per_seq, 0))

    out_widths = [(hk, F32), (hk, F32), (hv, BF16), (hv, F32), (hk, F32), (hk, F32), (nq, BF16), (nk, BF16), (nk, BF16)]
    return pl.pallas_call(
        _inproj_body,
        grid=(t // tm,),
        in_specs=[
            rows(d),
            pl.BlockSpec((None, 6, d), lambda i: (i // tiles_per_seq, 0, 0)),
            full(g_pre), full(wa), full(wlr), full(wgk), full(bgk), full(wb),
            table(), table(), table(),
        ],
        out_specs=[rows(w) for w, _ in out_widths],
        out_shape=[jax.ShapeDtypeStruct((t, w), dt) for w, dt in out_widths],
        compiler_params=_params(("arbitrary",), 56),
        name="inproj",
    )(x2, mod, g_pre, wa, wlr, wgk, bgk, wb, rc, rm, rp)


def _gla_chunk(q, k, la, v_ref, rows, o_ref, states, cum, tri, i_last, i_mid, head_masks):
    hi = la.astype(BF16)
    lo = (la - hi.astype(F32)).astype(BF16)
    b = jnp.dot(cum, hi, preferred_element_type=F32) + jnp.dot(cum, lo, preferred_element_type=F32)
    b_last = b[i_last:i_last + 1, :]
    b_mid = b[i_mid:i_mid + 1, :]
    qs = q * jnp.exp(b - b_mid)
    ks = k * jnp.exp(b_mid - b)
    qi = q * jnp.exp(b)
    kst = k * jnp.exp(b_last - b)
    decay = jnp.exp(b_last)
    new_states = []
    for h in range(GLA_HEADS):
        pair = slice((h // 2) * LANES, (h // 2 + 1) * LANES)
        mask = head_masks[h % 2]
        vcols = slice(h * GLA_DV, (h + 1) * GLA_DV)
        qs_h = jnp.where(mask, qs[:, pair], 0.0).astype(BF16)
        sc = lax.dot_general(qs_h, ks[:, pair].astype(BF16), NT_DIMS, preferred_element_type=F32)
        sc = jnp.where(tri, sc, 0.0)
        v_h = v_ref[rows, vcols]
        st = states[h]
        qi_h = jnp.where(mask, qi[:, pair], 0.0).astype(BF16)
        o = jnp.dot(sc.astype(BF16), v_h, preferred_element_type=F32)
        o = o + lax.dot_general(qi_h, st.astype(BF16), NT_DIMS, preferred_element_type=F32)
        o_ref[rows, vcols] = o
        kv = lax.dot_general(v_h, kst[:, pair].astype(BF16), TN_DIMS, preferred_element_type=F32)
        new_states.append(st * decay[:, pair] + kv)
    return new_states


def _gla_body(qf_ref, kf_ref, vf_ref, laf_ref, qb_ref, kb_ref, vb_ref, lab_ref, of_ref, ob_ref, sf_ref, sb_ref):
    @pl.when(pl.program_id(1) == 0)
    def _():
        sf_ref[...] = jnp.zeros_like(sf_ref)
        sb_ref[...] = jnp.zeros_like(sb_ref)

    c = GLA_CHUNK
    r_i = lax.broadcasted_iota(I32, (c, c), 0)
    c_i = lax.broadcasted_iota(I32, (c, c), 1)
    lower = c_i <= r_i
    upper = c_i >= r_i
    cum_f = jnp.where(lower, 1.0, 0.0).astype(BF16)
    cum_b = jnp.where(upper, 1.0, 0.0).astype(BF16)
    lane = lax.broadcasted_iota(I32, (1, LANES), 1)
    head_masks = (lane < GLA_DK, lane >= GLA_DK)

    st_f = [sf_ref[h] for h in range(GLA_HEADS)]
    st_b = [sb_ref[h] for h in range(GLA_HEADS)]
    for g in range(GLA_GROUP):
        rows_f = slice(g * c, (g + 1) * c)
        gb = GLA_GROUP - 1 - g
        rows_b = slice(gb * c, (gb + 1) * c)
        st_f = _gla_chunk(qf_ref[rows_f, :], kf_ref[rows_f, :], laf_ref[rows_f, :], vf_ref, rows_f, of_ref,
                          st_f, cum_f, lower, c - 1, c // 2 - 1, head_masks)
        st_b = _gla_chunk(qb_ref[rows_b, :], kb_ref[rows_b, :], lab_ref[rows_b, :], vb_ref, rows_b, ob_ref,
                          st_b, cum_b, upper, 0, c // 2, head_masks)
    for h in range(GLA_HEADS):
        sf_ref[h] = st_f[h]
        sb_ref[h] = st_b[h]


def _gla(q, k, v, laf, lab):
    t = q.shape[0]
    rows = GLA_GROUP * GLA_CHUNK
    ng = SEQ // rows
    hk = GLA_HEADS * GLA_DK
    hv = GLA_HEADS * GLA_DV

    def fwd(w):
        return pl.BlockSpec((rows, w), lambda b, n: (b * ng + n, 0))

    def bwd(w):
        return pl.BlockSpec((rows, w), lambda b, n: (b * ng + ng - 1 - n, 0))

    return pl.pallas_call(
        _gla_body,
        grid=(BATCH, ng),
        in_specs=[fwd(hk), fwd(hk), fwd(hv), fwd(hk), bwd(hk), bwd(hk), bwd(hv), bwd(hk)],
        out_specs=[fwd(hv), bwd(hv)],
        out_shape=[jax.ShapeDtypeStruct((t, hv), F32)] * 2,
        scratch_shapes=[pltpu.VMEM((GLA_HEADS, GLA_DV, 2 * GLA_DK), F32)] * 2,
        compiler_params=_params(("arbitrary", "arbitrary"), 32),
        name="gla",
    )(q, k, v, laf, q, k, v, lab)


def _attn_body(sink_ref, q_ref, kp_ref, kc_ref, kn_ref, vp_ref, vc_ref, vn_ref, o_ref):
    n = pl.program_id(1)
    nb = pl.num_programs(1)
    qb = ATT_BLOCK
    k_all = jnp.concatenate([kp_ref[...], kc_ref[...], kn_ref[...]], axis=0)
    v_all = jnp.concatenate([vp_ref[...], vc_ref[...], vn_ref[...]], axis=0)
    lane = lax.broadcasted_iota(I32, (1, LANES), 1)
    lo = lane < ATT_HEAD_DIM
    i_q = lax.broadcasted_iota(I32, (qb, 3 * qb), 0)
    j_k = lax.broadcasted_iota(I32, (qb, 3 * qb), 1)
    rel = j_k - qb - i_q
    valid = (jnp.abs(rel) <= ATT_WINDOW) & ((j_k >= qb) | (n > 0)) & ((j_k < 2 * qb) | (n < nb - 1))
    valid4 = jnp.concatenate([valid] * 4, axis=0)
    for g in range(ATT_KV_HEADS):
        kg = k_all[:, g * LANES:(g + 1) * LANES]
        vg = v_all[:, g * LANES:(g + 1) * LANES]
        qa = q_ref[:, (2 * g) * LANES:(2 * g + 1) * LANES]
        qc = q_ref[:, (2 * g + 1) * LANES:(2 * g + 2) * LANES]
        zero = jnp.zeros_like(qa)
        lhs = jnp.concatenate([jnp.where(lo, qa, zero), jnp.where(lo, zero, qa),
                               jnp.where(lo, qc, zero), jnp.where(lo, zero, qc)], axis=0)
        s = lax.dot_general(lhs, kg, NT_DIMS, preferred_element_type=F32)
        s = jnp.where(valid4, s, NEG_INF)
        sink = jnp.concatenate([jnp.full((qb, 1), sink_ref[4 * g + r], F32) for r in range(4)], axis=0)
        m = jnp.maximum(jnp.max(s, axis=-1, keepdims=True), sink)
        p = jnp.exp(s - m)
        denom = jnp.sum(p, axis=-1, keepdims=True) + jnp.exp(sink - m)
        o = jnp.dot(p.astype(BF16), vg, preferred_element_type=F32) / denom
        o_ref[:, (2 * g) * LANES:(2 * g + 1) * LANES] = jnp.where(lo, o[0:qb], o[qb:2 * qb]).astype(o_ref.dtype)
        o_ref[:, (2 * g + 1) * LANES:(2 * g + 2) * LANES] = jnp.where(lo, o[2 * qb:3 * qb], o[3 * qb:4 * qb]).astype(o_ref.dtype)


def _attn(sink, aq, ak2, av2):
    t = aq.shape[0]
    qb = ATT_BLOCK
    nb = SEQ // qb
    nq = ATT_Q_HEADS * ATT_HEAD_DIM
    nk = 2 * ATT_KV_HEADS * ATT_HEAD_DIM

    def kv(shift):
        return pl.BlockSpec((qb, nk), lambda b, n: (b * nb + jnp.clip(n + shift, 0, nb - 1), 0))

    return pl.pallas_call(
        _attn_body,
        grid=(BATCH, nb),
        in_specs=[
            pl.BlockSpec(memory_space=pltpu.SMEM),
            pl.BlockSpec((qb, nq), lambda b, n: (b * nb + n, 0)),
            kv(-1), kv(0), kv(1), kv(-1), kv(0), kv(1),
        ],
        out_specs=pl.BlockSpec((qb, nq), lambda b, n: (b * nb + n, 0)),
        out_shape=jax.ShapeDtypeStruct((t, nq), BF16),
        compiler_params=_params(("arbitrary", "arbitrary"), 32),
        name="attn",
    )(sink, aq, ak2, ak2, ak2, av2, av2, av2)


def _post_body(of_ref, ob_ref, gg_ref, oa_ref, x_ref, mod_ref, ggla_ref, gpm_ref, gpf_ref, wout_ref,
               wrh_ref, wrl_ref, br_ref,
               x1_ref, h2_ref, ti_ref, gt_ref, rk_ref, cnt_ref, base_ref):
    tm = TM_POST

    @pl.when(pl.program_id(0) == 0)
    def _():
        base_ref[...] = jnp.zeros_like(base_ref)

    og = of_ref[...] + ob_ref[...]
    gg = gg_ref[...]
    parts = []
    for h in range(GLA_HEADS):
        cols = slice(h * GLA_DV, (h + 1) * GLA_DV)
        parts.append((_rms(og[:, cols], ggla_ref[...]) * _silu(gg[:, cols])).astype(BF16))
    o = jnp.concatenate(parts + [oa_ref[...]], axis=1)
    y = jnp.dot(o, wout_ref[...], preferred_element_type=F32)

    gate1 = mod_ref[2:3, :]
    shift2 = mod_ref[3:4, :]
    scale2 = mod_ref[4:5, :]
    x1 = x_ref[...] + gate1 * _rms(y, gpm_ref[...])
    x1_ref[...] = x1
    h2 = _rms(x1, gpf_ref[...]) * (1.0 + scale2) + shift2
    h2_hi = h2.astype(BF16)
    h2_hi32 = h2_hi.astype(F32)
    h2_lo = (h2 - h2_hi32).astype(BF16)
    _pack_rows(h2_ref, h2_hi32)

    wrh = wrh_ref[...]
    logits = (lax.dot_general(wrh, h2_hi, NT_DIMS, preferred_element_type=F32)
              + lax.dot_general(wrh, h2_lo, NT_DIMS, preferred_element_type=F32)
              + lax.dot_general(wrl_ref[...], h2_hi, NT_DIMS, preferred_element_type=F32)
              + br_ref[...])
    e_iota = lax.broadcasted_iota(I32, (N_EXPERTS, tm), 0)
    idxs, vals = [], []
    work = logits
    for _ in range(TOP_K):
        m = jnp.max(work, axis=0, keepdims=True)
        idx = jnp.min(jnp.where(work == m, e_iota, N_EXPERTS), axis=0, keepdims=True)
        idxs.append(idx)
        vals.append(m)
        work = jnp.where(e_iota == idx, -jnp.inf, work)
    exps = [jnp.exp(v - vals[0]) for v in vals]
    inv = 1.0 / (exps[0] + exps[1] + exps[2] + exps[3])
    gt_ref[...] = jnp.concatenate([e * inv for e in exps], axis=0)
    ti_ref[...] = jnp.concatenate(idxs, axis=0)

    onehots = [e_iota == idx for idx in idxs]
    member = jnp.where(onehots[0] | onehots[1] | onehots[2] | onehots[3], 1.0, 0.0)
    t_row = lax.broadcasted_iota(I32, (tm, tm), 0)
    t_col = lax.broadcasted_iota(I32, (tm, tm), 1)
    strict = jnp.where(t_row < t_col, 1.0, 0.0).astype(BF16)
    before = base_ref[...] + jnp.dot(member.astype(BF16), strict, preferred_element_type=F32)
    rk_ref[...] = jnp.concatenate(
        [jnp.sum(jnp.where(oh, before, 0.0), axis=0, keepdims=True) for oh in onehots], axis=0).astype(I32)
    new_base = base_ref[...] + jnp.sum(member, axis=1, keepdims=True)
    base_ref[...] = new_base
    cnt_ref[...] = jnp.broadcast_to(new_base, cnt_ref.shape)


def _post(o_f, o_b, gg, o_att, x2, mod, g_gla, g_pm, g_pf, wout, wrh, wrl, br):
    t, d = x2.shape
    tm = TM_POST
    tiles_per_seq = SEQ // tm
    hv = GLA_HEADS * GLA_DV

    def full(a):
        return pl.BlockSpec(a.shape, lambda i: (0,) * a.ndim)

    def rows(w):
        return pl.BlockSpec((tm, w), lambda i: (i, 0))

    def lanes():
        return pl.BlockSpec((TOP_K, tm), lambda i: (0, i))

    return pl.pallas_call(
        _post_body,
        grid=(t // tm,),
        in_specs=[
            rows(hv), rows(hv), rows(hv), rows(hv), rows(d),
            pl.BlockSpec((None, 6, d), lambda i: (i // tiles_per_seq, 0, 0)),
            full(g_gla), full(g_pm), full(g_pf), full(wout), full(wrh), full(wrl), full(br),
        ],
        out_specs=[
            rows(d),
            pl.BlockSpec((tm * PACK_SUB, LANES), lambda i: (i, 0)),
            lanes(), lanes(), lanes(),
            pl.BlockSpec((N_EXPERTS, LANES), lambda i: (0, 0)),
        ],
        out_shape=[
            jax.ShapeDtypeStruct((t, d), F32),
            jax.ShapeDtypeStruct((t * PACK_SUB, LANES), I32),
            jax.ShapeDtypeStruct((TOP_K, t), I32),
            jax.ShapeDtypeStruct((TOP_K, t), F32),
            jax.ShapeDtypeStruct((TOP_K, t), I32),
            jax.ShapeDtypeStruct((N_EXPERTS, LANES), F32),
        ],
        scratch_shapes=[pltpu.VMEM((N_EXPERTS, 1), F32)],
        compiler_params=_params(("arbitrary",), 48),
        name="post",
    )(o_f, o_b, gg, o_att, x2, mod, g_gla, g_pm, g_pf, wout, wrh, wrl, br)


def _route_body(ti_ref, rk_ref, cnt_ref, pos_ref, be_ref, meta_ref):
    cnt = cnt_ref[...]
    padded = jnp.floor((cnt + (MOE_BM - 1)) * (1.0 / MOE_BM)) * MOE_BM
    starts, ends = [], []
    acc = jnp.zeros((1, LANES), F32)
    for e in range(N_EXPERTS):
        starts.append(acc)
        acc = acc + padded[e:e + 1, :]
        ends.append(acc)
    ti = ti_ref[...]
    off = jnp.zeros(ti.shape, F32)
    for e in range(N_EXPERTS):
        off = jnp.where(ti == e, starts[e][:, 0:1], off)
    pos_ref[...] = rk_ref[...] + off.astype(I32)

    block_start = lax.broadcasted_iota(I32, (1, MOE_NB_PAD), 1).astype(F32) * MOE_BM
    owner = jnp.zeros((1, MOE_NB_PAD), I32)
    for e in range(N_EXPERTS):
        owner = owner + jnp.where(ends[e][:, 0:1] <= block_start, 1, 0)
    be_ref[...] = jnp.minimum(owner, N_EXPERTS - 1)
    meta_ref[...] = jnp.concatenate(
        [starts[e] + cnt[e:e + 1, :] for e in range(N_EXPERTS)] + ends + [acc * (1.0 / MOE_BM)]
        + [jnp.zeros((SUBLANES - 1, LANES), F32)], axis=0).astype(I32)


def _route(top_i, rank, counts):
    return pl.pallas_call(
        _route_body,
        out_shape=[
            jax.ShapeDtypeStruct(top_i.shape, I32),
            jax.ShapeDtypeStruct((1, MOE_NB_PAD), I32),
            jax.ShapeDtypeStruct((2 * N_EXPERTS + SUBLANES, LANES), I32),
        ],
        compiler_params=pltpu.CompilerParams(vmem_limit_bytes=32 * 1024 * 1024),
        name="route",
    )(top_i, rank, counts)


def _sc_workers():
    info = plsc.get_sparse_core_info()
    return info.num_cores, info.num_subcores, info.num_lanes


def _sc_gather_loop(table_hbm, out_hbm, idx_v, base, chunks, buf0, buf1, sem0, sem1):
    window = SC_GATHER_WINDOW

    def fetch(c, buf, sem):
        return pltpu.make_async_copy(table_hbm.at[idx_v.at[pl.ds(c * window, window)]], buf, sem)

    def flush(c, buf):
        pltpu.sync_copy(buf, out_hbm.at[pl.ds(base + c * window, window)])

    fetch(0, buf0, sem0).start()

    @pl.loop(0, chunks, step=2)
    def _(c):
        fetch(c + 1, buf1, sem1).start()
        fetch(c, buf0, sem0).wait()
        flush(c, buf0)

        @pl.when(c + 2 < chunks)
        def _():
            fetch(c + 2, buf0, sem0).start()

        fetch(c + 1, buf1, sem1).wait()
        flush(c + 1, buf1)


def _sc_dispatch_rows(table, pos_flat, n_rows):
    cores, subcores, lanes = _sc_workers()
    workers = cores * subcores
    window = SC_GATHER_WINDOW
    per_worker = n_rows // workers
    chunks = per_worker // window
    n_assign = pos_flat.shape[0]
    scan = SC_SCAN_CHUNK
    assert per_worker * workers == n_rows and chunks * window == per_worker and chunks % 2 == 0
    assert n_assign % scan == 0 and scan % lanes == 0 and per_worker % lanes == 0
    row_shape = table.shape[1:]
    mesh = plsc.VectorSubcoreMesh(core_axis_name="core", subcore_axis_name="subcore")

    @functools.partial(
        pl.kernel,
        out_type=jax.ShapeDtypeStruct((n_rows,) + row_shape, table.dtype),
        mesh=mesh,
        scratch_types=[
            pltpu.VMEM((per_worker,), I32),
            pltpu.VMEM((scan,), I32),
            pltpu.VMEM((window,) + row_shape, table.dtype),
            pltpu.VMEM((window,) + row_shape, table.dtype),
            pltpu.SemaphoreType.DMA,
            pltpu.SemaphoreType.DMA,
        ],
        compiler_params=pltpu.CompilerParams(needs_layout_passes=False),
        name="sc_dispatch_rows",
    )
    def dispatch(table_hbm, pos_hbm, out_hbm, src_v, pos_v, buf0, buf1, sem0, sem1):
        wid = lax.axis_index("subcore") * cores + lax.axis_index("core")
        base = wid * per_worker
        lane = lax.iota(I32, lanes)

        @pl.loop(0, per_worker, step=lanes)
        def _(j):
            src_v[pl.ds(j, lanes)] = (base + j + lane) & (TOKENS - 1)

        @pl.loop(0, n_assign, step=scan)
        def _(a0):
            pltpu.sync_copy(pos_hbm.at[pl.ds(a0, scan)], pos_v)

            @pl.loop(0, scan, step=lanes)
            def _(j):
                rel = pos_v[pl.ds(j, lanes)] - base
                mine = (rel >= 0) & (rel < per_worker)
                tok = (a0 + j + lane) & (TOKENS - 1)
                plsc.store_scatter(src_v, [jnp.where(mine, rel, 0)], tok, mask=mine)

        _sc_gather_loop(table_hbm, out_hbm, src_v, base, chunks, buf0, buf1, sem0, sem1)

    return dispatch(table, pos_flat)


def _sc_gather_rows(table, idx):
    cores, subcores, _ = _sc_workers()
    workers = cores * subcores
    n = idx.shape[0]
    window = SC_GATHER_WINDOW
    per_worker = n // workers
    chunks = per_worker // window
    assert per_worker * workers == n and chunks * window == per_worker and chunks % 2 == 0
    row_shape = table.shape[1:]
    mesh = plsc.VectorSubcoreMesh(core_axis_name="core", subcore_axis_name="subcore")

    @functools.partial(
        pl.kernel,
        out_type=jax.ShapeDtypeStruct((n,) + row_shape, table.dtype),
        mesh=mesh,
        scratch_types=[
            pltpu.VMEM((per_worker,), I32),
            pltpu.VMEM((window,) + row_shape, table.dtype),
            pltpu.VMEM((window,) + row_shape, table.dtype),
            pltpu.SemaphoreType.DMA,
            pltpu.SemaphoreType.DMA,
        ],
        name="sc_gather_rows",
    )
    def gather(table_hbm, idx_hbm, out_hbm, idx_v, buf0, buf1, sem0, sem1):
        wid = lax.axis_index("subcore") * cores + lax.axis_index("core")
        base = wid * per_worker
        pltpu.sync_copy(idx_hbm.at[pl.ds(base, per_worker)], idx_v)
        _sc_gather_loop(table_hbm, out_hbm, idx_v, base, chunks, buf0, buf1, sem0, sem1)

    return gather(table, idx)


def _experts_body(be_ref, nu_ref, xs_ref, wgu_ref, bgu_ref, wd_ref, bd_ref, ys_ref, wgu_bf, wd_bf):
    i = pl.program_id(0)
    bm = MOE_BM
    used = i < nu_ref[0]
    prev = be_ref[jnp.maximum(i - 1, 0)]
    fresh = (i == 0) | (be_ref[i] != prev)

    @pl.when(used & fresh)
    def _():
        wgu_bf[...] = wgu_ref[...].astype(BF16)
        wd_bf[...] = wd_ref[...].astype(BF16)

    @pl.when(used)
    def _():
        x = jnp.concatenate(_unpack_rows(xs_ref, bm), axis=1).astype(BF16)
        gu = jnp.dot(x, wgu_bf[...], preferred_element_type=F32) + bgu_ref[...]
        gate = jnp.minimum(gu[:, 0:D_FF], SWIGLU_LIMIT)
        up = jnp.clip(gu[:, D_FF:2 * D_FF], -SWIGLU_LIMIT, SWIGLU_LIMIT)
        act = ((up + 1.0) * gate * jax.nn.sigmoid(SWIGLU_ALPHA * gate)).astype(BF16)
        y = jnp.dot(act, wd_bf[...], preferred_element_type=F32) + bd_ref[...]
        _pack_rows(ys_ref, y.astype(BF16).astype(F32))

    @pl.when(jnp.logical_not(used))
    def _():
        ys_ref[...] = jnp.zeros_like(ys_ref)


def _experts(block_expert, n_used, xs, w_gate_up, b_gate_up, w_down, b_down):
    bm = MOE_BM
    d = D_MODEL

    def blk(i, be, nu):
        return jnp.minimum(i, nu[0] - 1)

    grid_spec = pltpu.PrefetchScalarGridSpec(
        num_scalar_prefetch=2,
        grid=(MOE_NB,),
        in_specs=[
            pl.BlockSpec((bm * PACK_SUB, LANES), lambda i, be, nu: (blk(i, be, nu), 0)),
            pl.BlockSpec((None, d, 2 * D_FF), lambda i, be, nu: (be[blk(i, be, nu)], 0, 0)),
            pl.BlockSpec((None, 1, 2 * D_FF), lambda i, be, nu: (be[blk(i, be, nu)], 0, 0)),
            pl.BlockSpec((None, D_FF, d), lambda i, be, nu: (be[blk(i, be, nu)], 0, 0)),
            pl.BlockSpec((None, 1, d), lambda i, be, nu: (be[blk(i, be, nu)], 0, 0)),
        ],
        out_specs=pl.BlockSpec((bm * PACK_SUB, LANES), lambda i, be, nu: (i, 0)),
        scratch_shapes=[pltpu.VMEM((d, 2 * D_FF), BF16), pltpu.VMEM((D_FF, d), BF16)],
    )
    return pl.pallas_call(
        _experts_body,
        grid_spec=grid_spec,
        out_shape=jax.ShapeDtypeStruct((MOE_ROWS * PACK_SUB, LANES), I32),
        compiler_params=_params(("arbitrary",), 56),
        name="experts",
    )(block_expert, n_used, xs, w_gate_up, b_gate_up, w_down, b_down)


def _combine_body(gates_ref, x1_ref, mod_ref, gpost_ref, y0_ref, y1_ref, y2_ref, y3_ref, o_ref):
    tm = TM_COMB
    gates = gates_ref[...]
    y_hi = jnp.zeros((tm, PACK_COLS), F32)
    y_lo = jnp.zeros((tm, PACK_COLS), F32)
    for k, yk_ref in enumerate((y0_ref, y1_ref, y2_ref, y3_ref)):
        hi, lo = _unpack_rows(yk_ref, tm)
        y_hi = y_hi + hi * gates[:, k:k + 1]
        y_lo = y_lo + lo * gates[:, k:k + 1]
    y = jnp.concatenate([y_hi, y_lo], axis=1)
    gate2 = mod_ref[5:6, :]
    o_ref[...] = x1_ref[...] + gate2 * _rms(y, gpost_ref[...])


def _combine(gates_t, x1, mod, g_post, y4):
    t, d = x1.shape
    tm = TM_COMB
    tiles = t // tm
    tiles_per_seq = SEQ // tm

    def slab(k):
        return pl.BlockSpec((tm * PACK_SUB, LANES), lambda i: (k * tiles + i, 0))

    return pl.pallas_call(
        _combine_body,
        grid=(tiles,),
        in_specs=[
            pl.BlockSpec((tm, TOP_K), lambda i: (i, 0)),
            pl.BlockSpec((tm, d), lambda i: (i, 0)),
            pl.BlockSpec((None, 6, d), lambda i: (i // tiles_per_seq, 0, 0)),
            pl.BlockSpec(g_post.shape, lambda i: (0, 0)),
            slab(0), slab(1), slab(2), slab(3),
        ],
        out_specs=pl.BlockSpec((tm, d), lambda i: (i, 0)),
        out_shape=jax.ShapeDtypeStruct((t, d), F32),
        compiler_params=_params(("arbitrary",), 48),
        name="combine",
    )(gates_t, x1, mod, g_post, y4, y4, y4, y4)


def _rotary_tables():
    half = ROT_DIM // 2
    inv_freq = ROPE_THETA ** (-2.0 * jnp.arange(half, dtype=F32) / ROT_DIM)
    ang = jnp.arange(SEQ).astype(F32)[:, None] * inv_freq[None, :]
    cos, sin = jnp.cos(ang), jnp.sin(ang)
    ones = jnp.ones((SEQ, ATT_HEAD_DIM - ROT_DIM), F32)
    zeros = jnp.zeros((SEQ, ATT_HEAD_DIM - ROT_DIM), F32)
    zh = jnp.zeros((SEQ, half), F32)
    reps = LANES // ATT_HEAD_DIM
    rc = jnp.tile(jnp.concatenate([cos, cos, ones], axis=1), (1, reps))
    rm = jnp.tile(jnp.concatenate([-sin, zh, zeros], axis=1), (1, reps))
    rp = jnp.tile(jnp.concatenate([zh, sin, zeros], axis=1), (1, reps))
    return rc, rm, rp


def kernel(x, c, w_ada, b_ada, g_pre_mix, g_post_mix, w_in, w_gk_fwd, b_gk_fwd, w_gk_bwd, b_gk_bwd, g_gla_out,
           attn_sink, w_out, g_pre_ffn, g_post_ffn, w_router, b_router, w_gate_up, b_gate_up, w_down, b_down):
    assert x.shape == (BATCH, SEQ, D_MODEL) and w_ada.shape[0] == 1
    d = D_MODEL
    x2 = x.reshape(TOKENS, d)

    c_pad = jnp.pad(c, ((0, SUBLANES - BATCH), (0, 0)))
    mod = _ada(c_pad, w_ada[0], b_ada)[:BATCH].reshape(BATCH, 6, d)

    hk = GLA_HEADS * GLA_DK
    hv = GLA_HEADS * GLA_DV
    w = w_in[0]
    o_lr = 2 * hk + 2 * hv
    o_aq = o_lr + 2 * GLA_RANK
    o_ak = o_aq + ATT_Q_HEADS * ATT_HEAD_DIM
    o_av = o_ak + ATT_KV_HEADS * ATT_HEAD_DIM
    hd = ATT_HEAD_DIM
    wa = w[:, :o_lr].astype(BF16)
    wlr = w[:, o_lr:o_aq].astype(BF16)
    dup = lambda m: jnp.concatenate([m[:, g * hd:(g + 1) * hd] for g in range(ATT_KV_HEADS) for _ in range(2)], axis=1)
    wb = jnp.concatenate([w[:, o_aq:o_ak], dup(w[:, o_ak:o_av]), dup(w[:, o_av:o_av + ATT_KV_HEADS * hd])], axis=1).astype(BF16)
    zr = jnp.zeros((GLA_RANK, hk), F32)
    wgk = jnp.concatenate([jnp.concatenate([w_gk_fwd[0], zr], axis=1),
                           jnp.concatenate([zr, w_gk_bwd[0]], axis=1)], axis=0).astype(BF16)
    bgk = jnp.concatenate([b_gk_fwd[0], b_gk_bwd[0]])[None, :]
    rc, rm, rp = _rotary_tables()

    q, k, v, gg, laf, lab, aq, ak2, av2 = _inproj(x2, mod, g_pre_mix, wa, wlr, wgk, bgk, wb, rc, rm, rp)
    o_f, o_b = _gla(q, k, v, laf, lab)
    o_att = _attn(attn_sink[0], aq, ak2, av2)

    wr_t = w_router[0].T
    wrh = wr_t.astype(BF16)
    wrl = (wr_t - wrh.astype(F32)).astype(BF16)
    x1, h2_tiles, top_i, gates, rank, counts = _post(
        o_f, o_b, gg, o_att, x2, mod, g_gla_out, g_post_mix, g_pre_ffn, w_out[0].astype(BF16), wrh, wrl,
        b_router[0][:, None])

    pos, block_expert, meta = _route(top_i, rank, counts)
    pos_flat = pos.reshape(TOP_K * TOKENS)
    meta_flat = meta[:, 0]
    xs = _sc_dispatch_rows(h2_tiles.reshape(TOKENS, PACK_SUB, LANES), pos_flat, MOE_ROWS)
    xs = xs.reshape(MOE_ROWS * PACK_SUB, LANES)
    ys = _experts(block_expert[0, :MOE_NB], meta_flat[2 * N_EXPERTS:2 * N_EXPERTS + 1], xs,
                  w_gate_up[0], b_gate_up[0][:, None, :], w_down[0], b_down[0][:, None, :])
    y4 = _sc_gather_rows(ys.reshape(MOE_ROWS, PACK_SUB, LANES), pos_flat)
    out = _combine(gates.T, x1, mod, g_post_ffn, y4.reshape(TOP_K * TOKENS * PACK_SUB, LANES))
    return out.reshape(BATCH, SEQ, d)
```

```python
import functools

import jax
import jax.numpy as jnp
import numpy as np
from jax import lax
from jax.experimental import pallas as pl
from jax.experimental.pallas import tpu as pltpu
from jax.experimental.pallas import tpu_sc as plsc

F32 = jnp.float32
BF16 = jnp.bfloat16
I32 = jnp.int32

D_MODEL = 1024
BATCH = 2
SEQ = 8192
TOKENS = BATCH * SEQ
GLA_HEADS = 4
GLA_DV = 128
GLA_DK = 64
GLA_RANK = 16
GLA_GATE_NORMALIZER = 16.0
GLA_CHUNK = 64
ATT_Q_HEADS = 8
ATT_KV_HEADS = 2
ATT_HEAD_DIM = 64
ATT_WINDOW = 128
ATT_BLOCK = 128
ROT_DIM = 16
ROPE_THETA = 500000.0
N_EXPERTS = 32
TOP_K = 4
D_FF = 1024
SWIGLU_LIMIT = 7.0
SWIGLU_ALPHA = 1.702
NORM_EPS = 1e-6
NEG_INF = -1e30

LANES = 128
SUBLANES = 8
PACK_COLS = D_MODEL // 2
PACK_SUB = PACK_COLS // LANES

TM_IN = 512
GLA_GROUP = 4
ATT_GROUP = 4
TM_POST = 256
MOE_BM = 256
MOE_ROWS = TOKENS * TOP_K + N_EXPERTS * MOE_BM
MOE_NB = MOE_ROWS // MOE_BM
MOE_NB_PAD = ((MOE_NB + LANES - 1) // LANES) * LANES
SC_SCAN_CHUNK = 4096
TM_COMB = 256
SC_GATHER_WINDOW = 64

NT_DIMS = (((1,), (1,)), ((), ()))
TN_DIMS = (((0,), (0,)), ((), ()))


def _params(semantics, vmem_mib):
    return pltpu.CompilerParams(dimension_semantics=semantics, vmem_limit_bytes=vmem_mib * 1024 * 1024)


def _rms(x, g):
    return x * lax.rsqrt(jnp.mean(x * x, axis=-1, keepdims=True) + NORM_EPS) * g


def _silu(x):
    return x * jax.nn.sigmoid(x)


def _pack_rows(ref, v):
    m = v.shape[0]
    bits = lax.bitcast_convert_type(v, jnp.uint32)
    word = lax.bitcast_convert_type(bits[:, :PACK_COLS] | (bits[:, PACK_COLS:] >> 16), I32)
    for s in range(PACK_SUB):
        ref[pl.ds(s, m, stride=PACK_SUB), :] = word[:, s * LANES:(s + 1) * LANES]


def _unpack_rows(ref, m):
    word = jnp.concatenate([ref[pl.ds(s, m, stride=PACK_SUB), :] for s in range(PACK_SUB)], axis=1)
    bits = lax.bitcast_convert_type(word, jnp.uint32)
    hi = lax.bitcast_convert_type(bits & jnp.uint32(0xFFFF0000), F32)
    lo = lax.bitcast_convert_type(bits << 16, F32)
    return hi, lo


def _ada_body(c_ref, w_ref, b_ref, o_ref):
    ca = _silu(c_ref[...]).astype(BF16)
    o_ref[...] = jnp.dot(ca, w_ref[...].astype(BF16), preferred_element_type=F32) + b_ref[...]


def _ada(c_pad, w_ada, b_ada):
    d = D_MODEL
    return pl.pallas_call(
        _ada_body,
        grid=(6,),
        in_specs=[
            pl.BlockSpec((SUBLANES, d), lambda j: (0, 0)),
            pl.BlockSpec((d, d), lambda j: (0, j)),
            pl.BlockSpec((1, d), lambda j: (0, j)),
        ],
        out_specs=pl.BlockSpec((SUBLANES, d), lambda j: (0, j)),
        out_shape=jax.ShapeDtypeStruct((SUBLANES, 6 * d), F32),
        compiler_params=_params(("arbitrary",), 32),
        name="ada",
    )(c_pad, w_ada, b_ada)


def _rotary(x, cos_t, msin_t, psin_t):
    width = x.shape[1]
    reps = width // LANES
    c = jnp.concatenate([cos_t] * reps, axis=1)
    m = jnp.concatenate([msin_t] * reps, axis=1)
    p = jnp.concatenate([psin_t] * reps, axis=1)
    half = ROT_DIM // 2
    return x * c + pltpu.roll(x, width - half, 1) * m + pltpu.roll(x, half, 1) * p


def _inproj_body(x_ref, mod_ref, g_ref, wa_ref, wlr_ref, wgk_ref, bgk_ref, wb_ref, wvt_ref, rc_ref, rm_ref, rp_ref,
                 q_ref, k_ref, v_ref, gg_ref, laf_ref, lab_ref, aq_ref, ak_ref, avt_ref):
    shift = mod_ref[0:1, :]
    scale = mod_ref[1:2, :]
    h = (_rms(x_ref[...], g_ref[...]) * (1.0 + scale) + shift).astype(BF16)

    hk = GLA_HEADS * GLA_DK
    hv = GLA_HEADS * GLA_DV
    pa = jnp.dot(h, wa_ref[...], preferred_element_type=F32)
    q_ref[...] = pa[:, 0:hk] * (GLA_DK ** -0.5)
    k_ref[...] = pa[:, hk:2 * hk]
    v_ref[...] = pa[:, 2 * hk:2 * hk + hv].astype(BF16)
    gg_ref[...] = pa[:, 2 * hk + hv:2 * hk + 2 * hv]

    plr = jnp.dot(h, wlr_ref[...], preferred_element_type=F32)
    gk = jnp.dot(plr.astype(BF16), wgk_ref[...], preferred_element_type=F32) + bgk_ref[...]
    la = (jnp.minimum(gk, 0.0) - jnp.log1p(jnp.exp(-jnp.abs(gk)))) * (1.0 / GLA_GATE_NORMALIZER)
    laf_ref[...] = la[:, 0:hk]
    lab_ref[...] = la[:, hk:2 * hk]

    pb = jnp.dot(h, wb_ref[...], preferred_element_type=F32)
    nq = ATT_Q_HEADS * ATT_HEAD_DIM
    nk = 2 * ATT_KV_HEADS * ATT_HEAD_DIM
    rc, rm, rp = rc_ref[...], rm_ref[...], rp_ref[...]
    aq_ref[...] = (_rotary(pb[:, 0:nq], rc, rm, rp) * (ATT_HEAD_DIM ** -0.5)).astype(BF16)
    ak_ref[...] = _rotary(pb[:, nq:nq + nk], rc, rm, rp).astype(BF16)
    avt_ref[...] = lax.dot_general(wvt_ref[...], h, NT_DIMS, preferred_element_type=F32).astype(BF16)


def _inproj(x2, mod, g_pre, wa, wlr, wgk, bgk, wb, wvt, rc, rm, rp):
    t, d = x2.shape
    tm = TM_IN
    tiles_per_seq = SEQ // tm
    hk = GLA_HEADS * GLA_DK
    hv = GLA_HEADS * GLA_DV
    nq = ATT_Q_HEADS * ATT_HEAD_DIM
    nk = 2 * ATT_KV_HEADS * ATT_HEAD_DIM

    def full(a):
        return pl.BlockSpec(a.shape, lambda i: (0,) * a.ndim)

    def rows(w):
        return pl.BlockSpec((tm, w), lambda i: (i, 0))

    def table():
        return pl.BlockSpec((tm, LANES), lambda i: (i % tiles_per_seq, 0))

    out_widths = [(hk, F32), (hk, F32), (hv, BF16), (hv, F32), (hk, F32), (hk, F32), (nq, BF16), (nk, BF16)]
    return pl.pallas_call(
        _inproj_body,
        grid=(t // tm,),
        in_specs=[
            rows(d),
            pl.BlockSpec((None, 6, d), lambda i: (i // tiles_per_seq, 0, 0)),
            full(g_pre), full(wa), full(wlr), full(wgk), full(bgk), full(wb), full(wvt),
            table(), table(), table(),
        ],
        out_specs=[rows(w) for w, _ in out_widths] + [pl.BlockSpec((nk, tm), lambda i: (0, i))],
        out_shape=[jax.ShapeDtypeStruct((t, w), dt) for w, dt in out_widths] + [jax.ShapeDtypeStruct((nk, t), BF16)],
        compiler_params=_params(("arbitrary",), 56),
        name="inproj",
    )(x2, mod, g_pre, wa, wlr, wgk, bgk, wb, wvt, rc, rm, rp)


def _gla_chunk(q, k, la, v_ref, rows, o_ref, states, cum, tri, i_last, i_mid, head_masks):
    hi = la.astype(BF16)
    lo = (la - hi.astype(F32)).astype(BF16)
    b = jnp.dot(cum, hi, preferred_element_type=F32) + jnp.dot(cum, lo, preferred_element_type=F32)
    b_last = b[i_last:i_last + 1, :]
    b_mid = b[i_mid:i_mid + 1, :]
    qs = q * jnp.exp(b - b_mid)
    ks = k * jnp.exp(b_mid - b)
    qi = q * jnp.exp(b)
    kst = k * jnp.exp(b_last - b)
    decay = jnp.exp(b_last)
    new_states = []
    for h in range(GLA_HEADS):
        pair = slice((h // 2) * LANES, (h // 2 + 1) * LANES)
        mask = head_masks[h % 2]
        vcols = slice(h * GLA_DV, (h + 1) * GLA_DV)
        qs_h = jnp.where(mask, qs[:, pair], 0.0).astype(BF16)
        sc = lax.dot_general(qs_h, ks[:, pair].astype(BF16), NT_DIMS, preferred_element_type=F32)
        sc = jnp.where(tri, sc, 0.0)
        v_h = v_ref[rows, vcols]
        st = states[h]
        qi_h = jnp.where(mask, qi[:, pair], 0.0).astype(BF16)
        o = jnp.dot(sc.astype(BF16), v_h, preferred_element_type=F32)
        o = o + lax.dot_general(qi_h, st.astype(BF16), NT_DIMS, preferred_element_type=F32)
        o_ref[rows, vcols] = o
        kv = lax.dot_general(v_h, kst[:, pair].astype(BF16), TN_DIMS, preferred_element_type=F32)
        new_states.append(st * decay[:, pair] + kv)
    return new_states


def _gla_body(qf_ref, kf_ref, vf_ref, laf_ref, qb_ref, kb_ref, vb_ref, lab_ref, of_ref, ob_ref, sf_ref, sb_ref):
    @pl.when(pl.program_id(1) == 0)
    def _():
        sf_ref[...] = jnp.zeros_like(sf_ref)
        sb_ref[...] = jnp.zeros_like(sb_ref)

    c = GLA_CHUNK
    r_i = lax.broadcasted_iota(I32, (c, c), 0)
    c_i = lax.broadcasted_iota(I32, (c, c), 1)
    lower = c_i <= r_i
    upper = c_i >= r_i
    cum_f = jnp.where(lower, 1.0, 0.0).astype(BF16)
    cum_b = jnp.where(upper, 1.0, 0.0).astype(BF16)
    lane = lax.broadcasted_iota(I32, (1, LANES), 1)
    head_masks = (lane < GLA_DK, lane >= GLA_DK)

    st_f = [sf_ref[h] for h in range(GLA_HEADS)]
    st_b = [sb_ref[h] for h in range(GLA_HEADS)]
    for g in range(GLA_GROUP):
        rows_f = slice(g * c, (g + 1) * c)
        gb = GLA_GROUP - 1 - g
        rows_b = slice(gb * c, (gb + 1) * c)
        st_f = _gla_chunk(qf_ref[rows_f, :], kf_ref[rows_f, :], laf_ref[rows_f, :], vf_ref, rows_f, of_ref,
                          st_f, cum_f, lower, c - 1, c // 2 - 1, head_masks)
        st_b = _gla_chunk(qb_ref[rows_b, :], kb_ref[rows_b, :], lab_ref[rows_b, :], vb_ref, rows_b, ob_ref,
                          st_b, cum_b, upper, 0, c // 2, head_masks)
    for h in range(GLA_HEADS):
        sf_ref[h] = st_f[h]
        sb_ref[h] = st_b[h]


def _gla(q, k, v, laf, lab):
    t = q.shape[0]
    rows = GLA_GROUP * GLA_CHUNK
    ng = SEQ // rows
    hk = GLA_HEADS * GLA_DK
    hv = GLA_HEADS * GLA_DV

    def fwd(w):
        return pl.BlockSpec((rows, w), lambda b, n: (b * ng + n, 0))

    def bwd(w):
        return pl.BlockSpec((rows, w), lambda b, n: (b * ng + ng - 1 - n, 0))

    return pl.pallas_call(
        _gla_body,
        grid=(BATCH, ng),
        in_specs=[fwd(hk), fwd(hk), fwd(hv), fwd(hk), bwd(hk), bwd(hk), bwd(hv), bwd(hk)],
        out_specs=[fwd(hv), bwd(hv)],
        out_shape=[jax.ShapeDtypeStruct((t, hv), F32)] * 2,
        scratch_shapes=[pltpu.VMEM((GLA_HEADS, GLA_DV, 2 * GLA_DK), F32)] * 2,
        compiler_params=_params(("arbitrary", "arbitrary"), 32),
        name="gla",
    )(q, k, v, laf, q, k, v, lab)


def _attn_body(sink_ref, q_ref, kp_ref, kc_ref, kn_ref, vp_ref, vc_ref, vn_ref, o_ref):
    step = pl.program_id(1)
    last = pl.num_programs(1) - 1
    qb = ATT_BLOCK
    hd = ATT_HEAD_DIM
    k_all = jnp.concatenate([kp_ref[...], kc_ref[...], kn_ref[...]], axis=0)
    vt_all = jnp.concatenate([vp_ref[...], vc_ref[...], vn_ref[...]], axis=1)
    lane = lax.broadcasted_iota(I32, (1, LANES), 1)
    lo = lane < hd
    j_k = lax.broadcasted_iota(I32, (3 * qb, qb), 0)
    i_q = lax.broadcasted_iota(I32, (3 * qb, qb), 1)
    band = jnp.abs(j_k - qb - i_q) <= ATT_WINDOW
    sinks = [jnp.concatenate([jnp.full((1, qb), sink_ref[4 * g + r], F32) for r in range(4)], axis=1)
             for g in range(ATT_KV_HEADS)]
    for j in range(ATT_GROUP):
        valid = band
        if j == 0:
            valid = valid & ((j_k >= qb) | (step > 0))
        if j == ATT_GROUP - 1:
            valid = valid & ((j_k < 2 * qb) | (step < last))
        valid4 = jnp.concatenate([valid] * 4, axis=1)
        keys = slice(j * qb, (j + 3) * qb)
        rows = slice(j * qb, (j + 1) * qb)
        for g in range(ATT_KV_HEADS):
            kg = k_all[keys, g * LANES:(g + 1) * LANES]
            vgt = vt_all[g * LANES:(g + 1) * LANES, keys]
            qa = q_ref[rows, (2 * g) * LANES:(2 * g + 1) * LANES]
            qc = q_ref[rows, (2 * g + 1) * LANES:(2 * g + 2) * LANES]
            zero = jnp.zeros_like(qa)
            lhs = jnp.concatenate([jnp.where(lo, qa, zero), jnp.where(lo, zero, qa),
                                   jnp.where(lo, qc, zero), jnp.where(lo, zero, qc)], axis=0)
            st = lax.dot_general(kg, lhs, NT_DIMS, preferred_element_type=F32)
            st = jnp.where(valid4, st, NEG_INF)
            sink = sinks[g]
            m = jnp.maximum(jnp.max(st, axis=0, keepdims=True), sink)
            p = jnp.exp(st - m)
            inv = 1.0 / (jnp.sum(p, axis=0, keepdims=True) + jnp.exp(sink - m))
            ot = jnp.dot(vgt, p.astype(BF16), preferred_element_type=F32) * inv
            pair_a = jnp.concatenate([ot[0:hd, 0:qb], ot[hd:2 * hd, qb:2 * qb]], axis=0)
            pair_c = jnp.concatenate([ot[0:hd, 2 * qb:3 * qb], ot[hd:2 * hd, 3 * qb:4 * qb]], axis=0)
            o_ref[rows, (2 * g) * LANES:(2 * g + 1) * LANES] = pair_a.T.astype(o_ref.dtype)
            o_ref[rows, (2 * g + 1) * LANES:(2 * g + 2) * LANES] = pair_c.T.astype(o_ref.dtype)


def _attn(sink, aq, ak2, avt):
    t = aq.shape[0]
    qb = ATT_BLOCK
    nb = SEQ // qb
    steps = nb // ATT_GROUP
    nq = ATT_Q_HEADS * ATT_HEAD_DIM
    nk = 2 * ATT_KV_HEADS * ATT_HEAD_DIM

    def edge_block(b, n, shift):
        return b * nb + jnp.clip(n * ATT_GROUP + shift, 0, nb - 1)

    def k_edge(shift):
        return pl.BlockSpec((qb, nk), lambda b, n: (edge_block(b, n, shift), 0))

    def v_edge(shift):
        return pl.BlockSpec((nk, qb), lambda b, n: (0, edge_block(b, n, shift)))

    def group(w):
        return pl.BlockSpec((ATT_GROUP * qb, w), lambda b, n: (b * steps + n, 0))

    v_group = pl.BlockSpec((nk, ATT_GROUP * qb), lambda b, n: (0, b * steps + n))
    return pl.pallas_call(
        _attn_body,
        grid=(BATCH, steps),
        in_specs=[
            pl.BlockSpec(memory_space=pltpu.SMEM),
            group(nq),
            k_edge(-1), group(nk), k_edge(ATT_GROUP), v_edge(-1), v_group, v_edge(ATT_GROUP),
        ],
        out_specs=group(nq),
        out_shape=jax.ShapeDtypeStruct((t, nq), BF16),
        compiler_params=_params(("arbitrary", "arbitrary"), 48),
        name="attn",
    )(sink, aq, ak2, ak2, ak2, avt, avt, avt)


def _post_body(of_ref, ob_ref, gg_ref, oa_ref, x_ref, mod_ref, ggla_ref, gpm_ref, gpf_ref, wout_ref,
               wrh_ref, wrl_ref, br_ref,
               x1_ref, h2_ref, ti_ref, gt_ref, rk_ref, cnt_ref, base_ref):
    tm = TM_POST

    @pl.when(pl.program_id(0) == 0)
    def _():
        base_ref[...] = jnp.zeros_like(base_ref)

    og = of_ref[...] + ob_ref[...]
    gg = gg_ref[...]
    parts = []
    for h in range(GLA_HEADS):
        cols = slice(h * GLA_DV, (h + 1) * GLA_DV)
        parts.append((_rms(og[:, cols], ggla_ref[...]) * _silu(gg[:, cols])).astype(BF16))
    o = jnp.concatenate(parts + [oa_ref[...]], axis=1)
    y = jnp.dot(o, wout_ref[...], preferred_element_type=F32)

    gate1 = mod_ref[2:3, :]
    shift2 = mod_ref[3:4, :]
    scale2 = mod_ref[4:5, :]
    x1 = x_ref[...] + gate1 * _rms(y, gpm_ref[...])
    x1_ref[...] = x1
    h2 = _rms(x1, gpf_ref[...]) * (1.0 + scale2) + shift2
    h2_hi = h2.astype(BF16)
    h2_hi32 = h2_hi.astype(F32)
    h2_lo = (h2 - h2_hi32).astype(BF16)
    _pack_rows(h2_ref, h2_hi32)

    wrh = wrh_ref[...]
    logits = (lax.dot_general(wrh, h2_hi, NT_DIMS, preferred_element_type=F32)
              + lax.dot_general(wrh, h2_lo, NT_DIMS, preferred_element_type=F32)
              + lax.dot_general(wrl_ref[...], h2_hi, NT_DIMS, preferred_element_type=F32)
              + br_ref[...])
    e_iota = lax.broadcasted_iota(I32, (N_EXPERTS, tm), 0)
    idxs, vals = [], []
    work = logits
    for _ in range(TOP_K):
        m = jnp.max(work, axis=0, keepdims=True)
        idx = jnp.min(jnp.where(work == m, e_iota, N_EXPERTS), axis=0, keepdims=True)
        idxs.append(idx)
        vals.append(m)
        work = jnp.where(e_iota == idx, -jnp.inf, work)
    exps = [jnp.exp(v - vals[0]) for v in vals]
    inv = 1.0 / (exps[0] + exps[1] + exps[2] + exps[3])
    gt_ref[...] = jnp.concatenate([e * inv for e in exps], axis=0)
    ti_ref[...] = jnp.concatenate(idxs, axis=0)

    onehots = [e_iota == idx for idx in idxs]
    member = jnp.where(onehots[0] | onehots[1] | onehots[2] | onehots[3], 1.0, 0.0)
    t_row = lax.broadcasted_iota(I32, (tm, tm), 0)
    t_col = lax.broadcasted_iota(I32, (tm, tm), 1)
    strict = jnp.where(t_row < t_col, 1.0, 0.0).astype(BF16)
    before = base_ref[...] + jnp.dot(member.astype(BF16), strict, preferred_element_type=F32)
    rk_ref[...] = jnp.concatenate(
        [jnp.sum(jnp.where(oh, before, 0.0), axis=0, keepdims=True) for oh in onehots], axis=0).astype(I32)
    new_base = base_ref[...] + jnp.sum(member, axis=1, keepdims=True)
    base_ref[...] = new_base
    cnt_ref[...] = jnp.broadcast_to(new_base, cnt_ref.shape)


def _post(o_f, o_b, gg, o_att, x2, mod, g_gla, g_pm, g_pf, wout, wrh, wrl, br):
    t, d = x2.shape
    tm = TM_POST
    tiles_per_seq = SEQ // tm
    hv = GLA_HEADS * GLA_DV

    def full(a):
        return pl.BlockSpec(a.shape, lambda i: (0,) * a.ndim)

    def rows(w):
        return pl.BlockSpec((tm, w), lambda i: (i, 0))

    def lanes():
        return pl.BlockSpec((TOP_K, tm), lambda i: (0, i))

    return pl.pallas_call(
        _post_body,
        grid=(t // tm,),
        in_specs=[
            rows(hv), rows(hv), rows(hv), rows(hv), rows(d),
            pl.BlockSpec((None, 6, d), lambda i: (i // tiles_per_seq, 0, 0)),
            full(g_gla), full(g_pm), full(g_pf), full(wout), full(wrh), full(wrl), full(br),
        ],
        out_specs=[
            rows(d),
            pl.BlockSpec((tm * PACK_SUB, LANES), lambda i: (i, 0)),
            lanes(), lanes(), lanes(),
            pl.BlockSpec((N_EXPERTS, LANES), lambda i: (0, 0)),
        ],
        out_shape=[
            jax.ShapeDtypeStruct((t, d), F32),
            jax.ShapeDtypeStruct((t * PACK_SUB, LANES), I32),
            jax.ShapeDtypeStruct((TOP_K, t), I32),
            jax.ShapeDtypeStruct((TOP_K, t), F32),
            jax.ShapeDtypeStruct((TOP_K, t), I32),
            jax.ShapeDtypeStruct((N_EXPERTS, LANES), F32),
        ],
        scratch_shapes=[pltpu.VMEM((N_EXPERTS, 1), F32)],
        compiler_params=_params(("arbitrary",), 48),
        name="post",
    )(o_f, o_b, gg, o_att, x2, mod, g_gla, g_pm, g_pf, wout, wrh, wrl, br)


def _route_body(ti_ref, rk_ref, cnt_ref, pos_ref, be_ref, meta_ref):
    cnt = cnt_ref[...]
    padded = jnp.floor((cnt + (MOE_BM - 1)) * (1.0 / MOE_BM)) * MOE_BM
    starts, ends = [], []
    acc = jnp.zeros((1, LANES), F32)
    for e in range(N_EXPERTS):
        starts.append(acc)
        acc = acc + padded[e:e + 1, :]
        ends.append(acc)
    ti = ti_ref[...]
    off = jnp.zeros(ti.shape, F32)
    for e in range(N_EXPERTS):
        off = jnp.where(ti == e, starts[e][:, 0:1], off)
    pos_ref[...] = rk_ref[...] + off.astype(I32)

    block_start = lax.broadcasted_iota(I32, (1, MOE_NB_PAD), 1).astype(F32) * MOE_BM
    owner = jnp.zeros((1, MOE_NB_PAD), I32)
    for e in range(N_EXPERTS):
        owner = owner + jnp.where(ends[e][:, 0:1] <= block_start, 1, 0)
    be_ref[...] = jnp.minimum(owner, N_EXPERTS - 1)
    meta_ref[...] = jnp.concatenate(
        [starts[e] + cnt[e:e + 1, :] for e in range(N_EXPERTS)] + ends + [acc * (1.0 / MOE_BM)]
        + [jnp.zeros((SUBLANES - 1, LANES), F32)], axis=0).astype(I32)


def _route(top_i, rank, counts):
    return pl.pallas_call(
        _route_body,
        out_shape=[
            jax.ShapeDtypeStruct(top_i.shape, I32),
            jax.ShapeDtypeStruct((1, MOE_NB_PAD), I32),
            jax.ShapeDtypeStruct((2 * N_EXPERTS + SUBLANES, LANES), I32),
        ],
        compiler_params=pltpu.CompilerParams(vmem_limit_bytes=32 * 1024 * 1024),
        name="route",
    )(top_i, rank, counts)


def _sc_workers():
    info = plsc.get_sparse_core_info()
    return info.num_cores, info.num_subcores, info.num_lanes


def _sc_gather_loop(table_hbm, out_hbm, idx_v, base, chunks, buf0, buf1, sem0, sem1):
    window = SC_GATHER_WINDOW

    def fetch(c, buf, sem):
        return pltpu.make_async_copy(table_hbm.at[idx_v.at[pl.ds(c * window, window)]], buf, sem)

    def flush(c, buf):
        pltpu.sync_copy(buf, out_hbm.at[pl.ds(base + c * window, window)])

    fetch(0, buf0, sem0).start()

    @pl.loop(0, chunks, step=2)
    def _(c):
        fetch(c + 1, buf1, sem1).start()
        fetch(c, buf0, sem0).wait()
        flush(c, buf0)

        @pl.when(c + 2 < chunks)
        def _():
            fetch(c + 2, buf0, sem0).start()

        fetch(c + 1, buf1, sem1).wait()
        flush(c + 1, buf1)


def _sc_dispatch_rows(table, pos_flat, n_rows):
    cores, subcores, lanes = _sc_workers()
    workers = cores * subcores
    window = SC_GATHER_WINDOW
    per_worker = n_rows // workers
    chunks = per_worker // window
    n_assign = pos_flat.shape[0]
    scan = SC_SCAN_CHUNK
    assert per_worker * workers == n_rows and chunks * window == per_worker and chunks % 2 == 0
    assert n_assign % scan == 0 and scan % lanes == 0 and per_worker % lanes == 0
    row_shape = table.shape[1:]
    mesh = plsc.VectorSubcoreMesh(core_axis_name="core", subcore_axis_name="subcore")

    @functools.partial(
        pl.kernel,
        out_type=jax.ShapeDtypeStruct((n_rows,) + row_shape, table.dtype),
        mesh=mesh,
        scratch_types=[
            pltpu.VMEM((per_worker,), I32),
            pltpu.VMEM((scan,), I32),
            pltpu.VMEM((window,) + row_shape, table.dtype),
            pltpu.VMEM((window,) + row_shape, table.dtype),
            pltpu.SemaphoreType.DMA,
            pltpu.SemaphoreType.DMA,
        ],
        compiler_params=pltpu.CompilerParams(needs_layout_passes=False),
        name="sc_dispatch_rows",
    )
    def dispatch(table_hbm, pos_hbm, out_hbm, src_v, pos_v, buf0, buf1, sem0, sem1):
        wid = lax.axis_index("subcore") * cores + lax.axis_index("core")
        base = wid * per_worker
        lane = lax.iota(I32, lanes)

        @pl.loop(0, per_worker, step=lanes)
        def _(j):
            src_v[pl.ds(j, lanes)] = (base + j + lane) & (TOKENS - 1)

        @pl.loop(0, n_assign, step=scan)
        def _(a0):
            pltpu.sync_copy(pos_hbm.at[pl.ds(a0, scan)], pos_v)

            @pl.loop(0, scan, step=lanes)
            def _(j):
                rel = pos_v[pl.ds(j, lanes)] - base
                mine = (rel >= 0) & (rel < per_worker)
                tok = (a0 + j + lane) & (TOKENS - 1)
                plsc.store_scatter(src_v, [jnp.where(mine, rel, 0)], tok, mask=mine)

        _sc_gather_loop(table_hbm, out_hbm, src_v, base, chunks, buf0, buf1, sem0, sem1)

    return dispatch(table, pos_flat)


def _sc_gather_rows(table, idx):
    cores, subcores, _ = _sc_workers()
    workers = cores * subcores
    n = idx.shape[0]
    window = SC_GATHER_WINDOW
    per_worker = n // workers
    chunks = per_worker // window
    assert per_worker * workers == n and chunks * window == per_worker and chunks % 2 == 0
    row_shape = table.shape[1:]
    mesh = plsc.VectorSubcoreMesh(core_axis_name="core", subcore_axis_name="subcore")

    @functools.partial(
        pl.kernel,
        out_type=jax.ShapeDtypeStruct((n,) + row_shape, table.dtype),
        mesh=mesh,
        scratch_types=[
            pltpu.VMEM((per_worker,), I32),
            pltpu.VMEM((window,) + row_shape, table.dtype),
            pltpu.VMEM((window,) + row_shape, table.dtype),
            pltpu.SemaphoreType.DMA,
            pltpu.SemaphoreType.DMA,
        ],
        name="sc_gather_rows",
    )
    def gather(table_hbm, idx_hbm, out_hbm, idx_v, buf0, buf1, sem0, sem1):
        wid = lax.axis_index("subcore") * cores + lax.axis_index("core")
        base = wid * per_worker
        pltpu.sync_copy(idx_hbm.at[pl.ds(base, per_worker)], idx_v)
        _sc_gather_loop(table_hbm, out_hbm, idx_v, base, chunks, buf0, buf1, sem0, sem1)

    return gather(table, idx)


def _experts_body(be_ref, nu_ref, xs_ref, wgu_ref, bgu_ref, wd_ref, bd_ref, ys_ref, wgu_bf, wd_bf):
    i = pl.program_id(0)
    bm = MOE_BM
    used = i < nu_ref[0]
    prev = be_ref[jnp.maximum(i - 1, 0)]
    fresh = (i == 0) | (be_ref[i] != prev)

    @pl.when(used & fresh)
    def _():
        wgu_bf[...] = wgu_ref[...].astype(BF16)
        wd_bf[...] = wd_ref[...].astype(BF16)

    @pl.when(used)
    def _():
        x = jnp.concatenate(_unpack_rows(xs_ref, bm), axis=1).astype(BF16)
        gu = jnp.dot(x, wgu_bf[...], preferred_element_type=F32) + bgu_ref[...]
        gate = jnp.minimum(gu[:, 0:D_FF], SWIGLU_LIMIT)
        up = jnp.clip(gu[:, D_FF:2 * D_FF], -SWIGLU_LIMIT, SWIGLU_LIMIT)
        act = ((up + 1.0) * gate * jax.nn.sigmoid(SWIGLU_ALPHA * gate)).astype(BF16)
        y = jnp.dot(act, wd_bf[...], preferred_element_type=F32) + bd_ref[...]
        _pack_rows(ys_ref, y.astype(BF16).astype(F32))

    @pl.when(jnp.logical_not(used))
    def _():
        ys_ref[...] = jnp.zeros_like(ys_ref)


def _experts(block_expert, n_used, xs, w_gate_up, b_gate_up, w_down, b_down):
    bm = MOE_BM
    d = D_MODEL

    def blk(i, be, nu):
        return jnp.minimum(i, nu[0] - 1)

    grid_spec = pltpu.PrefetchScalarGridSpec(
        num_scalar_prefetch=2,
        grid=(MOE_NB,),
        in_specs=[
            pl.BlockSpec((bm * PACK_SUB, LANES), lambda i, be, nu: (blk(i, be, nu), 0)),
            pl.BlockSpec((None, d, 2 * D_FF), lambda i, be, nu: (be[blk(i, be, nu)], 0, 0)),
            pl.BlockSpec((None, 1, 2 * D_FF), lambda i, be, nu: (be[blk(i, be, nu)], 0, 0)),
            pl.BlockSpec((None, D_FF, d), lambda i, be, nu: (be[blk(i, be, nu)], 0, 0)),
            pl.BlockSpec((None, 1, d), lambda i, be, nu: (be[blk(i, be, nu)], 0, 0)),
        ],
        out_specs=pl.BlockSpec((bm * PACK_SUB, LANES), lambda i, be, nu: (i, 0)),
        scratch_shapes=[pltpu.VMEM((d, 2 * D_FF), BF16), pltpu.VMEM((D_FF, d), BF16)],
    )
    return pl.pallas_call(
        _experts_body,
        grid_spec=grid_spec,
        out_shape=jax.ShapeDtypeStruct((MOE_ROWS * PACK_SUB, LANES), I32),
        compiler_params=_params(("arbitrary",), 56),
        name="experts",
    )(block_expert, n_used, xs, w_gate_up, b_gate_up, w_down, b_down)


def _combine_body(gates_ref, x1_ref, mod_ref, gpost_ref, y0_ref, y1_ref, y2_ref, y3_ref, o_ref):
    tm = TM_COMB
    gates = gates_ref[...]
    y_hi = jnp.zeros((tm, PACK_COLS), F32)
    y_lo = jnp.zeros((tm, PACK_COLS), F32)
    for k, yk_ref in enumerate((y0_ref, y1_ref, y2_ref, y3_ref)):
        hi, lo = _unpack_rows(yk_ref, tm)
        y_hi = y_hi + hi * gates[:, k:k + 1]
        y_lo = y_lo + lo * gates[:, k:k + 1]
    y = jnp.concatenate([y_hi, y_lo], axis=1)
    gate2 = mod_ref[5:6, :]
    o_ref[...] = x1_ref[...] + gate2 * _rms(y, gpost_ref[...])


def _combine(gates_t, x1, mod, g_post, y4):
    t, d = x1.shape
    tm = TM_COMB
    tiles = t // tm
    tiles_per_seq = SEQ // tm

    def slab(k):
        return pl.BlockSpec((tm * PACK_SUB, LANES), lambda i: (k * tiles + i, 0))

    return pl.pallas_call(
        _combine_body,
        grid=(tiles,),
        in_specs=[
            pl.BlockSpec((tm, TOP_K), lambda i: (i, 0)),
            pl.BlockSpec((tm, d), lambda i: (i, 0)),
            pl.BlockSpec((None, 6, d), lambda i: (i // tiles_per_seq, 0, 0)),
            pl.BlockSpec(g_post.shape, lambda i: (0, 0)),
            slab(0), slab(1), slab(2), slab(3),
        ],
        out_specs=pl.BlockSpec((tm, d), lambda i: (i, 0)),
        out_shape=jax.ShapeDtypeStruct((t, d), F32),
        compiler_params=_params(("arbitrary",), 48),
        name="combine",
    )(gates_t, x1, mod, g_post, y4, y4, y4, y4)


def _rotary_tables():
    half = ROT_DIM // 2
    inv_freq = ROPE_THETA ** (-2.0 * jnp.arange(half, dtype=F32) / ROT_DIM)
    ang = jnp.arange(SEQ).astype(F32)[:, None] * inv_freq[None, :]
    cos, sin = jnp.cos(ang), jnp.sin(ang)
    ones = jnp.ones((SEQ, ATT_HEAD_DIM - ROT_DIM), F32)
    zeros = jnp.zeros((SEQ, ATT_HEAD_DIM - ROT_DIM), F32)
    zh = jnp.zeros((SEQ, half), F32)
    reps = LANES // ATT_HEAD_DIM
    rc = jnp.tile(jnp.concatenate([cos, cos, ones], axis=1), (1, reps))
    rm = jnp.tile(jnp.concatenate([-sin, zh, zeros], axis=1), (1, reps))
    rp = jnp.tile(jnp.concatenate([zh, sin, zeros], axis=1), (1, reps))
    return rc, rm, rp


def _mixer_inputs(w_in, w_gk_fwd, b_gk_fwd, w_gk_bwd, b_gk_bwd):
    hk = GLA_HEADS * GLA_DK
    hv = GLA_HEADS * GLA_DV
    w = w_in[0]
    o_lr = 2 * hk + 2 * hv
    o_aq = o_lr + 2 * GLA_RANK
    o_ak = o_aq + ATT_Q_HEADS * ATT_HEAD_DIM
    o_av = o_ak + ATT_KV_HEADS * ATT_HEAD_DIM
    hd = ATT_HEAD_DIM
    wa = w[:, :o_lr].astype(BF16)
    wlr = w[:, o_lr:o_aq].astype(BF16)
    dup = lambda m: jnp.concatenate([m[:, g * hd:(g + 1) * hd] for g in range(ATT_KV_HEADS) for _ in range(2)], axis=1)
    wb = jnp.concatenate([w[:, o_aq:o_ak], dup(w[:, o_ak:o_av])], axis=1).astype(BF16)
    wvt = dup(w[:, o_av:o_av + ATT_KV_HEADS * hd]).T.astype(BF16)
    zr = jnp.zeros((GLA_RANK, hk), F32)
    wgk = jnp.concatenate([jnp.concatenate([w_gk_fwd[0], zr], axis=1),
                           jnp.concatenate([zr, w_gk_bwd[0]], axis=1)], axis=0).astype(BF16)
    bgk = jnp.concatenate([b_gk_fwd[0], b_gk_bwd[0]])[None, :]
    return (wa, wlr, wgk, bgk, wb, wvt) + _rotary_tables()


def kernel(x, c, w_ada, b_ada, g_pre_mix, g_post_mix, w_in, w_gk_fwd, b_gk_fwd, w_gk_bwd, b_gk_bwd, g_gla_out,
           attn_sink, w_out, g_pre_ffn, g_post_ffn, w_router, b_router, w_gate_up, b_gate_up, w_down, b_down):
    assert x.shape == (BATCH, SEQ, D_MODEL) and w_ada.shape[0] == 1
    d = D_MODEL
    x2 = x.reshape(TOKENS, d)

    c_pad = jnp.pad(c, ((0, SUBLANES - BATCH), (0, 0)))
    mod = _ada(c_pad, w_ada[0], b_ada)[:BATCH].reshape(BATCH, 6, d)

    mixer_in = _mixer_inputs(w_in, w_gk_fwd, b_gk_fwd, w_gk_bwd, b_gk_bwd)
    q, k, v, gg, laf, lab, aq, ak2, avt = _inproj(x2, mod, g_pre_mix, *mixer_in)
    o_f, o_b = _gla(q, k, v, laf, lab)
    o_att = _attn(attn_sink[0], aq, ak2, avt)

    wr_t = w_router[0].T
    wrh = wr_t.astype(BF16)
    wrl = (wr_t - wrh.astype(F32)).astype(BF16)
    x1, h2_tiles, top_i, gates, rank, counts = _post(
        o_f, o_b, gg, o_att, x2, mod, g_gla_out, g_post_mix, g_pre_ffn, w_out[0].astype(BF16), wrh, wrl,
        b_router[0][:, None])

    pos, block_expert, meta = _route(top_i, rank, counts)
    pos_flat = pos.reshape(TOP_K * TOKENS)
    meta_flat = meta[:, 0]
    xs = _sc_dispatch_rows(h2_tiles.reshape(TOKENS, PACK_SUB, LANES), pos_flat, MOE_ROWS)
    xs = xs.reshape(MOE_ROWS * PACK_SUB, LANES)
    ys = _experts(block_expert[0, :MOE_NB], meta_flat[2 * N_EXPERTS:2 * N_EXPERTS + 1], xs,
                  w_gate_up[0], b_gate_up[0][:, None, :], w_down[0], b_down[0][:, None, :])
    y4 = _sc_gather_rows(ys.reshape(MOE_ROWS, PACK_SUB, LANES), pos_flat)
    out = _combine(gates.T, x1, mod, g_post_ffn, y4.reshape(TOP_K * TOKENS * PACK_SUB, LANES))
    return out.reshape(BATCH, SEQ, d)
```

```python
import functools

import jax
import jax.numpy as jnp
import numpy as np
from jax import lax
from jax.experimental import pallas as pl
from jax.experimental.pallas import tpu as pltpu
from jax.experimental.pallas import tpu_sc as plsc

F32 = jnp.float32
BF16 = jnp.bfloat16
I32 = jnp.int32

D_MODEL = 1024
BATCH = 2
SEQ = 8192
TOKENS = BATCH * SEQ
GLA_HEADS = 4
GLA_DV = 128
GLA_DK = 64
GLA_RANK = 16
GLA_GATE_NORMALIZER = 16.0
GLA_CHUNK = 64
ATT_Q_HEADS = 8
ATT_KV_HEADS = 2
ATT_HEAD_DIM = 64
ATT_WINDOW = 128
ATT_BLOCK = 128
ROT_DIM = 16
ROPE_THETA = 500000.0
N_EXPERTS = 32
TOP_K = 4
D_FF = 1024
SWIGLU_LIMIT = 7.0
SWIGLU_ALPHA = 1.702
NORM_EPS = 1e-6
NEG_INF = -1e30

LANES = 128
SUBLANES = 8
PACK_COLS = D_MODEL // 2
PACK_SUB = PACK_COLS // LANES

TM_IN = 512
GLA_GROUP = 4
ATT_GROUP = 4
TM_POST = 256
MOE_BM = 256
MOE_ROWS = TOKENS * TOP_K + N_EXPERTS * MOE_BM
MOE_NB = MOE_ROWS // MOE_BM
MOE_NB_PAD = ((MOE_NB + LANES - 1) // LANES) * LANES
SC_SCAN_CHUNK = 4096
TM_COMB = 256
SC_GATHER_WINDOW = 64

NT_DIMS = (((1,), (1,)), ((), ()))
TN_DIMS = (((0,), (0,)), ((), ()))


def _params(semantics, vmem_mib):
    return pltpu.CompilerParams(dimension_semantics=semantics, vmem_limit_bytes=vmem_mib * 1024 * 1024)


def _rms(x, g):
    return x * lax.rsqrt(jnp.mean(x * x, axis=-1, keepdims=True) + NORM_EPS) * g


def _silu(x):
    return x * jax.nn.sigmoid(x)


def _pack_rows(ref, v):
    m = v.shape[0]
    bits = lax.bitcast_convert_type(v, jnp.uint32)
    word = lax.bitcast_convert_type(bits[:, :PACK_COLS] | (bits[:, PACK_COLS:] >> 16), I32)
    for s in range(PACK_SUB):
        ref[pl.ds(s, m, stride=PACK_SUB), :] = word[:, s * LANES:(s + 1) * LANES]


def _unpack_rows(ref, m):
    word = jnp.concatenate([ref[pl.ds(s, m, stride=PACK_SUB), :] for s in range(PACK_SUB)], axis=1)
    bits = lax.bitcast_convert_type(word, jnp.uint32)
    hi = lax.bitcast_convert_type(bits & jnp.uint32(0xFFFF0000), F32)
    lo = lax.bitcast_convert_type(bits << 16, F32)
    return hi, lo


def _ada_body(c_ref, w_ref, b_ref, o_ref):
    ca = _silu(c_ref[...]).astype(BF16)
    o_ref[...] = jnp.dot(ca, w_ref[...].astype(BF16), preferred_element_type=F32) + b_ref[...]


def _ada(c_pad, w_ada, b_ada):
    d = D_MODEL
    return pl.pallas_call(
        _ada_body,
        grid=(6,),
        in_specs=[
            pl.BlockSpec((SUBLANES, d), lambda j: (0, 0)),
            pl.BlockSpec((d, d), lambda j: (0, j)),
            pl.BlockSpec((1, d), lambda j: (0, j)),
        ],
        out_specs=pl.BlockSpec((SUBLANES, d), lambda j: (0, j)),
        out_shape=jax.ShapeDtypeStruct((SUBLANES, 6 * d), F32),
        compiler_params=_params(("arbitrary",), 32),
        name="ada",
    )(c_pad, w_ada, b_ada)


def _rotary(x, cos_t, msin_t, psin_t):
    width = x.shape[1]
    reps = width // LANES
    c = jnp.concatenate([cos_t] * reps, axis=1)
    m = jnp.concatenate([msin_t] * reps, axis=1)
    p = jnp.concatenate([psin_t] * reps, axis=1)
    half = ROT_DIM // 2
    return x * c + pltpu.roll(x, width - half, 1) * m + pltpu.roll(x, half, 1) * p


def _inproj_body(x_ref, mod_ref, g_ref, wa_ref, wlr_ref, wgk_ref, bgk_ref, wb_ref, wvt_ref, rc_ref, rm_ref, rp_ref,
                 q_ref, k_ref, v_ref, gg_ref, laf_ref, lab_ref, aq_ref, ak_ref, avt_ref):
    shift = mod_ref[0:1, :]
    scale = mod_ref[1:2, :]
    h = (_rms(x_ref[...], g_ref[...]) * (1.0 + scale) + shift).astype(BF16)

    hk = GLA_HEADS * GLA_DK
    hv = GLA_HEADS * GLA_DV
    pa = jnp.dot(h, wa_ref[...], preferred_element_type=F32)
    q_ref[...] = pa[:, 0:hk] * (GLA_DK ** -0.5)
    k_ref[...] = pa[:, hk:2 * hk]
    v_ref[...] = pa[:, 2 * hk:2 * hk + hv].astype(BF16)
    gg_ref[...] = pa[:, 2 * hk + hv:2 * hk + 2 * hv]

    plr = jnp.dot(h, wlr_ref[...], preferred_element_type=F32)
    gk = jnp.dot(plr.astype(BF16), wgk_ref[...], preferred_element_type=F32) + bgk_ref[...]
    la = (jnp.minimum(gk, 0.0) - jnp.log1p(jnp.exp(-jnp.abs(gk)))) * (1.0 / GLA_GATE_NORMALIZER)
    laf_ref[...] = la[:, 0:hk]
    lab_ref[...] = la[:, hk:2 * hk]

    pb = jnp.dot(h, wb_ref[...], preferred_element_type=F32)
    nq = ATT_Q_HEADS * ATT_HEAD_DIM
    nk = 2 * ATT_KV_HEADS * ATT_HEAD_DIM
    rc, rm, rp = rc_ref[...], rm_ref[...], rp_ref[...]
    aq_ref[...] = (_rotary(pb[:, 0:nq], rc, rm, rp) * (ATT_HEAD_DIM ** -0.5)).astype(BF16)
    ak_ref[...] = _rotary(pb[:, nq:nq + nk], rc, rm, rp).astype(BF16)
    avt_ref[...] = lax.dot_general(wvt_ref[...], h, NT_DIMS, preferred_element_type=F32).astype(BF16)


def _inproj(x2, mod, g_pre, wa, wlr, wgk, bgk, wb, wvt, rc, rm, rp):
    t, d = x2.shape
    tm = TM_IN
    tiles_per_seq = SEQ // tm
    hk = GLA_HEADS * GLA_DK
    hv = GLA_HEADS * GLA_DV
    nq = ATT_Q_HEADS * ATT_HEAD_DIM
    nk = 2 * ATT_KV_HEADS * ATT_HEAD_DIM

    def full(a):
        return pl.BlockSpec(a.shape, lambda i: (0,) * a.ndim)

    def rows(w):
        return pl.BlockSpec((tm, w), lambda i: (i, 0))

    def table():
        return pl.BlockSpec((tm, LANES), lambda i: (i % tiles_per_seq, 0))

    out_widths = [(hk, F32), (hk, F32), (hv, BF16), (hv, F32), (hk, F32), (hk, F32), (nq, BF16), (nk, BF16)]
    return pl.pallas_call(
        _inproj_body,
        grid=(t // tm,),
        in_specs=[
            rows(d),
            pl.BlockSpec((None, 6, d), lambda i: (i // tiles_per_seq, 0, 0)),
            full(g_pre), full(wa), full(wlr), full(wgk), full(bgk), full(wb), full(wvt),
            table(), table(), table(),
        ],
        out_specs=[rows(w) for w, _ in out_widths] + [pl.BlockSpec((nk, tm), lambda i: (0, i))],
        out_shape=[jax.ShapeDtypeStruct((t, w), dt) for w, dt in out_widths] + [jax.ShapeDtypeStruct((nk, t), BF16)],
        compiler_params=_params(("arbitrary",), 56),
        name="inproj",
    )(x2, mod, g_pre, wa, wlr, wgk, bgk, wb, wvt, rc, rm, rp)


def _gla_chunk(q, k, la, v_ref, rows, o_ref, states, cum, tri, i_last, i_mid, head_masks):
    hi = la.astype(BF16)
    lo = (la - hi.astype(F32)).astype(BF16)
    b = jnp.dot(cum, hi, preferred_element_type=F32) + jnp.dot(cum, lo, preferred_element_type=F32)
    b_last = b[i_last:i_last + 1, :]
    b_mid = b[i_mid:i_mid + 1, :]
    qs = q * jnp.exp(b - b_mid)
    ks = k * jnp.exp(b_mid - b)
    qi = q * jnp.exp(b)
    kst = k * jnp.exp(b_last - b)
    decay = jnp.exp(b_last)
    new_states = []
    for h in range(GLA_HEADS):
        pair = slice((h // 2) * LANES, (h // 2 + 1) * LANES)
        mask = head_masks[h % 2]
        vcols = slice(h * GLA_DV, (h + 1) * GLA_DV)
        qs_h = jnp.where(mask, qs[:, pair], 0.0).astype(BF16)
        sc = lax.dot_general(qs_h, ks[:, pair].astype(BF16), NT_DIMS, preferred_element_type=F32)
        sc = jnp.where(tri, sc, 0.0)
        v_h = v_ref[rows, vcols]
        st = states[h]
        qi_h = jnp.where(mask, qi[:, pair], 0.0).astype(BF16)
        o = jnp.dot(sc.astype(BF16), v_h, preferred_element_type=F32)
        o = o + lax.dot_general(qi_h, st.astype(BF16), NT_DIMS, preferred_element_type=F32)
        o_ref[rows, vcols] = o
        kv = lax.dot_general(v_h, kst[:, pair].astype(BF16), TN_DIMS, preferred_element_type=F32)
        new_states.append(st * decay[:, pair] + kv)
    return new_states


def _gla_body(qf_ref, kf_ref, vf_ref, laf_ref, qb_ref, kb_ref, vb_ref, lab_ref, of_ref, ob_ref, sf_ref, sb_ref):
    @pl.when(pl.program_id(1) == 0)
    def _():
        sf_ref[...] = jnp.zeros_like(sf_ref)
        sb_ref[...] = jnp.zeros_like(sb_ref)

    c = GLA_CHUNK
    r_i = lax.broadcasted_iota(I32, (c, c), 0)
    c_i = lax.broadcasted_iota(I32, (c, c), 1)
    lower = c_i <= r_i
    upper = c_i >= r_i
    cum_f = jnp.where(lower, 1.0, 0.0).astype(BF16)
    cum_b = jnp.where(upper, 1.0, 0.0).astype(BF16)
    lane = lax.broadcasted_iota(I32, (1, LANES), 1)
    head_masks = (lane < GLA_DK, lane >= GLA_DK)

    st_f = [sf_ref[h] for h in range(GLA_HEADS)]
    st_b = [sb_ref[h] for h in range(GLA_HEADS)]
    for g in range(GLA_GROUP):
        rows_f = slice(g * c, (g + 1) * c)
        gb = GLA_GROUP - 1 - g
        rows_b = slice(gb * c, (gb + 1) * c)
        st_f = _gla_chunk(qf_ref[rows_f, :], kf_ref[rows_f, :], laf_ref[rows_f, :], vf_ref, rows_f, of_ref,
                          st_f, cum_f, lower, c - 1, c // 2 - 1, head_masks)
        st_b = _gla_chunk(qb_ref[rows_b, :], kb_ref[rows_b, :], lab_ref[rows_b, :], vb_ref, rows_b, ob_ref,
                          st_b, cum_b, upper, 0, c // 2, head_masks)
    for h in range(GLA_HEADS):
        sf_ref[h] = st_f[h]
        sb_ref[h] = st_b[h]


def _gla(q, k, v, laf, lab):
    t = q.shape[0]
    rows = GLA_GROUP * GLA_CHUNK
    ng = SEQ // rows
    hk = GLA_HEADS * GLA_DK
    hv = GLA_HEADS * GLA_DV

    def fwd(w):
        return pl.BlockSpec((rows, w), lambda b, n: (b * ng + n, 0))

    def bwd(w):
        return pl.BlockSpec((rows, w), lambda b, n: (b * ng + ng - 1 - n, 0))

    return pl.pallas_call(
        _gla_body,
        grid=(BATCH, ng),
        in_specs=[fwd(hk), fwd(hk), fwd(hv), fwd(hk), bwd(hk), bwd(hk), bwd(hv), bwd(hk)],
        out_specs=[fwd(hv), bwd(hv)],
        out_shape=[jax.ShapeDtypeStruct((t, hv), F32)] * 2,
        scratch_shapes=[pltpu.VMEM((GLA_HEADS, GLA_DV, 2 * GLA_DK), F32)] * 2,
        compiler_params=_params(("arbitrary", "arbitrary"), 32),
        name="gla",
    )(q, k, v, laf, q, k, v, lab)


def _attn_body(sink_ref, q_ref, kp_ref, kc_ref, kn_ref, vp_ref, vc_ref, vn_ref, o_ref):
    step = pl.program_id(1)
    last = pl.num_programs(1) - 1
    qb = ATT_BLOCK
    hd = ATT_HEAD_DIM
    k_all = jnp.concatenate([kp_ref[...], kc_ref[...], kn_ref[...]], axis=0)
    vt_all = jnp.concatenate([vp_ref[...], vc_ref[...], vn_ref[...]], axis=1)
    lane = lax.broadcasted_iota(I32, (1, LANES), 1)
    lo = lane < hd
    j_k = lax.broadcasted_iota(I32, (3 * qb, qb), 0)
    i_q = lax.broadcasted_iota(I32, (3 * qb, qb), 1)
    band = jnp.abs(j_k - qb - i_q) <= ATT_WINDOW
    sinks = [jnp.concatenate([jnp.full((1, qb), sink_ref[4 * g + r], F32) for r in range(4)], axis=1)
             for g in range(ATT_KV_HEADS)]
    for j in range(ATT_GROUP):
        valid = band
        if j == 0:
            valid = valid & ((j_k >= qb) | (step > 0))
        if j == ATT_GROUP - 1:
            valid = valid & ((j_k < 2 * qb) | (step < last))
        valid4 = jnp.concatenate([valid] * 4, axis=1)
        keys = slice(j * qb, (j + 3) * qb)
        rows = slice(j * qb, (j + 1) * qb)
        for g in range(ATT_KV_HEADS):
            kg = k_all[keys, g * LANES:(g + 1) * LANES]
            vgt = vt_all[g * LANES:(g + 1) * LANES, keys]
            qa = q_ref[rows, (2 * g) * LANES:(2 * g + 1) * LANES]
            qc = q_ref[rows, (2 * g + 1) * LANES:(2 * g + 2) * LANES]
            zero = jnp.zeros_like(qa)
            lhs = jnp.concatenate([jnp.where(lo, qa, zero), jnp.where(lo, zero, qa),
                                   jnp.where(lo, qc, zero), jnp.where(lo, zero, qc)], axis=0)
            st = lax.dot_general(kg, lhs, NT_DIMS, preferred_element_type=F32)
            st = jnp.where(valid4, st, NEG_INF)
            sink = sinks[g]
            m = jnp.maximum(jnp.max(st, axis=0, keepdims=True), sink)
            p = jnp.exp(st - m)
            inv = 1.0 / (jnp.sum(p, axis=0, keepdims=True) + jnp.exp(sink - m))
            ot = jnp.dot(vgt, p.astype(BF16), preferred_element_type=F32) * inv
            pair_a = jnp.concatenate([ot[0:hd, 0:qb], ot[hd:2 * hd, qb:2 * qb]], axis=0)
            pair_c = jnp.concatenate([ot[0:hd, 2 * qb:3 * qb], ot[hd:2 * hd, 3 * qb:4 * qb]], axis=0)
            o_ref[rows, (2 * g) * LANES:(2 * g + 1) * LANES] = pair_a.T.astype(o_ref.dtype)
            o_ref[rows, (2 * g + 1) * LANES:(2 * g + 2) * LANES] = pair_c.T.astype(o_ref.dtype)


def _attn(sink, aq, ak2, avt):
    t = aq.shape[0]
    qb = ATT_BLOCK
    nb = SEQ // qb
    steps = nb // ATT_GROUP
    nq = ATT_Q_HEADS * ATT_HEAD_DIM
    nk = 2 * ATT_KV_HEADS * ATT_HEAD_DIM

    def edge_block(b, n, shift):
        return b * nb + jnp.clip(n * ATT_GROUP + shift, 0, nb - 1)

    def k_edge(shift):
        return pl.BlockSpec((qb, nk), lambda b, n: (edge_block(b, n, shift), 0))

    def v_edge(shift):
        return pl.BlockSpec((nk, qb), lambda b, n: (0, edge_block(b, n, shift)))

    def group(w):
        return pl.BlockSpec((ATT_GROUP * qb, w), lambda b, n: (b * steps + n, 0))

    v_group = pl.BlockSpec((nk, ATT_GROUP * qb), lambda b, n: (0, b * steps + n))
    return pl.pallas_call(
        _attn_body,
        grid=(BATCH, steps),
        in_specs=[
            pl.BlockSpec(memory_space=pltpu.SMEM),
            group(nq),
            k_edge(-1), group(nk), k_edge(ATT_GROUP), v_edge(-1), v_group, v_edge(ATT_GROUP),
        ],
        out_specs=group(nq),
        out_shape=jax.ShapeDtypeStruct((t, nq), BF16),
        compiler_params=_params(("arbitrary", "arbitrary"), 48),
        name="attn",
    )(sink, aq, ak2, ak2, ak2, avt, avt, avt)


def _post_body(of_ref, ob_ref, gg_ref, oa_ref, x_ref, mod_ref, ggla_ref, gpm_ref, gpf_ref, wout_ref,
               wrh_ref, wrl_ref, br_ref,
               x1_ref, h2_ref, ti_ref, gt_ref, rk_ref, cnt_ref, base_ref):
    tm = TM_POST

    @pl.when(pl.program_id(0) == 0)
    def _():
        base_ref[...] = jnp.zeros_like(base_ref)

    og = of_ref[...] + ob_ref[...]
    gg = gg_ref[...]
    parts = []
    for h in range(GLA_HEADS):
        cols = slice(h * GLA_DV, (h + 1) * GLA_DV)
        parts.append((_rms(og[:, cols], ggla_ref[...]) * _silu(gg[:, cols])).astype(BF16))
    o = jnp.concatenate(parts + [oa_ref[...]], axis=1)
    y = jnp.dot(o, wout_ref[...], preferred_element_type=F32)

    gate1 = mod_ref[2:3, :]
    shift2 = mod_ref[3:4, :]
    scale2 = mod_ref[4:5, :]
    x1 = x_ref[...] + gate1 * _rms(y, gpm_ref[...])
    x1_ref[...] = x1
    h2 = _rms(x1, gpf_ref[...]) * (1.0 + scale2) + shift2
    h2_hi = h2.astype(BF16)
    h2_hi32 = h2_hi.astype(F32)
    h2_lo = (h2 - h2_hi32).astype(BF16)
    _pack_rows(h2_ref, h2_hi32)

    wrh = wrh_ref[...]
    logits = (lax.dot_general(wrh, h2_hi, NT_DIMS, preferred_element_type=F32)
              + lax.dot_general(wrh, h2_lo, NT_DIMS, preferred_element_type=F32)
              + lax.dot_general(wrl_ref[...], h2_hi, NT_DIMS, preferred_element_type=F32)
              + br_ref[...])
    e_iota = lax.broadcasted_iota(I32, (N_EXPERTS, tm), 0)
    idxs, vals = [], []
    work = logits
    for _ in range(TOP_K):
        m = jnp.max(work, axis=0, keepdims=True)
        idx = jnp.min(jnp.where(work == m, e_iota, N_EXPERTS), axis=0, keepdims=True)
        idxs.append(idx)
        vals.append(m)
        work = jnp.where(e_iota == idx, -jnp.inf, work)
    exps = [jnp.exp(v - vals[0]) for v in vals]
    inv = 1.0 / (exps[0] + exps[1] + exps[2] + exps[3])
    gt_ref[...] = jnp.concatenate([e * inv for e in exps], axis=0)
    ti_ref[...] = jnp.concatenate(idxs, axis=0)

    onehots = [e_iota == idx for idx in idxs]
    member = jnp.where(onehots[0] | onehots[1] | onehots[2] | onehots[3], 1.0, 0.0)
    t_row = lax.broadcasted_iota(I32, (tm, tm), 0)
    t_col = lax.broadcasted_iota(I32, (tm, tm), 1)
    strict = jnp.where(t_row < t_col, 1.0, 0.0).astype(BF16)
    before = base_ref[...] + jnp.dot(member.astype(BF16), strict, preferred_element_type=F32)
    rk_ref[...] = jnp.concatenate(
        [jnp.sum(jnp.where(oh, before, 0.0), axis=0, keepdims=True) for oh in onehots], axis=0).astype(I32)
    new_base = base_ref[...] + jnp.sum(member, axis=1, keepdims=True)
    base_ref[...] = new_base
    cnt_ref[...] = jnp.broadcast_to(new_base, cnt_ref.shape)


def _post(o_f, o_b, gg, o_att, x2, mod, g_gla, g_pm, g_pf, wout, wrh, wrl, br):
    t, d = x2.shape
    tm = TM_POST
    tiles_per_seq = SEQ // tm
    hv = GLA_HEADS * GLA_DV

    def full(a):
        return pl.BlockSpec(a.shape, lambda i: (0,) * a.ndim)

    def rows(w):
        return pl.BlockSpec((tm, w), lambda i: (i, 0))

    def lanes():
        return pl.BlockSpec((TOP_K, tm), lambda i: (0, i))

    return pl.pallas_call(
        _post_body,
        grid=(t // tm,),
        in_specs=[
            rows(hv), rows(hv), rows(hv), rows(hv), rows(d),
            pl.BlockSpec((None, 6, d), lambda i: (i // tiles_per_seq, 0, 0)),
            full(g_gla), full(g_pm), full(g_pf), full(wout), full(wrh), full(wrl), full(br),
        ],
        out_specs=[
            rows(d),
            pl.BlockSpec((tm * PACK_SUB, LANES), lambda i: (i, 0)),
            lanes(), lanes(), lanes(),
            pl.BlockSpec((N_EXPERTS, LANES), lambda i: (0, 0)),
        ],
        out_shape=[
            jax.ShapeDtypeStruct((t, d), F32),
            jax.ShapeDtypeStruct((t * PACK_SUB, LANES), I32),
            jax.ShapeDtypeStruct((TOP_K, t), I32),
            jax.ShapeDtypeStruct((TOP_K, t), F32),
            jax.ShapeDtypeStruct((TOP_K, t), I32),
            jax.ShapeDtypeStruct((N_EXPERTS, LANES), F32),
        ],
        scratch_shapes=[pltpu.VMEM((N_EXPERTS, 1), F32)],
        compiler_params=_params(("arbitrary",), 48),
        name="post",
    )(o_f, o_b, gg, o_att, x2, mod, g_gla, g_pm, g_pf, wout, wrh, wrl, br)


def _route_body(ti_ref, rk_ref, cnt_ref, pos_ref, blk_ref):
    cnt = cnt_ref[...]
    padded = jnp.floor((cnt + (MOE_BM - 1)) * (1.0 / MOE_BM)) * MOE_BM
    starts, ends = [], []
    acc = jnp.zeros((1, LANES), F32)
    for e in range(N_EXPERTS):
        starts.append(acc)
        acc = acc + padded[e:e + 1, :]
        ends.append(acc)
    ti = ti_ref[...]
    off = jnp.zeros(ti.shape, F32)
    for e in range(N_EXPERTS):
        off = jnp.where(ti == e, starts[e][:, 0:1], off)
    pos_ref[...] = rk_ref[...] + off.astype(I32)

    def owner_of(row):
        n_le = jnp.zeros(row.shape, I32)
        for e in range(N_EXPERTS):
            n_le = n_le + jnp.where(ends[e][:, 0:1] <= row, 1, 0)
        return jnp.minimum(n_le, N_EXPERTS - 1)

    block_start = lax.broadcasted_iota(I32, (1, MOE_NB_PAD), 1).astype(F32) * MOE_BM
    owner = owner_of(block_start)
    nxt = jnp.zeros((1, MOE_NB_PAD), I32)
    for e in range(N_EXPERTS):
        end_e = ends[e][:, 0:1]
        nxt = jnp.where(owner == e, jnp.where(end_e < acc[:, 0:1], owner_of(end_e), -1), nxt)
    used = jnp.broadcast_to((acc[:, 0:1] * (1.0 / MOE_BM)).astype(I32), (1, MOE_NB_PAD))
    blk_ref[...] = jnp.concatenate([owner, nxt, used, jnp.zeros((SUBLANES - 3, MOE_NB_PAD), I32)], axis=0)


def _route(top_i, rank, counts):
    return pl.pallas_call(
        _route_body,
        out_shape=[
            jax.ShapeDtypeStruct(top_i.shape, I32),
            jax.ShapeDtypeStruct((SUBLANES, MOE_NB_PAD), I32),
        ],
        compiler_params=pltpu.CompilerParams(vmem_limit_bytes=32 * 1024 * 1024),
        name="route",
    )(top_i, rank, counts)


def _sc_workers():
    info = plsc.get_sparse_core_info()
    return info.num_cores, info.num_subcores, info.num_lanes


def _sc_gather_loop(table_hbm, out_hbm, idx_v, base, chunks, buf0, buf1, sem0, sem1):
    window = SC_GATHER_WINDOW

    def fetch(c, buf, sem):
        return pltpu.make_async_copy(table_hbm.at[idx_v.at[pl.ds(c * window, window)]], buf, sem)

    def flush(c, buf):
        pltpu.sync_copy(buf, out_hbm.at[pl.ds(base + c * window, window)])

    fetch(0, buf0, sem0).start()

    @pl.loop(0, chunks, step=2)
    def _(c):
        fetch(c + 1, buf1, sem1).start()
        fetch(c, buf0, sem0).wait()
        flush(c, buf0)

        @pl.when(c + 2 < chunks)
        def _():
            fetch(c + 2, buf0, sem0).start()

        fetch(c + 1, buf1, sem1).wait()
        flush(c + 1, buf1)


def _sc_dispatch_rows(table, pos_flat, n_rows):
    cores, subcores, lanes = _sc_workers()
    workers = cores * subcores
    window = SC_GATHER_WINDOW
    per_worker = n_rows // workers
    chunks = per_worker // window
    n_assign = pos_flat.shape[0]
    scan = SC_SCAN_CHUNK
    assert per_worker * workers == n_rows and chunks * window == per_worker and chunks % 2 == 0
    assert n_assign % scan == 0 and scan % lanes == 0 and per_worker % lanes == 0
    row_shape = table.shape[1:]
    mesh = plsc.VectorSubcoreMesh(core_axis_name="core", subcore_axis_name="subcore")

    @functools.partial(
        pl.kernel,
        out_type=jax.ShapeDtypeStruct((n_rows,) + row_shape, table.dtype),
        mesh=mesh,
        scratch_types=[
            pltpu.VMEM((per_worker,), I32),
            pltpu.VMEM((scan,), I32),
            pltpu.VMEM((window,) + row_shape, table.dtype),
            pltpu.VMEM((window,) + row_shape, table.dtype),
            pltpu.SemaphoreType.DMA,
            pltpu.SemaphoreType.DMA,
        ],
        compiler_params=pltpu.CompilerParams(needs_layout_passes=False),
        name="sc_dispatch_rows",
    )
    def dispatch(table_hbm, pos_hbm, out_hbm, src_v, pos_v, buf0, buf1, sem0, sem1):
        wid = lax.axis_index("subcore") * cores + lax.axis_index("core")
        base = wid * per_worker
        lane = lax.iota(I32, lanes)

        @pl.loop(0, per_worker, step=lanes)
        def _(j):
            src_v[pl.ds(j, lanes)] = (base + j + lane) & (TOKENS - 1)

        @pl.loop(0, n_assign, step=scan)
        def _(a0):
            pltpu.sync_copy(pos_hbm.at[pl.ds(a0, scan)], pos_v)

            @pl.loop(0, scan, step=lanes)
            def _(j):
                rel = pos_v[pl.ds(j, lanes)] - base
                mine = (rel >= 0) & (rel < per_worker)
                tok = (a0 + j + lane) & (TOKENS - 1)
                plsc.store_scatter(src_v, [jnp.where(mine, rel, 0)], tok, mask=mine)

        _sc_gather_loop(table_hbm, out_hbm, src_v, base, chunks, buf0, buf1, sem0, sem1)

    return dispatch(table, pos_flat)


def _sc_gather_rows(table, idx):
    cores, subcores, _ = _sc_workers()
    workers = cores * subcores
    n = idx.shape[0]
    window = SC_GATHER_WINDOW
    per_worker = n // workers
    chunks = per_worker // window
    assert per_worker * workers == n and chunks * window == per_worker and chunks % 2 == 0
    row_shape = table.shape[1:]
    mesh = plsc.VectorSubcoreMesh(core_axis_name="core", subcore_axis_name="subcore")

    @functools.partial(
        pl.kernel,
        out_type=jax.ShapeDtypeStruct((n,) + row_shape, table.dtype),
        mesh=mesh,
        scratch_types=[
            pltpu.VMEM((per_worker,), I32),
            pltpu.VMEM((window,) + row_shape, table.dtype),
            pltpu.VMEM((window,) + row_shape, table.dtype),
            pltpu.SemaphoreType.DMA,
            pltpu.SemaphoreType.DMA,
        ],
        name="sc_gather_rows",
    )
    def gather(table_hbm, idx_hbm, out_hbm, idx_v, buf0, buf1, sem0, sem1):
        wid = lax.axis_index("subcore") * cores + lax.axis_index("core")
        base = wid * per_worker
        pltpu.sync_copy(idx_hbm.at[pl.ds(base, per_worker)], idx_v)
        _sc_gather_loop(table_hbm, out_hbm, idx_v, base, chunks, buf0, buf1, sem0, sem1)

    return gather(table, idx)


def _experts_body(be_ref, nxt_ref, nu_ref, xs_ref, bgu_ref, bd_ref, wgu_hbm, wd_hbm, ys_ref,
                  wgu_f32, wd_f32, wgu_bf, wd_bf, sems):
    i = pl.program_id(0)
    bm = MOE_BM
    used = i < nu_ref[0]
    prev = be_ref[jnp.maximum(i - 1, 0)]
    fresh = (i == 0) | (be_ref[i] != prev)

    def fetch(e):
        return (pltpu.make_async_copy(wgu_hbm.at[e], wgu_f32, sems.at[0]),
                pltpu.make_async_copy(wd_hbm.at[e], wd_f32, sems.at[1]))

    @pl.when(i == 0)
    def _():
        for cp in fetch(be_ref[0]):
            cp.start()

    @pl.when(used & fresh)
    def _():
        for cp in fetch(be_ref[i]):
            cp.wait()
        wgu_bf[...] = wgu_f32[...].astype(BF16)
        wd_bf[...] = wd_f32[...].astype(BF16)

        @pl.when(nxt_ref[i] >= 0)
        def _():
            for cp in fetch(nxt_ref[i]):
                cp.start()

    @pl.when(used)
    def _():
        x = jnp.concatenate(_unpack_rows(xs_ref, bm), axis=1).astype(BF16)
        gu = jnp.dot(x, wgu_bf[...], preferred_element_type=F32) + bgu_ref[...]
        gate = jnp.minimum(gu[:, 0:D_FF], SWIGLU_LIMIT)
        up = jnp.clip(gu[:, D_FF:2 * D_FF], -SWIGLU_LIMIT, SWIGLU_LIMIT)
        act = ((up + 1.0) * gate * jax.nn.sigmoid(SWIGLU_ALPHA * gate)).astype(BF16)
        y = jnp.dot(act, wd_bf[...], preferred_element_type=F32) + bd_ref[...]
        _pack_rows(ys_ref, y.astype(BF16).astype(F32))

    @pl.when(jnp.logical_not(used))
    def _():
        ys_ref[...] = jnp.zeros_like(ys_ref)


def _experts(block_expert, block_next, n_used, xs, w_gate_up, b_gate_up, w_down, b_down):
    bm = MOE_BM
    d = D_MODEL

    def blk(i, be, nx, nu):
        return jnp.minimum(i, nu[0] - 1)

    grid_spec = pltpu.PrefetchScalarGridSpec(
        num_scalar_prefetch=3,
        grid=(MOE_NB,),
        in_specs=[
            pl.BlockSpec((bm * PACK_SUB, LANES), lambda i, be, nx, nu: (blk(i, be, nx, nu), 0)),
            pl.BlockSpec((None, 1, 2 * D_FF), lambda i, be, nx, nu: (be[blk(i, be, nx, nu)], 0, 0)),
            pl.BlockSpec((None, 1, d), lambda i, be, nx, nu: (be[blk(i, be, nx, nu)], 0, 0)),
            pl.BlockSpec(memory_space=pl.ANY),
            pl.BlockSpec(memory_space=pl.ANY),
        ],
        out_specs=pl.BlockSpec((bm * PACK_SUB, LANES), lambda i, be, nx, nu: (i, 0)),
        scratch_shapes=[
            pltpu.VMEM((d, 2 * D_FF), F32), pltpu.VMEM((D_FF, d), F32),
            pltpu.VMEM((d, 2 * D_FF), BF16), pltpu.VMEM((D_FF, d), BF16),
            pltpu.SemaphoreType.DMA((2,)),
        ],
    )
    return pl.pallas_call(
        _experts_body,
        grid_spec=grid_spec,
        out_shape=jax.ShapeDtypeStruct((MOE_ROWS * PACK_SUB, LANES), I32),
        compiler_params=_params(("arbitrary",), 48),
        name="experts",
    )(block_expert, block_next, n_used, xs, b_gate_up, b_down, w_gate_up, w_down)


def _combine_body(gates_ref, x1_ref, mod_ref, gpost_ref, y0_ref, y1_ref, y2_ref, y3_ref, o_ref):
    tm = TM_COMB
    gates = gates_ref[...]
    y_hi = jnp.zeros((tm, PACK_COLS), F32)
    y_lo = jnp.zeros((tm, PACK_COLS), F32)
    for k, yk_ref in enumerate((y0_ref, y1_ref, y2_ref, y3_ref)):
        hi, lo = _unpack_rows(yk_ref, tm)
        y_hi = y_hi + hi * gates[:, k:k + 1]
        y_lo = y_lo + lo * gates[:, k:k + 1]
    y = jnp.concatenate([y_hi, y_lo], axis=1)
    gate2 = mod_ref[5:6, :]
    o_ref[...] = x1_ref[...] + gate2 * _rms(y, gpost_ref[...])


def _combine(gates_t, x1, mod, g_post, y4):
    t, d = x1.shape
    tm = TM_COMB
    tiles = t // tm
    tiles_per_seq = SEQ // tm

    def slab(k):
        return pl.BlockSpec((tm * PACK_SUB, LANES), lambda i: (k * tiles + i, 0))

    return pl.pallas_call(
        _combine_body,
        grid=(tiles,),
        in_specs=[
            pl.BlockSpec((tm, TOP_K), lambda i: (i, 0)),
            pl.BlockSpec((tm, d), lambda i: (i, 0)),
            pl.BlockSpec((None, 6, d), lambda i: (i // tiles_per_seq, 0, 0)),
            pl.BlockSpec(g_post.shape, lambda i: (0, 0)),
            slab(0), slab(1), slab(2), slab(3),
        ],
        out_specs=pl.BlockSpec((tm, d), lambda i: (i, 0)),
        out_shape=jax.ShapeDtypeStruct((t, d), F32),
        compiler_params=_params(("arbitrary",), 48),
        name="combine",
    )(gates_t, x1, mod, g_post, y4, y4, y4, y4)


def _rotary_tables():
    half = ROT_DIM // 2
    inv_freq = ROPE_THETA ** (-2.0 * jnp.arange(half, dtype=F32) / ROT_DIM)
    ang = jnp.arange(SEQ).astype(F32)[:, None] * inv_freq[None, :]
    cos, sin = jnp.cos(ang), jnp.sin(ang)
    ones = jnp.ones((SEQ, ATT_HEAD_DIM - ROT_DIM), F32)
    zeros = jnp.zeros((SEQ, ATT_HEAD_DIM - ROT_DIM), F32)
    zh = jnp.zeros((SEQ, half), F32)
    reps = LANES // ATT_HEAD_DIM
    rc = jnp.tile(jnp.concatenate([cos, cos, ones], axis=1), (1, reps))
    rm = jnp.tile(jnp.concatenate([-sin, zh, zeros], axis=1), (1, reps))
    rp = jnp.tile(jnp.concatenate([zh, sin, zeros], axis=1), (1, reps))
    return rc, rm, rp


def _mixer_inputs(w_in, w_gk_fwd, b_gk_fwd, w_gk_bwd, b_gk_bwd):
    hk = GLA_HEADS * GLA_DK
    hv = GLA_HEADS * GLA_DV
    w = w_in[0]
    o_lr = 2 * hk + 2 * hv
    o_aq = o_lr + 2 * GLA_RANK
    o_ak = o_aq + ATT_Q_HEADS * ATT_HEAD_DIM
    o_av = o_ak + ATT_KV_HEADS * ATT_HEAD_DIM
    hd = ATT_HEAD_DIM
    wa = w[:, :o_lr].astype(BF16)
    wlr = w[:, o_lr:o_aq].astype(BF16)
    dup = lambda m: jnp.concatenate([m[:, g * hd:(g + 1) * hd] for g in range(ATT_KV_HEADS) for _ in range(2)], axis=1)
    wb = jnp.concatenate([w[:, o_aq:o_ak], dup(w[:, o_ak:o_av])], axis=1).astype(BF16)
    wvt = dup(w[:, o_av:o_av + ATT_KV_HEADS * hd]).T.astype(BF16)
    zr = jnp.zeros((GLA_RANK, hk), F32)
    wgk = jnp.concatenate([jnp.concatenate([w_gk_fwd[0], zr], axis=1),
                           jnp.concatenate([zr, w_gk_bwd[0]], axis=1)], axis=0).astype(BF16)
    bgk = jnp.concatenate([b_gk_fwd[0], b_gk_bwd[0]])[None, :]
    return (wa, wlr, wgk, bgk, wb, wvt) + _rotary_tables()


def kernel(x, c, w_ada, b_ada, g_pre_mix, g_post_mix, w_in, w_gk_fwd, b_gk_fwd, w_gk_bwd, b_gk_bwd, g_gla_out,
           attn_sink, w_out, g_pre_ffn, g_post_ffn, w_router, b_router, w_gate_up, b_gate_up, w_down, b_down):
    assert x.shape == (BATCH, SEQ, D_MODEL) and w_ada.shape[0] == 1
    d = D_MODEL
    x2 = x.reshape(TOKENS, d)

    c_pad = jnp.pad(c, ((0, SUBLANES - BATCH), (0, 0)))
    mod = _ada(c_pad, w_ada[0], b_ada)[:BATCH].reshape(BATCH, 6, d)

    mixer_in = _mixer_inputs(w_in, w_gk_fwd, b_gk_fwd, w_gk_bwd, b_gk_bwd)
    q, k, v, gg, laf, lab, aq, ak2, avt = _inproj(x2, mod, g_pre_mix, *mixer_in)
    o_f, o_b = _gla(q, k, v, laf, lab)
    o_att = _attn(attn_sink[0], aq, ak2, avt)

    wr_t = w_router[0].T
    wrh = wr_t.astype(BF16)
    wrl = (wr_t - wrh.astype(F32)).astype(BF16)
    x1, h2_tiles, top_i, gates, rank, counts = _post(
        o_f, o_b, gg, o_att, x2, mod, g_gla_out, g_post_mix, g_pre_ffn, w_out[0].astype(BF16), wrh, wrl,
        b_router[0][:, None])

    pos, blocks = _route(top_i, rank, counts)
    pos_flat = pos.reshape(TOP_K * TOKENS)
    xs = _sc_dispatch_rows(h2_tiles.reshape(TOKENS, PACK_SUB, LANES), pos_flat, MOE_ROWS)
    xs = xs.reshape(MOE_ROWS * PACK_SUB, LANES)
    ys = _experts(blocks[0, :MOE_NB], blocks[1, :MOE_NB], blocks[2, :1], xs,
                  w_gate_up[0], b_gate_up[0][:, None, :], w_down[0], b_down[0][:, None, :])
    y4 = _sc_gather_rows(ys.reshape(MOE_ROWS, PACK_SUB, LANES), pos_flat)
    out = _combine(gates.T, x1, mod, g_post_ffn, y4.reshape(TOP_K * TOKENS * PACK_SUB, LANES))
    return out.reshape(BATCH, SEQ, d)
```

```python
import functools

import jax
import jax.numpy as jnp
import numpy as np
from jax import lax
from jax.experimental import pallas as pl
from jax.experimental.pallas import tpu as pltpu
from jax.experimental.pallas import tpu_sc as plsc

F32 = jnp.float32
BF16 = jnp.bfloat16
I32 = jnp.int32

D_MODEL = 1024
BATCH = 2
SEQ = 8192
TOKENS = BATCH * SEQ
GLA_HEADS = 4
GLA_DV = 128
GLA_DK = 64
GLA_RANK = 16
GLA_GATE_NORMALIZER = 16.0
GLA_CHUNK = 64
ATT_Q_HEADS = 8
ATT_KV_HEADS = 2
ATT_HEAD_DIM = 64
ATT_WINDOW = 128
ATT_BLOCK = 128
ROT_DIM = 16
ROPE_THETA = 500000.0
N_EXPERTS = 32
TOP_K = 4
D_FF = 1024
SWIGLU_LIMIT = 7.0
SWIGLU_ALPHA = 1.702
NORM_EPS = 1e-6
NEG_INF = -1e30

LANES = 128
SUBLANES = 8
PACK_COLS = D_MODEL // 2
PACK_SUB = PACK_COLS // LANES

TM_IN = 512
GLA_GROUP = 4
ATT_GROUP = 4
TM_POST = 256
MOE_BM = 256
MOE_ROWS = TOKENS * TOP_K + N_EXPERTS * MOE_BM
MOE_NB = MOE_ROWS // MOE_BM
MOE_SUB = 2
MOE_NB_PAD = ((MOE_NB + LANES - 1) // LANES) * LANES
SC_SCAN_CHUNK = 4096
TM_COMB = 256
SC_GATHER_WINDOW = 64

NT_DIMS = (((1,), (1,)), ((), ()))
TN_DIMS = (((0,), (0,)), ((), ()))


def _params(semantics, vmem_mib):
    return pltpu.CompilerParams(dimension_semantics=semantics, vmem_limit_bytes=vmem_mib * 1024 * 1024)


def _rms(x, g):
    return x * lax.rsqrt(jnp.mean(x * x, axis=-1, keepdims=True) + NORM_EPS) * g


def _silu(x):
    return x * jax.nn.sigmoid(x)


def _pack_rows(ref, v):
    m = v.shape[0]
    bits = lax.bitcast_convert_type(v, jnp.uint32)
    word = lax.bitcast_convert_type(bits[:, :PACK_COLS] | (bits[:, PACK_COLS:] >> 16), I32)
    for s in range(PACK_SUB):
        ref[pl.ds(s, m, stride=PACK_SUB), :] = word[:, s * LANES:(s + 1) * LANES]


def _unpack_rows(ref, m):
    word = jnp.concatenate([ref[pl.ds(s, m, stride=PACK_SUB), :] for s in range(PACK_SUB)], axis=1)
    bits = lax.bitcast_convert_type(word, jnp.uint32)
    hi = lax.bitcast_convert_type(bits & jnp.uint32(0xFFFF0000), F32)
    lo = lax.bitcast_convert_type(bits << 16, F32)
    return hi, lo


def _ada_body(c_ref, w_ref, b_ref, o_ref):
    ca = _silu(c_ref[...]).astype(BF16)
    o_ref[...] = jnp.dot(ca, w_ref[...].astype(BF16), preferred_element_type=F32) + b_ref[...]


def _ada(c_pad, w_ada, b_ada):
    d = D_MODEL
    return pl.pallas_call(
        _ada_body,
        grid=(6,),
        in_specs=[
            pl.BlockSpec((SUBLANES, d), lambda j: (0, 0)),
            pl.BlockSpec((d, d), lambda j: (0, j)),
            pl.BlockSpec((1, d), lambda j: (0, j)),
        ],
        out_specs=pl.BlockSpec((SUBLANES, d), lambda j: (0, j)),
        out_shape=jax.ShapeDtypeStruct((SUBLANES, 6 * d), F32),
        compiler_params=_params(("arbitrary",), 32),
        name="ada",
    )(c_pad, w_ada, b_ada)


def _rotary(x, cos_t, msin_t, psin_t):
    width = x.shape[1]
    reps = width // LANES
    c = jnp.concatenate([cos_t] * reps, axis=1)
    m = jnp.concatenate([msin_t] * reps, axis=1)
    p = jnp.concatenate([psin_t] * reps, axis=1)
    half = ROT_DIM // 2
    return x * c + pltpu.roll(x, width - half, 1) * m + pltpu.roll(x, half, 1) * p


def _inproj_body(x_ref, mod_ref, g_ref, wa_ref, wlr_ref, wgk_ref, bgk_ref, wb_ref, wvt_ref, rc_ref, rm_ref, rp_ref,
                 q_ref, k_ref, v_ref, gg_ref, laf_ref, lab_ref, aq_ref, ak_ref, avt_ref):
    shift = mod_ref[0:1, :]
    scale = mod_ref[1:2, :]
    h = (_rms(x_ref[...], g_ref[...]) * (1.0 + scale) + shift).astype(BF16)

    hk = GLA_HEADS * GLA_DK
    hv = GLA_HEADS * GLA_DV
    pa = jnp.dot(h, wa_ref[...], preferred_element_type=F32)
    q_ref[...] = pa[:, 0:hk] * (GLA_DK ** -0.5)
    k_ref[...] = pa[:, hk:2 * hk]
    v_ref[...] = pa[:, 2 * hk:2 * hk + hv].astype(BF16)
    gg_ref[...] = pa[:, 2 * hk + hv:2 * hk + 2 * hv]

    plr = jnp.dot(h, wlr_ref[...], preferred_element_type=F32)
    gk = jnp.dot(plr.astype(BF16), wgk_ref[...], preferred_element_type=F32) + bgk_ref[...]
    la = (jnp.minimum(gk, 0.0) - jnp.log1p(jnp.exp(-jnp.abs(gk)))) * (1.0 / GLA_GATE_NORMALIZER)
    laf_ref[...] = la[:, 0:hk]
    lab_ref[...] = la[:, hk:2 * hk]

    pb = jnp.dot(h, wb_ref[...], preferred_element_type=F32)
    nq = ATT_Q_HEADS * ATT_HEAD_DIM
    nk = 2 * ATT_KV_HEADS * ATT_HEAD_DIM
    rc, rm, rp = rc_ref[...], rm_ref[...], rp_ref[...]
    aq_ref[...] = (_rotary(pb[:, 0:nq], rc, rm, rp) * (ATT_HEAD_DIM ** -0.5)).astype(BF16)
    ak_ref[...] = _rotary(pb[:, nq:nq + nk], rc, rm, rp).astype(BF16)
    avt_ref[...] = lax.dot_general(wvt_ref[...], h, NT_DIMS, preferred_element_type=F32).astype(BF16)


def _inproj(x2, mod, g_pre, wa, wlr, wgk, bgk, wb, wvt, rc, rm, rp):
    t, d = x2.shape
    tm = TM_IN
    tiles_per_seq = SEQ // tm
    hk = GLA_HEADS * GLA_DK
    hv = GLA_HEADS * GLA_DV
    nq = ATT_Q_HEADS * ATT_HEAD_DIM
    nk = 2 * ATT_KV_HEADS * ATT_HEAD_DIM

    def full(a):
        return pl.BlockSpec(a.shape, lambda i: (0,) * a.ndim)

    def rows(w):
        return pl.BlockSpec((tm, w), lambda i: (i, 0))

    def table():
        return pl.BlockSpec((tm, LANES), lambda i: (i % tiles_per_seq, 0))

    out_widths = [(hk, F32), (hk, F32), (hv, BF16), (hv, F32), (hk, F32), (hk, F32), (nq, BF16), (nk, BF16)]
    return pl.pallas_call(
        _inproj_body,
        grid=(t // tm,),
        in_specs=[
            rows(d),
            pl.BlockSpec((None, 6, d), lambda i: (i // tiles_per_seq, 0, 0)),
            full(g_pre), full(wa), full(wlr), full(wgk), full(bgk), full(wb), full(wvt),
            table(), table(), table(),
        ],
        out_specs=[rows(w) for w, _ in out_widths] + [pl.BlockSpec((nk, tm), lambda i: (0, i))],
        out_shape=[jax.ShapeDtypeStruct((t, w), dt) for w, dt in out_widths] + [jax.ShapeDtypeStruct((nk, t), BF16)],
        compiler_params=_params(("arbitrary",), 56),
        name="inproj",
    )(x2, mod, g_pre, wa, wlr, wgk, bgk, wb, wvt, rc, rm, rp)


def _gla_chunk(q, k, la, v_ref, rows, o_ref, states, cum, tri, i_last, i_mid, head_masks):
    hi = la.astype(BF16)
    lo = (la - hi.astype(F32)).astype(BF16)
    b = jnp.dot(cum, hi, preferred_element_type=F32) + jnp.dot(cum, lo, preferred_element_type=F32)
    b_last = b[i_last:i_last + 1, :]
    b_mid = b[i_mid:i_mid + 1, :]
    qs = q * jnp.exp(b - b_mid)
    ks = k * jnp.exp(b_mid - b)
    qi = q * jnp.exp(b)
    kst = k * jnp.exp(b_last - b)
    decay = jnp.exp(b_last)
    new_states = []
    for h in range(GLA_HEADS):
        pair = slice((h // 2) * LANES, (h // 2 + 1) * LANES)
        mask = head_masks[h % 2]
        vcols = slice(h * GLA_DV, (h + 1) * GLA_DV)
        qs_h = jnp.where(mask, qs[:, pair], 0.0).astype(BF16)
        sc = lax.dot_general(qs_h, ks[:, pair].astype(BF16), NT_DIMS, preferred_element_type=F32)
        sc = jnp.where(tri, sc, 0.0)
        v_h = v_ref[rows, vcols]
        st = states[h]
        qi_h = jnp.where(mask, qi[:, pair], 0.0).astype(BF16)
        o = jnp.dot(sc.astype(BF16), v_h, preferred_element_type=F32)
        o = o + lax.dot_general(qi_h, st.astype(BF16), NT_DIMS, preferred_element_type=F32)
        o_ref[rows, vcols] = o
        kv = lax.dot_general(v_h, kst[:, pair].astype(BF16), TN_DIMS, preferred_element_type=F32)
        new_states.append(st * decay[:, pair] + kv)
    return new_states


def _gla_body(qf_ref, kf_ref, vf_ref, laf_ref, qb_ref, kb_ref, vb_ref, lab_ref, of_ref, ob_ref, sf_ref, sb_ref):
    @pl.when(pl.program_id(1) == 0)
    def _():
        sf_ref[...] = jnp.zeros_like(sf_ref)
        sb_ref[...] = jnp.zeros_like(sb_ref)

    c = GLA_CHUNK
    r_i = lax.broadcasted_iota(I32, (c, c), 0)
    c_i = lax.broadcasted_iota(I32, (c, c), 1)
    lower = c_i <= r_i
    upper = c_i >= r_i
    cum_f = jnp.where(lower, 1.0, 0.0).astype(BF16)
    cum_b = jnp.where(upper, 1.0, 0.0).astype(BF16)
    lane = lax.broadcasted_iota(I32, (1, LANES), 1)
    head_masks = (lane < GLA_DK, lane >= GLA_DK)

    st_f = [sf_ref[h] for h in range(GLA_HEADS)]
    st_b = [sb_ref[h] for h in range(GLA_HEADS)]
    for g in range(GLA_GROUP):
        rows_f = slice(g * c, (g + 1) * c)
        gb = GLA_GROUP - 1 - g
        rows_b = slice(gb * c, (gb + 1) * c)
        st_f = _gla_chunk(qf_ref[rows_f, :], kf_ref[rows_f, :], laf_ref[rows_f, :], vf_ref, rows_f, of_ref,
                          st_f, cum_f, lower, c - 1, c // 2 - 1, head_masks)
        st_b = _gla_chunk(qb_ref[rows_b, :], kb_ref[rows_b, :], lab_ref[rows_b, :], vb_ref, rows_b, ob_ref,
                          st_b, cum_b, upper, 0, c // 2, head_masks)
    for h in range(GLA_HEADS):
        sf_ref[h] = st_f[h]
        sb_ref[h] = st_b[h]


def _gla(q, k, v, laf, lab):
    t = q.shape[0]
    rows = GLA_GROUP * GLA_CHUNK
    ng = SEQ // rows
    hk = GLA_HEADS * GLA_DK
    hv = GLA_HEADS * GLA_DV

    def fwd(w):
        return pl.BlockSpec((rows, w), lambda b, n: (b * ng + n, 0))

    def bwd(w):
        return pl.BlockSpec((rows, w), lambda b, n: (b * ng + ng - 1 - n, 0))

    return pl.pallas_call(
        _gla_body,
        grid=(BATCH, ng),
        in_specs=[fwd(hk), fwd(hk), fwd(hv), fwd(hk), bwd(hk), bwd(hk), bwd(hv), bwd(hk)],
        out_specs=[fwd(hv), bwd(hv)],
        out_shape=[jax.ShapeDtypeStruct((t, hv), F32)] * 2,
        scratch_shapes=[pltpu.VMEM((GLA_HEADS, GLA_DV, 2 * GLA_DK), F32)] * 2,
        compiler_params=_params(("arbitrary", "arbitrary"), 32),
        name="gla",
    )(q, k, v, laf, q, k, v, lab)


def _attn_body(sink_ref, q_ref, kp_ref, kc_ref, kn_ref, vp_ref, vc_ref, vn_ref, o_ref):
    step = pl.program_id(1)
    last = pl.num_programs(1) - 1
    qb = ATT_BLOCK
    hd = ATT_HEAD_DIM
    k_all = jnp.concatenate([kp_ref[...], kc_ref[...], kn_ref[...]], axis=0)
    vt_all = jnp.concatenate([vp_ref[...], vc_ref[...], vn_ref[...]], axis=1)
    lane = lax.broadcasted_iota(I32, (1, LANES), 1)
    lo = lane < hd
    j_k = lax.broadcasted_iota(I32, (3 * qb, qb), 0)
    i_q = lax.broadcasted_iota(I32, (3 * qb, qb), 1)
    band = jnp.abs(j_k - qb - i_q) <= ATT_WINDOW
    sinks = [jnp.concatenate([jnp.full((1, qb), sink_ref[4 * g + r], F32) for r in range(4)], axis=1)
             for g in range(ATT_KV_HEADS)]
    for j in range(ATT_GROUP):
        valid = band
        if j == 0:
            valid = valid & ((j_k >= qb) | (step > 0))
        if j == ATT_GROUP - 1:
            valid = valid & ((j_k < 2 * qb) | (step < last))
        valid4 = jnp.concatenate([valid] * 4, axis=1)
        keys = slice(j * qb, (j + 3) * qb)
        rows = slice(j * qb, (j + 1) * qb)
        for g in range(ATT_KV_HEADS):
            kg = k_all[keys, g * LANES:(g + 1) * LANES]
            vgt = vt_all[g * LANES:(g + 1) * LANES, keys]
            qa = q_ref[rows, (2 * g) * LANES:(2 * g + 1) * LANES]
            qc = q_ref[rows, (2 * g + 1) * LANES:(2 * g + 2) * LANES]
            zero = jnp.zeros_like(qa)
            lhs = jnp.concatenate([jnp.where(lo, qa, zero), jnp.where(lo, zero, qa),
                                   jnp.where(lo, qc, zero), jnp.where(lo, zero, qc)], axis=0)
            st = lax.dot_general(kg, lhs, NT_DIMS, preferred_element_type=F32)
            st = jnp.where(valid4, st, NEG_INF)
            sink = sinks[g]
            m = jnp.maximum(jnp.max(st, axis=0, keepdims=True), sink)
            p = jnp.exp(st - m)
            inv = 1.0 / (jnp.sum(p, axis=0, keepdims=True) + jnp.exp(sink - m))
            ot = jnp.dot(vgt, p.astype(BF16), preferred_element_type=F32) * inv
            pair_a = jnp.concatenate([ot[0:hd, 0:qb], ot[hd:2 * hd, qb:2 * qb]], axis=0)
            pair_c = jnp.concatenate([ot[0:hd, 2 * qb:3 * qb], ot[hd:2 * hd, 3 * qb:4 * qb]], axis=0)
            o_ref[rows, (2 * g) * LANES:(2 * g + 1) * LANES] = pair_a.T.astype(o_ref.dtype)
            o_ref[rows, (2 * g + 1) * LANES:(2 * g + 2) * LANES] = pair_c.T.astype(o_ref.dtype)


def _attn(sink, aq, ak2, avt):
    t = aq.shape[0]
    qb = ATT_BLOCK
    nb = SEQ // qb
    steps = nb // ATT_GROUP
    nq = ATT_Q_HEADS * ATT_HEAD_DIM
    nk = 2 * ATT_KV_HEADS * ATT_HEAD_DIM

    def edge_block(b, n, shift):
        return b * nb + jnp.clip(n * ATT_GROUP + shift, 0, nb - 1)

    def k_edge(shift):
        return pl.BlockSpec((qb, nk), lambda b, n: (edge_block(b, n, shift), 0))

    def v_edge(shift):
        return pl.BlockSpec((nk, qb), lambda b, n: (0, edge_block(b, n, shift)))

    def group(w):
        return pl.BlockSpec((ATT_GROUP * qb, w), lambda b, n: (b * steps + n, 0))

    v_group = pl.BlockSpec((nk, ATT_GROUP * qb), lambda b, n: (0, b * steps + n))
    return pl.pallas_call(
        _attn_body,
        grid=(BATCH, steps),
        in_specs=[
            pl.BlockSpec(memory_space=pltpu.SMEM),
            group(nq),
            k_edge(-1), group(nk), k_edge(ATT_GROUP), v_edge(-1), v_group, v_edge(ATT_GROUP),
        ],
        out_specs=group(nq),
        out_shape=jax.ShapeDtypeStruct((t, nq), BF16),
        compiler_params=_params(("arbitrary", "arbitrary"), 48),
        name="attn",
    )(sink, aq, ak2, ak2, ak2, avt, avt, avt)


def _post_body(of_ref, ob_ref, gg_ref, oa_ref, x_ref, mod_ref, ggla_ref, gpm_ref, gpf_ref, wout_ref,
               wrh_ref, wrl_ref, br_ref,
               x1_ref, h2_ref, ti_ref, gt_ref, rk_ref, cnt_ref, base_ref):
    tm = TM_POST

    @pl.when(pl.program_id(0) == 0)
    def _():
        base_ref[...] = jnp.zeros_like(base_ref)

    og = of_ref[...] + ob_ref[...]
    gg = gg_ref[...]
    parts = []
    for h in range(GLA_HEADS):
        cols = slice(h * GLA_DV, (h + 1) * GLA_DV)
        parts.append((_rms(og[:, cols], ggla_ref[...]) * _silu(gg[:, cols])).astype(BF16))
    o = jnp.concatenate(parts + [oa_ref[...]], axis=1)
    y = jnp.dot(o, wout_ref[...], preferred_element_type=F32)

    gate1 = mod_ref[2:3, :]
    shift2 = mod_ref[3:4, :]
    scale2 = mod_ref[4:5, :]
    x1 = x_ref[...] + gate1 * _rms(y, gpm_ref[...])
    x1_ref[...] = x1
    h2 = _rms(x1, gpf_ref[...]) * (1.0 + scale2) + shift2
    h2_hi = h2.astype(BF16)
    h2_hi32 = h2_hi.astype(F32)
    h2_lo = (h2 - h2_hi32).astype(BF16)
    _pack_rows(h2_ref, h2_hi32)

    wrh = wrh_ref[...]
    logits = (lax.dot_general(wrh, h2_hi, NT_DIMS, preferred_element_type=F32)
              + lax.dot_general(wrh, h2_lo, NT_DIMS, preferred_element_type=F32)
              + lax.dot_general(wrl_ref[...], h2_hi, NT_DIMS, preferred_element_type=F32)
              + br_ref[...])
    e_iota = lax.broadcasted_iota(I32, (N_EXPERTS, tm), 0)
    idxs, vals = [], []
    work = logits
    for _ in range(TOP_K):
        m = jnp.max(work, axis=0, keepdims=True)
        idx = jnp.min(jnp.where(work == m, e_iota, N_EXPERTS), axis=0, keepdims=True)
        idxs.append(idx)
        vals.append(m)
        work = jnp.where(e_iota == idx, -jnp.inf, work)
    exps = [jnp.exp(v - vals[0]) for v in vals]
    inv = 1.0 / (exps[0] + exps[1] + exps[2] + exps[3])
    gt_ref[...] = jnp.concatenate([e * inv for e in exps], axis=0)
    ti_ref[...] = jnp.concatenate(idxs, axis=0)

    onehots = [e_iota == idx for idx in idxs]
    member = jnp.where(onehots[0] | onehots[1] | onehots[2] | onehots[3], 1.0, 0.0)
    t_row = lax.broadcasted_iota(I32, (tm, tm), 0)
    t_col = lax.broadcasted_iota(I32, (tm, tm), 1)
    strict = jnp.where(t_row < t_col, 1.0, 0.0).astype(BF16)
    before = base_ref[...] + jnp.dot(member.astype(BF16), strict, preferred_element_type=F32)
    rk_ref[...] = jnp.concatenate(
        [jnp.sum(jnp.where(oh, before, 0.0), axis=0, keepdims=True) for oh in onehots], axis=0).astype(I32)
    new_base = base_ref[...] + jnp.sum(member, axis=1, keepdims=True)
    base_ref[...] = new_base
    cnt_ref[...] = jnp.broadcast_to(new_base, cnt_ref.shape)


def _post(o_f, o_b, gg, o_att, x2, mod, g_gla, g_pm, g_pf, wout, wrh, wrl, br):
    t, d = x2.shape
    tm = TM_POST
    tiles_per_seq = SEQ // tm
    hv = GLA_HEADS * GLA_DV

    def full(a):
        return pl.BlockSpec(a.shape, lambda i: (0,) * a.ndim)

    def rows(w):
        return pl.BlockSpec((tm, w), lambda i: (i, 0))

    def lanes():
        return pl.BlockSpec((TOP_K, tm), lambda i: (0, i))

    return pl.pallas_call(
        _post_body,
        grid=(t // tm,),
        in_specs=[
            rows(hv), rows(hv), rows(hv), rows(hv), rows(d),
            pl.BlockSpec((None, 6, d), lambda i: (i // tiles_per_seq, 0, 0)),
            full(g_gla), full(g_pm), full(g_pf), full(wout), full(wrh), full(wrl), full(br),
        ],
        out_specs=[
            rows(d),
            pl.BlockSpec((tm * PACK_SUB, LANES), lambda i: (i, 0)),
            lanes(), lanes(), lanes(),
            pl.BlockSpec((N_EXPERTS, LANES), lambda i: (0, 0)),
        ],
        out_shape=[
            jax.ShapeDtypeStruct((t, d), F32),
            jax.ShapeDtypeStruct((t * PACK_SUB, LANES), I32),
            jax.ShapeDtypeStruct((TOP_K, t), I32),
            jax.ShapeDtypeStruct((TOP_K, t), F32),
            jax.ShapeDtypeStruct((TOP_K, t), I32),
            jax.ShapeDtypeStruct((N_EXPERTS, LANES), F32),
        ],
        scratch_shapes=[pltpu.VMEM((N_EXPERTS, 1), F32)],
        compiler_params=_params(("arbitrary",), 48),
        name="post",
    )(o_f, o_b, gg, o_att, x2, mod, g_gla, g_pm, g_pf, wout, wrh, wrl, br)


def _route_body(ti_ref, rk_ref, cnt_ref, pos_ref, blk_ref):
    cnt = cnt_ref[...]
    padded = jnp.floor((cnt + (MOE_BM - 1)) * (1.0 / MOE_BM)) * MOE_BM
    starts, ends = [], []
    acc = jnp.zeros((1, LANES), F32)
    for e in range(N_EXPERTS):
        starts.append(acc)
        acc = acc + padded[e:e + 1, :]
        ends.append(acc)
    ti = ti_ref[...]
    off = jnp.zeros(ti.shape, F32)
    for e in range(N_EXPERTS):
        off = jnp.where(ti == e, starts[e][:, 0:1], off)
    pos_ref[...] = rk_ref[...] + off.astype(I32)

    def owner_of(row):
        n_le = jnp.zeros(row.shape, I32)
        for e in range(N_EXPERTS):
            n_le = n_le + jnp.where(ends[e][:, 0:1] <= row, 1, 0)
        return jnp.minimum(n_le, N_EXPERTS - 1)

    block_start = lax.broadcasted_iota(I32, (1, MOE_NB_PAD), 1).astype(F32) * MOE_BM
    owner = owner_of(block_start)
    nxt = jnp.zeros((1, MOE_NB_PAD), I32)
    for e in range(N_EXPERTS):
        end_e = ends[e][:, 0:1]
        nxt = jnp.where(owner == e, jnp.where(end_e < acc[:, 0:1], owner_of(end_e), -1), nxt)
    used = jnp.broadcast_to((acc[:, 0:1] * (1.0 / MOE_BM)).astype(I32), (1, MOE_NB_PAD))
    blk_ref[...] = jnp.concatenate([owner, nxt, used, jnp.zeros((SUBLANES - 3, MOE_NB_PAD), I32)], axis=0)


def _route(top_i, rank, counts):
    return pl.pallas_call(
        _route_body,
        out_shape=[
            jax.ShapeDtypeStruct(top_i.shape, I32),
            jax.ShapeDtypeStruct((SUBLANES, MOE_NB_PAD), I32),
        ],
        compiler_params=pltpu.CompilerParams(vmem_limit_bytes=32 * 1024 * 1024),
        name="route",
    )(top_i, rank, counts)


def _sc_workers():
    info = plsc.get_sparse_core_info()
    return info.num_cores, info.num_subcores, info.num_lanes


def _sc_gather_loop(table_hbm, out_hbm, idx_v, base, chunks, buf0, buf1, sem0, sem1):
    window = SC_GATHER_WINDOW

    def fetch(c, buf, sem):
        return pltpu.make_async_copy(table_hbm.at[idx_v.at[pl.ds(c * window, window)]], buf, sem)

    def flush(c, buf):
        pltpu.sync_copy(buf, out_hbm.at[pl.ds(base + c * window, window)])

    fetch(0, buf0, sem0).start()

    @pl.loop(0, chunks, step=2)
    def _(c):
        fetch(c + 1, buf1, sem1).start()
        fetch(c, buf0, sem0).wait()
        flush(c, buf0)

        @pl.when(c + 2 < chunks)
        def _():
            fetch(c + 2, buf0, sem0).start()

        fetch(c + 1, buf1, sem1).wait()
        flush(c + 1, buf1)


def _sc_dispatch_rows(table, pos_flat, n_rows):
    cores, subcores, lanes = _sc_workers()
    workers = cores * subcores
    window = SC_GATHER_WINDOW
    per_worker = n_rows // workers
    chunks = per_worker // window
    n_assign = pos_flat.shape[0]
    scan = SC_SCAN_CHUNK
    assert per_worker * workers == n_rows and chunks * window == per_worker and chunks % 2 == 0
    assert n_assign % scan == 0 and scan % lanes == 0 and per_worker % lanes == 0
    row_shape = table.shape[1:]
    mesh = plsc.VectorSubcoreMesh(core_axis_name="core", subcore_axis_name="subcore")

    @functools.partial(
        pl.kernel,
        out_type=jax.ShapeDtypeStruct((n_rows,) + row_shape, table.dtype),
        mesh=mesh,
        scratch_types=[
            pltpu.VMEM((per_worker,), I32),
            pltpu.VMEM((scan,), I32),
            pltpu.VMEM((window,) + row_shape, table.dtype),
            pltpu.VMEM((window,) + row_shape, table.dtype),
            pltpu.SemaphoreType.DMA,
            pltpu.SemaphoreType.DMA,
        ],
        compiler_params=pltpu.CompilerParams(needs_layout_passes=False),
        name="sc_dispatch_rows",
    )
    def dispatch(table_hbm, pos_hbm, out_hbm, src_v, pos_v, buf0, buf1, sem0, sem1):
        wid = lax.axis_index("subcore") * cores + lax.axis_index("core")
        base = wid * per_worker
        lane = lax.iota(I32, lanes)

        @pl.loop(0, per_worker, step=lanes)
        def _(j):
            src_v[pl.ds(j, lanes)] = (base + j + lane) & (TOKENS - 1)

        @pl.loop(0, n_assign, step=scan)
        def _(a0):
            pltpu.sync_copy(pos_hbm.at[pl.ds(a0, scan)], pos_v)

            @pl.loop(0, scan, step=lanes)
            def _(j):
                rel = pos_v[pl.ds(j, lanes)] - base
                mine = (rel >= 0) & (rel < per_worker)
                tok = (a0 + j + lane) & (TOKENS - 1)
                plsc.store_scatter(src_v, [jnp.where(mine, rel, 0)], tok, mask=mine)

        _sc_gather_loop(table_hbm, out_hbm, src_v, base, chunks, buf0, buf1, sem0, sem1)

    return dispatch(table, pos_flat)


def _sc_gather_rows(table, idx):
    cores, subcores, _ = _sc_workers()
    workers = cores * subcores
    n = idx.shape[0]
    window = SC_GATHER_WINDOW
    per_worker = n // workers
    chunks = per_worker // window
    assert per_worker * workers == n and chunks * window == per_worker and chunks % 2 == 0
    row_shape = table.shape[1:]
    mesh = plsc.VectorSubcoreMesh(core_axis_name="core", subcore_axis_name="subcore")

    @functools.partial(
        pl.kernel,
        out_type=jax.ShapeDtypeStruct((n,) + row_shape, table.dtype),
        mesh=mesh,
        scratch_types=[
            pltpu.VMEM((per_worker,), I32),
            pltpu.VMEM((window,) + row_shape, table.dtype),
            pltpu.VMEM((window,) + row_shape, table.dtype),
            pltpu.SemaphoreType.DMA,
            pltpu.SemaphoreType.DMA,
        ],
        name="sc_gather_rows",
    )
    def gather(table_hbm, idx_hbm, out_hbm, idx_v, buf0, buf1, sem0, sem1):
        wid = lax.axis_index("subcore") * cores + lax.axis_index("core")
        base = wid * per_worker
        pltpu.sync_copy(idx_hbm.at[pl.ds(base, per_worker)], idx_v)
        _sc_gather_loop(table_hbm, out_hbm, idx_v, base, chunks, buf0, buf1, sem0, sem1)

    return gather(table, idx)


def _experts_body(be_ref, nxt_ref, nu_ref, xs_ref, bgu_ref, bd_ref, wgu_hbm, wd_hbm, ys_ref,
                  wgu_f32, wd_f32, wgu_bf, wd_bf, sems):
    step = pl.program_id(0)
    bm = MOE_BM

    def fetch(e):
        return (pltpu.make_async_copy(wgu_hbm.at[e], wgu_f32, sems.at[0]),
                pltpu.make_async_copy(wd_hbm.at[e], wd_f32, sems.at[1]))

    @pl.when(step == 0)
    def _():
        for cp in fetch(be_ref[0]):
            cp.start()

    for sub in range(MOE_SUB):
        i = step * MOE_SUB + sub
        e = be_ref[i]
        used = i < nu_ref[0]
        fresh = (i == 0) | (e != be_ref[jnp.maximum(i - 1, 0)])
        rows = pl.ds(sub * bm * PACK_SUB, bm * PACK_SUB)
        xs_sub = xs_ref.at[rows, :]
        ys_sub = ys_ref.at[rows, :]

        @pl.when(used & fresh)
        def _():
            for cp in fetch(e):
                cp.wait()
            wgu_bf[...] = wgu_f32[...].astype(BF16)
            wd_bf[...] = wd_f32[...].astype(BF16)

            @pl.when(nxt_ref[i] >= 0)
            def _():
                for cp in fetch(nxt_ref[i]):
                    cp.start()

        @pl.when(used)
        def _():
            x = jnp.concatenate(_unpack_rows(xs_sub, bm), axis=1).astype(BF16)
            gu = jnp.dot(x, wgu_bf[...], preferred_element_type=F32) + bgu_ref[pl.ds(e, 1), :]
            gate = jnp.minimum(gu[:, 0:D_FF], SWIGLU_LIMIT)
            up = jnp.clip(gu[:, D_FF:2 * D_FF], -SWIGLU_LIMIT, SWIGLU_LIMIT)
            act = ((up + 1.0) * gate * jax.nn.sigmoid(SWIGLU_ALPHA * gate)).astype(BF16)
            y = jnp.dot(act, wd_bf[...], preferred_element_type=F32) + bd_ref[pl.ds(e, 1), :]
            _pack_rows(ys_sub, y.astype(BF16).astype(F32))

        @pl.when(jnp.logical_not(used))
        def _():
            ys_sub[...] = jnp.zeros((bm * PACK_SUB, LANES), I32)


def _experts(block_expert, block_next, n_used, xs, w_gate_up, b_gate_up, w_down, b_down):
    rows = MOE_SUB * MOE_BM * PACK_SUB
    d = D_MODEL

    def last_used_step(i, be, nx, nu):
        return jnp.minimum(i, (nu[0] - 1) // MOE_SUB)

    grid_spec = pltpu.PrefetchScalarGridSpec(
        num_scalar_prefetch=3,
        grid=(MOE_NB // MOE_SUB,),
        in_specs=[
            pl.BlockSpec((rows, LANES), lambda i, be, nx, nu: (last_used_step(i, be, nx, nu), 0)),
            pl.BlockSpec(b_gate_up.shape, lambda i, be, nx, nu: (0, 0)),
            pl.BlockSpec(b_down.shape, lambda i, be, nx, nu: (0, 0)),
            pl.BlockSpec(memory_space=pl.ANY),
            pl.BlockSpec(memory_space=pl.ANY),
        ],
        out_specs=pl.BlockSpec((rows, LANES), lambda i, be, nx, nu: (i, 0)),
        scratch_shapes=[
            pltpu.VMEM((d, 2 * D_FF), F32), pltpu.VMEM((D_FF, d), F32),
            pltpu.VMEM((d, 2 * D_FF), BF16), pltpu.VMEM((D_FF, d), BF16),
            pltpu.SemaphoreType.DMA((2,)),
        ],
    )
    return pl.pallas_call(
        _experts_body,
        grid_spec=grid_spec,
        out_shape=jax.ShapeDtypeStruct((MOE_ROWS * PACK_SUB, LANES), I32),
        compiler_params=_params(("arbitrary",), 48),
        name="experts",
    )(block_expert, block_next, n_used, xs, b_gate_up, b_down, w_gate_up, w_down)


def _combine_body(gates_ref, x1_ref, mod_ref, gpost_ref, y0_ref, y1_ref, y2_ref, y3_ref, o_ref):
    tm = TM_COMB
    gates = gates_ref[...]
    y_hi = jnp.zeros((tm, PACK_COLS), F32)
    y_lo = jnp.zeros((tm, PACK_COLS), F32)
    for k, yk_ref in enumerate((y0_ref, y1_ref, y2_ref, y3_ref)):
        hi, lo = _unpack_rows(yk_ref, tm)
        y_hi = y_hi + hi * gates[:, k:k + 1]
        y_lo = y_lo + lo * gates[:, k:k + 1]
    y = jnp.concatenate([y_hi, y_lo], axis=1)
    gate2 = mod_ref[5:6, :]
    o_ref[...] = x1_ref[...] + gate2 * _rms(y, gpost_ref[...])


def _combine(gates_t, x1, mod, g_post, y4):
    t, d = x1.shape
    tm = TM_COMB
    tiles = t // tm
    tiles_per_seq = SEQ // tm

    def slab(k):
        return pl.BlockSpec((tm * PACK_SUB, LANES), lambda i: (k * tiles + i, 0))

    return pl.pallas_call(
        _combine_body,
        grid=(tiles,),
        in_specs=[
            pl.BlockSpec((tm, TOP_K), lambda i: (i, 0)),
            pl.BlockSpec((tm, d), lambda i: (i, 0)),
            pl.BlockSpec((None, 6, d), lambda i: (i // tiles_per_seq, 0, 0)),
            pl.BlockSpec(g_post.shape, lambda i: (0, 0)),
            slab(0), slab(1), slab(2), slab(3),
        ],
        out_specs=pl.BlockSpec((tm, d), lambda i: (i, 0)),
        out_shape=jax.ShapeDtypeStruct((t, d), F32),
        compiler_params=_params(("arbitrary",), 48),
        name="combine",
    )(gates_t, x1, mod, g_post, y4, y4, y4, y4)


def _rotary_tables():
    half = ROT_DIM // 2
    inv_freq = ROPE_THETA ** (-2.0 * jnp.arange(half, dtype=F32) / ROT_DIM)
    ang = jnp.arange(SEQ).astype(F32)[:, None] * inv_freq[None, :]
    cos, sin = jnp.cos(ang), jnp.sin(ang)
    ones = jnp.ones((SEQ, ATT_HEAD_DIM - ROT_DIM), F32)
    zeros = jnp.zeros((SEQ, ATT_HEAD_DIM - ROT_DIM), F32)
    zh = jnp.zeros((SEQ, half), F32)
    reps = LANES // ATT_HEAD_DIM
    rc = jnp.tile(jnp.concatenate([cos, cos, ones], axis=1), (1, reps))
    rm = jnp.tile(jnp.concatenate([-sin, zh, zeros], axis=1), (1, reps))
    rp = jnp.tile(jnp.concatenate([zh, sin, zeros], axis=1), (1, reps))
    return rc, rm, rp


def _mixer_inputs(w_in, w_gk_fwd, b_gk_fwd, w_gk_bwd, b_gk_bwd):
    hk = GLA_HEADS * GLA_DK
    hv = GLA_HEADS * GLA_DV
    w = w_in[0]
    o_lr = 2 * hk + 2 * hv
    o_aq = o_lr + 2 * GLA_RANK
    o_ak = o_aq + ATT_Q_HEADS * ATT_HEAD_DIM
    o_av = o_ak + ATT_KV_HEADS * ATT_HEAD_DIM
    hd = ATT_HEAD_DIM
    wa = w[:, :o_lr].astype(BF16)
    wlr = w[:, o_lr:o_aq].astype(BF16)
    dup = lambda m: jnp.concatenate([m[:, g * hd:(g + 1) * hd] for g in range(ATT_KV_HEADS) for _ in range(2)], axis=1)
    wb = jnp.concatenate([w[:, o_aq:o_ak], dup(w[:, o_ak:o_av])], axis=1).astype(BF16)
    wvt = dup(w[:, o_av:o_av + ATT_KV_HEADS * hd]).T.astype(BF16)
    zr = jnp.zeros((GLA_RANK, hk), F32)
    wgk = jnp.concatenate([jnp.concatenate([w_gk_fwd[0], zr], axis=1),
                           jnp.concatenate([zr, w_gk_bwd[0]], axis=1)], axis=0).astype(BF16)
    bgk = jnp.concatenate([b_gk_fwd[0], b_gk_bwd[0]])[None, :]
    return (wa, wlr, wgk, bgk, wb, wvt) + _rotary_tables()


def kernel(x, c, w_ada, b_ada, g_pre_mix, g_post_mix, w_in, w_gk_fwd, b_gk_fwd, w_gk_bwd, b_gk_bwd, g_gla_out,
           attn_sink, w_out, g_pre_ffn, g_post_ffn, w_router, b_router, w_gate_up, b_gate_up, w_down, b_down):
    assert x.shape == (BATCH, SEQ, D_MODEL) and w_ada.shape[0] == 1
    d = D_MODEL
    x2 = x.reshape(TOKENS, d)

    c_pad = jnp.pad(c, ((0, SUBLANES - BATCH), (0, 0)))
    mod = _ada(c_pad, w_ada[0], b_ada)[:BATCH].reshape(BATCH, 6, d)

    mixer_in = _mixer_inputs(w_in, w_gk_fwd, b_gk_fwd, w_gk_bwd, b_gk_bwd)
    q, k, v, gg, laf, lab, aq, ak2, avt = _inproj(x2, mod, g_pre_mix, *mixer_in)
    o_f, o_b = _gla(q, k, v, laf, lab)
    o_att = _attn(attn_sink[0], aq, ak2, avt)

    wr_t = w_router[0].T
    wrh = wr_t.astype(BF16)
    wrl = (wr_t - wrh.astype(F32)).astype(BF16)
    x1, h2_tiles, top_i, gates, rank, counts = _post(
        o_f, o_b, gg, o_att, x2, mod, g_gla_out, g_post_mix, g_pre_ffn, w_out[0].astype(BF16), wrh, wrl,
        b_router[0][:, None])

    pos, blocks = _route(top_i, rank, counts)
    pos_flat = pos.reshape(TOP_K * TOKENS)
    xs = _sc_dispatch_rows(h2_tiles.reshape(TOKENS, PACK_SUB, LANES), pos_flat, MOE_ROWS)
    xs = xs.reshape(MOE_ROWS * PACK_SUB, LANES)
    ys = _experts(blocks[0, :MOE_NB], blocks[1, :MOE_NB], blocks[2, :1], xs,
                  w_gate_up[0], b_gate_up[0], w_down[0], b_down[0])
    y4 = _sc_gather_rows(ys.reshape(MOE_ROWS, PACK_SUB, LANES), pos_flat)
    out = _combine(gates.T, x1, mod, g_post_ffn, y4.reshape(TOP_K * TOKENS * PACK_SUB, LANES))
    return out.reshape(BATCH, SEQ, d)
```

```python
import functools

import jax
import jax.numpy as jnp
import numpy as np
from jax import lax
from jax.experimental import pallas as pl
from jax.experimental.pallas import tpu as pltpu
from jax.experimental.pallas import tpu_sc as plsc

F32 = jnp.float32
BF16 = jnp.bfloat16
I32 = jnp.int32

D_MODEL = 1024
BATCH = 2
SEQ = 8192
TOKENS = BATCH * SEQ
GLA_HEADS = 4
GLA_DV = 128
GLA_DK = 64
GLA_RANK = 16
GLA_GATE_NORMALIZER = 16.0
GLA_CHUNK = 64
ATT_Q_HEADS = 8
ATT_KV_HEADS = 2
ATT_HEAD_DIM = 64
ATT_WINDOW = 128
ATT_BLOCK = 128
ROT_DIM = 16
ROPE_THETA = 500000.0
N_EXPERTS = 32
TOP_K = 4
D_FF = 1024
SWIGLU_LIMIT = 7.0
SWIGLU_ALPHA = 1.702
NORM_EPS = 1e-6
NEG_INF = -1e30

LANES = 128
SUBLANES = 8
PACK_COLS = D_MODEL // 2
PACK_SUB = PACK_COLS // LANES

TM_IN = 512
GLA_GROUP = 4
ATT_GROUP = 4
TM_POST = 512
MOE_BM = 256
MOE_ROWS = TOKENS * TOP_K + N_EXPERTS * MOE_BM
MOE_NB = MOE_ROWS // MOE_BM
MOE_SUB = 2
MOE_NB_PAD = ((MOE_NB + LANES - 1) // LANES) * LANES
SC_SCAN_CHUNK = 4096
TM_COMB = 256
SC_GATHER_WINDOW = 64

NT_DIMS = (((1,), (1,)), ((), ()))
TN_DIMS = (((0,), (0,)), ((), ()))


def _params(semantics, vmem_mib):
    return pltpu.CompilerParams(dimension_semantics=semantics, vmem_limit_bytes=vmem_mib * 1024 * 1024)


def _rms(x, g):
    return x * lax.rsqrt(jnp.mean(x * x, axis=-1, keepdims=True) + NORM_EPS) * g


def _silu(x):
    return x * jax.nn.sigmoid(x)


def _pack_rows(ref, v):
    m = v.shape[0]
    bits = lax.bitcast_convert_type(v, jnp.uint32)
    word = lax.bitcast_convert_type(bits[:, :PACK_COLS] | (bits[:, PACK_COLS:] >> 16), I32)
    for s in range(PACK_SUB):
        ref[pl.ds(s, m, stride=PACK_SUB), :] = word[:, s * LANES:(s + 1) * LANES]


def _unpack_rows(ref, m):
    word = jnp.concatenate([ref[pl.ds(s, m, stride=PACK_SUB), :] for s in range(PACK_SUB)], axis=1)
    bits = lax.bitcast_convert_type(word, jnp.uint32)
    hi = lax.bitcast_convert_type(bits & jnp.uint32(0xFFFF0000), F32)
    lo = lax.bitcast_convert_type(bits << 16, F32)
    return hi, lo


def _ada_body(c_ref, w_ref, b_ref, o_ref):
    ca = _silu(c_ref[...]).astype(BF16)
    o_ref[...] = jnp.dot(ca, w_ref[...].astype(BF16), preferred_element_type=F32) + b_ref[...]


def _ada(c_pad, w_ada, b_ada):
    d = D_MODEL
    return pl.pallas_call(
        _ada_body,
        grid=(6,),
        in_specs=[
            pl.BlockSpec((SUBLANES, d), lambda j: (0, 0)),
            pl.BlockSpec((d, d), lambda j: (0, j)),
            pl.BlockSpec((1, d), lambda j: (0, j)),
        ],
        out_specs=pl.BlockSpec((SUBLANES, d), lambda j: (0, j)),
        out_shape=jax.ShapeDtypeStruct((SUBLANES, 6 * d), F32),
        compiler_params=_params(("arbitrary",), 32),
        name="ada",
    )(c_pad, w_ada, b_ada)


def _rotary(x, cos_t, msin_t, psin_t):
    width = x.shape[1]
    reps = width // LANES
    c = jnp.concatenate([cos_t] * reps, axis=1)
    m = jnp.concatenate([msin_t] * reps, axis=1)
    p = jnp.concatenate([psin_t] * reps, axis=1)
    half = ROT_DIM // 2
    return x * c + pltpu.roll(x, width - half, 1) * m + pltpu.roll(x, half, 1) * p


def _inproj_body(x_ref, mod_ref, g_ref, wa_ref, wlr_ref, wgk_ref, bgk_ref, wb_ref, wvt_ref, rc_ref, rm_ref, rp_ref,
                 q_ref, k_ref, v_ref, gg_ref, laf_ref, lab_ref, aq_ref, ak_ref, avt_ref):
    shift = mod_ref[0:1, :]
    scale = mod_ref[1:2, :]
    h = (_rms(x_ref[...], g_ref[...]) * (1.0 + scale) + shift).astype(BF16)

    hk = GLA_HEADS * GLA_DK
    hv = GLA_HEADS * GLA_DV
    pa = jnp.dot(h, wa_ref[...], preferred_element_type=F32)
    q_ref[...] = pa[:, 0:hk] * (GLA_DK ** -0.5)
    k_ref[...] = pa[:, hk:2 * hk]
    v_ref[...] = pa[:, 2 * hk:2 * hk + hv].astype(BF16)
    gg_ref[...] = pa[:, 2 * hk + hv:2 * hk + 2 * hv]

    plr = jnp.dot(h, wlr_ref[...], preferred_element_type=F32)
    gk = jnp.dot(plr.astype(BF16), wgk_ref[...], preferred_element_type=F32) + bgk_ref[...]
    la = (jnp.minimum(gk, 0.0) - jnp.log1p(jnp.exp(-jnp.abs(gk)))) * (1.0 / GLA_GATE_NORMALIZER)
    laf_ref[...] = la[:, 0:hk]
    lab_ref[...] = la[:, hk:2 * hk]

    pb = jnp.dot(h, wb_ref[...], preferred_element_type=F32)
    nq = ATT_Q_HEADS * ATT_HEAD_DIM
    nk = 2 * ATT_KV_HEADS * ATT_HEAD_DIM
    rc, rm, rp = rc_ref[...], rm_ref[...], rp_ref[...]
    aq_ref[...] = (_rotary(pb[:, 0:nq], rc, rm, rp) * (ATT_HEAD_DIM ** -0.5)).astype(BF16)
    ak_ref[...] = _rotary(pb[:, nq:nq + nk], rc, rm, rp).astype(BF16)
    avt_ref[...] = lax.dot_general(wvt_ref[...], h, NT_DIMS, preferred_element_type=F32).astype(BF16)


def _inproj(x2, mod, g_pre, wa, wlr, wgk, bgk, wb, wvt, rc, rm, rp):
    t, d = x2.shape
    tm = TM_IN
    tiles_per_seq = SEQ // tm
    hk = GLA_HEADS * GLA_DK
    hv = GLA_HEADS * GLA_DV
    nq = ATT_Q_HEADS * ATT_HEAD_DIM
    nk = 2 * ATT_KV_HEADS * ATT_HEAD_DIM

    def full(a):
        return pl.BlockSpec(a.shape, lambda i: (0,) * a.ndim)

    def rows(w):
        return pl.BlockSpec((tm, w), lambda i: (i, 0))

    def table():
        return pl.BlockSpec((tm, LANES), lambda i: (i % tiles_per_seq, 0))

    out_widths = [(hk, F32), (hk, F32), (hv, BF16), (hv, F32), (hk, F32), (hk, F32), (nq, BF16), (nk, BF16)]
    return pl.pallas_call(
        _inproj_body,
        grid=(t // tm,),
        in_specs=[
            rows(d),
            pl.BlockSpec((None, 6, d), lambda i: (i // tiles_per_seq, 0, 0)),
            full(g_pre), full(wa), full(wlr), full(wgk), full(bgk), full(wb), full(wvt),
            table(), table(), table(),
        ],
        out_specs=[rows(w) for w, _ in out_widths] + [pl.BlockSpec((nk, tm), lambda i: (0, i))],
        out_shape=[jax.ShapeDtypeStruct((t, w), dt) for w, dt in out_widths] + [jax.ShapeDtypeStruct((nk, t), BF16)],
        compiler_params=_params(("arbitrary",), 56),
        name="inproj",
    )(x2, mod, g_pre, wa, wlr, wgk, bgk, wb, wvt, rc, rm, rp)


def _gla_chunk(q, k, la, v_ref, rows, o_ref, states, cum, tri, i_last, i_mid, head_masks):
    hi = la.astype(BF16)
    lo = (la - hi.astype(F32)).astype(BF16)
    b = jnp.dot(cum, hi, preferred_element_type=F32) + jnp.dot(cum, lo, preferred_element_type=F32)
    b_last = b[i_last:i_last + 1, :]
    b_mid = b[i_mid:i_mid + 1, :]
    qs = q * jnp.exp(b - b_mid)
    ks = k * jnp.exp(b_mid - b)
    qi = q * jnp.exp(b)
    kst = k * jnp.exp(b_last - b)
    decay = jnp.exp(b_last)
    new_states = []
    for h in range(GLA_HEADS):
        pair = slice((h // 2) * LANES, (h // 2 + 1) * LANES)
        mask = head_masks[h % 2]
        vcols = slice(h * GLA_DV, (h + 1) * GLA_DV)
        qs_h = jnp.where(mask, qs[:, pair], 0.0).astype(BF16)
        sc = lax.dot_general(qs_h, ks[:, pair].astype(BF16), NT_DIMS, preferred_element_type=F32)
        sc = jnp.where(tri, sc, 0.0)
        v_h = v_ref[rows, vcols]
        st = states[h]
        qi_h = jnp.where(mask, qi[:, pair], 0.0).astype(BF16)
        o = jnp.dot(sc.astype(BF16), v_h, preferred_element_type=F32)
        o = o + lax.dot_general(qi_h, st.astype(BF16), NT_DIMS, preferred_element_type=F32)
        o_ref[rows, vcols] = o
        kv = lax.dot_general(v_h, kst[:, pair].astype(BF16), TN_DIMS, preferred_element_type=F32)
        new_states.append(st * decay[:, pair] + kv)
    return new_states


def _gla_body(qf_ref, kf_ref, vf_ref, laf_ref, qb_ref, kb_ref, vb_ref, lab_ref, of_ref, ob_ref, sf_ref, sb_ref):
    @pl.when(pl.program_id(1) == 0)
    def _():
        sf_ref[...] = jnp.zeros_like(sf_ref)
        sb_ref[...] = jnp.zeros_like(sb_ref)

    c = GLA_CHUNK
    r_i = lax.broadcasted_iota(I32, (c, c), 0)
    c_i = lax.broadcasted_iota(I32, (c, c), 1)
    lower = c_i <= r_i
    upper = c_i >= r_i
    cum_f = jnp.where(lower, 1.0, 0.0).astype(BF16)
    cum_b = jnp.where(upper, 1.0, 0.0).astype(BF16)
    lane = lax.broadcasted_iota(I32, (1, LANES), 1)
    head_masks = (lane < GLA_DK, lane >= GLA_DK)

    st_f = [sf_ref[h] for h in range(GLA_HEADS)]
    st_b = [sb_ref[h] for h in range(GLA_HEADS)]
    for g in range(GLA_GROUP):
        rows_f = slice(g * c, (g + 1) * c)
        gb = GLA_GROUP - 1 - g
        rows_b = slice(gb * c, (gb + 1) * c)
        st_f = _gla_chunk(qf_ref[rows_f, :], kf_ref[rows_f, :], laf_ref[rows_f, :], vf_ref, rows_f, of_ref,
                          st_f, cum_f, lower, c - 1, c // 2 - 1, head_masks)
        st_b = _gla_chunk(qb_ref[rows_b, :], kb_ref[rows_b, :], lab_ref[rows_b, :], vb_ref, rows_b, ob_ref,
                          st_b, cum_b, upper, 0, c // 2, head_masks)
    for h in range(GLA_HEADS):
        sf_ref[h] = st_f[h]
        sb_ref[h] = st_b[h]


def _gla(q, k, v, laf, lab):
    t = q.shape[0]
    rows = GLA_GROUP * GLA_CHUNK
    ng = SEQ // rows
    hk = GLA_HEADS * GLA_DK
    hv = GLA_HEADS * GLA_DV

    def fwd(w):
        return pl.BlockSpec((rows, w), lambda b, n: (b * ng + n, 0))

    def bwd(w):
        return pl.BlockSpec((rows, w), lambda b, n: (b * ng + ng - 1 - n, 0))

    return pl.pallas_call(
        _gla_body,
        grid=(BATCH, ng),
        in_specs=[fwd(hk), fwd(hk), fwd(hv), fwd(hk), bwd(hk), bwd(hk), bwd(hv), bwd(hk)],
        out_specs=[fwd(hv), bwd(hv)],
        out_shape=[jax.ShapeDtypeStruct((t, hv), F32)] * 2,
        scratch_shapes=[pltpu.VMEM((GLA_HEADS, GLA_DV, 2 * GLA_DK), F32)] * 2,
        compiler_params=_params(("arbitrary", "arbitrary"), 32),
        name="gla",
    )(q, k, v, laf, q, k, v, lab)


def _attn_body(sink_ref, q_ref, kp_ref, kc_ref, kn_ref, vp_ref, vc_ref, vn_ref, o_ref):
    step = pl.program_id(1)
    last = pl.num_programs(1) - 1
    qb = ATT_BLOCK
    hd = ATT_HEAD_DIM
    k_all = jnp.concatenate([kp_ref[...], kc_ref[...], kn_ref[...]], axis=0)
    vt_all = jnp.concatenate([vp_ref[...], vc_ref[...], vn_ref[...]], axis=1)
    lane = lax.broadcasted_iota(I32, (1, LANES), 1)
    lo = lane < hd
    j_k = lax.broadcasted_iota(I32, (3 * qb, qb), 0)
    i_q = lax.broadcasted_iota(I32, (3 * qb, qb), 1)
    band = jnp.abs(j_k - qb - i_q) <= ATT_WINDOW
    sinks = [jnp.concatenate([jnp.full((1, qb), sink_ref[4 * g + r], F32) for r in range(4)], axis=1)
             for g in range(ATT_KV_HEADS)]
    for j in range(ATT_GROUP):
        valid = band
        if j == 0:
            valid = valid & ((j_k >= qb) | (step > 0))
        if j == ATT_GROUP - 1:
            valid = valid & ((j_k < 2 * qb) | (step < last))
        valid4 = jnp.concatenate([valid] * 4, axis=1)
        keys = slice(j * qb, (j + 3) * qb)
        rows = slice(j * qb, (j + 1) * qb)
        for g in range(ATT_KV_HEADS):
            kg = k_all[keys, g * LANES:(g + 1) * LANES]
            vgt = vt_all[g * LANES:(g + 1) * LANES, keys]
            qa = q_ref[rows, (2 * g) * LANES:(2 * g + 1) * LANES]
            qc = q_ref[rows, (2 * g + 1) * LANES:(2 * g + 2) * LANES]
            zero = jnp.zeros_like(qa)
            lhs = jnp.concatenate([jnp.where(lo, qa, zero), jnp.where(lo, zero, qa),
                                   jnp.where(lo, qc, zero), jnp.where(lo, zero, qc)], axis=0)
            st = lax.dot_general(kg, lhs, NT_DIMS, preferred_element_type=F32)
            st = jnp.where(valid4, st, NEG_INF)
            sink = sinks[g]
            m = jnp.maximum(jnp.max(st, axis=0, keepdims=True), sink)
            p = jnp.exp(st - m)
            inv = 1.0 / (jnp.sum(p, axis=0, keepdims=True) + jnp.exp(sink - m))
            ot = jnp.dot(vgt, p.astype(BF16), preferred_element_type=F32) * inv
            pair_a = jnp.concatenate([ot[0:hd, 0:qb], ot[hd:2 * hd, qb:2 * qb]], axis=0)
            pair_c = jnp.concatenate([ot[0:hd, 2 * qb:3 * qb], ot[hd:2 * hd, 3 * qb:4 * qb]], axis=0)
            o_ref[rows, (2 * g) * LANES:(2 * g + 1) * LANES] = pair_a.T.astype(o_ref.dtype)
            o_ref[rows, (2 * g + 1) * LANES:(2 * g + 2) * LANES] = pair_c.T.astype(o_ref.dtype)


def _attn(sink, aq, ak2, avt):
    t = aq.shape[0]
    qb = ATT_BLOCK
    nb = SEQ // qb
    steps = nb // ATT_GROUP
    nq = ATT_Q_HEADS * ATT_HEAD_DIM
    nk = 2 * ATT_KV_HEADS * ATT_HEAD_DIM

    def edge_block(b, n, shift):
        return b * nb + jnp.clip(n * ATT_GROUP + shift, 0, nb - 1)

    def k_edge(shift):
        return pl.BlockSpec((qb, nk), lambda b, n: (edge_block(b, n, shift), 0))

    def v_edge(shift):
        return pl.BlockSpec((nk, qb), lambda b, n: (0, edge_block(b, n, shift)))

    def group(w):
        return pl.BlockSpec((ATT_GROUP * qb, w), lambda b, n: (b * steps + n, 0))

    v_group = pl.BlockSpec((nk, ATT_GROUP * qb), lambda b, n: (0, b * steps + n))
    return pl.pallas_call(
        _attn_body,
        grid=(BATCH, steps),
        in_specs=[
            pl.BlockSpec(memory_space=pltpu.SMEM),
            group(nq),
            k_edge(-1), group(nk), k_edge(ATT_GROUP), v_edge(-1), v_group, v_edge(ATT_GROUP),
        ],
        out_specs=group(nq),
        out_shape=jax.ShapeDtypeStruct((t, nq), BF16),
        compiler_params=_params(("arbitrary", "arbitrary"), 48),
        name="attn",
    )(sink, aq, ak2, ak2, ak2, avt, avt, avt)


def _post_body(of_ref, ob_ref, gg_ref, oa_ref, x_ref, mod_ref, ggla_ref, gpm_ref, gpf_ref, wout_ref,
               wrh_ref, wrl_ref, br_ref,
               x1_ref, h2_ref, ti_ref, gt_ref, rk_ref, cnt_ref, base_ref):
    tm = TM_POST

    @pl.when(pl.program_id(0) == 0)
    def _():
        base_ref[...] = jnp.zeros_like(base_ref)

    og = of_ref[...] + ob_ref[...]
    gg = gg_ref[...]
    parts = []
    for h in range(GLA_HEADS):
        cols = slice(h * GLA_DV, (h + 1) * GLA_DV)
        parts.append((_rms(og[:, cols], ggla_ref[...]) * _silu(gg[:, cols])).astype(BF16))
    o = jnp.concatenate(parts + [oa_ref[...]], axis=1)
    y = jnp.dot(o, wout_ref[...], preferred_element_type=F32)

    gate1 = mod_ref[2:3, :]
    shift2 = mod_ref[3:4, :]
    scale2 = mod_ref[4:5, :]
    x1 = x_ref[...] + gate1 * _rms(y, gpm_ref[...])
    x1_ref[...] = x1
    h2 = _rms(x1, gpf_ref[...]) * (1.0 + scale2) + shift2
    h2_hi = h2.astype(BF16)
    h2_hi32 = h2_hi.astype(F32)
    h2_lo = (h2 - h2_hi32).astype(BF16)
    _pack_rows(h2_ref, h2_hi32)

    wrh = wrh_ref[...]
    logits = (lax.dot_general(wrh, h2_hi, NT_DIMS, preferred_element_type=F32)
              + lax.dot_general(wrh, h2_lo, NT_DIMS, preferred_element_type=F32)
              + lax.dot_general(wrl_ref[...], h2_hi, NT_DIMS, preferred_element_type=F32)
              + br_ref[...])
    e_iota = lax.broadcasted_iota(I32, (N_EXPERTS, tm), 0)
    idxs, vals = [], []
    work = logits
    for _ in range(TOP_K):
        m = jnp.max(work, axis=0, keepdims=True)
        idx = jnp.min(jnp.where(work == m, e_iota, N_EXPERTS), axis=0, keepdims=True)
        idxs.append(idx)
        vals.append(m)
        work = jnp.where(e_iota == idx, -jnp.inf, work)
    exps = [jnp.exp(v - vals[0]) for v in vals]
    inv = 1.0 / (exps[0] + exps[1] + exps[2] + exps[3])
    gt_ref[...] = jnp.concatenate([e * inv for e in exps], axis=0)
    ti_ref[...] = jnp.concatenate(idxs, axis=0)

    onehots = [e_iota == idx for idx in idxs]
    member = jnp.where(onehots[0] | onehots[1] | onehots[2] | onehots[3], 1.0, 0.0)
    t_row = lax.broadcasted_iota(I32, (tm, tm), 0)
    t_col = lax.broadcasted_iota(I32, (tm, tm), 1)
    strict = jnp.where(t_row < t_col, 1.0, 0.0).astype(BF16)
    before = base_ref[...] + jnp.dot(member.astype(BF16), strict, preferred_element_type=F32)
    rk_ref[...] = jnp.concatenate(
        [jnp.sum(jnp.where(oh, before, 0.0), axis=0, keepdims=True) for oh in onehots], axis=0).astype(I32)
    new_base = base_ref[...] + jnp.sum(member, axis=1, keepdims=True)
    base_ref[...] = new_base
    cnt_ref[...] = jnp.broadcast_to(new_base, cnt_ref.shape)


def _post(o_f, o_b, gg, o_att, x2, mod, g_gla, g_pm, g_pf, wout, wrh, wrl, br):
    t, d = x2.shape
    tm = TM_POST
    tiles_per_seq = SEQ // tm
    hv = GLA_HEADS * GLA_DV

    def full(a):
        return pl.BlockSpec(a.shape, lambda i: (0,) * a.ndim)

    def rows(w):
        return pl.BlockSpec((tm, w), lambda i: (i, 0))

    def lanes():
        return pl.BlockSpec((TOP_K, tm), lambda i: (0, i))

    return pl.pallas_call(
        _post_body,
        grid=(t // tm,),
        in_specs=[
            rows(hv), rows(hv), rows(hv), rows(hv), rows(d),
            pl.BlockSpec((None, 6, d), lambda i: (i // tiles_per_seq, 0, 0)),
            full(g_gla), full(g_pm), full(g_pf), full(wout), full(wrh), full(wrl), full(br),
        ],
        out_specs=[
            rows(d),
            pl.BlockSpec((tm * PACK_SUB, LANES), lambda i: (i, 0)),
            lanes(), lanes(), lanes(),
            pl.BlockSpec((N_EXPERTS, LANES), lambda i: (0, 0)),
        ],
        out_shape=[
            jax.ShapeDtypeStruct((t, d), F32),
            jax.ShapeDtypeStruct((t * PACK_SUB, LANES), I32),
            jax.ShapeDtypeStruct((TOP_K, t), I32),
            jax.ShapeDtypeStruct((TOP_K, t), F32),
            jax.ShapeDtypeStruct((TOP_K, t), I32),
            jax.ShapeDtypeStruct((N_EXPERTS, LANES), F32),
        ],
        scratch_shapes=[pltpu.VMEM((N_EXPERTS, 1), F32)],
        compiler_params=_params(("arbitrary",), 48),
        name="post",
    )(o_f, o_b, gg, o_att, x2, mod, g_gla, g_pm, g_pf, wout, wrh, wrl, br)


def _route_body(ti_ref, rk_ref, cnt_ref, pos_ref, blk_ref):
    cnt = cnt_ref[...]
    padded = jnp.floor((cnt + (MOE_BM - 1)) * (1.0 / MOE_BM)) * MOE_BM
    starts, ends = [], []
    acc = jnp.zeros((1, LANES), F32)
    for e in range(N_EXPERTS):
        starts.append(acc)
        acc = acc + padded[e:e + 1, :]
        ends.append(acc)
    ti = ti_ref[...]
    off = jnp.zeros(ti.shape, F32)
    for e in range(N_EXPERTS):
        off = jnp.where(ti == e, starts[e][:, 0:1], off)
    pos_ref[...] = rk_ref[...] + off.astype(I32)

    def owner_of(row):
        n_le = jnp.zeros(row.shape, I32)
        for e in range(N_EXPERTS):
            n_le = n_le + jnp.where(ends[e][:, 0:1] <= row, 1, 0)
        return jnp.minimum(n_le, N_EXPERTS - 1)

    block_start = lax.broadcasted_iota(I32, (1, MOE_NB_PAD), 1).astype(F32) * MOE_BM
    owner = owner_of(block_start)
    nxt = jnp.zeros((1, MOE_NB_PAD), I32)
    for e in range(N_EXPERTS):
        end_e = ends[e][:, 0:1]
        nxt = jnp.where(owner == e, jnp.where(end_e < acc[:, 0:1], owner_of(end_e), -1), nxt)
    used = jnp.broadcast_to((acc[:, 0:1] * (1.0 / MOE_BM)).astype(I32), (1, MOE_NB_PAD))
    blk_ref[...] = jnp.concatenate([owner, nxt, used, jnp.zeros((SUBLANES - 3, MOE_NB_PAD), I32)], axis=0)


def _route(top_i, rank, counts):
    return pl.pallas_call(
        _route_body,
        out_shape=[
            jax.ShapeDtypeStruct(top_i.shape, I32),
            jax.ShapeDtypeStruct((SUBLANES, MOE_NB_PAD), I32),
        ],
        compiler_params=pltpu.CompilerParams(vmem_limit_bytes=32 * 1024 * 1024),
        name="route",
    )(top_i, rank, counts)


def _sc_workers():
    info = plsc.get_sparse_core_info()
    return info.num_cores, info.num_subcores, info.num_lanes


def _sc_gather_loop(table_hbm, out_hbm, idx_v, base, chunks, buf0, buf1, sem0, sem1):
    window = SC_GATHER_WINDOW

    def fetch(c, buf, sem):
        return pltpu.make_async_copy(table_hbm.at[idx_v.at[pl.ds(c * window, window)]], buf, sem)

    def flush(c, buf):
        pltpu.sync_copy(buf, out_hbm.at[pl.ds(base + c * window, window)])

    fetch(0, buf0, sem0).start()

    @pl.loop(0, chunks, step=2)
    def _(c):
        fetch(c + 1, buf1, sem1).start()
        fetch(c, buf0, sem0).wait()
        flush(c, buf0)

        @pl.when(c + 2 < chunks)
        def _():
            fetch(c + 2, buf0, sem0).start()

        fetch(c + 1, buf1, sem1).wait()
        flush(c + 1, buf1)


def _sc_dispatch_rows(table, pos_flat, n_rows):
    cores, subcores, lanes = _sc_workers()
    workers = cores * subcores
    window = SC_GATHER_WINDOW
    per_worker = n_rows // workers
    chunks = per_worker // window
    n_assign = pos_flat.shape[0]
    scan = SC_SCAN_CHUNK
    assert per_worker * workers == n_rows and chunks * window == per_worker and chunks % 2 == 0
    assert n_assign % scan == 0 and scan % lanes == 0 and per_worker % lanes == 0
    row_shape = table.shape[1:]
    mesh = plsc.VectorSubcoreMesh(core_axis_name="core", subcore_axis_name="subcore")

    @functools.partial(
        pl.kernel,
        out_type=jax.ShapeDtypeStruct((n_rows,) + row_shape, table.dtype),
        mesh=mesh,
        scratch_types=[
            pltpu.VMEM((per_worker,), I32),
            pltpu.VMEM((scan,), I32),
            pltpu.VMEM((window,) + row_shape, table.dtype),
            pltpu.VMEM((window,) + row_shape, table.dtype),
            pltpu.SemaphoreType.DMA,
            pltpu.SemaphoreType.DMA,
        ],
        compiler_params=pltpu.CompilerParams(needs_layout_passes=False),
        name="sc_dispatch_rows",
    )
    def dispatch(table_hbm, pos_hbm, out_hbm, src_v, pos_v, buf0, buf1, sem0, sem1):
        wid = lax.axis_index("subcore") * cores + lax.axis_index("core")
        base = wid * per_worker
        lane = lax.iota(I32, lanes)

        @pl.loop(0, per_worker, step=lanes)
        def _(j):
            src_v[pl.ds(j, lanes)] = (base + j + lane) & (TOKENS - 1)

        @pl.loop(0, n_assign, step=scan)
        def _(a0):
            pltpu.sync_copy(pos_hbm.at[pl.ds(a0, scan)], pos_v)

            @pl.loop(0, scan, step=lanes)
            def _(j):
                rel = pos_v[pl.ds(j, lanes)] - base
                mine = (rel >= 0) & (rel < per_worker)
                tok = (a0 + j + lane) & (TOKENS - 1)
                plsc.store_scatter(src_v, [jnp.where(mine, rel, 0)], tok, mask=mine)

        _sc_gather_loop(table_hbm, out_hbm, src_v, base, chunks, buf0, buf1, sem0, sem1)

    return dispatch(table, pos_flat)


def _sc_gather_rows(table, idx):
    cores, subcores, _ = _sc_workers()
    workers = cores * subcores
    n = idx.shape[0]
    window = SC_GATHER_WINDOW
    per_worker = n // workers
    chunks = per_worker // window
    assert per_worker * workers == n and chunks * window == per_worker and chunks % 2 == 0
    row_shape = table.shape[1:]
    mesh = plsc.VectorSubcoreMesh(core_axis_name="core", subcore_axis_name="subcore")

    @functools.partial(
        pl.kernel,
        out_type=jax.ShapeDtypeStruct((n,) + row_shape, table.dtype),
        mesh=mesh,
        scratch_types=[
            pltpu.VMEM((per_worker,), I32),
            pltpu.VMEM((window,) + row_shape, table.dtype),
            pltpu.VMEM((window,) + row_shape, table.dtype),
            pltpu.SemaphoreType.DMA,
            pltpu.SemaphoreType.DMA,
        ],
        name="sc_gather_rows",
    )
    def gather(table_hbm, idx_hbm, out_hbm, idx_v, buf0, buf1, sem0, sem1):
        wid = lax.axis_index("subcore") * cores + lax.axis_index("core")
        base = wid * per_worker
        pltpu.sync_copy(idx_hbm.at[pl.ds(base, per_worker)], idx_v)
        _sc_gather_loop(table_hbm, out_hbm, idx_v, base, chunks, buf0, buf1, sem0, sem1)

    return gather(table, idx)


def _experts_body(be_ref, nxt_ref, nu_ref, xs_ref, bgu_ref, bd_ref, wgu_hbm, wd_hbm, ys_ref,
                  wgu_f32, wd_f32, wgu_bf, wd_bf, sems):
    step = pl.program_id(0)
    bm = MOE_BM

    def fetch(e):
        return (pltpu.make_async_copy(wgu_hbm.at[e], wgu_f32, sems.at[0]),
                pltpu.make_async_copy(wd_hbm.at[e], wd_f32, sems.at[1]))

    @pl.when(step == 0)
    def _():
        for cp in fetch(be_ref[0]):
            cp.start()

    for sub in range(MOE_SUB):
        i = step * MOE_SUB + sub
        e = be_ref[i]
        used = i < nu_ref[0]
        fresh = (i == 0) | (e != be_ref[jnp.maximum(i - 1, 0)])
        rows = pl.ds(sub * bm * PACK_SUB, bm * PACK_SUB)
        xs_sub = xs_ref.at[rows, :]
        ys_sub = ys_ref.at[rows, :]

        @pl.when(used & fresh)
        def _():
            for cp in fetch(e):
                cp.wait()
            wgu_bf[...] = wgu_f32[...].astype(BF16)
            wd_bf[...] = wd_f32[...].astype(BF16)

            @pl.when(nxt_ref[i] >= 0)
            def _():
                for cp in fetch(nxt_ref[i]):
                    cp.start()

        @pl.when(used)
        def _():
            x = jnp.concatenate(_unpack_rows(xs_sub, bm), axis=1).astype(BF16)
            gu = jnp.dot(x, wgu_bf[...], preferred_element_type=F32) + bgu_ref[pl.ds(e, 1), :]
            gate = jnp.minimum(gu[:, 0:D_FF], SWIGLU_LIMIT)
            up = jnp.clip(gu[:, D_FF:2 * D_FF], -SWIGLU_LIMIT, SWIGLU_LIMIT)
            act = ((up + 1.0) * gate * jax.nn.sigmoid(SWIGLU_ALPHA * gate)).astype(BF16)
            y = jnp.dot(act, wd_bf[...], preferred_element_type=F32) + bd_ref[pl.ds(e, 1), :]
            _pack_rows(ys_sub, y.astype(BF16).astype(F32))

        @pl.when(jnp.logical_not(used))
        def _():
            ys_sub[...] = jnp.zeros((bm * PACK_SUB, LANES), I32)


def _experts(block_expert, block_next, n_used, xs, w_gate_up, b_gate_up, w_down, b_down):
    rows = MOE_SUB * MOE_BM * PACK_SUB
    d = D_MODEL

    def last_used_step(i, be, nx, nu):
        return jnp.minimum(i, (nu[0] - 1) // MOE_SUB)

    grid_spec = pltpu.PrefetchScalarGridSpec(
        num_scalar_prefetch=3,
        grid=(MOE_NB // MOE_SUB,),
        in_specs=[
            pl.BlockSpec((rows, LANES), lambda i, be, nx, nu: (last_used_step(i, be, nx, nu), 0)),
            pl.BlockSpec(b_gate_up.shape, lambda i, be, nx, nu: (0, 0)),
            pl.BlockSpec(b_down.shape, lambda i, be, nx, nu: (0, 0)),
            pl.BlockSpec(memory_space=pl.ANY),
            pl.BlockSpec(memory_space=pl.ANY),
        ],
        out_specs=pl.BlockSpec((rows, LANES), lambda i, be, nx, nu: (i, 0)),
        scratch_shapes=[
            pltpu.VMEM((d, 2 * D_FF), F32), pltpu.VMEM((D_FF, d), F32),
            pltpu.VMEM((d, 2 * D_FF), BF16), pltpu.VMEM((D_FF, d), BF16),
            pltpu.SemaphoreType.DMA((2,)),
        ],
    )
    return pl.pallas_call(
        _experts_body,
        grid_spec=grid_spec,
        out_shape=jax.ShapeDtypeStruct((MOE_ROWS * PACK_SUB, LANES), I32),
        compiler_params=_params(("arbitrary",), 48),
        name="experts",
    )(block_expert, block_next, n_used, xs, b_gate_up, b_down, w_gate_up, w_down)


def _combine_body(gates_ref, x1_ref, mod_ref, gpost_ref, y0_ref, y1_ref, y2_ref, y3_ref, o_ref):
    tm = TM_COMB
    gates = gates_ref[...]
    y_hi = jnp.zeros((tm, PACK_COLS), F32)
    y_lo = jnp.zeros((tm, PACK_COLS), F32)
    for k, yk_ref in enumerate((y0_ref, y1_ref, y2_ref, y3_ref)):
        hi, lo = _unpack_rows(yk_ref, tm)
        y_hi = y_hi + hi * gates[:, k:k + 1]
        y_lo = y_lo + lo * gates[:, k:k + 1]
    y = jnp.concatenate([y_hi, y_lo], axis=1)
    gate2 = mod_ref[5:6, :]
    o_ref[...] = x1_ref[...] + gate2 * _rms(y, gpost_ref[...])


def _combine(gates_t, x1, mod, g_post, y4):
    t, d = x1.shape
    tm = TM_COMB
    tiles = t // tm
    tiles_per_seq = SEQ // tm

    def slab(k):
        return pl.BlockSpec((tm * PACK_SUB, LANES), lambda i: (k * tiles + i, 0))

    return pl.pallas_call(
        _combine_body,
        grid=(tiles,),
        in_specs=[
            pl.BlockSpec((tm, TOP_K), lambda i: (i, 0)),
            pl.BlockSpec((tm, d), lambda i: (i, 0)),
            pl.BlockSpec((None, 6, d), lambda i: (i // tiles_per_seq, 0, 0)),
            pl.BlockSpec(g_post.shape, lambda i: (0, 0)),
            slab(0), slab(1), slab(2), slab(3),
        ],
        out_specs=pl.BlockSpec((tm, d), lambda i: (i, 0)),
        out_shape=jax.ShapeDtypeStruct((t, d), F32),
        compiler_params=_params(("arbitrary",), 48),
        name="combine",
    )(gates_t, x1, mod, g_post, y4, y4, y4, y4)


def _rotary_tables():
    half = ROT_DIM // 2
    inv_freq = ROPE_THETA ** (-2.0 * jnp.arange(half, dtype=F32) / ROT_DIM)
    ang = jnp.arange(SEQ).astype(F32)[:, None] * inv_freq[None, :]
    cos, sin = jnp.cos(ang), jnp.sin(ang)
    ones = jnp.ones((SEQ, ATT_HEAD_DIM - ROT_DIM), F32)
    zeros = jnp.zeros((SEQ, ATT_HEAD_DIM - ROT_DIM), F32)
    zh = jnp.zeros((SEQ, half), F32)
    reps = LANES // ATT_HEAD_DIM
    rc = jnp.tile(jnp.concatenate([cos, cos, ones], axis=1), (1, reps))
    rm = jnp.tile(jnp.concatenate([-sin, zh, zeros], axis=1), (1, reps))
    rp = jnp.tile(jnp.concatenate([zh, sin, zeros], axis=1), (1, reps))
    return rc, rm, rp


def _mixer_inputs(w_in, w_gk_fwd, b_gk_fwd, w_gk_bwd, b_gk_bwd):
    hk = GLA_HEADS * GLA_DK
    hv = GLA_HEADS * GLA_DV
    w = w_in[0]
    o_lr = 2 * hk + 2 * hv
    o_aq = o_lr + 2 * GLA_RANK
    o_ak = o_aq + ATT_Q_HEADS * ATT_HEAD_DIM
    o_av = o_ak + ATT_KV_HEADS * ATT_HEAD_DIM
    hd = ATT_HEAD_DIM
    wa = w[:, :o_lr].astype(BF16)
    wlr = w[:, o_lr:o_aq].astype(BF16)
    dup = lambda m: jnp.concatenate([m[:, g * hd:(g + 1) * hd] for g in range(ATT_KV_HEADS) for _ in range(2)], axis=1)
    wb = jnp.concatenate([w[:, o_aq:o_ak], dup(w[:, o_ak:o_av])], axis=1).astype(BF16)
    wvt = dup(w[:, o_av:o_av + ATT_KV_HEADS * hd]).T.astype(BF16)
    zr = jnp.zeros((GLA_RANK, hk), F32)
    wgk = jnp.concatenate([jnp.concatenate([w_gk_fwd[0], zr], axis=1),
                           jnp.concatenate([zr, w_gk_bwd[0]], axis=1)], axis=0).astype(BF16)
    bgk = jnp.concatenate([b_gk_fwd[0], b_gk_bwd[0]])[None, :]
    return (wa, wlr, wgk, bgk, wb, wvt) + _rotary_tables()


def kernel(x, c, w_ada, b_ada, g_pre_mix, g_post_mix, w_in, w_gk_fwd, b_gk_fwd, w_gk_bwd, b_gk_bwd, g_gla_out,
           attn_sink, w_out, g_pre_ffn, g_post_ffn, w_router, b_router, w_gate_up, b_gate_up, w_down, b_down):
    assert x.shape == (BATCH, SEQ, D_MODEL) and w_ada.shape[0] == 1
    d = D_MODEL
    x2 = x.reshape(TOKENS, d)

    c_pad = jnp.pad(c, ((0, SUBLANES - BATCH), (0, 0)))
    mod = _ada(c_pad, w_ada[0], b_ada)[:BATCH].reshape(BATCH, 6, d)

    mixer_in = _mixer_inputs(w_in, w_gk_fwd, b_gk_fwd, w_gk_bwd, b_gk_bwd)
    q, k, v, gg, laf, lab, aq, ak2, avt = _inproj(x2, mod, g_pre_mix, *mixer_in)
    o_f, o_b = _gla(q, k, v, laf, lab)
    o_att = _attn(attn_sink[0], aq, ak2, avt)

    wr_t = w_router[0].T
    wrh = wr_t.astype(BF16)
    wrl = (wr_t - wrh.astype(F32)).astype(BF16)
    x1, h2_tiles, top_i, gates, rank, counts = _post(
        o_f, o_b, gg, o_att, x2, mod, g_gla_out, g_post_mix, g_pre_ffn, w_out[0].astype(BF16), wrh, wrl,
        b_router[0][:, None])

    pos, blocks = _route(top_i, rank, counts)
    pos_flat = pos.reshape(TOP_K * TOKENS)
    xs = _sc_dispatch_rows(h2_tiles.reshape(TOKENS, PACK_SUB, LANES), pos_flat, MOE_ROWS)
    xs = xs.reshape(MOE_ROWS * PACK_SUB, LANES)
    ys = _experts(blocks[0, :MOE_NB], blocks[1, :MOE_NB], blocks[2, :1], xs,
                  w_gate_up[0], b_gate_up[0], w_down[0], b_down[0])
    y4 = _sc_gather_rows(ys.reshape(MOE_ROWS, PACK_SUB, LANES), pos_flat)
    out = _combine(gates.T, x1, mod, g_post_ffn, y4.reshape(TOP_K * TOKENS * PACK_SUB, LANES))
    return out.reshape(BATCH, SEQ, d)
```

```python
import functools

import jax
import jax.numpy as jnp
import numpy as np
from jax import lax
from jax.experimental import pallas as pl
from jax.experimental.pallas import tpu as pltpu
from jax.experimental.pallas import tpu_sc as plsc

F32 = jnp.float32
BF16 = jnp.bfloat16
I32 = jnp.int32

D_MODEL = 1024
BATCH = 2
SEQ = 8192
TOKENS = BATCH * SEQ
GLA_HEADS = 4
GLA_DV = 128
GLA_DK = 64
GLA_RANK = 16
GLA_GATE_NORMALIZER = 16.0
GLA_CHUNK = 64
ATT_Q_HEADS = 8
ATT_KV_HEADS = 2
ATT_HEAD_DIM = 64
ATT_WINDOW = 128
ATT_BLOCK = 128
ROT_DIM = 16
ROPE_THETA = 500000.0
N_EXPERTS = 32
TOP_K = 4
D_FF = 1024
SWIGLU_LIMIT = 7.0
SWIGLU_ALPHA = 1.702
NORM_EPS = 1e-6
NEG_INF = -1e30

LANES = 128
SUBLANES = 8
PACK_COLS = D_MODEL // 2
PACK_SUB = PACK_COLS // LANES

TM_IN = 512
GLA_GROUP = 8
ATT_GROUP = 8
TM_POST = 512
MOE_BM = 256
MOE_ROWS = TOKENS * TOP_K + N_EXPERTS * MOE_BM
MOE_NB = MOE_ROWS // MOE_BM
MOE_SUB = 2
MOE_NB_PAD = ((MOE_NB + LANES - 1) // LANES) * LANES
SC_SCAN_CHUNK = 4096
TM_COMB = 256
SC_GATHER_WINDOW = 64

NT_DIMS = (((1,), (1,)), ((), ()))
TN_DIMS = (((0,), (0,)), ((), ()))


def _params(semantics, vmem_mib):
    return pltpu.CompilerParams(dimension_semantics=semantics, vmem_limit_bytes=vmem_mib * 1024 * 1024)


def _rms(x, g):
    return x * lax.rsqrt(jnp.mean(x * x, axis=-1, keepdims=True) + NORM_EPS) * g


def _silu(x):
    return x * jax.nn.sigmoid(x)


def _pack_rows(ref, v):
    m = v.shape[0]
    bits = lax.bitcast_convert_type(v, jnp.uint32)
    word = lax.bitcast_convert_type(bits[:, :PACK_COLS] | (bits[:, PACK_COLS:] >> 16), I32)
    for s in range(PACK_SUB):
        ref[pl.ds(s, m, stride=PACK_SUB), :] = word[:, s * LANES:(s + 1) * LANES]


def _unpack_rows(ref, m):
    word = jnp.concatenate([ref[pl.ds(s, m, stride=PACK_SUB), :] for s in range(PACK_SUB)], axis=1)
    bits = lax.bitcast_convert_type(word, jnp.uint32)
    hi = lax.bitcast_convert_type(bits & jnp.uint32(0xFFFF0000), F32)
    lo = lax.bitcast_convert_type(bits << 16, F32)
    return hi, lo


def _ada_body(c_ref, w_ref, b_ref, o_ref):
    ca = _silu(c_ref[...]).astype(BF16)
    o_ref[...] = jnp.dot(ca, w_ref[...].astype(BF16), preferred_element_type=F32) + b_ref[...]


def _ada(c_pad, w_ada, b_ada):
    d = D_MODEL
    return pl.pallas_call(
        _ada_body,
        grid=(6,),
        in_specs=[
            pl.BlockSpec((SUBLANES, d), lambda j: (0, 0)),
            pl.BlockSpec((d, d), lambda j: (0, j)),
            pl.BlockSpec((1, d), lambda j: (0, j)),
        ],
        out_specs=pl.BlockSpec((SUBLANES, d), lambda j: (0, j)),
        out_shape=jax.ShapeDtypeStruct((SUBLANES, 6 * d), F32),
        compiler_params=_params(("arbitrary",), 32),
        name="ada",
    )(c_pad, w_ada, b_ada)


def _rotary(x, cos_t, msin_t, psin_t):
    width = x.shape[1]
    reps = width // LANES
    c = jnp.concatenate([cos_t] * reps, axis=1)
    m = jnp.concatenate([msin_t] * reps, axis=1)
    p = jnp.concatenate([psin_t] * reps, axis=1)
    half = ROT_DIM // 2
    return x * c + pltpu.roll(x, width - half, 1) * m + pltpu.roll(x, half, 1) * p


def _inproj_body(x_ref, mod_ref, g_ref, wa_ref, wlr_ref, wgk_ref, bgk_ref, wb_ref, wvt_ref, rc_ref, rm_ref, rp_ref,
                 q_ref, k_ref, v_ref, gg_ref, laf_ref, lab_ref, aq_ref, ak_ref, avt_ref):
    shift = mod_ref[0:1, :]
    scale = mod_ref[1:2, :]
    h = (_rms(x_ref[...], g_ref[...]) * (1.0 + scale) + shift).astype(BF16)

    hk = GLA_HEADS * GLA_DK
    hv = GLA_HEADS * GLA_DV
    pa = jnp.dot(h, wa_ref[...], preferred_element_type=F32)
    q_ref[...] = pa[:, 0:hk] * (GLA_DK ** -0.5)
    k_ref[...] = pa[:, hk:2 * hk]
    v_ref[...] = pa[:, 2 * hk:2 * hk + hv].astype(BF16)
    gg_ref[...] = pa[:, 2 * hk + hv:2 * hk + 2 * hv]

    plr = jnp.dot(h, wlr_ref[...], preferred_element_type=F32)
    gk = jnp.dot(plr.astype(BF16), wgk_ref[...], preferred_element_type=F32) + bgk_ref[...]
    la = (jnp.minimum(gk, 0.0) - jnp.log1p(jnp.exp(-jnp.abs(gk)))) * (1.0 / GLA_GATE_NORMALIZER)
    laf_ref[...] = la[:, 0:hk]
    lab_ref[...] = la[:, hk:2 * hk]

    pb = jnp.dot(h, wb_ref[...], preferred_element_type=F32)
    nq = ATT_Q_HEADS * ATT_HEAD_DIM
    nk = 2 * ATT_KV_HEADS * ATT_HEAD_DIM
    rc, rm, rp = rc_ref[...], rm_ref[...], rp_ref[...]
    aq_ref[...] = (_rotary(pb[:, 0:nq], rc, rm, rp) * (ATT_HEAD_DIM ** -0.5)).astype(BF16)
    ak_ref[...] = _rotary(pb[:, nq:nq + nk], rc, rm, rp).astype(BF16)
    avt_ref[...] = lax.dot_general(wvt_ref[...], h, NT_DIMS, preferred_element_type=F32).astype(BF16)


def _inproj(x2, mod, g_pre, wa, wlr, wgk, bgk, wb, wvt, rc, rm, rp):
    t, d = x2.shape
    tm = TM_IN
    tiles_per_seq = SEQ // tm
    hk = GLA_HEADS * GLA_DK
    hv = GLA_HEADS * GLA_DV
    nq = ATT_Q_HEADS * ATT_HEAD_DIM
    nk = 2 * ATT_KV_HEADS * ATT_HEAD_DIM

    def full(a):
        return pl.BlockSpec(a.shape, lambda i: (0,) * a.ndim)

    def rows(w):
        return pl.BlockSpec((tm, w), lambda i: (i, 0))

    def table():
        return pl.BlockSpec((tm, LANES), lambda i: (i % tiles_per_seq, 0))

    out_widths = [(hk, F32), (hk, F32), (hv, BF16), (hv, F32), (hk, F32), (hk, F32), (nq, BF16), (nk, BF16)]
    return pl.pallas_call(
        _inproj_body,
        grid=(t // tm,),
        in_specs=[
            rows(d),
            pl.BlockSpec((None, 6, d), lambda i: (i // tiles_per_seq, 0, 0)),
            full(g_pre), full(wa), full(wlr), full(wgk), full(bgk), full(wb), full(wvt),
            table(), table(), table(),
        ],
        out_specs=[rows(w) for w, _ in out_widths] + [pl.BlockSpec((nk, tm), lambda i: (0, i))],
        out_shape=[jax.ShapeDtypeStruct((t, w), dt) for w, dt in out_widths] + [jax.ShapeDtypeStruct((nk, t), BF16)],
        compiler_params=_params(("arbitrary",), 56),
        name="inproj",
    )(x2, mod, g_pre, wa, wlr, wgk, bgk, wb, wvt, rc, rm, rp)


def _gla_body(qf_ref, kf_ref, vf_ref, laf_ref, qb_ref, kb_ref, vb_ref, lab_ref, of_ref, ob_ref, sf_ref, sb_ref):
    @pl.when(pl.program_id(1) == 0)
    def _():
        sf_ref[...] = jnp.zeros_like(sf_ref)
        sb_ref[...] = jnp.zeros_like(sb_ref)

    c = GLA_CHUNK
    r_i = lax.broadcasted_iota(I32, (c, c), 0)
    c_i = lax.broadcasted_iota(I32, (c, c), 1)
    lower = c_i <= r_i
    upper = c_i >= r_i
    cum_f = jnp.where(lower, 1.0, 0.0).astype(BF16)
    cum_b = jnp.where(upper, 1.0, 0.0).astype(BF16)
    lane = lax.broadcasted_iota(I32, (1, LANES), 1)
    head_masks = (lane < GLA_DK, lane >= GLA_DK)

    fwd = [(qf_ref, kf_ref, laf_ref, vf_ref, of_ref, slice(g * c, (g + 1) * c), cum_f, lower, c - 1, c // 2 - 1)
           for g in range(GLA_GROUP)]
    bwd = [(qb_ref, kb_ref, lab_ref, vb_ref, ob_ref, slice(g * c, (g + 1) * c), cum_b, upper, 0, c // 2)
           for g in reversed(range(GLA_GROUP))]
    heads = range(GLA_HEADS)
    pair = [slice((h // 2) * LANES, (h // 2 + 1) * LANES) for h in heads]
    vcols = [slice(h * GLA_DV, (h + 1) * GLA_DV) for h in heads]

    def stage1(item):
        q_ref, k_ref, la_ref, v_ref, o_ref, rows, cum, tri, i_last, i_mid = item
        la = la_ref[rows, :]
        hi = la.astype(BF16)
        lo = (la - hi.astype(F32)).astype(BF16)
        b = jnp.dot(cum, hi, preferred_element_type=F32) + jnp.dot(cum, lo, preferred_element_type=F32)
        b_last = b[i_last:i_last + 1, :]
        b_mid = b[i_mid:i_mid + 1, :]
        q, k = q_ref[rows, :], k_ref[rows, :]
        return (q * jnp.exp(b - b_mid), (k * jnp.exp(b_mid - b)).astype(BF16), q * jnp.exp(b),
                (k * jnp.exp(b_last - b)).astype(BF16), jnp.exp(b_last))

    def stage2(item, pre):
        v_ref, rows, tri = item[3], item[5], item[7]
        qs, ks, qi, kst, decay = pre
        out = []
        for h in heads:
            mask = head_masks[h % 2]
            qs_h = jnp.where(mask, qs[:, pair[h]], 0.0).astype(BF16)
            sc = lax.dot_general(qs_h, ks[:, pair[h]], NT_DIMS, preferred_element_type=F32)
            v_h = v_ref[rows, vcols[h]]
            kv = lax.dot_general(v_h, kst[:, pair[h]], TN_DIMS, preferred_element_type=F32)
            out.append((jnp.where(tri, sc, 0.0).astype(BF16), kv,
                        jnp.where(mask, qi[:, pair[h]], 0.0).astype(BF16), v_h))
        return out

    def run(items, s_ref):
        pre = [stage1(it) for it in items]
        mid = [stage2(it, p) for it, p in zip(items, pre)]
        states = [s_ref[h] for h in heads]
        for it, p, m in zip(items, pre, mid):
            o_ref, rows, decay = it[4], it[5], p[4]
            for h in heads:
                sc, kv, qi_h, v_h = m[h]
                o = jnp.dot(sc, v_h, preferred_element_type=F32)
                o = o + lax.dot_general(qi_h, states[h].astype(BF16), NT_DIMS, preferred_element_type=F32)
                o_ref[rows, vcols[h]] = o
                states[h] = states[h] * decay[:, pair[h]] + kv
        for h in heads:
            s_ref[h] = states[h]

    run(fwd, sf_ref)
    run(bwd, sb_ref)


def _gla(q, k, v, laf, lab):
    t = q.shape[0]
    rows = GLA_GROUP * GLA_CHUNK
    ng = SEQ // rows
    hk = GLA_HEADS * GLA_DK
    hv = GLA_HEADS * GLA_DV

    def fwd(w):
        return pl.BlockSpec((rows, w), lambda b, n: (b * ng + n, 0))

    def bwd(w):
        return pl.BlockSpec((rows, w), lambda b, n: (b * ng + ng - 1 - n, 0))

    return pl.pallas_call(
        _gla_body,
        grid=(BATCH, ng),
        in_specs=[fwd(hk), fwd(hk), fwd(hv), fwd(hk), bwd(hk), bwd(hk), bwd(hv), bwd(hk)],
        out_specs=[fwd(hv), bwd(hv)],
        out_shape=[jax.ShapeDtypeStruct((t, hv), F32)] * 2,
        scratch_shapes=[pltpu.VMEM((GLA_HEADS, GLA_DV, 2 * GLA_DK), F32)] * 2,
        compiler_params=_params(("arbitrary", "arbitrary"), 32),
        name="gla",
    )(q, k, v, laf, q, k, v, lab)


def _attn_body(sink_ref, q_ref, kp_ref, kc_ref, kn_ref, vp_ref, vc_ref, vn_ref, o_ref):
    step = pl.program_id(1)
    last = pl.num_programs(1) - 1
    qb = ATT_BLOCK
    hd = ATT_HEAD_DIM
    k_all = jnp.concatenate([kp_ref[...], kc_ref[...], kn_ref[...]], axis=0)
    vt_all = jnp.concatenate([vp_ref[...], vc_ref[...], vn_ref[...]], axis=1)
    lane = lax.broadcasted_iota(I32, (1, LANES), 1)
    lo = lane < hd
    j_k = lax.broadcasted_iota(I32, (3 * qb, qb), 0)
    i_q = lax.broadcasted_iota(I32, (3 * qb, qb), 1)
    band = jnp.abs(j_k - qb - i_q) <= ATT_WINDOW
    sinks = [jnp.concatenate([jnp.full((1, qb), sink_ref[4 * g + r], F32) for r in range(4)], axis=1)
             for g in range(ATT_KV_HEADS)]
    work = [(j, g) for j in range(ATT_GROUP) for g in range(ATT_KV_HEADS)]

    def scores(j, g):
        valid = band
        if j == 0:
            valid = valid & ((j_k >= qb) | (step > 0))
        if j == ATT_GROUP - 1:
            valid = valid & ((j_k < 2 * qb) | (step < last))
        valid4 = jnp.concatenate([valid] * 4, axis=1)
        rows = slice(j * qb, (j + 1) * qb)
        kg = k_all[j * qb:(j + 3) * qb, g * LANES:(g + 1) * LANES]
        qa = q_ref[rows, (2 * g) * LANES:(2 * g + 1) * LANES]
        qc = q_ref[rows, (2 * g + 1) * LANES:(2 * g + 2) * LANES]
        zero = jnp.zeros_like(qa)
        lhs = jnp.concatenate([jnp.where(lo, qa, zero), jnp.where(lo, zero, qa),
                               jnp.where(lo, qc, zero), jnp.where(lo, zero, qc)], axis=0)
        st = lax.dot_general(kg, lhs, NT_DIMS, preferred_element_type=F32)
        return jnp.where(valid4, st, NEG_INF)

    def softmax(st, g):
        sink = sinks[g]
        m = jnp.maximum(jnp.max(st, axis=0, keepdims=True), sink)
        p = jnp.exp(st - m)
        return p.astype(BF16), 1.0 / (jnp.sum(p, axis=0, keepdims=True) + jnp.exp(sink - m))

    def output(j, g, p, inv):
        rows = slice(j * qb, (j + 1) * qb)
        vgt = vt_all[g * LANES:(g + 1) * LANES, j * qb:(j + 3) * qb]
        ot = jnp.dot(vgt, p, preferred_element_type=F32) * inv
        pair_a = jnp.concatenate([ot[0:hd, 0:qb], ot[hd:2 * hd, qb:2 * qb]], axis=0)
        pair_c = jnp.concatenate([ot[0:hd, 2 * qb:3 * qb], ot[hd:2 * hd, 3 * qb:4 * qb]], axis=0)
        o_ref[rows, (2 * g) * LANES:(2 * g + 1) * LANES] = pair_a.T.astype(o_ref.dtype)
        o_ref[rows, (2 * g + 1) * LANES:(2 * g + 2) * LANES] = pair_c.T.astype(o_ref.dtype)

    s_all = [scores(j, g) for j, g in work]
    p_all = [softmax(st, g) for st, (j, g) in zip(s_all, work)]
    for (j, g), (p, inv) in zip(work, p_all):
        output(j, g, p, inv)


def _attn(sink, aq, ak2, avt):
    t = aq.shape[0]
    qb = ATT_BLOCK
    nb = SEQ // qb
    steps = nb // ATT_GROUP
    nq = ATT_Q_HEADS * ATT_HEAD_DIM
    nk = 2 * ATT_KV_HEADS * ATT_HEAD_DIM

    def edge_block(b, n, shift):
        return b * nb + jnp.clip(n * ATT_GROUP + shift, 0, nb - 1)

    def k_edge(shift):
        return pl.BlockSpec((qb, nk), lambda b, n: (edge_block(b, n, shift), 0))

    def v_edge(shift):
        return pl.BlockSpec((nk, qb), lambda b, n: (0, edge_block(b, n, shift)))

    def group(w):
        return pl.BlockSpec((ATT_GROUP * qb, w), lambda b, n: (b * steps + n, 0))

    v_group = pl.BlockSpec((nk, ATT_GROUP * qb), lambda b, n: (0, b * steps + n))
    return pl.pallas_call(
        _attn_body,
        grid=(BATCH, steps),
        in_specs=[
            pl.BlockSpec(memory_space=pltpu.SMEM),
            group(nq),
            k_edge(-1), group(nk), k_edge(ATT_GROUP), v_edge(-1), v_group, v_edge(ATT_GROUP),
        ],
        out_specs=group(nq),
        out_shape=jax.ShapeDtypeStruct((t, nq), BF16),
        compiler_params=_params(("arbitrary", "arbitrary"), 48),
        name="attn",
    )(sink, aq, ak2, ak2, ak2, avt, avt, avt)


def _post_body(of_ref, ob_ref, gg_ref, oa_ref, x_ref, mod_ref, ggla_ref, gpm_ref, gpf_ref, wout_ref,
               wrh_ref, wrl_ref, br_ref,
               x1_ref, h2_ref, ti_ref, gt_ref, rk_ref, cnt_ref, base_ref):
    tm = TM_POST

    @pl.when(pl.program_id(0) == 0)
    def _():
        base_ref[...] = jnp.zeros_like(base_ref)

    og = of_ref[...] + ob_ref[...]
    gg = gg_ref[...]
    parts = []
    for h in range(GLA_HEADS):
        cols = slice(h * GLA_DV, (h + 1) * GLA_DV)
        parts.append((_rms(og[:, cols], ggla_ref[...]) * _silu(gg[:, cols])).astype(BF16))
    o = jnp.concatenate(parts + [oa_ref[...]], axis=1)
    y = jnp.dot(o, wout_ref[...], preferred_element_type=F32)

    gate1 = mod_ref[2:3, :]
    shift2 = mod_ref[3:4, :]
    scale2 = mod_ref[4:5, :]
    x1 = x_ref[...] + gate1 * _rms(y, gpm_ref[...])
    x1_ref[...] = x1
    h2 = _rms(x1, gpf_ref[...]) * (1.0 + scale2) + shift2
    h2_hi = h2.astype(BF16)
    h2_hi32 = h2_hi.astype(F32)
    h2_lo = (h2 - h2_hi32).astype(BF16)
    _pack_rows(h2_ref, h2_hi32)

    wrh = wrh_ref[...]
    logits = (lax.dot_general(wrh, h2_hi, NT_DIMS, preferred_element_type=F32)
              + lax.dot_general(wrh, h2_lo, NT_DIMS, preferred_element_type=F32)
              + lax.dot_general(wrl_ref[...], h2_hi, NT_DIMS, preferred_element_type=F32)
              + br_ref[...])
    e_iota = lax.broadcasted_iota(I32, (N_EXPERTS, tm), 0)
    idxs, vals = [], []
    work = logits
    for _ in range(TOP_K):
        m = jnp.max(work, axis=0, keepdims=True)
        idx = jnp.min(jnp.where(work == m, e_iota, N_EXPERTS), axis=0, keepdims=True)
        idxs.append(idx)
        vals.append(m)
        work = jnp.where(e_iota == idx, -jnp.inf, work)
    exps = [jnp.exp(v - vals[0]) for v in vals]
    inv = 1.0 / (exps[0] + exps[1] + exps[2] + exps[3])
    gt_ref[...] = jnp.concatenate([e * inv for e in exps], axis=0)
    ti_ref[...] = jnp.concatenate(idxs, axis=0)

    onehots = [e_iota == idx for idx in idxs]
    member = jnp.where(onehots[0] | onehots[1] | onehots[2] | onehots[3], 1.0, 0.0)
    t_row = lax.broadcasted_iota(I32, (tm, tm), 0)
    t_col = lax.broadcasted_iota(I32, (tm, tm), 1)
    strict = jnp.where(t_row < t_col, 1.0, 0.0).astype(BF16)
    before = base_ref[...] + jnp.dot(member.astype(BF16), strict, preferred_element_type=F32)
    rk_ref[...] = jnp.concatenate(
        [jnp.sum(jnp.where(oh, before, 0.0), axis=0, keepdims=True) for oh in onehots], axis=0).astype(I32)
    new_base = base_ref[...] + jnp.sum(member, axis=1, keepdims=True)
    base_ref[...] = new_base
    cnt_ref[...] = jnp.broadcast_to(new_base, cnt_ref.shape)


def _post(o_f, o_b, gg, o_att, x2, mod, g_gla, g_pm, g_pf, wout, wrh, wrl, br):
    t, d = x2.shape
    tm = TM_POST
    tiles_per_seq = SEQ // tm
    hv = GLA_HEADS * GLA_DV

    def full(a):
        return pl.BlockSpec(a.shape, lambda i: (0,) * a.ndim)

    def rows(w):
        return pl.BlockSpec((tm, w), lambda i: (i, 0))

    def lanes():
        return pl.BlockSpec((TOP_K, tm), lambda i: (0, i))

    return pl.pallas_call(
        _post_body,
        grid=(t // tm,),
        in_specs=[
            rows(hv), rows(hv), rows(hv), rows(hv), rows(d),
            pl.BlockSpec((None, 6, d), lambda i: (i // tiles_per_seq, 0, 0)),
            full(g_gla), full(g_pm), full(g_pf), full(wout), full(wrh), full(wrl), full(br),
        ],
        out_specs=[
            rows(d),
            pl.BlockSpec((tm * PACK_SUB, LANES), lambda i: (i, 0)),
            lanes(), lanes(), lanes(),
            pl.BlockSpec((N_EXPERTS, LANES), lambda i: (0, 0)),
        ],
        out_shape=[
            jax.ShapeDtypeStruct((t, d), F32),
            jax.ShapeDtypeStruct((t * PACK_SUB, LANES), I32),
            jax.ShapeDtypeStruct((TOP_K, t), I32),
            jax.ShapeDtypeStruct((TOP_K, t), F32),
            jax.ShapeDtypeStruct((TOP_K, t), I32),
            jax.ShapeDtypeStruct((N_EXPERTS, LANES), F32),
        ],
        scratch_shapes=[pltpu.VMEM((N_EXPERTS, 1), F32)],
        compiler_params=_params(("arbitrary",), 48),
        name="post",
    )(o_f, o_b, gg, o_att, x2, mod, g_gla, g_pm, g_pf, wout, wrh, wrl, br)


def _route_body(ti_ref, rk_ref, cnt_ref, pos_ref, blk_ref):
    cnt = cnt_ref[...]
    padded = jnp.floor((cnt + (MOE_BM - 1)) * (1.0 / MOE_BM)) * MOE_BM
    starts, ends = [], []
    acc = jnp.zeros((1, LANES), F32)
    for e in range(N_EXPERTS):
        starts.append(acc)
        acc = acc + padded[e:e + 1, :]
        ends.append(acc)
    ti = ti_ref[...]
    off = jnp.zeros(ti.shape, F32)
    for e in range(N_EXPERTS):
        off = jnp.where(ti == e, starts[e][:, 0:1], off)
    pos_ref[...] = rk_ref[...] + off.astype(I32)

    def owner_of(row):
        n_le = jnp.zeros(row.shape, I32)
        for e in range(N_EXPERTS):
            n_le = n_le + jnp.where(ends[e][:, 0:1] <= row, 1, 0)
        return jnp.minimum(n_le, N_EXPERTS - 1)

    block_start = lax.broadcasted_iota(I32, (1, MOE_NB_PAD), 1).astype(F32) * MOE_BM
    owner = owner_of(block_start)
    nxt = jnp.zeros((1, MOE_NB_PAD), I32)
    for e in range(N_EXPERTS):
        end_e = ends[e][:, 0:1]
        nxt = jnp.where(owner == e, jnp.where(end_e < acc[:, 0:1], owner_of(end_e), -1), nxt)
    used = jnp.broadcast_to((acc[:, 0:1] * (1.0 / MOE_BM)).astype(I32), (1, MOE_NB_PAD))
    blk_ref[...] = jnp.concatenate([owner, nxt, used, jnp.zeros((SUBLANES - 3, MOE_NB_PAD), I32)], axis=0)


def _route(top_i, rank, counts):
    return pl.pallas_call(
        _route_body,
        out_shape=[
            jax.ShapeDtypeStruct(top_i.shape, I32),
            jax.ShapeDtypeStruct((SUBLANES, MOE_NB_PAD), I32),
        ],
        compiler_params=pltpu.CompilerParams(vmem_limit_bytes=32 * 1024 * 1024),
        name="route",
    )(top_i, rank, counts)


def _sc_workers():
    info = plsc.get_sparse_core_info()
    return info.num_cores, info.num_subcores, info.num_lanes


def _sc_gather_loop(table_hbm, out_hbm, idx_v, base, chunks, buf0, buf1, sem0, sem1):
    window = SC_GATHER_WINDOW

    def fetch(c, buf, sem):
        return pltpu.make_async_copy(table_hbm.at[idx_v.at[pl.ds(c * window, window)]], buf, sem)

    def flush(c, buf):
        pltpu.sync_copy(buf, out_hbm.at[pl.ds(base + c * window, window)])

    fetch(0, buf0, sem0).start()

    @pl.loop(0, chunks, step=2)
    def _(c):
        fetch(c + 1, buf1, sem1).start()
        fetch(c, buf0, sem0).wait()
        flush(c, buf0)

        @pl.when(c + 2 < chunks)
        def _():
            fetch(c + 2, buf0, sem0).start()

        fetch(c + 1, buf1, sem1).wait()
        flush(c + 1, buf1)


def _sc_dispatch_rows(table, pos_flat, n_rows):
    cores, subcores, lanes = _sc_workers()
    workers = cores * subcores
    window = SC_GATHER_WINDOW
    per_worker = n_rows // workers
    chunks = per_worker // window
    n_assign = pos_flat.shape[0]
    scan = SC_SCAN_CHUNK
    assert per_worker * workers == n_rows and chunks * window == per_worker and chunks % 2 == 0
    assert n_assign % scan == 0 and scan % lanes == 0 and per_worker % lanes == 0
    row_shape = table.shape[1:]
    mesh = plsc.VectorSubcoreMesh(core_axis_name="core", subcore_axis_name="subcore")

    @functools.partial(
        pl.kernel,
        out_type=jax.ShapeDtypeStruct((n_rows,) + row_shape, table.dtype),
        mesh=mesh,
        scratch_types=[
            pltpu.VMEM((per_worker,), I32),
            pltpu.VMEM((scan,), I32),
            pltpu.VMEM((window,) + row_shape, table.dtype),
            pltpu.VMEM((window,) + row_shape, table.dtype),
            pltpu.SemaphoreType.DMA,
            pltpu.SemaphoreType.DMA,
        ],
        compiler_params=pltpu.CompilerParams(needs_layout_passes=False),
        name="sc_dispatch_rows",
    )
    def dispatch(table_hbm, pos_hbm, out_hbm, src_v, pos_v, buf0, buf1, sem0, sem1):
        wid = lax.axis_index("subcore") * cores + lax.axis_index("core")
        base = wid * per_worker
        lane = lax.iota(I32, lanes)

        @pl.loop(0, per_worker, step=lanes)
        def _(j):
            src_v[pl.ds(j, lanes)] = (base + j + lane) & (TOKENS - 1)

        @pl.loop(0, n_assign, step=scan)
        def _(a0):
            pltpu.sync_copy(pos_hbm.at[pl.ds(a0, scan)], pos_v)

            @pl.loop(0, scan, step=lanes)
            def _(j):
                rel = pos_v[pl.ds(j, lanes)] - base
                mine = (rel >= 0) & (rel < per_worker)
                tok = (a0 + j + lane) & (TOKENS - 1)
                plsc.store_scatter(src_v, [jnp.where(mine, rel, 0)], tok, mask=mine)

        _sc_gather_loop(table_hbm, out_hbm, src_v, base, chunks, buf0, buf1, sem0, sem1)

    return dispatch(table, pos_flat)


def _sc_gather_rows(table, idx):
    cores, subcores, _ = _sc_workers()
    workers = cores * subcores
    n = idx.shape[0]
    window = SC_GATHER_WINDOW
    per_worker = n // workers
    chunks = per_worker // window
    assert per_worker * workers == n and chunks * window == per_worker and chunks % 2 == 0
    row_shape = table.shape[1:]
    mesh = plsc.VectorSubcoreMesh(core_axis_name="core", subcore_axis_name="subcore")

    @functools.partial(
        pl.kernel,
        out_type=jax.ShapeDtypeStruct((n,) + row_shape, table.dtype),
        mesh=mesh,
        scratch_types=[
            pltpu.VMEM((per_worker,), I32),
            pltpu.VMEM((window,) + row_shape, table.dtype),
            pltpu.VMEM((window,) + row_shape, table.dtype),
            pltpu.SemaphoreType.DMA,
            pltpu.SemaphoreType.DMA,
        ],
        name="sc_gather_rows",
    )
    def gather(table_hbm, idx_hbm, out_hbm, idx_v, buf0, buf1, sem0, sem1):
        wid = lax.axis_index("subcore") * cores + lax.axis_index("core")
        base = wid * per_worker
        pltpu.sync_copy(idx_hbm.at[pl.ds(base, per_worker)], idx_v)
        _sc_gather_loop(table_hbm, out_hbm, idx_v, base, chunks, buf0, buf1, sem0, sem1)

    return gather(table, idx)


def _experts_body(be_ref, nxt_ref, nu_ref, xs_ref, bgu_ref, bd_ref, wgu_hbm, wd_hbm, ys_ref,
                  wgu_f32, wd_f32, wgu_bf, wd_bf, sems):
    step = pl.program_id(0)
    bm = MOE_BM

    def fetch(e):
        return (pltpu.make_async_copy(wgu_hbm.at[e], wgu_f32, sems.at[0]),
                pltpu.make_async_copy(wd_hbm.at[e], wd_f32, sems.at[1]))

    @pl.when(step == 0)
    def _():
        for cp in fetch(be_ref[0]):
            cp.start()

    for sub in range(MOE_SUB):
        i = step * MOE_SUB + sub
        e = be_ref[i]
        used = i < nu_ref[0]
        fresh = (i == 0) | (e != be_ref[jnp.maximum(i - 1, 0)])
        rows = pl.ds(sub * bm * PACK_SUB, bm * PACK_SUB)
        xs_sub = xs_ref.at[rows, :]
        ys_sub = ys_ref.at[rows, :]

        @pl.when(used & fresh)
        def _():
            for cp in fetch(e):
                cp.wait()
            wgu_bf[...] = wgu_f32[...].astype(BF16)
            wd_bf[...] = wd_f32[...].astype(BF16)

            @pl.when(nxt_ref[i] >= 0)
            def _():
                for cp in fetch(nxt_ref[i]):
                    cp.start()

        @pl.when(used)
        def _():
            x = jnp.concatenate(_unpack_rows(xs_sub, bm), axis=1).astype(BF16)
            gu = jnp.dot(x, wgu_bf[...], preferred_element_type=F32) + bgu_ref[pl.ds(e, 1), :]
            gate = jnp.minimum(gu[:, 0:D_FF], SWIGLU_LIMIT)
            up = jnp.clip(gu[:, D_FF:2 * D_FF], -SWIGLU_LIMIT, SWIGLU_LIMIT)
            act = ((up + 1.0) * gate * jax.nn.sigmoid(SWIGLU_ALPHA * gate)).astype(BF16)
            y = jnp.dot(act, wd_bf[...], preferred_element_type=F32) + bd_ref[pl.ds(e, 1), :]
            _pack_rows(ys_sub, y.astype(BF16).astype(F32))

        @pl.when(jnp.logical_not(used))
        def _():
            ys_sub[...] = jnp.zeros((bm * PACK_SUB, LANES), I32)


def _experts(block_expert, block_next, n_used, xs, w_gate_up, b_gate_up, w_down, b_down):
    rows = MOE_SUB * MOE_BM * PACK_SUB
    d = D_MODEL

    def last_used_step(i, be, nx, nu):
        return jnp.minimum(i, (nu[0] - 1) // MOE_SUB)

    grid_spec = pltpu.PrefetchScalarGridSpec(
        num_scalar_prefetch=3,
        grid=(MOE_NB // MOE_SUB,),
        in_specs=[
            pl.BlockSpec((rows, LANES), lambda i, be, nx, nu: (last_used_step(i, be, nx, nu), 0)),
            pl.BlockSpec(b_gate_up.shape, lambda i, be, nx, nu: (0, 0)),
            pl.BlockSpec(b_down.shape, lambda i, be, nx, nu: (0, 0)),
            pl.BlockSpec(memory_space=pl.ANY),
            pl.BlockSpec(memory_space=pl.ANY),
        ],
        out_specs=pl.BlockSpec((rows, LANES), lambda i, be, nx, nu: (i, 0)),
        scratch_shapes=[
            pltpu.VMEM((d, 2 * D_FF), F32), pltpu.VMEM((D_FF, d), F32),
            pltpu.VMEM((d, 2 * D_FF), BF16), pltpu.VMEM((D_FF, d), BF16),
            pltpu.SemaphoreType.DMA((2,)),
        ],
    )
    return pl.pallas_call(
        _experts_body,
        grid_spec=grid_spec,
        out_shape=jax.ShapeDtypeStruct((MOE_ROWS * PACK_SUB, LANES), I32),
        compiler_params=_params(("arbitrary",), 48),
        name="experts",
    )(block_expert, block_next, n_used, xs, b_gate_up, b_down, w_gate_up, w_down)


def _combine_body(gates_ref, x1_ref, mod_ref, gpost_ref, y0_ref, y1_ref, y2_ref, y3_ref, o_ref):
    tm = TM_COMB
    gates = gates_ref[...]
    y_hi = jnp.zeros((tm, PACK_COLS), F32)
    y_lo = jnp.zeros((tm, PACK_COLS), F32)
    for k, yk_ref in enumerate((y0_ref, y1_ref, y2_ref, y3_ref)):
        hi, lo = _unpack_rows(yk_ref, tm)
        y_hi = y_hi + hi * gates[:, k:k + 1]
        y_lo = y_lo + lo * gates[:, k:k + 1]
    y = jnp.concatenate([y_hi, y_lo], axis=1)
    gate2 = mod_ref[5:6, :]
    o_ref[...] = x1_ref[...] + gate2 * _rms(y, gpost_ref[...])


def _combine(gates_t, x1, mod, g_post, y4):
    t, d = x1.shape
    tm = TM_COMB
    tiles = t // tm
    tiles_per_seq = SEQ // tm

    def slab(k):
        return pl.BlockSpec((tm * PACK_SUB, LANES), lambda i: (k * tiles + i, 0))

    return pl.pallas_call(
        _combine_body,
        grid=(tiles,),
        in_specs=[
            pl.BlockSpec((tm, TOP_K), lambda i: (i, 0)),
            pl.BlockSpec((tm, d), lambda i: (i, 0)),
            pl.BlockSpec((None, 6, d), lambda i: (i // tiles_per_seq, 0, 0)),
            pl.BlockSpec(g_post.shape, lambda i: (0, 0)),
            slab(0), slab(1), slab(2), slab(3),
        ],
        out_specs=pl.BlockSpec((tm, d), lambda i: (i, 0)),
        out_shape=jax.ShapeDtypeStruct((t, d), F32),
        compiler_params=_params(("arbitrary",), 48),
        name="combine",
    )(gates_t, x1, mod, g_post, y4, y4, y4, y4)


def _rotary_tables():
    half = ROT_DIM // 2
    inv_freq = ROPE_THETA ** (-2.0 * jnp.arange(half, dtype=F32) / ROT_DIM)
    ang = jnp.arange(SEQ).astype(F32)[:, None] * inv_freq[None, :]
    cos, sin = jnp.cos(ang), jnp.sin(ang)
    ones = jnp.ones((SEQ, ATT_HEAD_DIM - ROT_DIM), F32)
    zeros = jnp.zeros((SEQ, ATT_HEAD_DIM - ROT_DIM), F32)
    zh = jnp.zeros((SEQ, half), F32)
    reps = LANES // ATT_HEAD_DIM
    rc = jnp.tile(jnp.concatenate([cos, cos, ones], axis=1), (1, reps))
    rm = jnp.tile(jnp.concatenate([-sin, zh, zeros], axis=1), (1, reps))
    rp = jnp.tile(jnp.concatenate([zh, sin, zeros], axis=1), (1, reps))
    return rc, rm, rp


def _mixer_inputs(w_in, w_gk_fwd, b_gk_fwd, w_gk_bwd, b_gk_bwd):
    hk = GLA_HEADS * GLA_DK
    hv = GLA_HEADS * GLA_DV
    w = w_in[0]
    o_lr = 2 * hk + 2 * hv
    o_aq = o_lr + 2 * GLA_RANK
    o_ak = o_aq + ATT_Q_HEADS * ATT_HEAD_DIM
    o_av = o_ak + ATT_KV_HEADS * ATT_HEAD_DIM
    hd = ATT_HEAD_DIM
    wa = w[:, :o_lr].astype(BF16)
    wlr = w[:, o_lr:o_aq].astype(BF16)
    dup = lambda m: jnp.concatenate([m[:, g * hd:(g + 1) * hd] for g in range(ATT_KV_HEADS) for _ in range(2)], axis=1)
    wb = jnp.concatenate([w[:, o_aq:o_ak], dup(w[:, o_ak:o_av])], axis=1).astype(BF16)
    wvt = dup(w[:, o_av:o_av + ATT_KV_HEADS * hd]).T.astype(BF16)
    zr = jnp.zeros((GLA_RANK, hk), F32)
    wgk = jnp.concatenate([jnp.concatenate([w_gk_fwd[0], zr], axis=1),
                           jnp.concatenate([zr, w_gk_bwd[0]], axis=1)], axis=0).astype(BF16)
    bgk = jnp.concatenate([b_gk_fwd[0], b_gk_bwd[0]])[None, :]
    return (wa, wlr, wgk, bgk, wb, wvt) + _rotary_tables()


def kernel(x, c, w_ada, b_ada, g_pre_mix, g_post_mix, w_in, w_gk_fwd, b_gk_fwd, w_gk_bwd, b_gk_bwd, g_gla_out,
           attn_sink, w_out, g_pre_ffn, g_post_ffn, w_router, b_router, w_gate_up, b_gate_up, w_down, b_down):
    assert x.shape == (BATCH, SEQ, D_MODEL) and w_ada.shape[0] == 1
    d = D_MODEL
    x2 = x.reshape(TOKENS, d)

    c_pad = jnp.pad(c, ((0, SUBLANES - BATCH), (0, 0)))
    mod = _ada(c_pad, w_ada[0], b_ada)[:BATCH].reshape(BATCH, 6, d)

    mixer_in = _mixer_inputs(w_in, w_gk_fwd, b_gk_fwd, w_gk_bwd, b_gk_bwd)
    q, k, v, gg, laf, lab, aq, ak2, avt = _inproj(x2, mod, g_pre_mix, *mixer_in)
    o_f, o_b = _gla(q, k, v, laf, lab)
    o_att = _attn(attn_sink[0], aq, ak2, avt)

    wr_t = w_router[0].T
    wrh = wr_t.astype(BF16)
    wrl = (wr_t - wrh.astype(F32)).astype(BF16)
    x1, h2_tiles, top_i, gates, rank, counts = _post(
        o_f, o_b, gg, o_att, x2, mod, g_gla_out, g_post_mix, g_pre_ffn, w_out[0].astype(BF16), wrh, wrl,
        b_router[0][:, None])

    pos, blocks = _route(top_i, rank, counts)
    pos_flat = pos.reshape(TOP_K * TOKENS)
    xs = _sc_dispatch_rows(h2_tiles.reshape(TOKENS, PACK_SUB, LANES), pos_flat, MOE_ROWS)
    xs = xs.reshape(MOE_ROWS * PACK_SUB, LANES)
    ys = _experts(blocks[0, :MOE_NB], blocks[1, :MOE_NB], blocks[2, :1], xs,
                  w_gate_up[0], b_gate_up[0], w_down[0], b_down[0])
    y4 = _sc_gather_rows(ys.reshape(MOE_ROWS, PACK_SUB, LANES), pos_flat)
    out = _combine(gates.T, x1, mod, g_post_ffn, y4.reshape(TOP_K * TOKENS * PACK_SUB, LANES))
    return out.reshape(BATCH, SEQ, d)
```

```python
import functools

import jax
import jax.numpy as jnp
import numpy as np
from jax import lax
from jax.experimental import pallas as pl
from jax.experimental.pallas import tpu as pltpu
from jax.experimental.pallas import tpu_sc as plsc

F32 = jnp.float32
BF16 = jnp.bfloat16
I32 = jnp.int32

D_MODEL = 1024
BATCH = 2
SEQ = 8192
TOKENS = BATCH * SEQ
GLA_HEADS = 4
GLA_DV = 128
GLA_DK = 64
GLA_RANK = 16
GLA_GATE_NORMALIZER = 16.0
GLA_CHUNK = 64
ATT_Q_HEADS = 8
ATT_KV_HEADS = 2
ATT_HEAD_DIM = 64
ATT_WINDOW = 128
ATT_BLOCK = 128
ROT_DIM = 16
ROPE_THETA = 500000.0
N_EXPERTS = 32
TOP_K = 4
D_FF = 1024
SWIGLU_LIMIT = 7.0
SWIGLU_ALPHA = 1.702
NORM_EPS = 1e-6
NEG_INF = -1e30

LANES = 128
SUBLANES = 8
PACK_COLS = D_MODEL // 2
PACK_SUB = PACK_COLS // LANES

TM_IN = 512
GLA_GROUP = 8
ATT_GROUP = 8
TM_POST = 512
MOE_BM = 256
MOE_ROWS = TOKENS * TOP_K + N_EXPERTS * MOE_BM
MOE_NB = MOE_ROWS // MOE_BM
MOE_SUB = 2
MOE_NB_PAD = ((MOE_NB + LANES - 1) // LANES) * LANES
SC_SCAN_CHUNK = 4096
SC_SCAN_UNROLL = 8
TM_COMB = 256
SC_GATHER_WINDOW = 64

NT_DIMS = (((1,), (1,)), ((), ()))
TN_DIMS = (((0,), (0,)), ((), ()))


def _params(semantics, vmem_mib):
    return pltpu.CompilerParams(dimension_semantics=semantics, vmem_limit_bytes=vmem_mib * 1024 * 1024)


def _rms(x, g):
    return x * lax.rsqrt(jnp.mean(x * x, axis=-1, keepdims=True) + NORM_EPS) * g


def _silu(x):
    return x * jax.nn.sigmoid(x)


def _pack_rows(ref, v):
    m = v.shape[0]
    bits = lax.bitcast_convert_type(v, jnp.uint32)
    word = lax.bitcast_convert_type(bits[:, :PACK_COLS] | (bits[:, PACK_COLS:] >> 16), I32)
    for s in range(PACK_SUB):
        ref[pl.ds(s, m, stride=PACK_SUB), :] = word[:, s * LANES:(s + 1) * LANES]


def _unpack_rows(ref, m):
    word = jnp.concatenate([ref[pl.ds(s, m, stride=PACK_SUB), :] for s in range(PACK_SUB)], axis=1)
    bits = lax.bitcast_convert_type(word, jnp.uint32)
    hi = lax.bitcast_convert_type(bits & jnp.uint32(0xFFFF0000), F32)
    lo = lax.bitcast_convert_type(bits << 16, F32)
    return hi, lo


def _ada_body(c_ref, w_ref, b_ref, o_ref):
    ca = _silu(c_ref[...]).astype(BF16)
    o_ref[...] = jnp.dot(ca, w_ref[...].astype(BF16), preferred_element_type=F32) + b_ref[...]


def _ada(c_pad, w_ada, b_ada):
    d = D_MODEL
    return pl.pallas_call(
        _ada_body,
        grid=(6,),
        in_specs=[
            pl.BlockSpec((SUBLANES, d), lambda j: (0, 0)),
            pl.BlockSpec((d, d), lambda j: (0, j)),
            pl.BlockSpec((1, d), lambda j: (0, j)),
        ],
        out_specs=pl.BlockSpec((SUBLANES, d), lambda j: (0, j)),
        out_shape=jax.ShapeDtypeStruct((SUBLANES, 6 * d), F32),
        compiler_params=_params(("arbitrary",), 32),
        name="ada",
    )(c_pad, w_ada, b_ada)


def _rotary(x, cos_t, msin_t, psin_t):
    width = x.shape[1]
    reps = width // LANES
    c = jnp.concatenate([cos_t] * reps, axis=1)
    m = jnp.concatenate([msin_t] * reps, axis=1)
    p = jnp.concatenate([psin_t] * reps, axis=1)
    half = ROT_DIM // 2
    return x * c + pltpu.roll(x, width - half, 1) * m + pltpu.roll(x, half, 1) * p


def _inproj_body(x_ref, mod_ref, g_ref, wa_ref, wlr_ref, wgk_ref, bgk_ref, wb_ref, wvt_ref, rc_ref, rm_ref, rp_ref,
                 q_ref, k_ref, v_ref, gg_ref, laf_ref, lab_ref, aq_ref, ak_ref, avt_ref):
    shift = mod_ref[0:1, :]
    scale = mod_ref[1:2, :]
    h = (_rms(x_ref[...], g_ref[...]) * (1.0 + scale) + shift).astype(BF16)

    hk = GLA_HEADS * GLA_DK
    hv = GLA_HEADS * GLA_DV
    pa = jnp.dot(h, wa_ref[...], preferred_element_type=F32)
    q_ref[...] = pa[:, 0:hk] * (GLA_DK ** -0.5)
    k_ref[...] = pa[:, hk:2 * hk]
    v_ref[...] = pa[:, 2 * hk:2 * hk + hv].astype(BF16)
    gg_ref[...] = pa[:, 2 * hk + hv:2 * hk + 2 * hv]

    plr = jnp.dot(h, wlr_ref[...], preferred_element_type=F32)
    gk = jnp.dot(plr.astype(BF16), wgk_ref[...], preferred_element_type=F32) + bgk_ref[...]
    la = (jnp.minimum(gk, 0.0) - jnp.log1p(jnp.exp(-jnp.abs(gk)))) * (1.0 / GLA_GATE_NORMALIZER)
    laf_ref[...] = la[:, 0:hk]
    lab_ref[...] = la[:, hk:2 * hk]

    pb = jnp.dot(h, wb_ref[...], preferred_element_type=F32)
    nq = ATT_Q_HEADS * ATT_HEAD_DIM
    nk = 2 * ATT_KV_HEADS * ATT_HEAD_DIM
    rc, rm, rp = rc_ref[...], rm_ref[...], rp_ref[...]
    aq_ref[...] = (_rotary(pb[:, 0:nq], rc, rm, rp) * (ATT_HEAD_DIM ** -0.5)).astype(BF16)
    ak_ref[...] = _rotary(pb[:, nq:nq + nk], rc, rm, rp).astype(BF16)
    avt_ref[...] = lax.dot_general(wvt_ref[...], h, NT_DIMS, preferred_element_type=F32).astype(BF16)


def _inproj(x2, mod, g_pre, wa, wlr, wgk, bgk, wb, wvt, rc, rm, rp):
    t, d = x2.shape
    tm = TM_IN
    tiles_per_seq = SEQ // tm
    hk = GLA_HEADS * GLA_DK
    hv = GLA_HEADS * GLA_DV
    nq = ATT_Q_HEADS * ATT_HEAD_DIM
    nk = 2 * ATT_KV_HEADS * ATT_HEAD_DIM

    def full(a):
        return pl.BlockSpec(a.shape, lambda i: (0,) * a.ndim)

    def rows(w):
        return pl.BlockSpec((tm, w), lambda i: (i, 0))

    def table():
        return pl.BlockSpec((tm, LANES), lambda i: (i % tiles_per_seq, 0))

    out_widths = [(hk, F32), (hk, F32), (hv, BF16), (hv, F32), (hk, F32), (hk, F32), (nq, BF16), (nk, BF16)]
    return pl.pallas_call(
        _inproj_body,
        grid=(t // tm,),
        in_specs=[
            rows(d),
            pl.BlockSpec((None, 6, d), lambda i: (i // tiles_per_seq, 0, 0)),
            full(g_pre), full(wa), full(wlr), full(wgk), full(bgk), full(wb), full(wvt),
            table(), table(), table(),
        ],
        out_specs=[rows(w) for w, _ in out_widths] + [pl.BlockSpec((nk, tm), lambda i: (0, i))],
        out_shape=[jax.ShapeDtypeStruct((t, w), dt) for w, dt in out_widths] + [jax.ShapeDtypeStruct((nk, t), BF16)],
        compiler_params=_params(("arbitrary",), 56),
        name="inproj",
    )(x2, mod, g_pre, wa, wlr, wgk, bgk, wb, wvt, rc, rm, rp)


def _gla_body(qf_ref, kf_ref, vf_ref, laf_ref, qb_ref, kb_ref, vb_ref, lab_ref, of_ref, ob_ref, sf_ref, sb_ref):
    @pl.when(pl.program_id(1) == 0)
    def _():
        sf_ref[...] = jnp.zeros_like(sf_ref)
        sb_ref[...] = jnp.zeros_like(sb_ref)

    c = GLA_CHUNK
    r_i = lax.broadcasted_iota(I32, (c, c), 0)
    c_i = lax.broadcasted_iota(I32, (c, c), 1)
    lower = c_i <= r_i
    upper = c_i >= r_i
    cum_f = jnp.where(lower, 1.0, 0.0).astype(BF16)
    cum_b = jnp.where(upper, 1.0, 0.0).astype(BF16)
    lane = lax.broadcasted_iota(I32, (1, LANES), 1)
    head_masks = (lane < GLA_DK, lane >= GLA_DK)

    fwd = [(qf_ref, kf_ref, laf_ref, vf_ref, of_ref, slice(g * c, (g + 1) * c), cum_f, lower, c - 1, c // 2 - 1)
           for g in range(GLA_GROUP)]
    bwd = [(qb_ref, kb_ref, lab_ref, vb_ref, ob_ref, slice(g * c, (g + 1) * c), cum_b, upper, 0, c // 2)
           for g in reversed(range(GLA_GROUP))]
    heads = range(GLA_HEADS)
    pair = [slice((h // 2) * LANES, (h // 2 + 1) * LANES) for h in heads]
    vcols = [slice(h * GLA_DV, (h + 1) * GLA_DV) for h in heads]

    def stage1(item):
        q_ref, k_ref, la_ref, v_ref, o_ref, rows, cum, tri, i_last, i_mid = item
        la = la_ref[rows, :]
        hi = la.astype(BF16)
        lo = (la - hi.astype(F32)).astype(BF16)
        b = jnp.dot(cum, hi, preferred_element_type=F32) + jnp.dot(cum, lo, preferred_element_type=F32)
        b_last = b[i_last:i_last + 1, :]
        b_mid = b[i_mid:i_mid + 1, :]
        q, k = q_ref[rows, :], k_ref[rows, :]
        return (q * jnp.exp(b - b_mid), (k * jnp.exp(b_mid - b)).astype(BF16), q * jnp.exp(b),
                (k * jnp.exp(b_last - b)).astype(BF16), jnp.exp(b_last))

    def stage2(item, pre):
        v_ref, rows, tri = item[3], item[5], item[7]
        qs, ks, qi, kst, decay = pre
        out = []
        for h in heads:
            mask = head_masks[h % 2]
            qs_h = jnp.where(mask, qs[:, pair[h]], 0.0).astype(BF16)
            sc = lax.dot_general(qs_h, ks[:, pair[h]], NT_DIMS, preferred_element_type=F32)
            v_h = v_ref[rows, vcols[h]]
            kv = lax.dot_general(v_h, kst[:, pair[h]], TN_DIMS, preferred_element_type=F32)
            out.append((jnp.where(tri, sc, 0.0).astype(BF16), kv,
                        jnp.where(mask, qi[:, pair[h]], 0.0).astype(BF16), v_h))
        return out

    def run(items, s_ref):
        pre = [stage1(it) for it in items]
        mid = [stage2(it, p) for it, p in zip(items, pre)]
        states = [s_ref[h] for h in heads]
        for it, p, m in zip(items, pre, mid):
            o_ref, rows, decay = it[4], it[5], p[4]
            for h in heads:
                sc, kv, qi_h, v_h = m[h]
                o = jnp.dot(sc, v_h, preferred_element_type=F32)
                o = o + lax.dot_general(qi_h, states[h].astype(BF16), NT_DIMS, preferred_element_type=F32)
                o_ref[rows, vcols[h]] = o
                states[h] = states[h] * decay[:, pair[h]] + kv
        for h in heads:
            s_ref[h] = states[h]

    run(fwd, sf_ref)
    run(bwd, sb_ref)


def _gla(q, k, v, laf, lab):
    t = q.shape[0]
    rows = GLA_GROUP * GLA_CHUNK
    ng = SEQ // rows
    hk = GLA_HEADS * GLA_DK
    hv = GLA_HEADS * GLA_DV

    def fwd(w):
        return pl.BlockSpec((rows, w), lambda b, n: (b * ng + n, 0))

    def bwd(w):
        return pl.BlockSpec((rows, w), lambda b, n: (b * ng + ng - 1 - n, 0))

    return pl.pallas_call(
        _gla_body,
        grid=(BATCH, ng),
        in_specs=[fwd(hk), fwd(hk), fwd(hv), fwd(hk), bwd(hk), bwd(hk), bwd(hv), bwd(hk)],
        out_specs=[fwd(hv), bwd(hv)],
        out_shape=[jax.ShapeDtypeStruct((t, hv), F32)] * 2,
        scratch_shapes=[pltpu.VMEM((GLA_HEADS, GLA_DV, 2 * GLA_DK), F32)] * 2,
        compiler_params=_params(("arbitrary", "arbitrary"), 32),
        name="gla",
    )(q, k, v, laf, q, k, v, lab)


def _attn_body(sink_ref, q_ref, kp_ref, kc_ref, kn_ref, vp_ref, vc_ref, vn_ref, o_ref):
    step = pl.program_id(1)
    last = pl.num_programs(1) - 1
    qb = ATT_BLOCK
    hd = ATT_HEAD_DIM
    k_all = jnp.concatenate([kp_ref[...], kc_ref[...], kn_ref[...]], axis=0)
    vt_all = jnp.concatenate([vp_ref[...], vc_ref[...], vn_ref[...]], axis=1)
    lane = lax.broadcasted_iota(I32, (1, LANES), 1)
    lo = lane < hd
    j_k = lax.broadcasted_iota(I32, (3 * qb, qb), 0)
    i_q = lax.broadcasted_iota(I32, (3 * qb, qb), 1)
    band = jnp.abs(j_k - qb - i_q) <= ATT_WINDOW
    sinks = [jnp.concatenate([jnp.full((1, qb), sink_ref[4 * g + r], F32) for r in range(4)], axis=1)
             for g in range(ATT_KV_HEADS)]
    work = [(j, g) for j in range(ATT_GROUP) for g in range(ATT_KV_HEADS)]

    def scores(j, g):
        valid = band
        if j == 0:
            valid = valid & ((j_k >= qb) | (step > 0))
        if j == ATT_GROUP - 1:
            valid = valid & ((j_k < 2 * qb) | (step < last))
        valid4 = jnp.concatenate([valid] * 4, axis=1)
        rows = slice(j * qb, (j + 1) * qb)
        kg = k_all[j * qb:(j + 3) * qb, g * LANES:(g + 1) * LANES]
        qa = q_ref[rows, (2 * g) * LANES:(2 * g + 1) * LANES]
        qc = q_ref[rows, (2 * g + 1) * LANES:(2 * g + 2) * LANES]
        zero = jnp.zeros_like(qa)
        lhs = jnp.concatenate([jnp.where(lo, qa, zero), jnp.where(lo, zero, qa),
                               jnp.where(lo, qc, zero), jnp.where(lo, zero, qc)], axis=0)
        st = lax.dot_general(kg, lhs, NT_DIMS, preferred_element_type=F32)
        return jnp.where(valid4, st, NEG_INF)

    def softmax(st, g):
        sink = sinks[g]
        m = jnp.maximum(jnp.max(st, axis=0, keepdims=True), sink)
        p = jnp.exp(st - m)
        return p.astype(BF16), 1.0 / (jnp.sum(p, axis=0, keepdims=True) + jnp.exp(sink - m))

    def output(j, g, p, inv):
        rows = slice(j * qb, (j + 1) * qb)
        vgt = vt_all[g * LANES:(g + 1) * LANES, j * qb:(j + 3) * qb]
        ot = jnp.dot(vgt, p, preferred_element_type=F32) * inv
        pair_a = jnp.concatenate([ot[0:hd, 0:qb], ot[hd:2 * hd, qb:2 * qb]], axis=0)
        pair_c = jnp.concatenate([ot[0:hd, 2 * qb:3 * qb], ot[hd:2 * hd, 3 * qb:4 * qb]], axis=0)
        o_ref[rows, (2 * g) * LANES:(2 * g + 1) * LANES] = pair_a.T.astype(o_ref.dtype)
        o_ref[rows, (2 * g + 1) * LANES:(2 * g + 2) * LANES] = pair_c.T.astype(o_ref.dtype)

    s_all = [scores(j, g) for j, g in work]
    p_all = [softmax(st, g) for st, (j, g) in zip(s_all, work)]
    for (j, g), (p, inv) in zip(work, p_all):
        output(j, g, p, inv)


def _attn(sink, aq, ak2, avt):
    t = aq.shape[0]
    qb = ATT_BLOCK
    nb = SEQ // qb
    steps = nb // ATT_GROUP
    nq = ATT_Q_HEADS * ATT_HEAD_DIM
    nk = 2 * ATT_KV_HEADS * ATT_HEAD_DIM

    def edge_block(b, n, shift):
        return b * nb + jnp.clip(n * ATT_GROUP + shift, 0, nb - 1)

    def k_edge(shift):
        return pl.BlockSpec((qb, nk), lambda b, n: (edge_block(b, n, shift), 0))

    def v_edge(shift):
        return pl.BlockSpec((nk, qb), lambda b, n: (0, edge_block(b, n, shift)))

    def group(w):
        return pl.BlockSpec((ATT_GROUP * qb, w), lambda b, n: (b * steps + n, 0))

    v_group = pl.BlockSpec((nk, ATT_GROUP * qb), lambda b, n: (0, b * steps + n))
    return pl.pallas_call(
        _attn_body,
        grid=(BATCH, steps),
        in_specs=[
            pl.BlockSpec(memory_space=pltpu.SMEM),
            group(nq),
            k_edge(-1), group(nk), k_edge(ATT_GROUP), v_edge(-1), v_group, v_edge(ATT_GROUP),
        ],
        out_specs=group(nq),
        out_shape=jax.ShapeDtypeStruct((t, nq), BF16),
        compiler_params=_params(("arbitrary", "arbitrary"), 48),
        name="attn",
    )(sink, aq, ak2, ak2, ak2, avt, avt, avt)


def _post_body(of_ref, ob_ref, gg_ref, oa_ref, x_ref, mod_ref, ggla_ref, gpm_ref, gpf_ref, wout_ref,
               wrh_ref, wrl_ref, br_ref,
               x1_ref, h2_ref, ti_ref, gt_ref, rk_ref, cnt_ref, base_ref):
    tm = TM_POST

    @pl.when(pl.program_id(0) == 0)
    def _():
        base_ref[...] = jnp.zeros_like(base_ref)

    og = of_ref[...] + ob_ref[...]
    gg = gg_ref[...]
    parts = []
    for h in range(GLA_HEADS):
        cols = slice(h * GLA_DV, (h + 1) * GLA_DV)
        parts.append((_rms(og[:, cols], ggla_ref[...]) * _silu(gg[:, cols])).astype(BF16))
    o = jnp.concatenate(parts + [oa_ref[...]], axis=1)
    y = jnp.dot(o, wout_ref[...], preferred_element_type=F32)

    gate1 = mod_ref[2:3, :]
    shift2 = mod_ref[3:4, :]
    scale2 = mod_ref[4:5, :]
    x1 = x_ref[...] + gate1 * _rms(y, gpm_ref[...])
    x1_ref[...] = x1
    h2 = _rms(x1, gpf_ref[...]) * (1.0 + scale2) + shift2
    h2_hi = h2.astype(BF16)
    h2_hi32 = h2_hi.astype(F32)
    h2_lo = (h2 - h2_hi32).astype(BF16)
    _pack_rows(h2_ref, h2_hi32)

    wrh = wrh_ref[...]
    logits = (lax.dot_general(wrh, h2_hi, NT_DIMS, preferred_element_type=F32)
              + lax.dot_general(wrh, h2_lo, NT_DIMS, preferred_element_type=F32)
              + lax.dot_general(wrl_ref[...], h2_hi, NT_DIMS, preferred_element_type=F32)
              + br_ref[...])
    e_iota = lax.broadcasted_iota(I32, (N_EXPERTS, tm), 0)
    idxs, vals = [], []
    work = logits
    for _ in range(TOP_K):
        m = jnp.max(work, axis=0, keepdims=True)
        idx = jnp.min(jnp.where(work == m, e_iota, N_EXPERTS), axis=0, keepdims=True)
        idxs.append(idx)
        vals.append(m)
        work = jnp.where(e_iota == idx, -jnp.inf, work)
    exps = [jnp.exp(v - vals[0]) for v in vals]
    inv = 1.0 / (exps[0] + exps[1] + exps[2] + exps[3])
    gt_ref[...] = jnp.concatenate([e * inv for e in exps], axis=0)
    ti_ref[...] = jnp.concatenate(idxs, axis=0)

    onehots = [e_iota == idx for idx in idxs]
    member = jnp.where(onehots[0] | onehots[1] | onehots[2] | onehots[3], 1.0, 0.0)
    t_row = lax.broadcasted_iota(I32, (tm, tm), 0)
    t_col = lax.broadcasted_iota(I32, (tm, tm), 1)
    strict = jnp.where(t_row < t_col, 1.0, 0.0).astype(BF16)
    before = base_ref[...] + jnp.dot(member.astype(BF16), strict, preferred_element_type=F32)
    rk_ref[...] = jnp.concatenate(
        [jnp.sum(jnp.where(oh, before, 0.0), axis=0, keepdims=True) for oh in onehots], axis=0).astype(I32)
    new_base = base_ref[...] + jnp.sum(member, axis=1, keepdims=True)
    base_ref[...] = new_base
    cnt_ref[...] = jnp.broadcast_to(new_base, cnt_ref.shape)


def _post(o_f, o_b, gg, o_att, x2, mod, g_gla, g_pm, g_pf, wout, wrh, wrl, br):
    t, d = x2.shape
    tm = TM_POST
    tiles_per_seq = SEQ // tm
    hv = GLA_HEADS * GLA_DV

    def full(a):
        return pl.BlockSpec(a.shape, lambda i: (0,) * a.ndim)

    def rows(w):
        return pl.BlockSpec((tm, w), lambda i: (i, 0))

    def lanes():
        return pl.BlockSpec((TOP_K, tm), lambda i: (0, i))

    return pl.pallas_call(
        _post_body,
        grid=(t // tm,),
        in_specs=[
            rows(hv), rows(hv), rows(hv), rows(hv), rows(d),
            pl.BlockSpec((None, 6, d), lambda i: (i // tiles_per_seq, 0, 0)),
            full(g_gla), full(g_pm), full(g_pf), full(wout), full(wrh), full(wrl), full(br),
        ],
        out_specs=[
            rows(d),
            pl.BlockSpec((tm * PACK_SUB, LANES), lambda i: (i, 0)),
            lanes(), lanes(), lanes(),
            pl.BlockSpec((N_EXPERTS, LANES), lambda i: (0, 0)),
        ],
        out_shape=[
            jax.ShapeDtypeStruct((t, d), F32),
            jax.ShapeDtypeStruct((t * PACK_SUB, LANES), I32),
            jax.ShapeDtypeStruct((TOP_K, t), I32),
            jax.ShapeDtypeStruct((TOP_K, t), F32),
            jax.ShapeDtypeStruct((TOP_K, t), I32),
            jax.ShapeDtypeStruct((N_EXPERTS, LANES), F32),
        ],
        scratch_shapes=[pltpu.VMEM((N_EXPERTS, 1), F32)],
        compiler_params=_params(("arbitrary",), 48),
        name="post",
    )(o_f, o_b, gg, o_att, x2, mod, g_gla, g_pm, g_pf, wout, wrh, wrl, br)


def _route_body(ti_ref, rk_ref, cnt_ref, pos_ref, blk_ref):
    cnt = cnt_ref[...]
    padded = jnp.floor((cnt + (MOE_BM - 1)) * (1.0 / MOE_BM)) * MOE_BM
    starts, ends = [], []
    acc = jnp.zeros((1, LANES), F32)
    for e in range(N_EXPERTS):
        starts.append(acc)
        acc = acc + padded[e:e + 1, :]
        ends.append(acc)
    ti = ti_ref[...]
    off = jnp.zeros(ti.shape, F32)
    for e in range(N_EXPERTS):
        off = jnp.where(ti == e, starts[e][:, 0:1], off)
    pos_ref[...] = rk_ref[...] + off.astype(I32)

    def owner_of(row):
        n_le = jnp.zeros(row.shape, I32)
        for e in range(N_EXPERTS):
            n_le = n_le + jnp.where(ends[e][:, 0:1] <= row, 1, 0)
        return jnp.minimum(n_le, N_EXPERTS - 1)

    block_start = lax.broadcasted_iota(I32, (1, MOE_NB_PAD), 1).astype(F32) * MOE_BM
    owner = owner_of(block_start)
    nxt = jnp.zeros((1, MOE_NB_PAD), I32)
    for e in range(N_EXPERTS):
        end_e = ends[e][:, 0:1]
        nxt = jnp.where(owner == e, jnp.where(end_e < acc[:, 0:1], owner_of(end_e), -1), nxt)
    used = jnp.broadcast_to((acc[:, 0:1] * (1.0 / MOE_BM)).astype(I32), (1, MOE_NB_PAD))
    blk_ref[...] = jnp.concatenate([owner, nxt, used, jnp.zeros((SUBLANES - 3, MOE_NB_PAD), I32)], axis=0)


def _route(top_i, rank, counts):
    return pl.pallas_call(
        _route_body,
        out_shape=[
            jax.ShapeDtypeStruct(top_i.shape, I32),
            jax.ShapeDtypeStruct((SUBLANES, MOE_NB_PAD), I32),
        ],
        compiler_params=pltpu.CompilerParams(vmem_limit_bytes=32 * 1024 * 1024),
        name="route",
    )(top_i, rank, counts)


def _sc_workers():
    info = plsc.get_sparse_core_info()
    return info.num_cores, info.num_subcores, info.num_lanes


def _sc_gather_loop(table_hbm, out_hbm, idx_v, base, chunks, buf0, buf1, sem0, sem1):
    window = SC_GATHER_WINDOW

    def fetch(c, buf, sem):
        return pltpu.make_async_copy(table_hbm.at[idx_v.at[pl.ds(c * window, window)]], buf, sem)

    def flush(c, buf):
        pltpu.sync_copy(buf, out_hbm.at[pl.ds(base + c * window, window)])

    fetch(0, buf0, sem0).start()

    @pl.loop(0, chunks, step=2)
    def _(c):
        fetch(c + 1, buf1, sem1).start()
        fetch(c, buf0, sem0).wait()
        flush(c, buf0)

        @pl.when(c + 2 < chunks)
        def _():
            fetch(c + 2, buf0, sem0).start()

        fetch(c + 1, buf1, sem1).wait()
        flush(c + 1, buf1)


def _sc_dispatch_rows(table, pos_flat, n_rows):
    cores, subcores, lanes = _sc_workers()
    workers = cores * subcores
    window = SC_GATHER_WINDOW
    per_worker = n_rows // workers
    chunks = per_worker // window
    n_assign = pos_flat.shape[0]
    scan = SC_SCAN_CHUNK
    assert per_worker * workers == n_rows and chunks * window == per_worker and chunks % 2 == 0
    assert n_assign % scan == 0 and scan % lanes == 0 and per_worker % lanes == 0
    row_shape = table.shape[1:]
    mesh = plsc.VectorSubcoreMesh(core_axis_name="core", subcore_axis_name="subcore")

    @functools.partial(
        pl.kernel,
        out_type=jax.ShapeDtypeStruct((n_rows,) + row_shape, table.dtype),
        mesh=mesh,
        scratch_types=[
            pltpu.VMEM((per_worker,), I32),
            pltpu.VMEM((scan,), I32),
            pltpu.VMEM((window,) + row_shape, table.dtype),
            pltpu.VMEM((window,) + row_shape, table.dtype),
            pltpu.SemaphoreType.DMA,
            pltpu.SemaphoreType.DMA,
        ],
        compiler_params=pltpu.CompilerParams(needs_layout_passes=False),
        name="sc_dispatch_rows",
    )
    def dispatch(table_hbm, pos_hbm, out_hbm, src_v, pos_v, buf0, buf1, sem0, sem1):
        wid = lax.axis_index("subcore") * cores + lax.axis_index("core")
        base = wid * per_worker
        lane = lax.iota(I32, lanes)

        @pl.loop(0, per_worker, step=lanes)
        def _(j):
            src_v[pl.ds(j, lanes)] = (base + j + lane) & (TOKENS - 1)

        @pl.loop(0, n_assign, step=scan)
        def _(a0):
            pltpu.sync_copy(pos_hbm.at[pl.ds(a0, scan)], pos_v)

            @plsc.parallel_loop(0, scan, step=lanes, unroll=SC_SCAN_UNROLL)
            def _(j):
                rel = pos_v[pl.ds(j, lanes)] - base
                mine = (rel >= 0) & (rel < per_worker)
                tok = (a0 + j + lane) & (TOKENS - 1)
                plsc.store_scatter(src_v, [jnp.where(mine, rel, 0)], tok, mask=mine)

        _sc_gather_loop(table_hbm, out_hbm, src_v, base, chunks, buf0, buf1, sem0, sem1)

    return dispatch(table, pos_flat)


def _sc_gather_rows(table, idx):
    cores, subcores, _ = _sc_workers()
    workers = cores * subcores
    n = idx.shape[0]
    window = SC_GATHER_WINDOW
    per_worker = n // workers
    chunks = per_worker // window
    assert per_worker * workers == n and chunks * window == per_worker and chunks % 2 == 0
    row_shape = table.shape[1:]
    mesh = plsc.VectorSubcoreMesh(core_axis_name="core", subcore_axis_name="subcore")

    @functools.partial(
        pl.kernel,
        out_type=jax.ShapeDtypeStruct((n,) + row_shape, table.dtype),
        mesh=mesh,
        scratch_types=[
            pltpu.VMEM((per_worker,), I32),
            pltpu.VMEM((window,) + row_shape, table.dtype),
            pltpu.VMEM((window,) + row_shape, table.dtype),
            pltpu.SemaphoreType.DMA,
            pltpu.SemaphoreType.DMA,
        ],
        name="sc_gather_rows",
    )
    def gather(table_hbm, idx_hbm, out_hbm, idx_v, buf0, buf1, sem0, sem1):
        wid = lax.axis_index("subcore") * cores + lax.axis_index("core")
        base = wid * per_worker
        pltpu.sync_copy(idx_hbm.at[pl.ds(base, per_worker)], idx_v)
        _sc_gather_loop(table_hbm, out_hbm, idx_v, base, chunks, buf0, buf1, sem0, sem1)

    return gather(table, idx)


def _experts_body(be_ref, nxt_ref, nu_ref, xs_ref, bgu_ref, bd_ref, wgu_hbm, wd_hbm, ys_ref,
                  wgu_f32, wd_f32, wgu_bf, wd_bf, sems):
    step = pl.program_id(0)
    bm = MOE_BM

    def fetch(e):
        return (pltpu.make_async_copy(wgu_hbm.at[e], wgu_f32, sems.at[0]),
                pltpu.make_async_copy(wd_hbm.at[e], wd_f32, sems.at[1]))

    @pl.when(step == 0)
    def _():
        for cp in fetch(be_ref[0]):
            cp.start()

    for sub in range(MOE_SUB):
        i = step * MOE_SUB + sub
        e = be_ref[i]
        used = i < nu_ref[0]
        fresh = (i == 0) | (e != be_ref[jnp.maximum(i - 1, 0)])
        rows = pl.ds(sub * bm * PACK_SUB, bm * PACK_SUB)
        xs_sub = xs_ref.at[rows, :]
        ys_sub = ys_ref.at[rows, :]

        @pl.when(used & fresh)
        def _():
            for cp in fetch(e):
                cp.wait()
            wgu_bf[...] = wgu_f32[...].astype(BF16)
            wd_bf[...] = wd_f32[...].astype(BF16)

            @pl.when(nxt_ref[i] >= 0)
            def _():
                for cp in fetch(nxt_ref[i]):
                    cp.start()

        @pl.when(used)
        def _():
            x = jnp.concatenate(_unpack_rows(xs_sub, bm), axis=1).astype(BF16)
            gu = jnp.dot(x, wgu_bf[...], preferred_element_type=F32) + bgu_ref[pl.ds(e, 1), :]
            gate = jnp.minimum(gu[:, 0:D_FF], SWIGLU_LIMIT)
            up = jnp.clip(gu[:, D_FF:2 * D_FF], -SWIGLU_LIMIT, SWIGLU_LIMIT)
            act = ((up + 1.0) * gate * jax.nn.sigmoid(SWIGLU_ALPHA * gate)).astype(BF16)
            y = jnp.dot(act, wd_bf[...], preferred_element_type=F32) + bd_ref[pl.ds(e, 1), :]
            _pack_rows(ys_sub, y.astype(BF16).astype(F32))

        @pl.when(jnp.logical_not(used))
        def _():
            ys_sub[...] = jnp.zeros((bm * PACK_SUB, LANES), I32)


def _experts(block_expert, block_next, n_used, xs, w_gate_up, b_gate_up, w_down, b_down):
    rows = MOE_SUB * MOE_BM * PACK_SUB
    d = D_MODEL

    def last_used_step(i, be, nx, nu):
        return jnp.minimum(i, (nu[0] - 1) // MOE_SUB)

    grid_spec = pltpu.PrefetchScalarGridSpec(
        num_scalar_prefetch=3,
        grid=(MOE_NB // MOE_SUB,),
        in_specs=[
            pl.BlockSpec((rows, LANES), lambda i, be, nx, nu: (last_used_step(i, be, nx, nu), 0)),
            pl.BlockSpec(b_gate_up.shape, lambda i, be, nx, nu: (0, 0)),
            pl.BlockSpec(b_down.shape, lambda i, be, nx, nu: (0, 0)),
            pl.BlockSpec(memory_space=pl.ANY),
            pl.BlockSpec(memory_space=pl.ANY),
        ],
        out_specs=pl.BlockSpec((rows, LANES), lambda i, be, nx, nu: (i, 0)),
        scratch_shapes=[
            pltpu.VMEM((d, 2 * D_FF), F32), pltpu.VMEM((D_FF, d), F32),
            pltpu.VMEM((d, 2 * D_FF), BF16), pltpu.VMEM((D_FF, d), BF16),
            pltpu.SemaphoreType.DMA((2,)),
        ],
    )
    return pl.pallas_call(
        _experts_body,
        grid_spec=grid_spec,
        out_shape=jax.ShapeDtypeStruct((MOE_ROWS * PACK_SUB, LANES), I32),
        compiler_params=_params(("arbitrary",), 48),
        name="experts",
    )(block_expert, block_next, n_used, xs, b_gate_up, b_down, w_gate_up, w_down)


def _combine_body(gates_ref, x1_ref, mod_ref, gpost_ref, y0_ref, y1_ref, y2_ref, y3_ref, o_ref):
    tm = TM_COMB
    gates = gates_ref[...]
    y_hi = jnp.zeros((tm, PACK_COLS), F32)
    y_lo = jnp.zeros((tm, PACK_COLS), F32)
    for k, yk_ref in enumerate((y0_ref, y1_ref, y2_ref, y3_ref)):
        hi, lo = _unpack_rows(yk_ref, tm)
        y_hi = y_hi + hi * gates[:, k:k + 1]
        y_lo = y_lo + lo * gates[:, k:k + 1]
    y = jnp.concatenate([y_hi, y_lo], axis=1)
    gate2 = mod_ref[5:6, :]
    o_ref[...] = x1_ref[...] + gate2 * _rms(y, gpost_ref[...])


def _combine(gates_t, x1, mod, g_post, y4):
    t, d = x1.shape
    tm = TM_COMB
    tiles = t // tm
    tiles_per_seq = SEQ // tm

    def slab(k):
        return pl.BlockSpec((tm * PACK_SUB, LANES), lambda i: (k * tiles + i, 0))

    return pl.pallas_call(
        _combine_body,
        grid=(tiles,),
        in_specs=[
            pl.BlockSpec((tm, TOP_K), lambda i: (i, 0)),
            pl.BlockSpec((tm, d), lambda i: (i, 0)),
            pl.BlockSpec((None, 6, d), lambda i: (i // tiles_per_seq, 0, 0)),
            pl.BlockSpec(g_post.shape, lambda i: (0, 0)),
            slab(0), slab(1), slab(2), slab(3),
        ],
        out_specs=pl.BlockSpec((tm, d), lambda i: (i, 0)),
        out_shape=jax.ShapeDtypeStruct((t, d), F32),
        compiler_params=_params(("arbitrary",), 48),
        name="combine",
    )(gates_t, x1, mod, g_post, y4, y4, y4, y4)


def _rotary_tables():
    half = ROT_DIM // 2
    inv_freq = ROPE_THETA ** (-2.0 * np.arange(half, dtype=np.float32) / ROT_DIM)
    ang = np.arange(SEQ, dtype=np.float32)[:, None] * inv_freq[None, :].astype(np.float32)
    cos, sin = np.cos(ang), np.sin(ang)
    ones = np.ones((SEQ, ATT_HEAD_DIM - ROT_DIM), np.float32)
    zeros = np.zeros((SEQ, ATT_HEAD_DIM - ROT_DIM), np.float32)
    zh = np.zeros((SEQ, half), np.float32)
    reps = LANES // ATT_HEAD_DIM
    rc = np.tile(np.concatenate([cos, cos, ones], axis=1), (1, reps))
    rm = np.tile(np.concatenate([-sin, zh, zeros], axis=1), (1, reps))
    rp = np.tile(np.concatenate([zh, sin, zeros], axis=1), (1, reps))
    return tuple(jnp.asarray(t, F32) for t in (rc, rm, rp))


def _mixer_inputs(w_in, w_gk_fwd, b_gk_fwd, w_gk_bwd, b_gk_bwd):
    hk = GLA_HEADS * GLA_DK
    hv = GLA_HEADS * GLA_DV
    w = w_in[0]
    o_lr = 2 * hk + 2 * hv
    o_aq = o_lr + 2 * GLA_RANK
    o_ak = o_aq + ATT_Q_HEADS * ATT_HEAD_DIM
    o_av = o_ak + ATT_KV_HEADS * ATT_HEAD_DIM
    hd = ATT_HEAD_DIM
    wa = w[:, :o_lr].astype(BF16)
    wlr = w[:, o_lr:o_aq].astype(BF16)
    dup = lambda m: jnp.concatenate([m[:, g * hd:(g + 1) * hd] for g in range(ATT_KV_HEADS) for _ in range(2)], axis=1)
    wb = jnp.concatenate([w[:, o_aq:o_ak], dup(w[:, o_ak:o_av])], axis=1).astype(BF16)
    wvt = dup(w[:, o_av:o_av + ATT_KV_HEADS * hd]).T.astype(BF16)
    zr = jnp.zeros((GLA_RANK, hk), F32)
    wgk = jnp.concatenate([jnp.concatenate([w_gk_fwd[0], zr], axis=1),
                           jnp.concatenate([zr, w_gk_bwd[0]], axis=1)], axis=0).astype(BF16)
    bgk = jnp.concatenate([b_gk_fwd[0], b_gk_bwd[0]])[None, :]
    return (wa, wlr, wgk, bgk, wb, wvt) + _rotary_tables()


def kernel(x, c, w_ada, b_ada, g_pre_mix, g_post_mix, w_in, w_gk_fwd, b_gk_fwd, w_gk_bwd, b_gk_bwd, g_gla_out,
           attn_sink, w_out, g_pre_ffn, g_post_ffn, w_router, b_router, w_gate_up, b_gate_up, w_down, b_down):
    assert x.shape == (BATCH, SEQ, D_MODEL) and w_ada.shape[0] == 1
    d = D_MODEL
    x2 = x.reshape(TOKENS, d)

    c_pad = jnp.pad(c, ((0, SUBLANES - BATCH), (0, 0)))
    mod = _ada(c_pad, w_ada[0], b_ada)[:BATCH].reshape(BATCH, 6, d)

    mixer_in = _mixer_inputs(w_in, w_gk_fwd, b_gk_fwd, w_gk_bwd, b_gk_bwd)
    q, k, v, gg, laf, lab, aq, ak2, avt = _inproj(x2, mod, g_pre_mix, *mixer_in)
    o_f, o_b = _gla(q, k, v, laf, lab)
    o_att = _attn(attn_sink[0], aq, ak2, avt)

    wr_t = w_router[0].T
    wrh = wr_t.astype(BF16)
    wrl = (wr_t - wrh.astype(F32)).astype(BF16)
    x1, h2_tiles, top_i, gates, rank, counts = _post(
        o_f, o_b, gg, o_att, x2, mod, g_gla_out, g_post_mix, g_pre_ffn, w_out[0].astype(BF16), wrh, wrl,
        b_router[0][:, None])

    pos, blocks = _route(top_i, rank, counts)
    pos_flat = pos.reshape(TOP_K * TOKENS)
    xs = _sc_dispatch_rows(h2_tiles.reshape(TOKENS, PACK_SUB, LANES), pos_flat, MOE_ROWS)
    xs = xs.reshape(MOE_ROWS * PACK_SUB, LANES)
    ys = _experts(blocks[0, :MOE_NB], blocks[1, :MOE_NB], blocks[2, :1], xs,
                  w_gate_up[0], b_gate_up[0], w_down[0], b_down[0])
    y4 = _sc_gather_rows(ys.reshape(MOE_ROWS, PACK_SUB, LANES), pos_flat)
    out = _combine(gates.T, x1, mod, g_post_ffn, y4.reshape(TOP_K * TOKENS * PACK_SUB, LANES))
    return out.reshape(BATCH, SEQ, d)
```

```python
import functools

import jax
import jax.numpy as jnp
import numpy as np
from jax import lax
from jax.experimental import pallas as pl
from jax.experimental.pallas import tpu as pltpu
from jax.experimental.pallas import tpu_sc as plsc

F32 = jnp.float32
BF16 = jnp.bfloat16
I32 = jnp.int32

D_MODEL = 1024
BATCH = 2
SEQ = 8192
TOKENS = BATCH * SEQ
GLA_HEADS = 4
GLA_DV = 128
GLA_DK = 64
GLA_RANK = 16
GLA_GATE_NORMALIZER = 16.0
GLA_CHUNK = 64
ATT_Q_HEADS = 8
ATT_KV_HEADS = 2
ATT_HEAD_DIM = 64
ATT_WINDOW = 128
ATT_BLOCK = 128
ROT_DIM = 16
ROPE_THETA = 500000.0
N_EXPERTS = 32
TOP_K = 4
D_FF = 1024
SWIGLU_LIMIT = 7.0
SWIGLU_ALPHA = 1.702
NORM_EPS = 1e-6
NEG_INF = -1e30

LANES = 128
SUBLANES = 8
PACK_COLS = D_MODEL // 2
PACK_SUB = PACK_COLS // LANES

TM_IN = 512
GLA_GROUP = 8
ATT_GROUP = 8
TM_POST = 512
MOE_BM = 256
MOE_ROWS = TOKENS * TOP_K + N_EXPERTS * MOE_BM
MOE_NB = MOE_ROWS // MOE_BM
MOE_SUB = 2
MOE_PARTS = 3
COMB_PARTS = 4
MOE_NB_PAD = ((MOE_NB + LANES - 1) // LANES) * LANES
SC_SCAN_CHUNK = 4096
SC_SCAN_UNROLL = 8
TM_COMB = 256
SC_GATHER_WINDOW = 64

NT_DIMS = (((1,), (1,)), ((), ()))
TN_DIMS = (((0,), (0,)), ((), ()))


def _params(semantics, vmem_mib):
    return pltpu.CompilerParams(dimension_semantics=semantics, vmem_limit_bytes=vmem_mib * 1024 * 1024)


def _rms(x, g):
    return x * lax.rsqrt(jnp.mean(x * x, axis=-1, keepdims=True) + NORM_EPS) * g


def _silu(x):
    return x * jax.nn.sigmoid(x)


def _pack_rows(ref, v):
    m = v.shape[0]
    bits = lax.bitcast_convert_type(v, jnp.uint32)
    word = lax.bitcast_convert_type(bits[:, :PACK_COLS] | (bits[:, PACK_COLS:] >> 16), I32)
    for s in range(PACK_SUB):
        ref[pl.ds(s, m, stride=PACK_SUB), :] = word[:, s * LANES:(s + 1) * LANES]


def _unpack_rows(ref, m):
    word = jnp.concatenate([ref[pl.ds(s, m, stride=PACK_SUB), :] for s in range(PACK_SUB)], axis=1)
    bits = lax.bitcast_convert_type(word, jnp.uint32)
    hi = lax.bitcast_convert_type(bits & jnp.uint32(0xFFFF0000), F32)
    lo = lax.bitcast_convert_type(bits << 16, F32)
    return hi, lo


def _ada_body(c_ref, w_ref, b_ref, o_ref):
    ca = _silu(c_ref[...]).astype(BF16)
    o_ref[...] = jnp.dot(ca, w_ref[...].astype(BF16), preferred_element_type=F32) + b_ref[...]


def _ada(c_pad, w_ada, b_ada):
    d = D_MODEL
    return pl.pallas_call(
        _ada_body,
        grid=(6,),
        in_specs=[
            pl.BlockSpec((SUBLANES, d), lambda j: (0, 0)),
            pl.BlockSpec((d, d), lambda j: (0, j)),
            pl.BlockSpec((1, d), lambda j: (0, j)),
        ],
        out_specs=pl.BlockSpec((SUBLANES, d), lambda j: (0, j)),
        out_shape=jax.ShapeDtypeStruct((SUBLANES, 6 * d), F32),
        compiler_params=_params(("arbitrary",), 32),
        name="ada",
    )(c_pad, w_ada, b_ada)


def _rotary(x, cos_t, msin_t, psin_t):
    width = x.shape[1]
    reps = width // LANES
    c = jnp.concatenate([cos_t] * reps, axis=1)
    m = jnp.concatenate([msin_t] * reps, axis=1)
    p = jnp.concatenate([psin_t] * reps, axis=1)
    half = ROT_DIM // 2
    return x * c + pltpu.roll(x, width - half, 1) * m + pltpu.roll(x, half, 1) * p


def _inproj_body(x_ref, mod_ref, g_ref, wa_ref, wlr_ref, wgk_ref, bgk_ref, wb_ref, wvt_ref, rc_ref, rm_ref, rp_ref,
                 q_ref, k_ref, v_ref, gg_ref, laf_ref, lab_ref, aq_ref, ak_ref, avt_ref):
    shift = mod_ref[0:1, :]
    scale = mod_ref[1:2, :]
    h = (_rms(x_ref[...], g_ref[...]) * (1.0 + scale) + shift).astype(BF16)

    hk = GLA_HEADS * GLA_DK
    hv = GLA_HEADS * GLA_DV
    pa = jnp.dot(h, wa_ref[...], preferred_element_type=F32)
    q_ref[...] = pa[:, 0:hk] * (GLA_DK ** -0.5)
    k_ref[...] = pa[:, hk:2 * hk]
    v_ref[...] = pa[:, 2 * hk:2 * hk + hv].astype(BF16)
    gg_ref[...] = pa[:, 2 * hk + hv:2 * hk + 2 * hv]

    plr = jnp.dot(h, wlr_ref[...], preferred_element_type=F32)
    gk = jnp.dot(plr.astype(BF16), wgk_ref[...], preferred_element_type=F32) + bgk_ref[...]
    la = (jnp.minimum(gk, 0.0) - jnp.log1p(jnp.exp(-jnp.abs(gk)))) * (1.0 / GLA_GATE_NORMALIZER)
    laf_ref[...] = la[:, 0:hk]
    lab_ref[...] = la[:, hk:2 * hk]

    pb = jnp.dot(h, wb_ref[...], preferred_element_type=F32)
    nq = ATT_Q_HEADS * ATT_HEAD_DIM
    nk = 2 * ATT_KV_HEADS * ATT_HEAD_DIM
    rc, rm, rp = rc_ref[...], rm_ref[...], rp_ref[...]
    aq_ref[...] = (_rotary(pb[:, 0:nq], rc, rm, rp) * (ATT_HEAD_DIM ** -0.5)).astype(BF16)
    ak_ref[...] = _rotary(pb[:, nq:nq + nk], rc, rm, rp).astype(BF16)
    avt_ref[...] = lax.dot_general(wvt_ref[...], h, NT_DIMS, preferred_element_type=F32).astype(BF16)


def _inproj(x2, mod, g_pre, wa, wlr, wgk, bgk, wb, wvt, rc, rm, rp):
    t, d = x2.shape
    tm = TM_IN
    tiles_per_seq = SEQ // tm
    hk = GLA_HEADS * GLA_DK
    hv = GLA_HEADS * GLA_DV
    nq = ATT_Q_HEADS * ATT_HEAD_DIM
    nk = 2 * ATT_KV_HEADS * ATT_HEAD_DIM

    def full(a):
        return pl.BlockSpec(a.shape, lambda i: (0,) * a.ndim)

    def rows(w):
        return pl.BlockSpec((tm, w), lambda i: (i, 0))

    def table():
        return pl.BlockSpec((tm, LANES), lambda i: (i % tiles_per_seq, 0))

    out_widths = [(hk, F32), (hk, F32), (hv, BF16), (hv, F32), (hk, F32), (hk, F32), (nq, BF16), (nk, BF16)]
    return pl.pallas_call(
        _inproj_body,
        grid=(t // tm,),
        in_specs=[
            rows(d),
            pl.BlockSpec((None, 6, d), lambda i: (i // tiles_per_seq, 0, 0)),
            full(g_pre), full(wa), full(wlr), full(wgk), full(bgk), full(wb), full(wvt),
            table(), table(), table(),
        ],
        out_specs=[rows(w) for w, _ in out_widths] + [pl.BlockSpec((nk, tm), lambda i: (0, i))],
        out_shape=[jax.ShapeDtypeStruct((t, w), dt) for w, dt in out_widths] + [jax.ShapeDtypeStruct((nk, t), BF16)],
        compiler_params=_params(("arbitrary",), 56),
        name="inproj",
    )(x2, mod, g_pre, wa, wlr, wgk, bgk, wb, wvt, rc, rm, rp)


def _gla_body(qf_ref, kf_ref, vf_ref, laf_ref, qb_ref, kb_ref, vb_ref, lab_ref, of_ref, ob_ref, sf_ref, sb_ref):
    @pl.when(pl.program_id(1) == 0)
    def _():
        sf_ref[...] = jnp.zeros_like(sf_ref)
        sb_ref[...] = jnp.zeros_like(sb_ref)

    c = GLA_CHUNK
    r_i = lax.broadcasted_iota(I32, (c, c), 0)
    c_i = lax.broadcasted_iota(I32, (c, c), 1)
    lower = c_i <= r_i
    upper = c_i >= r_i
    cum_f = jnp.where(lower, 1.0, 0.0).astype(BF16)
    cum_b = jnp.where(upper, 1.0, 0.0).astype(BF16)
    lane = lax.broadcasted_iota(I32, (1, LANES), 1)
    head_masks = (lane < GLA_DK, lane >= GLA_DK)

    fwd = [(qf_ref, kf_ref, laf_ref, vf_ref, of_ref, slice(g * c, (g + 1) * c), cum_f, lower, c - 1, c // 2 - 1)
           for g in range(GLA_GROUP)]
    bwd = [(qb_ref, kb_ref, lab_ref, vb_ref, ob_ref, slice(g * c, (g + 1) * c), cum_b, upper, 0, c // 2)
           for g in reversed(range(GLA_GROUP))]
    heads = range(GLA_HEADS)
    pair = [slice((h // 2) * LANES, (h // 2 + 1) * LANES) for h in heads]
    vcols = [slice(h * GLA_DV, (h + 1) * GLA_DV) for h in heads]

    def stage1(item):
        q_ref, k_ref, la_ref, v_ref, o_ref, rows, cum, tri, i_last, i_mid = item
        la = la_ref[rows, :]
        hi = la.astype(BF16)
        lo = (la - hi.astype(F32)).astype(BF16)
        b = jnp.dot(cum, hi, preferred_element_type=F32) + jnp.dot(cum, lo, preferred_element_type=F32)
        b_last = b[i_last:i_last + 1, :]
        b_mid = b[i_mid:i_mid + 1, :]
        q, k = q_ref[rows, :], k_ref[rows, :]
        return (q * jnp.exp(b - b_mid), (k * jnp.exp(b_mid - b)).astype(BF16), q * jnp.exp(b),
                (k * jnp.exp(b_last - b)).astype(BF16), jnp.exp(b_last))

    def stage2(item, pre):
        v_ref, rows, tri = item[3], item[5], item[7]
        qs, ks, qi, kst, decay = pre
        out = []
        for h in heads:
            mask = head_masks[h % 2]
            qs_h = jnp.where(mask, qs[:, pair[h]], 0.0).astype(BF16)
            sc = lax.dot_general(qs_h, ks[:, pair[h]], NT_DIMS, preferred_element_type=F32)
            v_h = v_ref[rows, vcols[h]]
            kv = lax.dot_general(v_h, kst[:, pair[h]], TN_DIMS, preferred_element_type=F32)
            out.append((jnp.where(tri, sc, 0.0).astype(BF16), kv,
                        jnp.where(mask, qi[:, pair[h]], 0.0).astype(BF16), v_h))
        return out

    def run(items, s_ref):
        pre = [stage1(it) for it in items]
        mid = [stage2(it, p) for it, p in zip(items, pre)]
        states = [s_ref[h] for h in heads]
        for it, p, m in zip(items, pre, mid):
            o_ref, rows, decay = it[4], it[5], p[4]
            for h in heads:
                sc, kv, qi_h, v_h = m[h]
                o = jnp.dot(sc, v_h, preferred_element_type=F32)
                o = o + lax.dot_general(qi_h, states[h].astype(BF16), NT_DIMS, preferred_element_type=F32)
                o_ref[rows, vcols[h]] = o
                states[h] = states[h] * decay[:, pair[h]] + kv
        for h in heads:
            s_ref[h] = states[h]

    run(fwd, sf_ref)
    run(bwd, sb_ref)


def _gla(q, k, v, laf, lab):
    t = q.shape[0]
    rows = GLA_GROUP * GLA_CHUNK
    ng = SEQ // rows
    hk = GLA_HEADS * GLA_DK
    hv = GLA_HEADS * GLA_DV

    def fwd(w):
        return pl.BlockSpec((rows, w), lambda b, n: (b * ng + n, 0))

    def bwd(w):
        return pl.BlockSpec((rows, w), lambda b, n: (b * ng + ng - 1 - n, 0))

    return pl.pallas_call(
        _gla_body,
        grid=(BATCH, ng),
        in_specs=[fwd(hk), fwd(hk), fwd(hv), fwd(hk), bwd(hk), bwd(hk), bwd(hv), bwd(hk)],
        out_specs=[fwd(hv), bwd(hv)],
        out_shape=[jax.ShapeDtypeStruct((t, hv), F32)] * 2,
        scratch_shapes=[pltpu.VMEM((GLA_HEADS, GLA_DV, 2 * GLA_DK), F32)] * 2,
        compiler_params=_params(("arbitrary", "arbitrary"), 32),
        name="gla",
    )(q, k, v, laf, q, k, v, lab)


def _attn_body(sink_ref, q_ref, kp_ref, kc_ref, kn_ref, vp_ref, vc_ref, vn_ref, o_ref):
    step = pl.program_id(1)
    last = pl.num_programs(1) - 1
    qb = ATT_BLOCK
    hd = ATT_HEAD_DIM
    k_all = jnp.concatenate([kp_ref[...], kc_ref[...], kn_ref[...]], axis=0)
    vt_all = jnp.concatenate([vp_ref[...], vc_ref[...], vn_ref[...]], axis=1)
    lane = lax.broadcasted_iota(I32, (1, LANES), 1)
    lo = lane < hd
    j_k = lax.broadcasted_iota(I32, (3 * qb, qb), 0)
    i_q = lax.broadcasted_iota(I32, (3 * qb, qb), 1)
    band = jnp.abs(j_k - qb - i_q) <= ATT_WINDOW
    sinks = [jnp.concatenate([jnp.full((1, qb), sink_ref[4 * g + r], F32) for r in range(4)], axis=1)
             for g in range(ATT_KV_HEADS)]
    work = [(j, g) for j in range(ATT_GROUP) for g in range(ATT_KV_HEADS)]

    def scores(j, g):
        valid = band
        if j == 0:
            valid = valid & ((j_k >= qb) | (step > 0))
        if j == ATT_GROUP - 1:
            valid = valid & ((j_k < 2 * qb) | (step < last))
        valid4 = jnp.concatenate([valid] * 4, axis=1)
        rows = slice(j * qb, (j + 1) * qb)
        kg = k_all[j * qb:(j + 3) * qb, g * LANES:(g + 1) * LANES]
        qa = q_ref[rows, (2 * g) * LANES:(2 * g + 1) * LANES]
        qc = q_ref[rows, (2 * g + 1) * LANES:(2 * g + 2) * LANES]
        zero = jnp.zeros_like(qa)
        lhs = jnp.concatenate([jnp.where(lo, qa, zero), jnp.where(lo, zero, qa),
                               jnp.where(lo, qc, zero), jnp.where(lo, zero, qc)], axis=0)
        st = lax.dot_general(kg, lhs, NT_DIMS, preferred_element_type=F32)
        return jnp.where(valid4, st, NEG_INF)

    def softmax(st, g):
        sink = sinks[g]
        m = jnp.maximum(jnp.max(st, axis=0, keepdims=True), sink)
        p = jnp.exp(st - m)
        return p.astype(BF16), 1.0 / (jnp.sum(p, axis=0, keepdims=True) + jnp.exp(sink - m))

    def output(j, g, p, inv):
        rows = slice(j * qb, (j + 1) * qb)
        vgt = vt_all[g * LANES:(g + 1) * LANES, j * qb:(j + 3) * qb]
        ot = jnp.dot(vgt, p, preferred_element_type=F32) * inv
        pair_a = jnp.concatenate([ot[0:hd, 0:qb], ot[hd:2 * hd, qb:2 * qb]], axis=0)
        pair_c = jnp.concatenate([ot[0:hd, 2 * qb:3 * qb], ot[hd:2 * hd, 3 * qb:4 * qb]], axis=0)
        o_ref[rows, (2 * g) * LANES:(2 * g + 1) * LANES] = pair_a.T.astype(o_ref.dtype)
        o_ref[rows, (2 * g + 1) * LANES:(2 * g + 2) * LANES] = pair_c.T.astype(o_ref.dtype)

    s_all = [scores(j, g) for j, g in work]
    p_all = [softmax(st, g) for st, (j, g) in zip(s_all, work)]
    for (j, g), (p, inv) in zip(work, p_all):
        output(j, g, p, inv)


def _attn(sink, aq, ak2, avt):
    t = aq.shape[0]
    qb = ATT_BLOCK
    nb = SEQ // qb
    steps = nb // ATT_GROUP
    nq = ATT_Q_HEADS * ATT_HEAD_DIM
    nk = 2 * ATT_KV_HEADS * ATT_HEAD_DIM

    def edge_block(b, n, shift):
        return b * nb + jnp.clip(n * ATT_GROUP + shift, 0, nb - 1)

    def k_edge(shift):
        return pl.BlockSpec((qb, nk), lambda b, n: (edge_block(b, n, shift), 0))

    def v_edge(shift):
        return pl.BlockSpec((nk, qb), lambda b, n: (0, edge_block(b, n, shift)))

    def group(w):
        return pl.BlockSpec((ATT_GROUP * qb, w), lambda b, n: (b * steps + n, 0))

    v_group = pl.BlockSpec((nk, ATT_GROUP * qb), lambda b, n: (0, b * steps + n))
    return pl.pallas_call(
        _attn_body,
        grid=(BATCH, steps),
        in_specs=[
            pl.BlockSpec(memory_space=pltpu.SMEM),
            group(nq),
            k_edge(-1), group(nk), k_edge(ATT_GROUP), v_edge(-1), v_group, v_edge(ATT_GROUP),
        ],
        out_specs=group(nq),
        out_shape=jax.ShapeDtypeStruct((t, nq), BF16),
        compiler_params=_params(("arbitrary", "arbitrary"), 48),
        name="attn",
    )(sink, aq, ak2, ak2, ak2, avt, avt, avt)


def _post_body(of_ref, ob_ref, gg_ref, oa_ref, x_ref, mod_ref, ggla_ref, gpm_ref, gpf_ref, wout_ref,
               wrh_ref, wrl_ref, br_ref,
               x1_ref, h2_ref, ti_ref, gt_ref, rk_ref, cnt_ref, base_ref):
    tm = TM_POST

    @pl.when(pl.program_id(0) == 0)
    def _():
        base_ref[...] = jnp.zeros_like(base_ref)

    og = of_ref[...] + ob_ref[...]
    gg = gg_ref[...]
    parts = []
    for h in range(GLA_HEADS):
        cols = slice(h * GLA_DV, (h + 1) * GLA_DV)
        parts.append((_rms(og[:, cols], ggla_ref[...]) * _silu(gg[:, cols])).astype(BF16))
    o = jnp.concatenate(parts + [oa_ref[...]], axis=1)
    y = jnp.dot(o, wout_ref[...], preferred_element_type=F32)

    gate1 = mod_ref[2:3, :]
    shift2 = mod_ref[3:4, :]
    scale2 = mod_ref[4:5, :]
    x1 = x_ref[...] + gate1 * _rms(y, gpm_ref[...])
    x1_ref[...] = x1
    h2 = _rms(x1, gpf_ref[...]) * (1.0 + scale2) + shift2
    h2_hi = h2.astype(BF16)
    h2_hi32 = h2_hi.astype(F32)
    h2_lo = (h2 - h2_hi32).astype(BF16)
    _pack_rows(h2_ref, h2_hi32)

    wrh = wrh_ref[...]
    logits = (lax.dot_general(wrh, h2_hi, NT_DIMS, preferred_element_type=F32)
              + lax.dot_general(wrh, h2_lo, NT_DIMS, preferred_element_type=F32)
              + lax.dot_general(wrl_ref[...], h2_hi, NT_DIMS, preferred_element_type=F32)
              + br_ref[...])
    e_iota = lax.broadcasted_iota(I32, (N_EXPERTS, tm), 0)
    idxs, vals = [], []
    work = logits
    for _ in range(TOP_K):
        m = jnp.max(work, axis=0, keepdims=True)
        idx = jnp.min(jnp.where(work == m, e_iota, N_EXPERTS), axis=0, keepdims=True)
        idxs.append(idx)
        vals.append(m)
        work = jnp.where(e_iota == idx, -jnp.inf, work)
    exps = [jnp.exp(v - vals[0]) for v in vals]
    inv = 1.0 / (exps[0] + exps[1] + exps[2] + exps[3])
    gt_ref[...] = jnp.concatenate([e * inv for e in exps], axis=0)
    ti_ref[...] = jnp.concatenate(idxs, axis=0)

    onehots = [e_iota == idx for idx in idxs]
    member = jnp.where(onehots[0] | onehots[1] | onehots[2] | onehots[3], 1.0, 0.0)
    t_row = lax.broadcasted_iota(I32, (tm, tm), 0)
    t_col = lax.broadcasted_iota(I32, (tm, tm), 1)
    strict = jnp.where(t_row < t_col, 1.0, 0.0).astype(BF16)
    before = base_ref[...] + jnp.dot(member.astype(BF16), strict, preferred_element_type=F32)
    rk_ref[...] = jnp.concatenate(
        [jnp.sum(jnp.where(oh, before, 0.0), axis=0, keepdims=True) for oh in onehots], axis=0).astype(I32)
    new_base = base_ref[...] + jnp.sum(member, axis=1, keepdims=True)
    base_ref[...] = new_base
    cnt_ref[...] = jnp.broadcast_to(new_base, cnt_ref.shape)


def _post(o_f, o_b, gg, o_att, x2, mod, g_gla, g_pm, g_pf, wout, wrh, wrl, br):
    t, d = x2.shape
    tm = TM_POST
    tiles_per_seq = SEQ // tm
    hv = GLA_HEADS * GLA_DV

    def full(a):
        return pl.BlockSpec(a.shape, lambda i: (0,) * a.ndim)

    def rows(w):
        return pl.BlockSpec((tm, w), lambda i: (i, 0))

    def lanes():
        return pl.BlockSpec((TOP_K, tm), lambda i: (0, i))

    return pl.pallas_call(
        _post_body,
        grid=(t // tm,),
        in_specs=[
            rows(hv), rows(hv), rows(hv), rows(hv), rows(d),
            pl.BlockSpec((None, 6, d), lambda i: (i // tiles_per_seq, 0, 0)),
            full(g_gla), full(g_pm), full(g_pf), full(wout), full(wrh), full(wrl), full(br),
        ],
        out_specs=[
            rows(d),
            pl.BlockSpec((tm * PACK_SUB, LANES), lambda i: (i, 0)),
            lanes(), lanes(), lanes(),
            pl.BlockSpec((N_EXPERTS, LANES), lambda i: (0, 0)),
        ],
        out_shape=[
            jax.ShapeDtypeStruct((t, d), F32),
            jax.ShapeDtypeStruct((t * PACK_SUB, LANES), I32),
            jax.ShapeDtypeStruct((TOP_K, t), I32),
            jax.ShapeDtypeStruct((TOP_K, t), F32),
            jax.ShapeDtypeStruct((TOP_K, t), I32),
            jax.ShapeDtypeStruct((N_EXPERTS, LANES), F32),
        ],
        scratch_shapes=[pltpu.VMEM((N_EXPERTS, 1), F32)],
        compiler_params=_params(("arbitrary",), 48),
        name="post",
    )(o_f, o_b, gg, o_att, x2, mod, g_gla, g_pm, g_pf, wout, wrh, wrl, br)


def _route_body(ti_ref, rk_ref, cnt_ref, pos_ref, blk_ref):
    cnt = cnt_ref[...]
    padded = jnp.floor((cnt + (MOE_BM - 1)) * (1.0 / MOE_BM)) * MOE_BM
    starts, ends = [], []
    acc = jnp.zeros((1, LANES), F32)
    for e in range(N_EXPERTS):
        starts.append(acc)
        acc = acc + padded[e:e + 1, :]
        ends.append(acc)
    ti = ti_ref[...]
    off = jnp.zeros(ti.shape, F32)
    for e in range(N_EXPERTS):
        off = jnp.where(ti == e, starts[e][:, 0:1], off)
    pos_ref[...] = rk_ref[...] + off.astype(I32)

    def owner_of(row):
        n_le = jnp.zeros(row.shape, I32)
        for e in range(N_EXPERTS):
            n_le = n_le + jnp.where(ends[e][:, 0:1] <= row, 1, 0)
        return jnp.minimum(n_le, N_EXPERTS - 1)

    block_start = lax.broadcasted_iota(I32, (1, MOE_NB_PAD), 1).astype(F32) * MOE_BM
    owner = owner_of(block_start)
    nxt = jnp.zeros((1, MOE_NB_PAD), I32)
    nxt_blk = jnp.zeros((1, MOE_NB_PAD), I32)
    for e in range(N_EXPERTS):
        end_e = ends[e][:, 0:1]
        nxt = jnp.where(owner == e, jnp.where(end_e < acc[:, 0:1], owner_of(end_e), -1), nxt)
        nxt_blk = jnp.where(owner == e, (end_e * (1.0 / MOE_BM)).astype(I32), nxt_blk)
    used = jnp.broadcast_to((acc[:, 0:1] * (1.0 / MOE_BM)).astype(I32), (1, MOE_NB_PAD))
    blk_ref[...] = jnp.concatenate([owner, nxt, used, nxt_blk, jnp.zeros((SUBLANES - 4, MOE_NB_PAD), I32)], axis=0)


def _route(top_i, rank, counts):
    return pl.pallas_call(
        _route_body,
        out_shape=[
            jax.ShapeDtypeStruct(top_i.shape, I32),
            jax.ShapeDtypeStruct((SUBLANES, MOE_NB_PAD), I32),
        ],
        compiler_params=pltpu.CompilerParams(vmem_limit_bytes=32 * 1024 * 1024),
        name="route",
    )(top_i, rank, counts)


def _sc_workers():
    info = plsc.get_sparse_core_info()
    return info.num_cores, info.num_subcores, info.num_lanes


def _sc_gather_loop(table_hbm, out_hbm, idx_v, base, chunks, buf0, buf1, sem0, sem1):
    window = SC_GATHER_WINDOW

    def fetch(c, buf, sem):
        return pltpu.make_async_copy(table_hbm.at[idx_v.at[pl.ds(c * window, window)]], buf, sem)

    def flush(c, buf):
        pltpu.sync_copy(buf, out_hbm.at[pl.ds(base + c * window, window)])

    fetch(0, buf0, sem0).start()

    @pl.loop(0, chunks, step=2)
    def _(c):
        fetch(c + 1, buf1, sem1).start()
        fetch(c, buf0, sem0).wait()
        flush(c, buf0)

        @pl.when(c + 2 < chunks)
        def _():
            fetch(c + 2, buf0, sem0).start()

        fetch(c + 1, buf1, sem1).wait()
        flush(c + 1, buf1)


def _sc_source_rows(pos_flat, n_rows):
    cores, subcores, lanes = _sc_workers()
    workers = cores * subcores
    per_worker = n_rows // workers
    n_assign = pos_flat.shape[0]
    scan = SC_SCAN_CHUNK
    assert per_worker * workers == n_rows and per_worker % lanes == 0
    assert n_assign % scan == 0 and scan % lanes == 0
    mesh = plsc.VectorSubcoreMesh(core_axis_name="core", subcore_axis_name="subcore")

    @functools.partial(
        pl.kernel,
        out_type=jax.ShapeDtypeStruct((n_rows,), I32),
        mesh=mesh,
        scratch_types=[pltpu.VMEM((per_worker,), I32), pltpu.VMEM((scan,), I32)],
        compiler_params=pltpu.CompilerParams(needs_layout_passes=False),
        name="sc_source_rows",
    )
    def invert(pos_hbm, out_hbm, src_v, pos_v):
        wid = lax.axis_index("subcore") * cores + lax.axis_index("core")
        base = wid * per_worker
        lane = lax.iota(I32, lanes)

        @pl.loop(0, per_worker, step=lanes)
        def _(j):
            src_v[pl.ds(j, lanes)] = (base + j + lane) & (TOKENS - 1)

        @pl.loop(0, n_assign, step=scan)
        def _(a0):
            pltpu.sync_copy(pos_hbm.at[pl.ds(a0, scan)], pos_v)

            @plsc.parallel_loop(0, scan, step=lanes, unroll=SC_SCAN_UNROLL)
            def _(j):
                rel = pos_v[pl.ds(j, lanes)] - base
                mine = (rel >= 0) & (rel < per_worker)
                tok = (a0 + j + lane) & (TOKENS - 1)
                plsc.store_scatter(src_v, [jnp.where(mine, rel, 0)], tok, mask=mine)

        pltpu.sync_copy(src_v, out_hbm.at[pl.ds(base, per_worker)])

    return invert(pos_flat)


def _sc_gather_rows(table, idx):
    cores, subcores, _ = _sc_workers()
    workers = cores * subcores
    n = idx.shape[0]
    window = SC_GATHER_WINDOW
    per_worker = n // workers
    chunks = per_worker // window
    assert per_worker * workers == n and chunks * window == per_worker and chunks % 2 == 0
    row_shape = table.shape[1:]
    mesh = plsc.VectorSubcoreMesh(core_axis_name="core", subcore_axis_name="subcore")

    @functools.partial(
        pl.kernel,
        out_type=jax.ShapeDtypeStruct((n,) + row_shape, table.dtype),
        mesh=mesh,
        scratch_types=[
            pltpu.VMEM((per_worker,), I32),
            pltpu.VMEM((window,) + row_shape, table.dtype),
            pltpu.VMEM((window,) + row_shape, table.dtype),
            pltpu.SemaphoreType.DMA,
            pltpu.SemaphoreType.DMA,
        ],
        name="sc_gather_rows",
    )
    def gather(table_hbm, idx_hbm, out_hbm, idx_v, buf0, buf1, sem0, sem1):
        wid = lax.axis_index("subcore") * cores + lax.axis_index("core")
        base = wid * per_worker
        pltpu.sync_copy(idx_hbm.at[pl.ds(base, per_worker)], idx_v)
        _sc_gather_loop(table_hbm, out_hbm, idx_v, base, chunks, buf0, buf1, sem0, sem1)

    return gather(table, idx)


def _experts_body(first, be_ref, nxt_ref, nxtblk_ref, nu_ref, xs_ref, bgu_ref, bd_ref, wgu_hbm, wd_hbm, *rest):
    ys_ref, wgu_f32, wd_f32, wgu_bf, wd_bf, sems = rest[-6:]
    step = pl.program_id(0)
    end = first + pl.num_programs(0) * MOE_SUB
    bm = MOE_BM
    n_used = nu_ref[0]

    def fetch(e):
        return (pltpu.make_async_copy(wgu_hbm.at[e], wgu_f32, sems.at[0]),
                pltpu.make_async_copy(wd_hbm.at[e], wd_f32, sems.at[1]))

    @pl.when((step == 0) & (first < n_used))
    def _():
        for cp in fetch(be_ref[first]):
            cp.start()

    for sub in range(MOE_SUB):
        i = first + step * MOE_SUB + sub
        e = be_ref[i]
        used = i < n_used
        fresh = (i == first) | (e != be_ref[jnp.maximum(i - 1, 0)])
        rows = pl.ds(sub * bm * PACK_SUB, bm * PACK_SUB)
        xs_sub = xs_ref.at[rows, :]
        ys_sub = ys_ref.at[rows, :]

        @pl.when(used & fresh)
        def _():
            for cp in fetch(e):
                cp.wait()
            wgu_bf[...] = wgu_f32[...].astype(BF16)
            wd_bf[...] = wd_f32[...].astype(BF16)

            @pl.when((nxt_ref[i] >= 0) & (nxtblk_ref[i] < end))
            def _():
                for cp in fetch(nxt_ref[i]):
                    cp.start()

        @pl.when(used)
        def _():
            x = jnp.concatenate(_unpack_rows(xs_sub, bm), axis=1).astype(BF16)
            gu = jnp.dot(x, wgu_bf[...], preferred_element_type=F32) + bgu_ref[pl.ds(e, 1), :]
            gate = jnp.minimum(gu[:, 0:D_FF], SWIGLU_LIMIT)
            up = jnp.clip(gu[:, D_FF:2 * D_FF], -SWIGLU_LIMIT, SWIGLU_LIMIT)
            act = ((up + 1.0) * gate * jax.nn.sigmoid(SWIGLU_ALPHA * gate)).astype(BF16)
            y = jnp.dot(act, wd_bf[...], preferred_element_type=F32) + bd_ref[pl.ds(e, 1), :]
            _pack_rows(ys_sub, y.astype(BF16).astype(F32))

        @pl.when(jnp.logical_not(used))
        def _():
            ys_sub[...] = jnp.zeros((bm * PACK_SUB, LANES), I32)


def _experts(part, blocks, xs_part, w_gate_up, b_gate_up, w_down, b_down, ys_prev):
    rows = MOE_SUB * MOE_BM * PACK_SUB
    d = D_MODEL
    part_blocks = MOE_NB // MOE_PARTS
    steps = part_blocks // MOE_SUB
    first = part * part_blocks

    def x_block(i, be, nx, nb, nu):
        last = jnp.maximum((jnp.minimum(nu[0], first + part_blocks) - 1 - first) // MOE_SUB, 0)
        return jnp.minimum(i, last)

    in_specs = [
        pl.BlockSpec((rows, LANES), lambda i, be, nx, nb, nu: (x_block(i, be, nx, nb, nu), 0)),
        pl.BlockSpec(b_gate_up.shape, lambda i, be, nx, nb, nu: (0, 0)),
        pl.BlockSpec(b_down.shape, lambda i, be, nx, nb, nu: (0, 0)),
        pl.BlockSpec(memory_space=pl.ANY),
        pl.BlockSpec(memory_space=pl.ANY),
    ]
    operands = [blocks[0, :MOE_NB], blocks[1, :MOE_NB], blocks[3, :MOE_NB], blocks[2, :1],
                xs_part, b_gate_up, b_down, w_gate_up, w_down]
    aliases = {}
    if ys_prev is not None:
        in_specs.append(pl.BlockSpec(memory_space=pl.ANY))
        aliases = {len(operands): 0}
        operands.append(ys_prev)
    grid_spec = pltpu.PrefetchScalarGridSpec(
        num_scalar_prefetch=4,
        grid=(steps,),
        in_specs=in_specs,
        out_specs=pl.BlockSpec((rows, LANES), lambda i, be, nx, nb, nu: (first // MOE_SUB + i, 0)),
        scratch_shapes=[
            pltpu.VMEM((d, 2 * D_FF), F32), pltpu.VMEM((D_FF, d), F32),
            pltpu.VMEM((d, 2 * D_FF), BF16), pltpu.VMEM((D_FF, d), BF16),
            pltpu.SemaphoreType.DMA((2,)),
        ],
    )
    return pl.pallas_call(
        functools.partial(_experts_body, first),
        grid_spec=grid_spec,
        out_shape=jax.ShapeDtypeStruct((MOE_ROWS * PACK_SUB, LANES), I32),
        input_output_aliases=aliases,
        compiler_params=_params(("arbitrary",), 48),
        name="experts",
    )(*operands)


def _combine_body(gates_ref, x1_ref, mod_ref, gpost_ref, y0_ref, y1_ref, y2_ref, y3_ref, *rest):
    o_ref = rest[-1]
    tm = TM_COMB
    gates = gates_ref[...]
    y_hi = jnp.zeros((tm, PACK_COLS), F32)
    y_lo = jnp.zeros((tm, PACK_COLS), F32)
    for k, yk_ref in enumerate((y0_ref, y1_ref, y2_ref, y3_ref)):
        hi, lo = _unpack_rows(yk_ref, tm)
        y_hi = y_hi + hi * gates[:, k:k + 1]
        y_lo = y_lo + lo * gates[:, k:k + 1]
    y = jnp.concatenate([y_hi, y_lo], axis=1)
    gate2 = mod_ref[5:6, :]
    o_ref[...] = x1_ref[...] + gate2 * _rms(y, gpost_ref[...])


def _combine(part, gates_t, x1, mod, g_post, y4_part, out_prev):
    t, d = x1.shape
    tm = TM_COMB
    tiles = t // COMB_PARTS // tm
    tile0 = part * tiles
    tiles_per_seq = SEQ // tm

    def slab(k):
        return pl.BlockSpec((tm * PACK_SUB, LANES), lambda i: (k * tiles + i, 0))

    in_specs = [
        pl.BlockSpec((tm, TOP_K), lambda i: (tile0 + i, 0)),
        pl.BlockSpec((tm, d), lambda i: (tile0 + i, 0)),
        pl.BlockSpec((None, 6, d), lambda i: ((tile0 + i) // tiles_per_seq, 0, 0)),
        pl.BlockSpec(g_post.shape, lambda i: (0, 0)),
        slab(0), slab(1), slab(2), slab(3),
    ]
    operands = [gates_t, x1, mod, g_post, y4_part, y4_part, y4_part, y4_part]
    aliases = {}
    if out_prev is not None:
        in_specs.append(pl.BlockSpec(memory_space=pl.ANY))
        aliases = {len(operands): 0}
        operands.append(out_prev)
    return pl.pallas_call(
        _combine_body,
        grid=(tiles,),
        in_specs=in_specs,
        out_specs=pl.BlockSpec((tm, d), lambda i: (tile0 + i, 0)),
        out_shape=jax.ShapeDtypeStruct((t, d), F32),
        input_output_aliases=aliases,
        compiler_params=_params(("arbitrary",), 48),
        name="combine",
    )(*operands)


def _rotary_tables():
    half = ROT_DIM // 2
    inv_freq = ROPE_THETA ** (-2.0 * np.arange(half, dtype=np.float32) / ROT_DIM)
    ang = np.arange(SEQ, dtype=np.float32)[:, None] * inv_freq[None, :].astype(np.float32)
    cos, sin = np.cos(ang), np.sin(ang)
    ones = np.ones((SEQ, ATT_HEAD_DIM - ROT_DIM), np.float32)
    zeros = np.zeros((SEQ, ATT_HEAD_DIM - ROT_DIM), np.float32)
    zh = np.zeros((SEQ, half), np.float32)
    reps = LANES // ATT_HEAD_DIM
    rc = np.tile(np.concatenate([cos, cos, ones], axis=1), (1, reps))
    rm = np.tile(np.concatenate([-sin, zh, zeros], axis=1), (1, reps))
    rp = np.tile(np.concatenate([zh, sin, zeros], axis=1), (1, reps))
    return tuple(jnp.asarray(t, F32) for t in (rc, rm, rp))


def _mixer_inputs(w_in, w_gk_fwd, b_gk_fwd, w_gk_bwd, b_gk_bwd):
    hk = GLA_HEADS * GLA_DK
    hv = GLA_HEADS * GLA_DV
    w = w_in[0]
    o_lr = 2 * hk + 2 * hv
    o_aq = o_lr + 2 * GLA_RANK
    o_ak = o_aq + ATT_Q_HEADS * ATT_HEAD_DIM
    o_av = o_ak + ATT_KV_HEADS * ATT_HEAD_DIM
    hd = ATT_HEAD_DIM
    wa = w[:, :o_lr].astype(BF16)
    wlr = w[:, o_lr:o_aq].astype(BF16)
    dup = lambda m: jnp.concatenate([m[:, g * hd:(g + 1) * hd] for g in range(ATT_KV_HEADS) for _ in range(2)], axis=1)
    wb = jnp.concatenate([w[:, o_aq:o_ak], dup(w[:, o_ak:o_av])], axis=1).astype(BF16)
    wvt = dup(w[:, o_av:o_av + ATT_KV_HEADS * hd]).T.astype(BF16)
    zr = jnp.zeros((GLA_RANK, hk), F32)
    wgk = jnp.concatenate([jnp.concatenate([w_gk_fwd[0], zr], axis=1),
                           jnp.concatenate([zr, w_gk_bwd[0]], axis=1)], axis=0).astype(BF16)
    bgk = jnp.concatenate([b_gk_fwd[0], b_gk_bwd[0]])[None, :]
    return (wa, wlr, wgk, bgk, wb, wvt) + _rotary_tables()


def kernel(x, c, w_ada, b_ada, g_pre_mix, g_post_mix, w_in, w_gk_fwd, b_gk_fwd, w_gk_bwd, b_gk_bwd, g_gla_out,
           attn_sink, w_out, g_pre_ffn, g_post_ffn, w_router, b_router, w_gate_up, b_gate_up, w_down, b_down):
    assert x.shape == (BATCH, SEQ, D_MODEL) and w_ada.shape[0] == 1
    d = D_MODEL
    x2 = x.reshape(TOKENS, d)

    c_pad = jnp.pad(c, ((0, SUBLANES - BATCH), (0, 0)))
    mod = _ada(c_pad, w_ada[0], b_ada)[:BATCH].reshape(BATCH, 6, d)

    mixer_in = _mixer_inputs(w_in, w_gk_fwd, b_gk_fwd, w_gk_bwd, b_gk_bwd)
    q, k, v, gg, laf, lab, aq, ak2, avt = _inproj(x2, mod, g_pre_mix, *mixer_in)
    o_f, o_b = _gla(q, k, v, laf, lab)
    o_att = _attn(attn_sink[0], aq, ak2, avt)

    wr_t = w_router[0].T
    wrh = wr_t.astype(BF16)
    wrl = (wr_t - wrh.astype(F32)).astype(BF16)
    x1, h2_tiles, top_i, gates, rank, counts = _post(
        o_f, o_b, gg, o_att, x2, mod, g_gla_out, g_post_mix, g_pre_ffn, w_out[0].astype(BF16), wrh, wrl,
        b_router[0][:, None])

    pos, blocks = _route(top_i, rank, counts)

    src = _sc_source_rows(pos.reshape(TOP_K * TOKENS), MOE_ROWS)
    h2_rows = h2_tiles.reshape(TOKENS, PACK_SUB, LANES)
    part_rows = MOE_ROWS // MOE_PARTS
    ys = None
    for p in range(MOE_PARTS):
        xs_p = _sc_gather_rows(h2_rows, src[p * part_rows:(p + 1) * part_rows])
        ys = _experts(p, blocks, xs_p.reshape(part_rows * PACK_SUB, LANES),
                      w_gate_up[0], b_gate_up[0], w_down[0], b_down[0], ys)

    ys_rows = ys.reshape(MOE_ROWS, PACK_SUB, LANES)
    gates_t = gates.T
    part_tokens = TOKENS // COMB_PARTS
    out = None
    for p in range(COMB_PARTS):
        idx = pos[:, p * part_tokens:(p + 1) * part_tokens].reshape(TOP_K * part_tokens)
        y4_p = _sc_gather_rows(ys_rows, idx).reshape(TOP_K * part_tokens * PACK_SUB, LANES)
        out = _combine(p, gates_t, x1, mod, g_post_ffn, y4_p, out)
    return out.reshape(BATCH, SEQ, d)
```

```python
import functools

import jax
import jax.numpy as jnp
import numpy as np
from jax import lax
from jax.experimental import pallas as pl
from jax.experimental.pallas import tpu as pltpu
from jax.experimental.pallas import tpu_sc as plsc

F32 = jnp.float32
BF16 = jnp.bfloat16
I32 = jnp.int32

D_MODEL = 1024
BATCH = 2
SEQ = 8192
TOKENS = BATCH * SEQ
GLA_HEADS = 4
GLA_DV = 128
GLA_DK = 64
GLA_RANK = 16
GLA_GATE_NORMALIZER = 16.0
GLA_CHUNK = 64
ATT_Q_HEADS = 8
ATT_KV_HEADS = 2
ATT_HEAD_DIM = 64
ATT_WINDOW = 128
ATT_BLOCK = 128
ROT_DIM = 16
ROPE_THETA = 500000.0
N_EXPERTS = 32
TOP_K = 4
D_FF = 1024
SWIGLU_LIMIT = 7.0
SWIGLU_ALPHA = 1.702
NORM_EPS = 1e-6
NEG_INF = -1e30

LANES = 128
SUBLANES = 8
PACK_COLS = D_MODEL // 2
PACK_SUB = PACK_COLS // LANES

TM_IN = 512
GLA_GROUP = 8
ATT_GROUP = 8
TM_POST = 512
MOE_BM = 256
MOE_ROWS = TOKENS * TOP_K + N_EXPERTS * MOE_BM
MOE_NB = MOE_ROWS // MOE_BM
MOE_SUB = 2
MOE_PARTS = 3
COMB_PARTS = 1
MOE_NB_PAD = ((MOE_NB + LANES - 1) // LANES) * LANES
SC_SCAN_CHUNK = 4096
SC_SCAN_UNROLL = 8
TM_COMB = 256
SC_GATHER_WINDOW = 64

NT_DIMS = (((1,), (1,)), ((), ()))
TN_DIMS = (((0,), (0,)), ((), ()))


def _params(semantics, vmem_mib):
    return pltpu.CompilerParams(dimension_semantics=semantics, vmem_limit_bytes=vmem_mib * 1024 * 1024)


def _rms(x, g):
    return x * lax.rsqrt(jnp.mean(x * x, axis=-1, keepdims=True) + NORM_EPS) * g


def _silu(x):
    return x * jax.nn.sigmoid(x)


def _pack_rows(ref, v):
    m = v.shape[0]
    bits = lax.bitcast_convert_type(v, jnp.uint32)
    word = lax.bitcast_convert_type(bits[:, :PACK_COLS] | (bits[:, PACK_COLS:] >> 16), I32)
    for s in range(PACK_SUB):
        ref[pl.ds(s, m, stride=PACK_SUB), :] = word[:, s * LANES:(s + 1) * LANES]


def _unpack_rows(ref, m):
    word = jnp.concatenate([ref[pl.ds(s, m, stride=PACK_SUB), :] for s in range(PACK_SUB)], axis=1)
    bits = lax.bitcast_convert_type(word, jnp.uint32)
    hi = lax.bitcast_convert_type(bits & jnp.uint32(0xFFFF0000), F32)
    lo = lax.bitcast_convert_type(bits << 16, F32)
    return hi, lo


def _ada_body(c_ref, w_ref, b_ref, o_ref):
    ca = _silu(c_ref[...]).astype(BF16)
    o_ref[...] = jnp.dot(ca, w_ref[...].astype(BF16), preferred_element_type=F32) + b_ref[...]


def _ada(c_pad, w_ada, b_ada):
    d = D_MODEL
    return pl.pallas_call(
        _ada_body,
        grid=(6,),
        in_specs=[
            pl.BlockSpec((SUBLANES, d), lambda j: (0, 0)),
            pl.BlockSpec((d, d), lambda j: (0, j)),
            pl.BlockSpec((1, d), lambda j: (0, j)),
        ],
        out_specs=pl.BlockSpec((SUBLANES, d), lambda j: (0, j)),
        out_shape=jax.ShapeDtypeStruct((SUBLANES, 6 * d), F32),
        compiler_params=_params(("arbitrary",), 32),
        name="ada",
    )(c_pad, w_ada, b_ada)


def _rotary(x, cos_t, msin_t, psin_t):
    width = x.shape[1]
    reps = width // LANES
    c = jnp.concatenate([cos_t] * reps, axis=1)
    m = jnp.concatenate([msin_t] * reps, axis=1)
    p = jnp.concatenate([psin_t] * reps, axis=1)
    half = ROT_DIM // 2
    return x * c + pltpu.roll(x, width - half, 1) * m + pltpu.roll(x, half, 1) * p


def _inproj_body(x_ref, mod_ref, g_ref, wa_ref, wlr_ref, wgk_ref, bgk_ref, wb_ref, wvt_ref, rc_ref, rm_ref, rp_ref,
                 q_ref, k_ref, v_ref, gg_ref, laf_ref, lab_ref, aq_ref, ak_ref, avt_ref):
    shift = mod_ref[0:1, :]
    scale = mod_ref[1:2, :]
    h = (_rms(x_ref[...], g_ref[...]) * (1.0 + scale) + shift).astype(BF16)

    hk = GLA_HEADS * GLA_DK
    hv = GLA_HEADS * GLA_DV
    pa = jnp.dot(h, wa_ref[...], preferred_element_type=F32)
    q_ref[...] = pa[:, 0:hk] * (GLA_DK ** -0.5)
    k_ref[...] = pa[:, hk:2 * hk]
    v_ref[...] = pa[:, 2 * hk:2 * hk + hv].astype(BF16)
    gg_ref[...] = pa[:, 2 * hk + hv:2 * hk + 2 * hv]

    plr = jnp.dot(h, wlr_ref[...], preferred_element_type=F32)
    gk = jnp.dot(plr.astype(BF16), wgk_ref[...], preferred_element_type=F32) + bgk_ref[...]
    la = (jnp.minimum(gk, 0.0) - jnp.log1p(jnp.exp(-jnp.abs(gk)))) * (1.0 / GLA_GATE_NORMALIZER)
    laf_ref[...] = la[:, 0:hk]
    lab_ref[...] = la[:, hk:2 * hk]

    pb = jnp.dot(h, wb_ref[...], preferred_element_type=F32)
    nq = ATT_Q_HEADS * ATT_HEAD_DIM
    nk = 2 * ATT_KV_HEADS * ATT_HEAD_DIM
    rc, rm, rp = rc_ref[...], rm_ref[...], rp_ref[...]
    aq_ref[...] = (_rotary(pb[:, 0:nq], rc, rm, rp) * (ATT_HEAD_DIM ** -0.5)).astype(BF16)
    ak_ref[...] = _rotary(pb[:, nq:nq + nk], rc, rm, rp).astype(BF16)
    avt_ref[...] = lax.dot_general(wvt_ref[...], h, NT_DIMS, preferred_element_type=F32).astype(BF16)


def _inproj(x2, mod, g_pre, wa, wlr, wgk, bgk, wb, wvt, rc, rm, rp):
    t, d = x2.shape
    tm = TM_IN
    tiles_per_seq = SEQ // tm
    hk = GLA_HEADS * GLA_DK
    hv = GLA_HEADS * GLA_DV
    nq = ATT_Q_HEADS * ATT_HEAD_DIM
    nk = 2 * ATT_KV_HEADS * ATT_HEAD_DIM

    def full(a):
        return pl.BlockSpec(a.shape, lambda i: (0,) * a.ndim)

    def rows(w):
        return pl.BlockSpec((tm, w), lambda i: (i, 0))

    def table():
        return pl.BlockSpec((tm, LANES), lambda i: (i % tiles_per_seq, 0))

    out_widths = [(hk, F32), (hk, F32), (hv, BF16), (hv, F32), (hk, F32), (hk, F32), (nq, BF16), (nk, BF16)]
    return pl.pallas_call(
        _inproj_body,
        grid=(t // tm,),
        in_specs=[
            rows(d),
            pl.BlockSpec((None, 6, d), lambda i: (i // tiles_per_seq, 0, 0)),
            full(g_pre), full(wa), full(wlr), full(wgk), full(bgk), full(wb), full(wvt),
            table(), table(), table(),
        ],
        out_specs=[rows(w) for w, _ in out_widths] + [pl.BlockSpec((nk, tm), lambda i: (0, i))],
        out_shape=[jax.ShapeDtypeStruct((t, w), dt) for w, dt in out_widths] + [jax.ShapeDtypeStruct((nk, t), BF16)],
        compiler_params=_params(("arbitrary",), 56),
        name="inproj",
    )(x2, mod, g_pre, wa, wlr, wgk, bgk, wb, wvt, rc, rm, rp)


def _gla_body(qf_ref, kf_ref, vf_ref, laf_ref, qb_ref, kb_ref, vb_ref, lab_ref, of_ref, ob_ref, sf_ref, sb_ref):
    @pl.when(pl.program_id(1) == 0)
    def _():
        sf_ref[...] = jnp.zeros_like(sf_ref)
        sb_ref[...] = jnp.zeros_like(sb_ref)

    c = GLA_CHUNK
    r_i = lax.broadcasted_iota(I32, (c, c), 0)
    c_i = lax.broadcasted_iota(I32, (c, c), 1)
    lower = c_i <= r_i
    upper = c_i >= r_i
    cum_f = jnp.where(lower, 1.0, 0.0).astype(BF16)
    cum_b = jnp.where(upper, 1.0, 0.0).astype(BF16)
    lane = lax.broadcasted_iota(I32, (1, LANES), 1)
    head_masks = (lane < GLA_DK, lane >= GLA_DK)

    fwd = [(qf_ref, kf_ref, laf_ref, vf_ref, of_ref, slice(g * c, (g + 1) * c), cum_f, lower, c - 1, c // 2 - 1)
           for g in range(GLA_GROUP)]
    bwd = [(qb_ref, kb_ref, lab_ref, vb_ref, ob_ref, slice(g * c, (g + 1) * c), cum_b, upper, 0, c // 2)
           for g in reversed(range(GLA_GROUP))]
    heads = range(GLA_HEADS)
    pair = [slice((h // 2) * LANES, (h // 2 + 1) * LANES) for h in heads]
    vcols = [slice(h * GLA_DV, (h + 1) * GLA_DV) for h in heads]

    def stage1(item):
        q_ref, k_ref, la_ref, v_ref, o_ref, rows, cum, tri, i_last, i_mid = item
        la = la_ref[rows, :]
        hi = la.astype(BF16)
        lo = (la - hi.astype(F32)).astype(BF16)
        b = jnp.dot(cum, hi, preferred_element_type=F32) + jnp.dot(cum, lo, preferred_element_type=F32)
        b_last = b[i_last:i_last + 1, :]
        b_mid = b[i_mid:i_mid + 1, :]
        q, k = q_ref[rows, :], k_ref[rows, :]
        return (q * jnp.exp(b - b_mid), (k * jnp.exp(b_mid - b)).astype(BF16), q * jnp.exp(b),
                (k * jnp.exp(b_last - b)).astype(BF16), jnp.exp(b_last))

    def stage2(item, pre):
        v_ref, rows, tri = item[3], item[5], item[7]
        qs, ks, qi, kst, decay = pre
        out = []
        for h in heads:
            mask = head_masks[h % 2]
            qs_h = jnp.where(mask, qs[:, pair[h]], 0.0).astype(BF16)
            sc = lax.dot_general(qs_h, ks[:, pair[h]], NT_DIMS, preferred_element_type=F32)
            v_h = v_ref[rows, vcols[h]]
            kv = lax.dot_general(v_h, kst[:, pair[h]], TN_DIMS, preferred_element_type=F32)
            out.append((jnp.where(tri, sc, 0.0).astype(BF16), kv,
                        jnp.where(mask, qi[:, pair[h]], 0.0).astype(BF16), v_h))
        return out

    def run(items, s_ref):
        pre = [stage1(it) for it in items]
        mid = [stage2(it, p) for it, p in zip(items, pre)]
        states = [s_ref[h] for h in heads]
        for it, p, m in zip(items, pre, mid):
            o_ref, rows, decay = it[4], it[5], p[4]
            for h in heads:
                sc, kv, qi_h, v_h = m[h]
                o = jnp.dot(sc, v_h, preferred_element_type=F32)
                o = o + lax.dot_general(qi_h, states[h].astype(BF16), NT_DIMS, preferred_element_type=F32)
                o_ref[rows, vcols[h]] = o
                states[h] = states[h] * decay[:, pair[h]] + kv
        for h in heads:
            s_ref[h] = states[h]

    run(fwd, sf_ref)
    run(bwd, sb_ref)


def _gla(q, k, v, laf, lab):
    t = q.shape[0]
    rows = GLA_GROUP * GLA_CHUNK
    ng = SEQ // rows
    hk = GLA_HEADS * GLA_DK
    hv = GLA_HEADS * GLA_DV

    def fwd(w):
        return pl.BlockSpec((rows, w), lambda b, n: (b * ng + n, 0))

    def bwd(w):
        return pl.BlockSpec((rows, w), lambda b, n: (b * ng + ng - 1 - n, 0))

    return pl.pallas_call(
        _gla_body,
        grid=(BATCH, ng),
        in_specs=[fwd(hk), fwd(hk), fwd(hv), fwd(hk), bwd(hk), bwd(hk), bwd(hv), bwd(hk)],
        out_specs=[fwd(hv), bwd(hv)],
        out_shape=[jax.ShapeDtypeStruct((t, hv), F32)] * 2,
        scratch_shapes=[pltpu.VMEM((GLA_HEADS, GLA_DV, 2 * GLA_DK), F32)] * 2,
        compiler_params=_params(("arbitrary", "arbitrary"), 32),
        name="gla",
    )(q, k, v, laf, q, k, v, lab)


def _attn_body(sink_ref, q_ref, kp_ref, kc_ref, kn_ref, vp_ref, vc_ref, vn_ref, o_ref):
    step = pl.program_id(1)
    last = pl.num_programs(1) - 1
    qb = ATT_BLOCK
    hd = ATT_HEAD_DIM
    k_all = jnp.concatenate([kp_ref[...], kc_ref[...], kn_ref[...]], axis=0)
    vt_all = jnp.concatenate([vp_ref[...], vc_ref[...], vn_ref[...]], axis=1)
    lane = lax.broadcasted_iota(I32, (1, LANES), 1)
    lo = lane < hd
    j_k = lax.broadcasted_iota(I32, (3 * qb, qb), 0)
    i_q = lax.broadcasted_iota(I32, (3 * qb, qb), 1)
    band = jnp.abs(j_k - qb - i_q) <= ATT_WINDOW
    sinks = [jnp.concatenate([jnp.full((1, qb), sink_ref[4 * g + r], F32) for r in range(4)], axis=1)
             for g in range(ATT_KV_HEADS)]
    work = [(j, g) for j in range(ATT_GROUP) for g in range(ATT_KV_HEADS)]

    def scores(j, g):
        valid = band
        if j == 0:
            valid = valid & ((j_k >= qb) | (step > 0))
        if j == ATT_GROUP - 1:
            valid = valid & ((j_k < 2 * qb) | (step < last))
        valid4 = jnp.concatenate([valid] * 4, axis=1)
        rows = slice(j * qb, (j + 1) * qb)
        kg = k_all[j * qb:(j + 3) * qb, g * LANES:(g + 1) * LANES]
        qa = q_ref[rows, (2 * g) * LANES:(2 * g + 1) * LANES]
        qc = q_ref[rows, (2 * g + 1) * LANES:(2 * g + 2) * LANES]
        zero = jnp.zeros_like(qa)
        lhs = jnp.concatenate([jnp.where(lo, qa, zero), jnp.where(lo, zero, qa),
                               jnp.where(lo, qc, zero), jnp.where(lo, zero, qc)], axis=0)
        st = lax.dot_general(kg, lhs, NT_DIMS, preferred_element_type=F32)
        return jnp.where(valid4, st, NEG_INF)

    def softmax(st, g):
        sink = sinks[g]
        m = jnp.maximum(jnp.max(st, axis=0, keepdims=True), sink)
        p = jnp.exp(st - m)
        return p.astype(BF16), 1.0 / (jnp.sum(p, axis=0, keepdims=True) + jnp.exp(sink - m))

    def output(j, g, p, inv):
        rows = slice(j * qb, (j + 1) * qb)
        vgt = vt_all[g * LANES:(g + 1) * LANES, j * qb:(j + 3) * qb]
        ot = jnp.dot(vgt, p, preferred_element_type=F32) * inv
        pair_a = jnp.concatenate([ot[0:hd, 0:qb], ot[hd:2 * hd, qb:2 * qb]], axis=0)
        pair_c = jnp.concatenate([ot[0:hd, 2 * qb:3 * qb], ot[hd:2 * hd, 3 * qb:4 * qb]], axis=0)
        o_ref[rows, (2 * g) * LANES:(2 * g + 1) * LANES] = pair_a.T.astype(o_ref.dtype)
        o_ref[rows, (2 * g + 1) * LANES:(2 * g + 2) * LANES] = pair_c.T.astype(o_ref.dtype)

    s_all = [scores(j, g) for j, g in work]
    p_all = [softmax(st, g) for st, (j, g) in zip(s_all, work)]
    for (j, g), (p, inv) in zip(work, p_all):
        output(j, g, p, inv)


def _attn(sink, aq, ak2, avt):
    t = aq.shape[0]
    qb = ATT_BLOCK
    nb = SEQ // qb
    steps = nb // ATT_GROUP
    nq = ATT_Q_HEADS * ATT_HEAD_DIM
    nk = 2 * ATT_KV_HEADS * ATT_HEAD_DIM

    def edge_block(b, n, shift):
        return b * nb + jnp.clip(n * ATT_GROUP + shift, 0, nb - 1)

    def k_edge(shift):
        return pl.BlockSpec((qb, nk), lambda b, n: (edge_block(b, n, shift), 0))

    def v_edge(shift):
        return pl.BlockSpec((nk, qb), lambda b, n: (0, edge_block(b, n, shift)))

    def group(w):
        return pl.BlockSpec((ATT_GROUP * qb, w), lambda b, n: (b * steps + n, 0))

    v_group = pl.BlockSpec((nk, ATT_GROUP * qb), lambda b, n: (0, b * steps + n))
    return pl.pallas_call(
        _attn_body,
        grid=(BATCH, steps),
        in_specs=[
            pl.BlockSpec(memory_space=pltpu.SMEM),
            group(nq),
            k_edge(-1), group(nk), k_edge(ATT_GROUP), v_edge(-1), v_group, v_edge(ATT_GROUP),
        ],
        out_specs=group(nq),
        out_shape=jax.ShapeDtypeStruct((t, nq), BF16),
        compiler_params=_params(("arbitrary", "arbitrary"), 48),
        name="attn",
    )(sink, aq, ak2, ak2, ak2, avt, avt, avt)


def _post_body(of_ref, ob_ref, gg_ref, oa_ref, x_ref, mod_ref, ggla_ref, gpm_ref, gpf_ref, wout_ref,
               wrh_ref, wrl_ref, br_ref,
               x1_ref, h2_ref, ti_ref, gt_ref, rk_ref, cnt_ref, base_ref):
    tm = TM_POST

    @pl.when(pl.program_id(0) == 0)
    def _():
        base_ref[...] = jnp.zeros_like(base_ref)

    og = of_ref[...] + ob_ref[...]
    gg = gg_ref[...]
    parts = []
    for h in range(GLA_HEADS):
        cols = slice(h * GLA_DV, (h + 1) * GLA_DV)
        parts.append((_rms(og[:, cols], ggla_ref[...]) * _silu(gg[:, cols])).astype(BF16))
    o = jnp.concatenate(parts + [oa_ref[...]], axis=1)
    y = jnp.dot(o, wout_ref[...], preferred_element_type=F32)

    gate1 = mod_ref[2:3, :]
    shift2 = mod_ref[3:4, :]
    scale2 = mod_ref[4:5, :]
    x1 = x_ref[...] + gate1 * _rms(y, gpm_ref[...])
    x1_ref[...] = x1
    h2 = _rms(x1, gpf_ref[...]) * (1.0 + scale2) + shift2
    h2_hi = h2.astype(BF16)
    h2_hi32 = h2_hi.astype(F32)
    h2_lo = (h2 - h2_hi32).astype(BF16)
    _pack_rows(h2_ref, h2_hi32)

    wrh = wrh_ref[...]
    logits = (lax.dot_general(wrh, h2_hi, NT_DIMS, preferred_element_type=F32)
              + lax.dot_general(wrh, h2_lo, NT_DIMS, preferred_element_type=F32)
              + lax.dot_general(wrl_ref[...], h2_hi, NT_DIMS, preferred_element_type=F32)
              + br_ref[...])
    e_iota = lax.broadcasted_iota(I32, (N_EXPERTS, tm), 0)
    idxs, vals = [], []
    work = logits
    for _ in range(TOP_K):
        m = jnp.max(work, axis=0, keepdims=True)
        idx = jnp.min(jnp.where(work == m, e_iota, N_EXPERTS), axis=0, keepdims=True)
        idxs.append(idx)
        vals.append(m)
        work = jnp.where(e_iota == idx, -jnp.inf, work)
    exps = [jnp.exp(v - vals[0]) for v in vals]
    inv = 1.0 / (exps[0] + exps[1] + exps[2] + exps[3])
    gt_ref[...] = jnp.concatenate([e * inv for e in exps], axis=0)
    ti_ref[...] = jnp.concatenate(idxs, axis=0)

    onehots = [e_iota == idx for idx in idxs]
    member = jnp.where(onehots[0] | onehots[1] | onehots[2] | onehots[3], 1.0, 0.0)
    t_row = lax.broadcasted_iota(I32, (tm, tm), 0)
    t_col = lax.broadcasted_iota(I32, (tm, tm), 1)
    strict = jnp.where(t_row < t_col, 1.0, 0.0).astype(BF16)
    before = base_ref[...] + jnp.dot(member.astype(BF16), strict, preferred_element_type=F32)
    rk_ref[...] = jnp.concatenate(
        [jnp.sum(jnp.where(oh, before, 0.0), axis=0, keepdims=True) for oh in onehots], axis=0).astype(I32)
    new_base = base_ref[...] + jnp.sum(member, axis=1, keepdims=True)
    base_ref[...] = new_base
    cnt_ref[...] = jnp.broadcast_to(new_base, cnt_ref.shape)


def _post(o_f, o_b, gg, o_att, x2, mod, g_gla, g_pm, g_pf, wout, wrh, wrl, br):
    t, d = x2.shape
    tm = TM_POST
    tiles_per_seq = SEQ // tm
    hv = GLA_HEADS * GLA_DV

    def full(a):
        return pl.BlockSpec(a.shape, lambda i: (0,) * a.ndim)

    def rows(w):
        return pl.BlockSpec((tm, w), lambda i: (i, 0))

    def lanes():
        return pl.BlockSpec((TOP_K, tm), lambda i: (0, i))

    return pl.pallas_call(
        _post_body,
        grid=(t // tm,),
        in_specs=[
            rows(hv), rows(hv), rows(hv), rows(hv), rows(d),
            pl.BlockSpec((None, 6, d), lambda i: (i // tiles_per_seq, 0, 0)),
            full(g_gla), full(g_pm), full(g_pf), full(wout), full(wrh), full(wrl), full(br),
        ],
        out_specs=[
            rows(d),
            pl.BlockSpec((tm * PACK_SUB, LANES), lambda i: (i, 0)),
            lanes(), lanes(), lanes(),
            pl.BlockSpec((N_EXPERTS, LANES), lambda i: (0, 0)),
        ],
        out_shape=[
            jax.ShapeDtypeStruct((t, d), F32),
            jax.ShapeDtypeStruct((t * PACK_SUB, LANES), I32),
            jax.ShapeDtypeStruct((TOP_K, t), I32),
            jax.ShapeDtypeStruct((TOP_K, t), F32),
            jax.ShapeDtypeStruct((TOP_K, t), I32),
            jax.ShapeDtypeStruct((N_EXPERTS, LANES), F32),
        ],
        scratch_shapes=[pltpu.VMEM((N_EXPERTS, 1), F32)],
        compiler_params=_params(("arbitrary",), 48),
        name="post",
    )(o_f, o_b, gg, o_att, x2, mod, g_gla, g_pm, g_pf, wout, wrh, wrl, br)


def _route_body(ti_ref, rk_ref, cnt_ref, pos_ref, blk_ref):
    cnt = cnt_ref[...]
    padded = jnp.floor((cnt + (MOE_BM - 1)) * (1.0 / MOE_BM)) * MOE_BM
    starts, ends = [], []
    acc = jnp.zeros((1, LANES), F32)
    for e in range(N_EXPERTS):
        starts.append(acc)
        acc = acc + padded[e:e + 1, :]
        ends.append(acc)
    ti = ti_ref[...]
    off = jnp.zeros(ti.shape, F32)
    for e in range(N_EXPERTS):
        off = jnp.where(ti == e, starts[e][:, 0:1], off)
    pos_ref[...] = rk_ref[...] + off.astype(I32)

    def owner_of(row):
        n_le = jnp.zeros(row.shape, I32)
        for e in range(N_EXPERTS):
            n_le = n_le + jnp.where(ends[e][:, 0:1] <= row, 1, 0)
        return jnp.minimum(n_le, N_EXPERTS - 1)

    block_start = lax.broadcasted_iota(I32, (1, MOE_NB_PAD), 1).astype(F32) * MOE_BM
    owner = owner_of(block_start)
    nxt = jnp.zeros((1, MOE_NB_PAD), I32)
    nxt_blk = jnp.zeros((1, MOE_NB_PAD), I32)
    for e in range(N_EXPERTS):
        end_e = ends[e][:, 0:1]
        nxt = jnp.where(owner == e, jnp.where(end_e < acc[:, 0:1], owner_of(end_e), -1), nxt)
        nxt_blk = jnp.where(owner == e, (end_e * (1.0 / MOE_BM)).astype(I32), nxt_blk)
    used = jnp.broadcast_to((acc[:, 0:1] * (1.0 / MOE_BM)).astype(I32), (1, MOE_NB_PAD))
    blk_ref[...] = jnp.concatenate([owner, nxt, used, nxt_blk, jnp.zeros((SUBLANES - 4, MOE_NB_PAD), I32)], axis=0)


def _route(top_i, rank, counts):
    return pl.pallas_call(
        _route_body,
        out_shape=[
            jax.ShapeDtypeStruct(top_i.shape, I32),
            jax.ShapeDtypeStruct((SUBLANES, MOE_NB_PAD), I32),
        ],
        compiler_params=pltpu.CompilerParams(vmem_limit_bytes=32 * 1024 * 1024),
        name="route",
    )(top_i, rank, counts)


def _sc_workers():
    info = plsc.get_sparse_core_info()
    return info.num_cores, info.num_subcores, info.num_lanes


def _sc_gather_loop(table_hbm, out_hbm, idx_v, base, chunks, buf0, buf1, sem0, sem1):
    window = SC_GATHER_WINDOW

    def fetch(c, buf, sem):
        return pltpu.make_async_copy(table_hbm.at[idx_v.at[pl.ds(c * window, window)]], buf, sem)

    def flush(c, buf):
        pltpu.sync_copy(buf, out_hbm.at[pl.ds(base + c * window, window)])

    fetch(0, buf0, sem0).start()

    @pl.loop(0, chunks, step=2)
    def _(c):
        fetch(c + 1, buf1, sem1).start()
        fetch(c, buf0, sem0).wait()
        flush(c, buf0)

        @pl.when(c + 2 < chunks)
        def _():
            fetch(c + 2, buf0, sem0).start()

        fetch(c + 1, buf1, sem1).wait()
        flush(c + 1, buf1)


def _sc_source_rows(pos_flat, n_rows):
    cores, subcores, lanes = _sc_workers()
    workers = cores * subcores
    per_worker = n_rows // workers
    n_assign = pos_flat.shape[0]
    scan = SC_SCAN_CHUNK
    assert per_worker * workers == n_rows and per_worker % lanes == 0
    assert n_assign % scan == 0 and scan % lanes == 0
    mesh = plsc.VectorSubcoreMesh(core_axis_name="core", subcore_axis_name="subcore")

    @functools.partial(
        pl.kernel,
        out_type=jax.ShapeDtypeStruct((n_rows,), I32),
        mesh=mesh,
        scratch_types=[pltpu.VMEM((per_worker,), I32), pltpu.VMEM((scan,), I32)],
        compiler_params=pltpu.CompilerParams(needs_layout_passes=False),
        name="sc_source_rows",
    )
    def invert(pos_hbm, out_hbm, src_v, pos_v):
        wid = lax.axis_index("subcore") * cores + lax.axis_index("core")
        base = wid * per_worker
        lane = lax.iota(I32, lanes)

        @pl.loop(0, per_worker, step=lanes)
        def _(j):
            src_v[pl.ds(j, lanes)] = (base + j + lane) & (TOKENS - 1)

        @pl.loop(0, n_assign, step=scan)
        def _(a0):
            pltpu.sync_copy(pos_hbm.at[pl.ds(a0, scan)], pos_v)

            @plsc.parallel_loop(0, scan, step=lanes, unroll=SC_SCAN_UNROLL)
            def _(j):
                rel = pos_v[pl.ds(j, lanes)] - base
                mine = (rel >= 0) & (rel < per_worker)
                tok = (a0 + j + lane) & (TOKENS - 1)
                plsc.store_scatter(src_v, [jnp.where(mine, rel, 0)], tok, mask=mine)

        pltpu.sync_copy(src_v, out_hbm.at[pl.ds(base, per_worker)])

    return invert(pos_flat)


def _sc_gather_rows(table, idx):
    cores, subcores, _ = _sc_workers()
    workers = cores * subcores
    n = idx.shape[0]
    window = SC_GATHER_WINDOW
    per_worker = n // workers
    chunks = per_worker // window
    assert per_worker * workers == n and chunks * window == per_worker and chunks % 2 == 0
    row_shape = table.shape[1:]
    mesh = plsc.VectorSubcoreMesh(core_axis_name="core", subcore_axis_name="subcore")

    @functools.partial(
        pl.kernel,
        out_type=jax.ShapeDtypeStruct((n,) + row_shape, table.dtype),
        mesh=mesh,
        scratch_types=[
            pltpu.VMEM((per_worker,), I32),
            pltpu.VMEM((window,) + row_shape, table.dtype),
            pltpu.VMEM((window,) + row_shape, table.dtype),
            pltpu.SemaphoreType.DMA,
            pltpu.SemaphoreType.DMA,
        ],
        name="sc_gather_rows",
    )
    def gather(table_hbm, idx_hbm, out_hbm, idx_v, buf0, buf1, sem0, sem1):
        wid = lax.axis_index("subcore") * cores + lax.axis_index("core")
        base = wid * per_worker
        pltpu.sync_copy(idx_hbm.at[pl.ds(base, per_worker)], idx_v)
        _sc_gather_loop(table_hbm, out_hbm, idx_v, base, chunks, buf0, buf1, sem0, sem1)

    return gather(table, idx)


def _experts_body(first, be_ref, nxt_ref, nxtblk_ref, nu_ref, xs_ref, bgu_ref, bd_ref, wgu_hbm, wd_hbm, *rest):
    ys_ref, wgu_f32, wd_f32, wgu_bf, wd_bf, sems = rest[-6:]
    step = pl.program_id(0)
    end = first + pl.num_programs(0) * MOE_SUB
    bm = MOE_BM
    n_used = nu_ref[0]

    def fetch(e):
        return (pltpu.make_async_copy(wgu_hbm.at[e], wgu_f32, sems.at[0]),
                pltpu.make_async_copy(wd_hbm.at[e], wd_f32, sems.at[1]))

    @pl.when((step == 0) & (first < n_used))
    def _():
        for cp in fetch(be_ref[first]):
            cp.start()

    for sub in range(MOE_SUB):
        i = first + step * MOE_SUB + sub
        e = be_ref[i]
        used = i < n_used
        fresh = (i == first) | (e != be_ref[jnp.maximum(i - 1, 0)])
        rows = pl.ds(sub * bm * PACK_SUB, bm * PACK_SUB)
        xs_sub = xs_ref.at[rows, :]
        ys_sub = ys_ref.at[rows, :]

        @pl.when(used & fresh)
        def _():
            for cp in fetch(e):
                cp.wait()
            wgu_bf[...] = wgu_f32[...].astype(BF16)
            wd_bf[...] = wd_f32[...].astype(BF16)

            @pl.when((nxt_ref[i] >= 0) & (nxtblk_ref[i] < end))
            def _():
                for cp in fetch(nxt_ref[i]):
                    cp.start()

        @pl.when(used)
        def _():
            x = jnp.concatenate(_unpack_rows(xs_sub, bm), axis=1).astype(BF16)
            gu = jnp.dot(x, wgu_bf[...], preferred_element_type=F32) + bgu_ref[pl.ds(e, 1), :]
            gate = jnp.minimum(gu[:, 0:D_FF], SWIGLU_LIMIT)
            up = jnp.clip(gu[:, D_FF:2 * D_FF], -SWIGLU_LIMIT, SWIGLU_LIMIT)
            act = ((up + 1.0) * gate * jax.nn.sigmoid(SWIGLU_ALPHA * gate)).astype(BF16)
            y = jnp.dot(act, wd_bf[...], preferred_element_type=F32) + bd_ref[pl.ds(e, 1), :]
            _pack_rows(ys_sub, y.astype(BF16).astype(F32))

        @pl.when(jnp.logical_not(used))
        def _():
            ys_sub[...] = jnp.zeros((bm * PACK_SUB, LANES), I32)


def _experts(part, blocks, xs_part, w_gate_up, b_gate_up, w_down, b_down, ys_prev):
    rows = MOE_SUB * MOE_BM * PACK_SUB
    d = D_MODEL
    part_blocks = MOE_NB // MOE_PARTS
    steps = part_blocks // MOE_SUB
    first = part * part_blocks

    def x_block(i, be, nx, nb, nu):
        last = jnp.maximum((jnp.minimum(nu[0], first + part_blocks) - 1 - first) // MOE_SUB, 0)
        return jnp.minimum(i, last)

    in_specs = [
        pl.BlockSpec((rows, LANES), lambda i, be, nx, nb, nu: (x_block(i, be, nx, nb, nu), 0)),
        pl.BlockSpec(b_gate_up.shape, lambda i, be, nx, nb, nu: (0, 0)),
        pl.BlockSpec(b_down.shape, lambda i, be, nx, nb, nu: (0, 0)),
        pl.BlockSpec(memory_space=pl.ANY),
        pl.BlockSpec(memory_space=pl.ANY),
    ]
    operands = [blocks[0, :MOE_NB], blocks[1, :MOE_NB], blocks[3, :MOE_NB], blocks[2, :1],
                xs_part, b_gate_up, b_down, w_gate_up, w_down]
    aliases = {}
    if ys_prev is not None:
        in_specs.append(pl.BlockSpec(memory_space=pl.ANY))
        aliases = {len(operands): 0}
        operands.append(ys_prev)
    grid_spec = pltpu.PrefetchScalarGridSpec(
        num_scalar_prefetch=4,
        grid=(steps,),
        in_specs=in_specs,
        out_specs=pl.BlockSpec((rows, LANES), lambda i, be, nx, nb, nu: (first // MOE_SUB + i, 0)),
        scratch_shapes=[
            pltpu.VMEM((d, 2 * D_FF), F32), pltpu.VMEM((D_FF, d), F32),
            pltpu.VMEM((d, 2 * D_FF), BF16), pltpu.VMEM((D_FF, d), BF16),
            pltpu.SemaphoreType.DMA((2,)),
        ],
    )
    return pl.pallas_call(
        functools.partial(_experts_body, first),
        grid_spec=grid_spec,
        out_shape=jax.ShapeDtypeStruct((MOE_ROWS * PACK_SUB, LANES), I32),
        input_output_aliases=aliases,
        compiler_params=_params(("arbitrary",), 48),
        name="experts",
    )(*operands)


def _combine_body(gates_ref, x1_ref, mod_ref, gpost_ref, y0_ref, y1_ref, y2_ref, y3_ref, *rest):
    o_ref = rest[-1]
    tm = TM_COMB
    gates = gates_ref[...]
    y_hi = jnp.zeros((tm, PACK_COLS), F32)
    y_lo = jnp.zeros((tm, PACK_COLS), F32)
    for k, yk_ref in enumerate((y0_ref, y1_ref, y2_ref, y3_ref)):
        hi, lo = _unpack_rows(yk_ref, tm)
        y_hi = y_hi + hi * gates[:, k:k + 1]
        y_lo = y_lo + lo * gates[:, k:k + 1]
    y = jnp.concatenate([y_hi, y_lo], axis=1)
    gate2 = mod_ref[5:6, :]
    o_ref[...] = x1_ref[...] + gate2 * _rms(y, gpost_ref[...])


def _combine(part, gates_t, x1, mod, g_post, y4_part, out_prev):
    t, d = x1.shape
    tm = TM_COMB
    tiles = t // COMB_PARTS // tm
    tile0 = part * tiles
    tiles_per_seq = SEQ // tm

    def slab(k):
        return pl.BlockSpec((tm * PACK_SUB, LANES), lambda i: (k * tiles + i, 0))

    in_specs = [
        pl.BlockSpec((tm, TOP_K), lambda i: (tile0 + i, 0)),
        pl.BlockSpec((tm, d), lambda i: (tile0 + i, 0)),
        pl.BlockSpec((None, 6, d), lambda i: ((tile0 + i) // tiles_per_seq, 0, 0)),
        pl.BlockSpec(g_post.shape, lambda i: (0, 0)),
        slab(0), slab(1), slab(2), slab(3),
    ]
    operands = [gates_t, x1, mod, g_post, y4_part, y4_part, y4_part, y4_part]
    aliases = {}
    if out_prev is not None:
        in_specs.append(pl.BlockSpec(memory_space=pl.ANY))
        aliases = {len(operands): 0}
        operands.append(out_prev)
    return pl.pallas_call(
        _combine_body,
        grid=(tiles,),
        in_specs=in_specs,
        out_specs=pl.BlockSpec((tm, d), lambda i: (tile0 + i, 0)),
        out_shape=jax.ShapeDtypeStruct((t, d), F32),
        input_output_aliases=aliases,
        compiler_params=_params(("arbitrary",), 48),
        name="combine",
    )(*operands)


def _rotary_tables():
    half = ROT_DIM // 2
    inv_freq = ROPE_THETA ** (-2.0 * np.arange(half, dtype=np.float32) / ROT_DIM)
    ang = np.arange(SEQ, dtype=np.float32)[:, None] * inv_freq[None, :].astype(np.float32)
    cos, sin = np.cos(ang), np.sin(ang)
    ones = np.ones((SEQ, ATT_HEAD_DIM - ROT_DIM), np.float32)
    zeros = np.zeros((SEQ, ATT_HEAD_DIM - ROT_DIM), np.float32)
    zh = np.zeros((SEQ, half), np.float32)
    reps = LANES // ATT_HEAD_DIM
    rc = np.tile(np.concatenate([cos, cos, ones], axis=1), (1, reps))
    rm = np.tile(np.concatenate([-sin, zh, zeros], axis=1), (1, reps))
    rp = np.tile(np.concatenate([zh, sin, zeros], axis=1), (1, reps))
    return tuple(jnp.asarray(t, F32) for t in (rc, rm, rp))


def _mixer_inputs(w_in, w_gk_fwd, b_gk_fwd, w_gk_bwd, b_gk_bwd):
    hk = GLA_HEADS * GLA_DK
    hv = GLA_HEADS * GLA_DV
    w = w_in[0]
    o_lr = 2 * hk + 2 * hv
    o_aq = o_lr + 2 * GLA_RANK
    o_ak = o_aq + ATT_Q_HEADS * ATT_HEAD_DIM
    o_av = o_ak + ATT_KV_HEADS * ATT_HEAD_DIM
    hd = ATT_HEAD_DIM
    wa = w[:, :o_lr].astype(BF16)
    wlr = w[:, o_lr:o_aq].astype(BF16)
    dup = lambda m: jnp.concatenate([m[:, g * hd:(g + 1) * hd] for g in range(ATT_KV_HEADS) for _ in range(2)], axis=1)
    wb = jnp.concatenate([w[:, o_aq:o_ak], dup(w[:, o_ak:o_av])], axis=1).astype(BF16)
    wvt = dup(w[:, o_av:o_av + ATT_KV_HEADS * hd]).T.astype(BF16)
    zr = jnp.zeros((GLA_RANK, hk), F32)
    wgk = jnp.concatenate([jnp.concatenate([w_gk_fwd[0], zr], axis=1),
                           jnp.concatenate([zr, w_gk_bwd[0]], axis=1)], axis=0).astype(BF16)
    bgk = jnp.concatenate([b_gk_fwd[0], b_gk_bwd[0]])[None, :]
    return (wa, wlr, wgk, bgk, wb, wvt) + _rotary_tables()


def kernel(x, c, w_ada, b_ada, g_pre_mix, g_post_mix, w_in, w_gk_fwd, b_gk_fwd, w_gk_bwd, b_gk_bwd, g_gla_out,
           attn_sink, w_out, g_pre_ffn, g_post_ffn, w_router, b_router, w_gate_up, b_gate_up, w_down, b_down):
    assert x.shape == (BATCH, SEQ, D_MODEL) and w_ada.shape[0] == 1
    d = D_MODEL
    x2 = x.reshape(TOKENS, d)

    c_pad = jnp.pad(c, ((0, SUBLANES - BATCH), (0, 0)))
    mod = _ada(c_pad, w_ada[0], b_ada)[:BATCH].reshape(BATCH, 6, d)

    mixer_in = _mixer_inputs(w_in, w_gk_fwd, b_gk_fwd, w_gk_bwd, b_gk_bwd)
    q, k, v, gg, laf, lab, aq, ak2, avt = _inproj(x2, mod, g_pre_mix, *mixer_in)
    o_f, o_b = _gla(q, k, v, laf, lab)
    o_att = _attn(attn_sink[0], aq, ak2, avt)

    wr_t = w_router[0].T
    wrh = wr_t.astype(BF16)
    wrl = (wr_t - wrh.astype(F32)).astype(BF16)
    x1, h2_tiles, top_i, gates, rank, counts = _post(
        o_f, o_b, gg, o_att, x2, mod, g_gla_out, g_post_mix, g_pre_ffn, w_out[0].astype(BF16), wrh, wrl,
        b_router[0][:, None])

    pos, blocks = _route(top_i, rank, counts)

    src = _sc_source_rows(pos.reshape(TOP_K * TOKENS), MOE_ROWS)
    h2_rows = h2_tiles.reshape(TOKENS, PACK_SUB, LANES)
    part_rows = MOE_ROWS // MOE_PARTS
    ys = None
    for p in range(MOE_PARTS):
        xs_p = _sc_gather_rows(h2_rows, src[p * part_rows:(p + 1) * part_rows])
        ys = _experts(p, blocks, xs_p.reshape(part_rows * PACK_SUB, LANES),
                      w_gate_up[0], b_gate_up[0], w_down[0], b_down[0], ys)

    ys_rows = ys.reshape(MOE_ROWS, PACK_SUB, LANES)
    gates_t = gates.T
    part_tokens = TOKENS // COMB_PARTS
    out = None
    for p in range(COMB_PARTS):
        idx = pos[:, p * part_tokens:(p + 1) * part_tokens].reshape(TOP_K * part_tokens)
        y4_p = _sc_gather_rows(ys_rows, idx).reshape(TOP_K * part_tokens * PACK_SUB, LANES)
        out = _combine(p, gates_t, x1, mod, g_post_ffn, y4_p, out)
    return out.reshape(BATCH, SEQ, d)
```

```python
import functools

import jax
import jax.numpy as jnp
import numpy as np
from jax import lax
from jax.experimental import pallas as pl
from jax.experimental.pallas import tpu as pltpu
from jax.experimental.pallas import tpu_sc as plsc

F32 = jnp.float32
BF16 = jnp.bfloat16
I32 = jnp.int32

D_MODEL = 1024
BATCH = 2
SEQ = 8192
TOKENS = BATCH * SEQ
GLA_HEADS = 4
GLA_DV = 128
GLA_DK = 64
GLA_RANK = 16
GLA_GATE_NORMALIZER = 16.0
GLA_CHUNK = 64
ATT_Q_HEADS = 8
ATT_KV_HEADS = 2
ATT_HEAD_DIM = 64
ATT_WINDOW = 128
ATT_BLOCK = 128
ROT_DIM = 16
ROPE_THETA = 500000.0
N_EXPERTS = 32
TOP_K = 4
D_FF = 1024
SWIGLU_LIMIT = 7.0
SWIGLU_ALPHA = 1.702
NORM_EPS = 1e-6
NEG_INF = -1e30

LANES = 128
SUBLANES = 8
PACK_COLS = D_MODEL // 2
PACK_SUB = PACK_COLS // LANES

TM_IN = 1024
IN_SUB = 4
GLA_GROUP = 8
ATT_GROUP = 8
TM_POST = 1024
POST_SUB = 4
MOE_BM = 256
MOE_ROWS = TOKENS * TOP_K + N_EXPERTS * MOE_BM
MOE_NB = MOE_ROWS // MOE_BM
MOE_SUB = 2
MOE_FF_TILE = 256
MOE_PARTS = 3
COMB_PARTS = 1
MOE_NB_PAD = ((MOE_NB + LANES - 1) // LANES) * LANES
SC_SCAN_CHUNK = 4096
SC_SCAN_UNROLL = 8
TM_COMB = 256
SC_GATHER_WINDOW = 64

NT_DIMS = (((1,), (1,)), ((), ()))
TN_DIMS = (((0,), (0,)), ((), ()))


def _params(semantics, vmem_mib):
    return pltpu.CompilerParams(dimension_semantics=semantics, vmem_limit_bytes=vmem_mib * 1024 * 1024)


def _rms(x, g):
    return x * lax.rsqrt(jnp.mean(x * x, axis=-1, keepdims=True) + NORM_EPS) * g


def _silu(x):
    return x * jax.nn.sigmoid(x)


def _pack_rows(ref, v):
    m = v.shape[0]
    bits = lax.bitcast_convert_type(v, jnp.uint32)
    word = lax.bitcast_convert_type(bits[:, :PACK_COLS] | (bits[:, PACK_COLS:] >> 16), I32)
    for s in range(PACK_SUB):
        ref[pl.ds(s, m, stride=PACK_SUB), :] = word[:, s * LANES:(s + 1) * LANES]


def _unpack_rows(ref, m):
    word = jnp.concatenate([ref[pl.ds(s, m, stride=PACK_SUB), :] for s in range(PACK_SUB)], axis=1)
    bits = lax.bitcast_convert_type(word, jnp.uint32)
    hi = lax.bitcast_convert_type(bits & jnp.uint32(0xFFFF0000), F32)
    lo = lax.bitcast_convert_type(bits << 16, F32)
    return hi, lo


def _ada_body(c_ref, w_ref, b_ref, o_ref):
    ca = _silu(c_ref[...]).astype(BF16)
    o_ref[...] = jnp.dot(ca, w_ref[...].astype(BF16), preferred_element_type=F32) + b_ref[...]


def _ada(c_pad, w_ada, b_ada):
    d = D_MODEL
    return pl.pallas_call(
        _ada_body,
        grid=(6,),
        in_specs=[
            pl.BlockSpec((SUBLANES, d), lambda j: (0, 0)),
            pl.BlockSpec((d, d), lambda j: (0, j)),
            pl.BlockSpec((1, d), lambda j: (0, j)),
        ],
        out_specs=pl.BlockSpec((SUBLANES, d), lambda j: (0, j)),
        out_shape=jax.ShapeDtypeStruct((SUBLANES, 6 * d), F32),
        compiler_params=_params(("arbitrary",), 32),
        name="ada",
    )(c_pad, w_ada, b_ada)


def _rotary(x, cos_t, msin_t, psin_t):
    width = x.shape[1]
    reps = width // LANES
    c = jnp.concatenate([cos_t] * reps, axis=1)
    m = jnp.concatenate([msin_t] * reps, axis=1)
    p = jnp.concatenate([psin_t] * reps, axis=1)
    half = ROT_DIM // 2
    return x * c + pltpu.roll(x, width - half, 1) * m + pltpu.roll(x, half, 1) * p


def _inproj_body(x_ref, mod_ref, g_ref, wa_ref, wlr_ref, wgk_ref, bgk_ref, wb_ref, wvt_ref, rc_ref, rm_ref, rp_ref,
                 q_ref, k_ref, v_ref, gg_ref, laf_ref, lab_ref, aq_ref, ak_ref, avt_ref):
    shift = mod_ref[0:1, :]
    scale = mod_ref[1:2, :]
    hk = GLA_HEADS * GLA_DK
    hv = GLA_HEADS * GLA_DV
    nq = ATT_Q_HEADS * ATT_HEAD_DIM
    nk = 2 * ATT_KV_HEADS * ATT_HEAD_DIM
    sub = x_ref.shape[0] // IN_SUB
    subs = [slice(s * sub, (s + 1) * sub) for s in range(IN_SUB)]

    def hidden(rows):
        return (_rms(x_ref[rows, :], g_ref[...]) * (1.0 + scale) + shift).astype(BF16)

    def project(h):
        return (jnp.dot(h, wa_ref[...], preferred_element_type=F32),
                jnp.dot(h, wlr_ref[...], preferred_element_type=F32),
                jnp.dot(h, wb_ref[...], preferred_element_type=F32),
                lax.dot_general(wvt_ref[...], h, NT_DIMS, preferred_element_type=F32))

    def finish(rows, pa, plr, pb, pvt):
        q_ref[rows, :] = pa[:, 0:hk] * (GLA_DK ** -0.5)
        k_ref[rows, :] = pa[:, hk:2 * hk]
        v_ref[rows, :] = pa[:, 2 * hk:2 * hk + hv].astype(BF16)
        gg_ref[rows, :] = pa[:, 2 * hk + hv:2 * hk + 2 * hv]
        gk = jnp.dot(plr.astype(BF16), wgk_ref[...], preferred_element_type=F32) + bgk_ref[...]
        la = (jnp.minimum(gk, 0.0) - jnp.log1p(jnp.exp(-jnp.abs(gk)))) * (1.0 / GLA_GATE_NORMALIZER)
        laf_ref[rows, :] = la[:, 0:hk]
        lab_ref[rows, :] = la[:, hk:2 * hk]
        rc, rm, rp = rc_ref[rows, :], rm_ref[rows, :], rp_ref[rows, :]
        aq_ref[rows, :] = (_rotary(pb[:, 0:nq], rc, rm, rp) * (ATT_HEAD_DIM ** -0.5)).astype(BF16)
        ak_ref[rows, :] = _rotary(pb[:, nq:nq + nk], rc, rm, rp).astype(BF16)
        avt_ref[:, rows] = pvt.astype(BF16)

    hs = [hidden(rows) for rows in subs]
    ps = [project(h) for h in hs]
    for rows, p in zip(subs, ps):
        finish(rows, *p)


def _inproj(x2, mod, g_pre, wa, wlr, wgk, bgk, wb, wvt, rc, rm, rp):
    t, d = x2.shape
    tm = TM_IN
    tiles_per_seq = SEQ // tm
    hk = GLA_HEADS * GLA_DK
    hv = GLA_HEADS * GLA_DV
    nq = ATT_Q_HEADS * ATT_HEAD_DIM
    nk = 2 * ATT_KV_HEADS * ATT_HEAD_DIM

    def full(a):
        return pl.BlockSpec(a.shape, lambda i: (0,) * a.ndim)

    def rows(w):
        return pl.BlockSpec((tm, w), lambda i: (i, 0))

    def table():
        return pl.BlockSpec((tm, LANES), lambda i: (i % tiles_per_seq, 0))

    out_widths = [(hk, F32), (hk, F32), (hv, BF16), (hv, F32), (hk, F32), (hk, F32), (nq, BF16), (nk, BF16)]
    return pl.pallas_call(
        _inproj_body,
        grid=(t // tm,),
        in_specs=[
            rows(d),
            pl.BlockSpec((None, 6, d), lambda i: (i // tiles_per_seq, 0, 0)),
            full(g_pre), full(wa), full(wlr), full(wgk), full(bgk), full(wb), full(wvt),
            table(), table(), table(),
        ],
        out_specs=[rows(w) for w, _ in out_widths] + [pl.BlockSpec((nk, tm), lambda i: (0, i))],
        out_shape=[jax.ShapeDtypeStruct((t, w), dt) for w, dt in out_widths] + [jax.ShapeDtypeStruct((nk, t), BF16)],
        compiler_params=_params(("arbitrary",), 56),
        name="inproj",
    )(x2, mod, g_pre, wa, wlr, wgk, bgk, wb, wvt, rc, rm, rp)


def _gla_body(qf_ref, kf_ref, vf_ref, laf_ref, qb_ref, kb_ref, vb_ref, lab_ref, of_ref, ob_ref, sf_ref, sb_ref):
    @pl.when(pl.program_id(1) == 0)
    def _():
        sf_ref[...] = jnp.zeros_like(sf_ref)
        sb_ref[...] = jnp.zeros_like(sb_ref)

    c = GLA_CHUNK
    r_i = lax.broadcasted_iota(I32, (c, c), 0)
    c_i = lax.broadcasted_iota(I32, (c, c), 1)
    lower = c_i <= r_i
    upper = c_i >= r_i
    cum_f = jnp.where(lower, 1.0, 0.0).astype(BF16)
    cum_b = jnp.where(upper, 1.0, 0.0).astype(BF16)
    lane = lax.broadcasted_iota(I32, (1, LANES), 1)
    head_masks = (lane < GLA_DK, lane >= GLA_DK)

    fwd = [(qf_ref, kf_ref, laf_ref, vf_ref, of_ref, slice(g * c, (g + 1) * c), cum_f, lower, c - 1, c // 2 - 1)
           for g in range(GLA_GROUP)]
    bwd = [(qb_ref, kb_ref, lab_ref, vb_ref, ob_ref, slice(g * c, (g + 1) * c), cum_b, upper, 0, c // 2)
           for g in reversed(range(GLA_GROUP))]
    heads = range(GLA_HEADS)
    pair = [slice((h // 2) * LANES, (h // 2 + 1) * LANES) for h in heads]
    vcols = [slice(h * GLA_DV, (h + 1) * GLA_DV) for h in heads]

    def stage1(item):
        q_ref, k_ref, la_ref, v_ref, o_ref, rows, cum, tri, i_last, i_mid = item
        la = la_ref[rows, :]
        hi = la.astype(BF16)
        lo = (la - hi.astype(F32)).astype(BF16)
        b = jnp.dot(cum, hi, preferred_element_type=F32) + jnp.dot(cum, lo, preferred_element_type=F32)
        b_last = b[i_last:i_last + 1, :]
        b_mid = b[i_mid:i_mid + 1, :]
        q, k = q_ref[rows, :], k_ref[rows, :]
        return (q * jnp.exp(b - b_mid), (k * jnp.exp(b_mid - b)).astype(BF16), q * jnp.exp(b),
                (k * jnp.exp(b_last - b)).astype(BF16), jnp.exp(b_last))

    def stage2(item, pre):
        v_ref, rows, tri = item[3], item[5], item[7]
        qs, ks, qi, kst, decay = pre
        out = []
        for h in heads:
            mask = head_masks[h % 2]
            qs_h = jnp.where(mask, qs[:, pair[h]], 0.0).astype(BF16)
            sc = lax.dot_general(qs_h, ks[:, pair[h]], NT_DIMS, preferred_element_type=F32)
            v_h = v_ref[rows, vcols[h]]
            kv = lax.dot_general(v_h, kst[:, pair[h]], TN_DIMS, preferred_element_type=F32)
            out.append((jnp.where(tri, sc, 0.0).astype(BF16), kv,
                        jnp.where(mask, qi[:, pair[h]], 0.0).astype(BF16), v_h))
        return out

    def run(items, s_ref):
        pre = [stage1(it) for it in items]
        mid = [stage2(it, p) for it, p in zip(items, pre)]
        states = [s_ref[h] for h in heads]
        for it, p, m in zip(items, pre, mid):
            o_ref, rows, decay = it[4], it[5], p[4]
            for h in heads:
                sc, kv, qi_h, v_h = m[h]
                o = jnp.dot(sc, v_h, preferred_element_type=F32)
                o = o + lax.dot_general(qi_h, states[h].astype(BF16), NT_DIMS, preferred_element_type=F32)
                o_ref[rows, vcols[h]] = o
                states[h] = states[h] * decay[:, pair[h]] + kv
        for h in heads:
            s_ref[h] = states[h]

    run(fwd, sf_ref)
    run(bwd, sb_ref)


def _gla(q, k, v, laf, lab):
    t = q.shape[0]
    rows = GLA_GROUP * GLA_CHUNK
    ng = SEQ // rows
    hk = GLA_HEADS * GLA_DK
    hv = GLA_HEADS * GLA_DV

    def fwd(w):
        return pl.BlockSpec((rows, w), lambda b, n: (b * ng + n, 0))

    def bwd(w):
        return pl.BlockSpec((rows, w), lambda b, n: (b * ng + ng - 1 - n, 0))

    return pl.pallas_call(
        _gla_body,
        grid=(BATCH, ng),
        in_specs=[fwd(hk), fwd(hk), fwd(hv), fwd(hk), bwd(hk), bwd(hk), bwd(hv), bwd(hk)],
        out_specs=[fwd(hv), bwd(hv)],
        out_shape=[jax.ShapeDtypeStruct((t, hv), F32)] * 2,
        scratch_shapes=[pltpu.VMEM((GLA_HEADS, GLA_DV, 2 * GLA_DK), F32)] * 2,
        compiler_params=_params(("arbitrary", "arbitrary"), 32),
        name="gla",
    )(q, k, v, laf, q, k, v, lab)


def _attn_body(sink_ref, q_ref, kp_ref, kc_ref, kn_ref, vp_ref, vc_ref, vn_ref, o_ref):
    step = pl.program_id(1)
    last = pl.num_programs(1) - 1
    qb = ATT_BLOCK
    hd = ATT_HEAD_DIM
    k_all = jnp.concatenate([kp_ref[...], kc_ref[...], kn_ref[...]], axis=0)
    vt_all = jnp.concatenate([vp_ref[...], vc_ref[...], vn_ref[...]], axis=1)
    lane = lax.broadcasted_iota(I32, (1, LANES), 1)
    lo = lane < hd
    j_k = lax.broadcasted_iota(I32, (3 * qb, qb), 0)
    i_q = lax.broadcasted_iota(I32, (3 * qb, qb), 1)
    band = jnp.abs(j_k - qb - i_q) <= ATT_WINDOW
    sinks = [jnp.concatenate([jnp.full((1, qb), sink_ref[4 * g + r], F32) for r in range(4)], axis=1)
             for g in range(ATT_KV_HEADS)]
    work = [(j, g) for j in range(ATT_GROUP) for g in range(ATT_KV_HEADS)]

    def scores(j, g):
        valid = band
        if j == 0:
            valid = valid & ((j_k >= qb) | (step > 0))
        if j == ATT_GROUP - 1:
            valid = valid & ((j_k < 2 * qb) | (step < last))
        valid4 = jnp.concatenate([valid] * 4, axis=1)
        rows = slice(j * qb, (j + 1) * qb)
        kg = k_all[j * qb:(j + 3) * qb, g * LANES:(g + 1) * LANES]
        qa = q_ref[rows, (2 * g) * LANES:(2 * g + 1) * LANES]
        qc = q_ref[rows, (2 * g + 1) * LANES:(2 * g + 2) * LANES]
        zero = jnp.zeros_like(qa)
        lhs = jnp.concatenate([jnp.where(lo, qa, zero), jnp.where(lo, zero, qa),
                               jnp.where(lo, qc, zero), jnp.where(lo, zero, qc)], axis=0)
        st = lax.dot_general(kg, lhs, NT_DIMS, preferred_element_type=F32)
        return jnp.where(valid4, st, NEG_INF)

    def softmax(st, g):
        sink = sinks[g]
        m = jnp.maximum(jnp.max(st, axis=0, keepdims=True), sink)
        p = jnp.exp(st - m)
        return p.astype(BF16), 1.0 / (jnp.sum(p, axis=0, keepdims=True) + jnp.exp(sink - m))

    def output(j, g, p, inv):
        rows = slice(j * qb, (j + 1) * qb)
        vgt = vt_all[g * LANES:(g + 1) * LANES, j * qb:(j + 3) * qb]
        ot = jnp.dot(vgt, p, preferred_element_type=F32) * inv
        pair_a = jnp.concatenate([ot[0:hd, 0:qb], ot[hd:2 * hd, qb:2 * qb]], axis=0)
        pair_c = jnp.concatenate([ot[0:hd, 2 * qb:3 * qb], ot[hd:2 * hd, 3 * qb:4 * qb]], axis=0)
        o_ref[rows, (2 * g) * LANES:(2 * g + 1) * LANES] = pair_a.T.astype(o_ref.dtype)
        o_ref[rows, (2 * g + 1) * LANES:(2 * g + 2) * LANES] = pair_c.T.astype(o_ref.dtype)

    s_all = [scores(j, g) for j, g in work]
    p_all = [softmax(st, g) for st, (j, g) in zip(s_all, work)]
    for (j, g), (p, inv) in zip(work, p_all):
        output(j, g, p, inv)


def _attn(sink, aq, ak2, avt):
    t = aq.shape[0]
    qb = ATT_BLOCK
    nb = SEQ // qb
    steps = nb // ATT_GROUP
    nq = ATT_Q_HEADS * ATT_HEAD_DIM
    nk = 2 * ATT_KV_HEADS * ATT_HEAD_DIM

    def edge_block(b, n, shift):
        return b * nb + jnp.clip(n * ATT_GROUP + shift, 0, nb - 1)

    def k_edge(shift):
        return pl.BlockSpec((qb, nk), lambda b, n: (edge_block(b, n, shift), 0))

    def v_edge(shift):
        return pl.BlockSpec((nk, qb), lambda b, n: (0, edge_block(b, n, shift)))

    def group(w):
        return pl.BlockSpec((ATT_GROUP * qb, w), lambda b, n: (b * steps + n, 0))

    v_group = pl.BlockSpec((nk, ATT_GROUP * qb), lambda b, n: (0, b * steps + n))
    return pl.pallas_call(
        _attn_body,
        grid=(BATCH, steps),
        in_specs=[
            pl.BlockSpec(memory_space=pltpu.SMEM),
            group(nq),
            k_edge(-1), group(nk), k_edge(ATT_GROUP), v_edge(-1), v_group, v_edge(ATT_GROUP),
        ],
        out_specs=group(nq),
        out_shape=jax.ShapeDtypeStruct((t, nq), BF16),
        compiler_params=_params(("arbitrary", "arbitrary"), 48),
        name="attn",
    )(sink, aq, ak2, ak2, ak2, avt, avt, avt)


def _post_body(of_ref, ob_ref, gg_ref, oa_ref, x_ref, mod_ref, ggla_ref, gpm_ref, gpf_ref, wout_ref,
               wrh_ref, wrl_ref, br_ref,
               x1_ref, h2_ref, ti_ref, gt_ref, rk_ref, cnt_ref, base_ref):
    tm = TM_POST

    @pl.when(pl.program_id(0) == 0)
    def _():
        base_ref[...] = jnp.zeros_like(base_ref)

    gate1 = mod_ref[2:3, :]
    shift2 = mod_ref[3:4, :]
    scale2 = mod_ref[4:5, :]
    sub = tm // POST_SUB
    subs = [slice(s * sub, (s + 1) * sub) for s in range(POST_SUB)]

    def mixer_out(rows):
        og = of_ref[rows, :] + ob_ref[rows, :]
        gg = gg_ref[rows, :]
        parts = []
        for h in range(GLA_HEADS):
            cols = slice(h * GLA_DV, (h + 1) * GLA_DV)
            parts.append((_rms(og[:, cols], ggla_ref[...]) * _silu(gg[:, cols])).astype(BF16))
        return jnp.concatenate(parts + [oa_ref[rows, :]], axis=1)

    def ffn_in(s, y):
        rows = subs[s]
        x1 = x_ref[rows, :] + gate1 * _rms(y, gpm_ref[...])
        x1_ref[rows, :] = x1
        h2 = _rms(x1, gpf_ref[...]) * (1.0 + scale2) + shift2
        hi = h2.astype(BF16)
        hi32 = hi.astype(F32)
        _pack_rows(h2_ref.at[pl.ds(s * sub * PACK_SUB, sub * PACK_SUB), :], hi32)
        return hi, (h2 - hi32).astype(BF16)

    o_subs = [mixer_out(rows) for rows in subs]
    y_subs = [jnp.dot(o, wout_ref[...], preferred_element_type=F32) for o in o_subs]
    split = [ffn_in(s, y) for s, y in enumerate(y_subs)]
    h2_hi = jnp.concatenate([hi for hi, _ in split], axis=0)
    h2_lo = jnp.concatenate([lo for _, lo in split], axis=0)

    wrh = wrh_ref[...]
    logits = (lax.dot_general(wrh, h2_hi, NT_DIMS, preferred_element_type=F32)
              + lax.dot_general(wrh, h2_lo, NT_DIMS, preferred_element_type=F32)
              + lax.dot_general(wrl_ref[...], h2_hi, NT_DIMS, preferred_element_type=F32)
              + br_ref[...])
    e_iota = lax.broadcasted_iota(I32, (N_EXPERTS, tm), 0)
    idxs, vals = [], []
    work = logits
    for _ in range(TOP_K):
        m = jnp.max(work, axis=0, keepdims=True)
        idx = jnp.min(jnp.where(work == m, e_iota, N_EXPERTS), axis=0, keepdims=True)
        idxs.append(idx)
        vals.append(m)
        work = jnp.where(e_iota == idx, -jnp.inf, work)
    exps = [jnp.exp(v - vals[0]) for v in vals]
    inv = 1.0 / (exps[0] + exps[1] + exps[2] + exps[3])
    gt_ref[...] = jnp.concatenate([e * inv for e in exps], axis=0)
    ti_ref[...] = jnp.concatenate(idxs, axis=0)

    onehots = [e_iota == idx for idx in idxs]
    member = jnp.where(onehots[0] | onehots[1] | onehots[2] | onehots[3], 1.0, 0.0)
    t_row = lax.broadcasted_iota(I32, (tm, tm), 0)
    t_col = lax.broadcasted_iota(I32, (tm, tm), 1)
    strict = jnp.where(t_row < t_col, 1.0, 0.0).astype(BF16)
    before = base_ref[...] + jnp.dot(member.astype(BF16), strict, preferred_element_type=F32)
    rk_ref[...] = jnp.concatenate(
        [jnp.sum(jnp.where(oh, before, 0.0), axis=0, keepdims=True) for oh in onehots], axis=0).astype(I32)
    new_base = base_ref[...] + jnp.sum(member, axis=1, keepdims=True)
    base_ref[...] = new_base
    cnt_ref[...] = jnp.broadcast_to(new_base, cnt_ref.shape)


def _post(o_f, o_b, gg, o_att, x2, mod, g_gla, g_pm, g_pf, wout, wrh, wrl, br):
    t, d = x2.shape
    tm = TM_POST
    tiles_per_seq = SEQ // tm
    hv = GLA_HEADS * GLA_DV

    def full(a):
        return pl.BlockSpec(a.shape, lambda i: (0,) * a.ndim)

    def rows(w):
        return pl.BlockSpec((tm, w), lambda i: (i, 0))

    def lanes():
        return pl.BlockSpec((TOP_K, tm), lambda i: (0, i))

    return pl.pallas_call(
        _post_body,
        grid=(t // tm,),
        in_specs=[
            rows(hv), rows(hv), rows(hv), rows(hv), rows(d),
            pl.BlockSpec((None, 6, d), lambda i: (i // tiles_per_seq, 0, 0)),
            full(g_gla), full(g_pm), full(g_pf), full(wout), full(wrh), full(wrl), full(br),
        ],
        out_specs=[
            rows(d),
            pl.BlockSpec((tm * PACK_SUB, LANES), lambda i: (i, 0)),
            lanes(), lanes(), lanes(),
            pl.BlockSpec((N_EXPERTS, LANES), lambda i: (0, 0)),
        ],
        out_shape=[
            jax.ShapeDtypeStruct((t, d), F32),
            jax.ShapeDtypeStruct((t * PACK_SUB, LANES), I32),
            jax.ShapeDtypeStruct((TOP_K, t), I32),
            jax.ShapeDtypeStruct((TOP_K, t), F32),
            jax.ShapeDtypeStruct((TOP_K, t), I32),
            jax.ShapeDtypeStruct((N_EXPERTS, LANES), F32),
        ],
        scratch_shapes=[pltpu.VMEM((N_EXPERTS, 1), F32)],
        compiler_params=_params(("arbitrary",), 48),
        name="post",
    )(o_f, o_b, gg, o_att, x2, mod, g_gla, g_pm, g_pf, wout, wrh, wrl, br)


def _route_body(ti_ref, rk_ref, cnt_ref, pos_ref, blk_ref):
    cnt = cnt_ref[...]
    padded = jnp.floor((cnt + (MOE_BM - 1)) * (1.0 / MOE_BM)) * MOE_BM
    starts, ends = [], []
    acc = jnp.zeros((1, LANES), F32)
    for e in range(N_EXPERTS):
        starts.append(acc)
        acc = acc + padded[e:e + 1, :]
        ends.append(acc)
    ti = ti_ref[...]
    off = jnp.zeros(ti.shape, F32)
    for e in range(N_EXPERTS):
        off = jnp.where(ti == e, starts[e][:, 0:1], off)
    pos_ref[...] = rk_ref[...] + off.astype(I32)

    def owner_of(row):
        n_le = jnp.zeros(row.shape, I32)
        for e in range(N_EXPERTS):
            n_le = n_le + jnp.where(ends[e][:, 0:1] <= row, 1, 0)
        return jnp.minimum(n_le, N_EXPERTS - 1)

    block_start = lax.broadcasted_iota(I32, (1, MOE_NB_PAD), 1).astype(F32) * MOE_BM
    owner = owner_of(block_start)
    nxt = jnp.zeros((1, MOE_NB_PAD), I32)
    nxt_blk = jnp.zeros((1, MOE_NB_PAD), I32)
    for e in range(N_EXPERTS):
        end_e = ends[e][:, 0:1]
        nxt = jnp.where(owner == e, jnp.where(end_e < acc[:, 0:1], owner_of(end_e), -1), nxt)
        nxt_blk = jnp.where(owner == e, (end_e * (1.0 / MOE_BM)).astype(I32), nxt_blk)
    used = jnp.broadcast_to((acc[:, 0:1] * (1.0 / MOE_BM)).astype(I32), (1, MOE_NB_PAD))
    blk_ref[...] = jnp.concatenate([owner, nxt, used, nxt_blk, jnp.zeros((SUBLANES - 4, MOE_NB_PAD), I32)], axis=0)


def _route(top_i, rank, counts):
    return pl.pallas_call(
        _route_body,
        out_shape=[
            jax.ShapeDtypeStruct(top_i.shape, I32),
            jax.ShapeDtypeStruct((SUBLANES, MOE_NB_PAD), I32),
        ],
        compiler_params=pltpu.CompilerParams(vmem_limit_bytes=32 * 1024 * 1024),
        name="route",
    )(top_i, rank, counts)


def _sc_workers():
    info = plsc.get_sparse_core_info()
    return info.num_cores, info.num_subcores, info.num_lanes


def _sc_gather_loop(table_hbm, out_hbm, idx_v, base, chunks, buf0, buf1, sem0, sem1):
    window = SC_GATHER_WINDOW

    def fetch(c, buf, sem):
        return pltpu.make_async_copy(table_hbm.at[idx_v.at[pl.ds(c * window, window)]], buf, sem)

    def flush(c, buf):
        pltpu.sync_copy(buf, out_hbm.at[pl.ds(base + c * window, window)])

    fetch(0, buf0, sem0).start()

    @pl.loop(0, chunks, step=2)
    def _(c):
        fetch(c + 1, buf1, sem1).start()
        fetch(c, buf0, sem0).wait()
        flush(c, buf0)

        @pl.when(c + 2 < chunks)
        def _():
            fetch(c + 2, buf0, sem0).start()

        fetch(c + 1, buf1, sem1).wait()
        flush(c + 1, buf1)


def _sc_source_rows(pos_flat, n_rows):
    cores, subcores, lanes = _sc_workers()
    workers = cores * subcores
    per_worker = n_rows // workers
    n_assign = pos_flat.shape[0]
    scan = SC_SCAN_CHUNK
    assert per_worker * workers == n_rows and per_worker % lanes == 0
    assert n_assign % scan == 0 and scan % lanes == 0
    mesh = plsc.VectorSubcoreMesh(core_axis_name="core", subcore_axis_name="subcore")

    @functools.partial(
        pl.kernel,
        out_type=jax.ShapeDtypeStruct((n_rows,), I32),
        mesh=mesh,
        scratch_types=[pltpu.VMEM((per_worker,), I32), pltpu.VMEM((scan,), I32)],
        compiler_params=pltpu.CompilerParams(needs_layout_passes=False),
        name="sc_source_rows",
    )
    def invert(pos_hbm, out_hbm, src_v, pos_v):
        wid = lax.axis_index("subcore") * cores + lax.axis_index("core")
        base = wid * per_worker
        lane = lax.iota(I32, lanes)

        @pl.loop(0, per_worker, step=lanes)
        def _(j):
            src_v[pl.ds(j, lanes)] = (base + j + lane) & (TOKENS - 1)

        @pl.loop(0, n_assign, step=scan)
        def _(a0):
            pltpu.sync_copy(pos_hbm.at[pl.ds(a0, scan)], pos_v)

            @plsc.parallel_loop(0, scan, step=lanes, unroll=SC_SCAN_UNROLL)
            def _(j):
                rel = pos_v[pl.ds(j, lanes)] - base
                mine = (rel >= 0) & (rel < per_worker)
                tok = (a0 + j + lane) & (TOKENS - 1)
                plsc.store_scatter(src_v, [jnp.where(mine, rel, 0)], tok, mask=mine)

        pltpu.sync_copy(src_v, out_hbm.at[pl.ds(base, per_worker)])

    return invert(pos_flat)


def _sc_gather_rows(table, idx):
    cores, subcores, _ = _sc_workers()
    workers = cores * subcores
    n = idx.shape[0]
    window = SC_GATHER_WINDOW
    per_worker = n // workers
    chunks = per_worker // window
    assert per_worker * workers == n and chunks * window == per_worker and chunks % 2 == 0
    row_shape = table.shape[1:]
    mesh = plsc.VectorSubcoreMesh(core_axis_name="core", subcore_axis_name="subcore")

    @functools.partial(
        pl.kernel,
        out_type=jax.ShapeDtypeStruct((n,) + row_shape, table.dtype),
        mesh=mesh,
        scratch_types=[
            pltpu.VMEM((per_worker,), I32),
            pltpu.VMEM((window,) + row_shape, table.dtype),
            pltpu.VMEM((window,) + row_shape, table.dtype),
            pltpu.SemaphoreType.DMA,
            pltpu.SemaphoreType.DMA,
        ],
        name="sc_gather_rows",
    )
    def gather(table_hbm, idx_hbm, out_hbm, idx_v, buf0, buf1, sem0, sem1):
        wid = lax.axis_index("subcore") * cores + lax.axis_index("core")
        base = wid * per_worker
        pltpu.sync_copy(idx_hbm.at[pl.ds(base, per_worker)], idx_v)
        _sc_gather_loop(table_hbm, out_hbm, idx_v, base, chunks, buf0, buf1, sem0, sem1)

    return gather(table, idx)


def _experts_body(first, be_ref, nxt_ref, nxtblk_ref, nu_ref, xs_ref, bgu_ref, bd_ref, wgu_hbm, wd_hbm, *rest):
    ys_ref, wgu_f32, wd_f32, wgu_bf, wd_bf, sems = rest[-6:]
    step = pl.program_id(0)
    end = first + pl.num_programs(0) * MOE_SUB
    bm = MOE_BM
    n_used = nu_ref[0]

    def fetch(e):
        return (pltpu.make_async_copy(wgu_hbm.at[e], wgu_f32, sems.at[0]),
                pltpu.make_async_copy(wd_hbm.at[e], wd_f32, sems.at[1]))

    @pl.when((step == 0) & (first < n_used))
    def _():
        for cp in fetch(be_ref[first]):
            cp.start()

    for sub in range(MOE_SUB):
        i = first + step * MOE_SUB + sub
        e = be_ref[i]
        used = i < n_used
        fresh = (i == first) | (e != be_ref[jnp.maximum(i - 1, 0)])
        rows = pl.ds(sub * bm * PACK_SUB, bm * PACK_SUB)
        xs_sub = xs_ref.at[rows, :]
        ys_sub = ys_ref.at[rows, :]

        @pl.when(used & fresh)
        def _():
            for cp in fetch(e):
                cp.wait()
            for j in range(D_FF // MOE_FF_TILE):
                cols = slice(j * MOE_FF_TILE, (j + 1) * MOE_FF_TILE)
                up_cols = slice(D_FF + j * MOE_FF_TILE, D_FF + (j + 1) * MOE_FF_TILE)
                wgu_bf[:, 2 * j * MOE_FF_TILE:(2 * j + 1) * MOE_FF_TILE] = wgu_f32[:, cols].astype(BF16)
                wgu_bf[:, (2 * j + 1) * MOE_FF_TILE:(2 * j + 2) * MOE_FF_TILE] = wgu_f32[:, up_cols].astype(BF16)
            wd_bf[...] = wd_f32[...].astype(BF16)

            @pl.when((nxt_ref[i] >= 0) & (nxtblk_ref[i] < end))
            def _():
                for cp in fetch(nxt_ref[i]):
                    cp.start()

        @pl.when(used)
        def _():
            x = jnp.concatenate(_unpack_rows(xs_sub, bm), axis=1).astype(BF16)
            gu = jnp.dot(x, wgu_bf[...], preferred_element_type=F32)
            bgu = bgu_ref[pl.ds(e, 1), :]
            y = bd_ref[pl.ds(e, 1), :]
            for j in range(D_FF // MOE_FF_TILE):
                lo = 2 * j * MOE_FF_TILE
                gate = gu[:, lo:lo + MOE_FF_TILE] + bgu[:, j * MOE_FF_TILE:(j + 1) * MOE_FF_TILE]
                up = (gu[:, lo + MOE_FF_TILE:lo + 2 * MOE_FF_TILE]
                      + bgu[:, D_FF + j * MOE_FF_TILE:D_FF + (j + 1) * MOE_FF_TILE])
                gate = jnp.minimum(gate, SWIGLU_LIMIT)
                up = jnp.clip(up, -SWIGLU_LIMIT, SWIGLU_LIMIT)
                act = ((up + 1.0) * gate * jax.nn.sigmoid(SWIGLU_ALPHA * gate)).astype(BF16)
                y = y + jnp.dot(act, wd_bf[j * MOE_FF_TILE:(j + 1) * MOE_FF_TILE, :], preferred_element_type=F32)
            _pack_rows(ys_sub, y.astype(BF16).astype(F32))

        @pl.when(jnp.logical_not(used))
        def _():
            ys_sub[...] = jnp.zeros((bm * PACK_SUB, LANES), I32)


def _experts(part, blocks, xs_part, w_gate_up, b_gate_up, w_down, b_down, ys_prev):
    rows = MOE_SUB * MOE_BM * PACK_SUB
    d = D_MODEL
    part_blocks = MOE_NB // MOE_PARTS
    steps = part_blocks // MOE_SUB
    first = part * part_blocks

    def x_block(i, be, nx, nb, nu):
        last = jnp.maximum((jnp.minimum(nu[0], first + part_blocks) - 1 - first) // MOE_SUB, 0)
        return jnp.minimum(i, last)

    in_specs = [
        pl.BlockSpec((rows, LANES), lambda i, be, nx, nb, nu: (x_block(i, be, nx, nb, nu), 0)),
        pl.BlockSpec(b_gate_up.shape, lambda i, be, nx, nb, nu: (0, 0)),
        pl.BlockSpec(b_down.shape, lambda i, be, nx, nb, nu: (0, 0)),
        pl.BlockSpec(memory_space=pl.ANY),
        pl.BlockSpec(memory_space=pl.ANY),
    ]
    operands = [blocks[0, :MOE_NB], blocks[1, :MOE_NB], blocks[3, :MOE_NB], blocks[2, :1],
                xs_part, b_gate_up, b_down, w_gate_up, w_down]
    aliases = {}
    if ys_prev is not None:
        in_specs.append(pl.BlockSpec(memory_space=pl.ANY))
        aliases = {len(operands): 0}
        operands.append(ys_prev)
    grid_spec = pltpu.PrefetchScalarGridSpec(
        num_scalar_prefetch=4,
        grid=(steps,),
        in_specs=in_specs,
        out_specs=pl.BlockSpec((rows, LANES), lambda i, be, nx, nb, nu: (first // MOE_SUB + i, 0)),
        scratch_shapes=[
            pltpu.VMEM((d, 2 * D_FF), F32), pltpu.VMEM((D_FF, d), F32),
            pltpu.VMEM((d, 2 * D_FF), BF16), pltpu.VMEM((D_FF, d), BF16),
            pltpu.SemaphoreType.DMA((2,)),
        ],
    )
    return pl.pallas_call(
        functools.partial(_experts_body, first),
        grid_spec=grid_spec,
        out_shape=jax.ShapeDtypeStruct((MOE_ROWS * PACK_SUB, LANES), I32),
        input_output_aliases=aliases,
        compiler_params=_params(("arbitrary",), 48),
        name="experts",
    )(*operands)


def _combine_body(gates_ref, x1_ref, mod_ref, gpost_ref, y0_ref, y1_ref, y2_ref, y3_ref, *rest):
    o_ref = rest[-1]
    tm = TM_COMB
    gates = gates_ref[...]
    y_hi = jnp.zeros((tm, PACK_COLS), F32)
    y_lo = jnp.zeros((tm, PACK_COLS), F32)
    for k, yk_ref in enumerate((y0_ref, y1_ref, y2_ref, y3_ref)):
        hi, lo = _unpack_rows(yk_ref, tm)
        y_hi = y_hi + hi * gates[:, k:k + 1]
        y_lo = y_lo + lo * gates[:, k:k + 1]
    y = jnp.concatenate([y_hi, y_lo], axis=1)
    gate2 = mod_ref[5:6, :]
    o_ref[...] = x1_ref[...] + gate2 * _rms(y, gpost_ref[...])


def _combine(part, gates_t, x1, mod, g_post, y4_part, out_prev):
    t, d = x1.shape
    tm = TM_COMB
    tiles = t // COMB_PARTS // tm
    tile0 = part * tiles
    tiles_per_seq = SEQ // tm

    def slab(k):
        return pl.BlockSpec((tm * PACK_SUB, LANES), lambda i: (k * tiles + i, 0))

    in_specs = [
        pl.BlockSpec((tm, TOP_K), lambda i: (tile0 + i, 0)),
        pl.BlockSpec((tm, d), lambda i: (tile0 + i, 0)),
        pl.BlockSpec((None, 6, d), lambda i: ((tile0 + i) // tiles_per_seq, 0, 0)),
        pl.BlockSpec(g_post.shape, lambda i: (0, 0)),
        slab(0), slab(1), slab(2), slab(3),
    ]
    operands = [gates_t, x1, mod, g_post, y4_part, y4_part, y4_part, y4_part]
    aliases = {}
    if out_prev is not None:
        in_specs.append(pl.BlockSpec(memory_space=pl.ANY))
        aliases = {len(operands): 0}
        operands.append(out_prev)
    return pl.pallas_call(
        _combine_body,
        grid=(tiles,),
        in_specs=in_specs,
        out_specs=pl.BlockSpec((tm, d), lambda i: (tile0 + i, 0)),
        out_shape=jax.ShapeDtypeStruct((t, d), F32),
        input_output_aliases=aliases,
        compiler_params=_params(("arbitrary",), 48),
        name="combine",
    )(*operands)


def _rotary_tables():
    half = ROT_DIM // 2
    inv_freq = ROPE_THETA ** (-2.0 * np.arange(half, dtype=np.float32) / ROT_DIM)
    ang = np.arange(SEQ, dtype=np.float32)[:, None] * inv_freq[None, :].astype(np.float32)
    cos, sin = np.cos(ang), np.sin(ang)
    ones = np.ones((SEQ, ATT_HEAD_DIM - ROT_DIM), np.float32)
    zeros = np.zeros((SEQ, ATT_HEAD_DIM - ROT_DIM), np.float32)
    zh = np.zeros((SEQ, half), np.float32)
    reps = LANES // ATT_HEAD_DIM
    rc = np.tile(np.concatenate([cos, cos, ones], axis=1), (1, reps))
    rm = np.tile(np.concatenate([-sin, zh, zeros], axis=1), (1, reps))
    rp = np.tile(np.concatenate([zh, sin, zeros], axis=1), (1, reps))
    return tuple(jnp.asarray(t, F32) for t in (rc, rm, rp))


def _mixer_inputs(w_in, w_gk_fwd, b_gk_fwd, w_gk_bwd, b_gk_bwd):
    hk = GLA_HEADS * GLA_DK
    hv = GLA_HEADS * GLA_DV
    w = w_in[0]
    o_lr = 2 * hk + 2 * hv
    o_aq = o_lr + 2 * GLA_RANK
    o_ak = o_aq + ATT_Q_HEADS * ATT_HEAD_DIM
    o_av = o_ak + ATT_KV_HEADS * ATT_HEAD_DIM
    hd = ATT_HEAD_DIM
    wa = w[:, :o_lr].astype(BF16)
    wlr = w[:, o_lr:o_aq].astype(BF16)
    dup = lambda m: jnp.concatenate([m[:, g * hd:(g + 1) * hd] for g in range(ATT_KV_HEADS) for _ in range(2)], axis=1)
    wb = jnp.concatenate([w[:, o_aq:o_ak], dup(w[:, o_ak:o_av])], axis=1).astype(BF16)
    wvt = dup(w[:, o_av:o_av + ATT_KV_HEADS * hd]).T.astype(BF16)
    zr = jnp.zeros((GLA_RANK, hk), F32)
    wgk = jnp.concatenate([jnp.concatenate([w_gk_fwd[0], zr], axis=1),
                           jnp.concatenate([zr, w_gk_bwd[0]], axis=1)], axis=0).astype(BF16)
    bgk = jnp.concatenate([b_gk_fwd[0], b_gk_bwd[0]])[None, :]
    return (wa, wlr, wgk, bgk, wb, wvt) + _rotary_tables()


def kernel(x, c, w_ada, b_ada, g_pre_mix, g_post_mix, w_in, w_gk_fwd, b_gk_fwd, w_gk_bwd, b_gk_bwd, g_gla_out,
           attn_sink, w_out, g_pre_ffn, g_post_ffn, w_router, b_router, w_gate_up, b_gate_up, w_down, b_down):
    assert x.shape == (BATCH, SEQ, D_MODEL) and w_ada.shape[0] == 1
    d = D_MODEL
    x2 = x.reshape(TOKENS, d)

    c_pad = jnp.pad(c, ((0, SUBLANES - BATCH), (0, 0)))
    mod = _ada(c_pad, w_ada[0], b_ada)[:BATCH].reshape(BATCH, 6, d)

    mixer_in = _mixer_inputs(w_in, w_gk_fwd, b_gk_fwd, w_gk_bwd, b_gk_bwd)
    q, k, v, gg, laf, lab, aq, ak2, avt = _inproj(x2, mod, g_pre_mix, *mixer_in)
    o_f, o_b = _gla(q, k, v, laf, lab)
    o_att = _attn(attn_sink[0], aq, ak2, avt)

    wr_t = w_router[0].T
    wrh = wr_t.astype(BF16)
    wrl = (wr_t - wrh.astype(F32)).astype(BF16)
    x1, h2_tiles, top_i, gates, rank, counts = _post(
        o_f, o_b, gg, o_att, x2, mod, g_gla_out, g_post_mix, g_pre_ffn, w_out[0].astype(BF16), wrh, wrl,
        b_router[0][:, None])

    pos, blocks = _route(top_i, rank, counts)

    src = _sc_source_rows(pos.reshape(TOP_K * TOKENS), MOE_ROWS)
    h2_rows = h2_tiles.reshape(TOKENS, PACK_SUB, LANES)
    part_rows = MOE_ROWS // MOE_PARTS
    ys = None
    for p in range(MOE_PARTS):
        xs_p = _sc_gather_rows(h2_rows, src[p * part_rows:(p + 1) * part_rows])
        ys = _experts(p, blocks, xs_p.reshape(part_rows * PACK_SUB, LANES),
                      w_gate_up[0], b_gate_up[0], w_down[0], b_down[0], ys)

    ys_rows = ys.reshape(MOE_ROWS, PACK_SUB, LANES)
    gates_t = gates.T
    part_tokens = TOKENS // COMB_PARTS
    out = None
    for p in range(COMB_PARTS):
        idx = pos[:, p * part_tokens:(p + 1) * part_tokens].reshape(TOP_K * part_tokens)
        y4_p = _sc_gather_rows(ys_rows, idx).reshape(TOP_K * part_tokens * PACK_SUB, LANES)
        out = _combine(p, gates_t, x1, mod, g_post_ffn, y4_p, out)
    return out.reshape(BATCH, SEQ, d)
```

```python
import functools

import jax
import jax.numpy as jnp
import numpy as np
from jax import lax
from jax.experimental import pallas as pl
from jax.experimental.pallas import tpu as pltpu
from jax.experimental.pallas import tpu_sc as plsc

F32 = jnp.float32
BF16 = jnp.bfloat16
I32 = jnp.int32

D_MODEL = 1024
BATCH = 2
SEQ = 8192
TOKENS = BATCH * SEQ
GLA_HEADS = 4
GLA_DV = 128
GLA_DK = 64
GLA_RANK = 16
GLA_GATE_NORMALIZER = 16.0
GLA_CHUNK = 64
ATT_Q_HEADS = 8
ATT_KV_HEADS = 2
ATT_HEAD_DIM = 64
ATT_WINDOW = 128
ATT_BLOCK = 128
ROT_DIM = 16
ROPE_THETA = 500000.0
N_EXPERTS = 32
TOP_K = 4
D_FF = 1024
SWIGLU_LIMIT = 7.0
SWIGLU_ALPHA = 1.702
NORM_EPS = 1e-6
NEG_INF = -1e30

LANES = 128
SUBLANES = 8
PACK_COLS = D_MODEL // 2
PACK_SUB = PACK_COLS // LANES

TM_IN = 1024
IN_SUB = 4
GLA_GROUP = 8
ATT_GROUP = 8
TM_POST = 1024
POST_SUB = 4
MOE_BM = 256
MOE_ROWS = TOKENS * TOP_K + N_EXPERTS * MOE_BM
MOE_NB = MOE_ROWS // MOE_BM
MOE_SUB = 2
MOE_PARTS = 3
COMB_PARTS = 1
MOE_NB_PAD = ((MOE_NB + LANES - 1) // LANES) * LANES
SC_SCAN_CHUNK = 4096
SC_SCAN_UNROLL = 8
TM_COMB = 256
SC_GATHER_WINDOW = 64

NT_DIMS = (((1,), (1,)), ((), ()))
TN_DIMS = (((0,), (0,)), ((), ()))


def _params(semantics, vmem_mib):
    return pltpu.CompilerParams(dimension_semantics=semantics, vmem_limit_bytes=vmem_mib * 1024 * 1024)


def _rms(x, g):
    return x * lax.rsqrt(jnp.mean(x * x, axis=-1, keepdims=True) + NORM_EPS) * g


def _silu(x):
    return x * jax.nn.sigmoid(x)


def _pack_rows(ref, v):
    m = v.shape[0]
    bits = lax.bitcast_convert_type(v, jnp.uint32)
    word = lax.bitcast_convert_type(bits[:, :PACK_COLS] | (bits[:, PACK_COLS:] >> 16), I32)
    for s in range(PACK_SUB):
        ref[pl.ds(s, m, stride=PACK_SUB), :] = word[:, s * LANES:(s + 1) * LANES]


def _unpack_rows(ref, m):
    word = jnp.concatenate([ref[pl.ds(s, m, stride=PACK_SUB), :] for s in range(PACK_SUB)], axis=1)
    bits = lax.bitcast_convert_type(word, jnp.uint32)
    hi = lax.bitcast_convert_type(bits & jnp.uint32(0xFFFF0000), F32)
    lo = lax.bitcast_convert_type(bits << 16, F32)
    return hi, lo


def _ada_body(c_ref, w_ref, b_ref, o_ref):
    ca = _silu(c_ref[...]).astype(BF16)
    o_ref[...] = jnp.dot(ca, w_ref[...].astype(BF16), preferred_element_type=F32) + b_ref[...]


def _ada(c_pad, w_ada, b_ada):
    d = D_MODEL
    return pl.pallas_call(
        _ada_body,
        grid=(6,),
        in_specs=[
            pl.BlockSpec((SUBLANES, d), lambda j: (0, 0)),
            pl.BlockSpec((d, d), lambda j: (0, j)),
            pl.BlockSpec((1, d), lambda j: (0, j)),
        ],
        out_specs=pl.BlockSpec((SUBLANES, d), lambda j: (0, j)),
        out_shape=jax.ShapeDtypeStruct((SUBLANES, 6 * d), F32),
        compiler_params=_params(("arbitrary",), 32),
        name="ada",
    )(c_pad, w_ada, b_ada)


def _rotary(x, cos_t, msin_t, psin_t):
    width = x.shape[1]
    reps = width // LANES
    c = jnp.concatenate([cos_t] * reps, axis=1)
    m = jnp.concatenate([msin_t] * reps, axis=1)
    p = jnp.concatenate([psin_t] * reps, axis=1)
    half = ROT_DIM // 2
    return x * c + pltpu.roll(x, width - half, 1) * m + pltpu.roll(x, half, 1) * p


def _inproj_body(x_ref, mod_ref, g_ref, wa_ref, wlr_ref, wgk_ref, bgk_ref, wb_ref, wvt_ref, rc_ref, rm_ref, rp_ref,
                 q_ref, k_ref, v_ref, gg_ref, laf_ref, lab_ref, aq_ref, ak_ref, avt_ref):
    shift = mod_ref[0:1, :]
    scale = mod_ref[1:2, :]
    hk = GLA_HEADS * GLA_DK
    hv = GLA_HEADS * GLA_DV
    nq = ATT_Q_HEADS * ATT_HEAD_DIM
    nk = 2 * ATT_KV_HEADS * ATT_HEAD_DIM
    sub = x_ref.shape[0] // IN_SUB
    subs = [slice(s * sub, (s + 1) * sub) for s in range(IN_SUB)]

    def hidden(rows):
        return (_rms(x_ref[rows, :], g_ref[...]) * (1.0 + scale) + shift).astype(BF16)

    def project(h):
        return (jnp.dot(h, wa_ref[...], preferred_element_type=F32),
                jnp.dot(h, wlr_ref[...], preferred_element_type=F32),
                jnp.dot(h, wb_ref[...], preferred_element_type=F32),
                lax.dot_general(wvt_ref[...], h, NT_DIMS, preferred_element_type=F32))

    def finish(rows, pa, plr, pb, pvt):
        q_ref[rows, :] = pa[:, 0:hk] * (GLA_DK ** -0.5)
        k_ref[rows, :] = pa[:, hk:2 * hk]
        v_ref[rows, :] = pa[:, 2 * hk:2 * hk + hv].astype(BF16)
        gg_ref[rows, :] = pa[:, 2 * hk + hv:2 * hk + 2 * hv]
        gk = jnp.dot(plr.astype(BF16), wgk_ref[...], preferred_element_type=F32) + bgk_ref[...]
        la = (jnp.minimum(gk, 0.0) - jnp.log1p(jnp.exp(-jnp.abs(gk)))) * (1.0 / GLA_GATE_NORMALIZER)
        laf_ref[rows, :] = la[:, 0:hk]
        lab_ref[rows, :] = la[:, hk:2 * hk]
        rc, rm, rp = rc_ref[rows, :], rm_ref[rows, :], rp_ref[rows, :]
        aq_ref[rows, :] = (_rotary(pb[:, 0:nq], rc, rm, rp) * (ATT_HEAD_DIM ** -0.5)).astype(BF16)
        ak_ref[rows, :] = _rotary(pb[:, nq:nq + nk], rc, rm, rp).astype(BF16)
        avt_ref[:, rows] = pvt.astype(BF16)

    hs = [hidden(rows) for rows in subs]
    ps = [project(h) for h in hs]
    for rows, p in zip(subs, ps):
        finish(rows, *p)


def _inproj(x2, mod, g_pre, wa, wlr, wgk, bgk, wb, wvt, rc, rm, rp):
    t, d = x2.shape
    tm = TM_IN
    tiles_per_seq = SEQ // tm
    hk = GLA_HEADS * GLA_DK
    hv = GLA_HEADS * GLA_DV
    nq = ATT_Q_HEADS * ATT_HEAD_DIM
    nk = 2 * ATT_KV_HEADS * ATT_HEAD_DIM

    def full(a):
        return pl.BlockSpec(a.shape, lambda i: (0,) * a.ndim)

    def rows(w):
        return pl.BlockSpec((tm, w), lambda i: (i, 0))

    def table():
        return pl.BlockSpec((tm, LANES), lambda i: (i % tiles_per_seq, 0))

    out_widths = [(hk, F32), (hk, F32), (hv, BF16), (hv, F32), (hk, F32), (hk, F32), (nq, BF16), (nk, BF16)]
    return pl.pallas_call(
        _inproj_body,
        grid=(t // tm,),
        in_specs=[
            rows(d),
            pl.BlockSpec((None, 6, d), lambda i: (i // tiles_per_seq, 0, 0)),
            full(g_pre), full(wa), full(wlr), full(wgk), full(bgk), full(wb), full(wvt),
            table(), table(), table(),
        ],
        out_specs=[rows(w) for w, _ in out_widths] + [pl.BlockSpec((nk, tm), lambda i: (0, i))],
        out_shape=[jax.ShapeDtypeStruct((t, w), dt) for w, dt in out_widths] + [jax.ShapeDtypeStruct((nk, t), BF16)],
        compiler_params=_params(("arbitrary",), 56),
        name="inproj",
    )(x2, mod, g_pre, wa, wlr, wgk, bgk, wb, wvt, rc, rm, rp)


def _gla_body(qf_ref, kf_ref, vf_ref, laf_ref, qb_ref, kb_ref, vb_ref, lab_ref, of_ref, ob_ref, sf_ref, sb_ref):
    @pl.when(pl.program_id(1) == 0)
    def _():
        sf_ref[...] = jnp.zeros_like(sf_ref)
        sb_ref[...] = jnp.zeros_like(sb_ref)

    c = GLA_CHUNK
    r_i = lax.broadcasted_iota(I32, (c, c), 0)
    c_i = lax.broadcasted_iota(I32, (c, c), 1)
    lower = c_i <= r_i
    upper = c_i >= r_i
    cum_f = jnp.where(lower, 1.0, 0.0).astype(BF16)
    cum_b = jnp.where(upper, 1.0, 0.0).astype(BF16)
    lane = lax.broadcasted_iota(I32, (1, LANES), 1)
    head_masks = (lane < GLA_DK, lane >= GLA_DK)

    fwd = [(qf_ref, kf_ref, laf_ref, vf_ref, of_ref, slice(g * c, (g + 1) * c), cum_f, lower, c - 1, c // 2 - 1)
           for g in range(GLA_GROUP)]
    bwd = [(qb_ref, kb_ref, lab_ref, vb_ref, ob_ref, slice(g * c, (g + 1) * c), cum_b, upper, 0, c // 2)
           for g in reversed(range(GLA_GROUP))]
    heads = range(GLA_HEADS)
    pair = [slice((h // 2) * LANES, (h // 2 + 1) * LANES) for h in heads]
    vcols = [slice(h * GLA_DV, (h + 1) * GLA_DV) for h in heads]

    def stage1(item):
        q_ref, k_ref, la_ref, v_ref, o_ref, rows, cum, tri, i_last, i_mid = item
        la = la_ref[rows, :]
        hi = la.astype(BF16)
        lo = (la - hi.astype(F32)).astype(BF16)
        b = jnp.dot(cum, hi, preferred_element_type=F32) + jnp.dot(cum, lo, preferred_element_type=F32)
        b_last = b[i_last:i_last + 1, :]
        b_mid = b[i_mid:i_mid + 1, :]
        q, k = q_ref[rows, :], k_ref[rows, :]
        return (q * jnp.exp(b - b_mid), (k * jnp.exp(b_mid - b)).astype(BF16), q * jnp.exp(b),
                (k * jnp.exp(b_last - b)).astype(BF16), jnp.exp(b_last))

    def stage2(item, pre):
        v_ref, rows, tri = item[3], item[5], item[7]
        qs, ks, qi, kst, decay = pre
        out = []
        for h in heads:
            mask = head_masks[h % 2]
            qs_h = jnp.where(mask, qs[:, pair[h]], 0.0).astype(BF16)
            sc = lax.dot_general(qs_h, ks[:, pair[h]], NT_DIMS, preferred_element_type=F32)
            v_h = v_ref[rows, vcols[h]]
            kv = lax.dot_general(v_h, kst[:, pair[h]], TN_DIMS, preferred_element_type=F32)
            out.append((jnp.where(tri, sc, 0.0).astype(BF16), kv,
                        jnp.where(mask, qi[:, pair[h]], 0.0).astype(BF16), v_h))
        return out

    def run(items, s_ref):
        pre = [stage1(it) for it in items]
        mid = [stage2(it, p) for it, p in zip(items, pre)]
        states = [s_ref[h] for h in heads]
        for it, p, m in zip(items, pre, mid):
            o_ref, rows, decay = it[4], it[5], p[4]
            for h in heads:
                sc, kv, qi_h, v_h = m[h]
                o = jnp.dot(sc, v_h, preferred_element_type=F32)
                o = o + lax.dot_general(qi_h, states[h].astype(BF16), NT_DIMS, preferred_element_type=F32)
                o_ref[rows, vcols[h]] = o
                states[h] = states[h] * decay[:, pair[h]] + kv
        for h in heads:
            s_ref[h] = states[h]

    run(fwd, sf_ref)
    run(bwd, sb_ref)


def _gla(q, k, v, laf, lab):
    t = q.shape[0]
    rows = GLA_GROUP * GLA_CHUNK
    ng = SEQ // rows
    hk = GLA_HEADS * GLA_DK
    hv = GLA_HEADS * GLA_DV

    def fwd(w):
        return pl.BlockSpec((rows, w), lambda b, n: (b * ng + n, 0))

    def bwd(w):
        return pl.BlockSpec((rows, w), lambda b, n: (b * ng + ng - 1 - n, 0))

    return pl.pallas_call(
        _gla_body,
        grid=(BATCH, ng),
        in_specs=[fwd(hk), fwd(hk), fwd(hv), fwd(hk), bwd(hk), bwd(hk), bwd(hv), bwd(hk)],
        out_specs=[fwd(hv), bwd(hv)],
        out_shape=[jax.ShapeDtypeStruct((t, hv), F32)] * 2,
        scratch_shapes=[pltpu.VMEM((GLA_HEADS, GLA_DV, 2 * GLA_DK), F32)] * 2,
        compiler_params=_params(("arbitrary", "arbitrary"), 32),
        name="gla",
    )(q, k, v, laf, q, k, v, lab)


def _attn_body(sink_ref, q_ref, kp_ref, kc_ref, kn_ref, vp_ref, vc_ref, vn_ref, o_ref):
    step = pl.program_id(1)
    last = pl.num_programs(1) - 1
    qb = ATT_BLOCK
    hd = ATT_HEAD_DIM
    k_all = jnp.concatenate([kp_ref[...], kc_ref[...], kn_ref[...]], axis=0)
    vt_all = jnp.concatenate([vp_ref[...], vc_ref[...], vn_ref[...]], axis=1)
    lane = lax.broadcasted_iota(I32, (1, LANES), 1)
    lo = lane < hd
    j_k = lax.broadcasted_iota(I32, (3 * qb, qb), 0)
    i_q = lax.broadcasted_iota(I32, (3 * qb, qb), 1)
    band = jnp.abs(j_k - qb - i_q) <= ATT_WINDOW
    sinks = [jnp.concatenate([jnp.full((1, qb), sink_ref[4 * g + r], F32) for r in range(4)], axis=1)
             for g in range(ATT_KV_HEADS)]
    work = [(j, g) for j in range(ATT_GROUP) for g in range(ATT_KV_HEADS)]

    def scores(j, g):
        valid = band
        if j == 0:
            valid = valid & ((j_k >= qb) | (step > 0))
        if j == ATT_GROUP - 1:
            valid = valid & ((j_k < 2 * qb) | (step < last))
        valid4 = jnp.concatenate([valid] * 4, axis=1)
        rows = slice(j * qb, (j + 1) * qb)
        kg = k_all[j * qb:(j + 3) * qb, g * LANES:(g + 1) * LANES]
        qa = q_ref[rows, (2 * g) * LANES:(2 * g + 1) * LANES]
        qc = q_ref[rows, (2 * g + 1) * LANES:(2 * g + 2) * LANES]
        zero = jnp.zeros_like(qa)
        lhs = jnp.concatenate([jnp.where(lo, qa, zero), jnp.where(lo, zero, qa),
                               jnp.where(lo, qc, zero), jnp.where(lo, zero, qc)], axis=0)
        st = lax.dot_general(kg, lhs, NT_DIMS, preferred_element_type=F32)
        return jnp.where(valid4, st, NEG_INF)

    def softmax(st, g):
        sink = sinks[g]
        m = jnp.maximum(jnp.max(st, axis=0, keepdims=True), sink)
        p = jnp.exp(st - m)
        return p.astype(BF16), 1.0 / (jnp.sum(p, axis=0, keepdims=True) + jnp.exp(sink - m))

    def output(j, g, p, inv):
        rows = slice(j * qb, (j + 1) * qb)
        vgt = vt_all[g * LANES:(g + 1) * LANES, j * qb:(j + 3) * qb]
        ot = jnp.dot(vgt, p, preferred_element_type=F32) * inv
        pair_a = jnp.concatenate([ot[0:hd, 0:qb], ot[hd:2 * hd, qb:2 * qb]], axis=0)
        pair_c = jnp.concatenate([ot[0:hd, 2 * qb:3 * qb], ot[hd:2 * hd, 3 * qb:4 * qb]], axis=0)
        o_ref[rows, (2 * g) * LANES:(2 * g + 1) * LANES] = pair_a.T.astype(o_ref.dtype)
        o_ref[rows, (2 * g + 1) * LANES:(2 * g + 2) * LANES] = pair_c.T.astype(o_ref.dtype)

    s_all = [scores(j, g) for j, g in work]
    p_all = [softmax(st, g) for st, (j, g) in zip(s_all, work)]
    for (j, g), (p, inv) in zip(work, p_all):
        output(j, g, p, inv)


def _attn(sink, aq, ak2, avt):
    t = aq.shape[0]
    qb = ATT_BLOCK
    nb = SEQ // qb
    steps = nb // ATT_GROUP
    nq = ATT_Q_HEADS * ATT_HEAD_DIM
    nk = 2 * ATT_KV_HEADS * ATT_HEAD_DIM

    def edge_block(b, n, shift):
        return b * nb + jnp.clip(n * ATT_GROUP + shift, 0, nb - 1)

    def k_edge(shift):
        return pl.BlockSpec((qb, nk), lambda b, n: (edge_block(b, n, shift), 0))

    def v_edge(shift):
        return pl.BlockSpec((nk, qb), lambda b, n: (0, edge_block(b, n, shift)))

    def group(w):
        return pl.BlockSpec((ATT_GROUP * qb, w), lambda b, n: (b * steps + n, 0))

    v_group = pl.BlockSpec((nk, ATT_GROUP * qb), lambda b, n: (0, b * steps + n))
    return pl.pallas_call(
        _attn_body,
        grid=(BATCH, steps),
        in_specs=[
            pl.BlockSpec(memory_space=pltpu.SMEM),
            group(nq),
            k_edge(-1), group(nk), k_edge(ATT_GROUP), v_edge(-1), v_group, v_edge(ATT_GROUP),
        ],
        out_specs=group(nq),
        out_shape=jax.ShapeDtypeStruct((t, nq), BF16),
        compiler_params=_params(("arbitrary", "arbitrary"), 48),
        name="attn",
    )(sink, aq, ak2, ak2, ak2, avt, avt, avt)


def _post_body(of_ref, ob_ref, gg_ref, oa_ref, x_ref, mod_ref, ggla_ref, gpm_ref, gpf_ref, wout_ref,
               wrh_ref, wrl_ref, br_ref,
               x1_ref, h2_ref, ti_ref, gt_ref, rk_ref, cnt_ref, base_ref):
    tm = TM_POST

    @pl.when(pl.program_id(0) == 0)
    def _():
        base_ref[...] = jnp.zeros_like(base_ref)

    gate1 = mod_ref[2:3, :]
    shift2 = mod_ref[3:4, :]
    scale2 = mod_ref[4:5, :]
    sub = tm // POST_SUB
    subs = [slice(s * sub, (s + 1) * sub) for s in range(POST_SUB)]

    def mixer_out(rows):
        og = of_ref[rows, :] + ob_ref[rows, :]
        gg = gg_ref[rows, :]
        parts = []
        for h in range(GLA_HEADS):
            cols = slice(h * GLA_DV, (h + 1) * GLA_DV)
            parts.append((_rms(og[:, cols], ggla_ref[...]) * _silu(gg[:, cols])).astype(BF16))
        return jnp.concatenate(parts + [oa_ref[rows, :]], axis=1)

    def ffn_in(s, y):
        rows = subs[s]
        x1 = x_ref[rows, :] + gate1 * _rms(y, gpm_ref[...])
        x1_ref[rows, :] = x1
        h2 = _rms(x1, gpf_ref[...]) * (1.0 + scale2) + shift2
        hi = h2.astype(BF16)
        hi32 = hi.astype(F32)
        _pack_rows(h2_ref.at[pl.ds(s * sub * PACK_SUB, sub * PACK_SUB), :], hi32)
        return hi, (h2 - hi32).astype(BF16)

    o_subs = [mixer_out(rows) for rows in subs]
    y_subs = [jnp.dot(o, wout_ref[...], preferred_element_type=F32) for o in o_subs]
    split = [ffn_in(s, y) for s, y in enumerate(y_subs)]
    h2_hi = jnp.concatenate([hi for hi, _ in split], axis=0)
    h2_lo = jnp.concatenate([lo for _, lo in split], axis=0)

    wrh = wrh_ref[...]
    logits = (lax.dot_general(wrh, h2_hi, NT_DIMS, preferred_element_type=F32)
              + lax.dot_general(wrh, h2_lo, NT_DIMS, preferred_element_type=F32)
              + lax.dot_general(wrl_ref[...], h2_hi, NT_DIMS, preferred_element_type=F32)
              + br_ref[...])
    e_iota = lax.broadcasted_iota(I32, (N_EXPERTS, tm), 0)
    idxs, vals = [], []
    work = logits
    for _ in range(TOP_K):
        m = jnp.max(work, axis=0, keepdims=True)
        idx = jnp.min(jnp.where(work == m, e_iota, N_EXPERTS), axis=0, keepdims=True)
        idxs.append(idx)
        vals.append(m)
        work = jnp.where(e_iota == idx, -jnp.inf, work)
    exps = [jnp.exp(v - vals[0]) for v in vals]
    inv = 1.0 / (exps[0] + exps[1] + exps[2] + exps[3])
    gt_ref[...] = jnp.concatenate([e * inv for e in exps], axis=0)
    ti_ref[...] = jnp.concatenate(idxs, axis=0)

    onehots = [e_iota == idx for idx in idxs]
    member = jnp.where(onehots[0] | onehots[1] | onehots[2] | onehots[3], 1.0, 0.0)
    t_row = lax.broadcasted_iota(I32, (tm, tm), 0)
    t_col = lax.broadcasted_iota(I32, (tm, tm), 1)
    strict = jnp.where(t_row < t_col, 1.0, 0.0).astype(BF16)
    before = base_ref[...] + jnp.dot(member.astype(BF16), strict, preferred_element_type=F32)
    rk_ref[...] = jnp.concatenate(
        [jnp.sum(jnp.where(oh, before, 0.0), axis=0, keepdims=True) for oh in onehots], axis=0).astype(I32)
    new_base = base_ref[...] + jnp.sum(member, axis=1, keepdims=True)
    base_ref[...] = new_base
    cnt_ref[...] = jnp.broadcast_to(new_base, cnt_ref.shape)


def _post(o_f, o_b, gg, o_att, x2, mod, g_gla, g_pm, g_pf, wout, wrh, wrl, br):
    t, d = x2.shape
    tm = TM_POST
    tiles_per_seq = SEQ // tm
    hv = GLA_HEADS * GLA_DV

    def full(a):
        return pl.BlockSpec(a.shape, lambda i: (0,) * a.ndim)

    def rows(w):
        return pl.BlockSpec((tm, w), lambda i: (i, 0))

    def lanes():
        return pl.BlockSpec((TOP_K, tm), lambda i: (0, i))

    return pl.pallas_call(
        _post_body,
        grid=(t // tm,),
        in_specs=[
            rows(hv), rows(hv), rows(hv), rows(hv), rows(d),
            pl.BlockSpec((None, 6, d), lambda i: (i // tiles_per_seq, 0, 0)),
            full(g_gla), full(g_pm), full(g_pf), full(wout), full(wrh), full(wrl), full(br),
        ],
        out_specs=[
            rows(d),
            pl.BlockSpec((tm * PACK_SUB, LANES), lambda i: (i, 0)),
            lanes(), lanes(), lanes(),
            pl.BlockSpec((N_EXPERTS, LANES), lambda i: (0, 0)),
        ],
        out_shape=[
            jax.ShapeDtypeStruct((t, d), F32),
            jax.ShapeDtypeStruct((t * PACK_SUB, LANES), I32),
            jax.ShapeDtypeStruct((TOP_K, t), I32),
            jax.ShapeDtypeStruct((TOP_K, t), F32),
            jax.ShapeDtypeStruct((TOP_K, t), I32),
            jax.ShapeDtypeStruct((N_EXPERTS, LANES), F32),
        ],
        scratch_shapes=[pltpu.VMEM((N_EXPERTS, 1), F32)],
        compiler_params=_params(("arbitrary",), 48),
        name="post",
    )(o_f, o_b, gg, o_att, x2, mod, g_gla, g_pm, g_pf, wout, wrh, wrl, br)


def _route_body(ti_ref, rk_ref, cnt_ref, pos_ref, blk_ref):
    cnt = cnt_ref[...]
    padded = jnp.floor((cnt + (MOE_BM - 1)) * (1.0 / MOE_BM)) * MOE_BM
    starts, ends = [], []
    acc = jnp.zeros((1, LANES), F32)
    for e in range(N_EXPERTS):
        starts.append(acc)
        acc = acc + padded[e:e + 1, :]
        ends.append(acc)
    ti = ti_ref[...]
    off = jnp.zeros(ti.shape, F32)
    for e in range(N_EXPERTS):
        off = jnp.where(ti == e, starts[e][:, 0:1], off)
    pos_ref[...] = rk_ref[...] + off.astype(I32)

    def owner_of(row):
        n_le = jnp.zeros(row.shape, I32)
        for e in range(N_EXPERTS):
            n_le = n_le + jnp.where(ends[e][:, 0:1] <= row, 1, 0)
        return jnp.minimum(n_le, N_EXPERTS - 1)

    block_start = lax.broadcasted_iota(I32, (1, MOE_NB_PAD), 1).astype(F32) * MOE_BM
    owner = owner_of(block_start)
    nxt = jnp.zeros((1, MOE_NB_PAD), I32)
    nxt_blk = jnp.zeros((1, MOE_NB_PAD), I32)
    for e in range(N_EXPERTS):
        end_e = ends[e][:, 0:1]
        nxt = jnp.where(owner == e, jnp.where(end_e < acc[:, 0:1], owner_of(end_e), -1), nxt)
        nxt_blk = jnp.where(owner == e, (end_e * (1.0 / MOE_BM)).astype(I32), nxt_blk)
    used = jnp.broadcast_to((acc[:, 0:1] * (1.0 / MOE_BM)).astype(I32), (1, MOE_NB_PAD))
    blk_ref[...] = jnp.concatenate([owner, nxt, used, nxt_blk, jnp.zeros((SUBLANES - 4, MOE_NB_PAD), I32)], axis=0)


def _route(top_i, rank, counts):
    return pl.pallas_call(
        _route_body,
        out_shape=[
            jax.ShapeDtypeStruct(top_i.shape, I32),
            jax.ShapeDtypeStruct((SUBLANES, MOE_NB_PAD), I32),
        ],
        compiler_params=pltpu.CompilerParams(vmem_limit_bytes=32 * 1024 * 1024),
        name="route",
    )(top_i, rank, counts)


def _sc_workers():
    info = plsc.get_sparse_core_info()
    return info.num_cores, info.num_subcores, info.num_lanes


def _sc_gather_loop(table_hbm, out_hbm, idx_v, base, chunks, buf0, buf1, sem0, sem1):
    window = SC_GATHER_WINDOW

    def fetch(c, buf, sem):
        return pltpu.make_async_copy(table_hbm.at[idx_v.at[pl.ds(c * window, window)]], buf, sem)

    def flush(c, buf):
        pltpu.sync_copy(buf, out_hbm.at[pl.ds(base + c * window, window)])

    fetch(0, buf0, sem0).start()

    @pl.loop(0, chunks, step=2)
    def _(c):
        fetch(c + 1, buf1, sem1).start()
        fetch(c, buf0, sem0).wait()
        flush(c, buf0)

        @pl.when(c + 2 < chunks)
        def _():
            fetch(c + 2, buf0, sem0).start()

        fetch(c + 1, buf1, sem1).wait()
        flush(c + 1, buf1)


def _sc_source_rows(pos_flat, n_rows):
    cores, subcores, lanes = _sc_workers()
    workers = cores * subcores
    per_worker = n_rows // workers
    n_assign = pos_flat.shape[0]
    scan = SC_SCAN_CHUNK
    assert per_worker * workers == n_rows and per_worker % lanes == 0
    assert n_assign % scan == 0 and scan % lanes == 0
    mesh = plsc.VectorSubcoreMesh(core_axis_name="core", subcore_axis_name="subcore")

    @functools.partial(
        pl.kernel,
        out_type=jax.ShapeDtypeStruct((n_rows,), I32),
        mesh=mesh,
        scratch_types=[pltpu.VMEM((per_worker,), I32), pltpu.VMEM((scan,), I32)],
        compiler_params=pltpu.CompilerParams(needs_layout_passes=False),
        name="sc_source_rows",
    )
    def invert(pos_hbm, out_hbm, src_v, pos_v):
        wid = lax.axis_index("subcore") * cores + lax.axis_index("core")
        base = wid * per_worker
        lane = lax.iota(I32, lanes)

        @pl.loop(0, per_worker, step=lanes)
        def _(j):
            src_v[pl.ds(j, lanes)] = (base + j + lane) & (TOKENS - 1)

        @pl.loop(0, n_assign, step=scan)
        def _(a0):
            pltpu.sync_copy(pos_hbm.at[pl.ds(a0, scan)], pos_v)

            @plsc.parallel_loop(0, scan, step=lanes, unroll=SC_SCAN_UNROLL)
            def _(j):
                rel = pos_v[pl.ds(j, lanes)] - base
                mine = (rel >= 0) & (rel < per_worker)
                tok = (a0 + j + lane) & (TOKENS - 1)
                plsc.store_scatter(src_v, [jnp.where(mine, rel, 0)], tok, mask=mine)

        pltpu.sync_copy(src_v, out_hbm.at[pl.ds(base, per_worker)])

    return invert(pos_flat)


def _sc_gather_rows(table, idx):
    cores, subcores, _ = _sc_workers()
    workers = cores * subcores
    n = idx.shape[0]
    window = SC_GATHER_WINDOW
    per_worker = n // workers
    chunks = per_worker // window
    assert per_worker * workers == n and chunks * window == per_worker and chunks % 2 == 0
    row_shape = table.shape[1:]
    mesh = plsc.VectorSubcoreMesh(core_axis_name="core", subcore_axis_name="subcore")

    @functools.partial(
        pl.kernel,
        out_type=jax.ShapeDtypeStruct((n,) + row_shape, table.dtype),
        mesh=mesh,
        scratch_types=[
            pltpu.VMEM((per_worker,), I32),
            pltpu.VMEM((window,) + row_shape, table.dtype),
            pltpu.VMEM((window,) + row_shape, table.dtype),
            pltpu.SemaphoreType.DMA,
            pltpu.SemaphoreType.DMA,
        ],
        name="sc_gather_rows",
    )
    def gather(table_hbm, idx_hbm, out_hbm, idx_v, buf0, buf1, sem0, sem1):
        wid = lax.axis_index("subcore") * cores + lax.axis_index("core")
        base = wid * per_worker
        pltpu.sync_copy(idx_hbm.at[pl.ds(base, per_worker)], idx_v)
        _sc_gather_loop(table_hbm, out_hbm, idx_v, base, chunks, buf0, buf1, sem0, sem1)

    return gather(table, idx)


def _experts_body(first, be_ref, nxt_ref, nxtblk_ref, nu_ref, xs_ref, bgu_ref, bd_ref, wgu_hbm, wd_hbm, *rest):
    ys_ref, wgu_f32, wd_f32, wgu_bf, wd_bf, sems = rest[-6:]
    step = pl.program_id(0)
    end = first + pl.num_programs(0) * MOE_SUB
    bm = MOE_BM
    n_used = nu_ref[0]

    def fetch(e):
        return (pltpu.make_async_copy(wgu_hbm.at[e], wgu_f32, sems.at[0]),
                pltpu.make_async_copy(wd_hbm.at[e], wd_f32, sems.at[1]))

    @pl.when((step == 0) & (first < n_used))
    def _():
        for cp in fetch(be_ref[first]):
            cp.start()

    for sub in range(MOE_SUB):
        i = first + step * MOE_SUB + sub
        e = be_ref[i]
        used = i < n_used
        fresh = (i == first) | (e != be_ref[jnp.maximum(i - 1, 0)])
        rows = pl.ds(sub * bm * PACK_SUB, bm * PACK_SUB)
        xs_sub = xs_ref.at[rows, :]
        ys_sub = ys_ref.at[rows, :]

        @pl.when(used & fresh)
        def _():
            for cp in fetch(e):
                cp.wait()
            wgu_bf[...] = wgu_f32[...].astype(BF16)
            wd_bf[...] = wd_f32[...].astype(BF16)

            @pl.when((nxt_ref[i] >= 0) & (nxtblk_ref[i] < end))
            def _():
                for cp in fetch(nxt_ref[i]):
                    cp.start()

        @pl.when(used)
        def _():
            x = jnp.concatenate(_unpack_rows(xs_sub, bm), axis=1).astype(BF16)
            gu = jnp.dot(x, wgu_bf[...], preferred_element_type=F32) + bgu_ref[pl.ds(e, 1), :]
            gate = jnp.minimum(gu[:, 0:D_FF], SWIGLU_LIMIT)
            up = jnp.clip(gu[:, D_FF:2 * D_FF], -SWIGLU_LIMIT, SWIGLU_LIMIT)
            act = ((up + 1.0) * gate * jax.nn.sigmoid(SWIGLU_ALPHA * gate)).astype(BF16)
            y = jnp.dot(act, wd_bf[...], preferred_element_type=F32) + bd_ref[pl.ds(e, 1), :]
            _pack_rows(ys_sub, y.astype(BF16).astype(F32))

        @pl.when(jnp.logical_not(used))
        def _():
            ys_sub[...] = jnp.zeros((bm * PACK_SUB, LANES), I32)


def _experts(part, blocks, xs_part, w_gate_up, b_gate_up, w_down, b_down, ys_prev):
    rows = MOE_SUB * MOE_BM * PACK_SUB
    d = D_MODEL
    part_blocks = MOE_NB // MOE_PARTS
    steps = part_blocks // MOE_SUB
    first = part * part_blocks

    def x_block(i, be, nx, nb, nu):
        last = jnp.maximum((jnp.minimum(nu[0], first + part_blocks) - 1 - first) // MOE_SUB, 0)
        return jnp.minimum(i, last)

    in_specs = [
        pl.BlockSpec((rows, LANES), lambda i, be, nx, nb, nu: (x_block(i, be, nx, nb, nu), 0)),
        pl.BlockSpec(b_gate_up.shape, lambda i, be, nx, nb, nu: (0, 0)),
        pl.BlockSpec(b_down.shape, lambda i, be, nx, nb, nu: (0, 0)),
        pl.BlockSpec(memory_space=pl.ANY),
        pl.BlockSpec(memory_space=pl.ANY),
    ]
    operands = [blocks[0, :MOE_NB], blocks[1, :MOE_NB], blocks[3, :MOE_NB], blocks[2, :1],
                xs_part, b_gate_up, b_down, w_gate_up, w_down]
    aliases = {}
    if ys_prev is not None:
        in_specs.append(pl.BlockSpec(memory_space=pl.ANY))
        aliases = {len(operands): 0}
        operands.append(ys_prev)
    grid_spec = pltpu.PrefetchScalarGridSpec(
        num_scalar_prefetch=4,
        grid=(steps,),
        in_specs=in_specs,
        out_specs=pl.BlockSpec((rows, LANES), lambda i, be, nx, nb, nu: (first // MOE_SUB + i, 0)),
        scratch_shapes=[
            pltpu.VMEM((d, 2 * D_FF), F32), pltpu.VMEM((D_FF, d), F32),
            pltpu.VMEM((d, 2 * D_FF), BF16), pltpu.VMEM((D_FF, d), BF16),
            pltpu.SemaphoreType.DMA((2,)),
        ],
    )
    return pl.pallas_call(
        functools.partial(_experts_body, first),
        grid_spec=grid_spec,
        out_shape=jax.ShapeDtypeStruct((MOE_ROWS * PACK_SUB, LANES), I32),
        input_output_aliases=aliases,
        compiler_params=_params(("arbitrary",), 48),
        name="experts",
    )(*operands)


def _combine_body(gates_ref, x1_ref, mod_ref, gpost_ref, y0_ref, y1_ref, y2_ref, y3_ref, *rest):
    o_ref = rest[-1]
    tm = TM_COMB
    gates = gates_ref[...]
    y_hi = jnp.zeros((tm, PACK_COLS), F32)
    y_lo = jnp.zeros((tm, PACK_COLS), F32)
    for k, yk_ref in enumerate((y0_ref, y1_ref, y2_ref, y3_ref)):
        hi, lo = _unpack_rows(yk_ref, tm)
        y_hi = y_hi + hi * gates[:, k:k + 1]
        y_lo = y_lo + lo * gates[:, k:k + 1]
    y = jnp.concatenate([y_hi, y_lo], axis=1)
    gate2 = mod_ref[5:6, :]
    o_ref[...] = x1_ref[...] + gate2 * _rms(y, gpost_ref[...])


def _combine(part, gates_t, x1, mod, g_post, y4_part, out_prev):
    t, d = x1.shape
    tm = TM_COMB
    tiles = t // COMB_PARTS // tm
    tile0 = part * tiles
    tiles_per_seq = SEQ // tm

    def slab(k):
        return pl.BlockSpec((tm * PACK_SUB, LANES), lambda i: (k * tiles + i, 0))

    in_specs = [
        pl.BlockSpec((tm, TOP_K), lambda i: (tile0 + i, 0)),
        pl.BlockSpec((tm, d), lambda i: (tile0 + i, 0)),
        pl.BlockSpec((None, 6, d), lambda i: ((tile0 + i) // tiles_per_seq, 0, 0)),
        pl.BlockSpec(g_post.shape, lambda i: (0, 0)),
        slab(0), slab(1), slab(2), slab(3),
    ]
    operands = [gates_t, x1, mod, g_post, y4_part, y4_part, y4_part, y4_part]
    aliases = {}
    if out_prev is not None:
        in_specs.append(pl.BlockSpec(memory_space=pl.ANY))
        aliases = {len(operands): 0}
        operands.append(out_prev)
    return pl.pallas_call(
        _combine_body,
        grid=(tiles,),
        in_specs=in_specs,
        out_specs=pl.BlockSpec((tm, d), lambda i: (tile0 + i, 0)),
        out_shape=jax.ShapeDtypeStruct((t, d), F32),
        input_output_aliases=aliases,
        compiler_params=_params(("arbitrary",), 48),
        name="combine",
    )(*operands)


def _rotary_tables():
    half = ROT_DIM // 2
    inv_freq = ROPE_THETA ** (-2.0 * np.arange(half, dtype=np.float32) / ROT_DIM)
    ang = np.arange(SEQ, dtype=np.float32)[:, None] * inv_freq[None, :].astype(np.float32)
    cos, sin = np.cos(ang), np.sin(ang)
    ones = np.ones((SEQ, ATT_HEAD_DIM - ROT_DIM), np.float32)
    zeros = np.zeros((SEQ, ATT_HEAD_DIM - ROT_DIM), np.float32)
    zh = np.zeros((SEQ, half), np.float32)
    reps = LANES // ATT_HEAD_DIM
    rc = np.tile(np.concatenate([cos, cos, ones], axis=1), (1, reps))
    rm = np.tile(np.concatenate([-sin, zh, zeros], axis=1), (1, reps))
    rp = np.tile(np.concatenate([zh, sin, zeros], axis=1), (1, reps))
    return tuple(jnp.asarray(t, F32) for t in (rc, rm, rp))


def _mixer_inputs(w_in, w_gk_fwd, b_gk_fwd, w_gk_bwd, b_gk_bwd):
    hk = GLA_HEADS * GLA_DK
    hv = GLA_HEADS * GLA_DV
    w = w_in[0]
    o_lr = 2 * hk + 2 * hv
    o_aq = o_lr + 2 * GLA_RANK
    o_ak = o_aq + ATT_Q_HEADS * ATT_HEAD_DIM
    o_av = o_ak + ATT_KV_HEADS * ATT_HEAD_DIM
    hd = ATT_HEAD_DIM
    wa = w[:, :o_lr].astype(BF16)
    wlr = w[:, o_lr:o_aq].astype(BF16)
    dup = lambda m: jnp.concatenate([m[:, g * hd:(g + 1) * hd] for g in range(ATT_KV_HEADS) for _ in range(2)], axis=1)
    wb = jnp.concatenate([w[:, o_aq:o_ak], dup(w[:, o_ak:o_av])], axis=1).astype(BF16)
    wvt = dup(w[:, o_av:o_av + ATT_KV_HEADS * hd]).T.astype(BF16)
    zr = jnp.zeros((GLA_RANK, hk), F32)
    wgk = jnp.concatenate([jnp.concatenate([w_gk_fwd[0], zr], axis=1),
                           jnp.concatenate([zr, w_gk_bwd[0]], axis=1)], axis=0).astype(BF16)
    bgk = jnp.concatenate([b_gk_fwd[0], b_gk_bwd[0]])[None, :]
    return (wa, wlr, wgk, bgk, wb, wvt) + _rotary_tables()


def kernel(x, c, w_ada, b_ada, g_pre_mix, g_post_mix, w_in, w_gk_fwd, b_gk_fwd, w_gk_bwd, b_gk_bwd, g_gla_out,
           attn_sink, w_out, g_pre_ffn, g_post_ffn, w_router, b_router, w_gate_up, b_gate_up, w_down, b_down):
    assert x.shape == (BATCH, SEQ, D_MODEL) and w_ada.shape[0] == 1
    d = D_MODEL
    x2 = x.reshape(TOKENS, d)

    c_pad = jnp.pad(c, ((0, SUBLANES - BATCH), (0, 0)))
    mod = _ada(c_pad, w_ada[0], b_ada)[:BATCH].reshape(BATCH, 6, d)

    mixer_in = _mixer_inputs(w_in, w_gk_fwd, b_gk_fwd, w_gk_bwd, b_gk_bwd)
    q, k, v, gg, laf, lab, aq, ak2, avt = _inproj(x2, mod, g_pre_mix, *mixer_in)
    o_f, o_b = _gla(q, k, v, laf, lab)
    o_att = _attn(attn_sink[0], aq, ak2, avt)

    wr_t = w_router[0].T
    wrh = wr_t.astype(BF16)
    wrl = (wr_t - wrh.astype(F32)).astype(BF16)
    x1, h2_tiles, top_i, gates, rank, counts = _post(
        o_f, o_b, gg, o_att, x2, mod, g_gla_out, g_post_mix, g_pre_ffn, w_out[0].astype(BF16), wrh, wrl,
        b_router[0][:, None])

    pos, blocks = _route(top_i, rank, counts)

    src = _sc_source_rows(pos.reshape(TOP_K * TOKENS), MOE_ROWS)
    h2_rows = h2_tiles.reshape(TOKENS, PACK_SUB, LANES)
    part_rows = MOE_ROWS // MOE_PARTS
    ys = None
    for p in range(MOE_PARTS):
        xs_p = _sc_gather_rows(h2_rows, src[p * part_rows:(p + 1) * part_rows])
        ys = _experts(p, blocks, xs_p.reshape(part_rows * PACK_SUB, LANES),
                      w_gate_up[0], b_gate_up[0], w_down[0], b_down[0], ys)

    ys_rows = ys.reshape(MOE_ROWS, PACK_SUB, LANES)
    gates_t = gates.T
    part_tokens = TOKENS // COMB_PARTS
    out = None
    for p in range(COMB_PARTS):
        idx = pos[:, p * part_tokens:(p + 1) * part_tokens].reshape(TOP_K * part_tokens)
        y4_p = _sc_gather_rows(ys_rows, idx).reshape(TOP_K * part_tokens * PACK_SUB, LANES)
        out = _combine(p, gates_t, x1, mod, g_post_ffn, y4_p, out)
    return out.reshape(BATCH, SEQ, d)
```

```python
import functools

import jax
import jax.numpy as jnp
import numpy as np
from jax import lax
from jax.experimental import pallas as pl
from jax.experimental.pallas import tpu as pltpu
from jax.experimental.pallas import tpu_sc as plsc

F32 = jnp.float32
BF16 = jnp.bfloat16
I32 = jnp.int32

D_MODEL = 1024
BATCH = 2
SEQ = 8192
TOKENS = BATCH * SEQ
GLA_HEADS = 4
GLA_DV = 128
GLA_DK = 64
GLA_RANK = 16
GLA_GATE_NORMALIZER = 16.0
GLA_CHUNK = 64
ATT_Q_HEADS = 8
ATT_KV_HEADS = 2
ATT_HEAD_DIM = 64
ATT_WINDOW = 128
ATT_BLOCK = 128
ROT_DIM = 16
ROPE_THETA = 500000.0
N_EXPERTS = 32
TOP_K = 4
D_FF = 1024
SWIGLU_LIMIT = 7.0
SWIGLU_ALPHA = 1.702
NORM_EPS = 1e-6
NEG_INF = -1e30

LANES = 128
SUBLANES = 8
PACK_COLS = D_MODEL // 2
PACK_SUB = PACK_COLS // LANES

TM_IN = 1024
IN_SUB = 4
GLA_GROUP = 16
ATT_GROUP = 16
TM_POST = 1024
POST_SUB = 4
MOE_BM = 256
MOE_ROWS = TOKENS * TOP_K + N_EXPERTS * MOE_BM
MOE_NB = MOE_ROWS // MOE_BM
MOE_SUB = 2
MOE_PARTS = 3
COMB_PARTS = 1
MOE_NB_PAD = ((MOE_NB + LANES - 1) // LANES) * LANES
SC_SCAN_CHUNK = 4096
SC_SCAN_UNROLL = 8
TM_COMB = 256
SC_GATHER_WINDOW = 32
SC_GATHER_RING = 4

NT_DIMS = (((1,), (1,)), ((), ()))
TN_DIMS = (((0,), (0,)), ((), ()))


def _params(semantics, vmem_mib):
    return pltpu.CompilerParams(dimension_semantics=semantics, vmem_limit_bytes=vmem_mib * 1024 * 1024)


def _rms(x, g):
    return x * lax.rsqrt(jnp.mean(x * x, axis=-1, keepdims=True) + NORM_EPS) * g


def _silu(x):
    return x * jax.nn.sigmoid(x)


def _pack_rows(ref, v):
    m = v.shape[0]
    bits = lax.bitcast_convert_type(v, jnp.uint32)
    word = lax.bitcast_convert_type(bits[:, :PACK_COLS] | (bits[:, PACK_COLS:] >> 16), I32)
    for s in range(PACK_SUB):
        ref[pl.ds(s, m, stride=PACK_SUB), :] = word[:, s * LANES:(s + 1) * LANES]


def _unpack_rows(ref, m):
    word = jnp.concatenate([ref[pl.ds(s, m, stride=PACK_SUB), :] for s in range(PACK_SUB)], axis=1)
    bits = lax.bitcast_convert_type(word, jnp.uint32)
    hi = lax.bitcast_convert_type(bits & jnp.uint32(0xFFFF0000), F32)
    lo = lax.bitcast_convert_type(bits << 16, F32)
    return hi, lo


def _ada_body(c_ref, w_ref, b_ref, o_ref):
    ca = _silu(c_ref[...]).astype(BF16)
    o_ref[...] = jnp.dot(ca, w_ref[...].astype(BF16), preferred_element_type=F32) + b_ref[...]


def _ada(c_pad, w_ada, b_ada):
    d = D_MODEL
    return pl.pallas_call(
        _ada_body,
        grid=(6,),
        in_specs=[
            pl.BlockSpec((SUBLANES, d), lambda j: (0, 0)),
            pl.BlockSpec((d, d), lambda j: (0, j)),
            pl.BlockSpec((1, d), lambda j: (0, j)),
        ],
        out_specs=pl.BlockSpec((SUBLANES, d), lambda j: (0, j)),
        out_shape=jax.ShapeDtypeStruct((SUBLANES, 6 * d), F32),
        compiler_params=_params(("arbitrary",), 32),
        name="ada",
    )(c_pad, w_ada, b_ada)


def _rotary(x, cos_t, msin_t, psin_t):
    width = x.shape[1]
    reps = width // LANES
    c = jnp.concatenate([cos_t] * reps, axis=1)
    m = jnp.concatenate([msin_t] * reps, axis=1)
    p = jnp.concatenate([psin_t] * reps, axis=1)
    half = ROT_DIM // 2
    return x * c + pltpu.roll(x, width - half, 1) * m + pltpu.roll(x, half, 1) * p


def _inproj_body(x_ref, mod_ref, g_ref, wa_ref, wlr_ref, wgk_ref, bgk_ref, wb_ref, wvt_ref, rc_ref, rm_ref, rp_ref,
                 q_ref, k_ref, v_ref, gg_ref, laf_ref, lab_ref, aq_ref, ak_ref, avt_ref):
    shift = mod_ref[0:1, :]
    scale = mod_ref[1:2, :]
    hk = GLA_HEADS * GLA_DK
    hv = GLA_HEADS * GLA_DV
    nq = ATT_Q_HEADS * ATT_HEAD_DIM
    nk = 2 * ATT_KV_HEADS * ATT_HEAD_DIM
    sub = x_ref.shape[0] // IN_SUB
    subs = [slice(s * sub, (s + 1) * sub) for s in range(IN_SUB)]

    def hidden(rows):
        return (_rms(x_ref[rows, :], g_ref[...]) * (1.0 + scale) + shift).astype(BF16)

    def project(h):
        return (jnp.dot(h, wa_ref[...], preferred_element_type=F32),
                jnp.dot(h, wlr_ref[...], preferred_element_type=F32),
                jnp.dot(h, wb_ref[...], preferred_element_type=F32),
                lax.dot_general(wvt_ref[...], h, NT_DIMS, preferred_element_type=F32))

    def finish(rows, pa, plr, pb, pvt):
        q_ref[rows, :] = pa[:, 0:hk] * (GLA_DK ** -0.5)
        k_ref[rows, :] = pa[:, hk:2 * hk]
        v_ref[rows, :] = pa[:, 2 * hk:2 * hk + hv].astype(BF16)
        gg_ref[rows, :] = pa[:, 2 * hk + hv:2 * hk + 2 * hv]
        gk = jnp.dot(plr.astype(BF16), wgk_ref[...], preferred_element_type=F32) + bgk_ref[...]
        la = (jnp.minimum(gk, 0.0) - jnp.log1p(jnp.exp(-jnp.abs(gk)))) * (1.0 / GLA_GATE_NORMALIZER)
        laf_ref[rows, :] = la[:, 0:hk]
        lab_ref[rows, :] = la[:, hk:2 * hk]
        rc, rm, rp = rc_ref[rows, :], rm_ref[rows, :], rp_ref[rows, :]
        aq_ref[rows, :] = (_rotary(pb[:, 0:nq], rc, rm, rp) * (ATT_HEAD_DIM ** -0.5)).astype(BF16)
        ak_ref[rows, :] = _rotary(pb[:, nq:nq + nk], rc, rm, rp).astype(BF16)
        avt_ref[:, rows] = pvt.astype(BF16)

    hs = [hidden(rows) for rows in subs]
    ps = [project(h) for h in hs]
    for rows, p in zip(subs, ps):
        finish(rows, *p)


def _inproj(x2, mod, g_pre, wa, wlr, wgk, bgk, wb, wvt, rc, rm, rp):
    t, d = x2.shape
    tm = TM_IN
    tiles_per_seq = SEQ // tm
    hk = GLA_HEADS * GLA_DK
    hv = GLA_HEADS * GLA_DV
    nq = ATT_Q_HEADS * ATT_HEAD_DIM
    nk = 2 * ATT_KV_HEADS * ATT_HEAD_DIM

    def full(a):
        return pl.BlockSpec(a.shape, lambda i: (0,) * a.ndim)

    def rows(w):
        return pl.BlockSpec((tm, w), lambda i: (i, 0))

    def table():
        return pl.BlockSpec((tm, LANES), lambda i: (i % tiles_per_seq, 0))

    out_widths = [(hk, F32), (hk, F32), (hv, BF16), (hv, F32), (hk, F32), (hk, F32), (nq, BF16), (nk, BF16)]
    return pl.pallas_call(
        _inproj_body,
        grid=(t // tm,),
        in_specs=[
            rows(d),
            pl.BlockSpec((None, 6, d), lambda i: (i // tiles_per_seq, 0, 0)),
            full(g_pre), full(wa), full(wlr), full(wgk), full(bgk), full(wb), full(wvt),
            table(), table(), table(),
        ],
        out_specs=[rows(w) for w, _ in out_widths] + [pl.BlockSpec((nk, tm), lambda i: (0, i))],
        out_shape=[jax.ShapeDtypeStruct((t, w), dt) for w, dt in out_widths] + [jax.ShapeDtypeStruct((nk, t), BF16)],
        compiler_params=_params(("arbitrary",), 56),
        name="inproj",
    )(x2, mod, g_pre, wa, wlr, wgk, bgk, wb, wvt, rc, rm, rp)


def _gla_body(qf_ref, kf_ref, vf_ref, laf_ref, qb_ref, kb_ref, vb_ref, lab_ref, of_ref, ob_ref, sf_ref, sb_ref):
    @pl.when(pl.program_id(1) == 0)
    def _():
        sf_ref[...] = jnp.zeros_like(sf_ref)
        sb_ref[...] = jnp.zeros_like(sb_ref)

    c = GLA_CHUNK
    r_i = lax.broadcasted_iota(I32, (c, c), 0)
    c_i = lax.broadcasted_iota(I32, (c, c), 1)
    lower = c_i <= r_i
    upper = c_i >= r_i
    cum_f = jnp.where(lower, 1.0, 0.0).astype(BF16)
    cum_b = jnp.where(upper, 1.0, 0.0).astype(BF16)
    lane = lax.broadcasted_iota(I32, (1, LANES), 1)
    head_masks = (lane < GLA_DK, lane >= GLA_DK)

    fwd = [(qf_ref, kf_ref, laf_ref, vf_ref, of_ref, slice(g * c, (g + 1) * c), cum_f, lower, c - 1, c // 2 - 1)
           for g in range(GLA_GROUP)]
    bwd = [(qb_ref, kb_ref, lab_ref, vb_ref, ob_ref, slice(g * c, (g + 1) * c), cum_b, upper, 0, c // 2)
           for g in reversed(range(GLA_GROUP))]
    heads = range(GLA_HEADS)
    pair = [slice((h // 2) * LANES, (h // 2 + 1) * LANES) for h in heads]
    vcols = [slice(h * GLA_DV, (h + 1) * GLA_DV) for h in heads]

    def stage1(item):
        q_ref, k_ref, la_ref, v_ref, o_ref, rows, cum, tri, i_last, i_mid = item
        la = la_ref[rows, :]
        hi = la.astype(BF16)
        lo = (la - hi.astype(F32)).astype(BF16)
        b = jnp.dot(cum, hi, preferred_element_type=F32) + jnp.dot(cum, lo, preferred_element_type=F32)
        b_last = b[i_last:i_last + 1, :]
        b_mid = b[i_mid:i_mid + 1, :]
        q, k = q_ref[rows, :], k_ref[rows, :]
        return (q * jnp.exp(b - b_mid), (k * jnp.exp(b_mid - b)).astype(BF16), q * jnp.exp(b),
                (k * jnp.exp(b_last - b)).astype(BF16), jnp.exp(b_last))

    def stage2(item, pre):
        v_ref, rows, tri = item[3], item[5], item[7]
        qs, ks, qi, kst, decay = pre
        out = []
        for h in heads:
            mask = head_masks[h % 2]
            qs_h = jnp.where(mask, qs[:, pair[h]], 0.0).astype(BF16)
            sc = lax.dot_general(qs_h, ks[:, pair[h]], NT_DIMS, preferred_element_type=F32)
            v_h = v_ref[rows, vcols[h]]
            kv = lax.dot_general(v_h, kst[:, pair[h]], TN_DIMS, preferred_element_type=F32)
            out.append((jnp.where(tri, sc, 0.0).astype(BF16), kv,
                        jnp.where(mask, qi[:, pair[h]], 0.0).astype(BF16), v_h))
        return out

    def run(items, s_ref):
        pre = [stage1(it) for it in items]
        mid = [stage2(it, p) for it, p in zip(items, pre)]
        states = [s_ref[h] for h in heads]
        for it, p, m in zip(items, pre, mid):
            o_ref, rows, decay = it[4], it[5], p[4]
            for h in heads:
                sc, kv, qi_h, v_h = m[h]
                o = jnp.dot(sc, v_h, preferred_element_type=F32)
                o = o + lax.dot_general(qi_h, states[h].astype(BF16), NT_DIMS, preferred_element_type=F32)
                o_ref[rows, vcols[h]] = o
                states[h] = states[h] * decay[:, pair[h]] + kv
        for h in heads:
            s_ref[h] = states[h]

    run(fwd, sf_ref)
    run(bwd, sb_ref)


def _gla(q, k, v, laf, lab):
    t = q.shape[0]
    rows = GLA_GROUP * GLA_CHUNK
    ng = SEQ // rows
    hk = GLA_HEADS * GLA_DK
    hv = GLA_HEADS * GLA_DV

    def fwd(w):
        return pl.BlockSpec((rows, w), lambda b, n: (b * ng + n, 0))

    def bwd(w):
        return pl.BlockSpec((rows, w), lambda b, n: (b * ng + ng - 1 - n, 0))

    return pl.pallas_call(
        _gla_body,
        grid=(BATCH, ng),
        in_specs=[fwd(hk), fwd(hk), fwd(hv), fwd(hk), bwd(hk), bwd(hk), bwd(hv), bwd(hk)],
        out_specs=[fwd(hv), bwd(hv)],
        out_shape=[jax.ShapeDtypeStruct((t, hv), F32)] * 2,
        scratch_shapes=[pltpu.VMEM((GLA_HEADS, GLA_DV, 2 * GLA_DK), F32)] * 2,
        compiler_params=_params(("arbitrary", "arbitrary"), 32),
        name="gla",
    )(q, k, v, laf, q, k, v, lab)


def _attn_body(sink_ref, q_ref, kp_ref, kc_ref, kn_ref, vp_ref, vc_ref, vn_ref, o_ref):
    step = pl.program_id(1)
    last = pl.num_programs(1) - 1
    qb = ATT_BLOCK
    hd = ATT_HEAD_DIM
    k_all = jnp.concatenate([kp_ref[...], kc_ref[...], kn_ref[...]], axis=0)
    vt_all = jnp.concatenate([vp_ref[...], vc_ref[...], vn_ref[...]], axis=1)
    lane = lax.broadcasted_iota(I32, (1, LANES), 1)
    lo = lane < hd
    j_k = lax.broadcasted_iota(I32, (3 * qb, qb), 0)
    i_q = lax.broadcasted_iota(I32, (3 * qb, qb), 1)
    band = jnp.abs(j_k - qb - i_q) <= ATT_WINDOW
    sinks = [jnp.concatenate([jnp.full((1, qb), sink_ref[4 * g + r], F32) for r in range(4)], axis=1)
             for g in range(ATT_KV_HEADS)]
    work = [(j, g) for j in range(ATT_GROUP) for g in range(ATT_KV_HEADS)]

    def scores(j, g):
        valid = band
        if j == 0:
            valid = valid & ((j_k >= qb) | (step > 0))
        if j == ATT_GROUP - 1:
            valid = valid & ((j_k < 2 * qb) | (step < last))
        valid4 = jnp.concatenate([valid] * 4, axis=1)
        rows = slice(j * qb, (j + 1) * qb)
        kg = k_all[j * qb:(j + 3) * qb, g * LANES:(g + 1) * LANES]
        qa = q_ref[rows, (2 * g) * LANES:(2 * g + 1) * LANES]
        qc = q_ref[rows, (2 * g + 1) * LANES:(2 * g + 2) * LANES]
        zero = jnp.zeros_like(qa)
        lhs = jnp.concatenate([jnp.where(lo, qa, zero), jnp.where(lo, zero, qa),
                               jnp.where(lo, qc, zero), jnp.where(lo, zero, qc)], axis=0)
        st = lax.dot_general(kg, lhs, NT_DIMS, preferred_element_type=F32)
        return jnp.where(valid4, st, NEG_INF)

    def softmax(st, g):
        sink = sinks[g]
        m = jnp.maximum(jnp.max(st, axis=0, keepdims=True), sink)
        p = jnp.exp(st - m)
        return p.astype(BF16), 1.0 / (jnp.sum(p, axis=0, keepdims=True) + jnp.exp(sink - m))

    def output(j, g, p, inv):
        rows = slice(j * qb, (j + 1) * qb)
        vgt = vt_all[g * LANES:(g + 1) * LANES, j * qb:(j + 3) * qb]
        ot = jnp.dot(vgt, p, preferred_element_type=F32) * inv
        pair_a = jnp.concatenate([ot[0:hd, 0:qb], ot[hd:2 * hd, qb:2 * qb]], axis=0)
        pair_c = jnp.concatenate([ot[0:hd, 2 * qb:3 * qb], ot[hd:2 * hd, 3 * qb:4 * qb]], axis=0)
        o_ref[rows, (2 * g) * LANES:(2 * g + 1) * LANES] = pair_a.T.astype(o_ref.dtype)
        o_ref[rows, (2 * g + 1) * LANES:(2 * g + 2) * LANES] = pair_c.T.astype(o_ref.dtype)

    s_all = [scores(j, g) for j, g in work]
    p_all = [softmax(st, g) for st, (j, g) in zip(s_all, work)]
    for (j, g), (p, inv) in zip(work, p_all):
        output(j, g, p, inv)


def _attn(sink, aq, ak2, avt):
    t = aq.shape[0]
    qb = ATT_BLOCK
    nb = SEQ // qb
    steps = nb // ATT_GROUP
    nq = ATT_Q_HEADS * ATT_HEAD_DIM
    nk = 2 * ATT_KV_HEADS * ATT_HEAD_DIM

    def edge_block(b, n, shift):
        return b * nb + jnp.clip(n * ATT_GROUP + shift, 0, nb - 1)

    def k_edge(shift):
        return pl.BlockSpec((qb, nk), lambda b, n: (edge_block(b, n, shift), 0))

    def v_edge(shift):
        return pl.BlockSpec((nk, qb), lambda b, n: (0, edge_block(b, n, shift)))

    def group(w):
        return pl.BlockSpec((ATT_GROUP * qb, w), lambda b, n: (b * steps + n, 0))

    v_group = pl.BlockSpec((nk, ATT_GROUP * qb), lambda b, n: (0, b * steps + n))
    return pl.pallas_call(
        _attn_body,
        grid=(BATCH, steps),
        in_specs=[
            pl.BlockSpec(memory_space=pltpu.SMEM),
            group(nq),
            k_edge(-1), group(nk), k_edge(ATT_GROUP), v_edge(-1), v_group, v_edge(ATT_GROUP),
        ],
        out_specs=group(nq),
        out_shape=jax.ShapeDtypeStruct((t, nq), BF16),
        compiler_params=_params(("arbitrary", "arbitrary"), 48),
        name="attn",
    )(sink, aq, ak2, ak2, ak2, avt, avt, avt)


def _post_body(of_ref, ob_ref, gg_ref, oa_ref, x_ref, mod_ref, ggla_ref, gpm_ref, gpf_ref, wout_ref,
               wrh_ref, wrl_ref, br_ref,
               x1_ref, h2_ref, ti_ref, gt_ref, rk_ref, cnt_ref, base_ref):
    tm = TM_POST

    @pl.when(pl.program_id(0) == 0)
    def _():
        base_ref[...] = jnp.zeros_like(base_ref)

    gate1 = mod_ref[2:3, :]
    shift2 = mod_ref[3:4, :]
    scale2 = mod_ref[4:5, :]
    sub = tm // POST_SUB
    subs = [slice(s * sub, (s + 1) * sub) for s in range(POST_SUB)]

    def mixer_out(rows):
        og = of_ref[rows, :] + ob_ref[rows, :]
        gg = gg_ref[rows, :]
        parts = []
        for h in range(GLA_HEADS):
            cols = slice(h * GLA_DV, (h + 1) * GLA_DV)
            parts.append((_rms(og[:, cols], ggla_ref[...]) * _silu(gg[:, cols])).astype(BF16))
        return jnp.concatenate(parts + [oa_ref[rows, :]], axis=1)

    def ffn_in(s, y):
        rows = subs[s]
        x1 = x_ref[rows, :] + gate1 * _rms(y, gpm_ref[...])
        x1_ref[rows, :] = x1
        h2 = _rms(x1, gpf_ref[...]) * (1.0 + scale2) + shift2
        hi = h2.astype(BF16)
        hi32 = hi.astype(F32)
        _pack_rows(h2_ref.at[pl.ds(s * sub * PACK_SUB, sub * PACK_SUB), :], hi32)
        return hi, (h2 - hi32).astype(BF16)

    o_subs = [mixer_out(rows) for rows in subs]
    y_subs = [jnp.dot(o, wout_ref[...], preferred_element_type=F32) for o in o_subs]
    split = [ffn_in(s, y) for s, y in enumerate(y_subs)]
    h2_hi = jnp.concatenate([hi for hi, _ in split], axis=0)
    h2_lo = jnp.concatenate([lo for _, lo in split], axis=0)

    wrh = wrh_ref[...]
    logits = (lax.dot_general(wrh, h2_hi, NT_DIMS, preferred_element_type=F32)
              + lax.dot_general(wrh, h2_lo, NT_DIMS, preferred_element_type=F32)
              + lax.dot_general(wrl_ref[...], h2_hi, NT_DIMS, preferred_element_type=F32)
              + br_ref[...])
    e_iota = lax.broadcasted_iota(I32, (N_EXPERTS, tm), 0)
    idxs, vals = [], []
    work = logits
    for _ in range(TOP_K):
        m = jnp.max(work, axis=0, keepdims=True)
        idx = jnp.min(jnp.where(work == m, e_iota, N_EXPERTS), axis=0, keepdims=True)
        idxs.append(idx)
        vals.append(m)
        work = jnp.where(e_iota == idx, -jnp.inf, work)
    exps = [jnp.exp(v - vals[0]) for v in vals]
    inv = 1.0 / (exps[0] + exps[1] + exps[2] + exps[3])
    gt_ref[...] = jnp.concatenate([e * inv for e in exps], axis=0)
    ti_ref[...] = jnp.concatenate(idxs, axis=0)

    onehots = [e_iota == idx for idx in idxs]
    member = jnp.where(onehots[0] | onehots[1] | onehots[2] | onehots[3], 1.0, 0.0)
    t_row = lax.broadcasted_iota(I32, (tm, tm), 0)
    t_col = lax.broadcasted_iota(I32, (tm, tm), 1)
    strict = jnp.where(t_row < t_col, 1.0, 0.0).astype(BF16)
    before = base_ref[...] + jnp.dot(member.astype(BF16), strict, preferred_element_type=F32)
    rk_ref[...] = jnp.concatenate(
        [jnp.sum(jnp.where(oh, before, 0.0), axis=0, keepdims=True) for oh in onehots], axis=0).astype(I32)
    new_base = base_ref[...] + jnp.sum(member, axis=1, keepdims=True)
    base_ref[...] = new_base
    cnt_ref[...] = jnp.broadcast_to(new_base, cnt_ref.shape)


def _post(o_f, o_b, gg, o_att, x2, mod, g_gla, g_pm, g_pf, wout, wrh, wrl, br):
    t, d = x2.shape
    tm = TM_POST
    tiles_per_seq = SEQ // tm
    hv = GLA_HEADS * GLA_DV

    def full(a):
        return pl.BlockSpec(a.shape, lambda i: (0,) * a.ndim)

    def rows(w):
        return pl.BlockSpec((tm, w), lambda i: (i, 0))

    def lanes():
        return pl.BlockSpec((TOP_K, tm), lambda i: (0, i))

    return pl.pallas_call(
        _post_body,
        grid=(t // tm,),
        in_specs=[
            rows(hv), rows(hv), rows(hv), rows(hv), rows(d),
            pl.BlockSpec((None, 6, d), lambda i: (i // tiles_per_seq, 0, 0)),
            full(g_gla), full(g_pm), full(g_pf), full(wout), full(wrh), full(wrl), full(br),
        ],
        out_specs=[
            rows(d),
            pl.BlockSpec((tm * PACK_SUB, LANES), lambda i: (i, 0)),
            lanes(), lanes(), lanes(),
            pl.BlockSpec((N_EXPERTS, LANES), lambda i: (0, 0)),
        ],
        out_shape=[
            jax.ShapeDtypeStruct((t, d), F32),
            jax.ShapeDtypeStruct((t * PACK_SUB, LANES), I32),
            jax.ShapeDtypeStruct((TOP_K, t), I32),
            jax.ShapeDtypeStruct((TOP_K, t), F32),
            jax.ShapeDtypeStruct((TOP_K, t), I32),
            jax.ShapeDtypeStruct((N_EXPERTS, LANES), F32),
        ],
        scratch_shapes=[pltpu.VMEM((N_EXPERTS, 1), F32)],
        compiler_params=_params(("arbitrary",), 48),
        name="post",
    )(o_f, o_b, gg, o_att, x2, mod, g_gla, g_pm, g_pf, wout, wrh, wrl, br)


def _route_body(ti_ref, rk_ref, cnt_ref, pos_ref, blk_ref):
    cnt = cnt_ref[...]
    padded = jnp.floor((cnt + (MOE_BM - 1)) * (1.0 / MOE_BM)) * MOE_BM
    starts, ends = [], []
    acc = jnp.zeros((1, LANES), F32)
    for e in range(N_EXPERTS):
        starts.append(acc)
        acc = acc + padded[e:e + 1, :]
        ends.append(acc)
    ti = ti_ref[...]
    off = jnp.zeros(ti.shape, F32)
    for e in range(N_EXPERTS):
        off = jnp.where(ti == e, starts[e][:, 0:1], off)
    pos_ref[...] = rk_ref[...] + off.astype(I32)

    def owner_of(row):
        n_le = jnp.zeros(row.shape, I32)
        for e in range(N_EXPERTS):
            n_le = n_le + jnp.where(ends[e][:, 0:1] <= row, 1, 0)
        return jnp.minimum(n_le, N_EXPERTS - 1)

    block_start = lax.broadcasted_iota(I32, (1, MOE_NB_PAD), 1).astype(F32) * MOE_BM
    owner = owner_of(block_start)
    nxt = jnp.zeros((1, MOE_NB_PAD), I32)
    nxt_blk = jnp.zeros((1, MOE_NB_PAD), I32)
    for e in range(N_EXPERTS):
        end_e = ends[e][:, 0:1]
        nxt = jnp.where(owner == e, jnp.where(end_e < acc[:, 0:1], owner_of(end_e), -1), nxt)
        nxt_blk = jnp.where(owner == e, (end_e * (1.0 / MOE_BM)).astype(I32), nxt_blk)
    used = jnp.broadcast_to((acc[:, 0:1] * (1.0 / MOE_BM)).astype(I32), (1, MOE_NB_PAD))
    blk_ref[...] = jnp.concatenate([owner, nxt, used, nxt_blk, jnp.zeros((SUBLANES - 4, MOE_NB_PAD), I32)], axis=0)


def _route(top_i, rank, counts):
    return pl.pallas_call(
        _route_body,
        out_shape=[
            jax.ShapeDtypeStruct(top_i.shape, I32),
            jax.ShapeDtypeStruct((SUBLANES, MOE_NB_PAD), I32),
        ],
        compiler_params=pltpu.CompilerParams(vmem_limit_bytes=32 * 1024 * 1024),
        name="route",
    )(top_i, rank, counts)


def _sc_workers():
    info = plsc.get_sparse_core_info()
    return info.num_cores, info.num_subcores, info.num_lanes


def _sc_gather_loop(table_hbm, out_hbm, idx_v, base, chunks, bufs, gather_sems, write_sems):
    window = SC_GATHER_WINDOW
    ring = len(bufs)

    def fetch(c, b):
        return pltpu.make_async_copy(table_hbm.at[idx_v.at[pl.ds(c * window, window)]], bufs[b], gather_sems[b])

    def flush(c, b):
        return pltpu.make_async_copy(bufs[b], out_hbm.at[pl.ds(base + c * window, window)], write_sems[b])

    for b in range(ring):
        fetch(b, b).start()

    @pl.loop(0, chunks, step=ring)
    def _(c0):
        for b in range(ring):
            c = c0 + b
            fetch(c, b).wait()
            flush(c, b).start()

            @pl.when(c + ring < chunks)
            def _():
                flush(c, b).wait()
                fetch(c + ring, b).start()

    for b in range(ring):
        flush(chunks - ring + b, b).wait()


def _sc_source_rows(pos_flat, n_rows):
    cores, subcores, lanes = _sc_workers()
    workers = cores * subcores
    per_worker = n_rows // workers
    n_assign = pos_flat.shape[0]
    scan = SC_SCAN_CHUNK
    assert per_worker * workers == n_rows and per_worker % lanes == 0
    assert n_assign % scan == 0 and scan % lanes == 0
    mesh = plsc.VectorSubcoreMesh(core_axis_name="core", subcore_axis_name="subcore")

    @functools.partial(
        pl.kernel,
        out_type=jax.ShapeDtypeStruct((n_rows,), I32),
        mesh=mesh,
        scratch_types=[pltpu.VMEM((per_worker,), I32), pltpu.VMEM((scan,), I32)],
        compiler_params=pltpu.CompilerParams(needs_layout_passes=False),
        name="sc_source_rows",
    )
    def invert(pos_hbm, out_hbm, src_v, pos_v):
        wid = lax.axis_index("subcore") * cores + lax.axis_index("core")
        base = wid * per_worker
        lane = lax.iota(I32, lanes)

        @pl.loop(0, per_worker, step=lanes)
        def _(j):
            src_v[pl.ds(j, lanes)] = (base + j + lane) & (TOKENS - 1)

        @pl.loop(0, n_assign, step=scan)
        def _(a0):
            pltpu.sync_copy(pos_hbm.at[pl.ds(a0, scan)], pos_v)

            @plsc.parallel_loop(0, scan, step=lanes, unroll=SC_SCAN_UNROLL)
            def _(j):
                rel = pos_v[pl.ds(j, lanes)] - base
                mine = (rel >= 0) & (rel < per_worker)
                tok = (a0 + j + lane) & (TOKENS - 1)
                plsc.store_scatter(src_v, [jnp.where(mine, rel, 0)], tok, mask=mine)

        pltpu.sync_copy(src_v, out_hbm.at[pl.ds(base, per_worker)])

    return invert(pos_flat)


def _sc_gather_rows(table, idx):
    cores, subcores, _ = _sc_workers()
    workers = cores * subcores
    n = idx.shape[0]
    window = SC_GATHER_WINDOW
    per_worker = n // workers
    chunks = per_worker // window
    ring = SC_GATHER_RING
    assert per_worker * workers == n and chunks * window == per_worker and chunks % ring == 0
    row_shape = table.shape[1:]
    mesh = plsc.VectorSubcoreMesh(core_axis_name="core", subcore_axis_name="subcore")

    @functools.partial(
        pl.kernel,
        out_type=jax.ShapeDtypeStruct((n,) + row_shape, table.dtype),
        mesh=mesh,
        scratch_types=[pltpu.VMEM((per_worker,), I32)]
        + [pltpu.VMEM((window,) + row_shape, table.dtype)] * ring
        + [pltpu.SemaphoreType.DMA] * (2 * ring),
        name="sc_gather_rows",
    )
    def gather(table_hbm, idx_hbm, out_hbm, idx_v, *scratch):
        wid = lax.axis_index("subcore") * cores + lax.axis_index("core")
        base = wid * per_worker
        pltpu.sync_copy(idx_hbm.at[pl.ds(base, per_worker)], idx_v)
        _sc_gather_loop(table_hbm, out_hbm, idx_v, base, chunks,
                        scratch[:ring], scratch[ring:2 * ring], scratch[2 * ring:])

    return gather(table, idx)


def _experts_body(first, be_ref, nxt_ref, nxtblk_ref, nu_ref, xs_ref, bgu_ref, bd_ref, wgu_hbm, wd_hbm, *rest):
    ys_ref, wgu_f32, wd_f32, wgu_bf, wd_bf, sems = rest[-6:]
    step = pl.program_id(0)
    end = first + pl.num_programs(0) * MOE_SUB
    bm = MOE_BM
    n_used = nu_ref[0]

    def fetch(e):
        return (pltpu.make_async_copy(wgu_hbm.at[e], wgu_f32, sems.at[0]),
                pltpu.make_async_copy(wd_hbm.at[e], wd_f32, sems.at[1]))

    @pl.when((step == 0) & (first < n_used))
    def _():
        for cp in fetch(be_ref[first]):
            cp.start()

    for sub in range(MOE_SUB):
        i = first + step * MOE_SUB + sub
        e = be_ref[i]
        used = i < n_used
        fresh = (i == first) | (e != be_ref[jnp.maximum(i - 1, 0)])
        rows = pl.ds(sub * bm * PACK_SUB, bm * PACK_SUB)
        xs_sub = xs_ref.at[rows, :]
        ys_sub = ys_ref.at[rows, :]

        @pl.when(used & fresh)
        def _():
            for cp in fetch(e):
                cp.wait()
            wgu_bf[...] = wgu_f32[...].astype(BF16)
            wd_bf[...] = wd_f32[...].astype(BF16)

            @pl.when((nxt_ref[i] >= 0) & (nxtblk_ref[i] < end))
            def _():
                for cp in fetch(nxt_ref[i]):
                    cp.start()

        @pl.when(used)
        def _():
            x = jnp.concatenate(_unpack_rows(xs_sub, bm), axis=1).astype(BF16)
            gu = jnp.dot(x, wgu_bf[...], preferred_element_type=F32) + bgu_ref[pl.ds(e, 1), :]
            gate = jnp.minimum(gu[:, 0:D_FF], SWIGLU_LIMIT)
            up = jnp.clip(gu[:, D_FF:2 * D_FF], -SWIGLU_LIMIT, SWIGLU_LIMIT)
            act = ((up + 1.0) * gate * jax.nn.sigmoid(SWIGLU_ALPHA * gate)).astype(BF16)
            y = jnp.dot(act, wd_bf[...], preferred_element_type=F32) + bd_ref[pl.ds(e, 1), :]
            _pack_rows(ys_sub, y.astype(BF16).astype(F32))

        @pl.when(jnp.logical_not(used))
        def _():
            ys_sub[...] = jnp.zeros((bm * PACK_SUB, LANES), I32)


def _experts(part, blocks, xs_part, w_gate_up, b_gate_up, w_down, b_down, ys_prev):
    rows = MOE_SUB * MOE_BM * PACK_SUB
    d = D_MODEL
    part_blocks = MOE_NB // MOE_PARTS
    steps = part_blocks // MOE_SUB
    first = part * part_blocks

    def x_block(i, be, nx, nb, nu):
        last = jnp.maximum((jnp.minimum(nu[0], first + part_blocks) - 1 - first) // MOE_SUB, 0)
        return jnp.minimum(i, last)

    in_specs = [
        pl.BlockSpec((rows, LANES), lambda i, be, nx, nb, nu: (x_block(i, be, nx, nb, nu), 0)),
        pl.BlockSpec(b_gate_up.shape, lambda i, be, nx, nb, nu: (0, 0)),
        pl.BlockSpec(b_down.shape, lambda i, be, nx, nb, nu: (0, 0)),
        pl.BlockSpec(memory_space=pl.ANY),
        pl.BlockSpec(memory_space=pl.ANY),
    ]
    operands = [blocks[0, :MOE_NB], blocks[1, :MOE_NB], blocks[3, :MOE_NB], blocks[2, :1],
                xs_part, b_gate_up, b_down, w_gate_up, w_down]
    aliases = {}
    if ys_prev is not None:
        in_specs.append(pl.BlockSpec(memory_space=pl.ANY))
        aliases = {len(operands): 0}
        operands.append(ys_prev)
    grid_spec = pltpu.PrefetchScalarGridSpec(
        num_scalar_prefetch=4,
        grid=(steps,),
        in_specs=in_specs,
        out_specs=pl.BlockSpec((rows, LANES), lambda i, be, nx, nb, nu: (first // MOE_SUB + i, 0)),
        scratch_shapes=[
            pltpu.VMEM((d, 2 * D_FF), F32), pltpu.VMEM((D_FF, d), F32),
            pltpu.VMEM((d, 2 * D_FF), BF16), pltpu.VMEM((D_FF, d), BF16),
            pltpu.SemaphoreType.DMA((2,)),
        ],
    )
    return pl.pallas_call(
        functools.partial(_experts_body, first),
        grid_spec=grid_spec,
        out_shape=jax.ShapeDtypeStruct((MOE_ROWS * PACK_SUB, LANES), I32),
        input_output_aliases=aliases,
        compiler_params=_params(("arbitrary",), 48),
        name="experts",
    )(*operands)


def _combine_body(gates_ref, x1_ref, mod_ref, gpost_ref, y0_ref, y1_ref, y2_ref, y3_ref, *rest):
    o_ref = rest[-1]
    tm = TM_COMB
    gates = gates_ref[...]
    y_hi = jnp.zeros((tm, PACK_COLS), F32)
    y_lo = jnp.zeros((tm, PACK_COLS), F32)
    for k, yk_ref in enumerate((y0_ref, y1_ref, y2_ref, y3_ref)):
        hi, lo = _unpack_rows(yk_ref, tm)
        y_hi = y_hi + hi * gates[:, k:k + 1]
        y_lo = y_lo + lo * gates[:, k:k + 1]
    y = jnp.concatenate([y_hi, y_lo], axis=1)
    gate2 = mod_ref[5:6, :]
    o_ref[...] = x1_ref[...] + gate2 * _rms(y, gpost_ref[...])


def _combine(part, gates_t, x1, mod, g_post, y4_part, out_prev):
    t, d = x1.shape
    tm = TM_COMB
    tiles = t // COMB_PARTS // tm
    tile0 = part * tiles
    tiles_per_seq = SEQ // tm

    def slab(k):
        return pl.BlockSpec((tm * PACK_SUB, LANES), lambda i: (k * tiles + i, 0))

    in_specs = [
        pl.BlockSpec((tm, TOP_K), lambda i: (tile0 + i, 0)),
        pl.BlockSpec((tm, d), lambda i: (tile0 + i, 0)),
        pl.BlockSpec((None, 6, d), lambda i: ((tile0 + i) // tiles_per_seq, 0, 0)),
        pl.BlockSpec(g_post.shape, lambda i: (0, 0)),
        slab(0), slab(1), slab(2), slab(3),
    ]
    operands = [gates_t, x1, mod, g_post, y4_part, y4_part, y4_part, y4_part]
    aliases = {}
    if out_prev is not None:
        in_specs.append(pl.BlockSpec(memory_space=pl.ANY))
        aliases = {len(operands): 0}
        operands.append(out_prev)
    return pl.pallas_call(
        _combine_body,
        grid=(tiles,),
        in_specs=in_specs,
        out_specs=pl.BlockSpec((tm, d), lambda i: (tile0 + i, 0)),
        out_shape=jax.ShapeDtypeStruct((t, d), F32),
        input_output_aliases=aliases,
        compiler_params=_params(("arbitrary",), 48),
        name="combine",
    )(*operands)


def _rotary_tables():
    half = ROT_DIM // 2
    inv_freq = ROPE_THETA ** (-2.0 * np.arange(half, dtype=np.float32) / ROT_DIM)
    ang = np.arange(SEQ, dtype=np.float32)[:, None] * inv_freq[None, :].astype(np.float32)
    cos, sin = np.cos(ang), np.sin(ang)
    ones = np.ones((SEQ, ATT_HEAD_DIM - ROT_DIM), np.float32)
    zeros = np.zeros((SEQ, ATT_HEAD_DIM - ROT_DIM), np.float32)
    zh = np.zeros((SEQ, half), np.float32)
    reps = LANES // ATT_HEAD_DIM
    rc = np.tile(np.concatenate([cos, cos, ones], axis=1), (1, reps))
    rm = np.tile(np.concatenate([-sin, zh, zeros], axis=1), (1, reps))
    rp = np.tile(np.concatenate([zh, sin, zeros], axis=1), (1, reps))
    return tuple(jnp.asarray(t, F32) for t in (rc, rm, rp))


def _mixer_inputs(w_in, w_gk_fwd, b_gk_fwd, w_gk_bwd, b_gk_bwd):
    hk = GLA_HEADS * GLA_DK
    hv = GLA_HEADS * GLA_DV
    w = w_in[0]
    o_lr = 2 * hk + 2 * hv
    o_aq = o_lr + 2 * GLA_RANK
    o_ak = o_aq + ATT_Q_HEADS * ATT_HEAD_DIM
    o_av = o_ak + ATT_KV_HEADS * ATT_HEAD_DIM
    hd = ATT_HEAD_DIM
    wa = w[:, :o_lr].astype(BF16)
    wlr = w[:, o_lr:o_aq].astype(BF16)
    dup = lambda m: jnp.concatenate([m[:, g * hd:(g + 1) * hd] for g in range(ATT_KV_HEADS) for _ in range(2)], axis=1)
    wb = jnp.concatenate([w[:, o_aq:o_ak], dup(w[:, o_ak:o_av])], axis=1).astype(BF16)
    wvt = dup(w[:, o_av:o_av + ATT_KV_HEADS * hd]).T.astype(BF16)
    zr = jnp.zeros((GLA_RANK, hk), F32)
    wgk = jnp.concatenate([jnp.concatenate([w_gk_fwd[0], zr], axis=1),
                           jnp.concatenate([zr, w_gk_bwd[0]], axis=1)], axis=0).astype(BF16)
    bgk = jnp.concatenate([b_gk_fwd[0], b_gk_bwd[0]])[None, :]
    return (wa, wlr, wgk, bgk, wb, wvt) + _rotary_tables()


def kernel(x, c, w_ada, b_ada, g_pre_mix, g_post_mix, w_in, w_gk_fwd, b_gk_fwd, w_gk_bwd, b_gk_bwd, g_gla_out,
           attn_sink, w_out, g_pre_ffn, g_post_ffn, w_router, b_router, w_gate_up, b_gate_up, w_down, b_down):
    assert x.shape == (BATCH, SEQ, D_MODEL) and w_ada.shape[0] == 1
    d = D_MODEL
    x2 = x.reshape(TOKENS, d)

    c_pad = jnp.pad(c, ((0, SUBLANES - BATCH), (0, 0)))
    mod = _ada(c_pad, w_ada[0], b_ada)[:BATCH].reshape(BATCH, 6, d)

    mixer_in = _mixer_inputs(w_in, w_gk_fwd, b_gk_fwd, w_gk_bwd, b_gk_bwd)
    q, k, v, gg, laf, lab, aq, ak2, avt = _inproj(x2, mod, g_pre_mix, *mixer_in)
    o_f, o_b = _gla(q, k, v, laf, lab)
    o_att = _attn(attn_sink[0], aq, ak2, avt)

    wr_t = w_router[0].T
    wrh = wr_t.astype(BF16)
    wrl = (wr_t - wrh.astype(F32)).astype(BF16)
    x1, h2_tiles, top_i, gates, rank, counts = _post(
        o_f, o_b, gg, o_att, x2, mod, g_gla_out, g_post_mix, g_pre_ffn, w_out[0].astype(BF16), wrh, wrl,
        b_router[0][:, None])

    pos, blocks = _route(top_i, rank, counts)

    src = _sc_source_rows(pos.reshape(TOP_K * TOKENS), MOE_ROWS)
    h2_rows = h2_tiles.reshape(TOKENS, PACK_SUB, LANES)
    part_rows = MOE_ROWS // MOE_PARTS
    ys = None
    for p in range(MOE_PARTS):
        xs_p = _sc_gather_rows(h2_rows, src[p * part_rows:(p + 1) * part_rows])
        ys = _experts(p, blocks, xs_p.reshape(part_rows * PACK_SUB, LANES),
                      w_gate_up[0], b_gate_up[0], w_down[0], b_down[0], ys)

    ys_rows = ys.reshape(MOE_ROWS, PACK_SUB, LANES)
    gates_t = gates.T
    part_tokens = TOKENS // COMB_PARTS
    out = None
    for p in range(COMB_PARTS):
        idx = pos[:, p * part_tokens:(p + 1) * part_tokens].reshape(TOP_K * part_tokens)
        y4_p = _sc_gather_rows(ys_rows, idx).reshape(TOP_K * part_tokens * PACK_SUB, LANES)
        out = _combine(p, gates_t, x1, mod, g_post_ffn, y4_p, out)
    return out.reshape(BATCH, SEQ, d)
```

```python
import functools

import jax
import jax.numpy as jnp
import numpy as np
from jax import lax
from jax.experimental import pallas as pl
from jax.experimental.pallas import tpu as pltpu
from jax.experimental.pallas import tpu_sc as plsc

F32 = jnp.float32
BF16 = jnp.bfloat16
I32 = jnp.int32

D_MODEL = 1024
BATCH = 2
SEQ = 8192
TOKENS = BATCH * SEQ
GLA_HEADS = 4
GLA_DV = 128
GLA_DK = 64
GLA_RANK = 16
GLA_GATE_NORMALIZER = 16.0
GLA_CHUNK = 64
ATT_Q_HEADS = 8
ATT_KV_HEADS = 2
ATT_HEAD_DIM = 64
ATT_WINDOW = 128
ATT_BLOCK = 128
ROT_DIM = 16
ROPE_THETA = 500000.0
N_EXPERTS = 32
TOP_K = 4
D_FF = 1024
SWIGLU_LIMIT = 7.0
SWIGLU_ALPHA = 1.702
NORM_EPS = 1e-6
NEG_INF = -1e30

LANES = 128
SUBLANES = 8
PACK_COLS = D_MODEL // 2
PACK_SUB = PACK_COLS // LANES

TM_IN = 1024
IN_SUB = 4
GLA_GROUP = 16
ATT_GROUP = 16
TM_POST = 1024
POST_SUB = 4
MOE_BM = 256
MOE_ROWS = TOKENS * TOP_K + N_EXPERTS * MOE_BM
MOE_NB = MOE_ROWS // MOE_BM
MOE_SUB = 2
MOE_PART_BLOCKS = (32, 128, 128)
COMB_PARTS = 1
MOE_NB_PAD = ((MOE_NB + LANES - 1) // LANES) * LANES
SC_SCAN_CHUNK = 4096
SC_SCAN_UNROLL = 8
TM_COMB = 512
SC_GATHER_WINDOW = 32
SC_GATHER_RING = 4

NT_DIMS = (((1,), (1,)), ((), ()))
TN_DIMS = (((0,), (0,)), ((), ()))


def _params(semantics, vmem_mib):
    return pltpu.CompilerParams(dimension_semantics=semantics, vmem_limit_bytes=vmem_mib * 1024 * 1024)


def _rms(x, g):
    return x * lax.rsqrt(jnp.mean(x * x, axis=-1, keepdims=True) + NORM_EPS) * g


def _silu(x):
    return x * jax.nn.sigmoid(x)


def _pack_rows(ref, v):
    m = v.shape[0]
    bits = lax.bitcast_convert_type(v, jnp.uint32)
    word = lax.bitcast_convert_type(bits[:, :PACK_COLS] | (bits[:, PACK_COLS:] >> 16), I32)
    for s in range(PACK_SUB):
        ref[pl.ds(s, m, stride=PACK_SUB), :] = word[:, s * LANES:(s + 1) * LANES]


def _unpack_rows(ref, m):
    word = jnp.concatenate([ref[pl.ds(s, m, stride=PACK_SUB), :] for s in range(PACK_SUB)], axis=1)
    bits = lax.bitcast_convert_type(word, jnp.uint32)
    hi = lax.bitcast_convert_type(bits & jnp.uint32(0xFFFF0000), F32)
    lo = lax.bitcast_convert_type(bits << 16, F32)
    return hi, lo


def _ada_body(c_ref, w_ref, b_ref, o_ref):
    ca = _silu(c_ref[...]).astype(BF16)
    o_ref[...] = jnp.dot(ca, w_ref[...].astype(BF16), preferred_element_type=F32) + b_ref[...]


def _ada(c_pad, w_ada, b_ada):
    d = D_MODEL
    return pl.pallas_call(
        _ada_body,
        grid=(6,),
        in_specs=[
            pl.BlockSpec((SUBLANES, d), lambda j: (0, 0)),
            pl.BlockSpec((d, d), lambda j: (0, j)),
            pl.BlockSpec((1, d), lambda j: (0, j)),
        ],
        out_specs=pl.BlockSpec((SUBLANES, d), lambda j: (0, j)),
        out_shape=jax.ShapeDtypeStruct((SUBLANES, 6 * d), F32),
        compiler_params=_params(("arbitrary",), 32),
        name="ada",
    )(c_pad, w_ada, b_ada)


def _rotary(x, cos_t, msin_t, psin_t):
    width = x.shape[1]
    reps = width // LANES
    c = jnp.concatenate([cos_t] * reps, axis=1)
    m = jnp.concatenate([msin_t] * reps, axis=1)
    p = jnp.concatenate([psin_t] * reps, axis=1)
    half = ROT_DIM // 2
    return x * c + pltpu.roll(x, width - half, 1) * m + pltpu.roll(x, half, 1) * p


def _inproj_body(x_ref, mod_ref, g_ref, wa_ref, wlr_ref, wgk_ref, bgk_ref, wb_ref, wvt_ref, rc_ref, rm_ref, rp_ref,
                 q_ref, k_ref, v_ref, gg_ref, laf_ref, lab_ref, aq_ref, ak_ref, avt_ref):
    shift = mod_ref[0:1, :]
    scale = mod_ref[1:2, :]
    hk = GLA_HEADS * GLA_DK
    hv = GLA_HEADS * GLA_DV
    nq = ATT_Q_HEADS * ATT_HEAD_DIM
    nk = 2 * ATT_KV_HEADS * ATT_HEAD_DIM
    sub = x_ref.shape[0] // IN_SUB
    subs = [slice(s * sub, (s + 1) * sub) for s in range(IN_SUB)]

    def hidden(rows):
        return (_rms(x_ref[rows, :], g_ref[...]) * (1.0 + scale) + shift).astype(BF16)

    def project(h):
        return (jnp.dot(h, wa_ref[...], preferred_element_type=F32),
                jnp.dot(h, wlr_ref[...], preferred_element_type=F32),
                jnp.dot(h, wb_ref[...], preferred_element_type=F32),
                lax.dot_general(wvt_ref[...], h, NT_DIMS, preferred_element_type=F32))

    def finish(rows, pa, plr, pb, pvt):
        q_ref[rows, :] = pa[:, 0:hk] * (GLA_DK ** -0.5)
        k_ref[rows, :] = pa[:, hk:2 * hk]
        v_ref[rows, :] = pa[:, 2 * hk:2 * hk + hv].astype(BF16)
        gg_ref[rows, :] = pa[:, 2 * hk + hv:2 * hk + 2 * hv]
        gk = jnp.dot(plr.astype(BF16), wgk_ref[...], preferred_element_type=F32) + bgk_ref[...]
        la = (jnp.minimum(gk, 0.0) - jnp.log1p(jnp.exp(-jnp.abs(gk)))) * (1.0 / GLA_GATE_NORMALIZER)
        laf_ref[rows, :] = la[:, 0:hk]
        lab_ref[rows, :] = la[:, hk:2 * hk]
        rc, rm, rp = rc_ref[rows, :], rm_ref[rows, :], rp_ref[rows, :]
        aq_ref[rows, :] = (_rotary(pb[:, 0:nq], rc, rm, rp) * (ATT_HEAD_DIM ** -0.5)).astype(BF16)
        ak_ref[rows, :] = _rotary(pb[:, nq:nq + nk], rc, rm, rp).astype(BF16)
        avt_ref[:, rows] = pvt.astype(BF16)

    hs = [hidden(rows) for rows in subs]
    ps = [project(h) for h in hs]
    for rows, p in zip(subs, ps):
        finish(rows, *p)


def _inproj(x2, mod, g_pre, wa, wlr, wgk, bgk, wb, wvt, rc, rm, rp):
    t, d = x2.shape
    tm = TM_IN
    tiles_per_seq = SEQ // tm
    hk = GLA_HEADS * GLA_DK
    hv = GLA_HEADS * GLA_DV
    nq = ATT_Q_HEADS * ATT_HEAD_DIM
    nk = 2 * ATT_KV_HEADS * ATT_HEAD_DIM

    def full(a):
        return pl.BlockSpec(a.shape, lambda i: (0,) * a.ndim)

    def rows(w):
        return pl.BlockSpec((tm, w), lambda i: (i, 0))

    def table():
        return pl.BlockSpec((tm, LANES), lambda i: (i % tiles_per_seq, 0))

    out_widths = [(hk, F32), (hk, F32), (hv, BF16), (hv, F32), (hk, F32), (hk, F32), (nq, BF16), (nk, BF16)]
    return pl.pallas_call(
        _inproj_body,
        grid=(t // tm,),
        in_specs=[
            rows(d),
            pl.BlockSpec((None, 6, d), lambda i: (i // tiles_per_seq, 0, 0)),
            full(g_pre), full(wa), full(wlr), full(wgk), full(bgk), full(wb), full(wvt),
            table(), table(), table(),
        ],
        out_specs=[rows(w) for w, _ in out_widths] + [pl.BlockSpec((nk, tm), lambda i: (0, i))],
        out_shape=[jax.ShapeDtypeStruct((t, w), dt) for w, dt in out_widths] + [jax.ShapeDtypeStruct((nk, t), BF16)],
        compiler_params=_params(("arbitrary",), 56),
        name="inproj",
    )(x2, mod, g_pre, wa, wlr, wgk, bgk, wb, wvt, rc, rm, rp)


def _gla_body(qf_ref, kf_ref, vf_ref, laf_ref, qb_ref, kb_ref, vb_ref, lab_ref, of_ref, ob_ref, sf_ref, sb_ref):
    @pl.when(pl.program_id(1) == 0)
    def _():
        sf_ref[...] = jnp.zeros_like(sf_ref)
        sb_ref[...] = jnp.zeros_like(sb_ref)

    c = GLA_CHUNK
    r_i = lax.broadcasted_iota(I32, (c, c), 0)
    c_i = lax.broadcasted_iota(I32, (c, c), 1)
    lower = c_i <= r_i
    upper = c_i >= r_i
    cum_f = jnp.where(lower, 1.0, 0.0).astype(BF16)
    cum_b = jnp.where(upper, 1.0, 0.0).astype(BF16)
    lane = lax.broadcasted_iota(I32, (1, LANES), 1)
    head_masks = (lane < GLA_DK, lane >= GLA_DK)

    fwd = [(qf_ref, kf_ref, laf_ref, vf_ref, of_ref, slice(g * c, (g + 1) * c), cum_f, lower, c - 1, c // 2 - 1)
           for g in range(GLA_GROUP)]
    bwd = [(qb_ref, kb_ref, lab_ref, vb_ref, ob_ref, slice(g * c, (g + 1) * c), cum_b, upper, 0, c // 2)
           for g in reversed(range(GLA_GROUP))]
    heads = range(GLA_HEADS)
    pair = [slice((h // 2) * LANES, (h // 2 + 1) * LANES) for h in heads]
    vcols = [slice(h * GLA_DV, (h + 1) * GLA_DV) for h in heads]

    def stage1(item):
        q_ref, k_ref, la_ref, v_ref, o_ref, rows, cum, tri, i_last, i_mid = item
        la = la_ref[rows, :]
        hi = la.astype(BF16)
        lo = (la - hi.astype(F32)).astype(BF16)
        b = jnp.dot(cum, hi, preferred_element_type=F32) + jnp.dot(cum, lo, preferred_element_type=F32)
        b_last = b[i_last:i_last + 1, :]
        b_mid = b[i_mid:i_mid + 1, :]
        q, k = q_ref[rows, :], k_ref[rows, :]
        return (q * jnp.exp(b - b_mid), (k * jnp.exp(b_mid - b)).astype(BF16), q * jnp.exp(b),
                (k * jnp.exp(b_last - b)).astype(BF16), jnp.exp(b_last))

    def stage2(item, pre):
        v_ref, rows, tri = item[3], item[5], item[7]
        qs, ks, qi, kst, decay = pre
        out = []
        for h in heads:
            mask = head_masks[h % 2]
            qs_h = jnp.where(mask, qs[:, pair[h]], 0.0).astype(BF16)
            sc = lax.dot_general(qs_h, ks[:, pair[h]], NT_DIMS, preferred_element_type=F32)
            v_h = v_ref[rows, vcols[h]]
            kv = lax.dot_general(v_h, kst[:, pair[h]], TN_DIMS, preferred_element_type=F32)
            out.append((jnp.where(tri, sc, 0.0).astype(BF16), kv,
                        jnp.where(mask, qi[:, pair[h]], 0.0).astype(BF16), v_h))
        return out

    def run(items, s_ref):
        pre = [stage1(it) for it in items]
        mid = [stage2(it, p) for it, p in zip(items, pre)]
        states = [s_ref[h] for h in heads]
        for it, p, m in zip(items, pre, mid):
            o_ref, rows, decay = it[4], it[5], p[4]
            for h in heads:
                sc, kv, qi_h, v_h = m[h]
                o = jnp.dot(sc, v_h, preferred_element_type=F32)
                o = o + lax.dot_general(qi_h, states[h].astype(BF16), NT_DIMS, preferred_element_type=F32)
                o_ref[rows, vcols[h]] = o
                states[h] = states[h] * decay[:, pair[h]] + kv
        for h in heads:
            s_ref[h] = states[h]

    run(fwd, sf_ref)
    run(bwd, sb_ref)


def _gla(q, k, v, laf, lab):
    t = q.shape[0]
    rows = GLA_GROUP * GLA_CHUNK
    ng = SEQ // rows
    hk = GLA_HEADS * GLA_DK
    hv = GLA_HEADS * GLA_DV

    def fwd(w):
        return pl.BlockSpec((rows, w), lambda b, n: (b * ng + n, 0))

    def bwd(w):
        return pl.BlockSpec((rows, w), lambda b, n: (b * ng + ng - 1 - n, 0))

    return pl.pallas_call(
        _gla_body,
        grid=(BATCH, ng),
        in_specs=[fwd(hk), fwd(hk), fwd(hv), fwd(hk), bwd(hk), bwd(hk), bwd(hv), bwd(hk)],
        out_specs=[fwd(hv), bwd(hv)],
        out_shape=[jax.ShapeDtypeStruct((t, hv), F32)] * 2,
        scratch_shapes=[pltpu.VMEM((GLA_HEADS, GLA_DV, 2 * GLA_DK), F32)] * 2,
        compiler_params=_params(("arbitrary", "arbitrary"), 32),
        name="gla",
    )(q, k, v, laf, q, k, v, lab)


def _attn_body(sink_ref, q_ref, kp_ref, kc_ref, kn_ref, vp_ref, vc_ref, vn_ref, o_ref):
    step = pl.program_id(1)
    last = pl.num_programs(1) - 1
    qb = ATT_BLOCK
    hd = ATT_HEAD_DIM
    k_all = jnp.concatenate([kp_ref[...], kc_ref[...], kn_ref[...]], axis=0)
    vt_all = jnp.concatenate([vp_ref[...], vc_ref[...], vn_ref[...]], axis=1)
    lane = lax.broadcasted_iota(I32, (1, LANES), 1)
    lo = lane < hd
    j_k = lax.broadcasted_iota(I32, (3 * qb, qb), 0)
    i_q = lax.broadcasted_iota(I32, (3 * qb, qb), 1)
    band = jnp.abs(j_k - qb - i_q) <= ATT_WINDOW
    sinks = [jnp.concatenate([jnp.full((1, qb), sink_ref[4 * g + r], F32) for r in range(4)], axis=1)
             for g in range(ATT_KV_HEADS)]
    work = [(j, g) for j in range(ATT_GROUP) for g in range(ATT_KV_HEADS)]

    def scores(j, g):
        valid = band
        if j == 0:
            valid = valid & ((j_k >= qb) | (step > 0))
        if j == ATT_GROUP - 1:
            valid = valid & ((j_k < 2 * qb) | (step < last))
        valid4 = jnp.concatenate([valid] * 4, axis=1)
        rows = slice(j * qb, (j + 1) * qb)
        kg = k_all[j * qb:(j + 3) * qb, g * LANES:(g + 1) * LANES]
        qa = q_ref[rows, (2 * g) * LANES:(2 * g + 1) * LANES]
        qc = q_ref[rows, (2 * g + 1) * LANES:(2 * g + 2) * LANES]
        zero = jnp.zeros_like(qa)
        lhs = jnp.concatenate([jnp.where(lo, qa, zero), jnp.where(lo, zero, qa),
                               jnp.where(lo, qc, zero), jnp.where(lo, zero, qc)], axis=0)
        st = lax.dot_general(kg, lhs, NT_DIMS, preferred_element_type=F32)
        return jnp.where(valid4, st, NEG_INF)

    def softmax(st, g):
        sink = sinks[g]
        m = jnp.maximum(jnp.max(st, axis=0, keepdims=True), sink)
        p = jnp.exp(st - m)
        return p.astype(BF16), 1.0 / (jnp.sum(p, axis=0, keepdims=True) + jnp.exp(sink - m))

    def output(j, g, p, inv):
        rows = slice(j * qb, (j + 1) * qb)
        vgt = vt_all[g * LANES:(g + 1) * LANES, j * qb:(j + 3) * qb]
        ot = jnp.dot(vgt, p, preferred_element_type=F32) * inv
        pair_a = jnp.concatenate([ot[0:hd, 0:qb], ot[hd:2 * hd, qb:2 * qb]], axis=0)
        pair_c = jnp.concatenate([ot[0:hd, 2 * qb:3 * qb], ot[hd:2 * hd, 3 * qb:4 * qb]], axis=0)
        o_ref[rows, (2 * g) * LANES:(2 * g + 1) * LANES] = pair_a.T.astype(o_ref.dtype)
        o_ref[rows, (2 * g + 1) * LANES:(2 * g + 2) * LANES] = pair_c.T.astype(o_ref.dtype)

    s_all = [scores(j, g) for j, g in work]
    p_all = [softmax(st, g) for st, (j, g) in zip(s_all, work)]
    for (j, g), (p, inv) in zip(work, p_all):
        output(j, g, p, inv)


def _attn(sink, aq, ak2, avt):
    t = aq.shape[0]
    qb = ATT_BLOCK
    nb = SEQ // qb
    steps = nb // ATT_GROUP
    nq = ATT_Q_HEADS * ATT_HEAD_DIM
    nk = 2 * ATT_KV_HEADS * ATT_HEAD_DIM

    def edge_block(b, n, shift):
        return b * nb + jnp.clip(n * ATT_GROUP + shift, 0, nb - 1)

    def k_edge(shift):
        return pl.BlockSpec((qb, nk), lambda b, n: (edge_block(b, n, shift), 0))

    def v_edge(shift):
        return pl.BlockSpec((nk, qb), lambda b, n: (0, edge_block(b, n, shift)))

    def group(w):
        return pl.BlockSpec((ATT_GROUP * qb, w), lambda b, n: (b * steps + n, 0))

    v_group = pl.BlockSpec((nk, ATT_GROUP * qb), lambda b, n: (0, b * steps + n))
    return pl.pallas_call(
        _attn_body,
        grid=(BATCH, steps),
        in_specs=[
            pl.BlockSpec(memory_space=pltpu.SMEM),
            group(nq),
            k_edge(-1), group(nk), k_edge(ATT_GROUP), v_edge(-1), v_group, v_edge(ATT_GROUP),
        ],
        out_specs=group(nq),
        out_shape=jax.ShapeDtypeStruct((t, nq), BF16),
        compiler_params=_params(("arbitrary", "arbitrary"), 48),
        name="attn",
    )(sink, aq, ak2, ak2, ak2, avt, avt, avt)


def _post_body(of_ref, ob_ref, gg_ref, oa_ref, x_ref, mod_ref, ggla_ref, gpm_ref, gpf_ref, wout_ref,
               wrh_ref, wrl_ref, br_ref,
               x1_ref, h2_ref, ti_ref, gt_ref, rk_ref, cnt_ref, base_ref):
    tm = TM_POST

    @pl.when(pl.program_id(0) == 0)
    def _():
        base_ref[...] = jnp.zeros_like(base_ref)

    gate1 = mod_ref[2:3, :]
    shift2 = mod_ref[3:4, :]
    scale2 = mod_ref[4:5, :]
    sub = tm // POST_SUB
    subs = [slice(s * sub, (s + 1) * sub) for s in range(POST_SUB)]

    def mixer_out(rows):
        og = of_ref[rows, :] + ob_ref[rows, :]
        gg = gg_ref[rows, :]
        parts = []
        for h in range(GLA_HEADS):
            cols = slice(h * GLA_DV, (h + 1) * GLA_DV)
            parts.append((_rms(og[:, cols], ggla_ref[...]) * _silu(gg[:, cols])).astype(BF16))
        return jnp.concatenate(parts + [oa_ref[rows, :]], axis=1)

    def ffn_in(s, y):
        rows = subs[s]
        x1 = x_ref[rows, :] + gate1 * _rms(y, gpm_ref[...])
        x1_ref[rows, :] = x1
        h2 = _rms(x1, gpf_ref[...]) * (1.0 + scale2) + shift2
        hi = h2.astype(BF16)
        hi32 = hi.astype(F32)
        _pack_rows(h2_ref.at[pl.ds(s * sub * PACK_SUB, sub * PACK_SUB), :], hi32)
        return hi, (h2 - hi32).astype(BF16)

    o_subs = [mixer_out(rows) for rows in subs]
    y_subs = [jnp.dot(o, wout_ref[...], preferred_element_type=F32) for o in o_subs]
    split = [ffn_in(s, y) for s, y in enumerate(y_subs)]
    h2_hi = jnp.concatenate([hi for hi, _ in split], axis=0)
    h2_lo = jnp.concatenate([lo for _, lo in split], axis=0)

    wrh = wrh_ref[...]
    logits = (lax.dot_general(wrh, h2_hi, NT_DIMS, preferred_element_type=F32)
              + lax.dot_general(wrh, h2_lo, NT_DIMS, preferred_element_type=F32)
              + lax.dot_general(wrl_ref[...], h2_hi, NT_DIMS, preferred_element_type=F32)
              + br_ref[...])
    e_iota = lax.broadcasted_iota(I32, (N_EXPERTS, tm), 0)
    idxs, vals = [], []
    work = logits
    for _ in range(TOP_K):
        m = jnp.max(work, axis=0, keepdims=True)
        idx = jnp.min(jnp.where(work == m, e_iota, N_EXPERTS), axis=0, keepdims=True)
        idxs.append(idx)
        vals.append(m)
        work = jnp.where(e_iota == idx, -jnp.inf, work)
    exps = [jnp.exp(v - vals[0]) for v in vals]
    inv = 1.0 / (exps[0] + exps[1] + exps[2] + exps[3])
    gt_ref[...] = jnp.concatenate([e * inv for e in exps], axis=0)
    ti_ref[...] = jnp.concatenate(idxs, axis=0)

    onehots = [e_iota == idx for idx in idxs]
    member = jnp.where(onehots[0] | onehots[1] | onehots[2] | onehots[3], 1.0, 0.0)
    t_row = lax.broadcasted_iota(I32, (tm, tm), 0)
    t_col = lax.broadcasted_iota(I32, (tm, tm), 1)
    strict = jnp.where(t_row < t_col, 1.0, 0.0).astype(BF16)
    before = base_ref[...] + jnp.dot(member.astype(BF16), strict, preferred_element_type=F32)
    rk_ref[...] = jnp.concatenate(
        [jnp.sum(jnp.where(oh, before, 0.0), axis=0, keepdims=True) for oh in onehots], axis=0).astype(I32)
    new_base = base_ref[...] + jnp.sum(member, axis=1, keepdims=True)
    base_ref[...] = new_base
    cnt_ref[...] = jnp.broadcast_to(new_base, cnt_ref.shape)


def _post(o_f, o_b, gg, o_att, x2, mod, g_gla, g_pm, g_pf, wout, wrh, wrl, br):
    t, d = x2.shape
    tm = TM_POST
    tiles_per_seq = SEQ // tm
    hv = GLA_HEADS * GLA_DV

    def full(a):
        return pl.BlockSpec(a.shape, lambda i: (0,) * a.ndim)

    def rows(w):
        return pl.BlockSpec((tm, w), lambda i: (i, 0))

    def lanes():
        return pl.BlockSpec((TOP_K, tm), lambda i: (0, i))

    return pl.pallas_call(
        _post_body,
        grid=(t // tm,),
        in_specs=[
            rows(hv), rows(hv), rows(hv), rows(hv), rows(d),
            pl.BlockSpec((None, 6, d), lambda i: (i // tiles_per_seq, 0, 0)),
            full(g_gla), full(g_pm), full(g_pf), full(wout), full(wrh), full(wrl), full(br),
        ],
        out_specs=[
            rows(d),
            pl.BlockSpec((tm * PACK_SUB, LANES), lambda i: (i, 0)),
            lanes(), lanes(), lanes(),
            pl.BlockSpec((N_EXPERTS, LANES), lambda i: (0, 0)),
        ],
        out_shape=[
            jax.ShapeDtypeStruct((t, d), F32),
            jax.ShapeDtypeStruct((t * PACK_SUB, LANES), I32),
            jax.ShapeDtypeStruct((TOP_K, t), I32),
            jax.ShapeDtypeStruct((TOP_K, t), F32),
            jax.ShapeDtypeStruct((TOP_K, t), I32),
            jax.ShapeDtypeStruct((N_EXPERTS, LANES), F32),
        ],
        scratch_shapes=[pltpu.VMEM((N_EXPERTS, 1), F32)],
        compiler_params=_params(("arbitrary",), 48),
        name="post",
    )(o_f, o_b, gg, o_att, x2, mod, g_gla, g_pm, g_pf, wout, wrh, wrl, br)


def _route_body(ti_ref, rk_ref, cnt_ref, pos_ref, blk_ref):
    cnt = cnt_ref[...]
    padded = jnp.floor((cnt + (MOE_BM - 1)) * (1.0 / MOE_BM)) * MOE_BM
    starts, ends = [], []
    acc = jnp.zeros((1, LANES), F32)
    for e in range(N_EXPERTS):
        starts.append(acc)
        acc = acc + padded[e:e + 1, :]
        ends.append(acc)
    ti = ti_ref[...]
    off = jnp.zeros(ti.shape, F32)
    for e in range(N_EXPERTS):
        off = jnp.where(ti == e, starts[e][:, 0:1], off)
    pos_ref[...] = rk_ref[...] + off.astype(I32)

    def owner_of(row):
        n_le = jnp.zeros(row.shape, I32)
        for e in range(N_EXPERTS):
            n_le = n_le + jnp.where(ends[e][:, 0:1] <= row, 1, 0)
        return jnp.minimum(n_le, N_EXPERTS - 1)

    block_start = lax.broadcasted_iota(I32, (1, MOE_NB_PAD), 1).astype(F32) * MOE_BM
    owner = owner_of(block_start)
    nxt = jnp.zeros((1, MOE_NB_PAD), I32)
    nxt_blk = jnp.zeros((1, MOE_NB_PAD), I32)
    for e in range(N_EXPERTS):
        end_e = ends[e][:, 0:1]
        nxt = jnp.where(owner == e, jnp.where(end_e < acc[:, 0:1], owner_of(end_e), -1), nxt)
        nxt_blk = jnp.where(owner == e, (end_e * (1.0 / MOE_BM)).astype(I32), nxt_blk)
    used = jnp.broadcast_to((acc[:, 0:1] * (1.0 / MOE_BM)).astype(I32), (1, MOE_NB_PAD))
    blk_ref[...] = jnp.concatenate([owner, nxt, used, nxt_blk, jnp.zeros((SUBLANES - 4, MOE_NB_PAD), I32)], axis=0)


def _route(top_i, rank, counts):
    return pl.pallas_call(
        _route_body,
        out_shape=[
            jax.ShapeDtypeStruct(top_i.shape, I32),
            jax.ShapeDtypeStruct((SUBLANES, MOE_NB_PAD), I32),
        ],
        compiler_params=pltpu.CompilerParams(vmem_limit_bytes=32 * 1024 * 1024),
        name="route",
    )(top_i, rank, counts)


def _sc_workers():
    info = plsc.get_sparse_core_info()
    return info.num_cores, info.num_subcores, info.num_lanes


def _sc_gather_loop(table_hbm, out_hbm, idx_v, base, chunks, bufs, gather_sems, write_sems):
    window = SC_GATHER_WINDOW
    ring = len(bufs)

    def fetch(c, b):
        return pltpu.make_async_copy(table_hbm.at[idx_v.at[pl.ds(c * window, window)]], bufs[b], gather_sems[b])

    def flush(c, b):
        return pltpu.make_async_copy(bufs[b], out_hbm.at[pl.ds(base + c * window, window)], write_sems[b])

    for b in range(ring):
        fetch(b, b).start()

    @pl.loop(0, chunks, step=ring)
    def _(c0):
        for b in range(ring):
            c = c0 + b
            fetch(c, b).wait()
            flush(c, b).start()

            @pl.when(c + ring < chunks)
            def _():
                flush(c, b).wait()
                fetch(c + ring, b).start()

    for b in range(ring):
        flush(chunks - ring + b, b).wait()


def _sc_source_rows(pos_flat, n_rows):
    cores, subcores, lanes = _sc_workers()
    workers = cores * subcores
    per_worker = n_rows // workers
    n_assign = pos_flat.shape[0]
    scan = SC_SCAN_CHUNK
    assert per_worker * workers == n_rows and per_worker % lanes == 0
    assert n_assign % scan == 0 and scan % lanes == 0
    mesh = plsc.VectorSubcoreMesh(core_axis_name="core", subcore_axis_name="subcore")

    @functools.partial(
        pl.kernel,
        out_type=jax.ShapeDtypeStruct((n_rows,), I32),
        mesh=mesh,
        scratch_types=[pltpu.VMEM((per_worker,), I32), pltpu.VMEM((scan,), I32)],
        compiler_params=pltpu.CompilerParams(needs_layout_passes=False),
        name="sc_source_rows",
    )
    def invert(pos_hbm, out_hbm, src_v, pos_v):
        wid = lax.axis_index("subcore") * cores + lax.axis_index("core")
        base = wid * per_worker
        lane = lax.iota(I32, lanes)

        @pl.loop(0, per_worker, step=lanes)
        def _(j):
            src_v[pl.ds(j, lanes)] = (base + j + lane) & (TOKENS - 1)

        @pl.loop(0, n_assign, step=scan)
        def _(a0):
            pltpu.sync_copy(pos_hbm.at[pl.ds(a0, scan)], pos_v)

            @plsc.parallel_loop(0, scan, step=lanes, unroll=SC_SCAN_UNROLL)
            def _(j):
                rel = pos_v[pl.ds(j, lanes)] - base
                mine = (rel >= 0) & (rel < per_worker)
                tok = (a0 + j + lane) & (TOKENS - 1)
                plsc.store_scatter(src_v, [jnp.where(mine, rel, 0)], tok, mask=mine)

        pltpu.sync_copy(src_v, out_hbm.at[pl.ds(base, per_worker)])

    return invert(pos_flat)


def _sc_gather_rows(table, idx):
    cores, subcores, _ = _sc_workers()
    workers = cores * subcores
    n = idx.shape[0]
    window = SC_GATHER_WINDOW
    per_worker = n // workers
    chunks = per_worker // window
    ring = SC_GATHER_RING
    assert per_worker * workers == n and chunks * window == per_worker and chunks % ring == 0
    row_shape = table.shape[1:]
    mesh = plsc.VectorSubcoreMesh(core_axis_name="core", subcore_axis_name="subcore")

    @functools.partial(
        pl.kernel,
        out_type=jax.ShapeDtypeStruct((n,) + row_shape, table.dtype),
        mesh=mesh,
        scratch_types=[pltpu.VMEM((per_worker,), I32)]
        + [pltpu.VMEM((window,) + row_shape, table.dtype)] * ring
        + [pltpu.SemaphoreType.DMA] * (2 * ring),
        name="sc_gather_rows",
    )
    def gather(table_hbm, idx_hbm, out_hbm, idx_v, *scratch):
        wid = lax.axis_index("subcore") * cores + lax.axis_index("core")
        base = wid * per_worker
        pltpu.sync_copy(idx_hbm.at[pl.ds(base, per_worker)], idx_v)
        _sc_gather_loop(table_hbm, out_hbm, idx_v, base, chunks,
                        scratch[:ring], scratch[ring:2 * ring], scratch[2 * ring:])

    return gather(table, idx)


def _experts_body(first, be_ref, nxt_ref, nxtblk_ref, nu_ref, xs_ref, bgu_ref, bd_ref, wgu_hbm, wd_hbm, *rest):
    ys_ref, wgu_f32, wd_f32, wgu_bf, wd_bf, sems = rest[-6:]
    step = pl.program_id(0)
    end = first + pl.num_programs(0) * MOE_SUB
    bm = MOE_BM
    n_used = nu_ref[0]

    def fetch(e):
        return (pltpu.make_async_copy(wgu_hbm.at[e], wgu_f32, sems.at[0]),
                pltpu.make_async_copy(wd_hbm.at[e], wd_f32, sems.at[1]))

    @pl.when((step == 0) & (first < n_used))
    def _():
        for cp in fetch(be_ref[first]):
            cp.start()

    for sub in range(MOE_SUB):
        i = first + step * MOE_SUB + sub
        e = be_ref[i]
        used = i < n_used
        fresh = (i == first) | (e != be_ref[jnp.maximum(i - 1, 0)])
        rows = pl.ds(sub * bm * PACK_SUB, bm * PACK_SUB)
        xs_sub = xs_ref.at[rows, :]
        ys_sub = ys_ref.at[rows, :]

        @pl.when(used & fresh)
        def _():
            for cp in fetch(e):
                cp.wait()
            wgu_bf[...] = wgu_f32[...].astype(BF16)
            wd_bf[...] = wd_f32[...].astype(BF16)

            @pl.when((nxt_ref[i] >= 0) & (nxtblk_ref[i] < end))
            def _():
                for cp in fetch(nxt_ref[i]):
                    cp.start()

        @pl.when(used)
        def _():
            x = jnp.concatenate(_unpack_rows(xs_sub, bm), axis=1).astype(BF16)
            gu = jnp.dot(x, wgu_bf[...], preferred_element_type=F32) + bgu_ref[pl.ds(e, 1), :]
            gate = jnp.minimum(gu[:, 0:D_FF], SWIGLU_LIMIT)
            up = jnp.clip(gu[:, D_FF:2 * D_FF], -SWIGLU_LIMIT, SWIGLU_LIMIT)
            act = ((up + 1.0) * gate * jax.nn.sigmoid(SWIGLU_ALPHA * gate)).astype(BF16)
            y = jnp.dot(act, wd_bf[...], preferred_element_type=F32) + bd_ref[pl.ds(e, 1), :]
            _pack_rows(ys_sub, y.astype(BF16).astype(F32))

        @pl.when(jnp.logical_not(used))
        def _():
            ys_sub[...] = jnp.zeros((bm * PACK_SUB, LANES), I32)


def _experts(first, part_blocks, blocks, xs_part, w_gate_up, b_gate_up, w_down, b_down, ys_prev):
    rows = MOE_SUB * MOE_BM * PACK_SUB
    d = D_MODEL
    steps = part_blocks // MOE_SUB
    assert steps * MOE_SUB == part_blocks and first % MOE_SUB == 0

    def x_block(i, be, nx, nb, nu):
        last = jnp.maximum((jnp.minimum(nu[0], first + part_blocks) - 1 - first) // MOE_SUB, 0)
        return jnp.minimum(i, last)

    in_specs = [
        pl.BlockSpec((rows, LANES), lambda i, be, nx, nb, nu: (x_block(i, be, nx, nb, nu), 0)),
        pl.BlockSpec(b_gate_up.shape, lambda i, be, nx, nb, nu: (0, 0)),
        pl.BlockSpec(b_down.shape, lambda i, be, nx, nb, nu: (0, 0)),
        pl.BlockSpec(memory_space=pl.ANY),
        pl.BlockSpec(memory_space=pl.ANY),
    ]
    operands = [blocks[0, :MOE_NB], blocks[1, :MOE_NB], blocks[3, :MOE_NB], blocks[2, :1],
                xs_part, b_gate_up, b_down, w_gate_up, w_down]
    aliases = {}
    if ys_prev is not None:
        in_specs.append(pl.BlockSpec(memory_space=pl.ANY))
        aliases = {len(operands): 0}
        operands.append(ys_prev)
    grid_spec = pltpu.PrefetchScalarGridSpec(
        num_scalar_prefetch=4,
        grid=(steps,),
        in_specs=in_specs,
        out_specs=pl.BlockSpec((rows, LANES), lambda i, be, nx, nb, nu: (first // MOE_SUB + i, 0)),
        scratch_shapes=[
            pltpu.VMEM((d, 2 * D_FF), F32), pltpu.VMEM((D_FF, d), F32),
            pltpu.VMEM((d, 2 * D_FF), BF16), pltpu.VMEM((D_FF, d), BF16),
            pltpu.SemaphoreType.DMA((2,)),
        ],
    )
    return pl.pallas_call(
        functools.partial(_experts_body, first),
        grid_spec=grid_spec,
        out_shape=jax.ShapeDtypeStruct((MOE_ROWS * PACK_SUB, LANES), I32),
        input_output_aliases=aliases,
        compiler_params=_params(("arbitrary",), 48),
        name="experts",
    )(*operands)


def _combine_body(gates_ref, x1_ref, mod_ref, gpost_ref, y0_ref, y1_ref, y2_ref, y3_ref, *rest):
    o_ref = rest[-1]
    tm = TM_COMB
    gates = gates_ref[...]
    y_hi = jnp.zeros((tm, PACK_COLS), F32)
    y_lo = jnp.zeros((tm, PACK_COLS), F32)
    for k, yk_ref in enumerate((y0_ref, y1_ref, y2_ref, y3_ref)):
        hi, lo = _unpack_rows(yk_ref, tm)
        y_hi = y_hi + hi * gates[:, k:k + 1]
        y_lo = y_lo + lo * gates[:, k:k + 1]
    y = jnp.concatenate([y_hi, y_lo], axis=1)
    gate2 = mod_ref[5:6, :]
    o_ref[...] = x1_ref[...] + gate2 * _rms(y, gpost_ref[...])


def _combine(part, gates_t, x1, mod, g_post, y4_part, out_prev):
    t, d = x1.shape
    tm = TM_COMB
    tiles = t // COMB_PARTS // tm
    tile0 = part * tiles
    tiles_per_seq = SEQ // tm

    def slab(k):
        return pl.BlockSpec((tm * PACK_SUB, LANES), lambda i: (k * tiles + i, 0))

    in_specs = [
        pl.BlockSpec((tm, TOP_K), lambda i: (tile0 + i, 0)),
        pl.BlockSpec((tm, d), lambda i: (tile0 + i, 0)),
        pl.BlockSpec((None, 6, d), lambda i: ((tile0 + i) // tiles_per_seq, 0, 0)),
        pl.BlockSpec(g_post.shape, lambda i: (0, 0)),
        slab(0), slab(1), slab(2), slab(3),
    ]
    operands = [gates_t, x1, mod, g_post, y4_part, y4_part, y4_part, y4_part]
    aliases = {}
    if out_prev is not None:
        in_specs.append(pl.BlockSpec(memory_space=pl.ANY))
        aliases = {len(operands): 0}
        operands.append(out_prev)
    return pl.pallas_call(
        _combine_body,
        grid=(tiles,),
        in_specs=in_specs,
        out_specs=pl.BlockSpec((tm, d), lambda i: (tile0 + i, 0)),
        out_shape=jax.ShapeDtypeStruct((t, d), F32),
        input_output_aliases=aliases,
        compiler_params=_params(("arbitrary",), 48),
        name="combine",
    )(*operands)


def _rotary_tables():
    half = ROT_DIM // 2
    inv_freq = ROPE_THETA ** (-2.0 * np.arange(half, dtype=np.float32) / ROT_DIM)
    ang = np.arange(SEQ, dtype=np.float32)[:, None] * inv_freq[None, :].astype(np.float32)
    cos, sin = np.cos(ang), np.sin(ang)
    ones = np.ones((SEQ, ATT_HEAD_DIM - ROT_DIM), np.float32)
    zeros = np.zeros((SEQ, ATT_HEAD_DIM - ROT_DIM), np.float32)
    zh = np.zeros((SEQ, half), np.float32)
    reps = LANES // ATT_HEAD_DIM
    rc = np.tile(np.concatenate([cos, cos, ones], axis=1), (1, reps))
    rm = np.tile(np.concatenate([-sin, zh, zeros], axis=1), (1, reps))
    rp = np.tile(np.concatenate([zh, sin, zeros], axis=1), (1, reps))
    return tuple(jnp.asarray(t, F32) for t in (rc, rm, rp))


def _mixer_inputs(w_in, w_gk_fwd, b_gk_fwd, w_gk_bwd, b_gk_bwd):
    hk = GLA_HEADS * GLA_DK
    hv = GLA_HEADS * GLA_DV
    w = w_in[0]
    o_lr = 2 * hk + 2 * hv
    o_aq = o_lr + 2 * GLA_RANK
    o_ak = o_aq + ATT_Q_HEADS * ATT_HEAD_DIM
    o_av = o_ak + ATT_KV_HEADS * ATT_HEAD_DIM
    hd = ATT_HEAD_DIM
    wa = w[:, :o_lr].astype(BF16)
    wlr = w[:, o_lr:o_aq].astype(BF16)
    dup = lambda m: jnp.concatenate([m[:, g * hd:(g + 1) * hd] for g in range(ATT_KV_HEADS) for _ in range(2)], axis=1)
    wb = jnp.concatenate([w[:, o_aq:o_ak], dup(w[:, o_ak:o_av])], axis=1).astype(BF16)
    wvt = dup(w[:, o_av:o_av + ATT_KV_HEADS * hd]).T.astype(BF16)
    zr = jnp.zeros((GLA_RANK, hk), F32)
    wgk = jnp.concatenate([jnp.concatenate([w_gk_fwd[0], zr], axis=1),
                           jnp.concatenate([zr, w_gk_bwd[0]], axis=1)], axis=0).astype(BF16)
    bgk = jnp.concatenate([b_gk_fwd[0], b_gk_bwd[0]])[None, :]
    return (wa, wlr, wgk, bgk, wb, wvt) + _rotary_tables()


def kernel(x, c, w_ada, b_ada, g_pre_mix, g_post_mix, w_in, w_gk_fwd, b_gk_fwd, w_gk_bwd, b_gk_bwd, g_gla_out,
           attn_sink, w_out, g_pre_ffn, g_post_ffn, w_router, b_router, w_gate_up, b_gate_up, w_down, b_down):
    assert x.shape == (BATCH, SEQ, D_MODEL) and w_ada.shape[0] == 1
    d = D_MODEL
    x2 = x.reshape(TOKENS, d)

    c_pad = jnp.pad(c, ((0, SUBLANES - BATCH), (0, 0)))
    mod = _ada(c_pad, w_ada[0], b_ada)[:BATCH].reshape(BATCH, 6, d)

    mixer_in = _mixer_inputs(w_in, w_gk_fwd, b_gk_fwd, w_gk_bwd, b_gk_bwd)
    q, k, v, gg, laf, lab, aq, ak2, avt = _inproj(x2, mod, g_pre_mix, *mixer_in)
    o_f, o_b = _gla(q, k, v, laf, lab)
    o_att = _attn(attn_sink[0], aq, ak2, avt)

    wr_t = w_router[0].T
    wrh = wr_t.astype(BF16)
    wrl = (wr_t - wrh.astype(F32)).astype(BF16)
    x1, h2_tiles, top_i, gates, rank, counts = _post(
        o_f, o_b, gg, o_att, x2, mod, g_gla_out, g_post_mix, g_pre_ffn, w_out[0].astype(BF16), wrh, wrl,
        b_router[0][:, None])

    pos, blocks = _route(top_i, rank, counts)

    src = _sc_source_rows(pos.reshape(TOP_K * TOKENS), MOE_ROWS)
    h2_rows = h2_tiles.reshape(TOKENS, PACK_SUB, LANES)
    ys = None
    first = 0
    for part_blocks in MOE_PART_BLOCKS:
        row0, n_rows = first * MOE_BM, part_blocks * MOE_BM
        xs_p = _sc_gather_rows(h2_rows, src[row0:row0 + n_rows])
        ys = _experts(first, part_blocks, blocks, xs_p.reshape(n_rows * PACK_SUB, LANES),
                      w_gate_up[0], b_gate_up[0], w_down[0], b_down[0], ys)
        first += part_blocks
    assert first == MOE_NB

    ys_rows = ys.reshape(MOE_ROWS, PACK_SUB, LANES)
    gates_t = gates.T
    part_tokens = TOKENS // COMB_PARTS
    out = None
    for p in range(COMB_PARTS):
        idx = pos[:, p * part_tokens:(p + 1) * part_tokens].reshape(TOP_K * part_tokens)
        y4_p = _sc_gather_rows(ys_rows, idx).reshape(TOP_K * part_tokens * PACK_SUB, LANES)
        out = _combine(p, gates_t, x1, mod, g_post_ffn, y4_p, out)
    return out.reshape(BATCH, SEQ, d)
```

```python
import functools

import jax
import jax.numpy as jnp
import numpy as np
from jax import lax
from jax.experimental import pallas as pl
from jax.experimental.pallas import tpu as pltpu
from jax.experimental.pallas import tpu_sc as plsc

F32 = jnp.float32
BF16 = jnp.bfloat16
I32 = jnp.int32

D_MODEL = 1024
BATCH = 2
SEQ = 8192
TOKENS = BATCH * SEQ
GLA_HEADS = 4
GLA_DV = 128
GLA_DK = 64
GLA_RANK = 16
GLA_GATE_NORMALIZER = 16.0
GLA_CHUNK = 64
ATT_Q_HEADS = 8
ATT_KV_HEADS = 2
ATT_HEAD_DIM = 64
ATT_WINDOW = 128
ATT_BLOCK = 128
ROT_DIM = 16
ROPE_THETA = 500000.0
N_EXPERTS = 32
TOP_K = 4
D_FF = 1024
SWIGLU_LIMIT = 7.0
SWIGLU_ALPHA = 1.702
NORM_EPS = 1e-6
NEG_INF = -1e30
LOG2_E = 1.4426950408889634

LANES = 128
SUBLANES = 8
PACK_COLS = D_MODEL // 2
PACK_SUB = PACK_COLS // LANES

TM_IN = 1024
IN_SUB = 4
GLA_GROUP = 16
ATT_GROUP = 16
TM_POST = 1024
POST_SUB = 4
MOE_BM = 256
MOE_ROWS = TOKENS * TOP_K + N_EXPERTS * MOE_BM
MOE_NB = MOE_ROWS // MOE_BM
MOE_SUB = 2
MOE_PART_BLOCKS = (32, 128, 128)
COMB_PARTS = 1
MOE_NB_PAD = ((MOE_NB + LANES - 1) // LANES) * LANES
SC_SCAN_CHUNK = 4096
SC_SCAN_UNROLL = 8
TM_COMB = 512
SC_GATHER_WINDOW = 32
SC_GATHER_RING = 4

NT_DIMS = (((1,), (1,)), ((), ()))
TN_DIMS = (((0,), (0,)), ((), ()))


def _params(semantics, vmem_mib):
    return pltpu.CompilerParams(dimension_semantics=semantics, vmem_limit_bytes=vmem_mib * 1024 * 1024)


def _rms(x, g):
    return x * lax.rsqrt(jnp.mean(x * x, axis=-1, keepdims=True) + NORM_EPS) * g


def _silu(x):
    return x * jax.nn.sigmoid(x)


def _pack_rows(ref, v):
    m = v.shape[0]
    bits = lax.bitcast_convert_type(v, jnp.uint32)
    word = lax.bitcast_convert_type(bits[:, :PACK_COLS] | (bits[:, PACK_COLS:] >> 16), I32)
    for s in range(PACK_SUB):
        ref[pl.ds(s, m, stride=PACK_SUB), :] = word[:, s * LANES:(s + 1) * LANES]


def _unpack_rows(ref, m):
    word = jnp.concatenate([ref[pl.ds(s, m, stride=PACK_SUB), :] for s in range(PACK_SUB)], axis=1)
    bits = lax.bitcast_convert_type(word, jnp.uint32)
    hi = lax.bitcast_convert_type(bits & jnp.uint32(0xFFFF0000), F32)
    lo = lax.bitcast_convert_type(bits << 16, F32)
    return hi, lo


def _ada_body(c_ref, w_ref, b_ref, o_ref):
    ca = _silu(c_ref[...]).astype(BF16)
    o_ref[...] = jnp.dot(ca, w_ref[...].astype(BF16), preferred_element_type=F32) + b_ref[...]


def _ada(c_pad, w_ada, b_ada):
    d = D_MODEL
    return pl.pallas_call(
        _ada_body,
        grid=(6,),
        in_specs=[
            pl.BlockSpec((SUBLANES, d), lambda j: (0, 0)),
            pl.BlockSpec((d, d), lambda j: (0, j)),
            pl.BlockSpec((1, d), lambda j: (0, j)),
        ],
        out_specs=pl.BlockSpec((SUBLANES, d), lambda j: (0, j)),
        out_shape=jax.ShapeDtypeStruct((SUBLANES, 6 * d), F32),
        compiler_params=_params(("arbitrary",), 32),
        name="ada",
    )(c_pad, w_ada, b_ada)


def _rotary(x, cos_t, msin_t, psin_t):
    width = x.shape[1]
    reps = width // LANES
    c = jnp.concatenate([cos_t] * reps, axis=1)
    m = jnp.concatenate([msin_t] * reps, axis=1)
    p = jnp.concatenate([psin_t] * reps, axis=1)
    half = ROT_DIM // 2
    return x * c + pltpu.roll(x, width - half, 1) * m + pltpu.roll(x, half, 1) * p


def _inproj_body(x_ref, mod_ref, g_ref, wa_ref, wlr_ref, wgk_ref, bgk_ref, wb_ref, wvt_ref, rc_ref, rm_ref, rp_ref,
                 q_ref, k_ref, v_ref, gg_ref, laf_ref, lab_ref, aq_ref, ak_ref, avt_ref):
    shift = mod_ref[0:1, :]
    scale = mod_ref[1:2, :]
    hk = GLA_HEADS * GLA_DK
    hv = GLA_HEADS * GLA_DV
    nq = ATT_Q_HEADS * ATT_HEAD_DIM
    nk = 2 * ATT_KV_HEADS * ATT_HEAD_DIM
    sub = x_ref.shape[0] // IN_SUB
    subs = [slice(s * sub, (s + 1) * sub) for s in range(IN_SUB)]

    def hidden(rows):
        return (_rms(x_ref[rows, :], g_ref[...]) * (1.0 + scale) + shift).astype(BF16)

    def project(h):
        return (jnp.dot(h, wa_ref[...], preferred_element_type=F32),
                jnp.dot(h, wlr_ref[...], preferred_element_type=F32),
                jnp.dot(h, wb_ref[...], preferred_element_type=F32),
                lax.dot_general(wvt_ref[...], h, NT_DIMS, preferred_element_type=F32))

    def finish(rows, pa, plr, pb, pvt):
        q_ref[rows, :] = pa[:, 0:hk] * (GLA_DK ** -0.5)
        k_ref[rows, :] = pa[:, hk:2 * hk]
        v_ref[rows, :] = pa[:, 2 * hk:2 * hk + hv].astype(BF16)
        gg_ref[rows, :] = pa[:, 2 * hk + hv:2 * hk + 2 * hv]
        gk = jnp.dot(plr.astype(BF16), wgk_ref[...], preferred_element_type=F32) + bgk_ref[...]
        la = (jnp.minimum(gk, 0.0) - jnp.log1p(jnp.exp(-jnp.abs(gk)))) * (1.0 / GLA_GATE_NORMALIZER)
        laf_ref[rows, :] = la[:, 0:hk]
        lab_ref[rows, :] = la[:, hk:2 * hk]
        rc, rm, rp = rc_ref[rows, :], rm_ref[rows, :], rp_ref[rows, :]
        aq_ref[rows, :] = (_rotary(pb[:, 0:nq], rc, rm, rp) * (ATT_HEAD_DIM ** -0.5 * LOG2_E)).astype(BF16)
        ak_ref[rows, :] = _rotary(pb[:, nq:nq + nk], rc, rm, rp).astype(BF16)
        avt_ref[:, rows] = pvt.astype(BF16)

    hs = [hidden(rows) for rows in subs]
    ps = [project(h) for h in hs]
    for rows, p in zip(subs, ps):
        finish(rows, *p)


def _inproj(x2, mod, g_pre, wa, wlr, wgk, bgk, wb, wvt, rc, rm, rp):
    t, d = x2.shape
    tm = TM_IN
    tiles_per_seq = SEQ // tm
    hk = GLA_HEADS * GLA_DK
    hv = GLA_HEADS * GLA_DV
    nq = ATT_Q_HEADS * ATT_HEAD_DIM
    nk = 2 * ATT_KV_HEADS * ATT_HEAD_DIM

    def full(a):
        return pl.BlockSpec(a.shape, lambda i: (0,) * a.ndim)

    def rows(w):
        return pl.BlockSpec((tm, w), lambda i: (i, 0))

    def table():
        return pl.BlockSpec((tm, LANES), lambda i: (i % tiles_per_seq, 0))

    out_widths = [(hk, F32), (hk, F32), (hv, BF16), (hv, F32), (hk, F32), (hk, F32), (nq, BF16), (nk, BF16)]
    return pl.pallas_call(
        _inproj_body,
        grid=(t // tm,),
        in_specs=[
            rows(d),
            pl.BlockSpec((None, 6, d), lambda i: (i // tiles_per_seq, 0, 0)),
            full(g_pre), full(wa), full(wlr), full(wgk), full(bgk), full(wb), full(wvt),
            table(), table(), table(),
        ],
        out_specs=[rows(w) for w, _ in out_widths] + [pl.BlockSpec((nk, tm), lambda i: (0, i))],
        out_shape=[jax.ShapeDtypeStruct((t, w), dt) for w, dt in out_widths] + [jax.ShapeDtypeStruct((nk, t), BF16)],
        compiler_params=_params(("arbitrary",), 56),
        name="inproj",
    )(x2, mod, g_pre, wa, wlr, wgk, bgk, wb, wvt, rc, rm, rp)


def _gla_body(qf_ref, kf_ref, vf_ref, laf_ref, qb_ref, kb_ref, vb_ref, lab_ref, of_ref, ob_ref, sf_ref, sb_ref):
    @pl.when(pl.program_id(1) == 0)
    def _():
        sf_ref[...] = jnp.zeros_like(sf_ref)
        sb_ref[...] = jnp.zeros_like(sb_ref)

    c = GLA_CHUNK
    r_i = lax.broadcasted_iota(I32, (c, c), 0)
    c_i = lax.broadcasted_iota(I32, (c, c), 1)
    lower = c_i <= r_i
    upper = c_i >= r_i
    cum_f = jnp.where(lower, 1.0, 0.0).astype(BF16)
    cum_b = jnp.where(upper, 1.0, 0.0).astype(BF16)
    lane = lax.broadcasted_iota(I32, (1, LANES), 1)
    head_masks = (lane < GLA_DK, lane >= GLA_DK)

    fwd = [(qf_ref, kf_ref, laf_ref, vf_ref, of_ref, slice(g * c, (g + 1) * c), cum_f, lower, c - 1, c // 2 - 1)
           for g in range(GLA_GROUP)]
    bwd = [(qb_ref, kb_ref, lab_ref, vb_ref, ob_ref, slice(g * c, (g + 1) * c), cum_b, upper, 0, c // 2)
           for g in reversed(range(GLA_GROUP))]
    heads = range(GLA_HEADS)
    pair = [slice((h // 2) * LANES, (h // 2 + 1) * LANES) for h in heads]
    vcols = [slice(h * GLA_DV, (h + 1) * GLA_DV) for h in heads]

    def stage1(item):
        q_ref, k_ref, la_ref, v_ref, o_ref, rows, cum, tri, i_last, i_mid = item
        la = la_ref[rows, :]
        hi = la.astype(BF16)
        lo = (la - hi.astype(F32)).astype(BF16)
        b = jnp.dot(cum, hi, preferred_element_type=F32) + jnp.dot(cum, lo, preferred_element_type=F32)
        b_last = b[i_last:i_last + 1, :]
        b_mid = b[i_mid:i_mid + 1, :]
        q, k = q_ref[rows, :], k_ref[rows, :]
        return (q * jnp.exp(b - b_mid), (k * jnp.exp(b_mid - b)).astype(BF16), q * jnp.exp(b),
                (k * jnp.exp(b_last - b)).astype(BF16), jnp.exp(b_last))

    def stage2(item, pre):
        v_ref, rows, tri = item[3], item[5], item[7]
        qs, ks, qi, kst, decay = pre
        out = []
        for h in heads:
            mask = head_masks[h % 2]
            qs_h = jnp.where(mask, qs[:, pair[h]], 0.0).astype(BF16)
            sc = lax.dot_general(qs_h, ks[:, pair[h]], NT_DIMS, preferred_element_type=F32)
            v_h = v_ref[rows, vcols[h]]
            kv = lax.dot_general(v_h, kst[:, pair[h]], TN_DIMS, preferred_element_type=F32)
            out.append((jnp.where(tri, sc, 0.0).astype(BF16), kv,
                        jnp.where(mask, qi[:, pair[h]], 0.0).astype(BF16), v_h))
        return out

    def run(items, s_ref):
        pre = [stage1(it) for it in items]
        mid = [stage2(it, p) for it, p in zip(items, pre)]
        states = [s_ref[h] for h in heads]
        for it, p, m in zip(items, pre, mid):
            o_ref, rows, decay = it[4], it[5], p[4]
            for h in heads:
                sc, kv, qi_h, v_h = m[h]
                o = jnp.dot(sc, v_h, preferred_element_type=F32)
                o = o + lax.dot_general(qi_h, states[h].astype(BF16), NT_DIMS, preferred_element_type=F32)
                o_ref[rows, vcols[h]] = o
                states[h] = states[h] * decay[:, pair[h]] + kv
        for h in heads:
            s_ref[h] = states[h]

    run(fwd, sf_ref)
    run(bwd, sb_ref)


def _gla(q, k, v, laf, lab):
    t = q.shape[0]
    rows = GLA_GROUP * GLA_CHUNK
    ng = SEQ // rows
    hk = GLA_HEADS * GLA_DK
    hv = GLA_HEADS * GLA_DV

    def fwd(w):
        return pl.BlockSpec((rows, w), lambda b, n: (b * ng + n, 0))

    def bwd(w):
        return pl.BlockSpec((rows, w), lambda b, n: (b * ng + ng - 1 - n, 0))

    return pl.pallas_call(
        _gla_body,
        grid=(BATCH, ng),
        in_specs=[fwd(hk), fwd(hk), fwd(hv), fwd(hk), bwd(hk), bwd(hk), bwd(hv), bwd(hk)],
        out_specs=[fwd(hv), bwd(hv)],
        out_shape=[jax.ShapeDtypeStruct((t, hv), F32)] * 2,
        scratch_shapes=[pltpu.VMEM((GLA_HEADS, GLA_DV, 2 * GLA_DK), F32)] * 2,
        compiler_params=_params(("arbitrary", "arbitrary"), 32),
        name="gla",
    )(q, k, v, laf, q, k, v, lab)


def _attn_body(sink_ref, q_ref, kp_ref, kc_ref, kn_ref, vp_ref, vc_ref, vn_ref, o_ref):
    step = pl.program_id(1)
    last = pl.num_programs(1) - 1
    qb = ATT_BLOCK
    hd = ATT_HEAD_DIM
    k_all = jnp.concatenate([kp_ref[...], kc_ref[...], kn_ref[...]], axis=0)
    vt_all = jnp.concatenate([vp_ref[...], vc_ref[...], vn_ref[...]], axis=1)
    lane = lax.broadcasted_iota(I32, (1, LANES), 1)
    lo = lane < hd
    j_k = lax.broadcasted_iota(I32, (3 * qb, qb), 0)
    i_q = lax.broadcasted_iota(I32, (3 * qb, qb), 1)
    band = jnp.abs(j_k - qb - i_q) <= ATT_WINDOW
    sinks = [jnp.concatenate([jnp.full((1, qb), sink_ref[4 * g + r] * LOG2_E, F32) for r in range(4)], axis=1)
             for g in range(ATT_KV_HEADS)]
    ones_rows = jnp.ones((SUBLANES, 3 * qb), BF16)
    work = [(j, g) for j in range(ATT_GROUP) for g in range(ATT_KV_HEADS)]

    def scores(j, g):
        valid = band
        if j == 0:
            valid = valid & ((j_k >= qb) | (step > 0))
        if j == ATT_GROUP - 1:
            valid = valid & ((j_k < 2 * qb) | (step < last))
        valid4 = jnp.concatenate([valid] * 4, axis=1)
        rows = slice(j * qb, (j + 1) * qb)
        kg = k_all[j * qb:(j + 3) * qb, g * LANES:(g + 1) * LANES]
        qa = q_ref[rows, (2 * g) * LANES:(2 * g + 1) * LANES]
        qc = q_ref[rows, (2 * g + 1) * LANES:(2 * g + 2) * LANES]
        zero = jnp.zeros_like(qa)
        lhs = jnp.concatenate([jnp.where(lo, qa, zero), jnp.where(lo, zero, qa),
                               jnp.where(lo, qc, zero), jnp.where(lo, zero, qc)], axis=0)
        st = lax.dot_general(kg, lhs, NT_DIMS, preferred_element_type=F32)
        return jnp.concatenate([jnp.where(valid4[0:qb], st[0:qb], NEG_INF), st[qb:2 * qb],
                                jnp.where(valid4[2 * qb:3 * qb], st[2 * qb:3 * qb], NEG_INF)], axis=0)

    def softmax(st, g):
        sink = sinks[g]
        m = jnp.maximum(jnp.max(st, axis=0, keepdims=True), sink)
        return jnp.exp2(st - m).astype(BF16), jnp.exp2(sink - m)

    def output(j, g, p, p_sink):
        rows = slice(j * qb, (j + 1) * qb)
        vgt = vt_all[g * LANES:(g + 1) * LANES, j * qb:(j + 3) * qb]
        res = jnp.dot(jnp.concatenate([vgt, ones_rows], axis=0), p, preferred_element_type=F32)
        ot = res[0:LANES] * (1.0 / (res[LANES:LANES + 1] + p_sink))
        pair_a = jnp.concatenate([ot[0:hd, 0:qb], ot[hd:2 * hd, qb:2 * qb]], axis=0)
        pair_c = jnp.concatenate([ot[0:hd, 2 * qb:3 * qb], ot[hd:2 * hd, 3 * qb:4 * qb]], axis=0)
        o_ref[rows, (2 * g) * LANES:(2 * g + 1) * LANES] = pair_a.T.astype(o_ref.dtype)
        o_ref[rows, (2 * g + 1) * LANES:(2 * g + 2) * LANES] = pair_c.T.astype(o_ref.dtype)

    s_all = [scores(j, g) for j, g in work]
    p_all = [softmax(st, g) for st, (j, g) in zip(s_all, work)]
    for (j, g), (p, p_sink) in zip(work, p_all):
        output(j, g, p, p_sink)


def _attn(sink, aq, ak2, avt):
    t = aq.shape[0]
    qb = ATT_BLOCK
    nb = SEQ // qb
    steps = nb // ATT_GROUP
    nq = ATT_Q_HEADS * ATT_HEAD_DIM
    nk = 2 * ATT_KV_HEADS * ATT_HEAD_DIM

    def edge_block(b, n, shift):
        return b * nb + jnp.clip(n * ATT_GROUP + shift, 0, nb - 1)

    def k_edge(shift):
        return pl.BlockSpec((qb, nk), lambda b, n: (edge_block(b, n, shift), 0))

    def v_edge(shift):
        return pl.BlockSpec((nk, qb), lambda b, n: (0, edge_block(b, n, shift)))

    def group(w):
        return pl.BlockSpec((ATT_GROUP * qb, w), lambda b, n: (b * steps + n, 0))

    v_group = pl.BlockSpec((nk, ATT_GROUP * qb), lambda b, n: (0, b * steps + n))
    return pl.pallas_call(
        _attn_body,
        grid=(BATCH, steps),
        in_specs=[
            pl.BlockSpec(memory_space=pltpu.SMEM),
            group(nq),
            k_edge(-1), group(nk), k_edge(ATT_GROUP), v_edge(-1), v_group, v_edge(ATT_GROUP),
        ],
        out_specs=group(nq),
        out_shape=jax.ShapeDtypeStruct((t, nq), BF16),
        compiler_params=_params(("arbitrary", "arbitrary"), 48),
        name="attn",
    )(sink, aq, ak2, ak2, ak2, avt, avt, avt)


def _post_body(of_ref, ob_ref, gg_ref, oa_ref, x_ref, mod_ref, ggla_ref, gpm_ref, gpf_ref, wout_ref,
               wrh_ref, wrl_ref, br_ref,
               x1_ref, h2_ref, ti_ref, gt_ref, rk_ref, cnt_ref, base_ref):
    tm = TM_POST

    @pl.when(pl.program_id(0) == 0)
    def _():
        base_ref[...] = jnp.zeros_like(base_ref)

    gate1 = mod_ref[2:3, :]
    shift2 = mod_ref[3:4, :]
    scale2 = mod_ref[4:5, :]
    sub = tm // POST_SUB
    subs = [slice(s * sub, (s + 1) * sub) for s in range(POST_SUB)]

    def mixer_out(rows):
        og = of_ref[rows, :] + ob_ref[rows, :]
        gg = gg_ref[rows, :]
        parts = []
        for h in range(GLA_HEADS):
            cols = slice(h * GLA_DV, (h + 1) * GLA_DV)
            parts.append((_rms(og[:, cols], ggla_ref[...]) * _silu(gg[:, cols])).astype(BF16))
        return jnp.concatenate(parts + [oa_ref[rows, :]], axis=1)

    def ffn_in(s, y):
        rows = subs[s]
        x1 = x_ref[rows, :] + gate1 * _rms(y, gpm_ref[...])
        x1_ref[rows, :] = x1
        h2 = _rms(x1, gpf_ref[...]) * (1.0 + scale2) + shift2
        hi = h2.astype(BF16)
        hi32 = hi.astype(F32)
        _pack_rows(h2_ref.at[pl.ds(s * sub * PACK_SUB, sub * PACK_SUB), :], hi32)
        return hi, (h2 - hi32).astype(BF16)

    o_subs = [mixer_out(rows) for rows in subs]
    y_subs = [jnp.dot(o, wout_ref[...], preferred_element_type=F32) for o in o_subs]
    split = [ffn_in(s, y) for s, y in enumerate(y_subs)]
    h2_hi = jnp.concatenate([hi for hi, _ in split], axis=0)
    h2_lo = jnp.concatenate([lo for _, lo in split], axis=0)

    wrh = wrh_ref[...]
    logits = (lax.dot_general(wrh, h2_hi, NT_DIMS, preferred_element_type=F32)
              + lax.dot_general(wrh, h2_lo, NT_DIMS, preferred_element_type=F32)
              + lax.dot_general(wrl_ref[...], h2_hi, NT_DIMS, preferred_element_type=F32)
              + br_ref[...])
    e_iota = lax.broadcasted_iota(I32, (N_EXPERTS, tm), 0)
    idxs, vals = [], []
    work = logits
    for _ in range(TOP_K):
        m = jnp.max(work, axis=0, keepdims=True)
        idx = jnp.min(jnp.where(work == m, e_iota, N_EXPERTS), axis=0, keepdims=True)
        idxs.append(idx)
        vals.append(m)
        work = jnp.where(e_iota == idx, -jnp.inf, work)
    exps = [jnp.exp(v - vals[0]) for v in vals]
    inv = 1.0 / (exps[0] + exps[1] + exps[2] + exps[3])
    gt_ref[...] = jnp.concatenate([e * inv for e in exps], axis=0)
    ti_ref[...] = jnp.concatenate(idxs, axis=0)

    onehots = [e_iota == idx for idx in idxs]
    member = jnp.where(onehots[0] | onehots[1] | onehots[2] | onehots[3], 1.0, 0.0)
    t_row = lax.broadcasted_iota(I32, (tm, tm), 0)
    t_col = lax.broadcasted_iota(I32, (tm, tm), 1)
    strict = jnp.where(t_row < t_col, 1.0, 0.0).astype(BF16)
    before = base_ref[...] + jnp.dot(member.astype(BF16), strict, preferred_element_type=F32)
    rk_ref[...] = jnp.concatenate(
        [jnp.sum(jnp.where(oh, before, 0.0), axis=0, keepdims=True) for oh in onehots], axis=0).astype(I32)
    new_base = base_ref[...] + jnp.sum(member, axis=1, keepdims=True)
    base_ref[...] = new_base
    cnt_ref[...] = jnp.broadcast_to(new_base, cnt_ref.shape)


def _post(o_f, o_b, gg, o_att, x2, mod, g_gla, g_pm, g_pf, wout, wrh, wrl, br):
    t, d = x2.shape
    tm = TM_POST
    tiles_per_seq = SEQ // tm
    hv = GLA_HEADS * GLA_DV

    def full(a):
        return pl.BlockSpec(a.shape, lambda i: (0,) * a.ndim)

    def rows(w):
        return pl.BlockSpec((tm, w), lambda i: (i, 0))

    def lanes():
        return pl.BlockSpec((TOP_K, tm), lambda i: (0, i))

    return pl.pallas_call(
        _post_body,
        grid=(t // tm,),
        in_specs=[
            rows(hv), rows(hv), rows(hv), rows(hv), rows(d),
            pl.BlockSpec((None, 6, d), lambda i: (i // tiles_per_seq, 0, 0)),
            full(g_gla), full(g_pm), full(g_pf), full(wout), full(wrh), full(wrl), full(br),
        ],
        out_specs=[
            rows(d),
            pl.BlockSpec((tm * PACK_SUB, LANES), lambda i: (i, 0)),
            lanes(), lanes(), lanes(),
            pl.BlockSpec((N_EXPERTS, LANES), lambda i: (0, 0)),
        ],
        out_shape=[
            jax.ShapeDtypeStruct((t, d), F32),
            jax.ShapeDtypeStruct((t * PACK_SUB, LANES), I32),
            jax.ShapeDtypeStruct((TOP_K, t), I32),
            jax.ShapeDtypeStruct((TOP_K, t), F32),
            jax.ShapeDtypeStruct((TOP_K, t), I32),
            jax.ShapeDtypeStruct((N_EXPERTS, LANES), F32),
        ],
        scratch_shapes=[pltpu.VMEM((N_EXPERTS, 1), F32)],
        compiler_params=_params(("arbitrary",), 48),
        name="post",
    )(o_f, o_b, gg, o_att, x2, mod, g_gla, g_pm, g_pf, wout, wrh, wrl, br)


def _route_body(ti_ref, rk_ref, cnt_ref, pos_ref, blk_ref):
    cnt = cnt_ref[...]
    padded = jnp.floor((cnt + (MOE_BM - 1)) * (1.0 / MOE_BM)) * MOE_BM
    starts, ends = [], []
    acc = jnp.zeros((1, LANES), F32)
    for e in range(N_EXPERTS):
        starts.append(acc)
        acc = acc + padded[e:e + 1, :]
        ends.append(acc)
    ti = ti_ref[...]
    off = jnp.zeros(ti.shape, F32)
    for e in range(N_EXPERTS):
        off = jnp.where(ti == e, starts[e][:, 0:1], off)
    pos_ref[...] = rk_ref[...] + off.astype(I32)

    def owner_of(row):
        n_le = jnp.zeros(row.shape, I32)
        for e in range(N_EXPERTS):
            n_le = n_le + jnp.where(ends[e][:, 0:1] <= row, 1, 0)
        return jnp.minimum(n_le, N_EXPERTS - 1)

    block_start = lax.broadcasted_iota(I32, (1, MOE_NB_PAD), 1).astype(F32) * MOE_BM
    owner = owner_of(block_start)
    nxt = jnp.zeros((1, MOE_NB_PAD), I32)
    nxt_blk = jnp.zeros((1, MOE_NB_PAD), I32)
    for e in range(N_EXPERTS):
        end_e = ends[e][:, 0:1]
        nxt = jnp.where(owner == e, jnp.where(end_e < acc[:, 0:1], owner_of(end_e), -1), nxt)
        nxt_blk = jnp.where(owner == e, (end_e * (1.0 / MOE_BM)).astype(I32), nxt_blk)
    used = jnp.broadcast_to((acc[:, 0:1] * (1.0 / MOE_BM)).astype(I32), (1, MOE_NB_PAD))
    blk_ref[...] = jnp.concatenate([owner, nxt, used, nxt_blk, jnp.zeros((SUBLANES - 4, MOE_NB_PAD), I32)], axis=0)


def _route(top_i, rank, counts):
    return pl.pallas_call(
        _route_body,
        out_shape=[
            jax.ShapeDtypeStruct(top_i.shape, I32),
            jax.ShapeDtypeStruct((SUBLANES, MOE_NB_PAD), I32),
        ],
        compiler_params=pltpu.CompilerParams(vmem_limit_bytes=32 * 1024 * 1024),
        name="route",
    )(top_i, rank, counts)


def _sc_workers():
    info = plsc.get_sparse_core_info()
    return info.num_cores, info.num_subcores, info.num_lanes


def _sc_gather_loop(table_hbm, out_hbm, idx_v, base, chunks, bufs, gather_sems, write_sems):
    window = SC_GATHER_WINDOW
    ring = len(bufs)

    def fetch(c, b):
        return pltpu.make_async_copy(table_hbm.at[idx_v.at[pl.ds(c * window, window)]], bufs[b], gather_sems[b])

    def flush(c, b):
        return pltpu.make_async_copy(bufs[b], out_hbm.at[pl.ds(base + c * window, window)], write_sems[b])

    for b in range(ring):
        fetch(b, b).start()

    @pl.loop(0, chunks, step=ring)
    def _(c0):
        for b in range(ring):
            c = c0 + b
            fetch(c, b).wait()
            flush(c, b).start()

            @pl.when(c + ring < chunks)
            def _():
                flush(c, b).wait()
                fetch(c + ring, b).start()

    for b in range(ring):
        flush(chunks - ring + b, b).wait()


def _sc_source_rows(pos_flat, n_rows):
    cores, subcores, lanes = _sc_workers()
    workers = cores * subcores
    per_worker = n_rows // workers
    n_assign = pos_flat.shape[0]
    scan = SC_SCAN_CHUNK
    assert per_worker * workers == n_rows and per_worker % lanes == 0
    assert n_assign % scan == 0 and scan % lanes == 0
    mesh = plsc.VectorSubcoreMesh(core_axis_name="core", subcore_axis_name="subcore")

    @functools.partial(
        pl.kernel,
        out_type=jax.ShapeDtypeStruct((n_rows,), I32),
        mesh=mesh,
        scratch_types=[pltpu.VMEM((per_worker,), I32), pltpu.VMEM((scan,), I32)],
        compiler_params=pltpu.CompilerParams(needs_layout_passes=False),
        name="sc_source_rows",
    )
    def invert(pos_hbm, out_hbm, src_v, pos_v):
        wid = lax.axis_index("subcore") * cores + lax.axis_index("core")
        base = wid * per_worker
        lane = lax.iota(I32, lanes)

        @pl.loop(0, per_worker, step=lanes)
        def _(j):
            src_v[pl.ds(j, lanes)] = (base + j + lane) & (TOKENS - 1)

        @pl.loop(0, n_assign, step=scan)
        def _(a0):
            pltpu.sync_copy(pos_hbm.at[pl.ds(a0, scan)], pos_v)

            @plsc.parallel_loop(0, scan, step=lanes, unroll=SC_SCAN_UNROLL)
            def _(j):
                rel = pos_v[pl.ds(j, lanes)] - base
                mine = (rel >= 0) & (rel < per_worker)
                tok = (a0 + j + lane) & (TOKENS - 1)
                plsc.store_scatter(src_v, [jnp.where(mine, rel, 0)], tok, mask=mine)

        pltpu.sync_copy(src_v, out_hbm.at[pl.ds(base, per_worker)])

    return invert(pos_flat)


def _sc_gather_rows(table, idx):
    cores, subcores, _ = _sc_workers()
    workers = cores * subcores
    n = idx.shape[0]
    window = SC_GATHER_WINDOW
    per_worker = n // workers
    chunks = per_worker // window
    ring = SC_GATHER_RING
    assert per_worker * workers == n and chunks * window == per_worker and chunks % ring == 0
    row_shape = table.shape[1:]
    mesh = plsc.VectorSubcoreMesh(core_axis_name="core", subcore_axis_name="subcore")

    @functools.partial(
        pl.kernel,
        out_type=jax.ShapeDtypeStruct((n,) + row_shape, table.dtype),
        mesh=mesh,
        scratch_types=[pltpu.VMEM((per_worker,), I32)]
        + [pltpu.VMEM((window,) + row_shape, table.dtype)] * ring
        + [pltpu.SemaphoreType.DMA] * (2 * ring),
        name="sc_gather_rows",
    )
    def gather(table_hbm, idx_hbm, out_hbm, idx_v, *scratch):
        wid = lax.axis_index("subcore") * cores + lax.axis_index("core")
        base = wid * per_worker
        pltpu.sync_copy(idx_hbm.at[pl.ds(base, per_worker)], idx_v)
        _sc_gather_loop(table_hbm, out_hbm, idx_v, base, chunks,
                        scratch[:ring], scratch[ring:2 * ring], scratch[2 * ring:])

    return gather(table, idx)


def _experts_body(first, be_ref, nxt_ref, nxtblk_ref, nu_ref, xs_ref, bgu_ref, bd_ref, wgu_hbm, wd_hbm, *rest):
    ys_ref, wgu_f32, wd_f32, wgu_bf, wd_bf, sems = rest[-6:]
    step = pl.program_id(0)
    end = first + pl.num_programs(0) * MOE_SUB
    bm = MOE_BM
    n_used = nu_ref[0]

    def fetch(e):
        return (pltpu.make_async_copy(wgu_hbm.at[e], wgu_f32, sems.at[0]),
                pltpu.make_async_copy(wd_hbm.at[e], wd_f32, sems.at[1]))

    @pl.when((step == 0) & (first < n_used))
    def _():
        for cp in fetch(be_ref[first]):
            cp.start()

    for sub in range(MOE_SUB):
        i = first + step * MOE_SUB + sub
        e = be_ref[i]
        used = i < n_used
        fresh = (i == first) | (e != be_ref[jnp.maximum(i - 1, 0)])
        rows = pl.ds(sub * bm * PACK_SUB, bm * PACK_SUB)
        xs_sub = xs_ref.at[rows, :]
        ys_sub = ys_ref.at[rows, :]

        @pl.when(used & fresh)
        def _():
            for cp in fetch(e):
                cp.wait()
            wgu_bf[...] = wgu_f32[...].astype(BF16)
            wd_bf[...] = wd_f32[...].astype(BF16)

            @pl.when((nxt_ref[i] >= 0) & (nxtblk_ref[i] < end))
            def _():
                for cp in fetch(nxt_ref[i]):
                    cp.start()

        @pl.when(used)
        def _():
            x = jnp.concatenate(_unpack_rows(xs_sub, bm), axis=1).astype(BF16)
            gu = jnp.dot(x, wgu_bf[...], preferred_element_type=F32) + bgu_ref[pl.ds(e, 1), :]
            gate = jnp.minimum(gu[:, 0:D_FF], SWIGLU_LIMIT)
            up = jnp.clip(gu[:, D_FF:2 * D_FF], -SWIGLU_LIMIT, SWIGLU_LIMIT)
            act = ((up + 1.0) * gate * jax.nn.sigmoid(SWIGLU_ALPHA * gate)).astype(BF16)
            y = jnp.dot(act, wd_bf[...], preferred_element_type=F32) + bd_ref[pl.ds(e, 1), :]
            _pack_rows(ys_sub, y.astype(BF16).astype(F32))

        @pl.when(jnp.logical_not(used))
        def _():
            ys_sub[...] = jnp.zeros((bm * PACK_SUB, LANES), I32)


def _experts(first, part_blocks, blocks, xs_part, w_gate_up, b_gate_up, w_down, b_down, ys_prev):
    rows = MOE_SUB * MOE_BM * PACK_SUB
    d = D_MODEL
    steps = part_blocks // MOE_SUB
    assert steps * MOE_SUB == part_blocks and first % MOE_SUB == 0

    def x_block(i, be, nx, nb, nu):
        last = jnp.maximum((jnp.minimum(nu[0], first + part_blocks) - 1 - first) // MOE_SUB, 0)
        return jnp.minimum(i, last)

    in_specs = [
        pl.BlockSpec((rows, LANES), lambda i, be, nx, nb, nu: (x_block(i, be, nx, nb, nu), 0)),
        pl.BlockSpec(b_gate_up.shape, lambda i, be, nx, nb, nu: (0, 0)),
        pl.BlockSpec(b_down.shape, lambda i, be, nx, nb, nu: (0, 0)),
        pl.BlockSpec(memory_space=pl.ANY),
        pl.BlockSpec(memory_space=pl.ANY),
    ]
    operands = [blocks[0, :MOE_NB], blocks[1, :MOE_NB], blocks[3, :MOE_NB], blocks[2, :1],
                xs_part, b_gate_up, b_down, w_gate_up, w_down]
    aliases = {}
    if ys_prev is not None:
        in_specs.append(pl.BlockSpec(memory_space=pl.ANY))
        aliases = {len(operands): 0}
        operands.append(ys_prev)
    grid_spec = pltpu.PrefetchScalarGridSpec(
        num_scalar_prefetch=4,
        grid=(steps,),
        in_specs=in_specs,
        out_specs=pl.BlockSpec((rows, LANES), lambda i, be, nx, nb, nu: (first // MOE_SUB + i, 0)),
        scratch_shapes=[
            pltpu.VMEM((d, 2 * D_FF), F32), pltpu.VMEM((D_FF, d), F32),
            pltpu.VMEM((d, 2 * D_FF), BF16), pltpu.VMEM((D_FF, d), BF16),
            pltpu.SemaphoreType.DMA((2,)),
        ],
    )
    return pl.pallas_call(
        functools.partial(_experts_body, first),
        grid_spec=grid_spec,
        out_shape=jax.ShapeDtypeStruct((MOE_ROWS * PACK_SUB, LANES), I32),
        input_output_aliases=aliases,
        compiler_params=_params(("arbitrary",), 48),
        name="experts",
    )(*operands)


def _combine_body(gates_ref, x1_ref, mod_ref, gpost_ref, y0_ref, y1_ref, y2_ref, y3_ref, *rest):
    o_ref = rest[-1]
    tm = TM_COMB
    gates = gates_ref[...]
    y_hi = jnp.zeros((tm, PACK_COLS), F32)
    y_lo = jnp.zeros((tm, PACK_COLS), F32)
    for k, yk_ref in enumerate((y0_ref, y1_ref, y2_ref, y3_ref)):
        hi, lo = _unpack_rows(yk_ref, tm)
        y_hi = y_hi + hi * gates[:, k:k + 1]
        y_lo = y_lo + lo * gates[:, k:k + 1]
    y = jnp.concatenate([y_hi, y_lo], axis=1)
    gate2 = mod_ref[5:6, :]
    o_ref[...] = x1_ref[...] + gate2 * _rms(y, gpost_ref[...])


def _combine(part, gates_t, x1, mod, g_post, y4_part, out_prev):
    t, d = x1.shape
    tm = TM_COMB
    tiles = t // COMB_PARTS // tm
    tile0 = part * tiles
    tiles_per_seq = SEQ // tm

    def slab(k):
        return pl.BlockSpec((tm * PACK_SUB, LANES), lambda i: (k * tiles + i, 0))

    in_specs = [
        pl.BlockSpec((tm, TOP_K), lambda i: (tile0 + i, 0)),
        pl.BlockSpec((tm, d), lambda i: (tile0 + i, 0)),
        pl.BlockSpec((None, 6, d), lambda i: ((tile0 + i) // tiles_per_seq, 0, 0)),
        pl.BlockSpec(g_post.shape, lambda i: (0, 0)),
        slab(0), slab(1), slab(2), slab(3),
    ]
    operands = [gates_t, x1, mod, g_post, y4_part, y4_part, y4_part, y4_part]
    aliases = {}
    if out_prev is not None:
        in_specs.append(pl.BlockSpec(memory_space=pl.ANY))
        aliases = {len(operands): 0}
        operands.append(out_prev)
    return pl.pallas_call(
        _combine_body,
        grid=(tiles,),
        in_specs=in_specs,
        out_specs=pl.BlockSpec((tm, d), lambda i: (tile0 + i, 0)),
        out_shape=jax.ShapeDtypeStruct((t, d), F32),
        input_output_aliases=aliases,
        compiler_params=_params(("arbitrary",), 48),
        name="combine",
    )(*operands)


def _rotary_tables():
    half = ROT_DIM // 2
    inv_freq = ROPE_THETA ** (-2.0 * np.arange(half, dtype=np.float32) / ROT_DIM)
    ang = np.arange(SEQ, dtype=np.float32)[:, None] * inv_freq[None, :].astype(np.float32)
    cos, sin = np.cos(ang), np.sin(ang)
    ones = np.ones((SEQ, ATT_HEAD_DIM - ROT_DIM), np.float32)
    zeros = np.zeros((SEQ, ATT_HEAD_DIM - ROT_DIM), np.float32)
    zh = np.zeros((SEQ, half), np.float32)
    reps = LANES // ATT_HEAD_DIM
    rc = np.tile(np.concatenate([cos, cos, ones], axis=1), (1, reps))
    rm = np.tile(np.concatenate([-sin, zh, zeros], axis=1), (1, reps))
    rp = np.tile(np.concatenate([zh, sin, zeros], axis=1), (1, reps))
    return tuple(jnp.asarray(t, F32) for t in (rc, rm, rp))


def _mixer_inputs(w_in, w_gk_fwd, b_gk_fwd, w_gk_bwd, b_gk_bwd):
    hk = GLA_HEADS * GLA_DK
    hv = GLA_HEADS * GLA_DV
    w = w_in[0]
    o_lr = 2 * hk + 2 * hv
    o_aq = o_lr + 2 * GLA_RANK
    o_ak = o_aq + ATT_Q_HEADS * ATT_HEAD_DIM
    o_av = o_ak + ATT_KV_HEADS * ATT_HEAD_DIM
    hd = ATT_HEAD_DIM
    wa = w[:, :o_lr].astype(BF16)
    wlr = w[:, o_lr:o_aq].astype(BF16)
    dup = lambda m: jnp.concatenate([m[:, g * hd:(g + 1) * hd] for g in range(ATT_KV_HEADS) for _ in range(2)], axis=1)
    wb = jnp.concatenate([w[:, o_aq:o_ak], dup(w[:, o_ak:o_av])], axis=1).astype(BF16)
    wvt = dup(w[:, o_av:o_av + ATT_KV_HEADS * hd]).T.astype(BF16)
    zr = jnp.zeros((GLA_RANK, hk), F32)
    wgk = jnp.concatenate([jnp.concatenate([w_gk_fwd[0], zr], axis=1),
                           jnp.concatenate([zr, w_gk_bwd[0]], axis=1)], axis=0).astype(BF16)
    bgk = jnp.concatenate([b_gk_fwd[0], b_gk_bwd[0]])[None, :]
    return (wa, wlr, wgk, bgk, wb, wvt) + _rotary_tables()


def kernel(x, c, w_ada, b_ada, g_pre_mix, g_post_mix, w_in, w_gk_fwd, b_gk_fwd, w_gk_bwd, b_gk_bwd, g_gla_out,
           attn_sink, w_out, g_pre_ffn, g_post_ffn, w_router, b_router, w_gate_up, b_gate_up, w_down, b_down):
    assert x.shape == (BATCH, SEQ, D_MODEL) and w_ada.shape[0] == 1
    d = D_MODEL
    x2 = x.reshape(TOKENS, d)

    c_pad = jnp.pad(c, ((0, SUBLANES - BATCH), (0, 0)))
    mod = _ada(c_pad, w_ada[0], b_ada)[:BATCH].reshape(BATCH, 6, d)

    mixer_in = _mixer_inputs(w_in, w_gk_fwd, b_gk_fwd, w_gk_bwd, b_gk_bwd)
    q, k, v, gg, laf, lab, aq, ak2, avt = _inproj(x2, mod, g_pre_mix, *mixer_in)
    o_f, o_b = _gla(q, k, v, laf, lab)
    o_att = _attn(attn_sink[0], aq, ak2, avt)

    wr_t = w_router[0].T
    wrh = wr_t.astype(BF16)
    wrl = (wr_t - wrh.astype(F32)).astype(BF16)
    x1, h2_tiles, top_i, gates, rank, counts = _post(
        o_f, o_b, gg, o_att, x2, mod, g_gla_out, g_post_mix, g_pre_ffn, w_out[0].astype(BF16), wrh, wrl,
        b_router[0][:, None])

    pos, blocks = _route(top_i, rank, counts)

    src = _sc_source_rows(pos.reshape(TOP_K * TOKENS), MOE_ROWS)
    h2_rows = h2_tiles.reshape(TOKENS, PACK_SUB, LANES)
    ys = None
    first = 0
    for part_blocks in MOE_PART_BLOCKS:
        row0, n_rows = first * MOE_BM, part_blocks * MOE_BM
        xs_p = _sc_gather_rows(h2_rows, src[row0:row0 + n_rows])
        ys = _experts(first, part_blocks, blocks, xs_p.reshape(n_rows * PACK_SUB, LANES),
                      w_gate_up[0], b_gate_up[0], w_down[0], b_down[0], ys)
        first += part_blocks
    assert first == MOE_NB

    ys_rows = ys.reshape(MOE_ROWS, PACK_SUB, LANES)
    gates_t = gates.T
    part_tokens = TOKENS // COMB_PARTS
    out = None
    for p in range(COMB_PARTS):
        idx = pos[:, p * part_tokens:(p + 1) * part_tokens].reshape(TOP_K * part_tokens)
        y4_p = _sc_gather_rows(ys_rows, idx).reshape(TOP_K * part_tokens * PACK_SUB, LANES)
        out = _combine(p, gates_t, x1, mod, g_post_ffn, y4_p, out)
    return out.reshape(BATCH, SEQ, d)
```

```python
import functools

import jax
import jax.numpy as jnp
import numpy as np
from jax import lax
from jax.experimental import pallas as pl
from jax.experimental.pallas import tpu as pltpu
from jax.experimental.pallas import tpu_sc as plsc

F32 = jnp.float32
BF16 = jnp.bfloat16
I32 = jnp.int32

D_MODEL = 1024
BATCH = 2
SEQ = 8192
TOKENS = BATCH * SEQ
GLA_HEADS = 4
GLA_DV = 128
GLA_DK = 64
GLA_RANK = 16
GLA_GATE_NORMALIZER = 16.0
GLA_CHUNK = 64
ATT_Q_HEADS = 8
ATT_KV_HEADS = 2
ATT_HEAD_DIM = 64
ATT_WINDOW = 128
ATT_BLOCK = 128
ROT_DIM = 16
ROPE_THETA = 500000.0
N_EXPERTS = 32
TOP_K = 4
D_FF = 1024
SWIGLU_LIMIT = 7.0
SWIGLU_ALPHA = 1.702
NORM_EPS = 1e-6
NEG_INF = -1e30
LOG2_E = 1.4426950408889634

LANES = 128
SUBLANES = 8
PACK_COLS = D_MODEL // 2
PACK_SUB = PACK_COLS // LANES

TM_IN = 1024
IN_SUB = 4
GLA_GROUP = 16
ATT_GROUP = 16
TM_POST = 1024
POST_SUB = 4
MOE_BM = 256
MOE_ROWS = TOKENS * TOP_K + N_EXPERTS * MOE_BM
MOE_NB = MOE_ROWS // MOE_BM
MOE_SUB = 4
MOE_PART_BLOCKS = (32, 128, 128)
MOE_NB_PAD = ((MOE_NB + LANES - 1) // LANES) * LANES
SC_SCAN_CHUNK = 4096
SC_SCAN_UNROLL = 8
TM_COMB = 512
SC_GATHER_WINDOW = 32
SC_GATHER_RING = 4

NT_DIMS = (((1,), (1,)), ((), ()))
TN_DIMS = (((0,), (0,)), ((), ()))


def _params(semantics, vmem_mib):
    return pltpu.CompilerParams(dimension_semantics=semantics, vmem_limit_bytes=vmem_mib * 1024 * 1024)


def _rms(x, g):
    return x * lax.rsqrt(jnp.mean(x * x, axis=-1, keepdims=True) + NORM_EPS) * g


def _silu(x):
    return x * jax.nn.sigmoid(x)


def _pack_rows(ref, v):
    m = v.shape[0]
    bits = lax.bitcast_convert_type(v, jnp.uint32)
    word = lax.bitcast_convert_type(bits[:, :PACK_COLS] | (bits[:, PACK_COLS:] >> 16), I32)
    for s in range(PACK_SUB):
        ref[pl.ds(s, m, stride=PACK_SUB), :] = word[:, s * LANES:(s + 1) * LANES]


def _unpack_rows(ref, m):
    word = jnp.concatenate([ref[pl.ds(s, m, stride=PACK_SUB), :] for s in range(PACK_SUB)], axis=1)
    bits = lax.bitcast_convert_type(word, jnp.uint32)
    hi = lax.bitcast_convert_type(bits & jnp.uint32(0xFFFF0000), F32)
    lo = lax.bitcast_convert_type(bits << 16, F32)
    return hi, lo


def _ada_body(c_ref, w_ref, b_ref, o_ref):
    ca = _silu(c_ref[...]).astype(BF16)
    o_ref[...] = jnp.dot(ca, w_ref[...].astype(BF16), preferred_element_type=F32) + b_ref[...]


def _ada(c_pad, w_ada, b_ada):
    d = D_MODEL
    return pl.pallas_call(
        _ada_body,
        grid=(6,),
        in_specs=[
            pl.BlockSpec((SUBLANES, d), lambda j: (0, 0)),
            pl.BlockSpec((d, d), lambda j: (0, j)),
            pl.BlockSpec((1, d), lambda j: (0, j)),
        ],
        out_specs=pl.BlockSpec((SUBLANES, d), lambda j: (0, j)),
        out_shape=jax.ShapeDtypeStruct((SUBLANES, 6 * d), F32),
        compiler_params=_params(("arbitrary",), 32),
        name="ada",
    )(c_pad, w_ada, b_ada)


def _rotary(x, cos_t, msin_t, psin_t):
    width = x.shape[1]
    reps = width // LANES
    c = jnp.concatenate([cos_t] * reps, axis=1)
    m = jnp.concatenate([msin_t] * reps, axis=1)
    p = jnp.concatenate([psin_t] * reps, axis=1)
    half = ROT_DIM // 2
    return x * c + pltpu.roll(x, width - half, 1) * m + pltpu.roll(x, half, 1) * p


def _inproj_body(x_ref, mod_ref, g_ref, wa_ref, wlr_ref, wgk_ref, bgk_ref, wb_ref, wvt_ref, rc_ref, rm_ref, rp_ref,
                 q_ref, k_ref, v_ref, gg_ref, laf_ref, lab_ref, aq_ref, ak_ref, avt_ref):
    shift = mod_ref[0:1, :]
    scale = mod_ref[1:2, :]
    hk = GLA_HEADS * GLA_DK
    hv = GLA_HEADS * GLA_DV
    nq = ATT_Q_HEADS * ATT_HEAD_DIM
    nk = 2 * ATT_KV_HEADS * ATT_HEAD_DIM
    sub = x_ref.shape[0] // IN_SUB
    subs = [slice(s * sub, (s + 1) * sub) for s in range(IN_SUB)]

    gain = g_ref[...] * (1.0 + scale)

    def hidden(rows):
        return (_rms(x_ref[rows, :], gain) + shift).astype(BF16)

    def project(h):
        return (jnp.dot(h, wa_ref[...], preferred_element_type=F32),
                jnp.dot(h, wlr_ref[...], preferred_element_type=F32),
                jnp.dot(h, wb_ref[...], preferred_element_type=F32),
                lax.dot_general(wvt_ref[...], h, NT_DIMS, preferred_element_type=F32))

    def finish(rows, pa, plr, pb, pvt):
        q_ref[rows, :] = pa[:, 0:hk] * (GLA_DK ** -0.5)
        k_ref[rows, :] = pa[:, hk:2 * hk]
        v_ref[rows, :] = pa[:, 2 * hk:2 * hk + hv].astype(BF16)
        gg_ref[rows, :] = pa[:, 2 * hk + hv:2 * hk + 2 * hv]
        gk = jnp.dot(plr.astype(BF16), wgk_ref[...], preferred_element_type=F32) + bgk_ref[...]
        la = (jnp.minimum(gk, 0.0) - jnp.log1p(jnp.exp(-jnp.abs(gk)))) * (1.0 / GLA_GATE_NORMALIZER)
        laf_ref[rows, :] = la[:, 0:hk]
        lab_ref[rows, :] = la[:, hk:2 * hk]
        rc, rm, rp = rc_ref[rows, :], rm_ref[rows, :], rp_ref[rows, :]
        aq_ref[rows, :] = (_rotary(pb[:, 0:nq], rc, rm, rp) * (ATT_HEAD_DIM ** -0.5 * LOG2_E)).astype(BF16)
        ak_ref[rows, :] = _rotary(pb[:, nq:nq + nk], rc, rm, rp).astype(BF16)
        avt_ref[:, rows] = pvt.astype(BF16)

    hs = [hidden(rows) for rows in subs]
    ps = [project(h) for h in hs]
    for rows, p in zip(subs, ps):
        finish(rows, *p)


def _inproj(x2, mod, g_pre, wa, wlr, wgk, bgk, wb, wvt, rc, rm, rp):
    t, d = x2.shape
    tm = TM_IN
    tiles_per_seq = SEQ // tm
    hk = GLA_HEADS * GLA_DK
    hv = GLA_HEADS * GLA_DV
    nq = ATT_Q_HEADS * ATT_HEAD_DIM
    nk = 2 * ATT_KV_HEADS * ATT_HEAD_DIM

    def full(a):
        return pl.BlockSpec(a.shape, lambda i: (0,) * a.ndim)

    def rows(w):
        return pl.BlockSpec((tm, w), lambda i: (i, 0))

    def table():
        return pl.BlockSpec((tm, LANES), lambda i: (i % tiles_per_seq, 0))

    out_widths = [(hk, F32), (hk, F32), (hv, BF16), (hv, F32), (hk, F32), (hk, F32), (nq, BF16), (nk, BF16)]
    return pl.pallas_call(
        _inproj_body,
        grid=(t // tm,),
        in_specs=[
            rows(d),
            pl.BlockSpec((None, 6, d), lambda i: (i // tiles_per_seq, 0, 0)),
            full(g_pre), full(wa), full(wlr), full(wgk), full(bgk), full(wb), full(wvt),
            table(), table(), table(),
        ],
        out_specs=[rows(w) for w, _ in out_widths] + [pl.BlockSpec((nk, tm), lambda i: (0, i))],
        out_shape=[jax.ShapeDtypeStruct((t, w), dt) for w, dt in out_widths] + [jax.ShapeDtypeStruct((nk, t), BF16)],
        compiler_params=_params(("arbitrary",), 56),
        name="inproj",
    )(x2, mod, g_pre, wa, wlr, wgk, bgk, wb, wvt, rc, rm, rp)


def _gla_body(qf_ref, kf_ref, vf_ref, laf_ref, qb_ref, kb_ref, vb_ref, lab_ref, of_ref, ob_ref, sf_ref, sb_ref):
    @pl.when(pl.program_id(1) == 0)
    def _():
        sf_ref[...] = jnp.zeros_like(sf_ref)
        sb_ref[...] = jnp.zeros_like(sb_ref)

    c = GLA_CHUNK
    r_i = lax.broadcasted_iota(I32, (c, c), 0)
    c_i = lax.broadcasted_iota(I32, (c, c), 1)
    lower = c_i <= r_i
    upper = c_i >= r_i
    cum_f = jnp.where(lower, 1.0, 0.0).astype(BF16)
    cum_b = jnp.where(upper, 1.0, 0.0).astype(BF16)
    lane = lax.broadcasted_iota(I32, (1, LANES), 1)
    head_masks = (lane < GLA_DK, lane >= GLA_DK)

    fwd = [(qf_ref, kf_ref, laf_ref, vf_ref, of_ref, slice(g * c, (g + 1) * c), cum_f, lower, c - 1, c // 2 - 1)
           for g in range(GLA_GROUP)]
    bwd = [(qb_ref, kb_ref, lab_ref, vb_ref, ob_ref, slice(g * c, (g + 1) * c), cum_b, upper, 0, c // 2)
           for g in reversed(range(GLA_GROUP))]
    heads = range(GLA_HEADS)
    pair = [slice((h // 2) * LANES, (h // 2 + 1) * LANES) for h in heads]
    vcols = [slice(h * GLA_DV, (h + 1) * GLA_DV) for h in heads]

    def stage1(item):
        q_ref, k_ref, la_ref, v_ref, o_ref, rows, cum, tri, i_last, i_mid = item
        la = la_ref[rows, :]
        hi = la.astype(BF16)
        lo = (la - hi.astype(F32)).astype(BF16)
        b = jnp.dot(cum, hi, preferred_element_type=F32) + jnp.dot(cum, lo, preferred_element_type=F32)
        b_last = b[i_last:i_last + 1, :]
        b_mid = b[i_mid:i_mid + 1, :]
        q, k = q_ref[rows, :], k_ref[rows, :]
        return (q * jnp.exp(b - b_mid), (k * jnp.exp(b_mid - b)).astype(BF16), q * jnp.exp(b),
                (k * jnp.exp(b_last - b)).astype(BF16), jnp.exp(b_last))

    def stage2(item, pre):
        v_ref, rows, tri = item[3], item[5], item[7]
        qs, ks, qi, kst, decay = pre
        out = []
        for h in heads:
            mask = head_masks[h % 2]
            qs_h = jnp.where(mask, qs[:, pair[h]], 0.0).astype(BF16)
            sc = lax.dot_general(qs_h, ks[:, pair[h]], NT_DIMS, preferred_element_type=F32)
            v_h = v_ref[rows, vcols[h]]
            kv = lax.dot_general(v_h, kst[:, pair[h]], TN_DIMS, preferred_element_type=F32)
            out.append((jnp.where(tri, sc, 0.0).astype(BF16), kv,
                        jnp.where(mask, qi[:, pair[h]], 0.0).astype(BF16), v_h))
        return out

    def run(items, s_ref):
        pre = [stage1(it) for it in items]
        mid = [stage2(it, p) for it, p in zip(items, pre)]
        states = [s_ref[h] for h in heads]
        for it, p, m in zip(items, pre, mid):
            o_ref, rows, decay = it[4], it[5], p[4]
            for h in heads:
                sc, kv, qi_h, v_h = m[h]
                o = jnp.dot(sc, v_h, preferred_element_type=F32)
                o = o + lax.dot_general(qi_h, states[h].astype(BF16), NT_DIMS, preferred_element_type=F32)
                o_ref[rows, vcols[h]] = o
                states[h] = states[h] * decay[:, pair[h]] + kv
        for h in heads:
            s_ref[h] = states[h]

    run(fwd, sf_ref)
    run(bwd, sb_ref)


def _gla(q, k, v, laf, lab):
    t = q.shape[0]
    rows = GLA_GROUP * GLA_CHUNK
    ng = SEQ // rows
    hk = GLA_HEADS * GLA_DK
    hv = GLA_HEADS * GLA_DV

    def fwd(w):
        return pl.BlockSpec((rows, w), lambda b, n: (b * ng + n, 0))

    def bwd(w):
        return pl.BlockSpec((rows, w), lambda b, n: (b * ng + ng - 1 - n, 0))

    return pl.pallas_call(
        _gla_body,
        grid=(BATCH, ng),
        in_specs=[fwd(hk), fwd(hk), fwd(hv), fwd(hk), bwd(hk), bwd(hk), bwd(hv), bwd(hk)],
        out_specs=[fwd(hv), bwd(hv)],
        out_shape=[jax.ShapeDtypeStruct((t, hv), F32)] * 2,
        scratch_shapes=[pltpu.VMEM((GLA_HEADS, GLA_DV, 2 * GLA_DK), F32)] * 2,
        compiler_params=_params(("arbitrary", "arbitrary"), 32),
        name="gla",
    )(q, k, v, laf, q, k, v, lab)


def _attn_body(sink_ref, q_ref, kp_ref, kc_ref, kn_ref, vp_ref, vc_ref, vn_ref, o_ref):
    step = pl.program_id(1)
    last = pl.num_programs(1) - 1
    qb = ATT_BLOCK
    hd = ATT_HEAD_DIM
    k_all = jnp.concatenate([kp_ref[...], kc_ref[...], kn_ref[...]], axis=0)
    vt_all = jnp.concatenate([vp_ref[...], vc_ref[...], vn_ref[...]], axis=1)
    lane = lax.broadcasted_iota(I32, (1, LANES), 1)
    lo = lane < hd
    j_k = lax.broadcasted_iota(I32, (3 * qb, qb), 0)
    i_q = lax.broadcasted_iota(I32, (3 * qb, qb), 1)
    band = jnp.abs(j_k - qb - i_q) <= ATT_WINDOW
    sinks = [jnp.concatenate([jnp.full((1, qb), sink_ref[4 * g + r] * LOG2_E, F32) for r in range(4)], axis=1)
             for g in range(ATT_KV_HEADS)]
    ones_rows = jnp.ones((SUBLANES, 3 * qb), BF16)
    work = [(j, g) for j in range(ATT_GROUP) for g in range(ATT_KV_HEADS)]

    def scores(j, g):
        valid = band
        if j == 0:
            valid = valid & ((j_k >= qb) | (step > 0))
        if j == ATT_GROUP - 1:
            valid = valid & ((j_k < 2 * qb) | (step < last))
        valid4 = jnp.concatenate([valid] * 4, axis=1)
        rows = slice(j * qb, (j + 1) * qb)
        kg = k_all[j * qb:(j + 3) * qb, g * LANES:(g + 1) * LANES]
        qa = q_ref[rows, (2 * g) * LANES:(2 * g + 1) * LANES]
        qc = q_ref[rows, (2 * g + 1) * LANES:(2 * g + 2) * LANES]
        zero = jnp.zeros_like(qa)
        lhs = jnp.concatenate([jnp.where(lo, qa, zero), jnp.where(lo, zero, qa),
                               jnp.where(lo, qc, zero), jnp.where(lo, zero, qc)], axis=0)
        st = lax.dot_general(kg, lhs, NT_DIMS, preferred_element_type=F32)
        return jnp.concatenate([jnp.where(valid4[0:qb], st[0:qb], NEG_INF), st[qb:2 * qb],
                                jnp.where(valid4[2 * qb:3 * qb], st[2 * qb:3 * qb], NEG_INF)], axis=0)

    def softmax(st, g):
        sink = sinks[g]
        m = jnp.maximum(jnp.max(st, axis=0, keepdims=True), sink)
        return jnp.exp2(st - m).astype(BF16), jnp.exp2(sink - m)

    def output(j, g, p, p_sink):
        rows = slice(j * qb, (j + 1) * qb)
        vgt = vt_all[g * LANES:(g + 1) * LANES, j * qb:(j + 3) * qb]
        res = jnp.dot(jnp.concatenate([vgt, ones_rows], axis=0), p, preferred_element_type=F32)
        ot = res[0:LANES] * (1.0 / (res[LANES:LANES + 1] + p_sink))
        pair_a = jnp.concatenate([ot[0:hd, 0:qb], ot[hd:2 * hd, qb:2 * qb]], axis=0)
        pair_c = jnp.concatenate([ot[0:hd, 2 * qb:3 * qb], ot[hd:2 * hd, 3 * qb:4 * qb]], axis=0)
        o_ref[rows, (2 * g) * LANES:(2 * g + 1) * LANES] = pair_a.T.astype(o_ref.dtype)
        o_ref[rows, (2 * g + 1) * LANES:(2 * g + 2) * LANES] = pair_c.T.astype(o_ref.dtype)

    s_all = [scores(j, g) for j, g in work]
    p_all = [softmax(st, g) for st, (j, g) in zip(s_all, work)]
    for (j, g), (p, p_sink) in zip(work, p_all):
        output(j, g, p, p_sink)


def _attn(sink, aq, ak2, avt):
    t = aq.shape[0]
    qb = ATT_BLOCK
    nb = SEQ // qb
    steps = nb // ATT_GROUP
    nq = ATT_Q_HEADS * ATT_HEAD_DIM
    nk = 2 * ATT_KV_HEADS * ATT_HEAD_DIM

    def edge_block(b, n, shift):
        return b * nb + jnp.clip(n * ATT_GROUP + shift, 0, nb - 1)

    def k_edge(shift):
        return pl.BlockSpec((qb, nk), lambda b, n: (edge_block(b, n, shift), 0))

    def v_edge(shift):
        return pl.BlockSpec((nk, qb), lambda b, n: (0, edge_block(b, n, shift)))

    def group(w):
        return pl.BlockSpec((ATT_GROUP * qb, w), lambda b, n: (b * steps + n, 0))

    v_group = pl.BlockSpec((nk, ATT_GROUP * qb), lambda b, n: (0, b * steps + n))
    return pl.pallas_call(
        _attn_body,
        grid=(BATCH, steps),
        in_specs=[
            pl.BlockSpec(memory_space=pltpu.SMEM),
            group(nq),
            k_edge(-1), group(nk), k_edge(ATT_GROUP), v_edge(-1), v_group, v_edge(ATT_GROUP),
        ],
        out_specs=group(nq),
        out_shape=jax.ShapeDtypeStruct((t, nq), BF16),
        compiler_params=_params(("arbitrary", "arbitrary"), 48),
        name="attn",
    )(sink, aq, ak2, ak2, ak2, avt, avt, avt)


def _post_body(of_ref, ob_ref, gg_ref, oa_ref, x_ref, mod_ref, ggla_ref, gpm_ref, gpf_ref, wout_ref,
               wrh_ref, wrl_ref, br_ref,
               x1_ref, h2_ref, ti_ref, gt_ref, rk_ref, cnt_ref, base_ref):
    tm = TM_POST

    @pl.when(pl.program_id(0) == 0)
    def _():
        base_ref[...] = jnp.zeros_like(base_ref)

    gain1 = mod_ref[2:3, :] * gpm_ref[...]
    shift2 = mod_ref[3:4, :]
    gain2 = gpf_ref[...] * (1.0 + mod_ref[4:5, :])
    sub = tm // POST_SUB
    subs = [slice(s * sub, (s + 1) * sub) for s in range(POST_SUB)]

    def mixer_out(rows):
        og = of_ref[rows, :] + ob_ref[rows, :]
        gg = gg_ref[rows, :]
        parts = []
        for h in range(GLA_HEADS):
            cols = slice(h * GLA_DV, (h + 1) * GLA_DV)
            parts.append((_rms(og[:, cols], ggla_ref[...]) * _silu(gg[:, cols])).astype(BF16))
        return jnp.concatenate(parts + [oa_ref[rows, :]], axis=1)

    def ffn_in(s, y):
        rows = subs[s]
        x1 = x_ref[rows, :] + _rms(y, gain1)
        x1_ref[rows, :] = x1
        h2 = _rms(x1, gain2) + shift2
        hi = h2.astype(BF16)
        hi32 = hi.astype(F32)
        _pack_rows(h2_ref.at[pl.ds(s * sub * PACK_SUB, sub * PACK_SUB), :], hi32)
        return hi, (h2 - hi32).astype(BF16)

    o_subs = [mixer_out(rows) for rows in subs]
    y_subs = [jnp.dot(o, wout_ref[...], preferred_element_type=F32) for o in o_subs]
    split = [ffn_in(s, y) for s, y in enumerate(y_subs)]
    h2_hi = jnp.concatenate([hi for hi, _ in split], axis=0)
    h2_lo = jnp.concatenate([lo for _, lo in split], axis=0)

    wrh = wrh_ref[...]
    logits = (lax.dot_general(wrh, h2_hi, NT_DIMS, preferred_element_type=F32)
              + lax.dot_general(wrh, h2_lo, NT_DIMS, preferred_element_type=F32)
              + lax.dot_general(wrl_ref[...], h2_hi, NT_DIMS, preferred_element_type=F32)
              + br_ref[...])
    e_iota = lax.broadcasted_iota(I32, (N_EXPERTS, tm), 0)
    idxs, vals = [], []
    work = logits
    for _ in range(TOP_K):
        m = jnp.max(work, axis=0, keepdims=True)
        idx = jnp.min(jnp.where(work == m, e_iota, N_EXPERTS), axis=0, keepdims=True)
        idxs.append(idx)
        vals.append(m)
        work = jnp.where(e_iota == idx, -jnp.inf, work)
    exps = [jnp.exp(v - vals[0]) for v in vals]
    inv = 1.0 / (exps[0] + exps[1] + exps[2] + exps[3])
    gt_ref[...] = jnp.concatenate([e * inv for e in exps], axis=0)
    ti_ref[...] = jnp.concatenate(idxs, axis=0)

    onehots = [e_iota == idx for idx in idxs]
    member = jnp.where(onehots[0] | onehots[1] | onehots[2] | onehots[3], 1.0, 0.0)
    t_row = lax.broadcasted_iota(I32, (tm, tm), 0)
    t_col = lax.broadcasted_iota(I32, (tm, tm), 1)
    strict = jnp.where(t_row < t_col, 1.0, 0.0).astype(BF16)
    before = base_ref[...] + jnp.dot(member.astype(BF16), strict, preferred_element_type=F32)
    rk_ref[...] = jnp.concatenate(
        [jnp.sum(jnp.where(oh, before, 0.0), axis=0, keepdims=True) for oh in onehots], axis=0).astype(I32)
    new_base = base_ref[...] + jnp.sum(member, axis=1, keepdims=True)
    base_ref[...] = new_base
    cnt_ref[...] = jnp.broadcast_to(new_base, cnt_ref.shape)


def _post(o_f, o_b, gg, o_att, x2, mod, g_gla, g_pm, g_pf, wout, wrh, wrl, br):
    t, d = x2.shape
    tm = TM_POST
    tiles_per_seq = SEQ // tm
    hv = GLA_HEADS * GLA_DV

    def full(a):
        return pl.BlockSpec(a.shape, lambda i: (0,) * a.ndim)

    def rows(w):
        return pl.BlockSpec((tm, w), lambda i: (i, 0))

    def lanes():
        return pl.BlockSpec((TOP_K, tm), lambda i: (0, i))

    return pl.pallas_call(
        _post_body,
        grid=(t // tm,),
        in_specs=[
            rows(hv), rows(hv), rows(hv), rows(hv), rows(d),
            pl.BlockSpec((None, 6, d), lambda i: (i // tiles_per_seq, 0, 0)),
            full(g_gla), full(g_pm), full(g_pf), full(wout), full(wrh), full(wrl), full(br),
        ],
        out_specs=[
            rows(d),
            pl.BlockSpec((tm * PACK_SUB, LANES), lambda i: (i, 0)),
            lanes(), lanes(), lanes(),
            pl.BlockSpec((N_EXPERTS, LANES), lambda i: (0, 0)),
        ],
        out_shape=[
            jax.ShapeDtypeStruct((t, d), F32),
            jax.ShapeDtypeStruct((t * PACK_SUB, LANES), I32),
            jax.ShapeDtypeStruct((TOP_K, t), I32),
            jax.ShapeDtypeStruct((TOP_K, t), F32),
            jax.ShapeDtypeStruct((TOP_K, t), I32),
            jax.ShapeDtypeStruct((N_EXPERTS, LANES), F32),
        ],
        scratch_shapes=[pltpu.VMEM((N_EXPERTS, 1), F32)],
        compiler_params=_params(("arbitrary",), 48),
        name="post",
    )(o_f, o_b, gg, o_att, x2, mod, g_gla, g_pm, g_pf, wout, wrh, wrl, br)


def _route_body(ti_ref, rk_ref, cnt_ref, pos_ref, blk_ref):
    cnt = cnt_ref[...]
    padded = jnp.floor((cnt + (MOE_BM - 1)) * (1.0 / MOE_BM)) * MOE_BM
    starts, ends = [], []
    acc = jnp.zeros((1, LANES), F32)
    for e in range(N_EXPERTS):
        starts.append(acc)
        acc = acc + padded[e:e + 1, :]
        ends.append(acc)
    ti = ti_ref[...]
    off = jnp.zeros(ti.shape, F32)
    for e in range(N_EXPERTS):
        off = jnp.where(ti == e, starts[e][:, 0:1], off)
    pos_ref[...] = rk_ref[...] + off.astype(I32)

    def owner_of(row):
        n_le = jnp.zeros(row.shape, I32)
        for e in range(N_EXPERTS):
            n_le = n_le + jnp.where(ends[e][:, 0:1] <= row, 1, 0)
        return jnp.minimum(n_le, N_EXPERTS - 1)

    block_start = lax.broadcasted_iota(I32, (1, MOE_NB_PAD), 1).astype(F32) * MOE_BM
    owner = owner_of(block_start)
    nxt = jnp.zeros((1, MOE_NB_PAD), I32)
    nxt_blk = jnp.zeros((1, MOE_NB_PAD), I32)
    for e in range(N_EXPERTS):
        end_e = ends[e][:, 0:1]
        nxt = jnp.where(owner == e, jnp.where(end_e < acc[:, 0:1], owner_of(end_e), -1), nxt)
        nxt_blk = jnp.where(owner == e, (end_e * (1.0 / MOE_BM)).astype(I32), nxt_blk)
    used = jnp.broadcast_to((acc[:, 0:1] * (1.0 / MOE_BM)).astype(I32), (1, MOE_NB_PAD))
    blk_ref[...] = jnp.concatenate([owner, nxt, used, nxt_blk, jnp.zeros((SUBLANES - 4, MOE_NB_PAD), I32)], axis=0)


def _route(top_i, rank, counts):
    return pl.pallas_call(
        _route_body,
        out_shape=[
            jax.ShapeDtypeStruct(top_i.shape, I32),
            jax.ShapeDtypeStruct((SUBLANES, MOE_NB_PAD), I32),
        ],
        compiler_params=pltpu.CompilerParams(vmem_limit_bytes=32 * 1024 * 1024),
        name="route",
    )(top_i, rank, counts)


def _sc_workers():
    info = plsc.get_sparse_core_info()
    return info.num_cores, info.num_subcores, info.num_lanes


def _sc_gather_loop(table_hbm, out_hbm, idx_v, base, chunks, bufs, gather_sems, write_sems):
    window = SC_GATHER_WINDOW
    ring = len(bufs)

    def fetch(c, b):
        return pltpu.make_async_copy(table_hbm.at[idx_v.at[pl.ds(c * window, window)]], bufs[b], gather_sems[b])

    def flush(c, b):
        return pltpu.make_async_copy(bufs[b], out_hbm.at[pl.ds(base + c * window, window)], write_sems[b])

    for b in range(ring):
        fetch(b, b).start()

    @pl.loop(0, chunks, step=ring)
    def _(c0):
        for b in range(ring):
            c = c0 + b
            fetch(c, b).wait()
            flush(c, b).start()

            @pl.when(c + ring < chunks)
            def _():
                flush(c, b).wait()
                fetch(c + ring, b).start()

    for b in range(ring):
        flush(chunks - ring + b, b).wait()


def _sc_source_rows(pos_flat, n_rows):
    cores, subcores, lanes = _sc_workers()
    workers = cores * subcores
    per_worker = n_rows // workers
    n_assign = pos_flat.shape[0]
    scan = SC_SCAN_CHUNK
    assert per_worker * workers == n_rows and per_worker % lanes == 0
    assert n_assign % scan == 0 and scan % lanes == 0
    mesh = plsc.VectorSubcoreMesh(core_axis_name="core", subcore_axis_name="subcore")

    @functools.partial(
        pl.kernel,
        out_type=jax.ShapeDtypeStruct((n_rows,), I32),
        mesh=mesh,
        scratch_types=[pltpu.VMEM((per_worker,), I32), pltpu.VMEM((scan,), I32)],
        compiler_params=pltpu.CompilerParams(needs_layout_passes=False),
        name="sc_source_rows",
    )
    def invert(pos_hbm, out_hbm, src_v, pos_v):
        wid = lax.axis_index("subcore") * cores + lax.axis_index("core")
        base = wid * per_worker
        lane = lax.iota(I32, lanes)

        @pl.loop(0, per_worker, step=lanes)
        def _(j):
            src_v[pl.ds(j, lanes)] = (base + j + lane) & (TOKENS - 1)

        @pl.loop(0, n_assign, step=scan)
        def _(a0):
            pltpu.sync_copy(pos_hbm.at[pl.ds(a0, scan)], pos_v)

            @plsc.parallel_loop(0, scan, step=lanes, unroll=SC_SCAN_UNROLL)
            def _(j):
                rel = pos_v[pl.ds(j, lanes)] - base
                mine = (rel >= 0) & (rel < per_worker)
                tok = (a0 + j + lane) & (TOKENS - 1)
                plsc.store_scatter(src_v, [jnp.where(mine, rel, 0)], tok, mask=mine)

        pltpu.sync_copy(src_v, out_hbm.at[pl.ds(base, per_worker)])

    return invert(pos_flat)


def _sc_gather_rows(table, idx):
    cores, subcores, _ = _sc_workers()
    workers = cores * subcores
    n = idx.shape[0]
    window = SC_GATHER_WINDOW
    per_worker = n // workers
    chunks = per_worker // window
    ring = SC_GATHER_RING
    assert per_worker * workers == n and chunks * window == per_worker and chunks % ring == 0
    row_shape = table.shape[1:]
    mesh = plsc.VectorSubcoreMesh(core_axis_name="core", subcore_axis_name="subcore")

    @functools.partial(
        pl.kernel,
        out_type=jax.ShapeDtypeStruct((n,) + row_shape, table.dtype),
        mesh=mesh,
        scratch_types=[pltpu.VMEM((per_worker,), I32)]
        + [pltpu.VMEM((window,) + row_shape, table.dtype)] * ring
        + [pltpu.SemaphoreType.DMA] * (2 * ring),
        name="sc_gather_rows",
    )
    def gather(table_hbm, idx_hbm, out_hbm, idx_v, *scratch):
        wid = lax.axis_index("subcore") * cores + lax.axis_index("core")
        base = wid * per_worker
        pltpu.sync_copy(idx_hbm.at[pl.ds(base, per_worker)], idx_v)
        _sc_gather_loop(table_hbm, out_hbm, idx_v, base, chunks,
                        scratch[:ring], scratch[ring:2 * ring], scratch[2 * ring:])

    return gather(table, idx)


def _experts_body(first, be_ref, nxt_ref, nxtblk_ref, nu_ref, xs_ref, bgu_ref, bd_ref, wgu_hbm, wd_hbm, *rest):
    ys_ref, wgu_f32, wd_f32, wgu_bf, wd_bf, sems = rest[-6:]
    step = pl.program_id(0)
    end = first + pl.num_programs(0) * MOE_SUB
    bm = MOE_BM
    n_used = nu_ref[0]

    def fetch(e):
        return (pltpu.make_async_copy(wgu_hbm.at[e], wgu_f32, sems.at[0]),
                pltpu.make_async_copy(wd_hbm.at[e], wd_f32, sems.at[1]))

    @pl.when((step == 0) & (first < n_used))
    def _():
        for cp in fetch(be_ref[first]):
            cp.start()

    for sub in range(MOE_SUB):
        i = first + step * MOE_SUB + sub
        e = be_ref[i]
        used = i < n_used
        fresh = (i == first) | (e != be_ref[jnp.maximum(i - 1, 0)])
        rows = pl.ds(sub * bm * PACK_SUB, bm * PACK_SUB)
        xs_sub = xs_ref.at[rows, :]
        ys_sub = ys_ref.at[rows, :]

        @pl.when(used & fresh)
        def _():
            for cp in fetch(e):
                cp.wait()
            wgu_bf[...] = wgu_f32[...].astype(BF16)
            wd_bf[...] = wd_f32[...].astype(BF16)

            @pl.when((nxt_ref[i] >= 0) & (nxtblk_ref[i] < end))
            def _():
                for cp in fetch(nxt_ref[i]):
                    cp.start()

        @pl.when(used)
        def _():
            x = jnp.concatenate(_unpack_rows(xs_sub, bm), axis=1).astype(BF16)
            gu = jnp.dot(x, wgu_bf[...], preferred_element_type=F32) + bgu_ref[pl.ds(e, 1), :]
            gate = jnp.minimum(gu[:, 0:D_FF], SWIGLU_LIMIT)
            up = jnp.clip(gu[:, D_FF:2 * D_FF], -SWIGLU_LIMIT, SWIGLU_LIMIT)
            act = ((up + 1.0) * gate * jax.nn.sigmoid(SWIGLU_ALPHA * gate)).astype(BF16)
            y = jnp.dot(act, wd_bf[...], preferred_element_type=F32) + bd_ref[pl.ds(e, 1), :]
            _pack_rows(ys_sub, y.astype(BF16).astype(F32))

        @pl.when(jnp.logical_not(used))
        def _():
            ys_sub[...] = jnp.zeros((bm * PACK_SUB, LANES), I32)


def _experts(first, part_blocks, blocks, xs_part, w_gate_up, b_gate_up, w_down, b_down, ys_prev):
    rows = MOE_SUB * MOE_BM * PACK_SUB
    d = D_MODEL
    steps = part_blocks // MOE_SUB
    assert steps * MOE_SUB == part_blocks and first % MOE_SUB == 0

    def x_block(i, be, nx, nb, nu):
        last = jnp.maximum((jnp.minimum(nu[0], first + part_blocks) - 1 - first) // MOE_SUB, 0)
        return jnp.minimum(i, last)

    in_specs = [
        pl.BlockSpec((rows, LANES), lambda i, be, nx, nb, nu: (x_block(i, be, nx, nb, nu), 0)),
        pl.BlockSpec(b_gate_up.shape, lambda i, be, nx, nb, nu: (0, 0)),
        pl.BlockSpec(b_down.shape, lambda i, be, nx, nb, nu: (0, 0)),
        pl.BlockSpec(memory_space=pl.ANY),
        pl.BlockSpec(memory_space=pl.ANY),
    ]
    operands = [blocks[0, :MOE_NB], blocks[1, :MOE_NB], blocks[3, :MOE_NB], blocks[2, :1],
                xs_part, b_gate_up, b_down, w_gate_up, w_down]
    aliases = {}
    if ys_prev is not None:
        in_specs.append(pl.BlockSpec(memory_space=pl.ANY))
        aliases = {len(operands): 0}
        operands.append(ys_prev)
    grid_spec = pltpu.PrefetchScalarGridSpec(
        num_scalar_prefetch=4,
        grid=(steps,),
        in_specs=in_specs,
        out_specs=pl.BlockSpec((rows, LANES), lambda i, be, nx, nb, nu: (first // MOE_SUB + i, 0)),
        scratch_shapes=[
            pltpu.VMEM((d, 2 * D_FF), F32), pltpu.VMEM((D_FF, d), F32),
            pltpu.VMEM((d, 2 * D_FF), BF16), pltpu.VMEM((D_FF, d), BF16),
            pltpu.SemaphoreType.DMA((2,)),
        ],
    )
    return pl.pallas_call(
        functools.partial(_experts_body, first),
        grid_spec=grid_spec,
        out_shape=jax.ShapeDtypeStruct((MOE_ROWS * PACK_SUB, LANES), I32),
        input_output_aliases=aliases,
        compiler_params=_params(("arbitrary",), 48),
        name="experts",
    )(*operands)


def _combine_body(gates_ref, x1_ref, mod_ref, gpost_ref, y0_ref, y1_ref, y2_ref, y3_ref, o_ref):
    tm = TM_COMB
    gates = gates_ref[...]
    y_hi = jnp.zeros((tm, PACK_COLS), F32)
    y_lo = jnp.zeros((tm, PACK_COLS), F32)
    for k, yk_ref in enumerate((y0_ref, y1_ref, y2_ref, y3_ref)):
        hi, lo = _unpack_rows(yk_ref, tm)
        y_hi = y_hi + hi * gates[:, k:k + 1]
        y_lo = y_lo + lo * gates[:, k:k + 1]
    y = jnp.concatenate([y_hi, y_lo], axis=1)
    gain = mod_ref[5:6, :] * gpost_ref[...]
    o_ref[...] = x1_ref[...] + _rms(y, gain)


def _combine(gates_t, x1, mod, g_post, y4):
    t, d = x1.shape
    tm = TM_COMB
    tiles = t // tm
    tiles_per_seq = SEQ // tm

    def slab(k):
        return pl.BlockSpec((tm * PACK_SUB, LANES), lambda i: (k * tiles + i, 0))

    return pl.pallas_call(
        _combine_body,
        grid=(tiles,),
        in_specs=[
            pl.BlockSpec((tm, TOP_K), lambda i: (i, 0)),
            pl.BlockSpec((tm, d), lambda i: (i, 0)),
            pl.BlockSpec((None, 6, d), lambda i: (i // tiles_per_seq, 0, 0)),
            pl.BlockSpec(g_post.shape, lambda i: (0, 0)),
            slab(0), slab(1), slab(2), slab(3),
        ],
        out_specs=pl.BlockSpec((tm, d), lambda i: (i, 0)),
        out_shape=jax.ShapeDtypeStruct((t, d), F32),
        compiler_params=_params(("arbitrary",), 48),
        name="combine",
    )(gates_t, x1, mod, g_post, y4, y4, y4, y4)


def _rotary_tables():
    half = ROT_DIM // 2
    inv_freq = ROPE_THETA ** (-2.0 * np.arange(half, dtype=np.float32) / ROT_DIM)
    ang = np.arange(SEQ, dtype=np.float32)[:, None] * inv_freq[None, :].astype(np.float32)
    cos, sin = np.cos(ang), np.sin(ang)
    ones = np.ones((SEQ, ATT_HEAD_DIM - ROT_DIM), np.float32)
    zeros = np.zeros((SEQ, ATT_HEAD_DIM - ROT_DIM), np.float32)
    zh = np.zeros((SEQ, half), np.float32)
    reps = LANES // ATT_HEAD_DIM
    rc = np.tile(np.concatenate([cos, cos, ones], axis=1), (1, reps))
    rm = np.tile(np.concatenate([-sin, zh, zeros], axis=1), (1, reps))
    rp = np.tile(np.concatenate([zh, sin, zeros], axis=1), (1, reps))
    return tuple(jnp.asarray(t, F32) for t in (rc, rm, rp))


def _mixer_inputs(w_in, w_gk_fwd, b_gk_fwd, w_gk_bwd, b_gk_bwd):
    hk = GLA_HEADS * GLA_DK
    hv = GLA_HEADS * GLA_DV
    w = w_in[0]
    o_lr = 2 * hk + 2 * hv
    o_aq = o_lr + 2 * GLA_RANK
    o_ak = o_aq + ATT_Q_HEADS * ATT_HEAD_DIM
    o_av = o_ak + ATT_KV_HEADS * ATT_HEAD_DIM
    hd = ATT_HEAD_DIM
    wa = w[:, :o_lr].astype(BF16)
    wlr = w[:, o_lr:o_aq].astype(BF16)
    dup = lambda m: jnp.concatenate([m[:, g * hd:(g + 1) * hd] for g in range(ATT_KV_HEADS) for _ in range(2)], axis=1)
    wb = jnp.concatenate([w[:, o_aq:o_ak], dup(w[:, o_ak:o_av])], axis=1).astype(BF16)
    wvt = dup(w[:, o_av:o_av + ATT_KV_HEADS * hd]).T.astype(BF16)
    zr = jnp.zeros((GLA_RANK, hk), F32)
    wgk = jnp.concatenate([jnp.concatenate([w_gk_fwd[0], zr], axis=1),
                           jnp.concatenate([zr, w_gk_bwd[0]], axis=1)], axis=0).astype(BF16)
    bgk = jnp.concatenate([b_gk_fwd[0], b_gk_bwd[0]])[None, :]
    return (wa, wlr, wgk, bgk, wb, wvt) + _rotary_tables()


def kernel(x, c, w_ada, b_ada, g_pre_mix, g_post_mix, w_in, w_gk_fwd, b_gk_fwd, w_gk_bwd, b_gk_bwd, g_gla_out,
           attn_sink, w_out, g_pre_ffn, g_post_ffn, w_router, b_router, w_gate_up, b_gate_up, w_down, b_down):
    assert x.shape == (BATCH, SEQ, D_MODEL) and w_ada.shape[0] == 1
    d = D_MODEL
    x2 = x.reshape(TOKENS, d)

    c_pad = jnp.pad(c, ((0, SUBLANES - BATCH), (0, 0)))
    mod = _ada(c_pad, w_ada[0], b_ada)[:BATCH].reshape(BATCH, 6, d)

    mixer_in = _mixer_inputs(w_in, w_gk_fwd, b_gk_fwd, w_gk_bwd, b_gk_bwd)
    q, k, v, gg, laf, lab, aq, ak2, avt = _inproj(x2, mod, g_pre_mix, *mixer_in)
    o_f, o_b = _gla(q, k, v, laf, lab)
    o_att = _attn(attn_sink[0], aq, ak2, avt)

    wr_t = w_router[0].T
    wrh = wr_t.astype(BF16)
    wrl = (wr_t - wrh.astype(F32)).astype(BF16)
    x1, h2_tiles, top_i, gates, rank, counts = _post(
        o_f, o_b, gg, o_att, x2, mod, g_gla_out, g_post_mix, g_pre_ffn, w_out[0].astype(BF16), wrh, wrl,
        b_router[0][:, None])

    pos, blocks = _route(top_i, rank, counts)

    src = _sc_source_rows(pos.reshape(TOP_K * TOKENS), MOE_ROWS)
    h2_rows = h2_tiles.reshape(TOKENS, PACK_SUB, LANES)
    ys = None
    first = 0
    for part_blocks in MOE_PART_BLOCKS:
        row0, n_rows = first * MOE_BM, part_blocks * MOE_BM
        xs_p = _sc_gather_rows(h2_rows, src[row0:row0 + n_rows])
        ys = _experts(first, part_blocks, blocks, xs_p.reshape(n_rows * PACK_SUB, LANES),
                      w_gate_up[0], b_gate_up[0], w_down[0], b_down[0], ys)
        first += part_blocks
    assert first == MOE_NB

    y4 = _sc_gather_rows(ys.reshape(MOE_ROWS, PACK_SUB, LANES), pos.reshape(TOP_K * TOKENS))
    out = _combine(gates.T, x1, mod, g_post_ffn, y4.reshape(TOP_K * TOKENS * PACK_SUB, LANES))
    return out.reshape(BATCH, SEQ, d)
```

```python
import functools

import jax
import jax.numpy as jnp
import numpy as np
from jax import lax
from jax.experimental import pallas as pl
from jax.experimental.pallas import tpu as pltpu
from jax.experimental.pallas import tpu_sc as plsc

F32 = jnp.float32
BF16 = jnp.bfloat16
I32 = jnp.int32

D_MODEL = 1024
BATCH = 2
SEQ = 8192
TOKENS = BATCH * SEQ
GLA_HEADS = 4
GLA_DV = 128
GLA_DK = 64
GLA_RANK = 16
GLA_GATE_NORMALIZER = 16.0
GLA_CHUNK = 64
ATT_Q_HEADS = 8
ATT_KV_HEADS = 2
ATT_HEAD_DIM = 64
ATT_WINDOW = 128
ATT_BLOCK = 128
ROT_DIM = 16
ROPE_THETA = 500000.0
N_EXPERTS = 32
TOP_K = 4
D_FF = 1024
SWIGLU_LIMIT = 7.0
SWIGLU_ALPHA = 1.702
NORM_EPS = 1e-6
NEG_INF = -1e30
LOG2_E = 1.4426950408889634

LANES = 128
SUBLANES = 8
PACK_COLS = D_MODEL // 2
PACK_SUB = PACK_COLS // LANES

TM_IN = 1024
IN_SUB = 4
GLA_GROUP = 16
ATT_GROUP = 16
TM_POST = 1024
POST_SUB = 4
MOE_BM = 256
MOE_ROWS = TOKENS * TOP_K + N_EXPERTS * MOE_BM
MOE_NB = MOE_ROWS // MOE_BM
MOE_SUB = 4
MOE_PART_BLOCKS = (32, 128, 128)
MOE_NB_PAD = ((MOE_NB + LANES - 1) // LANES) * LANES
SC_SCAN_CHUNK = 4096
SC_SCAN_UNROLL = 8
TM_COMB = 512
SC_GATHER_WINDOW = 32
SC_GATHER_RING = 4

NT_DIMS = (((1,), (1,)), ((), ()))
TN_DIMS = (((0,), (0,)), ((), ()))


def _params(semantics, vmem_mib):
    return pltpu.CompilerParams(dimension_semantics=semantics, vmem_limit_bytes=vmem_mib * 1024 * 1024)


def _rms(x, g):
    return x * lax.rsqrt(jnp.mean(x * x, axis=-1, keepdims=True) + NORM_EPS) * g


def _silu(x):
    return x * jax.nn.sigmoid(x)


def _pack_rows(ref, v):
    m = v.shape[0]
    bits = lax.bitcast_convert_type(v, jnp.uint32)
    word = lax.bitcast_convert_type(bits[:, :PACK_COLS] | (bits[:, PACK_COLS:] >> 16), I32)
    for s in range(PACK_SUB):
        ref[pl.ds(s, m, stride=PACK_SUB), :] = word[:, s * LANES:(s + 1) * LANES]


def _unpack_rows(ref, m):
    word = jnp.concatenate([ref[pl.ds(s, m, stride=PACK_SUB), :] for s in range(PACK_SUB)], axis=1)
    bits = lax.bitcast_convert_type(word, jnp.uint32)
    hi = lax.bitcast_convert_type(bits & jnp.uint32(0xFFFF0000), F32)
    lo = lax.bitcast_convert_type(bits << 16, F32)
    return hi, lo


def _ada_body(c_ref, w_ref, b_ref, o_ref):
    ca = _silu(c_ref[...]).astype(BF16)
    o_ref[...] = jnp.dot(ca, w_ref[...].astype(BF16), preferred_element_type=F32) + b_ref[...]


def _ada(c_pad, w_ada, b_ada):
    d = D_MODEL
    return pl.pallas_call(
        _ada_body,
        grid=(6,),
        in_specs=[
            pl.BlockSpec((SUBLANES, d), lambda j: (0, 0)),
            pl.BlockSpec((d, d), lambda j: (0, j)),
            pl.BlockSpec((1, d), lambda j: (0, j)),
        ],
        out_specs=pl.BlockSpec((SUBLANES, d), lambda j: (0, j)),
        out_shape=jax.ShapeDtypeStruct((SUBLANES, 6 * d), F32),
        compiler_params=_params(("arbitrary",), 32),
        name="ada",
    )(c_pad, w_ada, b_ada)


def _rotary(x, cos_t, msin_t, psin_t):
    width = x.shape[1]
    reps = width // LANES
    c = jnp.concatenate([cos_t] * reps, axis=1)
    m = jnp.concatenate([msin_t] * reps, axis=1)
    p = jnp.concatenate([psin_t] * reps, axis=1)
    half = ROT_DIM // 2
    return x * c + pltpu.roll(x, width - half, 1) * m + pltpu.roll(x, half, 1) * p


def _inproj_body(x_ref, mod_ref, g_ref, wa_ref, wlr_ref, wgk_ref, bgk_ref, wb_ref, wvt_ref, rc_ref, rm_ref, rp_ref,
                 q_ref, k_ref, v_ref, gg_ref, laf_ref, lab_ref, aq_ref, ak_ref, avt_ref):
    shift = mod_ref[0:1, :]
    scale = mod_ref[1:2, :]
    hk = GLA_HEADS * GLA_DK
    hv = GLA_HEADS * GLA_DV
    nq = ATT_Q_HEADS * ATT_HEAD_DIM
    nk = 2 * ATT_KV_HEADS * ATT_HEAD_DIM
    sub = x_ref.shape[0] // IN_SUB
    subs = [slice(s * sub, (s + 1) * sub) for s in range(IN_SUB)]

    gain = g_ref[...] * (1.0 + scale)

    def hidden(rows):
        return (_rms(x_ref[rows, :], gain) + shift).astype(BF16)

    def project(h):
        return (jnp.dot(h, wa_ref[...], preferred_element_type=F32),
                jnp.dot(h, wlr_ref[...], preferred_element_type=F32),
                jnp.dot(h, wb_ref[...], preferred_element_type=F32),
                lax.dot_general(wvt_ref[...], h, NT_DIMS, preferred_element_type=F32))

    def finish(rows, pa, plr, pb, pvt):
        q_ref[rows, :] = pa[:, 0:hk] * (GLA_DK ** -0.5)
        k_ref[rows, :] = pa[:, hk:2 * hk]
        v_ref[rows, :] = pa[:, 2 * hk:2 * hk + hv].astype(BF16)
        gg_ref[rows, :] = pa[:, 2 * hk + hv:2 * hk + 2 * hv]
        gk = jnp.dot(plr.astype(BF16), wgk_ref[...], preferred_element_type=F32) + bgk_ref[...]
        la = (jnp.minimum(gk, 0.0) - jnp.log1p(jnp.exp(-jnp.abs(gk)))) * (1.0 / GLA_GATE_NORMALIZER)
        laf_ref[rows, :] = la[:, 0:hk]
        lab_ref[rows, :] = la[:, hk:2 * hk]
        rc, rm, rp = rc_ref[rows, :], rm_ref[rows, :], rp_ref[rows, :]
        aq_ref[rows, :] = (_rotary(pb[:, 0:nq], rc, rm, rp) * (ATT_HEAD_DIM ** -0.5 * LOG2_E)).astype(BF16)
        ak_ref[rows, :] = _rotary(pb[:, nq:nq + nk], rc, rm, rp).astype(BF16)
        avt_ref[:, rows] = pvt.astype(BF16)

    hs = [hidden(rows) for rows in subs]
    ps = [project(h) for h in hs]
    for rows, p in zip(subs, ps):
        finish(rows, *p)


def _inproj(x2, mod, g_pre, wa, wlr, wgk, bgk, wb, wvt, rc, rm, rp):
    t, d = x2.shape
    tm = TM_IN
    tiles_per_seq = SEQ // tm
    hk = GLA_HEADS * GLA_DK
    hv = GLA_HEADS * GLA_DV
    nq = ATT_Q_HEADS * ATT_HEAD_DIM
    nk = 2 * ATT_KV_HEADS * ATT_HEAD_DIM

    def full(a):
        return pl.BlockSpec(a.shape, lambda i: (0,) * a.ndim)

    def rows(w):
        return pl.BlockSpec((tm, w), lambda i: (i, 0))

    def table():
        return pl.BlockSpec((tm, LANES), lambda i: (i % tiles_per_seq, 0))

    out_widths = [(hk, F32), (hk, F32), (hv, BF16), (hv, F32), (hk, F32), (hk, F32), (nq, BF16), (nk, BF16)]
    return pl.pallas_call(
        _inproj_body,
        grid=(t // tm,),
        in_specs=[
            rows(d),
            pl.BlockSpec((None, 6, d), lambda i: (i // tiles_per_seq, 0, 0)),
            full(g_pre), full(wa), full(wlr), full(wgk), full(bgk), full(wb), full(wvt),
            table(), table(), table(),
        ],
        out_specs=[rows(w) for w, _ in out_widths] + [pl.BlockSpec((nk, tm), lambda i: (0, i))],
        out_shape=[jax.ShapeDtypeStruct((t, w), dt) for w, dt in out_widths] + [jax.ShapeDtypeStruct((nk, t), BF16)],
        compiler_params=_params(("arbitrary",), 56),
        name="inproj",
    )(x2, mod, g_pre, wa, wlr, wgk, bgk, wb, wvt, rc, rm, rp)


def _gla_body(qf_ref, kf_ref, vf_ref, laf_ref, qb_ref, kb_ref, vb_ref, lab_ref, of_ref, ob_ref, sf_ref, sb_ref):
    @pl.when(pl.program_id(1) == 0)
    def _():
        sf_ref[...] = jnp.zeros_like(sf_ref)
        sb_ref[...] = jnp.zeros_like(sb_ref)

    c = GLA_CHUNK
    r_i = lax.broadcasted_iota(I32, (c, c), 0)
    c_i = lax.broadcasted_iota(I32, (c, c), 1)
    lower = c_i <= r_i
    upper = c_i >= r_i
    cum_f = jnp.where(lower, 1.0, 0.0).astype(BF16)
    cum_b = jnp.where(upper, 1.0, 0.0).astype(BF16)
    lane = lax.broadcasted_iota(I32, (1, LANES), 1)
    head_masks = (lane < GLA_DK, lane >= GLA_DK)

    fwd = [(qf_ref, kf_ref, laf_ref, vf_ref, of_ref, slice(g * c, (g + 1) * c), cum_f, lower, c - 1, c // 2 - 1)
           for g in range(GLA_GROUP)]
    bwd = [(qb_ref, kb_ref, lab_ref, vb_ref, ob_ref, slice(g * c, (g + 1) * c), cum_b, upper, 0, c // 2)
           for g in reversed(range(GLA_GROUP))]
    heads = range(GLA_HEADS)
    pair = [slice((h // 2) * LANES, (h // 2 + 1) * LANES) for h in heads]
    vcols = [slice(h * GLA_DV, (h + 1) * GLA_DV) for h in heads]

    def stage1(item):
        q_ref, k_ref, la_ref, v_ref, o_ref, rows, cum, tri, i_last, i_mid = item
        la = la_ref[rows, :]
        hi = la.astype(BF16)
        lo = (la - hi.astype(F32)).astype(BF16)
        b = jnp.dot(cum, hi, preferred_element_type=F32) + jnp.dot(cum, lo, preferred_element_type=F32)
        b_last = b[i_last:i_last + 1, :]
        b_mid = b[i_mid:i_mid + 1, :]
        q, k = q_ref[rows, :], k_ref[rows, :]
        return (q * jnp.exp(b - b_mid), (k * jnp.exp(b_mid - b)).astype(BF16), q * jnp.exp(b),
                (k * jnp.exp(b_last - b)).astype(BF16), jnp.exp(b_last))

    def stage2(item, pre):
        v_ref, rows, tri = item[3], item[5], item[7]
        qs, ks, qi, kst, decay = pre
        out = []
        for h in heads:
            mask = head_masks[h % 2]
            qs_h = jnp.where(mask, qs[:, pair[h]], 0.0).astype(BF16)
            sc = lax.dot_general(qs_h, ks[:, pair[h]], NT_DIMS, preferred_element_type=F32)
            v_h = v_ref[rows, vcols[h]]
            kv = lax.dot_general(v_h, kst[:, pair[h]], TN_DIMS, preferred_element_type=F32)
            out.append((jnp.where(tri, sc, 0.0).astype(BF16), kv,
                        jnp.where(mask, qi[:, pair[h]], 0.0).astype(BF16), v_h))
        return out

    def run(items, s_ref):
        pre = [stage1(it) for it in items]
        mid = [stage2(it, p) for it, p in zip(items, pre)]
        states = [s_ref[h] for h in heads]
        for it, p, m in zip(items, pre, mid):
            o_ref, rows, decay = it[4], it[5], p[4]
            for h in heads:
                sc, kv, qi_h, v_h = m[h]
                o = jnp.dot(sc, v_h, preferred_element_type=F32)
                o = o + lax.dot_general(qi_h, states[h].astype(BF16), NT_DIMS, preferred_element_type=F32)
                o_ref[rows, vcols[h]] = o
                states[h] = states[h] * decay[:, pair[h]] + kv
        for h in heads:
            s_ref[h] = states[h]

    run(fwd, sf_ref)
    run(bwd, sb_ref)


def _gla(q, k, v, laf, lab):
    t = q.shape[0]
    rows = GLA_GROUP * GLA_CHUNK
    ng = SEQ // rows
    hk = GLA_HEADS * GLA_DK
    hv = GLA_HEADS * GLA_DV

    def fwd(w):
        return pl.BlockSpec((rows, w), lambda b, n: (b * ng + n, 0))

    def bwd(w):
        return pl.BlockSpec((rows, w), lambda b, n: (b * ng + ng - 1 - n, 0))

    return pl.pallas_call(
        _gla_body,
        grid=(BATCH, ng),
        in_specs=[fwd(hk), fwd(hk), fwd(hv), fwd(hk), bwd(hk), bwd(hk), bwd(hv), bwd(hk)],
        out_specs=[fwd(hv), bwd(hv)],
        out_shape=[jax.ShapeDtypeStruct((t, hv), F32)] * 2,
        scratch_shapes=[pltpu.VMEM((GLA_HEADS, GLA_DV, 2 * GLA_DK), F32)] * 2,
        compiler_params=_params(("arbitrary", "arbitrary"), 32),
        name="gla",
    )(q, k, v, laf, q, k, v, lab)


def _attn_body(sink_ref, q_ref, kp_ref, kc_ref, kn_ref, vp_ref, vc_ref, vn_ref, o_ref):
    step = pl.program_id(1)
    last = pl.num_programs(1) - 1
    qb = ATT_BLOCK
    hd = ATT_HEAD_DIM
    k_all = jnp.concatenate([kp_ref[...], kc_ref[...], kn_ref[...]], axis=0)
    vt_all = jnp.concatenate([vp_ref[...], vc_ref[...], vn_ref[...]], axis=1)
    lane = lax.broadcasted_iota(I32, (1, LANES), 1)
    lo = lane < hd
    j_k = lax.broadcasted_iota(I32, (3 * qb, qb), 0)
    i_q = lax.broadcasted_iota(I32, (3 * qb, qb), 1)
    band = jnp.abs(j_k - qb - i_q) <= ATT_WINDOW
    sinks = [jnp.concatenate([jnp.full((1, qb), sink_ref[4 * g + r] * LOG2_E, F32) for r in range(4)], axis=1)
             for g in range(ATT_KV_HEADS)]
    ones_rows = jnp.ones((SUBLANES, 3 * qb), BF16)
    work = [(j, g) for j in range(ATT_GROUP) for g in range(ATT_KV_HEADS)]

    def scores(j, g):
        valid = band
        if j == 0:
            valid = valid & ((j_k >= qb) | (step > 0))
        if j == ATT_GROUP - 1:
            valid = valid & ((j_k < 2 * qb) | (step < last))
        valid4 = jnp.concatenate([valid] * 4, axis=1)
        rows = slice(j * qb, (j + 1) * qb)
        kg = k_all[j * qb:(j + 3) * qb, g * LANES:(g + 1) * LANES]
        qa = q_ref[rows, (2 * g) * LANES:(2 * g + 1) * LANES]
        qc = q_ref[rows, (2 * g + 1) * LANES:(2 * g + 2) * LANES]
        zero = jnp.zeros_like(qa)
        lhs = jnp.concatenate([jnp.where(lo, qa, zero), jnp.where(lo, zero, qa),
                               jnp.where(lo, qc, zero), jnp.where(lo, zero, qc)], axis=0)
        st = lax.dot_general(kg, lhs, NT_DIMS, preferred_element_type=F32)
        return jnp.concatenate([jnp.where(valid4[0:qb], st[0:qb], NEG_INF), st[qb:2 * qb],
                                jnp.where(valid4[2 * qb:3 * qb], st[2 * qb:3 * qb], NEG_INF)], axis=0)

    def softmax(st, g):
        sink = sinks[g]
        m = jnp.maximum(jnp.max(st, axis=0, keepdims=True), sink)
        return jnp.exp2(st - m).astype(BF16), jnp.exp2(sink - m)

    def output(j, g, p, p_sink):
        rows = slice(j * qb, (j + 1) * qb)
        vgt = vt_all[g * LANES:(g + 1) * LANES, j * qb:(j + 3) * qb]
        res = jnp.dot(jnp.concatenate([vgt, ones_rows], axis=0), p, preferred_element_type=F32)
        ot = res[0:LANES] * (1.0 / (res[LANES:LANES + 1] + p_sink))
        pair_a = jnp.concatenate([ot[0:hd, 0:qb], ot[hd:2 * hd, qb:2 * qb]], axis=0)
        pair_c = jnp.concatenate([ot[0:hd, 2 * qb:3 * qb], ot[hd:2 * hd, 3 * qb:4 * qb]], axis=0)
        o_ref[rows, (2 * g) * LANES:(2 * g + 1) * LANES] = pair_a.T.astype(o_ref.dtype)
        o_ref[rows, (2 * g + 1) * LANES:(2 * g + 2) * LANES] = pair_c.T.astype(o_ref.dtype)

    s_all = [scores(j, g) for j, g in work]
    p_all = [softmax(st, g) for st, (j, g) in zip(s_all, work)]
    for (j, g), (p, p_sink) in zip(work, p_all):
        output(j, g, p, p_sink)


def _attn(sink, aq, ak2, avt):
    t = aq.shape[0]
    qb = ATT_BLOCK
    nb = SEQ // qb
    steps = nb // ATT_GROUP
    nq = ATT_Q_HEADS * ATT_HEAD_DIM
    nk = 2 * ATT_KV_HEADS * ATT_HEAD_DIM

    def edge_block(b, n, shift):
        return b * nb + jnp.clip(n * ATT_GROUP + shift, 0, nb - 1)

    def k_edge(shift):
        return pl.BlockSpec((qb, nk), lambda b, n: (edge_block(b, n, shift), 0))

    def v_edge(shift):
        return pl.BlockSpec((nk, qb), lambda b, n: (0, edge_block(b, n, shift)))

    def group(w):
        return pl.BlockSpec((ATT_GROUP * qb, w), lambda b, n: (b * steps + n, 0))

    v_group = pl.BlockSpec((nk, ATT_GROUP * qb), lambda b, n: (0, b * steps + n))
    return pl.pallas_call(
        _attn_body,
        grid=(BATCH, steps),
        in_specs=[
            pl.BlockSpec(memory_space=pltpu.SMEM),
            group(nq),
            k_edge(-1), group(nk), k_edge(ATT_GROUP), v_edge(-1), v_group, v_edge(ATT_GROUP),
        ],
        out_specs=group(nq),
        out_shape=jax.ShapeDtypeStruct((t, nq), BF16),
        compiler_params=_params(("arbitrary", "arbitrary"), 48),
        name="attn",
    )(sink, aq, ak2, ak2, ak2, avt, avt, avt)


def _post_body(of_ref, ob_ref, gg_ref, oa_ref, x_ref, mod_ref, ggla_ref, gpm_ref, gpf_ref, wout_ref,
               wrh_ref, wrl_ref, br_ref,
               x1_ref, h2_ref, ti_ref, gt_ref, rk_ref, cnt_ref, base_ref):
    tm = TM_POST

    @pl.when(pl.program_id(0) == 0)
    def _():
        base_ref[...] = jnp.zeros_like(base_ref)

    gain1 = mod_ref[2:3, :] * gpm_ref[...]
    shift2 = mod_ref[3:4, :]
    gain2 = gpf_ref[...] * (1.0 + mod_ref[4:5, :])
    sub = tm // POST_SUB
    subs = [slice(s * sub, (s + 1) * sub) for s in range(POST_SUB)]

    def mixer_out(rows):
        og = of_ref[rows, :] + ob_ref[rows, :]
        gg = gg_ref[rows, :]
        parts = []
        for h in range(GLA_HEADS):
            cols = slice(h * GLA_DV, (h + 1) * GLA_DV)
            parts.append((_rms(og[:, cols], ggla_ref[...]) * _silu(gg[:, cols])).astype(BF16))
        return jnp.concatenate(parts + [oa_ref[rows, :]], axis=1)

    def ffn_in(s, y):
        rows = subs[s]
        x1 = x_ref[rows, :] + _rms(y, gain1)
        x1_ref[rows, :] = x1
        h2 = _rms(x1, gain2) + shift2
        hi = h2.astype(BF16)
        hi32 = hi.astype(F32)
        _pack_rows(h2_ref.at[pl.ds(s * sub * PACK_SUB, sub * PACK_SUB), :], hi32)
        return hi, (h2 - hi32).astype(BF16)

    o_subs = [mixer_out(rows) for rows in subs]
    y_subs = [jnp.dot(o, wout_ref[...], preferred_element_type=F32) for o in o_subs]
    split = [ffn_in(s, y) for s, y in enumerate(y_subs)]
    h2_hi = jnp.concatenate([hi for hi, _ in split], axis=0)
    h2_lo = jnp.concatenate([lo for _, lo in split], axis=0)

    wrh = wrh_ref[...]
    logits = (lax.dot_general(wrh, h2_hi, NT_DIMS, preferred_element_type=F32)
              + lax.dot_general(wrh, h2_lo, NT_DIMS, preferred_element_type=F32)
              + lax.dot_general(wrl_ref[...], h2_hi, NT_DIMS, preferred_element_type=F32)
              + br_ref[...])
    e_iota = lax.broadcasted_iota(I32, (N_EXPERTS, tm), 0)
    idxs, vals = [], []
    work = logits
    for _ in range(TOP_K):
        m = jnp.max(work, axis=0, keepdims=True)
        idx = jnp.min(jnp.where(work == m, e_iota, N_EXPERTS), axis=0, keepdims=True)
        idxs.append(idx)
        vals.append(m)
        work = jnp.where(e_iota == idx, -jnp.inf, work)
    exps = [jnp.exp(v - vals[0]) for v in vals]
    inv = 1.0 / (exps[0] + exps[1] + exps[2] + exps[3])
    gt_ref[...] = jnp.concatenate([e * inv for e in exps], axis=0)
    ti_ref[...] = jnp.concatenate(idxs, axis=0)

    onehots = [e_iota == idx for idx in idxs]
    member = jnp.where(onehots[0] | onehots[1] | onehots[2] | onehots[3], 1.0, 0.0)
    t_row = lax.broadcasted_iota(I32, (tm, tm), 0)
    t_col = lax.broadcasted_iota(I32, (tm, tm), 1)
    strict = jnp.where(t_row < t_col, 1.0, 0.0).astype(BF16)
    before = base_ref[...] + jnp.dot(member.astype(BF16), strict, preferred_element_type=F32)
    rk_ref[...] = jnp.concatenate(
        [jnp.sum(jnp.where(oh, before, 0.0), axis=0, keepdims=True) for oh in onehots], axis=0).astype(I32)
    new_base = base_ref[...] + jnp.sum(member, axis=1, keepdims=True)
    base_ref[...] = new_base
    cnt_ref[...] = jnp.broadcast_to(new_base, cnt_ref.shape)


def _post(o_f, o_b, gg, o_att, x2, mod, g_gla, g_pm, g_pf, wout, wrh, wrl, br):
    t, d = x2.shape
    tm = TM_POST
    tiles_per_seq = SEQ // tm
    hv = GLA_HEADS * GLA_DV

    def full(a):
        return pl.BlockSpec(a.shape, lambda i: (0,) * a.ndim)

    def rows(w):
        return pl.BlockSpec((tm, w), lambda i: (i, 0))

    def lanes():
        return pl.BlockSpec((TOP_K, tm), lambda i: (0, i))

    return pl.pallas_call(
        _post_body,
        grid=(t // tm,),
        in_specs=[
            rows(hv), rows(hv), rows(hv), rows(hv), rows(d),
            pl.BlockSpec((None, 6, d), lambda i: (i // tiles_per_seq, 0, 0)),
            full(g_gla), full(g_pm), full(g_pf), full(wout), full(wrh), full(wrl), full(br),
        ],
        out_specs=[
            rows(d),
            pl.BlockSpec((tm * PACK_SUB, LANES), lambda i: (i, 0)),
            lanes(), lanes(), lanes(),
            pl.BlockSpec((N_EXPERTS, LANES), lambda i: (0, 0)),
        ],
        out_shape=[
            jax.ShapeDtypeStruct((t, d), F32),
            jax.ShapeDtypeStruct((t * PACK_SUB, LANES), I32),
            jax.ShapeDtypeStruct((TOP_K, t), I32),
            jax.ShapeDtypeStruct((TOP_K, t), F32),
            jax.ShapeDtypeStruct((TOP_K, t), I32),
            jax.ShapeDtypeStruct((N_EXPERTS, LANES), F32),
        ],
        scratch_shapes=[pltpu.VMEM((N_EXPERTS, 1), F32)],
        compiler_params=_params(("arbitrary",), 48),
        name="post",
    )(o_f, o_b, gg, o_att, x2, mod, g_gla, g_pm, g_pf, wout, wrh, wrl, br)


def _route_body(ti_ref, rk_ref, cnt_ref, pos_ref, blk_ref):
    cnt = cnt_ref[...]
    padded = jnp.floor((cnt + (MOE_BM - 1)) * (1.0 / MOE_BM)) * MOE_BM
    starts, ends = [], []
    acc = jnp.zeros((1, LANES), F32)
    for e in range(N_EXPERTS):
        starts.append(acc)
        acc = acc + padded[e:e + 1, :]
        ends.append(acc)
    ti = ti_ref[...]
    off = jnp.zeros(ti.shape, F32)
    for e in range(N_EXPERTS):
        off = jnp.where(ti == e, starts[e][:, 0:1], off)
    pos_ref[...] = rk_ref[...] + off.astype(I32)

    def owner_of(row):
        n_le = jnp.zeros(row.shape, I32)
        for e in range(N_EXPERTS):
            n_le = n_le + jnp.where(ends[e][:, 0:1] <= row, 1, 0)
        return jnp.minimum(n_le, N_EXPERTS - 1)

    block_start = lax.broadcasted_iota(I32, (1, MOE_NB_PAD), 1).astype(F32) * MOE_BM
    owner = owner_of(block_start)
    nxt = jnp.zeros((1, MOE_NB_PAD), I32)
    nxt_blk = jnp.zeros((1, MOE_NB_PAD), I32)
    slot = jnp.zeros((1, MOE_NB_PAD), I32)
    ordinal = jnp.zeros((1, 1), I32)
    for e in range(N_EXPERTS):
        end_e = ends[e][:, 0:1]
        nxt = jnp.where(owner == e, jnp.where(end_e < acc[:, 0:1], owner_of(end_e), -1), nxt)
        nxt_blk = jnp.where(owner == e, (end_e * (1.0 / MOE_BM)).astype(I32), nxt_blk)
        slot = jnp.where(owner == e, ordinal & 1, slot)
        ordinal = ordinal + jnp.where(padded[e:e + 1, 0:1] > 0, 1, 0)
    used = jnp.broadcast_to((acc[:, 0:1] * (1.0 / MOE_BM)).astype(I32), (1, MOE_NB_PAD))
    blk_ref[...] = jnp.concatenate(
        [owner, nxt, used, nxt_blk, slot, jnp.zeros((SUBLANES - 5, MOE_NB_PAD), I32)], axis=0)


def _route(top_i, rank, counts):
    return pl.pallas_call(
        _route_body,
        out_shape=[
            jax.ShapeDtypeStruct(top_i.shape, I32),
            jax.ShapeDtypeStruct((SUBLANES, MOE_NB_PAD), I32),
        ],
        compiler_params=pltpu.CompilerParams(vmem_limit_bytes=32 * 1024 * 1024),
        name="route",
    )(top_i, rank, counts)


def _sc_workers():
    info = plsc.get_sparse_core_info()
    return info.num_cores, info.num_subcores, info.num_lanes


def _sc_gather_loop(table_hbm, out_hbm, idx_v, base, chunks, bufs, gather_sems, write_sems):
    window = SC_GATHER_WINDOW
    ring = len(bufs)

    def fetch(c, b):
        return pltpu.make_async_copy(table_hbm.at[idx_v.at[pl.ds(c * window, window)]], bufs[b], gather_sems[b])

    def flush(c, b):
        return pltpu.make_async_copy(bufs[b], out_hbm.at[pl.ds(base + c * window, window)], write_sems[b])

    for b in range(ring):
        fetch(b, b).start()

    @pl.loop(0, chunks, step=ring)
    def _(c0):
        for b in range(ring):
            c = c0 + b
            fetch(c, b).wait()
            flush(c, b).start()

            @pl.when(c + ring < chunks)
            def _():
                flush(c, b).wait()
                fetch(c + ring, b).start()

    for b in range(ring):
        flush(chunks - ring + b, b).wait()


def _sc_source_rows(pos_flat, n_rows):
    cores, subcores, lanes = _sc_workers()
    workers = cores * subcores
    per_worker = n_rows // workers
    n_assign = pos_flat.shape[0]
    scan = SC_SCAN_CHUNK
    assert per_worker * workers == n_rows and per_worker % lanes == 0
    assert n_assign % scan == 0 and scan % lanes == 0
    mesh = plsc.VectorSubcoreMesh(core_axis_name="core", subcore_axis_name="subcore")

    @functools.partial(
        pl.kernel,
        out_type=jax.ShapeDtypeStruct((n_rows,), I32),
        mesh=mesh,
        scratch_types=[pltpu.VMEM((per_worker,), I32), pltpu.VMEM((scan,), I32)],
        compiler_params=pltpu.CompilerParams(needs_layout_passes=False),
        name="sc_source_rows",
    )
    def invert(pos_hbm, out_hbm, src_v, pos_v):
        wid = lax.axis_index("subcore") * cores + lax.axis_index("core")
        base = wid * per_worker
        lane = lax.iota(I32, lanes)

        @pl.loop(0, per_worker, step=lanes)
        def _(j):
            src_v[pl.ds(j, lanes)] = (base + j + lane) & (TOKENS - 1)

        @pl.loop(0, n_assign, step=scan)
        def _(a0):
            pltpu.sync_copy(pos_hbm.at[pl.ds(a0, scan)], pos_v)

            @plsc.parallel_loop(0, scan, step=lanes, unroll=SC_SCAN_UNROLL)
            def _(j):
                rel = pos_v[pl.ds(j, lanes)] - base
                mine = (rel >= 0) & (rel < per_worker)
                tok = (a0 + j + lane) & (TOKENS - 1)
                plsc.store_scatter(src_v, [jnp.where(mine, rel, 0)], tok, mask=mine)

        pltpu.sync_copy(src_v, out_hbm.at[pl.ds(base, per_worker)])

    return invert(pos_flat)


def _sc_gather_rows(table, idx):
    cores, subcores, _ = _sc_workers()
    workers = cores * subcores
    n = idx.shape[0]
    window = SC_GATHER_WINDOW
    per_worker = n // workers
    chunks = per_worker // window
    ring = SC_GATHER_RING
    assert per_worker * workers == n and chunks * window == per_worker and chunks % ring == 0
    row_shape = table.shape[1:]
    mesh = plsc.VectorSubcoreMesh(core_axis_name="core", subcore_axis_name="subcore")

    @functools.partial(
        pl.kernel,
        out_type=jax.ShapeDtypeStruct((n,) + row_shape, table.dtype),
        mesh=mesh,
        scratch_types=[pltpu.VMEM((per_worker,), I32)]
        + [pltpu.VMEM((window,) + row_shape, table.dtype)] * ring
        + [pltpu.SemaphoreType.DMA] * (2 * ring),
        name="sc_gather_rows",
    )
    def gather(table_hbm, idx_hbm, out_hbm, idx_v, *scratch):
        wid = lax.axis_index("subcore") * cores + lax.axis_index("core")
        base = wid * per_worker
        pltpu.sync_copy(idx_hbm.at[pl.ds(base, per_worker)], idx_v)
        _sc_gather_loop(table_hbm, out_hbm, idx_v, base, chunks,
                        scratch[:ring], scratch[ring:2 * ring], scratch[2 * ring:])

    return gather(table, idx)


def _experts_body(first, be_ref, nxt_ref, nxtblk_ref, slot_ref, nu_ref, xs_ref, bgu_ref, bd_ref, wgu_hbm, wd_hbm,
                  *rest):
    ys_ref, wgu_f32, wd_f32, sems = rest[-4:]
    step = pl.program_id(0)
    end = first + pl.num_programs(0) * MOE_SUB
    bm = MOE_BM
    n_used = nu_ref[0]

    def fetch(e, slot):
        return (pltpu.make_async_copy(wgu_hbm.at[e], wgu_f32.at[slot], sems.at[0, slot]),
                pltpu.make_async_copy(wd_hbm.at[e], wd_f32.at[slot], sems.at[1, slot]))

    @pl.when((step == 0) & (first < n_used))
    def _():
        for cp in fetch(be_ref[first], slot_ref[first]):
            cp.start()

    for sub in range(MOE_SUB):
        i = first + step * MOE_SUB + sub
        e = be_ref[i]
        slot = slot_ref[i]
        used = i < n_used
        fresh = (i == first) | (e != be_ref[jnp.maximum(i - 1, 0)])
        rows = pl.ds(sub * bm * PACK_SUB, bm * PACK_SUB)
        xs_sub = xs_ref.at[rows, :]
        ys_sub = ys_ref.at[rows, :]

        @pl.when(used & fresh)
        def _():
            for cp in fetch(e, slot):
                cp.wait()

            @pl.when((nxt_ref[i] >= 0) & (nxtblk_ref[i] < end))
            def _():
                for cp in fetch(nxt_ref[i], 1 - slot):
                    cp.start()

        @pl.when(used)
        def _():
            x = jnp.concatenate(_unpack_rows(xs_sub, bm), axis=1).astype(BF16)
            gu = jnp.dot(x, wgu_f32[slot].astype(BF16), preferred_element_type=F32) + bgu_ref[pl.ds(e, 1), :]
            gate = jnp.minimum(gu[:, 0:D_FF], SWIGLU_LIMIT)
            up = jnp.clip(gu[:, D_FF:2 * D_FF], -SWIGLU_LIMIT, SWIGLU_LIMIT)
            act = ((up + 1.0) * gate * jax.nn.sigmoid(SWIGLU_ALPHA * gate)).astype(BF16)
            y = jnp.dot(act, wd_f32[slot].astype(BF16), preferred_element_type=F32) + bd_ref[pl.ds(e, 1), :]
            _pack_rows(ys_sub, y.astype(BF16).astype(F32))

        @pl.when(jnp.logical_not(used))
        def _():
            ys_sub[...] = jnp.zeros((bm * PACK_SUB, LANES), I32)


def _experts(first, part_blocks, blocks, xs_part, w_gate_up, b_gate_up, w_down, b_down, ys_prev):
    rows = MOE_SUB * MOE_BM * PACK_SUB
    d = D_MODEL
    steps = part_blocks // MOE_SUB
    assert steps * MOE_SUB == part_blocks and first % MOE_SUB == 0

    def x_block(i, be, nx, nb, sl, nu):
        last = jnp.maximum((jnp.minimum(nu[0], first + part_blocks) - 1 - first) // MOE_SUB, 0)
        return jnp.minimum(i, last)

    in_specs = [
        pl.BlockSpec((rows, LANES), lambda i, be, nx, nb, sl, nu: (x_block(i, be, nx, nb, sl, nu), 0)),
        pl.BlockSpec(b_gate_up.shape, lambda i, be, nx, nb, sl, nu: (0, 0)),
        pl.BlockSpec(b_down.shape, lambda i, be, nx, nb, sl, nu: (0, 0)),
        pl.BlockSpec(memory_space=pl.ANY),
        pl.BlockSpec(memory_space=pl.ANY),
    ]
    operands = [blocks[0, :MOE_NB], blocks[1, :MOE_NB], blocks[3, :MOE_NB], blocks[4, :MOE_NB], blocks[2, :1],
                xs_part, b_gate_up, b_down, w_gate_up, w_down]
    aliases = {}
    if ys_prev is not None:
        in_specs.append(pl.BlockSpec(memory_space=pl.ANY))
        aliases = {len(operands): 0}
        operands.append(ys_prev)
    grid_spec = pltpu.PrefetchScalarGridSpec(
        num_scalar_prefetch=5,
        grid=(steps,),
        in_specs=in_specs,
        out_specs=pl.BlockSpec((rows, LANES), lambda i, be, nx, nb, sl, nu: (first // MOE_SUB + i, 0)),
        scratch_shapes=[
            pltpu.VMEM((2, d, 2 * D_FF), F32), pltpu.VMEM((2, D_FF, d), F32),
            pltpu.SemaphoreType.DMA((2, 2)),
        ],
    )
    return pl.pallas_call(
        functools.partial(_experts_body, first),
        grid_spec=grid_spec,
        out_shape=jax.ShapeDtypeStruct((MOE_ROWS * PACK_SUB, LANES), I32),
        input_output_aliases=aliases,
        compiler_params=_params(("arbitrary",), 48),
        name="experts",
    )(*operands)


def _combine_body(gates_ref, x1_ref, mod_ref, gpost_ref, y0_ref, y1_ref, y2_ref, y3_ref, o_ref):
    tm = TM_COMB
    gates = gates_ref[...]
    y_hi = jnp.zeros((tm, PACK_COLS), F32)
    y_lo = jnp.zeros((tm, PACK_COLS), F32)
    for k, yk_ref in enumerate((y0_ref, y1_ref, y2_ref, y3_ref)):
        hi, lo = _unpack_rows(yk_ref, tm)
        y_hi = y_hi + hi * gates[:, k:k + 1]
        y_lo = y_lo + lo * gates[:, k:k + 1]
    y = jnp.concatenate([y_hi, y_lo], axis=1)
    gain = mod_ref[5:6, :] * gpost_ref[...]
    o_ref[...] = x1_ref[...] + _rms(y, gain)


def _combine(gates_t, x1, mod, g_post, y4):
    t, d = x1.shape
    tm = TM_COMB
    tiles = t // tm
    tiles_per_seq = SEQ // tm

    def slab(k):
        return pl.BlockSpec((tm * PACK_SUB, LANES), lambda i: (k * tiles + i, 0))

    return pl.pallas_call(
        _combine_body,
        grid=(tiles,),
        in_specs=[
            pl.BlockSpec((tm, TOP_K), lambda i: (i, 0)),
            pl.BlockSpec((tm, d), lambda i: (i, 0)),
            pl.BlockSpec((None, 6, d), lambda i: (i // tiles_per_seq, 0, 0)),
            pl.BlockSpec(g_post.shape, lambda i: (0, 0)),
            slab(0), slab(1), slab(2), slab(3),
        ],
        out_specs=pl.BlockSpec((tm, d), lambda i: (i, 0)),
        out_shape=jax.ShapeDtypeStruct((t, d), F32),
        compiler_params=_params(("arbitrary",), 48),
        name="combine",
    )(gates_t, x1, mod, g_post, y4, y4, y4, y4)


def _rotary_tables():
    half = ROT_DIM // 2
    inv_freq = ROPE_THETA ** (-2.0 * np.arange(half, dtype=np.float32) / ROT_DIM)
    ang = np.arange(SEQ, dtype=np.float32)[:, None] * inv_freq[None, :].astype(np.float32)
    cos, sin = np.cos(ang), np.sin(ang)
    ones = np.ones((SEQ, ATT_HEAD_DIM - ROT_DIM), np.float32)
    zeros = np.zeros((SEQ, ATT_HEAD_DIM - ROT_DIM), np.float32)
    zh = np.zeros((SEQ, half), np.float32)
    reps = LANES // ATT_HEAD_DIM
    rc = np.tile(np.concatenate([cos, cos, ones], axis=1), (1, reps))
    rm = np.tile(np.concatenate([-sin, zh, zeros], axis=1), (1, reps))
    rp = np.tile(np.concatenate([zh, sin, zeros], axis=1), (1, reps))
    return tuple(jnp.asarray(t, F32) for t in (rc, rm, rp))


def _mixer_inputs(w_in, w_gk_fwd, b_gk_fwd, w_gk_bwd, b_gk_bwd):
    hk = GLA_HEADS * GLA_DK
    hv = GLA_HEADS * GLA_DV
    w = w_in[0]
    o_lr = 2 * hk + 2 * hv
    o_aq = o_lr + 2 * GLA_RANK
    o_ak = o_aq + ATT_Q_HEADS * ATT_HEAD_DIM
    o_av = o_ak + ATT_KV_HEADS * ATT_HEAD_DIM
    hd = ATT_HEAD_DIM
    wa = w[:, :o_lr].astype(BF16)
    wlr = w[:, o_lr:o_aq].astype(BF16)
    dup = lambda m: jnp.concatenate([m[:, g * hd:(g + 1) * hd] for g in range(ATT_KV_HEADS) for _ in range(2)], axis=1)
    wb = jnp.concatenate([w[:, o_aq:o_ak], dup(w[:, o_ak:o_av])], axis=1).astype(BF16)
    wvt = dup(w[:, o_av:o_av + ATT_KV_HEADS * hd]).T.astype(BF16)
    zr = jnp.zeros((GLA_RANK, hk), F32)
    wgk = jnp.concatenate([jnp.concatenate([w_gk_fwd[0], zr], axis=1),
                           jnp.concatenate([zr, w_gk_bwd[0]], axis=1)], axis=0).astype(BF16)
    bgk = jnp.concatenate([b_gk_fwd[0], b_gk_bwd[0]])[None, :]
    return (wa, wlr, wgk, bgk, wb, wvt) + _rotary_tables()


def kernel(x, c, w_ada, b_ada, g_pre_mix, g_post_mix, w_in, w_gk_fwd, b_gk_fwd, w_gk_bwd, b_gk_bwd, g_gla_out,
           attn_sink, w_out, g_pre_ffn, g_post_ffn, w_router, b_router, w_gate_up, b_gate_up, w_down, b_down):
    assert x.shape == (BATCH, SEQ, D_MODEL) and w_ada.shape[0] == 1
    d = D_MODEL
    x2 = x.reshape(TOKENS, d)

    c_pad = jnp.pad(c, ((0, SUBLANES - BATCH), (0, 0)))
    mod = _ada(c_pad, w_ada[0], b_ada)[:BATCH].reshape(BATCH, 6, d)

    mixer_in = _mixer_inputs(w_in, w_gk_fwd, b_gk_fwd, w_gk_bwd, b_gk_bwd)
    q, k, v, gg, laf, lab, aq, ak2, avt = _inproj(x2, mod, g_pre_mix, *mixer_in)
    o_f, o_b = _gla(q, k, v, laf, lab)
    o_att = _attn(attn_sink[0], aq, ak2, avt)

    wr_t = w_router[0].T
    wrh = wr_t.astype(BF16)
    wrl = (wr_t - wrh.astype(F32)).astype(BF16)
    x1, h2_tiles, top_i, gates, rank, counts = _post(
        o_f, o_b, gg, o_att, x2, mod, g_gla_out, g_post_mix, g_pre_ffn, w_out[0].astype(BF16), wrh, wrl,
        b_router[0][:, None])

    pos, blocks = _route(top_i, rank, counts)

    src = _sc_source_rows(pos.reshape(TOP_K * TOKENS), MOE_ROWS)
    h2_rows = h2_tiles.reshape(TOKENS, PACK_SUB, LANES)
    ys = None
    first = 0
    for part_blocks in MOE_PART_BLOCKS:
        row0, n_rows = first * MOE_BM, part_blocks * MOE_BM
        xs_p = _sc_gather_rows(h2_rows, src[row0:row0 + n_rows])
        ys = _experts(first, part_blocks, blocks, xs_p.reshape(n_rows * PACK_SUB, LANES),
                      w_gate_up[0], b_gate_up[0], w_down[0], b_down[0], ys)
        first += part_blocks
    assert first == MOE_NB

    y4 = _sc_gather_rows(ys.reshape(MOE_ROWS, PACK_SUB, LANES), pos.reshape(TOP_K * TOKENS))
    out = _combine(gates.T, x1, mod, g_post_ffn, y4.reshape(TOP_K * TOKENS * PACK_SUB, LANES))
    return out.reshape(BATCH, SEQ, d)
```

```python
import functools

import jax
import jax.numpy as jnp
import numpy as np
from jax import lax
from jax.experimental import pallas as pl
from jax.experimental.pallas import tpu as pltpu
from jax.experimental.pallas import tpu_sc as plsc

F32 = jnp.float32
BF16 = jnp.bfloat16
I32 = jnp.int32

D_MODEL = 1024
BATCH = 2
SEQ = 8192
TOKENS = BATCH * SEQ
GLA_HEADS = 4
GLA_DV = 128
GLA_DK = 64
GLA_RANK = 16
GLA_GATE_NORMALIZER = 16.0
GLA_CHUNK = 64
ATT_Q_HEADS = 8
ATT_KV_HEADS = 2
ATT_HEAD_DIM = 64
ATT_WINDOW = 128
ATT_BLOCK = 128
ROT_DIM = 16
ROPE_THETA = 500000.0
N_EXPERTS = 32
TOP_K = 4
D_FF = 1024
SWIGLU_LIMIT = 7.0
SWIGLU_ALPHA = 1.702
NORM_EPS = 1e-6
NEG_INF = -1e30
LOG2_E = 1.4426950408889634

LANES = 128
SUBLANES = 8
PACK_COLS = D_MODEL // 2
PACK_SUB = PACK_COLS // LANES

TM_IN = 1024
IN_SUB = 4
GLA_GROUP = 16
ATT_GROUP = 16
TM_POST = 1024
POST_SUB = 4
MOE_BM = 256
MOE_ROWS = TOKENS * TOP_K + N_EXPERTS * MOE_BM
MOE_NB = MOE_ROWS // MOE_BM
MOE_SUB = 4
MOE_PART_BLOCKS = (32, 128, 128)
MOE_NB_PAD = ((MOE_NB + LANES - 1) // LANES) * LANES
SC_SCAN_CHUNK = 4096
SC_SCAN_UNROLL = 8
TM_COMB = 512
SC_GATHER_WINDOW = 32
SC_GATHER_RING = 4

NT_DIMS = (((1,), (1,)), ((), ()))
TN_DIMS = (((0,), (0,)), ((), ()))


def _params(semantics, vmem_mib):
    return pltpu.CompilerParams(dimension_semantics=semantics, vmem_limit_bytes=vmem_mib * 1024 * 1024)


def _rms(x, g):
    return x * lax.rsqrt(jnp.mean(x * x, axis=-1, keepdims=True) + NORM_EPS) * g


def _silu(x):
    return x * jax.nn.sigmoid(x)


def _pack_rows(ref, v):
    m = v.shape[0]
    bits = lax.bitcast_convert_type(v, jnp.uint32)
    word = lax.bitcast_convert_type(bits[:, :PACK_COLS] | (bits[:, PACK_COLS:] >> 16), I32)
    for s in range(PACK_SUB):
        ref[pl.ds(s, m, stride=PACK_SUB), :] = word[:, s * LANES:(s + 1) * LANES]


def _unpack_rows(ref, m):
    word = jnp.concatenate([ref[pl.ds(s, m, stride=PACK_SUB), :] for s in range(PACK_SUB)], axis=1)
    bits = lax.bitcast_convert_type(word, jnp.uint32)
    hi = lax.bitcast_convert_type(bits & jnp.uint32(0xFFFF0000), F32)
    lo = lax.bitcast_convert_type(bits << 16, F32)
    return hi, lo


def _ada_body(c_ref, w_ref, b_ref, o_ref):
    ca = _silu(c_ref[...]).astype(BF16)
    o_ref[...] = jnp.dot(ca, w_ref[...].astype(BF16), preferred_element_type=F32) + b_ref[...]


def _ada(c_pad, w_ada, b_ada):
    d = D_MODEL
    return pl.pallas_call(
        _ada_body,
        grid=(6,),
        in_specs=[
            pl.BlockSpec((SUBLANES, d), lambda j: (0, 0)),
            pl.BlockSpec((d, d), lambda j: (0, j)),
            pl.BlockSpec((1, d), lambda j: (0, j)),
        ],
        out_specs=pl.BlockSpec((SUBLANES, d), lambda j: (0, j)),
        out_shape=jax.ShapeDtypeStruct((SUBLANES, 6 * d), F32),
        compiler_params=_params(("arbitrary",), 32),
        name="ada",
    )(c_pad, w_ada, b_ada)


def _rotary(x, cos_t, msin_t, psin_t):
    width = x.shape[1]
    reps = width // LANES
    c = jnp.concatenate([cos_t] * reps, axis=1)
    m = jnp.concatenate([msin_t] * reps, axis=1)
    p = jnp.concatenate([psin_t] * reps, axis=1)
    half = ROT_DIM // 2
    return x * c + pltpu.roll(x, width - half, 1) * m + pltpu.roll(x, half, 1) * p


def _inproj_body(x_ref, mod_ref, g_ref, wa_ref, wlr_ref, wgk_ref, bgk_ref, wb_ref, wvt_ref, rc_ref, rm_ref, rp_ref,
                 q_ref, k_ref, v_ref, gg_ref, laf_ref, lab_ref, aq_ref, ak_ref, avt_ref):
    shift = mod_ref[0:1, :]
    scale = mod_ref[1:2, :]
    hk = GLA_HEADS * GLA_DK
    hv = GLA_HEADS * GLA_DV
    nq = ATT_Q_HEADS * ATT_HEAD_DIM
    nk = 2 * ATT_KV_HEADS * ATT_HEAD_DIM
    sub = x_ref.shape[0] // IN_SUB
    subs = [slice(s * sub, (s + 1) * sub) for s in range(IN_SUB)]

    gain = g_ref[...] * (1.0 + scale)

    def hidden(rows):
        return (_rms(x_ref[rows, :], gain) + shift).astype(BF16)

    def project(h):
        return (jnp.dot(h, wa_ref[...], preferred_element_type=F32),
                jnp.dot(h, wlr_ref[...], preferred_element_type=F32),
                jnp.dot(h, wb_ref[...], preferred_element_type=F32),
                lax.dot_general(wvt_ref[...], h, NT_DIMS, preferred_element_type=F32))

    def finish(rows, pa, plr, pb, pvt):
        q_ref[rows, :] = pa[:, 0:hk] * (GLA_DK ** -0.5)
        k_ref[rows, :] = pa[:, hk:2 * hk]
        v_ref[rows, :] = pa[:, 2 * hk:2 * hk + hv].astype(BF16)
        gg_ref[rows, :] = pa[:, 2 * hk + hv:2 * hk + 2 * hv]
        gk = jnp.dot(plr.astype(BF16), wgk_ref[...], preferred_element_type=F32) + bgk_ref[...]
        la = (jnp.minimum(gk, 0.0) - jnp.log1p(jnp.exp(-jnp.abs(gk)))) * (1.0 / GLA_GATE_NORMALIZER)
        laf_ref[rows, :] = la[:, 0:hk]
        lab_ref[rows, :] = la[:, hk:2 * hk]
        rc, rm, rp = rc_ref[rows, :], rm_ref[rows, :], rp_ref[rows, :]
        aq_ref[rows, :] = (_rotary(pb[:, 0:nq], rc, rm, rp) * (ATT_HEAD_DIM ** -0.5 * LOG2_E)).astype(BF16)
        ak_ref[rows, :] = _rotary(pb[:, nq:nq + nk], rc, rm, rp).astype(BF16)
        avt_ref[:, rows] = pvt.astype(BF16)

    hs = [hidden(rows) for rows in subs]
    ps = [project(h) for h in hs]
    for rows, p in zip(subs, ps):
        finish(rows, *p)


def _inproj(x2, mod, g_pre, wa, wlr, wgk, bgk, wb, wvt, rc, rm, rp):
    t, d = x2.shape
    tm = TM_IN
    tiles_per_seq = SEQ // tm
    hk = GLA_HEADS * GLA_DK
    hv = GLA_HEADS * GLA_DV
    nq = ATT_Q_HEADS * ATT_HEAD_DIM
    nk = 2 * ATT_KV_HEADS * ATT_HEAD_DIM

    def full(a):
        return pl.BlockSpec(a.shape, lambda i: (0,) * a.ndim)

    def rows(w):
        return pl.BlockSpec((tm, w), lambda i: (i, 0))

    def table():
        return pl.BlockSpec((tm, LANES), lambda i: (i % tiles_per_seq, 0))

    out_widths = [(hk, F32), (hk, F32), (hv, BF16), (hv, F32), (hk, F32), (hk, F32), (nq, BF16), (nk, BF16)]
    return pl.pallas_call(
        _inproj_body,
        grid=(t // tm,),
        in_specs=[
            rows(d),
            pl.BlockSpec((None, 6, d), lambda i: (i // tiles_per_seq, 0, 0)),
            full(g_pre), full(wa), full(wlr), full(wgk), full(bgk), full(wb), full(wvt),
            table(), table(), table(),
        ],
        out_specs=[rows(w) for w, _ in out_widths] + [pl.BlockSpec((nk, tm), lambda i: (0, i))],
        out_shape=[jax.ShapeDtypeStruct((t, w), dt) for w, dt in out_widths] + [jax.ShapeDtypeStruct((nk, t), BF16)],
        compiler_params=_params(("arbitrary",), 56),
        name="inproj",
    )(x2, mod, g_pre, wa, wlr, wgk, bgk, wb, wvt, rc, rm, rp)


def _gla_body(qf_ref, kf_ref, vf_ref, laf_ref, qb_ref, kb_ref, vb_ref, lab_ref, of_ref, ob_ref, sf_ref, sb_ref):
    @pl.when(pl.program_id(1) == 0)
    def _():
        sf_ref[...] = jnp.zeros_like(sf_ref)
        sb_ref[...] = jnp.zeros_like(sb_ref)

    c = GLA_CHUNK
    r_i = lax.broadcasted_iota(I32, (c, c), 0)
    c_i = lax.broadcasted_iota(I32, (c, c), 1)
    lower = c_i <= r_i
    upper = c_i >= r_i
    cum_f = jnp.where(lower, 1.0, 0.0).astype(BF16)
    cum_b = jnp.where(upper, 1.0, 0.0).astype(BF16)
    lane = lax.broadcasted_iota(I32, (1, LANES), 1)
    head_masks = (lane < GLA_DK, lane >= GLA_DK)

    fwd = [(qf_ref, kf_ref, laf_ref, vf_ref, of_ref, slice(g * c, (g + 1) * c), cum_f, lower, c - 1, c // 2 - 1)
           for g in range(GLA_GROUP)]
    bwd = [(qb_ref, kb_ref, lab_ref, vb_ref, ob_ref, slice(g * c, (g + 1) * c), cum_b, upper, 0, c // 2)
           for g in reversed(range(GLA_GROUP))]
    heads = range(GLA_HEADS)
    pair = [slice((h // 2) * LANES, (h // 2 + 1) * LANES) for h in heads]
    vcols = [slice(h * GLA_DV, (h + 1) * GLA_DV) for h in heads]

    def stage1(item):
        q_ref, k_ref, la_ref, v_ref, o_ref, rows, cum, tri, i_last, i_mid = item
        la = la_ref[rows, :]
        hi = la.astype(BF16)
        lo = (la - hi.astype(F32)).astype(BF16)
        b = jnp.dot(cum, hi, preferred_element_type=F32) + jnp.dot(cum, lo, preferred_element_type=F32)
        b_last = b[i_last:i_last + 1, :]
        b_mid = b[i_mid:i_mid + 1, :]
        q, k = q_ref[rows, :], k_ref[rows, :]
        return (q * jnp.exp(b - b_mid), (k * jnp.exp(b_mid - b)).astype(BF16), q * jnp.exp(b),
                (k * jnp.exp(b_last - b)).astype(BF16), jnp.exp(b_last))

    def stage2(item, pre):
        v_ref, rows, tri = item[3], item[5], item[7]
        qs, ks, qi, kst, decay = pre
        out = []
        for h in heads:
            mask = head_masks[h % 2]
            qs_h = jnp.where(mask, qs[:, pair[h]], 0.0).astype(BF16)
            sc = lax.dot_general(qs_h, ks[:, pair[h]], NT_DIMS, preferred_element_type=F32)
            v_h = v_ref[rows, vcols[h]]
            kv = lax.dot_general(v_h, kst[:, pair[h]], TN_DIMS, preferred_element_type=F32)
            out.append((jnp.where(tri, sc, 0.0).astype(BF16), kv,
                        jnp.where(mask, qi[:, pair[h]], 0.0).astype(BF16), v_h))
        return out

    def run(items, s_ref):
        pre = [stage1(it) for it in items]
        mid = [stage2(it, p) for it, p in zip(items, pre)]
        states = [s_ref[h] for h in heads]
        for it, p, m in zip(items, pre, mid):
            o_ref, rows, decay = it[4], it[5], p[4]
            for h in heads:
                sc, kv, qi_h, v_h = m[h]
                o = jnp.dot(sc, v_h, preferred_element_type=F32)
                o = o + lax.dot_general(qi_h, states[h].astype(BF16), NT_DIMS, preferred_element_type=F32)
                o_ref[rows, vcols[h]] = o
                states[h] = states[h] * decay[:, pair[h]] + kv
        for h in heads:
            s_ref[h] = states[h]

    run(fwd, sf_ref)
    run(bwd, sb_ref)


def _gla(q, k, v, laf, lab):
    t = q.shape[0]
    rows = GLA_GROUP * GLA_CHUNK
    ng = SEQ // rows
    hk = GLA_HEADS * GLA_DK
    hv = GLA_HEADS * GLA_DV

    def fwd(w):
        return pl.BlockSpec((rows, w), lambda b, n: (b * ng + n, 0))

    def bwd(w):
        return pl.BlockSpec((rows, w), lambda b, n: (b * ng + ng - 1 - n, 0))

    return pl.pallas_call(
        _gla_body,
        grid=(BATCH, ng),
        in_specs=[fwd(hk), fwd(hk), fwd(hv), fwd(hk), bwd(hk), bwd(hk), bwd(hv), bwd(hk)],
        out_specs=[fwd(hv), bwd(hv)],
        out_shape=[jax.ShapeDtypeStruct((t, hv), F32)] * 2,
        scratch_shapes=[pltpu.VMEM((GLA_HEADS, GLA_DV, 2 * GLA_DK), F32)] * 2,
        compiler_params=_params(("arbitrary", "arbitrary"), 32),
        name="gla",
    )(q, k, v, laf, q, k, v, lab)


def _attn_body(sink_ref, q_ref, kp_ref, kc_ref, kn_ref, vp_ref, vc_ref, vn_ref, o_ref):
    step = pl.program_id(1)
    last = pl.num_programs(1) - 1
    qb = ATT_BLOCK
    hd = ATT_HEAD_DIM
    k_all = jnp.concatenate([kp_ref[...], kc_ref[...], kn_ref[...]], axis=0)
    vt_all = jnp.concatenate([vp_ref[...], vc_ref[...], vn_ref[...]], axis=1)
    lane = lax.broadcasted_iota(I32, (1, LANES), 1)
    lo = lane < hd
    j_k = lax.broadcasted_iota(I32, (3 * qb, qb), 0)
    i_q = lax.broadcasted_iota(I32, (3 * qb, qb), 1)
    band = jnp.abs(j_k - qb - i_q) <= ATT_WINDOW
    sinks = [jnp.concatenate([jnp.full((1, qb), sink_ref[4 * g + r] * LOG2_E, F32) for r in range(4)], axis=1)
             for g in range(ATT_KV_HEADS)]
    ones_rows = jnp.ones((SUBLANES, 3 * qb), BF16)
    work = [(j, g) for j in range(ATT_GROUP) for g in range(ATT_KV_HEADS)]

    def scores(j, g):
        valid = band
        if j == 0:
            valid = valid & ((j_k >= qb) | (step > 0))
        if j == ATT_GROUP - 1:
            valid = valid & ((j_k < 2 * qb) | (step < last))
        valid4 = jnp.concatenate([valid] * 4, axis=1)
        rows = slice(j * qb, (j + 1) * qb)
        kg = k_all[j * qb:(j + 3) * qb, g * LANES:(g + 1) * LANES]
        qa = q_ref[rows, (2 * g) * LANES:(2 * g + 1) * LANES]
        qc = q_ref[rows, (2 * g + 1) * LANES:(2 * g + 2) * LANES]
        zero = jnp.zeros_like(qa)
        lhs = jnp.concatenate([jnp.where(lo, qa, zero), jnp.where(lo, zero, qa),
                               jnp.where(lo, qc, zero), jnp.where(lo, zero, qc)], axis=0)
        st = lax.dot_general(kg, lhs, NT_DIMS, preferred_element_type=F32)
        return jnp.concatenate([jnp.where(valid4[0:qb], st[0:qb], NEG_INF), st[qb:2 * qb],
                                jnp.where(valid4[2 * qb:3 * qb], st[2 * qb:3 * qb], NEG_INF)], axis=0)

    def softmax(st, g):
        sink = sinks[g]
        m = jnp.maximum(jnp.max(st, axis=0, keepdims=True), sink)
        return jnp.exp2(st - m).astype(BF16), jnp.exp2(sink - m)

    def output(j, g, p, p_sink):
        rows = slice(j * qb, (j + 1) * qb)
        vgt = vt_all[g * LANES:(g + 1) * LANES, j * qb:(j + 3) * qb]
        res = jnp.dot(jnp.concatenate([vgt, ones_rows], axis=0), p, preferred_element_type=F32)
        ot = res[0:LANES] * (1.0 / (res[LANES:LANES + 1] + p_sink))
        pair_a = jnp.concatenate([ot[0:hd, 0:qb], ot[hd:2 * hd, qb:2 * qb]], axis=0)
        pair_c = jnp.concatenate([ot[0:hd, 2 * qb:3 * qb], ot[hd:2 * hd, 3 * qb:4 * qb]], axis=0)
        o_ref[rows, (2 * g) * LANES:(2 * g + 1) * LANES] = pair_a.T.astype(o_ref.dtype)
        o_ref[rows, (2 * g + 1) * LANES:(2 * g + 2) * LANES] = pair_c.T.astype(o_ref.dtype)

    s_all = [scores(j, g) for j, g in work]
    p_all = [softmax(st, g) for st, (j, g) in zip(s_all, work)]
    for (j, g), (p, p_sink) in zip(work, p_all):
        output(j, g, p, p_sink)


def _attn(sink, aq, ak2, avt):
    t = aq.shape[0]
    qb = ATT_BLOCK
    nb = SEQ // qb
    steps = nb // ATT_GROUP
    nq = ATT_Q_HEADS * ATT_HEAD_DIM
    nk = 2 * ATT_KV_HEADS * ATT_HEAD_DIM

    def edge_block(b, n, shift):
        return b * nb + jnp.clip(n * ATT_GROUP + shift, 0, nb - 1)

    def k_edge(shift):
        return pl.BlockSpec((qb, nk), lambda b, n: (edge_block(b, n, shift), 0))

    def v_edge(shift):
        return pl.BlockSpec((nk, qb), lambda b, n: (0, edge_block(b, n, shift)))

    def group(w):
        return pl.BlockSpec((ATT_GROUP * qb, w), lambda b, n: (b * steps + n, 0))

    v_group = pl.BlockSpec((nk, ATT_GROUP * qb), lambda b, n: (0, b * steps + n))
    return pl.pallas_call(
        _attn_body,
        grid=(BATCH, steps),
        in_specs=[
            pl.BlockSpec(memory_space=pltpu.SMEM),
            group(nq),
            k_edge(-1), group(nk), k_edge(ATT_GROUP), v_edge(-1), v_group, v_edge(ATT_GROUP),
        ],
        out_specs=group(nq),
        out_shape=jax.ShapeDtypeStruct((t, nq), BF16),
        compiler_params=_params(("arbitrary", "arbitrary"), 48),
        name="attn",
    )(sink, aq, ak2, ak2, ak2, avt, avt, avt)


def _post_body(of_ref, ob_ref, gg_ref, oa_ref, x_ref, mod_ref, ggla_ref, gpm_ref, gpf_ref, wout_ref,
               wrh_ref, wrl_ref, br_ref,
               x1_ref, h2_ref, ti_ref, gt_ref, rk_ref, cnt_ref, base_ref):
    tm = TM_POST

    @pl.when(pl.program_id(0) == 0)
    def _():
        base_ref[...] = jnp.zeros_like(base_ref)

    gain1 = mod_ref[2:3, :] * gpm_ref[...]
    shift2 = mod_ref[3:4, :]
    gain2 = gpf_ref[...] * (1.0 + mod_ref[4:5, :])
    sub = tm // POST_SUB
    subs = [slice(s * sub, (s + 1) * sub) for s in range(POST_SUB)]

    def mixer_out(rows):
        og = of_ref[rows, :] + ob_ref[rows, :]
        gg = gg_ref[rows, :]
        parts = []
        for h in range(GLA_HEADS):
            cols = slice(h * GLA_DV, (h + 1) * GLA_DV)
            parts.append((_rms(og[:, cols], ggla_ref[...]) * _silu(gg[:, cols])).astype(BF16))
        return jnp.concatenate(parts + [oa_ref[rows, :]], axis=1)

    def ffn_in(s, y):
        rows = subs[s]
        x1 = x_ref[rows, :] + _rms(y, gain1)
        x1_ref[rows, :] = x1
        h2 = _rms(x1, gain2) + shift2
        hi = h2.astype(BF16)
        hi32 = hi.astype(F32)
        _pack_rows(h2_ref.at[pl.ds(s * sub * PACK_SUB, sub * PACK_SUB), :], hi32)
        return hi, (h2 - hi32).astype(BF16)

    o_subs = [mixer_out(rows) for rows in subs]
    y_subs = [jnp.dot(o, wout_ref[...], preferred_element_type=F32) for o in o_subs]
    split = [ffn_in(s, y) for s, y in enumerate(y_subs)]
    h2_hi = jnp.concatenate([hi for hi, _ in split], axis=0)
    h2_lo = jnp.concatenate([lo for _, lo in split], axis=0)

    wrh = wrh_ref[...]
    logits = (lax.dot_general(wrh, h2_hi, NT_DIMS, preferred_element_type=F32)
              + lax.dot_general(wrh, h2_lo, NT_DIMS, preferred_element_type=F32)
              + lax.dot_general(wrl_ref[...], h2_hi, NT_DIMS, preferred_element_type=F32)
              + br_ref[...])
    e_iota = lax.broadcasted_iota(I32, (N_EXPERTS, tm), 0)
    idxs, vals = [], []
    work = logits
    for _ in range(TOP_K):
        m = jnp.max(work, axis=0, keepdims=True)
        idx = jnp.min(jnp.where(work == m, e_iota, N_EXPERTS), axis=0, keepdims=True)
        idxs.append(idx)
        vals.append(m)
        work = jnp.where(e_iota == idx, -jnp.inf, work)
    exps = [jnp.exp(v - vals[0]) for v in vals]
    inv = 1.0 / (exps[0] + exps[1] + exps[2] + exps[3])
    gt_ref[...] = jnp.concatenate([e * inv for e in exps], axis=0)
    ti_ref[...] = jnp.concatenate(idxs, axis=0)

    onehots = [e_iota == idx for idx in idxs]
    member = jnp.where(onehots[0] | onehots[1] | onehots[2] | onehots[3], 1.0, 0.0)
    t_row = lax.broadcasted_iota(I32, (tm, tm), 0)
    t_col = lax.broadcasted_iota(I32, (tm, tm), 1)
    strict = jnp.where(t_row < t_col, 1.0, 0.0).astype(BF16)
    before = base_ref[...] + jnp.dot(member.astype(BF16), strict, preferred_element_type=F32)
    rk_ref[...] = jnp.concatenate(
        [jnp.sum(jnp.where(oh, before, 0.0), axis=0, keepdims=True) for oh in onehots], axis=0).astype(I32)
    new_base = base_ref[...] + jnp.sum(member, axis=1, keepdims=True)
    base_ref[...] = new_base
    cnt_ref[...] = jnp.broadcast_to(new_base, cnt_ref.shape)


def _post(o_f, o_b, gg, o_att, x2, mod, g_gla, g_pm, g_pf, wout, wrh, wrl, br):
    t, d = x2.shape
    tm = TM_POST
    tiles_per_seq = SEQ // tm
    hv = GLA_HEADS * GLA_DV

    def full(a):
        return pl.BlockSpec(a.shape, lambda i: (0,) * a.ndim)

    def rows(w):
        return pl.BlockSpec((tm, w), lambda i: (i, 0))

    def lanes():
        return pl.BlockSpec((TOP_K, tm), lambda i: (0, i))

    return pl.pallas_call(
        _post_body,
        grid=(t // tm,),
        in_specs=[
            rows(hv), rows(hv), rows(hv), rows(hv), rows(d),
            pl.BlockSpec((None, 6, d), lambda i: (i // tiles_per_seq, 0, 0)),
            full(g_gla), full(g_pm), full(g_pf), full(wout), full(wrh), full(wrl), full(br),
        ],
        out_specs=[
            rows(d),
            pl.BlockSpec((tm * PACK_SUB, LANES), lambda i: (i, 0)),
            lanes(), lanes(), lanes(),
            pl.BlockSpec((N_EXPERTS, LANES), lambda i: (0, 0)),
        ],
        out_shape=[
            jax.ShapeDtypeStruct((t, d), F32),
            jax.ShapeDtypeStruct((t * PACK_SUB, LANES), I32),
            jax.ShapeDtypeStruct((TOP_K, t), I32),
            jax.ShapeDtypeStruct((TOP_K, t), F32),
            jax.ShapeDtypeStruct((TOP_K, t), I32),
            jax.ShapeDtypeStruct((N_EXPERTS, LANES), F32),
        ],
        scratch_shapes=[pltpu.VMEM((N_EXPERTS, 1), F32)],
        compiler_params=_params(("arbitrary",), 48),
        name="post",
    )(o_f, o_b, gg, o_att, x2, mod, g_gla, g_pm, g_pf, wout, wrh, wrl, br)


def _route_body(ti_ref, rk_ref, cnt_ref, pos_ref, blk_ref):
    cnt = cnt_ref[...]
    padded = jnp.floor((cnt + (MOE_BM - 1)) * (1.0 / MOE_BM)) * MOE_BM
    starts, ends = [], []
    acc = jnp.zeros((1, LANES), F32)
    for e in range(N_EXPERTS):
        starts.append(acc)
        acc = acc + padded[e:e + 1, :]
        ends.append(acc)
    ti = ti_ref[...]
    off = jnp.zeros(ti.shape, F32)
    for e in range(N_EXPERTS):
        off = jnp.where(ti == e, starts[e][:, 0:1], off)
    pos_ref[...] = rk_ref[...] + off.astype(I32)

    def owner_of(row):
        n_le = jnp.zeros(row.shape, I32)
        for e in range(N_EXPERTS):
            n_le = n_le + jnp.where(ends[e][:, 0:1] <= row, 1, 0)
        return jnp.minimum(n_le, N_EXPERTS - 1)

    block_start = lax.broadcasted_iota(I32, (1, MOE_NB_PAD), 1).astype(F32) * MOE_BM
    owner = owner_of(block_start)
    nxt = jnp.zeros((1, MOE_NB_PAD), I32)
    nxt_blk = jnp.zeros((1, MOE_NB_PAD), I32)
    slot = jnp.zeros((1, MOE_NB_PAD), I32)
    ordinal = jnp.zeros((1, 1), I32)
    for e in range(N_EXPERTS):
        end_e = ends[e][:, 0:1]
        nxt = jnp.where(owner == e, jnp.where(end_e < acc[:, 0:1], owner_of(end_e), -1), nxt)
        nxt_blk = jnp.where(owner == e, (end_e * (1.0 / MOE_BM)).astype(I32), nxt_blk)
        slot = jnp.where(owner == e, ordinal & 1, slot)
        ordinal = ordinal + jnp.where(padded[e:e + 1, 0:1] > 0, 1, 0)
    used = jnp.broadcast_to((acc[:, 0:1] * (1.0 / MOE_BM)).astype(I32), (1, MOE_NB_PAD))
    blk_ref[...] = jnp.concatenate(
        [owner, nxt, used, nxt_blk, slot, jnp.zeros((SUBLANES - 5, MOE_NB_PAD), I32)], axis=0)


def _route(top_i, rank, counts):
    return pl.pallas_call(
        _route_body,
        out_shape=[
            jax.ShapeDtypeStruct(top_i.shape, I32),
            jax.ShapeDtypeStruct((SUBLANES, MOE_NB_PAD), I32),
        ],
        compiler_params=pltpu.CompilerParams(vmem_limit_bytes=32 * 1024 * 1024),
        name="route",
    )(top_i, rank, counts)


def _sc_workers():
    info = plsc.get_sparse_core_info()
    return info.num_cores, info.num_subcores, info.num_lanes


def _sc_gather_loop(table_hbm, out_hbm, idx_v, base, chunks, bufs, gather_sems, write_sems):
    window = SC_GATHER_WINDOW
    ring = len(bufs)

    def fetch(c, b):
        return pltpu.make_async_copy(table_hbm.at[idx_v.at[pl.ds(c * window, window)]], bufs[b], gather_sems[b])

    def flush(c, b):
        return pltpu.make_async_copy(bufs[b], out_hbm.at[pl.ds(base + c * window, window)], write_sems[b])

    for b in range(ring):
        fetch(b, b).start()

    @pl.loop(0, chunks, step=ring)
    def _(c0):
        for b in range(ring):
            c = c0 + b
            fetch(c, b).wait()
            flush(c, b).start()

            @pl.when(c + ring < chunks)
            def _():
                flush(c, b).wait()
                fetch(c + ring, b).start()

    for b in range(ring):
        flush(chunks - ring + b, b).wait()


def _sc_source_rows(pos_flat, n_rows):
    cores, subcores, lanes = _sc_workers()
    workers = cores * subcores
    per_worker = n_rows // workers
    n_assign = pos_flat.shape[0]
    scan = SC_SCAN_CHUNK
    assert per_worker * workers == n_rows and per_worker % lanes == 0
    assert n_assign % scan == 0 and scan % lanes == 0
    mesh = plsc.VectorSubcoreMesh(core_axis_name="core", subcore_axis_name="subcore")

    @functools.partial(
        pl.kernel,
        out_type=jax.ShapeDtypeStruct((n_rows,), I32),
        mesh=mesh,
        scratch_types=[pltpu.VMEM((per_worker,), I32), pltpu.VMEM((scan,), I32)],
        compiler_params=pltpu.CompilerParams(needs_layout_passes=False),
        name="sc_source_rows",
    )
    def invert(pos_hbm, out_hbm, src_v, pos_v):
        wid = lax.axis_index("subcore") * cores + lax.axis_index("core")
        base = wid * per_worker
        lane = lax.iota(I32, lanes)

        @pl.loop(0, per_worker, step=lanes)
        def _(j):
            src_v[pl.ds(j, lanes)] = (base + j + lane) & (TOKENS - 1)

        @pl.loop(0, n_assign, step=scan)
        def _(a0):
            pltpu.sync_copy(pos_hbm.at[pl.ds(a0, scan)], pos_v)

            @plsc.parallel_loop(0, scan, step=lanes, unroll=SC_SCAN_UNROLL)
            def _(j):
                rel = pos_v[pl.ds(j, lanes)] - base
                mine = (rel >= 0) & (rel < per_worker)
                tok = (a0 + j + lane) & (TOKENS - 1)
                plsc.store_scatter(src_v, [jnp.where(mine, rel, 0)], tok, mask=mine)

        pltpu.sync_copy(src_v, out_hbm.at[pl.ds(base, per_worker)])

    return invert(pos_flat)


def _sc_gather_rows(table, idx):
    cores, subcores, _ = _sc_workers()
    workers = cores * subcores
    n = idx.shape[0]
    window = SC_GATHER_WINDOW
    per_worker = n // workers
    chunks = per_worker // window
    ring = SC_GATHER_RING
    assert per_worker * workers == n and chunks * window == per_worker and chunks % ring == 0
    row_shape = table.shape[1:]
    mesh = plsc.VectorSubcoreMesh(core_axis_name="core", subcore_axis_name="subcore")

    @functools.partial(
        pl.kernel,
        out_type=jax.ShapeDtypeStruct((n,) + row_shape, table.dtype),
        mesh=mesh,
        scratch_types=[pltpu.VMEM((per_worker,), I32)]
        + [pltpu.VMEM((window,) + row_shape, table.dtype)] * ring
        + [pltpu.SemaphoreType.DMA] * (2 * ring),
        name="sc_gather_rows",
    )
    def gather(table_hbm, idx_hbm, out_hbm, idx_v, *scratch):
        wid = lax.axis_index("subcore") * cores + lax.axis_index("core")
        base = wid * per_worker
        pltpu.sync_copy(idx_hbm.at[pl.ds(base, per_worker)], idx_v)
        _sc_gather_loop(table_hbm, out_hbm, idx_v, base, chunks,
                        scratch[:ring], scratch[ring:2 * ring], scratch[2 * ring:])

    return gather(table, idx)


def _experts_body(first, be_ref, nxt_ref, nxtblk_ref, slot_ref, nu_ref, xs_ref, bgu_ref, bd_ref, wgu_hbm, wd_hbm,
                  *rest):
    ys_ref, wgu_f32, wd_f32, sems = rest[-4:]
    step = pl.program_id(0)
    end = first + pl.num_programs(0) * MOE_SUB
    bm = MOE_BM
    n_used = nu_ref[0]

    def fetch(e, slot):
        return (pltpu.make_async_copy(wgu_hbm.at[e], wgu_f32.at[slot], sems.at[0, slot]),
                pltpu.make_async_copy(wd_hbm.at[e], wd_f32.at[slot], sems.at[1, slot]))

    @pl.when((step == 0) & (first < n_used))
    def _():
        for cp in fetch(be_ref[first], slot_ref[first]):
            cp.start()

    for sub in range(MOE_SUB):
        i = first + step * MOE_SUB + sub
        e = be_ref[i]
        slot = slot_ref[i]
        used = i < n_used
        fresh = (i == first) | (e != be_ref[jnp.maximum(i - 1, 0)])
        rows = pl.ds(sub * bm * PACK_SUB, bm * PACK_SUB)
        xs_sub = xs_ref.at[rows, :]
        ys_sub = ys_ref.at[rows, :]

        @pl.when(used & fresh)
        def _():
            for cp in fetch(e, slot):
                cp.wait()

            @pl.when((nxt_ref[i] >= 0) & (nxtblk_ref[i] < end))
            def _():
                for cp in fetch(nxt_ref[i], 1 - slot):
                    cp.start()

        @pl.when(used)
        def _():
            x = jnp.concatenate(_unpack_rows(xs_sub, bm), axis=1).astype(BF16)
            gu = jnp.dot(x, wgu_f32[slot].astype(BF16), preferred_element_type=F32) + bgu_ref[pl.ds(e, 1), :]
            gate = jnp.minimum(gu[:, 0:D_FF], SWIGLU_LIMIT)
            up = jnp.clip(gu[:, D_FF:2 * D_FF], -SWIGLU_LIMIT, SWIGLU_LIMIT)
            act = ((up + 1.0) * gate * jax.nn.sigmoid(SWIGLU_ALPHA * gate)).astype(BF16)
            y = jnp.dot(act, wd_f32[slot].astype(BF16), preferred_element_type=F32) + bd_ref[pl.ds(e, 1), :]
            _pack_rows(ys_sub, y.astype(BF16).astype(F32))

        @pl.when(jnp.logical_not(used))
        def _():
            ys_sub[...] = jnp.zeros((bm * PACK_SUB, LANES), I32)


def _experts(first, part_blocks, blocks, xs_part, w_gate_up, b_gate_up, w_down, b_down, ys_prev):
    rows = MOE_SUB * MOE_BM * PACK_SUB
    d = D_MODEL
    steps = part_blocks // MOE_SUB
    assert steps * MOE_SUB == part_blocks and first % MOE_SUB == 0

    def x_block(i, be, nx, nb, sl, nu):
        last = jnp.maximum((jnp.minimum(nu[0], first + part_blocks) - 1 - first) // MOE_SUB, 0)
        return jnp.minimum(i, last)

    in_specs = [
        pl.BlockSpec((rows, LANES), lambda i, be, nx, nb, sl, nu: (x_block(i, be, nx, nb, sl, nu), 0)),
        pl.BlockSpec(b_gate_up.shape, lambda i, be, nx, nb, sl, nu: (0, 0)),
        pl.BlockSpec(b_down.shape, lambda i, be, nx, nb, sl, nu: (0, 0)),
        pl.BlockSpec(memory_space=pl.ANY),
        pl.BlockSpec(memory_space=pl.ANY),
    ]
    operands = [blocks[0, :MOE_NB], blocks[1, :MOE_NB], blocks[3, :MOE_NB], blocks[4, :MOE_NB], blocks[2, :1],
                xs_part, b_gate_up, b_down, w_gate_up, w_down]
    aliases = {}
    if ys_prev is not None:
        in_specs.append(pl.BlockSpec(memory_space=pl.ANY))
        aliases = {len(operands): 0}
        operands.append(ys_prev)
    grid_spec = pltpu.PrefetchScalarGridSpec(
        num_scalar_prefetch=5,
        grid=(steps,),
        in_specs=in_specs,
        out_specs=pl.BlockSpec((rows, LANES), lambda i, be, nx, nb, sl, nu: (first // MOE_SUB + i, 0)),
        scratch_shapes=[
            pltpu.VMEM((2, d, 2 * D_FF), F32), pltpu.VMEM((2, D_FF, d), F32),
            pltpu.SemaphoreType.DMA((2, 2)),
        ],
    )
    return pl.pallas_call(
        functools.partial(_experts_body, first),
        grid_spec=grid_spec,
        out_shape=jax.ShapeDtypeStruct((MOE_ROWS * PACK_SUB, LANES), I32),
        input_output_aliases=aliases,
        compiler_params=_params(("arbitrary",), 48),
        name="experts",
    )(*operands)


def _combine_body(gates_ref, x1_ref, mod_ref, gpost_ref, y0_ref, y1_ref, y2_ref, y3_ref, o_ref):
    tm = TM_COMB
    gates = jnp.concatenate([gates_ref[...], jnp.zeros((SUBLANES - TOP_K, tm), F32)], axis=0).T
    y_hi = jnp.zeros((tm, PACK_COLS), F32)
    y_lo = jnp.zeros((tm, PACK_COLS), F32)
    for k, yk_ref in enumerate((y0_ref, y1_ref, y2_ref, y3_ref)):
        hi, lo = _unpack_rows(yk_ref, tm)
        y_hi = y_hi + hi * gates[:, k:k + 1]
        y_lo = y_lo + lo * gates[:, k:k + 1]
    y = jnp.concatenate([y_hi, y_lo], axis=1)
    gain = mod_ref[5:6, :] * gpost_ref[...]
    o_ref[...] = x1_ref[...] + _rms(y, gain)


def _combine(gates, x1, mod, g_post, y4):
    t, d = x1.shape
    tm = TM_COMB
    tiles = t // tm
    tiles_per_seq = SEQ // tm

    def slab(k):
        return pl.BlockSpec((tm * PACK_SUB, LANES), lambda i: (k * tiles + i, 0))

    return pl.pallas_call(
        _combine_body,
        grid=(tiles,),
        in_specs=[
            pl.BlockSpec((TOP_K, tm), lambda i: (0, i)),
            pl.BlockSpec((tm, d), lambda i: (i, 0)),
            pl.BlockSpec((None, 6, d), lambda i: (i // tiles_per_seq, 0, 0)),
            pl.BlockSpec(g_post.shape, lambda i: (0, 0)),
            slab(0), slab(1), slab(2), slab(3),
        ],
        out_specs=pl.BlockSpec((tm, d), lambda i: (i, 0)),
        out_shape=jax.ShapeDtypeStruct((t, d), F32),
        compiler_params=_params(("arbitrary",), 48),
        name="combine",
    )(gates, x1, mod, g_post, y4, y4, y4, y4)


def _rotary_tables():
    half = ROT_DIM // 2
    inv_freq = ROPE_THETA ** (-2.0 * np.arange(half, dtype=np.float32) / ROT_DIM)
    ang = np.arange(SEQ, dtype=np.float32)[:, None] * inv_freq[None, :].astype(np.float32)
    cos, sin = np.cos(ang), np.sin(ang)
    ones = np.ones((SEQ, ATT_HEAD_DIM - ROT_DIM), np.float32)
    zeros = np.zeros((SEQ, ATT_HEAD_DIM - ROT_DIM), np.float32)
    zh = np.zeros((SEQ, half), np.float32)
    reps = LANES // ATT_HEAD_DIM
    rc = np.tile(np.concatenate([cos, cos, ones], axis=1), (1, reps))
    rm = np.tile(np.concatenate([-sin, zh, zeros], axis=1), (1, reps))
    rp = np.tile(np.concatenate([zh, sin, zeros], axis=1), (1, reps))
    return tuple(jnp.asarray(t, F32) for t in (rc, rm, rp))


def _mixer_inputs(w_in, w_gk_fwd, b_gk_fwd, w_gk_bwd, b_gk_bwd):
    hk = GLA_HEADS * GLA_DK
    hv = GLA_HEADS * GLA_DV
    w = w_in[0]
    o_lr = 2 * hk + 2 * hv
    o_aq = o_lr + 2 * GLA_RANK
    o_ak = o_aq + ATT_Q_HEADS * ATT_HEAD_DIM
    o_av = o_ak + ATT_KV_HEADS * ATT_HEAD_DIM
    hd = ATT_HEAD_DIM
    wa = w[:, :o_lr].astype(BF16)
    wlr = w[:, o_lr:o_aq].astype(BF16)
    dup = lambda m: jnp.concatenate([m[:, g * hd:(g + 1) * hd] for g in range(ATT_KV_HEADS) for _ in range(2)], axis=1)
    wb = jnp.concatenate([w[:, o_aq:o_ak], dup(w[:, o_ak:o_av])], axis=1).astype(BF16)
    wvt = dup(w[:, o_av:o_av + ATT_KV_HEADS * hd]).T.astype(BF16)
    zr = jnp.zeros((GLA_RANK, hk), F32)
    wgk = jnp.concatenate([jnp.concatenate([w_gk_fwd[0], zr], axis=1),
                           jnp.concatenate([zr, w_gk_bwd[0]], axis=1)], axis=0).astype(BF16)
    bgk = jnp.concatenate([b_gk_fwd[0], b_gk_bwd[0]])[None, :]
    return (wa, wlr, wgk, bgk, wb, wvt) + _rotary_tables()


def kernel(x, c, w_ada, b_ada, g_pre_mix, g_post_mix, w_in, w_gk_fwd, b_gk_fwd, w_gk_bwd, b_gk_bwd, g_gla_out,
           attn_sink, w_out, g_pre_ffn, g_post_ffn, w_router, b_router, w_gate_up, b_gate_up, w_down, b_down):
    assert x.shape == (BATCH, SEQ, D_MODEL) and w_ada.shape[0] == 1
    d = D_MODEL
    x2 = x.reshape(TOKENS, d)

    c_pad = jnp.pad(c, ((0, SUBLANES - BATCH), (0, 0)))
    mod = _ada(c_pad, w_ada[0], b_ada)[:BATCH].reshape(BATCH, 6, d)

    mixer_in = _mixer_inputs(w_in, w_gk_fwd, b_gk_fwd, w_gk_bwd, b_gk_bwd)
    q, k, v, gg, laf, lab, aq, ak2, avt = _inproj(x2, mod, g_pre_mix, *mixer_in)
    o_f, o_b = _gla(q, k, v, laf, lab)
    o_att = _attn(attn_sink[0], aq, ak2, avt)

    wr_t = w_router[0].T
    wrh = wr_t.astype(BF16)
    wrl = (wr_t - wrh.astype(F32)).astype(BF16)
    x1, h2_tiles, top_i, gates, rank, counts = _post(
        o_f, o_b, gg, o_att, x2, mod, g_gla_out, g_post_mix, g_pre_ffn, w_out[0].astype(BF16), wrh, wrl,
        b_router[0][:, None])

    pos, blocks = _route(top_i, rank, counts)

    src = _sc_source_rows(pos.reshape(TOP_K * TOKENS), MOE_ROWS)
    h2_rows = h2_tiles.reshape(TOKENS, PACK_SUB, LANES)
    ys = None
    first = 0
    for part_blocks in MOE_PART_BLOCKS:
        row0, n_rows = first * MOE_BM, part_blocks * MOE_BM
        xs_p = _sc_gather_rows(h2_rows, src[row0:row0 + n_rows])
        ys = _experts(first, part_blocks, blocks, xs_p.reshape(n_rows * PACK_SUB, LANES),
                      w_gate_up[0], b_gate_up[0], w_down[0], b_down[0], ys)
        first += part_blocks
    assert first == MOE_NB

    y4 = _sc_gather_rows(ys.reshape(MOE_ROWS, PACK_SUB, LANES), pos.reshape(TOP_K * TOKENS))
    out = _combine(gates, x1, mod, g_post_ffn, y4.reshape(TOP_K * TOKENS * PACK_SUB, LANES))
    return out.reshape(BATCH, SEQ, d)
```

```python
import functools

import jax
import jax.numpy as jnp
import numpy as np
from jax import lax
from jax.experimental import pallas as pl
from jax.experimental.pallas import tpu as pltpu
from jax.experimental.pallas import tpu_sc as plsc

F32 = jnp.float32
BF16 = jnp.bfloat16
I32 = jnp.int32

D_MODEL = 1024
BATCH = 2
SEQ = 8192
TOKENS = BATCH * SEQ
GLA_HEADS = 4
GLA_DV = 128
GLA_DK = 64
GLA_RANK = 16
GLA_GATE_NORMALIZER = 16.0
GLA_CHUNK = 64
ATT_Q_HEADS = 8
ATT_KV_HEADS = 2
ATT_HEAD_DIM = 64
ATT_WINDOW = 128
ATT_BLOCK = 128
ROT_DIM = 16
ROPE_THETA = 500000.0
N_EXPERTS = 32
TOP_K = 4
D_FF = 1024
SWIGLU_LIMIT = 7.0
SWIGLU_ALPHA = 1.702
NORM_EPS = 1e-6
NEG_INF = -1e30
LOG2_E = 1.4426950408889634

LANES = 128
SUBLANES = 8
PACK_COLS = D_MODEL // 2
PACK_SUB = PACK_COLS // LANES

TM_IN = 1024
IN_SUB = 4
GLA_GROUP = 16
ATT_GROUP = 16
TM_POST = 1024
POST_SUB = 4
MOE_BM = 256
MOE_ROWS = TOKENS * TOP_K + N_EXPERTS * MOE_BM
MOE_NB = MOE_ROWS // MOE_BM
MOE_SUB = 4
MOE_PART_BLOCKS = (32, 128, 128)
MOE_NB_PAD = ((MOE_NB + LANES - 1) // LANES) * LANES
SC_SCAN_CHUNK = 4096
SC_SCAN_UNROLL = 8
TM_COMB = 512
SC_GATHER_WINDOW = 32
SC_GATHER_RING = 4

NT_DIMS = (((1,), (1,)), ((), ()))
TN_DIMS = (((0,), (0,)), ((), ()))


def _params(semantics, vmem_mib):
    return pltpu.CompilerParams(dimension_semantics=semantics, vmem_limit_bytes=vmem_mib * 1024 * 1024)


def _rms(x, g):
    return x * lax.rsqrt(jnp.mean(x * x, axis=-1, keepdims=True) + NORM_EPS) * g


def _silu(x):
    return x * jax.nn.sigmoid(x)


def _pack_rows(ref, v):
    m = v.shape[0]
    bits = lax.bitcast_convert_type(v, jnp.uint32)
    word = lax.bitcast_convert_type(bits[:, :PACK_COLS] | (bits[:, PACK_COLS:] >> 16), I32)
    for s in range(PACK_SUB):
        ref[pl.ds(s, m, stride=PACK_SUB), :] = word[:, s * LANES:(s + 1) * LANES]


def _unpack_rows(ref, m):
    word = jnp.concatenate([ref[pl.ds(s, m, stride=PACK_SUB), :] for s in range(PACK_SUB)], axis=1)
    bits = lax.bitcast_convert_type(word, jnp.uint32)
    hi = lax.bitcast_convert_type(bits & jnp.uint32(0xFFFF0000), F32)
    lo = lax.bitcast_convert_type(bits << 16, F32)
    return hi, lo


def _ada_body(c_ref, w_ref, b_ref, o_ref):
    ca = _silu(c_ref[...]).astype(BF16)
    o_ref[...] = jnp.dot(ca, w_ref[...].astype(BF16), preferred_element_type=F32) + b_ref[...]


def _ada(c_pad, w_ada, b_ada):
    d = D_MODEL
    return pl.pallas_call(
        _ada_body,
        grid=(6,),
        in_specs=[
            pl.BlockSpec((SUBLANES, d), lambda j: (0, 0)),
            pl.BlockSpec((d, d), lambda j: (0, j)),
            pl.BlockSpec((1, d), lambda j: (0, j)),
        ],
        out_specs=pl.BlockSpec((SUBLANES, d), lambda j: (0, j)),
        out_shape=jax.ShapeDtypeStruct((SUBLANES, 6 * d), F32),
        compiler_params=_params(("arbitrary",), 32),
        name="ada",
    )(c_pad, w_ada, b_ada)


def _rotary(x, cos_t, msin_t, psin_t):
    width = x.shape[1]
    reps = width // LANES
    c = jnp.concatenate([cos_t] * reps, axis=1)
    m = jnp.concatenate([msin_t] * reps, axis=1)
    p = jnp.concatenate([psin_t] * reps, axis=1)
    half = ROT_DIM // 2
    return x * c + pltpu.roll(x, width - half, 1) * m + pltpu.roll(x, half, 1) * p


def _inproj_body(x_ref, mod_ref, g_ref, wa_ref, wlr_ref, wgk_ref, bgk_ref, wb_ref, wvt_ref, rc_ref, rm_ref, rp_ref,
                 q_ref, k_ref, v_ref, gg_ref, laf_ref, lab_ref, aq_ref, ak_ref, avt_ref):
    shift = mod_ref[0:1, :]
    scale = mod_ref[1:2, :]
    hk = GLA_HEADS * GLA_DK
    hv = GLA_HEADS * GLA_DV
    nq = ATT_Q_HEADS * ATT_HEAD_DIM
    nk = 2 * ATT_KV_HEADS * ATT_HEAD_DIM
    sub = x_ref.shape[0] // IN_SUB
    subs = [slice(s * sub, (s + 1) * sub) for s in range(IN_SUB)]

    gain = g_ref[...] * (1.0 + scale)

    def hidden(rows):
        return (_rms(x_ref[rows, :], gain) + shift).astype(BF16)

    def project(h):
        return (jnp.dot(h, wa_ref[...], preferred_element_type=F32),
                jnp.dot(h, wlr_ref[...], preferred_element_type=F32),
                jnp.dot(h, wb_ref[...], preferred_element_type=F32),
                lax.dot_general(wvt_ref[...], h, NT_DIMS, preferred_element_type=F32))

    def finish(rows, pa, plr, pb, pvt):
        q_ref[rows, :] = pa[:, 0:hk] * (GLA_DK ** -0.5)
        k_ref[rows, :] = pa[:, hk:2 * hk]
        v_ref[rows, :] = pa[:, 2 * hk:2 * hk + hv].astype(BF16)
        gg_ref[rows, :] = pa[:, 2 * hk + hv:2 * hk + 2 * hv]
        gk = jnp.dot(plr.astype(BF16), wgk_ref[...], preferred_element_type=F32) + bgk_ref[...]
        la = (jnp.minimum(gk, 0.0) - jnp.log1p(jnp.exp(-jnp.abs(gk)))) * (1.0 / GLA_GATE_NORMALIZER)
        laf_ref[rows, :] = la[:, 0:hk]
        lab_ref[rows, :] = la[:, hk:2 * hk]
        rc, rm, rp = rc_ref[rows, :], rm_ref[rows, :], rp_ref[rows, :]
        aq_ref[rows, :] = (_rotary(pb[:, 0:nq], rc, rm, rp) * (ATT_HEAD_DIM ** -0.5 * LOG2_E)).astype(BF16)
        ak_ref[rows, :] = _rotary(pb[:, nq:nq + nk], rc, rm, rp).astype(BF16)
        avt_ref[:, rows] = pvt.astype(BF16)

    hs = [hidden(rows) for rows in subs]
    ps = [project(h) for h in hs]
    for rows, p in zip(subs, ps):
        finish(rows, *p)


def _inproj(x2, mod, g_pre, wa, wlr, wgk, bgk, wb, wvt, rc, rm, rp):
    t, d = x2.shape
    tm = TM_IN
    tiles_per_seq = SEQ // tm
    hk = GLA_HEADS * GLA_DK
    hv = GLA_HEADS * GLA_DV
    nq = ATT_Q_HEADS * ATT_HEAD_DIM
    nk = 2 * ATT_KV_HEADS * ATT_HEAD_DIM

    def full(a):
        return pl.BlockSpec(a.shape, lambda i: (0,) * a.ndim)

    def rows(w):
        return pl.BlockSpec((tm, w), lambda i: (i, 0))

    def table():
        return pl.BlockSpec((tm, LANES), lambda i: (i % tiles_per_seq, 0))

    out_widths = [(hk, F32), (hk, F32), (hv, BF16), (hv, F32), (hk, F32), (hk, F32), (nq, BF16), (nk, BF16)]
    return pl.pallas_call(
        _inproj_body,
        grid=(t // tm,),
        in_specs=[
            rows(d),
            pl.BlockSpec((None, 6, d), lambda i: (i // tiles_per_seq, 0, 0)),
            full(g_pre), full(wa), full(wlr), full(wgk), full(bgk), full(wb), full(wvt),
            table(), table(), table(),
        ],
        out_specs=[rows(w) for w, _ in out_widths] + [pl.BlockSpec((nk, tm), lambda i: (0, i))],
        out_shape=[jax.ShapeDtypeStruct((t, w), dt) for w, dt in out_widths] + [jax.ShapeDtypeStruct((nk, t), BF16)],
        compiler_params=_params(("arbitrary",), 56),
        name="inproj",
    )(x2, mod, g_pre, wa, wlr, wgk, bgk, wb, wvt, rc, rm, rp)


def _gla_body(qf_ref, kf_ref, vf_ref, laf_ref, qb_ref, kb_ref, vb_ref, lab_ref, of_ref, ob_ref, sf_ref, sb_ref):
    @pl.when(pl.program_id(1) == 0)
    def _():
        sf_ref[...] = jnp.zeros_like(sf_ref)
        sb_ref[...] = jnp.zeros_like(sb_ref)

    c = GLA_CHUNK
    r_i = lax.broadcasted_iota(I32, (c, c), 0)
    c_i = lax.broadcasted_iota(I32, (c, c), 1)
    lower = c_i <= r_i
    upper = c_i >= r_i
    cum_f = jnp.where(lower, 1.0, 0.0).astype(BF16)
    cum_b = jnp.where(upper, 1.0, 0.0).astype(BF16)
    lane = lax.broadcasted_iota(I32, (1, LANES), 1)
    head_masks = (lane < GLA_DK, lane >= GLA_DK)

    fwd = [(qf_ref, kf_ref, laf_ref, vf_ref, of_ref, slice(g * c, (g + 1) * c), cum_f, lower, c - 1, c // 2 - 1)
           for g in range(GLA_GROUP)]
    bwd = [(qb_ref, kb_ref, lab_ref, vb_ref, ob_ref, slice(g * c, (g + 1) * c), cum_b, upper, 0, c // 2)
           for g in reversed(range(GLA_GROUP))]
    heads = range(GLA_HEADS)
    pair = [slice((h // 2) * LANES, (h // 2 + 1) * LANES) for h in heads]
    vcols = [slice(h * GLA_DV, (h + 1) * GLA_DV) for h in heads]

    def stage1(item):
        q_ref, k_ref, la_ref, v_ref, o_ref, rows, cum, tri, i_last, i_mid = item
        la = la_ref[rows, :]
        hi = la.astype(BF16)
        lo = (la - hi.astype(F32)).astype(BF16)
        b = jnp.dot(cum, hi, preferred_element_type=F32) + jnp.dot(cum, lo, preferred_element_type=F32)
        b_last = b[i_last:i_last + 1, :]
        b_mid = b[i_mid:i_mid + 1, :]
        q, k = q_ref[rows, :], k_ref[rows, :]
        return (q * jnp.exp(b - b_mid), (k * jnp.exp(b_mid - b)).astype(BF16), q * jnp.exp(b),
                (k * jnp.exp(b_last - b)).astype(BF16), jnp.exp(b_last))

    def stage2(item, pre):
        v_ref, rows, tri = item[3], item[5], item[7]
        qs, ks, qi, kst, decay = pre
        out = []
        for p in range(GLA_HEADS // 2):
            lanes = slice(p * LANES, (p + 1) * LANES)
            qs_p, qi_p = qs[:, lanes], qi[:, lanes]
            qs2 = jnp.concatenate([jnp.where(m, qs_p, 0.0) for m in head_masks], axis=0).astype(BF16)
            sc2 = lax.dot_general(qs2, ks[:, lanes], NT_DIMS, preferred_element_type=F32)
            v2 = v_ref[rows, 2 * p * GLA_DV:(2 * p + 2) * GLA_DV]
            kv2 = lax.dot_general(v2, kst[:, lanes], TN_DIMS, preferred_element_type=F32)
            for j, m in enumerate(head_masks):
                sc = sc2[j * c:(j + 1) * c]
                out.append((jnp.where(tri, sc, 0.0).astype(BF16), kv2[j * GLA_DV:(j + 1) * GLA_DV],
                            jnp.where(m, qi_p, 0.0).astype(BF16), v2[:, j * GLA_DV:(j + 1) * GLA_DV]))
        return out

    def run(items, s_ref):
        pre = [stage1(it) for it in items]
        mid = [stage2(it, p) for it, p in zip(items, pre)]
        states = [s_ref[h] for h in heads]
        for it, p, m in zip(items, pre, mid):
            o_ref, rows, decay = it[4], it[5], p[4]
            for h in heads:
                sc, kv, qi_h, v_h = m[h]
                o = jnp.dot(sc, v_h, preferred_element_type=F32)
                o = o + lax.dot_general(qi_h, states[h].astype(BF16), NT_DIMS, preferred_element_type=F32)
                o_ref[rows, vcols[h]] = o
                states[h] = states[h] * decay[:, pair[h]] + kv
        for h in heads:
            s_ref[h] = states[h]

    run(fwd, sf_ref)
    run(bwd, sb_ref)


def _gla(q, k, v, laf, lab):
    t = q.shape[0]
    rows = GLA_GROUP * GLA_CHUNK
    ng = SEQ // rows
    hk = GLA_HEADS * GLA_DK
    hv = GLA_HEADS * GLA_DV

    def fwd(w):
        return pl.BlockSpec((rows, w), lambda b, n: (b * ng + n, 0))

    def bwd(w):
        return pl.BlockSpec((rows, w), lambda b, n: (b * ng + ng - 1 - n, 0))

    return pl.pallas_call(
        _gla_body,
        grid=(BATCH, ng),
        in_specs=[fwd(hk), fwd(hk), fwd(hv), fwd(hk), bwd(hk), bwd(hk), bwd(hv), bwd(hk)],
        out_specs=[fwd(hv), bwd(hv)],
        out_shape=[jax.ShapeDtypeStruct((t, hv), F32)] * 2,
        scratch_shapes=[pltpu.VMEM((GLA_HEADS, GLA_DV, 2 * GLA_DK), F32)] * 2,
        compiler_params=_params(("arbitrary", "arbitrary"), 32),
        name="gla",
    )(q, k, v, laf, q, k, v, lab)


def _attn_body(sink_ref, q_ref, kp_ref, kc_ref, kn_ref, vp_ref, vc_ref, vn_ref, o_ref):
    step = pl.program_id(1)
    last = pl.num_programs(1) - 1
    qb = ATT_BLOCK
    hd = ATT_HEAD_DIM
    k_all = jnp.concatenate([kp_ref[...], kc_ref[...], kn_ref[...]], axis=0)
    vt_all = jnp.concatenate([vp_ref[...], vc_ref[...], vn_ref[...]], axis=1)
    lane = lax.broadcasted_iota(I32, (1, LANES), 1)
    lo = lane < hd
    j_k = lax.broadcasted_iota(I32, (3 * qb, qb), 0)
    i_q = lax.broadcasted_iota(I32, (3 * qb, qb), 1)
    band = jnp.abs(j_k - qb - i_q) <= ATT_WINDOW
    sinks = [jnp.concatenate([jnp.full((1, qb), sink_ref[4 * g + r] * LOG2_E, F32) for r in range(4)], axis=1)
             for g in range(ATT_KV_HEADS)]
    ones_rows = jnp.ones((SUBLANES, 3 * qb), BF16)
    work = [(j, g) for j in range(ATT_GROUP) for g in range(ATT_KV_HEADS)]

    def scores(j, g):
        valid = band
        if j == 0:
            valid = valid & ((j_k >= qb) | (step > 0))
        if j == ATT_GROUP - 1:
            valid = valid & ((j_k < 2 * qb) | (step < last))
        valid4 = jnp.concatenate([valid] * 4, axis=1)
        rows = slice(j * qb, (j + 1) * qb)
        kg = k_all[j * qb:(j + 3) * qb, g * LANES:(g + 1) * LANES]
        qa = q_ref[rows, (2 * g) * LANES:(2 * g + 1) * LANES]
        qc = q_ref[rows, (2 * g + 1) * LANES:(2 * g + 2) * LANES]
        zero = jnp.zeros_like(qa)
        lhs = jnp.concatenate([jnp.where(lo, qa, zero), jnp.where(lo, zero, qa),
                               jnp.where(lo, qc, zero), jnp.where(lo, zero, qc)], axis=0)
        st = lax.dot_general(kg, lhs, NT_DIMS, preferred_element_type=F32)
        return jnp.concatenate([jnp.where(valid4[0:qb], st[0:qb], NEG_INF), st[qb:2 * qb],
                                jnp.where(valid4[2 * qb:3 * qb], st[2 * qb:3 * qb], NEG_INF)], axis=0)

    def softmax(st, g):
        sink = sinks[g]
        m = jnp.maximum(jnp.max(st, axis=0, keepdims=True), sink)
        return jnp.exp2(st - m).astype(BF16), jnp.exp2(sink - m)

    def output(j, g, p, p_sink):
        rows = slice(j * qb, (j + 1) * qb)
        vgt = vt_all[g * LANES:(g + 1) * LANES, j * qb:(j + 3) * qb]
        res = jnp.dot(jnp.concatenate([vgt, ones_rows], axis=0), p, preferred_element_type=F32)
        ot = res[0:LANES] * (1.0 / (res[LANES:LANES + 1] + p_sink))
        pair_a = jnp.concatenate([ot[0:hd, 0:qb], ot[hd:2 * hd, qb:2 * qb]], axis=0)
        pair_c = jnp.concatenate([ot[0:hd, 2 * qb:3 * qb], ot[hd:2 * hd, 3 * qb:4 * qb]], axis=0)
        o_ref[rows, (2 * g) * LANES:(2 * g + 1) * LANES] = pair_a.T.astype(o_ref.dtype)
        o_ref[rows, (2 * g + 1) * LANES:(2 * g + 2) * LANES] = pair_c.T.astype(o_ref.dtype)

    s_all = [scores(j, g) for j, g in work]
    p_all = [softmax(st, g) for st, (j, g) in zip(s_all, work)]
    for (j, g), (p, p_sink) in zip(work, p_all):
        output(j, g, p, p_sink)


def _attn(sink, aq, ak2, avt):
    t = aq.shape[0]
    qb = ATT_BLOCK
    nb = SEQ // qb
    steps = nb // ATT_GROUP
    nq = ATT_Q_HEADS * ATT_HEAD_DIM
    nk = 2 * ATT_KV_HEADS * ATT_HEAD_DIM

    def edge_block(b, n, shift):
        return b * nb + jnp.clip(n * ATT_GROUP + shift, 0, nb - 1)

    def k_edge(shift):
        return pl.BlockSpec((qb, nk), lambda b, n: (edge_block(b, n, shift), 0))

    def v_edge(shift):
        return pl.BlockSpec((nk, qb), lambda b, n: (0, edge_block(b, n, shift)))

    def group(w):
        return pl.BlockSpec((ATT_GROUP * qb, w), lambda b, n: (b * steps + n, 0))

    v_group = pl.BlockSpec((nk, ATT_GROUP * qb), lambda b, n: (0, b * steps + n))
    return pl.pallas_call(
        _attn_body,
        grid=(BATCH, steps),
        in_specs=[
            pl.BlockSpec(memory_space=pltpu.SMEM),
            group(nq),
            k_edge(-1), group(nk), k_edge(ATT_GROUP), v_edge(-1), v_group, v_edge(ATT_GROUP),
        ],
        out_specs=group(nq),
        out_shape=jax.ShapeDtypeStruct((t, nq), BF16),
        compiler_params=_params(("arbitrary", "arbitrary"), 48),
        name="attn",
    )(sink, aq, ak2, ak2, ak2, avt, avt, avt)


def _post_body(of_ref, ob_ref, gg_ref, oa_ref, x_ref, mod_ref, ggla_ref, gpm_ref, gpf_ref, wout_ref,
               wrh_ref, wrl_ref, br_ref,
               x1_ref, h2_ref, ti_ref, gt_ref, rk_ref, cnt_ref, base_ref):
    tm = TM_POST

    @pl.when(pl.program_id(0) == 0)
    def _():
        base_ref[...] = jnp.zeros_like(base_ref)

    gain1 = mod_ref[2:3, :] * gpm_ref[...]
    shift2 = mod_ref[3:4, :]
    gain2 = gpf_ref[...] * (1.0 + mod_ref[4:5, :])
    sub = tm // POST_SUB
    subs = [slice(s * sub, (s + 1) * sub) for s in range(POST_SUB)]

    def mixer_out(rows):
        og = of_ref[rows, :] + ob_ref[rows, :]
        gg = gg_ref[rows, :]
        parts = []
        for h in range(GLA_HEADS):
            cols = slice(h * GLA_DV, (h + 1) * GLA_DV)
            parts.append((_rms(og[:, cols], ggla_ref[...]) * _silu(gg[:, cols])).astype(BF16))
        return jnp.concatenate(parts + [oa_ref[rows, :]], axis=1)

    def ffn_in(s, y):
        rows = subs[s]
        x1 = x_ref[rows, :] + _rms(y, gain1)
        x1_ref[rows, :] = x1
        h2 = _rms(x1, gain2) + shift2
        hi = h2.astype(BF16)
        hi32 = hi.astype(F32)
        _pack_rows(h2_ref.at[pl.ds(s * sub * PACK_SUB, sub * PACK_SUB), :], hi32)
        return hi, (h2 - hi32).astype(BF16)

    o_subs = [mixer_out(rows) for rows in subs]
    y_subs = [jnp.dot(o, wout_ref[...], preferred_element_type=F32) for o in o_subs]
    split = [ffn_in(s, y) for s, y in enumerate(y_subs)]
    h2_hi = jnp.concatenate([hi for hi, _ in split], axis=0)
    h2_lo = jnp.concatenate([lo for _, lo in split], axis=0)

    wrh = wrh_ref[...]
    logits = (lax.dot_general(wrh, h2_hi, NT_DIMS, preferred_element_type=F32)
              + lax.dot_general(wrh, h2_lo, NT_DIMS, preferred_element_type=F32)
              + lax.dot_general(wrl_ref[...], h2_hi, NT_DIMS, preferred_element_type=F32)
              + br_ref[...])
    e_iota = lax.broadcasted_iota(I32, (N_EXPERTS, tm), 0)
    idxs, vals = [], []
    work = logits
    for _ in range(TOP_K):
        m = jnp.max(work, axis=0, keepdims=True)
        idx = jnp.min(jnp.where(work == m, e_iota, N_EXPERTS), axis=0, keepdims=True)
        idxs.append(idx)
        vals.append(m)
        work = jnp.where(e_iota == idx, -jnp.inf, work)
    exps = [jnp.exp(v - vals[0]) for v in vals]
    inv = 1.0 / (exps[0] + exps[1] + exps[2] + exps[3])
    gt_ref[...] = jnp.concatenate([e * inv for e in exps], axis=0)
    ti_ref[...] = jnp.concatenate(idxs, axis=0)

    onehots = [e_iota == idx for idx in idxs]
    member = jnp.where(onehots[0] | onehots[1] | onehots[2] | onehots[3], 1.0, 0.0)
    t_row = lax.broadcasted_iota(I32, (tm, tm), 0)
    t_col = lax.broadcasted_iota(I32, (tm, tm), 1)
    strict = jnp.where(t_row < t_col, 1.0, 0.0).astype(BF16)
    before = base_ref[...] + jnp.dot(member.astype(BF16), strict, preferred_element_type=F32)
    rk_ref[...] = jnp.concatenate(
        [jnp.sum(jnp.where(oh, before, 0.0), axis=0, keepdims=True) for oh in onehots], axis=0).astype(I32)
    new_base = base_ref[...] + jnp.sum(member, axis=1, keepdims=True)
    base_ref[...] = new_base
    cnt_ref[...] = jnp.broadcast_to(new_base, cnt_ref.shape)


def _post(o_f, o_b, gg, o_att, x2, mod, g_gla, g_pm, g_pf, wout, wrh, wrl, br):
    t, d = x2.shape
    tm = TM_POST
    tiles_per_seq = SEQ // tm
    hv = GLA_HEADS * GLA_DV

    def full(a):
        return pl.BlockSpec(a.shape, lambda i: (0,) * a.ndim)

    def rows(w):
        return pl.BlockSpec((tm, w), lambda i: (i, 0))

    def lanes():
        return pl.BlockSpec((TOP_K, tm), lambda i: (0, i))

    return pl.pallas_call(
        _post_body,
        grid=(t // tm,),
        in_specs=[
            rows(hv), rows(hv), rows(hv), rows(hv), rows(d),
            pl.BlockSpec((None, 6, d), lambda i: (i // tiles_per_seq, 0, 0)),
            full(g_gla), full(g_pm), full(g_pf), full(wout), full(wrh), full(wrl), full(br),
        ],
        out_specs=[
            rows(d),
            pl.BlockSpec((tm * PACK_SUB, LANES), lambda i: (i, 0)),
            lanes(), lanes(), lanes(),
            pl.BlockSpec((N_EXPERTS, LANES), lambda i: (0, 0)),
        ],
        out_shape=[
            jax.ShapeDtypeStruct((t, d), F32),
            jax.ShapeDtypeStruct((t * PACK_SUB, LANES), I32),
            jax.ShapeDtypeStruct((TOP_K, t), I32),
            jax.ShapeDtypeStruct((TOP_K, t), F32),
            jax.ShapeDtypeStruct((TOP_K, t), I32),
            jax.ShapeDtypeStruct((N_EXPERTS, LANES), F32),
        ],
        scratch_shapes=[pltpu.VMEM((N_EXPERTS, 1), F32)],
        compiler_params=_params(("arbitrary",), 48),
        name="post",
    )(o_f, o_b, gg, o_att, x2, mod, g_gla, g_pm, g_pf, wout, wrh, wrl, br)


def _route_body(ti_ref, rk_ref, cnt_ref, pos_ref, blk_ref):
    cnt = cnt_ref[...]
    padded = jnp.floor((cnt + (MOE_BM - 1)) * (1.0 / MOE_BM)) * MOE_BM
    starts, ends = [], []
    acc = jnp.zeros((1, LANES), F32)
    for e in range(N_EXPERTS):
        starts.append(acc)
        acc = acc + padded[e:e + 1, :]
        ends.append(acc)
    ti = ti_ref[...]
    off = jnp.zeros(ti.shape, F32)
    for e in range(N_EXPERTS):
        off = jnp.where(ti == e, starts[e][:, 0:1], off)
    pos_ref[...] = rk_ref[...] + off.astype(I32)

    def owner_of(row):
        n_le = jnp.zeros(row.shape, I32)
        for e in range(N_EXPERTS):
            n_le = n_le + jnp.where(ends[e][:, 0:1] <= row, 1, 0)
        return jnp.minimum(n_le, N_EXPERTS - 1)

    block_start = lax.broadcasted_iota(I32, (1, MOE_NB_PAD), 1).astype(F32) * MOE_BM
    owner = owner_of(block_start)
    nxt = jnp.zeros((1, MOE_NB_PAD), I32)
    nxt_blk = jnp.zeros((1, MOE_NB_PAD), I32)
    slot = jnp.zeros((1, MOE_NB_PAD), I32)
    ordinal = jnp.zeros((1, 1), I32)
    for e in range(N_EXPERTS):
        end_e = ends[e][:, 0:1]
        nxt = jnp.where(owner == e, jnp.where(end_e < acc[:, 0:1], owner_of(end_e), -1), nxt)
        nxt_blk = jnp.where(owner == e, (end_e * (1.0 / MOE_BM)).astype(I32), nxt_blk)
        slot = jnp.where(owner == e, ordinal & 1, slot)
        ordinal = ordinal + jnp.where(padded[e:e + 1, 0:1] > 0, 1, 0)
    used = jnp.broadcast_to((acc[:, 0:1] * (1.0 / MOE_BM)).astype(I32), (1, MOE_NB_PAD))
    blk_ref[...] = jnp.concatenate(
        [owner, nxt, used, nxt_blk, slot, jnp.zeros((SUBLANES - 5, MOE_NB_PAD), I32)], axis=0)


def _route(top_i, rank, counts):
    return pl.pallas_call(
        _route_body,
        out_shape=[
            jax.ShapeDtypeStruct(top_i.shape, I32),
            jax.ShapeDtypeStruct((SUBLANES, MOE_NB_PAD), I32),
        ],
        compiler_params=pltpu.CompilerParams(vmem_limit_bytes=32 * 1024 * 1024),
        name="route",
    )(top_i, rank, counts)


def _sc_workers():
    info = plsc.get_sparse_core_info()
    return info.num_cores, info.num_subcores, info.num_lanes


def _sc_gather_loop(table_hbm, out_hbm, idx_v, base, chunks, bufs, gather_sems, write_sems):
    window = SC_GATHER_WINDOW
    ring = len(bufs)

    def fetch(c, b):
        return pltpu.make_async_copy(table_hbm.at[idx_v.at[pl.ds(c * window, window)]], bufs[b], gather_sems[b])

    def flush(c, b):
        return pltpu.make_async_copy(bufs[b], out_hbm.at[pl.ds(base + c * window, window)], write_sems[b])

    for b in range(ring):
        fetch(b, b).start()

    @pl.loop(0, chunks, step=ring)
    def _(c0):
        for b in range(ring):
            c = c0 + b
            fetch(c, b).wait()
            flush(c, b).start()

            @pl.when(c + ring < chunks)
            def _():
                flush(c, b).wait()
                fetch(c + ring, b).start()

    for b in range(ring):
        flush(chunks - ring + b, b).wait()


def _sc_source_rows(pos_flat, n_rows):
    cores, subcores, lanes = _sc_workers()
    workers = cores * subcores
    per_worker = n_rows // workers
    n_assign = pos_flat.shape[0]
    scan = SC_SCAN_CHUNK
    assert per_worker * workers == n_rows and per_worker % lanes == 0
    assert n_assign % scan == 0 and scan % lanes == 0
    mesh = plsc.VectorSubcoreMesh(core_axis_name="core", subcore_axis_name="subcore")

    @functools.partial(
        pl.kernel,
        out_type=jax.ShapeDtypeStruct((n_rows,), I32),
        mesh=mesh,
        scratch_types=[pltpu.VMEM((per_worker,), I32), pltpu.VMEM((scan,), I32)],
        compiler_params=pltpu.CompilerParams(needs_layout_passes=False),
        name="sc_source_rows",
    )
    def invert(pos_hbm, out_hbm, src_v, pos_v):
        wid = lax.axis_index("subcore") * cores + lax.axis_index("core")
        base = wid * per_worker
        lane = lax.iota(I32, lanes)

        @pl.loop(0, per_worker, step=lanes)
        def _(j):
            src_v[pl.ds(j, lanes)] = (base + j + lane) & (TOKENS - 1)

        @pl.loop(0, n_assign, step=scan)
        def _(a0):
            pltpu.sync_copy(pos_hbm.at[pl.ds(a0, scan)], pos_v)

            @plsc.parallel_loop(0, scan, step=lanes, unroll=SC_SCAN_UNROLL)
            def _(j):
                rel = pos_v[pl.ds(j, lanes)] - base
                mine = (rel >= 0) & (rel < per_worker)
                tok = (a0 + j + lane) & (TOKENS - 1)
                plsc.store_scatter(src_v, [jnp.where(mine, rel, 0)], tok, mask=mine)

        pltpu.sync_copy(src_v, out_hbm.at[pl.ds(base, per_worker)])

    return invert(pos_flat)


def _sc_gather_rows(table, idx):
    cores, subcores, _ = _sc_workers()
    workers = cores * subcores
    n = idx.shape[0]
    window = SC_GATHER_WINDOW
    per_worker = n // workers
    chunks = per_worker // window
    ring = SC_GATHER_RING
    assert per_worker * workers == n and chunks * window == per_worker and chunks % ring == 0
    row_shape = table.shape[1:]
    mesh = plsc.VectorSubcoreMesh(core_axis_name="core", subcore_axis_name="subcore")

    @functools.partial(
        pl.kernel,
        out_type=jax.ShapeDtypeStruct((n,) + row_shape, table.dtype),
        mesh=mesh,
        scratch_types=[pltpu.VMEM((per_worker,), I32)]
        + [pltpu.VMEM((window,) + row_shape, table.dtype)] * ring
        + [pltpu.SemaphoreType.DMA] * (2 * ring),
        name="sc_gather_rows",
    )
    def gather(table_hbm, idx_hbm, out_hbm, idx_v, *scratch):
        wid = lax.axis_index("subcore") * cores + lax.axis_index("core")
        base = wid * per_worker
        pltpu.sync_copy(idx_hbm.at[pl.ds(base, per_worker)], idx_v)
        _sc_gather_loop(table_hbm, out_hbm, idx_v, base, chunks,
                        scratch[:ring], scratch[ring:2 * ring], scratch[2 * ring:])

    return gather(table, idx)


def _experts_body(first, be_ref, nxt_ref, nxtblk_ref, slot_ref, nu_ref, xs_ref, bgu_ref, bd_ref, wgu_hbm, wd_hbm,
                  *rest):
    ys_ref, wgu_f32, wd_f32, sems = rest[-4:]
    step = pl.program_id(0)
    end = first + pl.num_programs(0) * MOE_SUB
    bm = MOE_BM
    n_used = nu_ref[0]

    def fetch(e, slot):
        return (pltpu.make_async_copy(wgu_hbm.at[e], wgu_f32.at[slot], sems.at[0, slot]),
                pltpu.make_async_copy(wd_hbm.at[e], wd_f32.at[slot], sems.at[1, slot]))

    @pl.when((step == 0) & (first < n_used))
    def _():
        for cp in fetch(be_ref[first], slot_ref[first]):
            cp.start()

    for sub in range(MOE_SUB):
        i = first + step * MOE_SUB + sub
        e = be_ref[i]
        slot = slot_ref[i]
        used = i < n_used
        fresh = (i == first) | (e != be_ref[jnp.maximum(i - 1, 0)])
        rows = pl.ds(sub * bm * PACK_SUB, bm * PACK_SUB)
        xs_sub = xs_ref.at[rows, :]
        ys_sub = ys_ref.at[rows, :]

        @pl.when(used & fresh)
        def _():
            for cp in fetch(e, slot):
                cp.wait()

            @pl.when((nxt_ref[i] >= 0) & (nxtblk_ref[i] < end))
            def _():
                for cp in fetch(nxt_ref[i], 1 - slot):
                    cp.start()

        @pl.when(used)
        def _():
            x = jnp.concatenate(_unpack_rows(xs_sub, bm), axis=1).astype(BF16)
            gu = jnp.dot(x, wgu_f32[slot].astype(BF16), preferred_element_type=F32) + bgu_ref[pl.ds(e, 1), :]
            gate = jnp.minimum(gu[:, 0:D_FF], SWIGLU_LIMIT)
            up = jnp.clip(gu[:, D_FF:2 * D_FF], -SWIGLU_LIMIT, SWIGLU_LIMIT)
            act = ((up + 1.0) * gate * jax.nn.sigmoid(SWIGLU_ALPHA * gate)).astype(BF16)
            y = jnp.dot(act, wd_f32[slot].astype(BF16), preferred_element_type=F32) + bd_ref[pl.ds(e, 1), :]
            _pack_rows(ys_sub, y.astype(BF16).astype(F32))

        @pl.when(jnp.logical_not(used))
        def _():
            ys_sub[...] = jnp.zeros((bm * PACK_SUB, LANES), I32)


def _experts(first, part_blocks, blocks, xs_part, w_gate_up, b_gate_up, w_down, b_down, ys_prev):
    rows = MOE_SUB * MOE_BM * PACK_SUB
    d = D_MODEL
    steps = part_blocks // MOE_SUB
    assert steps * MOE_SUB == part_blocks and first % MOE_SUB == 0

    def x_block(i, be, nx, nb, sl, nu):
        last = jnp.maximum((jnp.minimum(nu[0], first + part_blocks) - 1 - first) // MOE_SUB, 0)
        return jnp.minimum(i, last)

    in_specs = [
        pl.BlockSpec((rows, LANES), lambda i, be, nx, nb, sl, nu: (x_block(i, be, nx, nb, sl, nu), 0)),
        pl.BlockSpec(b_gate_up.shape, lambda i, be, nx, nb, sl, nu: (0, 0)),
        pl.BlockSpec(b_down.shape, lambda i, be, nx, nb, sl, nu: (0, 0)),
        pl.BlockSpec(memory_space=pl.ANY),
        pl.BlockSpec(memory_space=pl.ANY),
    ]
    operands = [blocks[0, :MOE_NB], blocks[1, :MOE_NB], blocks[3, :MOE_NB], blocks[4, :MOE_NB], blocks[2, :1],
                xs_part, b_gate_up, b_down, w_gate_up, w_down]
    aliases = {}
    if ys_prev is not None:
        in_specs.append(pl.BlockSpec(memory_space=pl.ANY))
        aliases = {len(operands): 0}
        operands.append(ys_prev)
    grid_spec = pltpu.PrefetchScalarGridSpec(
        num_scalar_prefetch=5,
        grid=(steps,),
        in_specs=in_specs,
        out_specs=pl.BlockSpec((rows, LANES), lambda i, be, nx, nb, sl, nu: (first // MOE_SUB + i, 0)),
        scratch_shapes=[
            pltpu.VMEM((2, d, 2 * D_FF), F32), pltpu.VMEM((2, D_FF, d), F32),
            pltpu.SemaphoreType.DMA((2, 2)),
        ],
    )
    return pl.pallas_call(
        functools.partial(_experts_body, first),
        grid_spec=grid_spec,
        out_shape=jax.ShapeDtypeStruct((MOE_ROWS * PACK_SUB, LANES), I32),
        input_output_aliases=aliases,
        compiler_params=_params(("arbitrary",), 48),
        name="experts",
    )(*operands)


def _combine_body(gates_ref, x1_ref, mod_ref, gpost_ref, y0_ref, y1_ref, y2_ref, y3_ref, o_ref):
    tm = TM_COMB
    gates = jnp.concatenate([gates_ref[...], jnp.zeros((SUBLANES - TOP_K, tm), F32)], axis=0).T
    y_hi = jnp.zeros((tm, PACK_COLS), F32)
    y_lo = jnp.zeros((tm, PACK_COLS), F32)
    for k, yk_ref in enumerate((y0_ref, y1_ref, y2_ref, y3_ref)):
        hi, lo = _unpack_rows(yk_ref, tm)
        y_hi = y_hi + hi * gates[:, k:k + 1]
        y_lo = y_lo + lo * gates[:, k:k + 1]
    y = jnp.concatenate([y_hi, y_lo], axis=1)
    gain = mod_ref[5:6, :] * gpost_ref[...]
    o_ref[...] = x1_ref[...] + _rms(y, gain)


def _combine(gates, x1, mod, g_post, y4):
    t, d = x1.shape
    tm = TM_COMB
    tiles = t // tm
    tiles_per_seq = SEQ // tm

    def slab(k):
        return pl.BlockSpec((tm * PACK_SUB, LANES), lambda i: (k * tiles + i, 0))

    return pl.pallas_call(
        _combine_body,
        grid=(tiles,),
        in_specs=[
            pl.BlockSpec((TOP_K, tm), lambda i: (0, i)),
            pl.BlockSpec((tm, d), lambda i: (i, 0)),
            pl.BlockSpec((None, 6, d), lambda i: (i // tiles_per_seq, 0, 0)),
            pl.BlockSpec(g_post.shape, lambda i: (0, 0)),
            slab(0), slab(1), slab(2), slab(3),
        ],
        out_specs=pl.BlockSpec((tm, d), lambda i: (i, 0)),
        out_shape=jax.ShapeDtypeStruct((t, d), F32),
        compiler_params=_params(("arbitrary",), 48),
        name="combine",
    )(gates, x1, mod, g_post, y4, y4, y4, y4)


def _rotary_tables():
    half = ROT_DIM // 2
    inv_freq = ROPE_THETA ** (-2.0 * np.arange(half, dtype=np.float32) / ROT_DIM)
    ang = np.arange(SEQ, dtype=np.float32)[:, None] * inv_freq[None, :].astype(np.float32)
    cos, sin = np.cos(ang), np.sin(ang)
    ones = np.ones((SEQ, ATT_HEAD_DIM - ROT_DIM), np.float32)
    zeros = np.zeros((SEQ, ATT_HEAD_DIM - ROT_DIM), np.float32)
    zh = np.zeros((SEQ, half), np.float32)
    reps = LANES // ATT_HEAD_DIM
    rc = np.tile(np.concatenate([cos, cos, ones], axis=1), (1, reps))
    rm = np.tile(np.concatenate([-sin, zh, zeros], axis=1), (1, reps))
    rp = np.tile(np.concatenate([zh, sin, zeros], axis=1), (1, reps))
    return tuple(jnp.asarray(t, F32) for t in (rc, rm, rp))


def _mixer_inputs(w_in, w_gk_fwd, b_gk_fwd, w_gk_bwd, b_gk_bwd):
    hk = GLA_HEADS * GLA_DK
    hv = GLA_HEADS * GLA_DV
    w = w_in[0]
    o_lr = 2 * hk + 2 * hv
    o_aq = o_lr + 2 * GLA_RANK
    o_ak = o_aq + ATT_Q_HEADS * ATT_HEAD_DIM
    o_av = o_ak + ATT_KV_HEADS * ATT_HEAD_DIM
    hd = ATT_HEAD_DIM
    wa = w[:, :o_lr].astype(BF16)
    wlr = w[:, o_lr:o_aq].astype(BF16)
    dup = lambda m: jnp.concatenate([m[:, g * hd:(g + 1) * hd] for g in range(ATT_KV_HEADS) for _ in range(2)], axis=1)
    wb = jnp.concatenate([w[:, o_aq:o_ak], dup(w[:, o_ak:o_av])], axis=1).astype(BF16)
    wvt = dup(w[:, o_av:o_av + ATT_KV_HEADS * hd]).T.astype(BF16)
    zr = jnp.zeros((GLA_RANK, hk), F32)
    wgk = jnp.concatenate([jnp.concatenate([w_gk_fwd[0], zr], axis=1),
                           jnp.concatenate([zr, w_gk_bwd[0]], axis=1)], axis=0).astype(BF16)
    bgk = jnp.concatenate([b_gk_fwd[0], b_gk_bwd[0]])[None, :]
    return (wa, wlr, wgk, bgk, wb, wvt) + _rotary_tables()


def kernel(x, c, w_ada, b_ada, g_pre_mix, g_post_mix, w_in, w_gk_fwd, b_gk_fwd, w_gk_bwd, b_gk_bwd, g_gla_out,
           attn_sink, w_out, g_pre_ffn, g_post_ffn, w_router, b_router, w_gate_up, b_gate_up, w_down, b_down):
    assert x.shape == (BATCH, SEQ, D_MODEL) and w_ada.shape[0] == 1
    d = D_MODEL
    x2 = x.reshape(TOKENS, d)

    c_pad = jnp.pad(c, ((0, SUBLANES - BATCH), (0, 0)))
    mod = _ada(c_pad, w_ada[0], b_ada)[:BATCH].reshape(BATCH, 6, d)

    mixer_in = _mixer_inputs(w_in, w_gk_fwd, b_gk_fwd, w_gk_bwd, b_gk_bwd)
    q, k, v, gg, laf, lab, aq, ak2, avt = _inproj(x2, mod, g_pre_mix, *mixer_in)
    o_f, o_b = _gla(q, k, v, laf, lab)
    o_att = _attn(attn_sink[0], aq, ak2, avt)

    wr_t = w_router[0].T
    wrh = wr_t.astype(BF16)
    wrl = (wr_t - wrh.astype(F32)).astype(BF16)
    x1, h2_tiles, top_i, gates, rank, counts = _post(
        o_f, o_b, gg, o_att, x2, mod, g_gla_out, g_post_mix, g_pre_ffn, w_out[0].astype(BF16), wrh, wrl,
        b_router[0][:, None])

    pos, blocks = _route(top_i, rank, counts)

    src = _sc_source_rows(pos.reshape(TOP_K * TOKENS), MOE_ROWS)
    h2_rows = h2_tiles.reshape(TOKENS, PACK_SUB, LANES)
    ys = None
    first = 0
    for part_blocks in MOE_PART_BLOCKS:
        row0, n_rows = first * MOE_BM, part_blocks * MOE_BM
        xs_p = _sc_gather_rows(h2_rows, src[row0:row0 + n_rows])
        ys = _experts(first, part_blocks, blocks, xs_p.reshape(n_rows * PACK_SUB, LANES),
                      w_gate_up[0], b_gate_up[0], w_down[0], b_down[0], ys)
        first += part_blocks
    assert first == MOE_NB

    y4 = _sc_gather_rows(ys.reshape(MOE_ROWS, PACK_SUB, LANES), pos.reshape(TOP_K * TOKENS))
    out = _combine(gates, x1, mod, g_post_ffn, y4.reshape(TOP_K * TOKENS * PACK_SUB, LANES))
    return out.reshape(BATCH, SEQ, d)
```

```python
import functools

import jax
import jax.numpy as jnp
import numpy as np
from jax import lax
from jax.experimental import pallas as pl
from jax.experimental.pallas import tpu as pltpu
from jax.experimental.pallas import tpu_sc as plsc

F32 = jnp.float32
BF16 = jnp.bfloat16
I32 = jnp.int32

D_MODEL = 1024
BATCH = 2
SEQ = 8192
TOKENS = BATCH * SEQ
GLA_HEADS = 4
GLA_DV = 128
GLA_DK = 64
GLA_RANK = 16
GLA_GATE_NORMALIZER = 16.0
GLA_CHUNK = 64
ATT_Q_HEADS = 8
ATT_KV_HEADS = 2
ATT_HEAD_DIM = 64
ATT_WINDOW = 128
ATT_BLOCK = 128
ROT_DIM = 16
ROPE_THETA = 500000.0
N_EXPERTS = 32
TOP_K = 4
D_FF = 1024
SWIGLU_LIMIT = 7.0
SWIGLU_ALPHA = 1.702
NORM_EPS = 1e-6
NEG_INF = -1e30
LOG2_E = 1.4426950408889634

LANES = 128
SUBLANES = 8
PACK_COLS = D_MODEL // 2
PACK_SUB = PACK_COLS // LANES

TM_IN = 1024
IN_SUB = 4
GLA_GROUP = 16
ATT_GROUP = 16
TM_POST = 1024
POST_SUB = 4
MOE_BM = 256
MOE_ROWS = TOKENS * TOP_K + N_EXPERTS * MOE_BM
MOE_NB = MOE_ROWS // MOE_BM
MOE_SUB = 4
MOE_PART_BLOCKS = (32, 128, 128)
MOE_NB_PAD = ((MOE_NB + LANES - 1) // LANES) * LANES
SC_SCAN_CHUNK = 4096
SC_SCAN_UNROLL = 8
TM_COMB = 1024
SC_GATHER_WINDOW = 32
SC_GATHER_RING = 4

NT_DIMS = (((1,), (1,)), ((), ()))
TN_DIMS = (((0,), (0,)), ((), ()))


def _params(semantics, vmem_mib):
    return pltpu.CompilerParams(dimension_semantics=semantics, vmem_limit_bytes=vmem_mib * 1024 * 1024)


def _rms(x, g):
    return x * lax.rsqrt(jnp.mean(x * x, axis=-1, keepdims=True) + NORM_EPS) * g


def _silu(x):
    return x * jax.nn.sigmoid(x)


def _pack_rows(ref, v):
    m = v.shape[0]
    bits = lax.bitcast_convert_type(v, jnp.uint32)
    word = lax.bitcast_convert_type(bits[:, :PACK_COLS] | (bits[:, PACK_COLS:] >> 16), I32)
    for s in range(PACK_SUB):
        ref[pl.ds(s, m, stride=PACK_SUB), :] = word[:, s * LANES:(s + 1) * LANES]


def _unpack_rows(ref, m):
    word = jnp.concatenate([ref[pl.ds(s, m, stride=PACK_SUB), :] for s in range(PACK_SUB)], axis=1)
    bits = lax.bitcast_convert_type(word, jnp.uint32)
    hi = lax.bitcast_convert_type(bits & jnp.uint32(0xFFFF0000), F32)
    lo = lax.bitcast_convert_type(bits << 16, F32)
    return hi, lo


def _ada_body(c_ref, w_ref, b_ref, o_ref):
    ca = _silu(c_ref[...]).astype(BF16)
    o_ref[...] = jnp.dot(ca, w_ref[...].astype(BF16), preferred_element_type=F32) + b_ref[...]


def _ada(c_pad, w_ada, b_ada):
    d = D_MODEL
    return pl.pallas_call(
        _ada_body,
        grid=(6,),
        in_specs=[
            pl.BlockSpec((SUBLANES, d), lambda j: (0, 0)),
            pl.BlockSpec((d, d), lambda j: (0, j)),
            pl.BlockSpec((1, d), lambda j: (0, j)),
        ],
        out_specs=pl.BlockSpec((SUBLANES, d), lambda j: (0, j)),
        out_shape=jax.ShapeDtypeStruct((SUBLANES, 6 * d), F32),
        compiler_params=_params(("arbitrary",), 32),
        name="ada",
    )(c_pad, w_ada, b_ada)


def _rotary(x, cos_t, msin_t, psin_t):
    width = x.shape[1]
    reps = width // LANES
    c = jnp.concatenate([cos_t] * reps, axis=1)
    m = jnp.concatenate([msin_t] * reps, axis=1)
    p = jnp.concatenate([psin_t] * reps, axis=1)
    half = ROT_DIM // 2
    return x * c + pltpu.roll(x, width - half, 1) * m + pltpu.roll(x, half, 1) * p


def _inproj_body(x_ref, mod_ref, g_ref, wa_ref, wlr_ref, wgk_ref, bgk_ref, wb_ref, wvt_ref, rc_ref, rm_ref, rp_ref,
                 q_ref, k_ref, v_ref, gg_ref, laf_ref, lab_ref, aq_ref, ak_ref, avt_ref):
    shift = mod_ref[0:1, :]
    scale = mod_ref[1:2, :]
    hk = GLA_HEADS * GLA_DK
    hv = GLA_HEADS * GLA_DV
    nq = ATT_Q_HEADS * ATT_HEAD_DIM
    nk = 2 * ATT_KV_HEADS * ATT_HEAD_DIM
    sub = x_ref.shape[0] // IN_SUB
    subs = [slice(s * sub, (s + 1) * sub) for s in range(IN_SUB)]

    gain = g_ref[...] * (1.0 + scale)

    def hidden(rows):
        return (_rms(x_ref[rows, :], gain) + shift).astype(BF16)

    def project(h):
        return (jnp.dot(h, wa_ref[...], preferred_element_type=F32),
                jnp.dot(h, wlr_ref[...], preferred_element_type=F32),
                jnp.dot(h, wb_ref[...], preferred_element_type=F32),
                lax.dot_general(wvt_ref[...], h, NT_DIMS, preferred_element_type=F32))

    def finish(rows, pa, plr, pb, pvt):
        q_ref[rows, :] = pa[:, 0:hk] * (GLA_DK ** -0.5)
        k_ref[rows, :] = pa[:, hk:2 * hk]
        v_ref[rows, :] = pa[:, 2 * hk:2 * hk + hv].astype(BF16)
        gg_ref[rows, :] = pa[:, 2 * hk + hv:2 * hk + 2 * hv]
        gk = jnp.dot(plr.astype(BF16), wgk_ref[...], preferred_element_type=F32) + bgk_ref[...]
        la = (jnp.minimum(gk, 0.0) - jnp.log1p(jnp.exp(-jnp.abs(gk)))) * (1.0 / GLA_GATE_NORMALIZER)
        laf_ref[rows, :] = la[:, 0:hk]
        lab_ref[rows, :] = la[:, hk:2 * hk]
        rc, rm, rp = rc_ref[rows, :], rm_ref[rows, :], rp_ref[rows, :]
        aq_ref[rows, :] = (_rotary(pb[:, 0:nq], rc, rm, rp) * (ATT_HEAD_DIM ** -0.5 * LOG2_E)).astype(BF16)
        ak_ref[rows, :] = _rotary(pb[:, nq:nq + nk], rc, rm, rp).astype(BF16)
        avt_ref[:, rows] = pvt.astype(BF16)

    hs = [hidden(rows) for rows in subs]
    ps = [project(h) for h in hs]
    for rows, p in zip(subs, ps):
        finish(rows, *p)


def _inproj(x2, mod, g_pre, wa, wlr, wgk, bgk, wb, wvt, rc, rm, rp):
    t, d = x2.shape
    tm = TM_IN
    tiles_per_seq = SEQ // tm
    hk = GLA_HEADS * GLA_DK
    hv = GLA_HEADS * GLA_DV
    nq = ATT_Q_HEADS * ATT_HEAD_DIM
    nk = 2 * ATT_KV_HEADS * ATT_HEAD_DIM

    def full(a):
        return pl.BlockSpec(a.shape, lambda i: (0,) * a.ndim)

    def rows(w):
        return pl.BlockSpec((tm, w), lambda i: (i, 0))

    def table():
        return pl.BlockSpec((tm, LANES), lambda i: (i % tiles_per_seq, 0))

    out_widths = [(hk, F32), (hk, F32), (hv, BF16), (hv, F32), (hk, F32), (hk, F32), (nq, BF16), (nk, BF16)]
    return pl.pallas_call(
        _inproj_body,
        grid=(t // tm,),
        in_specs=[
            rows(d),
            pl.BlockSpec((None, 6, d), lambda i: (i // tiles_per_seq, 0, 0)),
            full(g_pre), full(wa), full(wlr), full(wgk), full(bgk), full(wb), full(wvt),
            table(), table(), table(),
        ],
        out_specs=[rows(w) for w, _ in out_widths] + [pl.BlockSpec((nk, tm), lambda i: (0, i))],
        out_shape=[jax.ShapeDtypeStruct((t, w), dt) for w, dt in out_widths] + [jax.ShapeDtypeStruct((nk, t), BF16)],
        compiler_params=_params(("arbitrary",), 56),
        name="inproj",
    )(x2, mod, g_pre, wa, wlr, wgk, bgk, wb, wvt, rc, rm, rp)


def _gla_body(qf_ref, kf_ref, vf_ref, laf_ref, qb_ref, kb_ref, vb_ref, lab_ref, of_ref, ob_ref, sf_ref, sb_ref):
    @pl.when(pl.program_id(1) == 0)
    def _():
        sf_ref[...] = jnp.zeros_like(sf_ref)
        sb_ref[...] = jnp.zeros_like(sb_ref)

    c = GLA_CHUNK
    r_i = lax.broadcasted_iota(I32, (c, c), 0)
    c_i = lax.broadcasted_iota(I32, (c, c), 1)
    lower = c_i <= r_i
    upper = c_i >= r_i
    cum_f = jnp.where(lower, 1.0, 0.0).astype(BF16)
    cum_b = jnp.where(upper, 1.0, 0.0).astype(BF16)
    lane = lax.broadcasted_iota(I32, (1, LANES), 1)
    head_masks = (lane < GLA_DK, lane >= GLA_DK)

    fwd = [(qf_ref, kf_ref, laf_ref, vf_ref, of_ref, slice(g * c, (g + 1) * c), cum_f, lower, c - 1, c // 2 - 1)
           for g in range(GLA_GROUP)]
    bwd = [(qb_ref, kb_ref, lab_ref, vb_ref, ob_ref, slice(g * c, (g + 1) * c), cum_b, upper, 0, c // 2)
           for g in reversed(range(GLA_GROUP))]
    heads = range(GLA_HEADS)
    pair = [slice((h // 2) * LANES, (h // 2 + 1) * LANES) for h in heads]
    vcols = [slice(h * GLA_DV, (h + 1) * GLA_DV) for h in heads]

    def stage1(item):
        q_ref, k_ref, la_ref, v_ref, o_ref, rows, cum, tri, i_last, i_mid = item
        la = la_ref[rows, :]
        hi = la.astype(BF16)
        lo = (la - hi.astype(F32)).astype(BF16)
        b = jnp.dot(cum, hi, preferred_element_type=F32) + jnp.dot(cum, lo, preferred_element_type=F32)
        b_last = b[i_last:i_last + 1, :]
        b_mid = b[i_mid:i_mid + 1, :]
        q, k = q_ref[rows, :], k_ref[rows, :]
        return (q * jnp.exp(b - b_mid), (k * jnp.exp(b_mid - b)).astype(BF16), q * jnp.exp(b),
                (k * jnp.exp(b_last - b)).astype(BF16), jnp.exp(b_last))

    def stage2(item, pre):
        v_ref, rows, tri = item[3], item[5], item[7]
        qs, ks, qi, kst, decay = pre
        out = []
        for h in heads:
            mask = head_masks[h % 2]
            qs_h = jnp.where(mask, qs[:, pair[h]], 0.0).astype(BF16)
            sc = lax.dot_general(qs_h, ks[:, pair[h]], NT_DIMS, preferred_element_type=F32)
            v_h = v_ref[rows, vcols[h]]
            kv = lax.dot_general(v_h, kst[:, pair[h]], TN_DIMS, preferred_element_type=F32)
            out.append((jnp.where(tri, sc, 0.0).astype(BF16), kv,
                        jnp.where(mask, qi[:, pair[h]], 0.0).astype(BF16), v_h))
        return out

    def run(items, s_ref):
        pre = [stage1(it) for it in items]
        mid = [stage2(it, p) for it, p in zip(items, pre)]
        states = [s_ref[h] for h in heads]
        for it, p, m in zip(items, pre, mid):
            o_ref, rows, decay = it[4], it[5], p[4]
            for h in heads:
                sc, kv, qi_h, v_h = m[h]
                o = jnp.dot(sc, v_h, preferred_element_type=F32)
                o = o + lax.dot_general(qi_h, states[h].astype(BF16), NT_DIMS, preferred_element_type=F32)
                o_ref[rows, vcols[h]] = o
                states[h] = states[h] * decay[:, pair[h]] + kv
        for h in heads:
            s_ref[h] = states[h]

    run(fwd, sf_ref)
    run(bwd, sb_ref)


def _gla(q, k, v, laf, lab):
    t = q.shape[0]
    rows = GLA_GROUP * GLA_CHUNK
    ng = SEQ // rows
    hk = GLA_HEADS * GLA_DK
    hv = GLA_HEADS * GLA_DV

    def fwd(w):
        return pl.BlockSpec((rows, w), lambda b, n: (b * ng + n, 0))

    def bwd(w):
        return pl.BlockSpec((rows, w), lambda b, n: (b * ng + ng - 1 - n, 0))

    return pl.pallas_call(
        _gla_body,
        grid=(BATCH, ng),
        in_specs=[fwd(hk), fwd(hk), fwd(hv), fwd(hk), bwd(hk), bwd(hk), bwd(hv), bwd(hk)],
        out_specs=[fwd(hv), bwd(hv)],
        out_shape=[jax.ShapeDtypeStruct((t, hv), F32)] * 2,
        scratch_shapes=[pltpu.VMEM((GLA_HEADS, GLA_DV, 2 * GLA_DK), F32)] * 2,
        compiler_params=_params(("arbitrary", "arbitrary"), 32),
        name="gla",
    )(q, k, v, laf, q, k, v, lab)


def _attn_body(sink_ref, q_ref, kp_ref, kc_ref, kn_ref, vp_ref, vc_ref, vn_ref, o_ref):
    step = pl.program_id(1)
    last = pl.num_programs(1) - 1
    qb = ATT_BLOCK
    hd = ATT_HEAD_DIM
    k_all = jnp.concatenate([kp_ref[...], kc_ref[...], kn_ref[...]], axis=0)
    vt_all = jnp.concatenate([vp_ref[...], vc_ref[...], vn_ref[...]], axis=1)
    lane = lax.broadcasted_iota(I32, (1, LANES), 1)
    lo = lane < hd
    j_k = lax.broadcasted_iota(I32, (3 * qb, qb), 0)
    i_q = lax.broadcasted_iota(I32, (3 * qb, qb), 1)
    band = jnp.abs(j_k - qb - i_q) <= ATT_WINDOW
    sinks = [jnp.concatenate([jnp.full((1, qb), sink_ref[4 * g + r] * LOG2_E, F32) for r in range(4)], axis=1)
             for g in range(ATT_KV_HEADS)]
    ones_rows = jnp.ones((SUBLANES, 3 * qb), BF16)
    work = [(j, g) for j in range(ATT_GROUP) for g in range(ATT_KV_HEADS)]

    def scores(j, g):
        valid = band
        if j == 0:
            valid = valid & ((j_k >= qb) | (step > 0))
        if j == ATT_GROUP - 1:
            valid = valid & ((j_k < 2 * qb) | (step < last))
        valid4 = jnp.concatenate([valid] * 4, axis=1)
        rows = slice(j * qb, (j + 1) * qb)
        kg = k_all[j * qb:(j + 3) * qb, g * LANES:(g + 1) * LANES]
        qa = q_ref[rows, (2 * g) * LANES:(2 * g + 1) * LANES]
        qc = q_ref[rows, (2 * g + 1) * LANES:(2 * g + 2) * LANES]
        zero = jnp.zeros_like(qa)
        lhs = jnp.concatenate([jnp.where(lo, qa, zero), jnp.where(lo, zero, qa),
                               jnp.where(lo, qc, zero), jnp.where(lo, zero, qc)], axis=0)
        st = lax.dot_general(kg, lhs, NT_DIMS, preferred_element_type=F32)
        return jnp.concatenate([jnp.where(valid4[0:qb], st[0:qb], NEG_INF), st[qb:2 * qb],
                                jnp.where(valid4[2 * qb:3 * qb], st[2 * qb:3 * qb], NEG_INF)], axis=0)

    def softmax(st, g):
        sink = sinks[g]
        m = jnp.maximum(jnp.max(st, axis=0, keepdims=True), sink)
        return jnp.exp2(st - m).astype(BF16), jnp.exp2(sink - m)

    def output(j, g, p, p_sink):
        rows = slice(j * qb, (j + 1) * qb)
        vgt = vt_all[g * LANES:(g + 1) * LANES, j * qb:(j + 3) * qb]
        res = jnp.dot(jnp.concatenate([vgt, ones_rows], axis=0), p, preferred_element_type=F32)
        ot = res[0:LANES] * (1.0 / (res[LANES:LANES + 1] + p_sink))
        pair_a = jnp.concatenate([ot[0:hd, 0:qb], ot[hd:2 * hd, qb:2 * qb]], axis=0)
        pair_c = jnp.concatenate([ot[0:hd, 2 * qb:3 * qb], ot[hd:2 * hd, 3 * qb:4 * qb]], axis=0)
        o_ref[rows, (2 * g) * LANES:(2 * g + 1) * LANES] = pair_a.T.astype(o_ref.dtype)
        o_ref[rows, (2 * g + 1) * LANES:(2 * g + 2) * LANES] = pair_c.T.astype(o_ref.dtype)

    s_all = [scores(j, g) for j, g in work]
    p_all = [softmax(st, g) for st, (j, g) in zip(s_all, work)]
    for (j, g), (p, p_sink) in zip(work, p_all):
        output(j, g, p, p_sink)


def _attn(sink, aq, ak2, avt):
    t = aq.shape[0]
    qb = ATT_BLOCK
    nb = SEQ // qb
    steps = nb // ATT_GROUP
    nq = ATT_Q_HEADS * ATT_HEAD_DIM
    nk = 2 * ATT_KV_HEADS * ATT_HEAD_DIM

    def edge_block(b, n, shift):
        return b * nb + jnp.clip(n * ATT_GROUP + shift, 0, nb - 1)

    def k_edge(shift):
        return pl.BlockSpec((qb, nk), lambda b, n: (edge_block(b, n, shift), 0))

    def v_edge(shift):
        return pl.BlockSpec((nk, qb), lambda b, n: (0, edge_block(b, n, shift)))

    def group(w):
        return pl.BlockSpec((ATT_GROUP * qb, w), lambda b, n: (b * steps + n, 0))

    v_group = pl.BlockSpec((nk, ATT_GROUP * qb), lambda b, n: (0, b * steps + n))
    return pl.pallas_call(
        _attn_body,
        grid=(BATCH, steps),
        in_specs=[
            pl.BlockSpec(memory_space=pltpu.SMEM),
            group(nq),
            k_edge(-1), group(nk), k_edge(ATT_GROUP), v_edge(-1), v_group, v_edge(ATT_GROUP),
        ],
        out_specs=group(nq),
        out_shape=jax.ShapeDtypeStruct((t, nq), BF16),
        compiler_params=_params(("arbitrary", "arbitrary"), 48),
        name="attn",
    )(sink, aq, ak2, ak2, ak2, avt, avt, avt)


def _post_body(of_ref, ob_ref, gg_ref, oa_ref, x_ref, mod_ref, ggla_ref, gpm_ref, gpf_ref, wout_ref,
               wrh_ref, wrl_ref, br_ref,
               x1_ref, h2_ref, ti_ref, gt_ref, rk_ref, cnt_ref, base_ref):
    tm = TM_POST

    @pl.when(pl.program_id(0) == 0)
    def _():
        base_ref[...] = jnp.zeros_like(base_ref)

    gain1 = mod_ref[2:3, :] * gpm_ref[...]
    shift2 = mod_ref[3:4, :]
    gain2 = gpf_ref[...] * (1.0 + mod_ref[4:5, :])
    sub = tm // POST_SUB
    subs = [slice(s * sub, (s + 1) * sub) for s in range(POST_SUB)]

    def mixer_out(rows):
        og = of_ref[rows, :] + ob_ref[rows, :]
        gg = gg_ref[rows, :]
        parts = []
        for h in range(GLA_HEADS):
            cols = slice(h * GLA_DV, (h + 1) * GLA_DV)
            parts.append((_rms(og[:, cols], ggla_ref[...]) * _silu(gg[:, cols])).astype(BF16))
        return jnp.concatenate(parts + [oa_ref[rows, :]], axis=1)

    def ffn_in(s, y):
        rows = subs[s]
        x1 = x_ref[rows, :] + _rms(y, gain1)
        x1_ref[rows, :] = x1
        h2 = _rms(x1, gain2) + shift2
        hi = h2.astype(BF16)
        hi32 = hi.astype(F32)
        _pack_rows(h2_ref.at[pl.ds(s * sub * PACK_SUB, sub * PACK_SUB), :], hi32)
        return hi, (h2 - hi32).astype(BF16)

    o_subs = [mixer_out(rows) for rows in subs]
    y_subs = [jnp.dot(o, wout_ref[...], preferred_element_type=F32) for o in o_subs]
    split = [ffn_in(s, y) for s, y in enumerate(y_subs)]
    h2_hi = jnp.concatenate([hi for hi, _ in split], axis=0)
    h2_lo = jnp.concatenate([lo for _, lo in split], axis=0)

    wrh = wrh_ref[...]
    logits = (lax.dot_general(wrh, h2_hi, NT_DIMS, preferred_element_type=F32)
              + lax.dot_general(wrh, h2_lo, NT_DIMS, preferred_element_type=F32)
              + lax.dot_general(wrl_ref[...], h2_hi, NT_DIMS, preferred_element_type=F32)
              + br_ref[...])
    e_iota = lax.broadcasted_iota(I32, (N_EXPERTS, tm), 0)
    idxs, vals = [], []
    work = logits
    for _ in range(TOP_K):
        m = jnp.max(work, axis=0, keepdims=True)
        idx = jnp.min(jnp.where(work == m, e_iota, N_EXPERTS), axis=0, keepdims=True)
        idxs.append(idx)
        vals.append(m)
        work = jnp.where(e_iota == idx, -jnp.inf, work)
    exps = [jnp.exp(v - vals[0]) for v in vals]
    inv = 1.0 / (exps[0] + exps[1] + exps[2] + exps[3])
    gt_ref[...] = jnp.concatenate([e * inv for e in exps], axis=0)
    ti_ref[...] = jnp.concatenate(idxs, axis=0)

    onehots = [e_iota == idx for idx in idxs]
    member = jnp.where(onehots[0] | onehots[1] | onehots[2] | onehots[3], 1.0, 0.0)
    t_row = lax.broadcasted_iota(I32, (tm, tm), 0)
    t_col = lax.broadcasted_iota(I32, (tm, tm), 1)
    strict = jnp.where(t_row < t_col, 1.0, 0.0).astype(BF16)
    before = base_ref[...] + jnp.dot(member.astype(BF16), strict, preferred_element_type=F32)
    rk_ref[...] = jnp.concatenate(
        [jnp.sum(jnp.where(oh, before, 0.0), axis=0, keepdims=True) for oh in onehots], axis=0).astype(I32)
    new_base = base_ref[...] + jnp.sum(member, axis=1, keepdims=True)
    base_ref[...] = new_base
    cnt_ref[...] = jnp.broadcast_to(new_base, cnt_ref.shape)


def _post(o_f, o_b, gg, o_att, x2, mod, g_gla, g_pm, g_pf, wout, wrh, wrl, br):
    t, d = x2.shape
    tm = TM_POST
    tiles_per_seq = SEQ // tm
    hv = GLA_HEADS * GLA_DV

    def full(a):
        return pl.BlockSpec(a.shape, lambda i: (0,) * a.ndim)

    def rows(w):
        return pl.BlockSpec((tm, w), lambda i: (i, 0))

    def lanes():
        return pl.BlockSpec((TOP_K, tm), lambda i: (0, i))

    return pl.pallas_call(
        _post_body,
        grid=(t // tm,),
        in_specs=[
            rows(hv), rows(hv), rows(hv), rows(hv), rows(d),
            pl.BlockSpec((None, 6, d), lambda i: (i // tiles_per_seq, 0, 0)),
            full(g_gla), full(g_pm), full(g_pf), full(wout), full(wrh), full(wrl), full(br),
        ],
        out_specs=[
            rows(d),
            pl.BlockSpec((tm * PACK_SUB, LANES), lambda i: (i, 0)),
            lanes(), lanes(), lanes(),
            pl.BlockSpec((N_EXPERTS, LANES), lambda i: (0, 0)),
        ],
        out_shape=[
            jax.ShapeDtypeStruct((t, d), F32),
            jax.ShapeDtypeStruct((t * PACK_SUB, LANES), I32),
            jax.ShapeDtypeStruct((TOP_K, t), I32),
            jax.ShapeDtypeStruct((TOP_K, t), F32),
            jax.ShapeDtypeStruct((TOP_K, t), I32),
            jax.ShapeDtypeStruct((N_EXPERTS, LANES), F32),
        ],
        scratch_shapes=[pltpu.VMEM((N_EXPERTS, 1), F32)],
        compiler_params=_params(("arbitrary",), 48),
        name="post",
    )(o_f, o_b, gg, o_att, x2, mod, g_gla, g_pm, g_pf, wout, wrh, wrl, br)


def _route_body(ti_ref, rk_ref, cnt_ref, pos_ref, blk_ref):
    cnt = cnt_ref[...]
    padded = jnp.floor((cnt + (MOE_BM - 1)) * (1.0 / MOE_BM)) * MOE_BM
    starts, ends = [], []
    acc = jnp.zeros((1, LANES), F32)
    for e in range(N_EXPERTS):
        starts.append(acc)
        acc = acc + padded[e:e + 1, :]
        ends.append(acc)
    ti = ti_ref[...]
    off = jnp.zeros(ti.shape, F32)
    for e in range(N_EXPERTS):
        off = jnp.where(ti == e, starts[e][:, 0:1], off)
    pos_ref[...] = rk_ref[...] + off.astype(I32)

    def owner_of(row):
        n_le = jnp.zeros(row.shape, I32)
        for e in range(N_EXPERTS):
            n_le = n_le + jnp.where(ends[e][:, 0:1] <= row, 1, 0)
        return jnp.minimum(n_le, N_EXPERTS - 1)

    block_start = lax.broadcasted_iota(I32, (1, MOE_NB_PAD), 1).astype(F32) * MOE_BM
    owner = owner_of(block_start)
    nxt = jnp.zeros((1, MOE_NB_PAD), I32)
    nxt_blk = jnp.zeros((1, MOE_NB_PAD), I32)
    slot = jnp.zeros((1, MOE_NB_PAD), I32)
    ordinal = jnp.zeros((1, 1), I32)
    for e in range(N_EXPERTS):
        end_e = ends[e][:, 0:1]
        nxt = jnp.where(owner == e, jnp.where(end_e < acc[:, 0:1], owner_of(end_e), -1), nxt)
        nxt_blk = jnp.where(owner == e, (end_e * (1.0 / MOE_BM)).astype(I32), nxt_blk)
        slot = jnp.where(owner == e, ordinal & 1, slot)
        ordinal = ordinal + jnp.where(padded[e:e + 1, 0:1] > 0, 1, 0)
    used = jnp.broadcast_to((acc[:, 0:1] * (1.0 / MOE_BM)).astype(I32), (1, MOE_NB_PAD))
    blk_ref[...] = jnp.concatenate(
        [owner, nxt, used, nxt_blk, slot, jnp.zeros((SUBLANES - 5, MOE_NB_PAD), I32)], axis=0)


def _route(top_i, rank, counts):
    return pl.pallas_call(
        _route_body,
        out_shape=[
            jax.ShapeDtypeStruct(top_i.shape, I32),
            jax.ShapeDtypeStruct((SUBLANES, MOE_NB_PAD), I32),
        ],
        compiler_params=pltpu.CompilerParams(vmem_limit_bytes=32 * 1024 * 1024),
        name="route",
    )(top_i, rank, counts)


def _sc_workers():
    info = plsc.get_sparse_core_info()
    return info.num_cores, info.num_subcores, info.num_lanes


def _sc_gather_loop(table_hbm, out_hbm, idx_v, base, chunks, bufs, gather_sems, write_sems):
    window = SC_GATHER_WINDOW
    ring = len(bufs)

    def fetch(c, b):
        return pltpu.make_async_copy(table_hbm.at[idx_v.at[pl.ds(c * window, window)]], bufs[b], gather_sems[b])

    def flush(c, b):
        return pltpu.make_async_copy(bufs[b], out_hbm.at[pl.ds(base + c * window, window)], write_sems[b])

    for b in range(ring):
        fetch(b, b).start()

    @pl.loop(0, chunks, step=ring)
    def _(c0):
        for b in range(ring):
            c = c0 + b
            fetch(c, b).wait()
            flush(c, b).start()

            @pl.when(c + ring < chunks)
            def _():
                flush(c, b).wait()
                fetch(c + ring, b).start()

    for b in range(ring):
        flush(chunks - ring + b, b).wait()


def _sc_source_rows(pos_flat, n_rows):
    cores, subcores, lanes = _sc_workers()
    workers = cores * subcores
    per_worker = n_rows // workers
    n_assign = pos_flat.shape[0]
    scan = SC_SCAN_CHUNK
    assert per_worker * workers == n_rows and per_worker % lanes == 0
    assert n_assign % scan == 0 and scan % lanes == 0 and TOKENS & (TOKENS - 1) == 0
    mesh = plsc.VectorSubcoreMesh(core_axis_name="core", subcore_axis_name="subcore")

    @functools.partial(
        pl.kernel,
        out_type=jax.ShapeDtypeStruct((n_rows,), I32),
        mesh=mesh,
        scratch_types=[pltpu.VMEM((per_worker,), I32), pltpu.VMEM((scan,), I32)],
        compiler_params=pltpu.CompilerParams(needs_layout_passes=False),
        name="sc_source_rows",
    )
    def invert(pos_hbm, out_hbm, src_v, pos_v):
        wid = lax.axis_index("subcore") * cores + lax.axis_index("core")
        base = wid * per_worker
        lane = lax.iota(I32, lanes)

        @pl.loop(0, per_worker, step=lanes)
        def _(j):
            src_v[pl.ds(j, lanes)] = (base + j + lane) & (TOKENS - 1)

        @pl.loop(0, n_assign, step=scan)
        def _(a0):
            pltpu.sync_copy(pos_hbm.at[pl.ds(a0, scan)], pos_v)

            @plsc.parallel_loop(0, scan, step=lanes, unroll=SC_SCAN_UNROLL)
            def _(j):
                rel = pos_v[pl.ds(j, lanes)] - base
                mine = (rel >= 0) & (rel < per_worker)
                tok = (a0 + j + lane) & (TOKENS - 1)
                plsc.store_scatter(src_v, [jnp.where(mine, rel, 0)], tok, mask=mine)

        pltpu.sync_copy(src_v, out_hbm.at[pl.ds(base, per_worker)])

    return invert(pos_flat)


def _sc_gather_rows(table, idx):
    cores, subcores, _ = _sc_workers()
    workers = cores * subcores
    n = idx.shape[0]
    window = SC_GATHER_WINDOW
    per_worker = n // workers
    chunks = per_worker // window
    ring = SC_GATHER_RING
    assert per_worker * workers == n and chunks * window == per_worker and chunks % ring == 0
    row_shape = table.shape[1:]
    mesh = plsc.VectorSubcoreMesh(core_axis_name="core", subcore_axis_name="subcore")

    @functools.partial(
        pl.kernel,
        out_type=jax.ShapeDtypeStruct((n,) + row_shape, table.dtype),
        mesh=mesh,
        scratch_types=[pltpu.VMEM((per_worker,), I32)]
        + [pltpu.VMEM((window,) + row_shape, table.dtype)] * ring
        + [pltpu.SemaphoreType.DMA] * (2 * ring),
        name="sc_gather_rows",
    )
    def gather(table_hbm, idx_hbm, out_hbm, idx_v, *scratch):
        wid = lax.axis_index("subcore") * cores + lax.axis_index("core")
        base = wid * per_worker
        pltpu.sync_copy(idx_hbm.at[pl.ds(base, per_worker)], idx_v)
        _sc_gather_loop(table_hbm, out_hbm, idx_v, base, chunks,
                        scratch[:ring], scratch[ring:2 * ring], scratch[2 * ring:])

    return gather(table, idx)


def _experts_body(first, be_ref, nxt_ref, nxtblk_ref, slot_ref, nu_ref, xs_ref, bgu_ref, bd_ref, wgu_hbm, wd_hbm,
                  *rest):
    ys_ref, wgu_f32, wd_f32, sems = rest[-4:]
    step = pl.program_id(0)
    end = first + pl.num_programs(0) * MOE_SUB
    bm = MOE_BM
    n_used = nu_ref[0]

    def fetch(e, slot):
        return (pltpu.make_async_copy(wgu_hbm.at[e], wgu_f32.at[slot], sems.at[0, slot]),
                pltpu.make_async_copy(wd_hbm.at[e], wd_f32.at[slot], sems.at[1, slot]))

    @pl.when((step == 0) & (first < n_used))
    def _():
        for cp in fetch(be_ref[first], slot_ref[first]):
            cp.start()

    for sub in range(MOE_SUB):
        i = first + step * MOE_SUB + sub
        e = be_ref[i]
        slot = slot_ref[i]
        used = i < n_used
        fresh = (i == first) | (e != be_ref[jnp.maximum(i - 1, 0)])
        rows = pl.ds(sub * bm * PACK_SUB, bm * PACK_SUB)
        xs_sub = xs_ref.at[rows, :]
        ys_sub = ys_ref.at[rows, :]

        @pl.when(used & fresh)
        def _():
            for cp in fetch(e, slot):
                cp.wait()

            @pl.when((nxt_ref[i] >= 0) & (nxtblk_ref[i] < end))
            def _():
                for cp in fetch(nxt_ref[i], 1 - slot):
                    cp.start()

        @pl.when(used)
        def _():
            x = jnp.concatenate(_unpack_rows(xs_sub, bm), axis=1).astype(BF16)
            gu = jnp.dot(x, wgu_f32[slot].astype(BF16), preferred_element_type=F32) + bgu_ref[pl.ds(e, 1), :]
            gate = jnp.minimum(gu[:, 0:D_FF], SWIGLU_LIMIT)
            up = jnp.clip(gu[:, D_FF:2 * D_FF], -SWIGLU_LIMIT, SWIGLU_LIMIT)
            act = ((up + 1.0) * gate * jax.nn.sigmoid(SWIGLU_ALPHA * gate)).astype(BF16)
            y = jnp.dot(act, wd_f32[slot].astype(BF16), preferred_element_type=F32) + bd_ref[pl.ds(e, 1), :]
            _pack_rows(ys_sub, y.astype(BF16).astype(F32))

        @pl.when(jnp.logical_not(used))
        def _():
            ys_sub[...] = jnp.zeros((bm * PACK_SUB, LANES), I32)


def _experts(first, part_blocks, blocks, xs_part, w_gate_up, b_gate_up, w_down, b_down, ys_prev):
    rows = MOE_SUB * MOE_BM * PACK_SUB
    d = D_MODEL
    steps = part_blocks // MOE_SUB
    assert steps * MOE_SUB == part_blocks and first % MOE_SUB == 0

    def x_block(i, be, nx, nb, sl, nu):
        last = jnp.maximum((jnp.minimum(nu[0], first + part_blocks) - 1 - first) // MOE_SUB, 0)
        return jnp.minimum(i, last)

    in_specs = [
        pl.BlockSpec((rows, LANES), lambda i, be, nx, nb, sl, nu: (x_block(i, be, nx, nb, sl, nu), 0)),
        pl.BlockSpec(b_gate_up.shape, lambda i, be, nx, nb, sl, nu: (0, 0)),
        pl.BlockSpec(b_down.shape, lambda i, be, nx, nb, sl, nu: (0, 0)),
        pl.BlockSpec(memory_space=pl.ANY),
        pl.BlockSpec(memory_space=pl.ANY),
    ]
    operands = [blocks[0, :MOE_NB], blocks[1, :MOE_NB], blocks[3, :MOE_NB], blocks[4, :MOE_NB], blocks[2, :1],
                xs_part, b_gate_up, b_down, w_gate_up, w_down]
    aliases = {}
    if ys_prev is not None:
        in_specs.append(pl.BlockSpec(memory_space=pl.ANY))
        aliases = {len(operands): 0}
        operands.append(ys_prev)
    grid_spec = pltpu.PrefetchScalarGridSpec(
        num_scalar_prefetch=5,
        grid=(steps,),
        in_specs=in_specs,
        out_specs=pl.BlockSpec((rows, LANES), lambda i, be, nx, nb, sl, nu: (first // MOE_SUB + i, 0)),
        scratch_shapes=[
            pltpu.VMEM((2, d, 2 * D_FF), F32), pltpu.VMEM((2, D_FF, d), F32),
            pltpu.SemaphoreType.DMA((2, 2)),
        ],
    )
    return pl.pallas_call(
        functools.partial(_experts_body, first),
        grid_spec=grid_spec,
        out_shape=jax.ShapeDtypeStruct((MOE_ROWS * PACK_SUB, LANES), I32),
        input_output_aliases=aliases,
        compiler_params=_params(("arbitrary",), 48),
        name="experts",
    )(*operands)


def _combine_body(gates_ref, x1_ref, mod_ref, gpost_ref, y0_ref, y1_ref, y2_ref, y3_ref, o_ref):
    tm = TM_COMB
    gates = jnp.concatenate([gates_ref[...], jnp.zeros((SUBLANES - TOP_K, tm), F32)], axis=0).T
    y_hi = jnp.zeros((tm, PACK_COLS), F32)
    y_lo = jnp.zeros((tm, PACK_COLS), F32)
    for k, yk_ref in enumerate((y0_ref, y1_ref, y2_ref, y3_ref)):
        hi, lo = _unpack_rows(yk_ref, tm)
        y_hi = y_hi + hi * gates[:, k:k + 1]
        y_lo = y_lo + lo * gates[:, k:k + 1]
    y = jnp.concatenate([y_hi, y_lo], axis=1)
    gain = mod_ref[5:6, :] * gpost_ref[...]
    o_ref[...] = x1_ref[...] + _rms(y, gain)


def _combine(gates, x1, mod, g_post, y4):
    t, d = x1.shape
    tm = TM_COMB
    tiles = t // tm
    tiles_per_seq = SEQ // tm

    def slab(k):
        return pl.BlockSpec((tm * PACK_SUB, LANES), lambda i: (k * tiles + i, 0))

    return pl.pallas_call(
        _combine_body,
        grid=(tiles,),
        in_specs=[
            pl.BlockSpec((TOP_K, tm), lambda i: (0, i)),
            pl.BlockSpec((tm, d), lambda i: (i, 0)),
            pl.BlockSpec((None, 6, d), lambda i: (i // tiles_per_seq, 0, 0)),
            pl.BlockSpec(g_post.shape, lambda i: (0, 0)),
            slab(0), slab(1), slab(2), slab(3),
        ],
        out_specs=pl.BlockSpec((tm, d), lambda i: (i, 0)),
        out_shape=jax.ShapeDtypeStruct((t, d), F32),
        compiler_params=_params(("arbitrary",), 48),
        name="combine",
    )(gates, x1, mod, g_post, y4, y4, y4, y4)


def _rotary_tables():
    half = ROT_DIM // 2
    inv_freq = ROPE_THETA ** (-2.0 * np.arange(half, dtype=np.float32) / ROT_DIM)
    ang = np.arange(SEQ, dtype=np.float32)[:, None] * inv_freq[None, :].astype(np.float32)
    cos, sin = np.cos(ang), np.sin(ang)
    ones = np.ones((SEQ, ATT_HEAD_DIM - ROT_DIM), np.float32)
    zeros = np.zeros((SEQ, ATT_HEAD_DIM - ROT_DIM), np.float32)
    zh = np.zeros((SEQ, half), np.float32)
    reps = LANES // ATT_HEAD_DIM
    rc = np.tile(np.concatenate([cos, cos, ones], axis=1), (1, reps))
    rm = np.tile(np.concatenate([-sin, zh, zeros], axis=1), (1, reps))
    rp = np.tile(np.concatenate([zh, sin, zeros], axis=1), (1, reps))
    return tuple(jnp.asarray(t, F32) for t in (rc, rm, rp))


def _mixer_inputs(w_in, w_gk_fwd, b_gk_fwd, w_gk_bwd, b_gk_bwd):
    hk = GLA_HEADS * GLA_DK
    hv = GLA_HEADS * GLA_DV
    w = w_in[0]
    o_lr = 2 * hk + 2 * hv
    o_aq = o_lr + 2 * GLA_RANK
    o_ak = o_aq + ATT_Q_HEADS * ATT_HEAD_DIM
    o_av = o_ak + ATT_KV_HEADS * ATT_HEAD_DIM
    hd = ATT_HEAD_DIM
    wa = w[:, :o_lr].astype(BF16)
    wlr = w[:, o_lr:o_aq].astype(BF16)
    dup = lambda m: jnp.concatenate([m[:, g * hd:(g + 1) * hd] for g in range(ATT_KV_HEADS) for _ in range(2)], axis=1)
    wb = jnp.concatenate([w[:, o_aq:o_ak], dup(w[:, o_ak:o_av])], axis=1).astype(BF16)
    wvt = dup(w[:, o_av:o_av + ATT_KV_HEADS * hd]).T.astype(BF16)
    zr = jnp.zeros((GLA_RANK, hk), F32)
    wgk = jnp.concatenate([jnp.concatenate([w_gk_fwd[0], zr], axis=1),
                           jnp.concatenate([zr, w_gk_bwd[0]], axis=1)], axis=0).astype(BF16)
    bgk = jnp.concatenate([b_gk_fwd[0], b_gk_bwd[0]])[None, :]
    return (wa, wlr, wgk, bgk, wb, wvt) + _rotary_tables()


def kernel(x, c, w_ada, b_ada, g_pre_mix, g_post_mix, w_in, w_gk_fwd, b_gk_fwd, w_gk_bwd, b_gk_bwd, g_gla_out,
           attn_sink, w_out, g_pre_ffn, g_post_ffn, w_router, b_router, w_gate_up, b_gate_up, w_down, b_down):
    assert x.shape == (BATCH, SEQ, D_MODEL) and w_ada.shape[0] == 1
    d = D_MODEL
    x2 = x.reshape(TOKENS, d)

    c_pad = jnp.pad(c, ((0, SUBLANES - BATCH), (0, 0)))
    mod = _ada(c_pad, w_ada[0], b_ada)[:BATCH].reshape(BATCH, 6, d)

    mixer_in = _mixer_inputs(w_in, w_gk_fwd, b_gk_fwd, w_gk_bwd, b_gk_bwd)
    q, k, v, gg, laf, lab, aq, ak2, avt = _inproj(x2, mod, g_pre_mix, *mixer_in)
    o_f, o_b = _gla(q, k, v, laf, lab)
    o_att = _attn(attn_sink[0], aq, ak2, avt)

    wr_t = w_router[0].T
    wrh = wr_t.astype(BF16)
    wrl = (wr_t - wrh.astype(F32)).astype(BF16)
    x1, h2_tiles, top_i, gates, rank, counts = _post(
        o_f, o_b, gg, o_att, x2, mod, g_gla_out, g_post_mix, g_pre_ffn, w_out[0].astype(BF16), wrh, wrl,
        b_router[0][:, None])

    pos, blocks = _route(top_i, rank, counts)

    src = _sc_source_rows(pos.reshape(TOP_K * TOKENS), MOE_ROWS)
    h2_rows = h2_tiles.reshape(TOKENS, PACK_SUB, LANES)
    ys = None
    first = 0
    for part_blocks in MOE_PART_BLOCKS:
        row0, n_rows = first * MOE_BM, part_blocks * MOE_BM
        xs_p = _sc_gather_rows(h2_rows, src[row0:row0 + n_rows])
        ys = _experts(first, part_blocks, blocks, xs_p.reshape(n_rows * PACK_SUB, LANES),
                      w_gate_up[0], b_gate_up[0], w_down[0], b_down[0], ys)
        first += part_blocks
    assert first == MOE_NB

    y4 = _sc_gather_rows(ys.reshape(MOE_ROWS, PACK_SUB, LANES), pos.reshape(TOP_K * TOKENS))
    out = _combine(gates, x1, mod, g_post_ffn, y4.reshape(TOP_K * TOKENS * PACK_SUB, LANES))
    return out.reshape(BATCH, SEQ, d)
```

```python
import functools

import jax
import jax.numpy as jnp
import numpy as np
from jax import lax
from jax.experimental import pallas as pl
from jax.experimental.pallas import tpu as pltpu
from jax.experimental.pallas import tpu_sc as plsc

F32 = jnp.float32
BF16 = jnp.bfloat16
I32 = jnp.int32

D_MODEL = 1024
BATCH = 2
SEQ = 8192
TOKENS = BATCH * SEQ
GLA_HEADS = 4
GLA_DV = 128
GLA_DK = 64
GLA_RANK = 16
GLA_GATE_NORMALIZER = 16.0
GLA_CHUNK = 64
ATT_Q_HEADS = 8
ATT_KV_HEADS = 2
ATT_HEAD_DIM = 64
ATT_WINDOW = 128
ATT_BLOCK = 128
ROT_DIM = 16
ROPE_THETA = 500000.0
N_EXPERTS = 32
TOP_K = 4
D_FF = 1024
SWIGLU_LIMIT = 7.0
SWIGLU_ALPHA = 1.702
NORM_EPS = 1e-6
NEG_INF = -1e30
LOG2_E = 1.4426950408889634

LANES = 128
SUBLANES = 8
PACK_COLS = D_MODEL // 2
PACK_SUB = PACK_COLS // LANES

TM_IN = 1024
IN_SUB = 4
GLA_GROUP = 16
ATT_GROUP = 16
TM_POST = 1024
POST_SUB = 4
MOE_BM = 256
MOE_ROWS = TOKENS * TOP_K + N_EXPERTS * MOE_BM
MOE_NB = MOE_ROWS // MOE_BM
MOE_SUB = 4
WEIGHT_DMA_PRIORITY = 1
MOE_PART_BLOCKS = (32, 128, 128)
MOE_NB_PAD = ((MOE_NB + LANES - 1) // LANES) * LANES
SC_SCAN_CHUNK = 4096
SC_SCAN_UNROLL = 8
TM_COMB = 1024
SC_GATHER_WINDOW = 32
SC_GATHER_RING = 4

NT_DIMS = (((1,), (1,)), ((), ()))
TN_DIMS = (((0,), (0,)), ((), ()))


def _params(semantics, vmem_mib):
    return pltpu.CompilerParams(dimension_semantics=semantics, vmem_limit_bytes=vmem_mib * 1024 * 1024)


def _rms(x, g):
    return x * lax.rsqrt(jnp.mean(x * x, axis=-1, keepdims=True) + NORM_EPS) * g


def _silu(x):
    return x * jax.nn.sigmoid(x)


def _pack_rows(ref, v):
    m = v.shape[0]
    bits = lax.bitcast_convert_type(v, jnp.uint32)
    word = lax.bitcast_convert_type(bits[:, :PACK_COLS] | (bits[:, PACK_COLS:] >> 16), I32)
    for s in range(PACK_SUB):
        ref[pl.ds(s, m, stride=PACK_SUB), :] = word[:, s * LANES:(s + 1) * LANES]


def _unpack_rows(ref, m):
    word = jnp.concatenate([ref[pl.ds(s, m, stride=PACK_SUB), :] for s in range(PACK_SUB)], axis=1)
    bits = lax.bitcast_convert_type(word, jnp.uint32)
    hi = lax.bitcast_convert_type(bits & jnp.uint32(0xFFFF0000), F32)
    lo = lax.bitcast_convert_type(bits << 16, F32)
    return hi, lo


def _ada_body(c_ref, w_ref, b_ref, o_ref):
    ca = _silu(c_ref[...]).astype(BF16)
    o_ref[...] = jnp.dot(ca, w_ref[...].astype(BF16), preferred_element_type=F32) + b_ref[...]


def _ada(c_pad, w_ada, b_ada):
    d = D_MODEL
    return pl.pallas_call(
        _ada_body,
        grid=(6,),
        in_specs=[
            pl.BlockSpec((SUBLANES, d), lambda j: (0, 0)),
            pl.BlockSpec((d, d), lambda j: (0, j)),
            pl.BlockSpec((1, d), lambda j: (0, j)),
        ],
        out_specs=pl.BlockSpec((SUBLANES, d), lambda j: (0, j)),
        out_shape=jax.ShapeDtypeStruct((SUBLANES, 6 * d), F32),
        compiler_params=_params(("arbitrary",), 32),
        name="ada",
    )(c_pad, w_ada, b_ada)


def _rotary(x, cos_t, msin_t, psin_t):
    width = x.shape[1]
    reps = width // LANES
    c = jnp.concatenate([cos_t] * reps, axis=1)
    m = jnp.concatenate([msin_t] * reps, axis=1)
    p = jnp.concatenate([psin_t] * reps, axis=1)
    half = ROT_DIM // 2
    return x * c + pltpu.roll(x, width - half, 1) * m + pltpu.roll(x, half, 1) * p


def _inproj_body(x_ref, mod_ref, g_ref, wa_ref, wlr_ref, wgk_ref, bgk_ref, wb_ref, wvt_ref, rc_ref, rm_ref, rp_ref,
                 q_ref, k_ref, v_ref, gg_ref, laf_ref, lab_ref, aq_ref, ak_ref, avt_ref):
    shift = mod_ref[0:1, :]
    scale = mod_ref[1:2, :]
    hk = GLA_HEADS * GLA_DK
    hv = GLA_HEADS * GLA_DV
    nq = ATT_Q_HEADS * ATT_HEAD_DIM
    nk = 2 * ATT_KV_HEADS * ATT_HEAD_DIM
    sub = x_ref.shape[0] // IN_SUB
    subs = [slice(s * sub, (s + 1) * sub) for s in range(IN_SUB)]

    gain = g_ref[...] * (1.0 + scale)

    def hidden(rows):
        return (_rms(x_ref[rows, :], gain) + shift).astype(BF16)

    def project(h):
        return (jnp.dot(h, wa_ref[...], preferred_element_type=F32),
                jnp.dot(h, wlr_ref[...], preferred_element_type=F32),
                jnp.dot(h, wb_ref[...], preferred_element_type=F32),
                lax.dot_general(wvt_ref[...], h, NT_DIMS, preferred_element_type=F32))

    def finish(rows, pa, plr, pb, pvt):
        q_ref[rows, :] = pa[:, 0:hk] * (GLA_DK ** -0.5)
        k_ref[rows, :] = pa[:, hk:2 * hk]
        v_ref[rows, :] = pa[:, 2 * hk:2 * hk + hv].astype(BF16)
        gg_ref[rows, :] = pa[:, 2 * hk + hv:2 * hk + 2 * hv]
        gk = jnp.dot(plr.astype(BF16), wgk_ref[...], preferred_element_type=F32) + bgk_ref[...]
        la = (jnp.minimum(gk, 0.0) - jnp.log1p(jnp.exp(-jnp.abs(gk)))) * (1.0 / GLA_GATE_NORMALIZER)
        laf_ref[rows, :] = la[:, 0:hk]
        lab_ref[rows, :] = la[:, hk:2 * hk]
        rc, rm, rp = rc_ref[rows, :], rm_ref[rows, :], rp_ref[rows, :]
        aq_ref[rows, :] = (_rotary(pb[:, 0:nq], rc, rm, rp) * (ATT_HEAD_DIM ** -0.5 * LOG2_E)).astype(BF16)
        ak_ref[rows, :] = _rotary(pb[:, nq:nq + nk], rc, rm, rp).astype(BF16)
        avt_ref[:, rows] = pvt.astype(BF16)

    hs = [hidden(rows) for rows in subs]
    ps = [project(h) for h in hs]
    for rows, p in zip(subs, ps):
        finish(rows, *p)


def _inproj(x2, mod, g_pre, wa, wlr, wgk, bgk, wb, wvt, rc, rm, rp):
    t, d = x2.shape
    tm = TM_IN
    tiles_per_seq = SEQ // tm
    hk = GLA_HEADS * GLA_DK
    hv = GLA_HEADS * GLA_DV
    nq = ATT_Q_HEADS * ATT_HEAD_DIM
    nk = 2 * ATT_KV_HEADS * ATT_HEAD_DIM

    def full(a):
        return pl.BlockSpec(a.shape, lambda i: (0,) * a.ndim)

    def rows(w):
        return pl.BlockSpec((tm, w), lambda i: (i, 0))

    def table():
        return pl.BlockSpec((tm, LANES), lambda i: (i % tiles_per_seq, 0))

    out_widths = [(hk, F32), (hk, F32), (hv, BF16), (hv, F32), (hk, F32), (hk, F32), (nq, BF16), (nk, BF16)]
    return pl.pallas_call(
        _inproj_body,
        grid=(t // tm,),
        in_specs=[
            rows(d),
            pl.BlockSpec((None, 6, d), lambda i: (i // tiles_per_seq, 0, 0)),
            full(g_pre), full(wa), full(wlr), full(wgk), full(bgk), full(wb), full(wvt),
            table(), table(), table(),
        ],
        out_specs=[rows(w) for w, _ in out_widths] + [pl.BlockSpec((nk, tm), lambda i: (0, i))],
        out_shape=[jax.ShapeDtypeStruct((t, w), dt) for w, dt in out_widths] + [jax.ShapeDtypeStruct((nk, t), BF16)],
        compiler_params=_params(("arbitrary",), 56),
        name="inproj",
    )(x2, mod, g_pre, wa, wlr, wgk, bgk, wb, wvt, rc, rm, rp)


def _gla_body(qf_ref, kf_ref, vf_ref, laf_ref, qb_ref, kb_ref, vb_ref, lab_ref, of_ref, ob_ref, sf_ref, sb_ref):
    @pl.when(pl.program_id(1) == 0)
    def _():
        sf_ref[...] = jnp.zeros_like(sf_ref)
        sb_ref[...] = jnp.zeros_like(sb_ref)

    c = GLA_CHUNK
    r_i = lax.broadcasted_iota(I32, (c, c), 0)
    c_i = lax.broadcasted_iota(I32, (c, c), 1)
    lower = c_i <= r_i
    upper = c_i >= r_i
    cum_f = jnp.where(lower, 1.0, 0.0).astype(BF16)
    cum_b = jnp.where(upper, 1.0, 0.0).astype(BF16)
    lane = lax.broadcasted_iota(I32, (1, LANES), 1)
    head_masks = (lane < GLA_DK, lane >= GLA_DK)

    fwd = [(qf_ref, kf_ref, laf_ref, vf_ref, of_ref, slice(g * c, (g + 1) * c), cum_f, lower, c - 1, c // 2 - 1)
           for g in range(GLA_GROUP)]
    bwd = [(qb_ref, kb_ref, lab_ref, vb_ref, ob_ref, slice(g * c, (g + 1) * c), cum_b, upper, 0, c // 2)
           for g in reversed(range(GLA_GROUP))]
    heads = range(GLA_HEADS)
    pair = [slice((h // 2) * LANES, (h // 2 + 1) * LANES) for h in heads]
    vcols = [slice(h * GLA_DV, (h + 1) * GLA_DV) for h in heads]

    def stage1(item):
        q_ref, k_ref, la_ref, v_ref, o_ref, rows, cum, tri, i_last, i_mid = item
        la = la_ref[rows, :]
        hi = la.astype(BF16)
        lo = (la - hi.astype(F32)).astype(BF16)
        b = jnp.dot(cum, hi, preferred_element_type=F32) + jnp.dot(cum, lo, preferred_element_type=F32)
        b_last = b[i_last:i_last + 1, :]
        b_mid = b[i_mid:i_mid + 1, :]
        q, k = q_ref[rows, :], k_ref[rows, :]
        return (q * jnp.exp(b - b_mid), (k * jnp.exp(b_mid - b)).astype(BF16), q * jnp.exp(b),
                (k * jnp.exp(b_last - b)).astype(BF16), jnp.exp(b_last))

    def stage2(item, pre):
        v_ref, rows, tri = item[3], item[5], item[7]
        qs, ks, qi, kst, decay = pre
        out = []
        for h in heads:
            mask = head_masks[h % 2]
            qs_h = jnp.where(mask, qs[:, pair[h]], 0.0).astype(BF16)
            sc = lax.dot_general(qs_h, ks[:, pair[h]], NT_DIMS, preferred_element_type=F32)
            v_h = v_ref[rows, vcols[h]]
            kv = lax.dot_general(v_h, kst[:, pair[h]], TN_DIMS, preferred_element_type=F32)
            out.append((jnp.where(tri, sc, 0.0).astype(BF16), kv,
                        jnp.where(mask, qi[:, pair[h]], 0.0).astype(BF16), v_h))
        return out

    def run(items, s_ref):
        pre = [stage1(it) for it in items]
        mid = [stage2(it, p) for it, p in zip(items, pre)]
        states = [s_ref[h] for h in heads]
        for it, p, m in zip(items, pre, mid):
            o_ref, rows, decay = it[4], it[5], p[4]
            for h in heads:
                sc, kv, qi_h, v_h = m[h]
                o = jnp.dot(sc, v_h, preferred_element_type=F32)
                o = o + lax.dot_general(qi_h, states[h].astype(BF16), NT_DIMS, preferred_element_type=F32)
                o_ref[rows, vcols[h]] = o
                states[h] = states[h] * decay[:, pair[h]] + kv
        for h in heads:
            s_ref[h] = states[h]

    run(fwd, sf_ref)
    run(bwd, sb_ref)


def _gla(q, k, v, laf, lab):
    t = q.shape[0]
    rows = GLA_GROUP * GLA_CHUNK
    ng = SEQ // rows
    hk = GLA_HEADS * GLA_DK
    hv = GLA_HEADS * GLA_DV

    def fwd(w):
        return pl.BlockSpec((rows, w), lambda b, n: (b * ng + n, 0))

    def bwd(w):
        return pl.BlockSpec((rows, w), lambda b, n: (b * ng + ng - 1 - n, 0))

    return pl.pallas_call(
        _gla_body,
        grid=(BATCH, ng),
        in_specs=[fwd(hk), fwd(hk), fwd(hv), fwd(hk), bwd(hk), bwd(hk), bwd(hv), bwd(hk)],
        out_specs=[fwd(hv), bwd(hv)],
        out_shape=[jax.ShapeDtypeStruct((t, hv), F32)] * 2,
        scratch_shapes=[pltpu.VMEM((GLA_HEADS, GLA_DV, 2 * GLA_DK), F32)] * 2,
        compiler_params=_params(("arbitrary", "arbitrary"), 32),
        name="gla",
    )(q, k, v, laf, q, k, v, lab)


def _attn_body(sink_ref, q_ref, kp_ref, kc_ref, kn_ref, vp_ref, vc_ref, vn_ref, o_ref):
    step = pl.program_id(1)
    last = pl.num_programs(1) - 1
    qb = ATT_BLOCK
    hd = ATT_HEAD_DIM
    k_all = jnp.concatenate([kp_ref[...], kc_ref[...], kn_ref[...]], axis=0)
    vt_all = jnp.concatenate([vp_ref[...], vc_ref[...], vn_ref[...]], axis=1)
    lane = lax.broadcasted_iota(I32, (1, LANES), 1)
    lo = lane < hd
    j_k = lax.broadcasted_iota(I32, (3 * qb, qb), 0)
    i_q = lax.broadcasted_iota(I32, (3 * qb, qb), 1)
    band = jnp.abs(j_k - qb - i_q) <= ATT_WINDOW
    sinks = [jnp.concatenate([jnp.full((1, qb), sink_ref[4 * g + r] * LOG2_E, F32) for r in range(4)], axis=1)
             for g in range(ATT_KV_HEADS)]
    ones_rows = jnp.ones((SUBLANES, 3 * qb), BF16)
    work = [(j, g) for j in range(ATT_GROUP) for g in range(ATT_KV_HEADS)]

    def scores(j, g):
        valid = band
        if j == 0:
            valid = valid & ((j_k >= qb) | (step > 0))
        if j == ATT_GROUP - 1:
            valid = valid & ((j_k < 2 * qb) | (step < last))
        valid4 = jnp.concatenate([valid] * 4, axis=1)
        rows = slice(j * qb, (j + 1) * qb)
        kg = k_all[j * qb:(j + 3) * qb, g * LANES:(g + 1) * LANES]
        qa = q_ref[rows, (2 * g) * LANES:(2 * g + 1) * LANES]
        qc = q_ref[rows, (2 * g + 1) * LANES:(2 * g + 2) * LANES]
        zero = jnp.zeros_like(qa)
        lhs = jnp.concatenate([jnp.where(lo, qa, zero), jnp.where(lo, zero, qa),
                               jnp.where(lo, qc, zero), jnp.where(lo, zero, qc)], axis=0)
        st = lax.dot_general(kg, lhs, NT_DIMS, preferred_element_type=F32)
        return jnp.concatenate([jnp.where(valid4[0:qb], st[0:qb], NEG_INF), st[qb:2 * qb],
                                jnp.where(valid4[2 * qb:3 * qb], st[2 * qb:3 * qb], NEG_INF)], axis=0)

    def softmax(st, g):
        sink = sinks[g]
        m = jnp.maximum(jnp.max(st, axis=0, keepdims=True), sink)
        return jnp.exp2(st - m).astype(BF16), jnp.exp2(sink - m)

    def output(j, g, p, p_sink):
        rows = slice(j * qb, (j + 1) * qb)
        vgt = vt_all[g * LANES:(g + 1) * LANES, j * qb:(j + 3) * qb]
        res = jnp.dot(jnp.concatenate([vgt, ones_rows], axis=0), p, preferred_element_type=F32)
        ot = res[0:LANES] * (1.0 / (res[LANES:LANES + 1] + p_sink))
        pair_a = jnp.concatenate([ot[0:hd, 0:qb], ot[hd:2 * hd, qb:2 * qb]], axis=0)
        pair_c = jnp.concatenate([ot[0:hd, 2 * qb:3 * qb], ot[hd:2 * hd, 3 * qb:4 * qb]], axis=0)
        o_ref[rows, (2 * g) * LANES:(2 * g + 1) * LANES] = pair_a.T.astype(o_ref.dtype)
        o_ref[rows, (2 * g + 1) * LANES:(2 * g + 2) * LANES] = pair_c.T.astype(o_ref.dtype)

    s_all = [scores(j, g) for j, g in work]
    p_all = [softmax(st, g) for st, (j, g) in zip(s_all, work)]
    for (j, g), (p, p_sink) in zip(work, p_all):
        output(j, g, p, p_sink)


def _attn(sink, aq, ak2, avt):
    t = aq.shape[0]
    qb = ATT_BLOCK
    nb = SEQ // qb
    steps = nb // ATT_GROUP
    nq = ATT_Q_HEADS * ATT_HEAD_DIM
    nk = 2 * ATT_KV_HEADS * ATT_HEAD_DIM

    def edge_block(b, n, shift):
        return b * nb + jnp.clip(n * ATT_GROUP + shift, 0, nb - 1)

    def k_edge(shift):
        return pl.BlockSpec((qb, nk), lambda b, n: (edge_block(b, n, shift), 0))

    def v_edge(shift):
        return pl.BlockSpec((nk, qb), lambda b, n: (0, edge_block(b, n, shift)))

    def group(w):
        return pl.BlockSpec((ATT_GROUP * qb, w), lambda b, n: (b * steps + n, 0))

    v_group = pl.BlockSpec((nk, ATT_GROUP * qb), lambda b, n: (0, b * steps + n))
    return pl.pallas_call(
        _attn_body,
        grid=(BATCH, steps),
        in_specs=[
            pl.BlockSpec(memory_space=pltpu.SMEM),
            group(nq),
            k_edge(-1), group(nk), k_edge(ATT_GROUP), v_edge(-1), v_group, v_edge(ATT_GROUP),
        ],
        out_specs=group(nq),
        out_shape=jax.ShapeDtypeStruct((t, nq), BF16),
        compiler_params=_params(("arbitrary", "arbitrary"), 48),
        name="attn",
    )(sink, aq, ak2, ak2, ak2, avt, avt, avt)


def _post_body(of_ref, ob_ref, gg_ref, oa_ref, x_ref, mod_ref, ggla_ref, gpm_ref, gpf_ref, wout_ref,
               wrh_ref, wrl_ref, br_ref,
               x1_ref, h2_ref, ti_ref, gt_ref, rk_ref, cnt_ref, base_ref):
    tm = TM_POST

    @pl.when(pl.program_id(0) == 0)
    def _():
        base_ref[...] = jnp.zeros_like(base_ref)

    gain1 = mod_ref[2:3, :] * gpm_ref[...]
    shift2 = mod_ref[3:4, :]
    gain2 = gpf_ref[...] * (1.0 + mod_ref[4:5, :])
    sub = tm // POST_SUB
    subs = [slice(s * sub, (s + 1) * sub) for s in range(POST_SUB)]

    def mixer_out(rows):
        og = of_ref[rows, :] + ob_ref[rows, :]
        gg = gg_ref[rows, :]
        parts = []
        for h in range(GLA_HEADS):
            cols = slice(h * GLA_DV, (h + 1) * GLA_DV)
            parts.append((_rms(og[:, cols], ggla_ref[...]) * _silu(gg[:, cols])).astype(BF16))
        return jnp.concatenate(parts + [oa_ref[rows, :]], axis=1)

    def ffn_in(s, y):
        rows = subs[s]
        x1 = x_ref[rows, :] + _rms(y, gain1)
        x1_ref[rows, :] = x1
        h2 = _rms(x1, gain2) + shift2
        hi = h2.astype(BF16)
        hi32 = hi.astype(F32)
        _pack_rows(h2_ref.at[pl.ds(s * sub * PACK_SUB, sub * PACK_SUB), :], hi32)
        return hi, (h2 - hi32).astype(BF16)

    o_subs = [mixer_out(rows) for rows in subs]
    y_subs = [jnp.dot(o, wout_ref[...], preferred_element_type=F32) for o in o_subs]
    split = [ffn_in(s, y) for s, y in enumerate(y_subs)]
    h2_hi = jnp.concatenate([hi for hi, _ in split], axis=0)
    h2_lo = jnp.concatenate([lo for _, lo in split], axis=0)

    wrh = wrh_ref[...]
    logits = (lax.dot_general(wrh, h2_hi, NT_DIMS, preferred_element_type=F32)
              + lax.dot_general(wrh, h2_lo, NT_DIMS, preferred_element_type=F32)
              + lax.dot_general(wrl_ref[...], h2_hi, NT_DIMS, preferred_element_type=F32)
              + br_ref[...])
    e_iota = lax.broadcasted_iota(I32, (N_EXPERTS, tm), 0)
    idxs, vals = [], []
    work = logits
    for _ in range(TOP_K):
        m = jnp.max(work, axis=0, keepdims=True)
        idx = jnp.min(jnp.where(work == m, e_iota, N_EXPERTS), axis=0, keepdims=True)
        idxs.append(idx)
        vals.append(m)
        work = jnp.where(e_iota == idx, -jnp.inf, work)
    exps = [jnp.exp(v - vals[0]) for v in vals]
    inv = 1.0 / (exps[0] + exps[1] + exps[2] + exps[3])
    gt_ref[...] = jnp.concatenate([e * inv for e in exps], axis=0)
    ti_ref[...] = jnp.concatenate(idxs, axis=0)

    onehots = [e_iota == idx for idx in idxs]
    member = jnp.where(onehots[0] | onehots[1] | onehots[2] | onehots[3], 1.0, 0.0)
    t_row = lax.broadcasted_iota(I32, (tm, tm), 0)
    t_col = lax.broadcasted_iota(I32, (tm, tm), 1)
    strict = jnp.where(t_row < t_col, 1.0, 0.0).astype(BF16)
    before = base_ref[...] + jnp.dot(member.astype(BF16), strict, preferred_element_type=F32)
    rk_ref[...] = jnp.concatenate(
        [jnp.sum(jnp.where(oh, before, 0.0), axis=0, keepdims=True) for oh in onehots], axis=0).astype(I32)
    new_base = base_ref[...] + jnp.sum(member, axis=1, keepdims=True)
    base_ref[...] = new_base
    cnt_ref[...] = jnp.broadcast_to(new_base, cnt_ref.shape)


def _post(o_f, o_b, gg, o_att, x2, mod, g_gla, g_pm, g_pf, wout, wrh, wrl, br):
    t, d = x2.shape
    tm = TM_POST
    tiles_per_seq = SEQ // tm
    hv = GLA_HEADS * GLA_DV

    def full(a):
        return pl.BlockSpec(a.shape, lambda i: (0,) * a.ndim)

    def rows(w):
        return pl.BlockSpec((tm, w), lambda i: (i, 0))

    def lanes():
        return pl.BlockSpec((TOP_K, tm), lambda i: (0, i))

    return pl.pallas_call(
        _post_body,
        grid=(t // tm,),
        in_specs=[
            rows(hv), rows(hv), rows(hv), rows(hv), rows(d),
            pl.BlockSpec((None, 6, d), lambda i: (i // tiles_per_seq, 0, 0)),
            full(g_gla), full(g_pm), full(g_pf), full(wout), full(wrh), full(wrl), full(br),
        ],
        out_specs=[
            rows(d),
            pl.BlockSpec((tm * PACK_SUB, LANES), lambda i: (i, 0)),
            lanes(), lanes(), lanes(),
            pl.BlockSpec((N_EXPERTS, LANES), lambda i: (0, 0)),
        ],
        out_shape=[
            jax.ShapeDtypeStruct((t, d), F32),
            jax.ShapeDtypeStruct((t * PACK_SUB, LANES), I32),
            jax.ShapeDtypeStruct((TOP_K, t), I32),
            jax.ShapeDtypeStruct((TOP_K, t), F32),
            jax.ShapeDtypeStruct((TOP_K, t), I32),
            jax.ShapeDtypeStruct((N_EXPERTS, LANES), F32),
        ],
        scratch_shapes=[pltpu.VMEM((N_EXPERTS, 1), F32)],
        compiler_params=_params(("arbitrary",), 48),
        name="post",
    )(o_f, o_b, gg, o_att, x2, mod, g_gla, g_pm, g_pf, wout, wrh, wrl, br)


def _route_body(ti_ref, rk_ref, cnt_ref, pos_ref, blk_ref):
    cnt = cnt_ref[...]
    padded = jnp.floor((cnt + (MOE_BM - 1)) * (1.0 / MOE_BM)) * MOE_BM
    starts, ends = [], []
    acc = jnp.zeros((1, LANES), F32)
    for e in range(N_EXPERTS):
        starts.append(acc)
        acc = acc + padded[e:e + 1, :]
        ends.append(acc)
    ti = ti_ref[...]
    off = jnp.zeros(ti.shape, F32)
    for e in range(N_EXPERTS):
        off = jnp.where(ti == e, starts[e][:, 0:1], off)
    pos_ref[...] = rk_ref[...] + off.astype(I32)

    def owner_of(row):
        n_le = jnp.zeros(row.shape, I32)
        for e in range(N_EXPERTS):
            n_le = n_le + jnp.where(ends[e][:, 0:1] <= row, 1, 0)
        return jnp.minimum(n_le, N_EXPERTS - 1)

    block_start = lax.broadcasted_iota(I32, (1, MOE_NB_PAD), 1).astype(F32) * MOE_BM
    owner = owner_of(block_start)
    nxt = jnp.zeros((1, MOE_NB_PAD), I32)
    nxt_blk = jnp.zeros((1, MOE_NB_PAD), I32)
    slot = jnp.zeros((1, MOE_NB_PAD), I32)
    ordinal = jnp.zeros((1, 1), I32)
    for e in range(N_EXPERTS):
        end_e = ends[e][:, 0:1]
        nxt = jnp.where(owner == e, jnp.where(end_e < acc[:, 0:1], owner_of(end_e), -1), nxt)
        nxt_blk = jnp.where(owner == e, (end_e * (1.0 / MOE_BM)).astype(I32), nxt_blk)
        slot = jnp.where(owner == e, ordinal & 1, slot)
        ordinal = ordinal + jnp.where(padded[e:e + 1, 0:1] > 0, 1, 0)
    used = jnp.broadcast_to((acc[:, 0:1] * (1.0 / MOE_BM)).astype(I32), (1, MOE_NB_PAD))
    blk_ref[...] = jnp.concatenate(
        [owner, nxt, used, nxt_blk, slot, jnp.zeros((SUBLANES - 5, MOE_NB_PAD), I32)], axis=0)


def _route(top_i, rank, counts):
    return pl.pallas_call(
        _route_body,
        out_shape=[
            jax.ShapeDtypeStruct(top_i.shape, I32),
            jax.ShapeDtypeStruct((SUBLANES, MOE_NB_PAD), I32),
        ],
        compiler_params=pltpu.CompilerParams(vmem_limit_bytes=32 * 1024 * 1024),
        name="route",
    )(top_i, rank, counts)


def _sc_workers():
    info = plsc.get_sparse_core_info()
    return info.num_cores, info.num_subcores, info.num_lanes


def _sc_gather_loop(table_hbm, out_hbm, idx_v, base, chunks, bufs, gather_sems, write_sems):
    window = SC_GATHER_WINDOW
    ring = len(bufs)

    def fetch(c, b):
        return pltpu.make_async_copy(table_hbm.at[idx_v.at[pl.ds(c * window, window)]], bufs[b], gather_sems[b])

    def flush(c, b):
        return pltpu.make_async_copy(bufs[b], out_hbm.at[pl.ds(base + c * window, window)], write_sems[b])

    for b in range(ring):
        fetch(b, b).start()

    @pl.loop(0, chunks, step=ring)
    def _(c0):
        for b in range(ring):
            c = c0 + b
            fetch(c, b).wait()
            flush(c, b).start()

            @pl.when(c + ring < chunks)
            def _():
                flush(c, b).wait()
                fetch(c + ring, b).start()

    for b in range(ring):
        flush(chunks - ring + b, b).wait()


def _sc_source_rows(pos_flat, n_rows):
    cores, subcores, lanes = _sc_workers()
    workers = cores * subcores
    per_worker = n_rows // workers
    n_assign = pos_flat.shape[0]
    scan = SC_SCAN_CHUNK
    assert per_worker * workers == n_rows and per_worker % lanes == 0
    assert n_assign % scan == 0 and scan % lanes == 0 and TOKENS & (TOKENS - 1) == 0
    mesh = plsc.VectorSubcoreMesh(core_axis_name="core", subcore_axis_name="subcore")

    @functools.partial(
        pl.kernel,
        out_type=jax.ShapeDtypeStruct((n_rows,), I32),
        mesh=mesh,
        scratch_types=[pltpu.VMEM((per_worker,), I32), pltpu.VMEM((scan,), I32)],
        compiler_params=pltpu.CompilerParams(needs_layout_passes=False),
        name="sc_source_rows",
    )
    def invert(pos_hbm, out_hbm, src_v, pos_v):
        wid = lax.axis_index("subcore") * cores + lax.axis_index("core")
        base = wid * per_worker
        lane = lax.iota(I32, lanes)

        @pl.loop(0, per_worker, step=lanes)
        def _(j):
            src_v[pl.ds(j, lanes)] = (base + j + lane) & (TOKENS - 1)

        @pl.loop(0, n_assign, step=scan)
        def _(a0):
            pltpu.sync_copy(pos_hbm.at[pl.ds(a0, scan)], pos_v)

            @plsc.parallel_loop(0, scan, step=lanes, unroll=SC_SCAN_UNROLL)
            def _(j):
                rel = pos_v[pl.ds(j, lanes)] - base
                mine = (rel >= 0) & (rel < per_worker)
                tok = (a0 + j + lane) & (TOKENS - 1)
                plsc.store_scatter(src_v, [jnp.where(mine, rel, 0)], tok, mask=mine)

        pltpu.sync_copy(src_v, out_hbm.at[pl.ds(base, per_worker)])

    return invert(pos_flat)


def _sc_gather_rows(table, idx):
    cores, subcores, _ = _sc_workers()
    workers = cores * subcores
    n = idx.shape[0]
    window = SC_GATHER_WINDOW
    per_worker = n // workers
    chunks = per_worker // window
    ring = SC_GATHER_RING
    assert per_worker * workers == n and chunks * window == per_worker and chunks % ring == 0
    row_shape = table.shape[1:]
    mesh = plsc.VectorSubcoreMesh(core_axis_name="core", subcore_axis_name="subcore")

    @functools.partial(
        pl.kernel,
        out_type=jax.ShapeDtypeStruct((n,) + row_shape, table.dtype),
        mesh=mesh,
        scratch_types=[pltpu.VMEM((per_worker,), I32)]
        + [pltpu.VMEM((window,) + row_shape, table.dtype)] * ring
        + [pltpu.SemaphoreType.DMA] * (2 * ring),
        name="sc_gather_rows",
    )
    def gather(table_hbm, idx_hbm, out_hbm, idx_v, *scratch):
        wid = lax.axis_index("subcore") * cores + lax.axis_index("core")
        base = wid * per_worker
        pltpu.sync_copy(idx_hbm.at[pl.ds(base, per_worker)], idx_v)
        _sc_gather_loop(table_hbm, out_hbm, idx_v, base, chunks,
                        scratch[:ring], scratch[ring:2 * ring], scratch[2 * ring:])

    return gather(table, idx)


def _experts_body(first, be_ref, nxt_ref, nxtblk_ref, slot_ref, nu_ref, xs_ref, bgu_ref, bd_ref, wgu_hbm, wd_hbm,
                  *rest):
    ys_ref, wgu_f32, wd_f32, sems = rest[-4:]
    step = pl.program_id(0)
    end = first + pl.num_programs(0) * MOE_SUB
    bm = MOE_BM
    n_used = nu_ref[0]

    def fetch(e, slot):
        return (pltpu.make_async_copy(wgu_hbm.at[e], wgu_f32.at[slot], sems.at[0, slot]),
                pltpu.make_async_copy(wd_hbm.at[e], wd_f32.at[slot], sems.at[1, slot]))

    @pl.when((step == 0) & (first < n_used))
    def _():
        for cp in fetch(be_ref[first], slot_ref[first]):
            cp.start(priority=WEIGHT_DMA_PRIORITY)

    for sub in range(MOE_SUB):
        i = first + step * MOE_SUB + sub
        e = be_ref[i]
        slot = slot_ref[i]
        used = i < n_used
        fresh = (i == first) | (e != be_ref[jnp.maximum(i - 1, 0)])
        rows = pl.ds(sub * bm * PACK_SUB, bm * PACK_SUB)
        xs_sub = xs_ref.at[rows, :]
        ys_sub = ys_ref.at[rows, :]

        @pl.when(used & fresh)
        def _():
            for cp in fetch(e, slot):
                cp.wait()

            @pl.when((nxt_ref[i] >= 0) & (nxtblk_ref[i] < end))
            def _():
                for cp in fetch(nxt_ref[i], 1 - slot):
                    cp.start(priority=WEIGHT_DMA_PRIORITY)

        @pl.when(used)
        def _():
            x = jnp.concatenate(_unpack_rows(xs_sub, bm), axis=1).astype(BF16)
            gu = jnp.dot(x, wgu_f32[slot].astype(BF16), preferred_element_type=F32) + bgu_ref[pl.ds(e, 1), :]
            gate = jnp.minimum(gu[:, 0:D_FF], SWIGLU_LIMIT)
            up = jnp.clip(gu[:, D_FF:2 * D_FF], -SWIGLU_LIMIT, SWIGLU_LIMIT)
            act = ((up + 1.0) * gate * jax.nn.sigmoid(SWIGLU_ALPHA * gate)).astype(BF16)
            y = jnp.dot(act, wd_f32[slot].astype(BF16), preferred_element_type=F32) + bd_ref[pl.ds(e, 1), :]
            _pack_rows(ys_sub, y.astype(BF16).astype(F32))

        @pl.when(jnp.logical_not(used))
        def _():
            ys_sub[...] = jnp.zeros((bm * PACK_SUB, LANES), I32)


def _experts(first, part_blocks, blocks, xs_part, w_gate_up, b_gate_up, w_down, b_down, ys_prev):
    rows = MOE_SUB * MOE_BM * PACK_SUB
    d = D_MODEL
    steps = part_blocks // MOE_SUB
    assert steps * MOE_SUB == part_blocks and first % MOE_SUB == 0

    def x_block(i, be, nx, nb, sl, nu):
        last = jnp.maximum((jnp.minimum(nu[0], first + part_blocks) - 1 - first) // MOE_SUB, 0)
        return jnp.minimum(i, last)

    in_specs = [
        pl.BlockSpec((rows, LANES), lambda i, be, nx, nb, sl, nu: (x_block(i, be, nx, nb, sl, nu), 0)),
        pl.BlockSpec(b_gate_up.shape, lambda i, be, nx, nb, sl, nu: (0, 0)),
        pl.BlockSpec(b_down.shape, lambda i, be, nx, nb, sl, nu: (0, 0)),
        pl.BlockSpec(memory_space=pl.ANY),
        pl.BlockSpec(memory_space=pl.ANY),
    ]
    operands = [blocks[0, :MOE_NB], blocks[1, :MOE_NB], blocks[3, :MOE_NB], blocks[4, :MOE_NB], blocks[2, :1],
                xs_part, b_gate_up, b_down, w_gate_up, w_down]
    aliases = {}
    if ys_prev is not None:
        in_specs.append(pl.BlockSpec(memory_space=pl.ANY))
        aliases = {len(operands): 0}
        operands.append(ys_prev)
    grid_spec = pltpu.PrefetchScalarGridSpec(
        num_scalar_prefetch=5,
        grid=(steps,),
        in_specs=in_specs,
        out_specs=pl.BlockSpec((rows, LANES), lambda i, be, nx, nb, sl, nu: (first // MOE_SUB + i, 0)),
        scratch_shapes=[
            pltpu.VMEM((2, d, 2 * D_FF), F32), pltpu.VMEM((2, D_FF, d), F32),
            pltpu.SemaphoreType.DMA((2, 2)),
        ],
    )
    return pl.pallas_call(
        functools.partial(_experts_body, first),
        grid_spec=grid_spec,
        out_shape=jax.ShapeDtypeStruct((MOE_ROWS * PACK_SUB, LANES), I32),
        input_output_aliases=aliases,
        compiler_params=_params(("arbitrary",), 48),
        name="experts",
    )(*operands)


def _combine_body(gates_ref, x1_ref, mod_ref, gpost_ref, y0_ref, y1_ref, y2_ref, y3_ref, o_ref):
    tm = TM_COMB
    gates = jnp.concatenate([gates_ref[...], jnp.zeros((SUBLANES - TOP_K, tm), F32)], axis=0).T
    y_hi = jnp.zeros((tm, PACK_COLS), F32)
    y_lo = jnp.zeros((tm, PACK_COLS), F32)
    for k, yk_ref in enumerate((y0_ref, y1_ref, y2_ref, y3_ref)):
        hi, lo = _unpack_rows(yk_ref, tm)
        y_hi = y_hi + hi * gates[:, k:k + 1]
        y_lo = y_lo + lo * gates[:, k:k + 1]
    y = jnp.concatenate([y_hi, y_lo], axis=1)
    gain = mod_ref[5:6, :] * gpost_ref[...]
    o_ref[...] = x1_ref[...] + _rms(y, gain)


def _combine(gates, x1, mod, g_post, y4):
    t, d = x1.shape
    tm = TM_COMB
    tiles = t // tm
    tiles_per_seq = SEQ // tm

    def slab(k):
        return pl.BlockSpec((tm * PACK_SUB, LANES), lambda i: (k * tiles + i, 0))

    return pl.pallas_call(
        _combine_body,
        grid=(tiles,),
        in_specs=[
            pl.BlockSpec((TOP_K, tm), lambda i: (0, i)),
            pl.BlockSpec((tm, d), lambda i: (i, 0)),
            pl.BlockSpec((None, 6, d), lambda i: (i // tiles_per_seq, 0, 0)),
            pl.BlockSpec(g_post.shape, lambda i: (0, 0)),
            slab(0), slab(1), slab(2), slab(3),
        ],
        out_specs=pl.BlockSpec((tm, d), lambda i: (i, 0)),
        out_shape=jax.ShapeDtypeStruct((t, d), F32),
        compiler_params=_params(("arbitrary",), 48),
        name="combine",
    )(gates, x1, mod, g_post, y4, y4, y4, y4)


def _rotary_tables():
    half = ROT_DIM // 2
    inv_freq = ROPE_THETA ** (-2.0 * np.arange(half, dtype=np.float32) / ROT_DIM)
    ang = np.arange(SEQ, dtype=np.float32)[:, None] * inv_freq[None, :].astype(np.float32)
    cos, sin = np.cos(ang), np.sin(ang)
    ones = np.ones((SEQ, ATT_HEAD_DIM - ROT_DIM), np.float32)
    zeros = np.zeros((SEQ, ATT_HEAD_DIM - ROT_DIM), np.float32)
    zh = np.zeros((SEQ, half), np.float32)
    reps = LANES // ATT_HEAD_DIM
    rc = np.tile(np.concatenate([cos, cos, ones], axis=1), (1, reps))
    rm = np.tile(np.concatenate([-sin, zh, zeros], axis=1), (1, reps))
    rp = np.tile(np.concatenate([zh, sin, zeros], axis=1), (1, reps))
    return tuple(jnp.asarray(t, F32) for t in (rc, rm, rp))


def _mixer_inputs(w_in, w_gk_fwd, b_gk_fwd, w_gk_bwd, b_gk_bwd):
    hk = GLA_HEADS * GLA_DK
    hv = GLA_HEADS * GLA_DV
    w = w_in[0]
    o_lr = 2 * hk + 2 * hv
    o_aq = o_lr + 2 * GLA_RANK
    o_ak = o_aq + ATT_Q_HEADS * ATT_HEAD_DIM
    o_av = o_ak + ATT_KV_HEADS * ATT_HEAD_DIM
    hd = ATT_HEAD_DIM
    wa = w[:, :o_lr].astype(BF16)
    wlr = w[:, o_lr:o_aq].astype(BF16)
    dup = lambda m: jnp.concatenate([m[:, g * hd:(g + 1) * hd] for g in range(ATT_KV_HEADS) for _ in range(2)], axis=1)
    wb = jnp.concatenate([w[:, o_aq:o_ak], dup(w[:, o_ak:o_av])], axis=1).astype(BF16)
    wvt = dup(w[:, o_av:o_av + ATT_KV_HEADS * hd]).T.astype(BF16)
    zr = jnp.zeros((GLA_RANK, hk), F32)
    wgk = jnp.concatenate([jnp.concatenate([w_gk_fwd[0], zr], axis=1),
                           jnp.concatenate([zr, w_gk_bwd[0]], axis=1)], axis=0).astype(BF16)
    bgk = jnp.concatenate([b_gk_fwd[0], b_gk_bwd[0]])[None, :]
    return (wa, wlr, wgk, bgk, wb, wvt) + _rotary_tables()


def kernel(x, c, w_ada, b_ada, g_pre_mix, g_post_mix, w_in, w_gk_fwd, b_gk_fwd, w_gk_bwd, b_gk_bwd, g_gla_out,
           attn_sink, w_out, g_pre_ffn, g_post_ffn, w_router, b_router, w_gate_up, b_gate_up, w_down, b_down):
    assert x.shape == (BATCH, SEQ, D_MODEL) and w_ada.shape[0] == 1
    d = D_MODEL
    x2 = x.reshape(TOKENS, d)

    c_pad = jnp.pad(c, ((0, SUBLANES - BATCH), (0, 0)))
    mod = _ada(c_pad, w_ada[0], b_ada)[:BATCH].reshape(BATCH, 6, d)

    mixer_in = _mixer_inputs(w_in, w_gk_fwd, b_gk_fwd, w_gk_bwd, b_gk_bwd)
    q, k, v, gg, laf, lab, aq, ak2, avt = _inproj(x2, mod, g_pre_mix, *mixer_in)
    o_f, o_b = _gla(q, k, v, laf, lab)
    o_att = _attn(attn_sink[0], aq, ak2, avt)

    wr_t = w_router[0].T
    wrh = wr_t.astype(BF16)
    wrl = (wr_t - wrh.astype(F32)).astype(BF16)
    x1, h2_tiles, top_i, gates, rank, counts = _post(
        o_f, o_b, gg, o_att, x2, mod, g_gla_out, g_post_mix, g_pre_ffn, w_out[0].astype(BF16), wrh, wrl,
        b_router[0][:, None])

    pos, blocks = _route(top_i, rank, counts)

    src = _sc_source_rows(pos.reshape(TOP_K * TOKENS), MOE_ROWS)
    h2_rows = h2_tiles.reshape(TOKENS, PACK_SUB, LANES)
    ys = None
    first = 0
    for part_blocks in MOE_PART_BLOCKS:
        row0, n_rows = first * MOE_BM, part_blocks * MOE_BM
        xs_p = _sc_gather_rows(h2_rows, src[row0:row0 + n_rows])
        ys = _experts(first, part_blocks, blocks, xs_p.reshape(n_rows * PACK_SUB, LANES),
                      w_gate_up[0], b_gate_up[0], w_down[0], b_down[0], ys)
        first += part_blocks
    assert first == MOE_NB

    y4 = _sc_gather_rows(ys.reshape(MOE_ROWS, PACK_SUB, LANES), pos.reshape(TOP_K * TOKENS))
    out = _combine(gates, x1, mod, g_post_ffn, y4.reshape(TOP_K * TOKENS * PACK_SUB, LANES))
    return out.reshape(BATCH, SEQ, d)
```

```python
import functools

import jax
import jax.numpy as jnp
import numpy as np
from jax import lax
from jax.experimental import pallas as pl
from jax.experimental.pallas import tpu as pltpu
from jax.experimental.pallas import tpu_sc as plsc

F32 = jnp.float32
BF16 = jnp.bfloat16
I32 = jnp.int32

D_MODEL = 1024
BATCH = 2
SEQ = 8192
TOKENS = BATCH * SEQ
GLA_HEADS = 4
GLA_DV = 128
GLA_DK = 64
GLA_RANK = 16
GLA_GATE_NORMALIZER = 16.0
GLA_CHUNK = 64
ATT_Q_HEADS = 8
ATT_KV_HEADS = 2
ATT_HEAD_DIM = 64
ATT_WINDOW = 128
ATT_BLOCK = 128
ROT_DIM = 16
ROPE_THETA = 500000.0
N_EXPERTS = 32
TOP_K = 4
D_FF = 1024
SWIGLU_LIMIT = 7.0
SWIGLU_ALPHA = 1.702
NORM_EPS = 1e-6
NEG_INF = -1e30
LOG2_E = 1.4426950408889634

LANES = 128
SUBLANES = 8
PACK_COLS = D_MODEL // 2
PACK_SUB = PACK_COLS // LANES

TM_IN = 1024
IN_SUB = 4
GLA_GROUP = 16
ATT_GROUP = 16
TM_POST = 1024
POST_SUB = 4
MOE_BM = 256
MOE_ROWS = TOKENS * TOP_K + N_EXPERTS * MOE_BM
MOE_NB = MOE_ROWS // MOE_BM
MOE_SUB = 4
MOE_PART_BLOCKS = (32, 128, 128)
MOE_NB_PAD = ((MOE_NB + LANES - 1) // LANES) * LANES
SC_SCAN_CHUNK = 4096
SC_SCAN_UNROLL = 8
TM_COMB = 1024
SC_GATHER_WINDOW = 32
SC_GATHER_RING = 4
SC_BACKGROUND_RING = 1

NT_DIMS = (((1,), (1,)), ((), ()))
TN_DIMS = (((0,), (0,)), ((), ()))


def _params(semantics, vmem_mib):
    return pltpu.CompilerParams(dimension_semantics=semantics, vmem_limit_bytes=vmem_mib * 1024 * 1024)


def _rms(x, g):
    return x * lax.rsqrt(jnp.mean(x * x, axis=-1, keepdims=True) + NORM_EPS) * g


def _silu(x):
    return x * jax.nn.sigmoid(x)


def _pack_rows(ref, v):
    m = v.shape[0]
    bits = lax.bitcast_convert_type(v, jnp.uint32)
    word = lax.bitcast_convert_type(bits[:, :PACK_COLS] | (bits[:, PACK_COLS:] >> 16), I32)
    for s in range(PACK_SUB):
        ref[pl.ds(s, m, stride=PACK_SUB), :] = word[:, s * LANES:(s + 1) * LANES]


def _unpack_rows(ref, m):
    word = jnp.concatenate([ref[pl.ds(s, m, stride=PACK_SUB), :] for s in range(PACK_SUB)], axis=1)
    bits = lax.bitcast_convert_type(word, jnp.uint32)
    hi = lax.bitcast_convert_type(bits & jnp.uint32(0xFFFF0000), F32)
    lo = lax.bitcast_convert_type(bits << 16, F32)
    return hi, lo


def _ada_body(c_ref, w_ref, b_ref, o_ref):
    ca = _silu(c_ref[...]).astype(BF16)
    o_ref[...] = jnp.dot(ca, w_ref[...].astype(BF16), preferred_element_type=F32) + b_ref[...]


def _ada(c_pad, w_ada, b_ada):
    d = D_MODEL
    return pl.pallas_call(
        _ada_body,
        grid=(6,),
        in_specs=[
            pl.BlockSpec((SUBLANES, d), lambda j: (0, 0)),
            pl.BlockSpec((d, d), lambda j: (0, j)),
            pl.BlockSpec((1, d), lambda j: (0, j)),
        ],
        out_specs=pl.BlockSpec((SUBLANES, d), lambda j: (0, j)),
        out_shape=jax.ShapeDtypeStruct((SUBLANES, 6 * d), F32),
        compiler_params=_params(("arbitrary",), 32),
        name="ada",
    )(c_pad, w_ada, b_ada)


def _rotary(x, cos_t, msin_t, psin_t):
    width = x.shape[1]
    reps = width // LANES
    c = jnp.concatenate([cos_t] * reps, axis=1)
    m = jnp.concatenate([msin_t] * reps, axis=1)
    p = jnp.concatenate([psin_t] * reps, axis=1)
    half = ROT_DIM // 2
    return x * c + pltpu.roll(x, width - half, 1) * m + pltpu.roll(x, half, 1) * p


def _inproj_body(x_ref, mod_ref, g_ref, wa_ref, wlr_ref, wgk_ref, bgk_ref, wb_ref, wvt_ref, rc_ref, rm_ref, rp_ref,
                 q_ref, k_ref, v_ref, gg_ref, laf_ref, lab_ref, aq_ref, ak_ref, avt_ref):
    shift = mod_ref[0:1, :]
    scale = mod_ref[1:2, :]
    hk = GLA_HEADS * GLA_DK
    hv = GLA_HEADS * GLA_DV
    nq = ATT_Q_HEADS * ATT_HEAD_DIM
    nk = 2 * ATT_KV_HEADS * ATT_HEAD_DIM
    sub = x_ref.shape[0] // IN_SUB
    subs = [slice(s * sub, (s + 1) * sub) for s in range(IN_SUB)]

    gain = g_ref[...] * (1.0 + scale)

    def hidden(rows):
        return (_rms(x_ref[rows, :], gain) + shift).astype(BF16)

    def project(h):
        return (jnp.dot(h, wa_ref[...], preferred_element_type=F32),
                jnp.dot(h, wlr_ref[...], preferred_element_type=F32),
                jnp.dot(h, wb_ref[...], preferred_element_type=F32),
                lax.dot_general(wvt_ref[...], h, NT_DIMS, preferred_element_type=F32))

    def finish(rows, pa, plr, pb, pvt):
        q_ref[rows, :] = pa[:, 0:hk] * (GLA_DK ** -0.5)
        k_ref[rows, :] = pa[:, hk:2 * hk]
        v_ref[rows, :] = pa[:, 2 * hk:2 * hk + hv].astype(BF16)
        gg_ref[rows, :] = pa[:, 2 * hk + hv:2 * hk + 2 * hv]
        gk = jnp.dot(plr.astype(BF16), wgk_ref[...], preferred_element_type=F32) + bgk_ref[...]
        la = (jnp.minimum(gk, 0.0) - jnp.log1p(jnp.exp(-jnp.abs(gk)))) * (1.0 / GLA_GATE_NORMALIZER)
        laf_ref[rows, :] = la[:, 0:hk]
        lab_ref[rows, :] = la[:, hk:2 * hk]
        rc, rm, rp = rc_ref[rows, :], rm_ref[rows, :], rp_ref[rows, :]
        aq_ref[rows, :] = (_rotary(pb[:, 0:nq], rc, rm, rp) * (ATT_HEAD_DIM ** -0.5 * LOG2_E)).astype(BF16)
        ak_ref[rows, :] = _rotary(pb[:, nq:nq + nk], rc, rm, rp).astype(BF16)
        avt_ref[:, rows] = pvt.astype(BF16)

    hs = [hidden(rows) for rows in subs]
    ps = [project(h) for h in hs]
    for rows, p in zip(subs, ps):
        finish(rows, *p)


def _inproj(x2, mod, g_pre, wa, wlr, wgk, bgk, wb, wvt, rc, rm, rp):
    t, d = x2.shape
    tm = TM_IN
    tiles_per_seq = SEQ // tm
    hk = GLA_HEADS * GLA_DK
    hv = GLA_HEADS * GLA_DV
    nq = ATT_Q_HEADS * ATT_HEAD_DIM
    nk = 2 * ATT_KV_HEADS * ATT_HEAD_DIM

    def full(a):
        return pl.BlockSpec(a.shape, lambda i: (0,) * a.ndim)

    def rows(w):
        return pl.BlockSpec((tm, w), lambda i: (i, 0))

    def table():
        return pl.BlockSpec((tm, LANES), lambda i: (i % tiles_per_seq, 0))

    out_widths = [(hk, F32), (hk, F32), (hv, BF16), (hv, F32), (hk, F32), (hk, F32), (nq, BF16), (nk, BF16)]
    return pl.pallas_call(
        _inproj_body,
        grid=(t // tm,),
        in_specs=[
            rows(d),
            pl.BlockSpec((None, 6, d), lambda i: (i // tiles_per_seq, 0, 0)),
            full(g_pre), full(wa), full(wlr), full(wgk), full(bgk), full(wb), full(wvt),
            table(), table(), table(),
        ],
        out_specs=[rows(w) for w, _ in out_widths] + [pl.BlockSpec((nk, tm), lambda i: (0, i))],
        out_shape=[jax.ShapeDtypeStruct((t, w), dt) for w, dt in out_widths] + [jax.ShapeDtypeStruct((nk, t), BF16)],
        compiler_params=_params(("arbitrary",), 56),
        name="inproj",
    )(x2, mod, g_pre, wa, wlr, wgk, bgk, wb, wvt, rc, rm, rp)


def _gla_body(qf_ref, kf_ref, vf_ref, laf_ref, qb_ref, kb_ref, vb_ref, lab_ref, of_ref, ob_ref, sf_ref, sb_ref):
    @pl.when(pl.program_id(1) == 0)
    def _():
        sf_ref[...] = jnp.zeros_like(sf_ref)
        sb_ref[...] = jnp.zeros_like(sb_ref)

    c = GLA_CHUNK
    r_i = lax.broadcasted_iota(I32, (c, c), 0)
    c_i = lax.broadcasted_iota(I32, (c, c), 1)
    lower = c_i <= r_i
    upper = c_i >= r_i
    cum_f = jnp.where(lower, 1.0, 0.0).astype(BF16)
    cum_b = jnp.where(upper, 1.0, 0.0).astype(BF16)
    lane = lax.broadcasted_iota(I32, (1, LANES), 1)
    head_masks = (lane < GLA_DK, lane >= GLA_DK)

    fwd = [(qf_ref, kf_ref, laf_ref, vf_ref, of_ref, slice(g * c, (g + 1) * c), cum_f, lower, c - 1, c // 2 - 1)
           for g in range(GLA_GROUP)]
    bwd = [(qb_ref, kb_ref, lab_ref, vb_ref, ob_ref, slice(g * c, (g + 1) * c), cum_b, upper, 0, c // 2)
           for g in reversed(range(GLA_GROUP))]
    heads = range(GLA_HEADS)
    pair = [slice((h // 2) * LANES, (h // 2 + 1) * LANES) for h in heads]
    vcols = [slice(h * GLA_DV, (h + 1) * GLA_DV) for h in heads]

    def stage1(item):
        q_ref, k_ref, la_ref, v_ref, o_ref, rows, cum, tri, i_last, i_mid = item
        la = la_ref[rows, :]
        hi = la.astype(BF16)
        lo = (la - hi.astype(F32)).astype(BF16)
        b = jnp.dot(cum, hi, preferred_element_type=F32) + jnp.dot(cum, lo, preferred_element_type=F32)
        b_last = b[i_last:i_last + 1, :]
        b_mid = b[i_mid:i_mid + 1, :]
        q, k = q_ref[rows, :], k_ref[rows, :]
        return (q * jnp.exp(b - b_mid), (k * jnp.exp(b_mid - b)).astype(BF16), q * jnp.exp(b),
                (k * jnp.exp(b_last - b)).astype(BF16), jnp.exp(b_last))

    def stage2(item, pre):
        v_ref, rows, tri = item[3], item[5], item[7]
        qs, ks, qi, kst, decay = pre
        out = []
        for h in heads:
            mask = head_masks[h % 2]
            qs_h = jnp.where(mask, qs[:, pair[h]], 0.0).astype(BF16)
            sc = lax.dot_general(qs_h, ks[:, pair[h]], NT_DIMS, preferred_element_type=F32)
            v_h = v_ref[rows, vcols[h]]
            kv = lax.dot_general(v_h, kst[:, pair[h]], TN_DIMS, preferred_element_type=F32)
            out.append((jnp.where(tri, sc, 0.0).astype(BF16), kv,
                        jnp.where(mask, qi[:, pair[h]], 0.0).astype(BF16), v_h))
        return out

    def run(items, s_ref):
        pre = [stage1(it) for it in items]
        mid = [stage2(it, p) for it, p in zip(items, pre)]
        states = [s_ref[h] for h in heads]
        for it, p, m in zip(items, pre, mid):
            o_ref, rows, decay = it[4], it[5], p[4]
            for h in heads:
                sc, kv, qi_h, v_h = m[h]
                o = jnp.dot(sc, v_h, preferred_element_type=F32)
                o = o + lax.dot_general(qi_h, states[h].astype(BF16), NT_DIMS, preferred_element_type=F32)
                o_ref[rows, vcols[h]] = o
                states[h] = states[h] * decay[:, pair[h]] + kv
        for h in heads:
            s_ref[h] = states[h]

    run(fwd, sf_ref)
    run(bwd, sb_ref)


def _gla(q, k, v, laf, lab):
    t = q.shape[0]
    rows = GLA_GROUP * GLA_CHUNK
    ng = SEQ // rows
    hk = GLA_HEADS * GLA_DK
    hv = GLA_HEADS * GLA_DV

    def fwd(w):
        return pl.BlockSpec((rows, w), lambda b, n: (b * ng + n, 0))

    def bwd(w):
        return pl.BlockSpec((rows, w), lambda b, n: (b * ng + ng - 1 - n, 0))

    return pl.pallas_call(
        _gla_body,
        grid=(BATCH, ng),
        in_specs=[fwd(hk), fwd(hk), fwd(hv), fwd(hk), bwd(hk), bwd(hk), bwd(hv), bwd(hk)],
        out_specs=[fwd(hv), bwd(hv)],
        out_shape=[jax.ShapeDtypeStruct((t, hv), F32)] * 2,
        scratch_shapes=[pltpu.VMEM((GLA_HEADS, GLA_DV, 2 * GLA_DK), F32)] * 2,
        compiler_params=_params(("arbitrary", "arbitrary"), 32),
        name="gla",
    )(q, k, v, laf, q, k, v, lab)


def _attn_body(sink_ref, q_ref, kp_ref, kc_ref, kn_ref, vp_ref, vc_ref, vn_ref, o_ref):
    step = pl.program_id(1)
    last = pl.num_programs(1) - 1
    qb = ATT_BLOCK
    hd = ATT_HEAD_DIM
    k_all = jnp.concatenate([kp_ref[...], kc_ref[...], kn_ref[...]], axis=0)
    vt_all = jnp.concatenate([vp_ref[...], vc_ref[...], vn_ref[...]], axis=1)
    lane = lax.broadcasted_iota(I32, (1, LANES), 1)
    lo = lane < hd
    j_k = lax.broadcasted_iota(I32, (3 * qb, qb), 0)
    i_q = lax.broadcasted_iota(I32, (3 * qb, qb), 1)
    band = jnp.abs(j_k - qb - i_q) <= ATT_WINDOW
    sinks = [jnp.concatenate([jnp.full((1, qb), sink_ref[4 * g + r] * LOG2_E, F32) for r in range(4)], axis=1)
             for g in range(ATT_KV_HEADS)]
    ones_rows = jnp.ones((SUBLANES, 3 * qb), BF16)
    work = [(j, g) for j in range(ATT_GROUP) for g in range(ATT_KV_HEADS)]

    def scores(j, g):
        valid = band
        if j == 0:
            valid = valid & ((j_k >= qb) | (step > 0))
        if j == ATT_GROUP - 1:
            valid = valid & ((j_k < 2 * qb) | (step < last))
        valid4 = jnp.concatenate([valid] * 4, axis=1)
        rows = slice(j * qb, (j + 1) * qb)
        kg = k_all[j * qb:(j + 3) * qb, g * LANES:(g + 1) * LANES]
        qa = q_ref[rows, (2 * g) * LANES:(2 * g + 1) * LANES]
        qc = q_ref[rows, (2 * g + 1) * LANES:(2 * g + 2) * LANES]
        zero = jnp.zeros_like(qa)
        lhs = jnp.concatenate([jnp.where(lo, qa, zero), jnp.where(lo, zero, qa),
                               jnp.where(lo, qc, zero), jnp.where(lo, zero, qc)], axis=0)
        st = lax.dot_general(kg, lhs, NT_DIMS, preferred_element_type=F32)
        return jnp.concatenate([jnp.where(valid4[0:qb], st[0:qb], NEG_INF), st[qb:2 * qb],
                                jnp.where(valid4[2 * qb:3 * qb], st[2 * qb:3 * qb], NEG_INF)], axis=0)

    def softmax(st, g):
        sink = sinks[g]
        m = jnp.maximum(jnp.max(st, axis=0, keepdims=True), sink)
        return jnp.exp2(st - m).astype(BF16), jnp.exp2(sink - m)

    def output(j, g, p, p_sink):
        rows = slice(j * qb, (j + 1) * qb)
        vgt = vt_all[g * LANES:(g + 1) * LANES, j * qb:(j + 3) * qb]
        res = jnp.dot(jnp.concatenate([vgt, ones_rows], axis=0), p, preferred_element_type=F32)
        ot = res[0:LANES] * (1.0 / (res[LANES:LANES + 1] + p_sink))
        pair_a = jnp.concatenate([ot[0:hd, 0:qb], ot[hd:2 * hd, qb:2 * qb]], axis=0)
        pair_c = jnp.concatenate([ot[0:hd, 2 * qb:3 * qb], ot[hd:2 * hd, 3 * qb:4 * qb]], axis=0)
        o_ref[rows, (2 * g) * LANES:(2 * g + 1) * LANES] = pair_a.T.astype(o_ref.dtype)
        o_ref[rows, (2 * g + 1) * LANES:(2 * g + 2) * LANES] = pair_c.T.astype(o_ref.dtype)

    s_all = [scores(j, g) for j, g in work]
    p_all = [softmax(st, g) for st, (j, g) in zip(s_all, work)]
    for (j, g), (p, p_sink) in zip(work, p_all):
        output(j, g, p, p_sink)


def _attn(sink, aq, ak2, avt):
    t = aq.shape[0]
    qb = ATT_BLOCK
    nb = SEQ // qb
    steps = nb // ATT_GROUP
    nq = ATT_Q_HEADS * ATT_HEAD_DIM
    nk = 2 * ATT_KV_HEADS * ATT_HEAD_DIM

    def edge_block(b, n, shift):
        return b * nb + jnp.clip(n * ATT_GROUP + shift, 0, nb - 1)

    def k_edge(shift):
        return pl.BlockSpec((qb, nk), lambda b, n: (edge_block(b, n, shift), 0))

    def v_edge(shift):
        return pl.BlockSpec((nk, qb), lambda b, n: (0, edge_block(b, n, shift)))

    def group(w):
        return pl.BlockSpec((ATT_GROUP * qb, w), lambda b, n: (b * steps + n, 0))

    v_group = pl.BlockSpec((nk, ATT_GROUP * qb), lambda b, n: (0, b * steps + n))
    return pl.pallas_call(
        _attn_body,
        grid=(BATCH, steps),
        in_specs=[
            pl.BlockSpec(memory_space=pltpu.SMEM),
            group(nq),
            k_edge(-1), group(nk), k_edge(ATT_GROUP), v_edge(-1), v_group, v_edge(ATT_GROUP),
        ],
        out_specs=group(nq),
        out_shape=jax.ShapeDtypeStruct((t, nq), BF16),
        compiler_params=_params(("arbitrary", "arbitrary"), 48),
        name="attn",
    )(sink, aq, ak2, ak2, ak2, avt, avt, avt)


def _post_body(of_ref, ob_ref, gg_ref, oa_ref, x_ref, mod_ref, ggla_ref, gpm_ref, gpf_ref, wout_ref,
               wrh_ref, wrl_ref, br_ref,
               x1_ref, h2_ref, ti_ref, gt_ref, rk_ref, cnt_ref, base_ref):
    tm = TM_POST

    @pl.when(pl.program_id(0) == 0)
    def _():
        base_ref[...] = jnp.zeros_like(base_ref)

    gain1 = mod_ref[2:3, :] * gpm_ref[...]
    shift2 = mod_ref[3:4, :]
    gain2 = gpf_ref[...] * (1.0 + mod_ref[4:5, :])
    sub = tm // POST_SUB
    subs = [slice(s * sub, (s + 1) * sub) for s in range(POST_SUB)]

    def mixer_out(rows):
        og = of_ref[rows, :] + ob_ref[rows, :]
        gg = gg_ref[rows, :]
        parts = []
        for h in range(GLA_HEADS):
            cols = slice(h * GLA_DV, (h + 1) * GLA_DV)
            parts.append((_rms(og[:, cols], ggla_ref[...]) * _silu(gg[:, cols])).astype(BF16))
        return jnp.concatenate(parts + [oa_ref[rows, :]], axis=1)

    def ffn_in(s, y):
        rows = subs[s]
        x1 = x_ref[rows, :] + _rms(y, gain1)
        x1_ref[rows, :] = x1
        h2 = _rms(x1, gain2) + shift2
        hi = h2.astype(BF16)
        hi32 = hi.astype(F32)
        _pack_rows(h2_ref.at[pl.ds(s * sub * PACK_SUB, sub * PACK_SUB), :], hi32)
        return hi, (h2 - hi32).astype(BF16)

    o_subs = [mixer_out(rows) for rows in subs]
    y_subs = [jnp.dot(o, wout_ref[...], preferred_element_type=F32) for o in o_subs]
    split = [ffn_in(s, y) for s, y in enumerate(y_subs)]
    h2_hi = jnp.concatenate([hi for hi, _ in split], axis=0)
    h2_lo = jnp.concatenate([lo for _, lo in split], axis=0)

    wrh = wrh_ref[...]
    logits = (lax.dot_general(wrh, h2_hi, NT_DIMS, preferred_element_type=F32)
              + lax.dot_general(wrh, h2_lo, NT_DIMS, preferred_element_type=F32)
              + lax.dot_general(wrl_ref[...], h2_hi, NT_DIMS, preferred_element_type=F32)
              + br_ref[...])
    e_iota = lax.broadcasted_iota(I32, (N_EXPERTS, tm), 0)
    idxs, vals = [], []
    work = logits
    for _ in range(TOP_K):
        m = jnp.max(work, axis=0, keepdims=True)
        idx = jnp.min(jnp.where(work == m, e_iota, N_EXPERTS), axis=0, keepdims=True)
        idxs.append(idx)
        vals.append(m)
        work = jnp.where(e_iota == idx, -jnp.inf, work)
    exps = [jnp.exp(v - vals[0]) for v in vals]
    inv = 1.0 / (exps[0] + exps[1] + exps[2] + exps[3])
    gt_ref[...] = jnp.concatenate([e * inv for e in exps], axis=0)
    ti_ref[...] = jnp.concatenate(idxs, axis=0)

    onehots = [e_iota == idx for idx in idxs]
    member = jnp.where(onehots[0] | onehots[1] | onehots[2] | onehots[3], 1.0, 0.0)
    t_row = lax.broadcasted_iota(I32, (tm, tm), 0)
    t_col = lax.broadcasted_iota(I32, (tm, tm), 1)
    strict = jnp.where(t_row < t_col, 1.0, 0.0).astype(BF16)
    before = base_ref[...] + jnp.dot(member.astype(BF16), strict, preferred_element_type=F32)
    rk_ref[...] = jnp.concatenate(
        [jnp.sum(jnp.where(oh, before, 0.0), axis=0, keepdims=True) for oh in onehots], axis=0).astype(I32)
    new_base = base_ref[...] + jnp.sum(member, axis=1, keepdims=True)
    base_ref[...] = new_base
    cnt_ref[...] = jnp.broadcast_to(new_base, cnt_ref.shape)


def _post(o_f, o_b, gg, o_att, x2, mod, g_gla, g_pm, g_pf, wout, wrh, wrl, br):
    t, d = x2.shape
    tm = TM_POST
    tiles_per_seq = SEQ // tm
    hv = GLA_HEADS * GLA_DV

    def full(a):
        return pl.BlockSpec(a.shape, lambda i: (0,) * a.ndim)

    def rows(w):
        return pl.BlockSpec((tm, w), lambda i: (i, 0))

    def lanes():
        return pl.BlockSpec((TOP_K, tm), lambda i: (0, i))

    return pl.pallas_call(
        _post_body,
        grid=(t // tm,),
        in_specs=[
            rows(hv), rows(hv), rows(hv), rows(hv), rows(d),
            pl.BlockSpec((None, 6, d), lambda i: (i // tiles_per_seq, 0, 0)),
            full(g_gla), full(g_pm), full(g_pf), full(wout), full(wrh), full(wrl), full(br),
        ],
        out_specs=[
            rows(d),
            pl.BlockSpec((tm * PACK_SUB, LANES), lambda i: (i, 0)),
            lanes(), lanes(), lanes(),
            pl.BlockSpec((N_EXPERTS, LANES), lambda i: (0, 0)),
        ],
        out_shape=[
            jax.ShapeDtypeStruct((t, d), F32),
            jax.ShapeDtypeStruct((t * PACK_SUB, LANES), I32),
            jax.ShapeDtypeStruct((TOP_K, t), I32),
            jax.ShapeDtypeStruct((TOP_K, t), F32),
            jax.ShapeDtypeStruct((TOP_K, t), I32),
            jax.ShapeDtypeStruct((N_EXPERTS, LANES), F32),
        ],
        scratch_shapes=[pltpu.VMEM((N_EXPERTS, 1), F32)],
        compiler_params=_params(("arbitrary",), 48),
        name="post",
    )(o_f, o_b, gg, o_att, x2, mod, g_gla, g_pm, g_pf, wout, wrh, wrl, br)


def _route_body(ti_ref, rk_ref, cnt_ref, pos_ref, blk_ref):
    cnt = cnt_ref[...]
    padded = jnp.floor((cnt + (MOE_BM - 1)) * (1.0 / MOE_BM)) * MOE_BM
    starts, ends = [], []
    acc = jnp.zeros((1, LANES), F32)
    for e in range(N_EXPERTS):
        starts.append(acc)
        acc = acc + padded[e:e + 1, :]
        ends.append(acc)
    ti = ti_ref[...]
    off = jnp.zeros(ti.shape, F32)
    for e in range(N_EXPERTS):
        off = jnp.where(ti == e, starts[e][:, 0:1], off)
    pos_ref[...] = rk_ref[...] + off.astype(I32)

    def owner_of(row):
        n_le = jnp.zeros(row.shape, I32)
        for e in range(N_EXPERTS):
            n_le = n_le + jnp.where(ends[e][:, 0:1] <= row, 1, 0)
        return jnp.minimum(n_le, N_EXPERTS - 1)

    block_start = lax.broadcasted_iota(I32, (1, MOE_NB_PAD), 1).astype(F32) * MOE_BM
    owner = owner_of(block_start)
    nxt = jnp.zeros((1, MOE_NB_PAD), I32)
    nxt_blk = jnp.zeros((1, MOE_NB_PAD), I32)
    slot = jnp.zeros((1, MOE_NB_PAD), I32)
    ordinal = jnp.zeros((1, 1), I32)
    for e in range(N_EXPERTS):
        end_e = ends[e][:, 0:1]
        nxt = jnp.where(owner == e, jnp.where(end_e < acc[:, 0:1], owner_of(end_e), -1), nxt)
        nxt_blk = jnp.where(owner == e, (end_e * (1.0 / MOE_BM)).astype(I32), nxt_blk)
        slot = jnp.where(owner == e, ordinal & 1, slot)
        ordinal = ordinal + jnp.where(padded[e:e + 1, 0:1] > 0, 1, 0)
    used = jnp.broadcast_to((acc[:, 0:1] * (1.0 / MOE_BM)).astype(I32), (1, MOE_NB_PAD))
    blk_ref[...] = jnp.concatenate(
        [owner, nxt, used, nxt_blk, slot, jnp.zeros((SUBLANES - 5, MOE_NB_PAD), I32)], axis=0)


def _route(top_i, rank, counts):
    return pl.pallas_call(
        _route_body,
        out_shape=[
            jax.ShapeDtypeStruct(top_i.shape, I32),
            jax.ShapeDtypeStruct((SUBLANES, MOE_NB_PAD), I32),
        ],
        compiler_params=pltpu.CompilerParams(vmem_limit_bytes=32 * 1024 * 1024),
        name="route",
    )(top_i, rank, counts)


def _sc_workers():
    info = plsc.get_sparse_core_info()
    return info.num_cores, info.num_subcores, info.num_lanes


def _sc_gather_loop(table_hbm, out_hbm, idx_v, base, chunks, bufs, gather_sems, write_sems):
    window = SC_GATHER_WINDOW
    ring = len(bufs)

    def fetch(c, b):
        return pltpu.make_async_copy(table_hbm.at[idx_v.at[pl.ds(c * window, window)]], bufs[b], gather_sems[b])

    def flush(c, b):
        return pltpu.make_async_copy(bufs[b], out_hbm.at[pl.ds(base + c * window, window)], write_sems[b])

    for b in range(ring):
        fetch(b, b).start()

    @pl.loop(0, chunks, step=ring)
    def _(c0):
        for b in range(ring):
            c = c0 + b
            fetch(c, b).wait()
            flush(c, b).start()

            @pl.when(c + ring < chunks)
            def _():
                flush(c, b).wait()
                fetch(c + ring, b).start()

    for b in range(ring):
        flush(chunks - ring + b, b).wait()


def _sc_source_rows(pos_flat, n_rows):
    cores, subcores, lanes = _sc_workers()
    workers = cores * subcores
    per_worker = n_rows // workers
    n_assign = pos_flat.shape[0]
    scan = SC_SCAN_CHUNK
    assert per_worker * workers == n_rows and per_worker % lanes == 0
    assert n_assign % scan == 0 and scan % lanes == 0 and TOKENS & (TOKENS - 1) == 0
    mesh = plsc.VectorSubcoreMesh(core_axis_name="core", subcore_axis_name="subcore")

    @functools.partial(
        pl.kernel,
        out_type=jax.ShapeDtypeStruct((n_rows,), I32),
        mesh=mesh,
        scratch_types=[pltpu.VMEM((per_worker,), I32), pltpu.VMEM((scan,), I32)],
        compiler_params=pltpu.CompilerParams(needs_layout_passes=False),
        name="sc_source_rows",
    )
    def invert(pos_hbm, out_hbm, src_v, pos_v):
        wid = lax.axis_index("subcore") * cores + lax.axis_index("core")
        base = wid * per_worker
        lane = lax.iota(I32, lanes)

        @pl.loop(0, per_worker, step=lanes)
        def _(j):
            src_v[pl.ds(j, lanes)] = (base + j + lane) & (TOKENS - 1)

        @pl.loop(0, n_assign, step=scan)
        def _(a0):
            pltpu.sync_copy(pos_hbm.at[pl.ds(a0, scan)], pos_v)

            @plsc.parallel_loop(0, scan, step=lanes, unroll=SC_SCAN_UNROLL)
            def _(j):
                rel = pos_v[pl.ds(j, lanes)] - base
                mine = (rel >= 0) & (rel < per_worker)
                tok = (a0 + j + lane) & (TOKENS - 1)
                plsc.store_scatter(src_v, [jnp.where(mine, rel, 0)], tok, mask=mine)

        pltpu.sync_copy(src_v, out_hbm.at[pl.ds(base, per_worker)])

    return invert(pos_flat)


def _sc_gather_rows(table, idx, ring=SC_GATHER_RING):
    cores, subcores, _ = _sc_workers()
    workers = cores * subcores
    n = idx.shape[0]
    window = SC_GATHER_WINDOW
    per_worker = n // workers
    chunks = per_worker // window
    assert per_worker * workers == n and chunks * window == per_worker and chunks % ring == 0
    row_shape = table.shape[1:]
    mesh = plsc.VectorSubcoreMesh(core_axis_name="core", subcore_axis_name="subcore")

    @functools.partial(
        pl.kernel,
        out_type=jax.ShapeDtypeStruct((n,) + row_shape, table.dtype),
        mesh=mesh,
        scratch_types=[pltpu.VMEM((per_worker,), I32)]
        + [pltpu.VMEM((window,) + row_shape, table.dtype)] * ring
        + [pltpu.SemaphoreType.DMA] * (2 * ring),
        name="sc_gather_rows",
    )
    def gather(table_hbm, idx_hbm, out_hbm, idx_v, *scratch):
        wid = lax.axis_index("subcore") * cores + lax.axis_index("core")
        base = wid * per_worker
        pltpu.sync_copy(idx_hbm.at[pl.ds(base, per_worker)], idx_v)
        _sc_gather_loop(table_hbm, out_hbm, idx_v, base, chunks,
                        scratch[:ring], scratch[ring:2 * ring], scratch[2 * ring:])

    return gather(table, idx)


def _experts_body(first, be_ref, nxt_ref, nxtblk_ref, slot_ref, nu_ref, xs_ref, bgu_ref, bd_ref, wgu_hbm, wd_hbm,
                  *rest):
    ys_ref, wgu_f32, wd_f32, sems = rest[-4:]
    step = pl.program_id(0)
    end = first + pl.num_programs(0) * MOE_SUB
    bm = MOE_BM
    n_used = nu_ref[0]

    def fetch(e, slot):
        return (pltpu.make_async_copy(wgu_hbm.at[e], wgu_f32.at[slot], sems.at[0, slot]),
                pltpu.make_async_copy(wd_hbm.at[e], wd_f32.at[slot], sems.at[1, slot]))

    @pl.when((step == 0) & (first < n_used))
    def _():
        for cp in fetch(be_ref[first], slot_ref[first]):
            cp.start()

    for sub in range(MOE_SUB):
        i = first + step * MOE_SUB + sub
        e = be_ref[i]
        slot = slot_ref[i]
        used = i < n_used
        fresh = (i == first) | (e != be_ref[jnp.maximum(i - 1, 0)])
        rows = pl.ds(sub * bm * PACK_SUB, bm * PACK_SUB)
        xs_sub = xs_ref.at[rows, :]
        ys_sub = ys_ref.at[rows, :]

        @pl.when(used & fresh)
        def _():
            for cp in fetch(e, slot):
                cp.wait()

            @pl.when((nxt_ref[i] >= 0) & (nxtblk_ref[i] < end))
            def _():
                for cp in fetch(nxt_ref[i], 1 - slot):
                    cp.start()

        @pl.when(used)
        def _():
            x = jnp.concatenate(_unpack_rows(xs_sub, bm), axis=1).astype(BF16)
            gu = jnp.dot(x, wgu_f32[slot].astype(BF16), preferred_element_type=F32) + bgu_ref[pl.ds(e, 1), :]
            gate = jnp.minimum(gu[:, 0:D_FF], SWIGLU_LIMIT)
            up = jnp.clip(gu[:, D_FF:2 * D_FF], -SWIGLU_LIMIT, SWIGLU_LIMIT)
            act = ((up + 1.0) * gate * jax.nn.sigmoid(SWIGLU_ALPHA * gate)).astype(BF16)
            y = jnp.dot(act, wd_f32[slot].astype(BF16), preferred_element_type=F32) + bd_ref[pl.ds(e, 1), :]
            _pack_rows(ys_sub, y.astype(BF16).astype(F32))

        @pl.when(jnp.logical_not(used))
        def _():
            ys_sub[...] = jnp.zeros((bm * PACK_SUB, LANES), I32)


def _experts(first, part_blocks, blocks, xs_part, w_gate_up, b_gate_up, w_down, b_down, ys_prev):
    rows = MOE_SUB * MOE_BM * PACK_SUB
    d = D_MODEL
    steps = part_blocks // MOE_SUB
    assert steps * MOE_SUB == part_blocks and first % MOE_SUB == 0

    def x_block(i, be, nx, nb, sl, nu):
        last = jnp.maximum((jnp.minimum(nu[0], first + part_blocks) - 1 - first) // MOE_SUB, 0)
        return jnp.minimum(i, last)

    in_specs = [
        pl.BlockSpec((rows, LANES), lambda i, be, nx, nb, sl, nu: (x_block(i, be, nx, nb, sl, nu), 0)),
        pl.BlockSpec(b_gate_up.shape, lambda i, be, nx, nb, sl, nu: (0, 0)),
        pl.BlockSpec(b_down.shape, lambda i, be, nx, nb, sl, nu: (0, 0)),
        pl.BlockSpec(memory_space=pl.ANY),
        pl.BlockSpec(memory_space=pl.ANY),
    ]
    operands = [blocks[0, :MOE_NB], blocks[1, :MOE_NB], blocks[3, :MOE_NB], blocks[4, :MOE_NB], blocks[2, :1],
                xs_part, b_gate_up, b_down, w_gate_up, w_down]
    aliases = {}
    if ys_prev is not None:
        in_specs.append(pl.BlockSpec(memory_space=pl.ANY))
        aliases = {len(operands): 0}
        operands.append(ys_prev)
    grid_spec = pltpu.PrefetchScalarGridSpec(
        num_scalar_prefetch=5,
        grid=(steps,),
        in_specs=in_specs,
        out_specs=pl.BlockSpec((rows, LANES), lambda i, be, nx, nb, sl, nu: (first // MOE_SUB + i, 0)),
        scratch_shapes=[
            pltpu.VMEM((2, d, 2 * D_FF), F32), pltpu.VMEM((2, D_FF, d), F32),
            pltpu.SemaphoreType.DMA((2, 2)),
        ],
    )
    return pl.pallas_call(
        functools.partial(_experts_body, first),
        grid_spec=grid_spec,
        out_shape=jax.ShapeDtypeStruct((MOE_ROWS * PACK_SUB, LANES), I32),
        input_output_aliases=aliases,
        compiler_params=_params(("arbitrary",), 48),
        name="experts",
    )(*operands)


def _combine_body(gates_ref, x1_ref, mod_ref, gpost_ref, y0_ref, y1_ref, y2_ref, y3_ref, o_ref):
    tm = TM_COMB
    gates = jnp.concatenate([gates_ref[...], jnp.zeros((SUBLANES - TOP_K, tm), F32)], axis=0).T
    y_hi = jnp.zeros((tm, PACK_COLS), F32)
    y_lo = jnp.zeros((tm, PACK_COLS), F32)
    for k, yk_ref in enumerate((y0_ref, y1_ref, y2_ref, y3_ref)):
        hi, lo = _unpack_rows(yk_ref, tm)
        y_hi = y_hi + hi * gates[:, k:k + 1]
        y_lo = y_lo + lo * gates[:, k:k + 1]
    y = jnp.concatenate([y_hi, y_lo], axis=1)
    gain = mod_ref[5:6, :] * gpost_ref[...]
    o_ref[...] = x1_ref[...] + _rms(y, gain)


def _combine(gates, x1, mod, g_post, y4):
    t, d = x1.shape
    tm = TM_COMB
    tiles = t // tm
    tiles_per_seq = SEQ // tm

    def slab(k):
        return pl.BlockSpec((tm * PACK_SUB, LANES), lambda i: (k * tiles + i, 0))

    return pl.pallas_call(
        _combine_body,
        grid=(tiles,),
        in_specs=[
            pl.BlockSpec((TOP_K, tm), lambda i: (0, i)),
            pl.BlockSpec((tm, d), lambda i: (i, 0)),
            pl.BlockSpec((None, 6, d), lambda i: (i // tiles_per_seq, 0, 0)),
            pl.BlockSpec(g_post.shape, lambda i: (0, 0)),
            slab(0), slab(1), slab(2), slab(3),
        ],
        out_specs=pl.BlockSpec((tm, d), lambda i: (i, 0)),
        out_shape=jax.ShapeDtypeStruct((t, d), F32),
        compiler_params=_params(("arbitrary",), 48),
        name="combine",
    )(gates, x1, mod, g_post, y4, y4, y4, y4)


def _rotary_tables():
    half = ROT_DIM // 2
    inv_freq = ROPE_THETA ** (-2.0 * np.arange(half, dtype=np.float32) / ROT_DIM)
    ang = np.arange(SEQ, dtype=np.float32)[:, None] * inv_freq[None, :].astype(np.float32)
    cos, sin = np.cos(ang), np.sin(ang)
    ones = np.ones((SEQ, ATT_HEAD_DIM - ROT_DIM), np.float32)
    zeros = np.zeros((SEQ, ATT_HEAD_DIM - ROT_DIM), np.float32)
    zh = np.zeros((SEQ, half), np.float32)
    reps = LANES // ATT_HEAD_DIM
    rc = np.tile(np.concatenate([cos, cos, ones], axis=1), (1, reps))
    rm = np.tile(np.concatenate([-sin, zh, zeros], axis=1), (1, reps))
    rp = np.tile(np.concatenate([zh, sin, zeros], axis=1), (1, reps))
    return tuple(jnp.asarray(t, F32) for t in (rc, rm, rp))


def _mixer_inputs(w_in, w_gk_fwd, b_gk_fwd, w_gk_bwd, b_gk_bwd):
    hk = GLA_HEADS * GLA_DK
    hv = GLA_HEADS * GLA_DV
    w = w_in[0]
    o_lr = 2 * hk + 2 * hv
    o_aq = o_lr + 2 * GLA_RANK
    o_ak = o_aq + ATT_Q_HEADS * ATT_HEAD_DIM
    o_av = o_ak + ATT_KV_HEADS * ATT_HEAD_DIM
    hd = ATT_HEAD_DIM
    wa = w[:, :o_lr].astype(BF16)
    wlr = w[:, o_lr:o_aq].astype(BF16)
    dup = lambda m: jnp.concatenate([m[:, g * hd:(g + 1) * hd] for g in range(ATT_KV_HEADS) for _ in range(2)], axis=1)
    wb = jnp.concatenate([w[:, o_aq:o_ak], dup(w[:, o_ak:o_av])], axis=1).astype(BF16)
    wvt = dup(w[:, o_av:o_av + ATT_KV_HEADS * hd]).T.astype(BF16)
    zr = jnp.zeros((GLA_RANK, hk), F32)
    wgk = jnp.concatenate([jnp.concatenate([w_gk_fwd[0], zr], axis=1),
                           jnp.concatenate([zr, w_gk_bwd[0]], axis=1)], axis=0).astype(BF16)
    bgk = jnp.concatenate([b_gk_fwd[0], b_gk_bwd[0]])[None, :]
    return (wa, wlr, wgk, bgk, wb, wvt) + _rotary_tables()


def kernel(x, c, w_ada, b_ada, g_pre_mix, g_post_mix, w_in, w_gk_fwd, b_gk_fwd, w_gk_bwd, b_gk_bwd, g_gla_out,
           attn_sink, w_out, g_pre_ffn, g_post_ffn, w_router, b_router, w_gate_up, b_gate_up, w_down, b_down):
    assert x.shape == (BATCH, SEQ, D_MODEL) and w_ada.shape[0] == 1
    d = D_MODEL
    x2 = x.reshape(TOKENS, d)

    c_pad = jnp.pad(c, ((0, SUBLANES - BATCH), (0, 0)))
    mod = _ada(c_pad, w_ada[0], b_ada)[:BATCH].reshape(BATCH, 6, d)

    mixer_in = _mixer_inputs(w_in, w_gk_fwd, b_gk_fwd, w_gk_bwd, b_gk_bwd)
    q, k, v, gg, laf, lab, aq, ak2, avt = _inproj(x2, mod, g_pre_mix, *mixer_in)
    o_f, o_b = _gla(q, k, v, laf, lab)
    o_att = _attn(attn_sink[0], aq, ak2, avt)

    wr_t = w_router[0].T
    wrh = wr_t.astype(BF16)
    wrl = (wr_t - wrh.astype(F32)).astype(BF16)
    x1, h2_tiles, top_i, gates, rank, counts = _post(
        o_f, o_b, gg, o_att, x2, mod, g_gla_out, g_post_mix, g_pre_ffn, w_out[0].astype(BF16), wrh, wrl,
        b_router[0][:, None])

    pos, blocks = _route(top_i, rank, counts)

    src = _sc_source_rows(pos.reshape(TOP_K * TOKENS), MOE_ROWS)
    h2_rows = h2_tiles.reshape(TOKENS, PACK_SUB, LANES)
    ys = None
    first = 0
    for part_blocks in MOE_PART_BLOCKS:
        row0, n_rows = first * MOE_BM, part_blocks * MOE_BM
        xs_p = _sc_gather_rows(h2_rows, src[row0:row0 + n_rows], SC_GATHER_RING if first == 0 else SC_BACKGROUND_RING)
        ys = _experts(first, part_blocks, blocks, xs_p.reshape(n_rows * PACK_SUB, LANES),
                      w_gate_up[0], b_gate_up[0], w_down[0], b_down[0], ys)
        first += part_blocks
    assert first == MOE_NB

    y4 = _sc_gather_rows(ys.reshape(MOE_ROWS, PACK_SUB, LANES), pos.reshape(TOP_K * TOKENS))
    out = _combine(gates, x1, mod, g_post_ffn, y4.reshape(TOP_K * TOKENS * PACK_SUB, LANES))
    return out.reshape(BATCH, SEQ, d)
```

```python
import functools

import jax
import jax.numpy as jnp
import numpy as np
from jax import lax
from jax.experimental import pallas as pl
from jax.experimental.pallas import tpu as pltpu
from jax.experimental.pallas import tpu_sc as plsc

F32 = jnp.float32
BF16 = jnp.bfloat16
I32 = jnp.int32

D_MODEL = 1024
BATCH = 2
SEQ = 8192
TOKENS = BATCH * SEQ
GLA_HEADS = 4
GLA_DV = 128
GLA_DK = 64
GLA_RANK = 16
GLA_GATE_NORMALIZER = 16.0
GLA_CHUNK = 64
ATT_Q_HEADS = 8
ATT_KV_HEADS = 2
ATT_HEAD_DIM = 64
ATT_WINDOW = 128
ATT_BLOCK = 128
ROT_DIM = 16
ROPE_THETA = 500000.0
N_EXPERTS = 32
TOP_K = 4
D_FF = 1024
SWIGLU_LIMIT = 7.0
SWIGLU_ALPHA = 1.702
NORM_EPS = 1e-6
NEG_INF = -1e30
LOG2_E = 1.4426950408889634

LANES = 128
SUBLANES = 8
PACK_COLS = D_MODEL // 2
PACK_SUB = PACK_COLS // LANES

TM_IN = 1024
IN_SUB = 4
GLA_GROUP = 16
ATT_GROUP = 16
TM_POST = 1024
POST_SUB = 4
MOE_BM = 256
MOE_ROWS = TOKENS * TOP_K + N_EXPERTS * MOE_BM
MOE_NB = MOE_ROWS // MOE_BM
MOE_SUB = 4
MOE_PART_BLOCKS = (32, 128, 128)
MOE_NB_PAD = ((MOE_NB + LANES - 1) // LANES) * LANES
SC_SCAN_CHUNK = 4096
SC_SCAN_UNROLL = 8
TM_COMB = 1024
SC_GATHER_WINDOW = 32
SC_GATHER_RING = 4
MOE_PART_RINGS = (SC_GATHER_RING, 2, 1)

NT_DIMS = (((1,), (1,)), ((), ()))
TN_DIMS = (((0,), (0,)), ((), ()))


def _params(semantics, vmem_mib):
    return pltpu.CompilerParams(dimension_semantics=semantics, vmem_limit_bytes=vmem_mib * 1024 * 1024)


def _rms(x, g):
    return x * lax.rsqrt(jnp.mean(x * x, axis=-1, keepdims=True) + NORM_EPS) * g


def _silu(x):
    return x * jax.nn.sigmoid(x)


def _pack_rows(ref, v):
    m = v.shape[0]
    bits = lax.bitcast_convert_type(v, jnp.uint32)
    word = lax.bitcast_convert_type(bits[:, :PACK_COLS] | (bits[:, PACK_COLS:] >> 16), I32)
    for s in range(PACK_SUB):
        ref[pl.ds(s, m, stride=PACK_SUB), :] = word[:, s * LANES:(s + 1) * LANES]


def _unpack_rows(ref, m):
    word = jnp.concatenate([ref[pl.ds(s, m, stride=PACK_SUB), :] for s in range(PACK_SUB)], axis=1)
    bits = lax.bitcast_convert_type(word, jnp.uint32)
    hi = lax.bitcast_convert_type(bits & jnp.uint32(0xFFFF0000), F32)
    lo = lax.bitcast_convert_type(bits << 16, F32)
    return hi, lo


def _ada_body(c_ref, w_ref, b_ref, o_ref):
    ca = _silu(c_ref[...]).astype(BF16)
    o_ref[...] = jnp.dot(ca, w_ref[...].astype(BF16), preferred_element_type=F32) + b_ref[...]


def _ada(c_pad, w_ada, b_ada):
    d = D_MODEL
    return pl.pallas_call(
        _ada_body,
        grid=(6,),
        in_specs=[
            pl.BlockSpec((SUBLANES, d), lambda j: (0, 0)),
            pl.BlockSpec((d, d), lambda j: (0, j)),
            pl.BlockSpec((1, d), lambda j: (0, j)),
        ],
        out_specs=pl.BlockSpec((SUBLANES, d), lambda j: (0, j)),
        out_shape=jax.ShapeDtypeStruct((SUBLANES, 6 * d), F32),
        compiler_params=_params(("arbitrary",), 32),
        name="ada",
    )(c_pad, w_ada, b_ada)


def _rotary(x, cos_t, msin_t, psin_t):
    width = x.shape[1]
    reps = width // LANES
    c = jnp.concatenate([cos_t] * reps, axis=1)
    m = jnp.concatenate([msin_t] * reps, axis=1)
    p = jnp.concatenate([psin_t] * reps, axis=1)
    half = ROT_DIM // 2
    return x * c + pltpu.roll(x, width - half, 1) * m + pltpu.roll(x, half, 1) * p


def _inproj_body(x_ref, mod_ref, g_ref, wa_ref, wlr_ref, wgk_ref, bgk_ref, wb_ref, wvt_ref, rc_ref, rm_ref, rp_ref,
                 q_ref, k_ref, v_ref, gg_ref, laf_ref, lab_ref, aq_ref, ak_ref, avt_ref):
    shift = mod_ref[0:1, :]
    scale = mod_ref[1:2, :]
    hk = GLA_HEADS * GLA_DK
    hv = GLA_HEADS * GLA_DV
    nq = ATT_Q_HEADS * ATT_HEAD_DIM
    nk = 2 * ATT_KV_HEADS * ATT_HEAD_DIM
    sub = x_ref.shape[0] // IN_SUB
    subs = [slice(s * sub, (s + 1) * sub) for s in range(IN_SUB)]

    gain = g_ref[...] * (1.0 + scale)

    def hidden(rows):
        return (_rms(x_ref[rows, :], gain) + shift).astype(BF16)

    def project(h):
        return (jnp.dot(h, wa_ref[...], preferred_element_type=F32),
                jnp.dot(h, wlr_ref[...], preferred_element_type=F32),
                jnp.dot(h, wb_ref[...], preferred_element_type=F32),
                lax.dot_general(wvt_ref[...], h, NT_DIMS, preferred_element_type=F32))

    def finish(rows, pa, plr, pb, pvt):
        q_ref[rows, :] = pa[:, 0:hk] * (GLA_DK ** -0.5)
        k_ref[rows, :] = pa[:, hk:2 * hk]
        v_ref[rows, :] = pa[:, 2 * hk:2 * hk + hv].astype(BF16)
        gg_ref[rows, :] = pa[:, 2 * hk + hv:2 * hk + 2 * hv]
        gk = jnp.dot(plr.astype(BF16), wgk_ref[...], preferred_element_type=F32) + bgk_ref[...]
        la = (jnp.minimum(gk, 0.0) - jnp.log1p(jnp.exp(-jnp.abs(gk)))) * (1.0 / GLA_GATE_NORMALIZER)
        laf_ref[rows, :] = la[:, 0:hk]
        lab_ref[rows, :] = la[:, hk:2 * hk]
        rc, rm, rp = rc_ref[rows, :], rm_ref[rows, :], rp_ref[rows, :]
        aq_ref[rows, :] = (_rotary(pb[:, 0:nq], rc, rm, rp) * (ATT_HEAD_DIM ** -0.5 * LOG2_E)).astype(BF16)
        ak_ref[rows, :] = _rotary(pb[:, nq:nq + nk], rc, rm, rp).astype(BF16)
        avt_ref[:, rows] = pvt.astype(BF16)

    hs = [hidden(rows) for rows in subs]
    ps = [project(h) for h in hs]
    for rows, p in zip(subs, ps):
        finish(rows, *p)


def _inproj(x2, mod, g_pre, wa, wlr, wgk, bgk, wb, wvt, rc, rm, rp):
    t, d = x2.shape
    tm = TM_IN
    tiles_per_seq = SEQ // tm
    hk = GLA_HEADS * GLA_DK
    hv = GLA_HEADS * GLA_DV
    nq = ATT_Q_HEADS * ATT_HEAD_DIM
    nk = 2 * ATT_KV_HEADS * ATT_HEAD_DIM

    def full(a):
        return pl.BlockSpec(a.shape, lambda i: (0,) * a.ndim)

    def rows(w):
        return pl.BlockSpec((tm, w), lambda i: (i, 0))

    def table():
        return pl.BlockSpec((tm, LANES), lambda i: (i % tiles_per_seq, 0))

    out_widths = [(hk, F32), (hk, F32), (hv, BF16), (hv, F32), (hk, F32), (hk, F32), (nq, BF16), (nk, BF16)]
    return pl.pallas_call(
        _inproj_body,
        grid=(t // tm,),
        in_specs=[
            rows(d),
            pl.BlockSpec((None, 6, d), lambda i: (i // tiles_per_seq, 0, 0)),
            full(g_pre), full(wa), full(wlr), full(wgk), full(bgk), full(wb), full(wvt),
            table(), table(), table(),
        ],
        out_specs=[rows(w) for w, _ in out_widths] + [pl.BlockSpec((nk, tm), lambda i: (0, i))],
        out_shape=[jax.ShapeDtypeStruct((t, w), dt) for w, dt in out_widths] + [jax.ShapeDtypeStruct((nk, t), BF16)],
        compiler_params=_params(("arbitrary",), 56),
        name="inproj",
    )(x2, mod, g_pre, wa, wlr, wgk, bgk, wb, wvt, rc, rm, rp)


def _gla_body(qf_ref, kf_ref, vf_ref, laf_ref, qb_ref, kb_ref, vb_ref, lab_ref, of_ref, ob_ref, sf_ref, sb_ref):
    @pl.when(pl.program_id(1) == 0)
    def _():
        sf_ref[...] = jnp.zeros_like(sf_ref)
        sb_ref[...] = jnp.zeros_like(sb_ref)

    c = GLA_CHUNK
    r_i = lax.broadcasted_iota(I32, (c, c), 0)
    c_i = lax.broadcasted_iota(I32, (c, c), 1)
    lower = c_i <= r_i
    upper = c_i >= r_i
    cum_f = jnp.where(lower, 1.0, 0.0).astype(BF16)
    cum_b = jnp.where(upper, 1.0, 0.0).astype(BF16)
    lane = lax.broadcasted_iota(I32, (1, LANES), 1)
    head_masks = (lane < GLA_DK, lane >= GLA_DK)

    fwd = [(qf_ref, kf_ref, laf_ref, vf_ref, of_ref, slice(g * c, (g + 1) * c), cum_f, lower, c - 1, c // 2 - 1)
           for g in range(GLA_GROUP)]
    bwd = [(qb_ref, kb_ref, lab_ref, vb_ref, ob_ref, slice(g * c, (g + 1) * c), cum_b, upper, 0, c // 2)
           for g in reversed(range(GLA_GROUP))]
    heads = range(GLA_HEADS)
    pair = [slice((h // 2) * LANES, (h // 2 + 1) * LANES) for h in heads]
    vcols = [slice(h * GLA_DV, (h + 1) * GLA_DV) for h in heads]

    def stage1(item):
        q_ref, k_ref, la_ref, v_ref, o_ref, rows, cum, tri, i_last, i_mid = item
        la = la_ref[rows, :]
        hi = la.astype(BF16)
        lo = (la - hi.astype(F32)).astype(BF16)
        b = jnp.dot(cum, hi, preferred_element_type=F32) + jnp.dot(cum, lo, preferred_element_type=F32)
        b_last = b[i_last:i_last + 1, :]
        b_mid = b[i_mid:i_mid + 1, :]
        q, k = q_ref[rows, :], k_ref[rows, :]
        return (q * jnp.exp(b - b_mid), (k * jnp.exp(b_mid - b)).astype(BF16), q * jnp.exp(b),
                (k * jnp.exp(b_last - b)).astype(BF16), jnp.exp(b_last))

    def stage2(item, pre):
        v_ref, rows, tri = item[3], item[5], item[7]
        qs, ks, qi, kst, decay = pre
        out = []
        for h in heads:
            mask = head_masks[h % 2]
            qs_h = jnp.where(mask, qs[:, pair[h]], 0.0).astype(BF16)
            sc = lax.dot_general(qs_h, ks[:, pair[h]], NT_DIMS, preferred_element_type=F32)
            v_h = v_ref[rows, vcols[h]]
            kv = lax.dot_general(v_h, kst[:, pair[h]], TN_DIMS, preferred_element_type=F32)
            out.append((jnp.where(tri, sc, 0.0).astype(BF16), kv,
                        jnp.where(mask, qi[:, pair[h]], 0.0).astype(BF16), v_h))
        return out

    def run(items, s_ref):
        pre = [stage1(it) for it in items]
        mid = [stage2(it, p) for it, p in zip(items, pre)]
        states = [s_ref[h] for h in heads]
        for it, p, m in zip(items, pre, mid):
            o_ref, rows, decay = it[4], it[5], p[4]
            for h in heads:
                sc, kv, qi_h, v_h = m[h]
                o = jnp.dot(sc, v_h, preferred_element_type=F32)
                o = o + lax.dot_general(qi_h, states[h].astype(BF16), NT_DIMS, preferred_element_type=F32)
                o_ref[rows, vcols[h]] = o
                states[h] = states[h] * decay[:, pair[h]] + kv
        for h in heads:
            s_ref[h] = states[h]

    run(fwd, sf_ref)
    run(bwd, sb_ref)


def _gla(q, k, v, laf, lab):
    t = q.shape[0]
    rows = GLA_GROUP * GLA_CHUNK
    ng = SEQ // rows
    hk = GLA_HEADS * GLA_DK
    hv = GLA_HEADS * GLA_DV

    def fwd(w):
        return pl.BlockSpec((rows, w), lambda b, n: (b * ng + n, 0))

    def bwd(w):
        return pl.BlockSpec((rows, w), lambda b, n: (b * ng + ng - 1 - n, 0))

    return pl.pallas_call(
        _gla_body,
        grid=(BATCH, ng),
        in_specs=[fwd(hk), fwd(hk), fwd(hv), fwd(hk), bwd(hk), bwd(hk), bwd(hv), bwd(hk)],
        out_specs=[fwd(hv), bwd(hv)],
        out_shape=[jax.ShapeDtypeStruct((t, hv), F32)] * 2,
        scratch_shapes=[pltpu.VMEM((GLA_HEADS, GLA_DV, 2 * GLA_DK), F32)] * 2,
        compiler_params=_params(("arbitrary", "arbitrary"), 32),
        name="gla",
    )(q, k, v, laf, q, k, v, lab)


def _attn_body(sink_ref, q_ref, kp_ref, kc_ref, kn_ref, vp_ref, vc_ref, vn_ref, o_ref):
    step = pl.program_id(1)
    last = pl.num_programs(1) - 1
    qb = ATT_BLOCK
    hd = ATT_HEAD_DIM
    k_all = jnp.concatenate([kp_ref[...], kc_ref[...], kn_ref[...]], axis=0)
    vt_all = jnp.concatenate([vp_ref[...], vc_ref[...], vn_ref[...]], axis=1)
    lane = lax.broadcasted_iota(I32, (1, LANES), 1)
    lo = lane < hd
    j_k = lax.broadcasted_iota(I32, (3 * qb, qb), 0)
    i_q = lax.broadcasted_iota(I32, (3 * qb, qb), 1)
    band = jnp.abs(j_k - qb - i_q) <= ATT_WINDOW
    sinks = [jnp.concatenate([jnp.full((1, qb), sink_ref[4 * g + r] * LOG2_E, F32) for r in range(4)], axis=1)
             for g in range(ATT_KV_HEADS)]
    ones_rows = jnp.ones((SUBLANES, 3 * qb), BF16)
    work = [(j, g) for j in range(ATT_GROUP) for g in range(ATT_KV_HEADS)]

    def scores(j, g):
        valid = band
        if j == 0:
            valid = valid & ((j_k >= qb) | (step > 0))
        if j == ATT_GROUP - 1:
            valid = valid & ((j_k < 2 * qb) | (step < last))
        valid4 = jnp.concatenate([valid] * 4, axis=1)
        rows = slice(j * qb, (j + 1) * qb)
        kg = k_all[j * qb:(j + 3) * qb, g * LANES:(g + 1) * LANES]
        qa = q_ref[rows, (2 * g) * LANES:(2 * g + 1) * LANES]
        qc = q_ref[rows, (2 * g + 1) * LANES:(2 * g + 2) * LANES]
        zero = jnp.zeros_like(qa)
        lhs = jnp.concatenate([jnp.where(lo, qa, zero), jnp.where(lo, zero, qa),
                               jnp.where(lo, qc, zero), jnp.where(lo, zero, qc)], axis=0)
        st = lax.dot_general(kg, lhs, NT_DIMS, preferred_element_type=F32)
        return jnp.concatenate([jnp.where(valid4[0:qb], st[0:qb], NEG_INF), st[qb:2 * qb],
                                jnp.where(valid4[2 * qb:3 * qb], st[2 * qb:3 * qb], NEG_INF)], axis=0)

    def softmax(st, g):
        sink = sinks[g]
        m = jnp.maximum(jnp.max(st, axis=0, keepdims=True), sink)
        return jnp.exp2(st - m).astype(BF16), jnp.exp2(sink - m)

    def output(j, g, p, p_sink):
        rows = slice(j * qb, (j + 1) * qb)
        vgt = vt_all[g * LANES:(g + 1) * LANES, j * qb:(j + 3) * qb]
        res = jnp.dot(jnp.concatenate([vgt, ones_rows], axis=0), p, preferred_element_type=F32)
        ot = res[0:LANES] * (1.0 / (res[LANES:LANES + 1] + p_sink))
        pair_a = jnp.concatenate([ot[0:hd, 0:qb], ot[hd:2 * hd, qb:2 * qb]], axis=0)
        pair_c = jnp.concatenate([ot[0:hd, 2 * qb:3 * qb], ot[hd:2 * hd, 3 * qb:4 * qb]], axis=0)
        o_ref[rows, (2 * g) * LANES:(2 * g + 1) * LANES] = pair_a.T.astype(o_ref.dtype)
        o_ref[rows, (2 * g + 1) * LANES:(2 * g + 2) * LANES] = pair_c.T.astype(o_ref.dtype)

    s_all = [scores(j, g) for j, g in work]
    p_all = [softmax(st, g) for st, (j, g) in zip(s_all, work)]
    for (j, g), (p, p_sink) in zip(work, p_all):
        output(j, g, p, p_sink)


def _attn(sink, aq, ak2, avt):
    t = aq.shape[0]
    qb = ATT_BLOCK
    nb = SEQ // qb
    steps = nb // ATT_GROUP
    nq = ATT_Q_HEADS * ATT_HEAD_DIM
    nk = 2 * ATT_KV_HEADS * ATT_HEAD_DIM

    def edge_block(b, n, shift):
        return b * nb + jnp.clip(n * ATT_GROUP + shift, 0, nb - 1)

    def k_edge(shift):
        return pl.BlockSpec((qb, nk), lambda b, n: (edge_block(b, n, shift), 0))

    def v_edge(shift):
        return pl.BlockSpec((nk, qb), lambda b, n: (0, edge_block(b, n, shift)))

    def group(w):
        return pl.BlockSpec((ATT_GROUP * qb, w), lambda b, n: (b * steps + n, 0))

    v_group = pl.BlockSpec((nk, ATT_GROUP * qb), lambda b, n: (0, b * steps + n))
    return pl.pallas_call(
        _attn_body,
        grid=(BATCH, steps),
        in_specs=[
            pl.BlockSpec(memory_space=pltpu.SMEM),
            group(nq),
            k_edge(-1), group(nk), k_edge(ATT_GROUP), v_edge(-1), v_group, v_edge(ATT_GROUP),
        ],
        out_specs=group(nq),
        out_shape=jax.ShapeDtypeStruct((t, nq), BF16),
        compiler_params=_params(("arbitrary", "arbitrary"), 48),
        name="attn",
    )(sink, aq, ak2, ak2, ak2, avt, avt, avt)


def _post_body(of_ref, ob_ref, gg_ref, oa_ref, x_ref, mod_ref, ggla_ref, gpm_ref, gpf_ref, wout_ref,
               wrh_ref, wrl_ref, br_ref,
               x1_ref, h2_ref, ti_ref, gt_ref, rk_ref, cnt_ref, base_ref):
    tm = TM_POST

    @pl.when(pl.program_id(0) == 0)
    def _():
        base_ref[...] = jnp.zeros_like(base_ref)

    gain1 = mod_ref[2:3, :] * gpm_ref[...]
    shift2 = mod_ref[3:4, :]
    gain2 = gpf_ref[...] * (1.0 + mod_ref[4:5, :])
    sub = tm // POST_SUB
    subs = [slice(s * sub, (s + 1) * sub) for s in range(POST_SUB)]

    def mixer_out(rows):
        og = of_ref[rows, :] + ob_ref[rows, :]
        gg = gg_ref[rows, :]
        parts = []
        for h in range(GLA_HEADS):
            cols = slice(h * GLA_DV, (h + 1) * GLA_DV)
            parts.append((_rms(og[:, cols], ggla_ref[...]) * _silu(gg[:, cols])).astype(BF16))
        return jnp.concatenate(parts + [oa_ref[rows, :]], axis=1)

    def ffn_in(s, y):
        rows = subs[s]
        x1 = x_ref[rows, :] + _rms(y, gain1)
        x1_ref[rows, :] = x1
        h2 = _rms(x1, gain2) + shift2
        hi = h2.astype(BF16)
        hi32 = hi.astype(F32)
        _pack_rows(h2_ref.at[pl.ds(s * sub * PACK_SUB, sub * PACK_SUB), :], hi32)
        return hi, (h2 - hi32).astype(BF16)

    o_subs = [mixer_out(rows) for rows in subs]
    y_subs = [jnp.dot(o, wout_ref[...], preferred_element_type=F32) for o in o_subs]
    split = [ffn_in(s, y) for s, y in enumerate(y_subs)]
    h2_hi = jnp.concatenate([hi for hi, _ in split], axis=0)
    h2_lo = jnp.concatenate([lo for _, lo in split], axis=0)

    wrh = wrh_ref[...]
    logits = (lax.dot_general(wrh, h2_hi, NT_DIMS, preferred_element_type=F32)
              + lax.dot_general(wrh, h2_lo, NT_DIMS, preferred_element_type=F32)
              + lax.dot_general(wrl_ref[...], h2_hi, NT_DIMS, preferred_element_type=F32)
              + br_ref[...])
    e_iota = lax.broadcasted_iota(I32, (N_EXPERTS, tm), 0)
    idxs, vals = [], []
    work = logits
    for _ in range(TOP_K):
        m = jnp.max(work, axis=0, keepdims=True)
        idx = jnp.min(jnp.where(work == m, e_iota, N_EXPERTS), axis=0, keepdims=True)
        idxs.append(idx)
        vals.append(m)
        work = jnp.where(e_iota == idx, -jnp.inf, work)
    exps = [jnp.exp(v - vals[0]) for v in vals]
    inv = 1.0 / (exps[0] + exps[1] + exps[2] + exps[3])
    gt_ref[...] = jnp.concatenate([e * inv for e in exps], axis=0)
    ti_ref[...] = jnp.concatenate(idxs, axis=0)

    onehots = [e_iota == idx for idx in idxs]
    member = jnp.where(onehots[0] | onehots[1] | onehots[2] | onehots[3], 1.0, 0.0)
    t_row = lax.broadcasted_iota(I32, (tm, tm), 0)
    t_col = lax.broadcasted_iota(I32, (tm, tm), 1)
    strict = jnp.where(t_row < t_col, 1.0, 0.0).astype(BF16)
    before = base_ref[...] + jnp.dot(member.astype(BF16), strict, preferred_element_type=F32)
    rk_ref[...] = jnp.concatenate(
        [jnp.sum(jnp.where(oh, before, 0.0), axis=0, keepdims=True) for oh in onehots], axis=0).astype(I32)
    new_base = base_ref[...] + jnp.sum(member, axis=1, keepdims=True)
    base_ref[...] = new_base
    cnt_ref[...] = jnp.broadcast_to(new_base, cnt_ref.shape)


def _post(o_f, o_b, gg, o_att, x2, mod, g_gla, g_pm, g_pf, wout, wrh, wrl, br):
    t, d = x2.shape
    tm = TM_POST
    tiles_per_seq = SEQ // tm
    hv = GLA_HEADS * GLA_DV

    def full(a):
        return pl.BlockSpec(a.shape, lambda i: (0,) * a.ndim)

    def rows(w):
        return pl.BlockSpec((tm, w), lambda i: (i, 0))

    def lanes():
        return pl.BlockSpec((TOP_K, tm), lambda i: (0, i))

    return pl.pallas_call(
        _post_body,
        grid=(t // tm,),
        in_specs=[
            rows(hv), rows(hv), rows(hv), rows(hv), rows(d),
            pl.BlockSpec((None, 6, d), lambda i: (i // tiles_per_seq, 0, 0)),
            full(g_gla), full(g_pm), full(g_pf), full(wout), full(wrh), full(wrl), full(br),
        ],
        out_specs=[
            rows(d),
            pl.BlockSpec((tm * PACK_SUB, LANES), lambda i: (i, 0)),
            lanes(), lanes(), lanes(),
            pl.BlockSpec((N_EXPERTS, LANES), lambda i: (0, 0)),
        ],
        out_shape=[
            jax.ShapeDtypeStruct((t, d), F32),
            jax.ShapeDtypeStruct((t * PACK_SUB, LANES), I32),
            jax.ShapeDtypeStruct((TOP_K, t), I32),
            jax.ShapeDtypeStruct((TOP_K, t), F32),
            jax.ShapeDtypeStruct((TOP_K, t), I32),
            jax.ShapeDtypeStruct((N_EXPERTS, LANES), F32),
        ],
        scratch_shapes=[pltpu.VMEM((N_EXPERTS, 1), F32)],
        compiler_params=_params(("arbitrary",), 48),
        name="post",
    )(o_f, o_b, gg, o_att, x2, mod, g_gla, g_pm, g_pf, wout, wrh, wrl, br)


def _route_body(ti_ref, rk_ref, cnt_ref, pos_ref, blk_ref):
    cnt = cnt_ref[...]
    padded = jnp.floor((cnt + (MOE_BM - 1)) * (1.0 / MOE_BM)) * MOE_BM
    starts, ends = [], []
    acc = jnp.zeros((1, LANES), F32)
    for e in range(N_EXPERTS):
        starts.append(acc)
        acc = acc + padded[e:e + 1, :]
        ends.append(acc)
    ti = ti_ref[...]
    off = jnp.zeros(ti.shape, F32)
    for e in range(N_EXPERTS):
        off = jnp.where(ti == e, starts[e][:, 0:1], off)
    pos_ref[...] = rk_ref[...] + off.astype(I32)

    def owner_of(row):
        n_le = jnp.zeros(row.shape, I32)
        for e in range(N_EXPERTS):
            n_le = n_le + jnp.where(ends[e][:, 0:1] <= row, 1, 0)
        return jnp.minimum(n_le, N_EXPERTS - 1)

    block_start = lax.broadcasted_iota(I32, (1, MOE_NB_PAD), 1).astype(F32) * MOE_BM
    owner = owner_of(block_start)
    nxt = jnp.zeros((1, MOE_NB_PAD), I32)
    nxt_blk = jnp.zeros((1, MOE_NB_PAD), I32)
    slot = jnp.zeros((1, MOE_NB_PAD), I32)
    ordinal = jnp.zeros((1, 1), I32)
    for e in range(N_EXPERTS):
        end_e = ends[e][:, 0:1]
        nxt = jnp.where(owner == e, jnp.where(end_e < acc[:, 0:1], owner_of(end_e), -1), nxt)
        nxt_blk = jnp.where(owner == e, (end_e * (1.0 / MOE_BM)).astype(I32), nxt_blk)
        slot = jnp.where(owner == e, ordinal & 1, slot)
        ordinal = ordinal + jnp.where(padded[e:e + 1, 0:1] > 0, 1, 0)
    used = jnp.broadcast_to((acc[:, 0:1] * (1.0 / MOE_BM)).astype(I32), (1, MOE_NB_PAD))
    blk_ref[...] = jnp.concatenate(
        [owner, nxt, used, nxt_blk, slot, jnp.zeros((SUBLANES - 5, MOE_NB_PAD), I32)], axis=0)


def _route(top_i, rank, counts):
    return pl.pallas_call(
        _route_body,
        out_shape=[
            jax.ShapeDtypeStruct(top_i.shape, I32),
            jax.ShapeDtypeStruct((SUBLANES, MOE_NB_PAD), I32),
        ],
        compiler_params=pltpu.CompilerParams(vmem_limit_bytes=32 * 1024 * 1024),
        name="route",
    )(top_i, rank, counts)


def _sc_workers():
    info = plsc.get_sparse_core_info()
    return info.num_cores, info.num_subcores, info.num_lanes


def _sc_gather_loop(table_hbm, out_hbm, idx_v, base, chunks, bufs, gather_sems, write_sems):
    window = SC_GATHER_WINDOW
    ring = len(bufs)

    def fetch(c, b):
        return pltpu.make_async_copy(table_hbm.at[idx_v.at[pl.ds(c * window, window)]], bufs[b], gather_sems[b])

    def flush(c, b):
        return pltpu.make_async_copy(bufs[b], out_hbm.at[pl.ds(base + c * window, window)], write_sems[b])

    for b in range(ring):
        fetch(b, b).start()

    @pl.loop(0, chunks, step=ring)
    def _(c0):
        for b in range(ring):
            c = c0 + b
            fetch(c, b).wait()
            flush(c, b).start()

            @pl.when(c + ring < chunks)
            def _():
                flush(c, b).wait()
                fetch(c + ring, b).start()

    for b in range(ring):
        flush(chunks - ring + b, b).wait()


def _sc_source_rows(pos_flat, n_rows):
    cores, subcores, lanes = _sc_workers()
    workers = cores * subcores
    per_worker = n_rows // workers
    n_assign = pos_flat.shape[0]
    scan = SC_SCAN_CHUNK
    assert per_worker * workers == n_rows and per_worker % lanes == 0
    assert n_assign % scan == 0 and scan % lanes == 0 and TOKENS & (TOKENS - 1) == 0
    mesh = plsc.VectorSubcoreMesh(core_axis_name="core", subcore_axis_name="subcore")

    @functools.partial(
        pl.kernel,
        out_type=jax.ShapeDtypeStruct((n_rows,), I32),
        mesh=mesh,
        scratch_types=[pltpu.VMEM((per_worker,), I32), pltpu.VMEM((scan,), I32)],
        compiler_params=pltpu.CompilerParams(needs_layout_passes=False),
        name="sc_source_rows",
    )
    def invert(pos_hbm, out_hbm, src_v, pos_v):
        wid = lax.axis_index("subcore") * cores + lax.axis_index("core")
        base = wid * per_worker
        lane = lax.iota(I32, lanes)

        @pl.loop(0, per_worker, step=lanes)
        def _(j):
            src_v[pl.ds(j, lanes)] = (base + j + lane) & (TOKENS - 1)

        @pl.loop(0, n_assign, step=scan)
        def _(a0):
            pltpu.sync_copy(pos_hbm.at[pl.ds(a0, scan)], pos_v)

            @plsc.parallel_loop(0, scan, step=lanes, unroll=SC_SCAN_UNROLL)
            def _(j):
                rel = pos_v[pl.ds(j, lanes)] - base
                mine = (rel >= 0) & (rel < per_worker)
                tok = (a0 + j + lane) & (TOKENS - 1)
                plsc.store_scatter(src_v, [jnp.where(mine, rel, 0)], tok, mask=mine)

        pltpu.sync_copy(src_v, out_hbm.at[pl.ds(base, per_worker)])

    return invert(pos_flat)


def _sc_gather_rows(table, idx, ring=SC_GATHER_RING):
    cores, subcores, _ = _sc_workers()
    workers = cores * subcores
    n = idx.shape[0]
    window = SC_GATHER_WINDOW
    per_worker = n // workers
    chunks = per_worker // window
    assert per_worker * workers == n and chunks * window == per_worker and chunks % ring == 0
    row_shape = table.shape[1:]
    mesh = plsc.VectorSubcoreMesh(core_axis_name="core", subcore_axis_name="subcore")

    @functools.partial(
        pl.kernel,
        out_type=jax.ShapeDtypeStruct((n,) + row_shape, table.dtype),
        mesh=mesh,
        scratch_types=[pltpu.VMEM((per_worker,), I32)]
        + [pltpu.VMEM((window,) + row_shape, table.dtype)] * ring
        + [pltpu.SemaphoreType.DMA] * (2 * ring),
        name="sc_gather_rows",
    )
    def gather(table_hbm, idx_hbm, out_hbm, idx_v, *scratch):
        wid = lax.axis_index("subcore") * cores + lax.axis_index("core")
        base = wid * per_worker
        pltpu.sync_copy(idx_hbm.at[pl.ds(base, per_worker)], idx_v)
        _sc_gather_loop(table_hbm, out_hbm, idx_v, base, chunks,
                        scratch[:ring], scratch[ring:2 * ring], scratch[2 * ring:])

    return gather(table, idx)


def _experts_body(first, be_ref, nxt_ref, nxtblk_ref, slot_ref, nu_ref, xs_ref, bgu_ref, bd_ref, wgu_hbm, wd_hbm,
                  *rest):
    ys_ref, wgu_f32, wd_f32, sems = rest[-4:]
    step = pl.program_id(0)
    end = first + pl.num_programs(0) * MOE_SUB
    bm = MOE_BM
    n_used = nu_ref[0]

    def fetch(e, slot):
        return (pltpu.make_async_copy(wgu_hbm.at[e], wgu_f32.at[slot], sems.at[0, slot]),
                pltpu.make_async_copy(wd_hbm.at[e], wd_f32.at[slot], sems.at[1, slot]))

    @pl.when((step == 0) & (first < n_used))
    def _():
        for cp in fetch(be_ref[first], slot_ref[first]):
            cp.start()

    for sub in range(MOE_SUB):
        i = first + step * MOE_SUB + sub
        e = be_ref[i]
        slot = slot_ref[i]
        used = i < n_used
        fresh = (i == first) | (e != be_ref[jnp.maximum(i - 1, 0)])
        rows = pl.ds(sub * bm * PACK_SUB, bm * PACK_SUB)
        xs_sub = xs_ref.at[rows, :]
        ys_sub = ys_ref.at[rows, :]

        @pl.when(used & fresh)
        def _():
            for cp in fetch(e, slot):
                cp.wait()

            @pl.when((nxt_ref[i] >= 0) & (nxtblk_ref[i] < end))
            def _():
                for cp in fetch(nxt_ref[i], 1 - slot):
                    cp.start()

        @pl.when(used)
        def _():
            x = jnp.concatenate(_unpack_rows(xs_sub, bm), axis=1).astype(BF16)
            gu = jnp.dot(x, wgu_f32[slot].astype(BF16), preferred_element_type=F32) + bgu_ref[pl.ds(e, 1), :]
            gate = jnp.minimum(gu[:, 0:D_FF], SWIGLU_LIMIT)
            up = jnp.clip(gu[:, D_FF:2 * D_FF], -SWIGLU_LIMIT, SWIGLU_LIMIT)
            act = ((up + 1.0) * gate * jax.nn.sigmoid(SWIGLU_ALPHA * gate)).astype(BF16)
            y = jnp.dot(act, wd_f32[slot].astype(BF16), preferred_element_type=F32) + bd_ref[pl.ds(e, 1), :]
            _pack_rows(ys_sub, y.astype(BF16).astype(F32))

        @pl.when(jnp.logical_not(used))
        def _():
            ys_sub[...] = jnp.zeros((bm * PACK_SUB, LANES), I32)


def _experts(first, part_blocks, blocks, xs_part, w_gate_up, b_gate_up, w_down, b_down, ys_prev):
    rows = MOE_SUB * MOE_BM * PACK_SUB
    d = D_MODEL
    steps = part_blocks // MOE_SUB
    assert steps * MOE_SUB == part_blocks and first % MOE_SUB == 0

    def x_block(i, be, nx, nb, sl, nu):
        last = jnp.maximum((jnp.minimum(nu[0], first + part_blocks) - 1 - first) // MOE_SUB, 0)
        return jnp.minimum(i, last)

    in_specs = [
        pl.BlockSpec((rows, LANES), lambda i, be, nx, nb, sl, nu: (x_block(i, be, nx, nb, sl, nu), 0)),
        pl.BlockSpec(b_gate_up.shape, lambda i, be, nx, nb, sl, nu: (0, 0)),
        pl.BlockSpec(b_down.shape, lambda i, be, nx, nb, sl, nu: (0, 0)),
        pl.BlockSpec(memory_space=pl.ANY),
        pl.BlockSpec(memory_space=pl.ANY),
    ]
    operands = [blocks[0, :MOE_NB], blocks[1, :MOE_NB], blocks[3, :MOE_NB], blocks[4, :MOE_NB], blocks[2, :1],
                xs_part, b_gate_up, b_down, w_gate_up, w_down]
    aliases = {}
    if ys_prev is not None:
        in_specs.append(pl.BlockSpec(memory_space=pl.ANY))
        aliases = {len(operands): 0}
        operands.append(ys_prev)
    grid_spec = pltpu.PrefetchScalarGridSpec(
        num_scalar_prefetch=5,
        grid=(steps,),
        in_specs=in_specs,
        out_specs=pl.BlockSpec((rows, LANES), lambda i, be, nx, nb, sl, nu: (first // MOE_SUB + i, 0)),
        scratch_shapes=[
            pltpu.VMEM((2, d, 2 * D_FF), F32), pltpu.VMEM((2, D_FF, d), F32),
            pltpu.SemaphoreType.DMA((2, 2)),
        ],
    )
    return pl.pallas_call(
        functools.partial(_experts_body, first),
        grid_spec=grid_spec,
        out_shape=jax.ShapeDtypeStruct((MOE_ROWS * PACK_SUB, LANES), I32),
        input_output_aliases=aliases,
        compiler_params=_params(("arbitrary",), 48),
        name="experts",
    )(*operands)


def _combine_body(gates_ref, x1_ref, mod_ref, gpost_ref, y0_ref, y1_ref, y2_ref, y3_ref, o_ref):
    tm = TM_COMB
    gates = jnp.concatenate([gates_ref[...], jnp.zeros((SUBLANES - TOP_K, tm), F32)], axis=0).T
    y_hi = jnp.zeros((tm, PACK_COLS), F32)
    y_lo = jnp.zeros((tm, PACK_COLS), F32)
    for k, yk_ref in enumerate((y0_ref, y1_ref, y2_ref, y3_ref)):
        hi, lo = _unpack_rows(yk_ref, tm)
        y_hi = y_hi + hi * gates[:, k:k + 1]
        y_lo = y_lo + lo * gates[:, k:k + 1]
    y = jnp.concatenate([y_hi, y_lo], axis=1)
    gain = mod_ref[5:6, :] * gpost_ref[...]
    o_ref[...] = x1_ref[...] + _rms(y, gain)


def _combine(gates, x1, mod, g_post, y4):
    t, d = x1.shape
    tm = TM_COMB
    tiles = t // tm
    tiles_per_seq = SEQ // tm

    def slab(k):
        return pl.BlockSpec((tm * PACK_SUB, LANES), lambda i: (k * tiles + i, 0))

    return pl.pallas_call(
        _combine_body,
        grid=(tiles,),
        in_specs=[
            pl.BlockSpec((TOP_K, tm), lambda i: (0, i)),
            pl.BlockSpec((tm, d), lambda i: (i, 0)),
            pl.BlockSpec((None, 6, d), lambda i: (i // tiles_per_seq, 0, 0)),
            pl.BlockSpec(g_post.shape, lambda i: (0, 0)),
            slab(0), slab(1), slab(2), slab(3),
        ],
        out_specs=pl.BlockSpec((tm, d), lambda i: (i, 0)),
        out_shape=jax.ShapeDtypeStruct((t, d), F32),
        compiler_params=_params(("arbitrary",), 48),
        name="combine",
    )(gates, x1, mod, g_post, y4, y4, y4, y4)


def _rotary_tables():
    half = ROT_DIM // 2
    inv_freq = ROPE_THETA ** (-2.0 * np.arange(half, dtype=np.float32) / ROT_DIM)
    ang = np.arange(SEQ, dtype=np.float32)[:, None] * inv_freq[None, :].astype(np.float32)
    cos, sin = np.cos(ang), np.sin(ang)
    ones = np.ones((SEQ, ATT_HEAD_DIM - ROT_DIM), np.float32)
    zeros = np.zeros((SEQ, ATT_HEAD_DIM - ROT_DIM), np.float32)
    zh = np.zeros((SEQ, half), np.float32)
    reps = LANES // ATT_HEAD_DIM
    rc = np.tile(np.concatenate([cos, cos, ones], axis=1), (1, reps))
    rm = np.tile(np.concatenate([-sin, zh, zeros], axis=1), (1, reps))
    rp = np.tile(np.concatenate([zh, sin, zeros], axis=1), (1, reps))
    return tuple(jnp.asarray(t, F32) for t in (rc, rm, rp))


def _mixer_inputs(w_in, w_gk_fwd, b_gk_fwd, w_gk_bwd, b_gk_bwd):
    hk = GLA_HEADS * GLA_DK
    hv = GLA_HEADS * GLA_DV
    w = w_in[0]
    o_lr = 2 * hk + 2 * hv
    o_aq = o_lr + 2 * GLA_RANK
    o_ak = o_aq + ATT_Q_HEADS * ATT_HEAD_DIM
    o_av = o_ak + ATT_KV_HEADS * ATT_HEAD_DIM
    hd = ATT_HEAD_DIM
    wa = w[:, :o_lr].astype(BF16)
    wlr = w[:, o_lr:o_aq].astype(BF16)
    dup = lambda m: jnp.concatenate([m[:, g * hd:(g + 1) * hd] for g in range(ATT_KV_HEADS) for _ in range(2)], axis=1)
    wb = jnp.concatenate([w[:, o_aq:o_ak], dup(w[:, o_ak:o_av])], axis=1).astype(BF16)
    wvt = dup(w[:, o_av:o_av + ATT_KV_HEADS * hd]).T.astype(BF16)
    zr = jnp.zeros((GLA_RANK, hk), F32)
    wgk = jnp.concatenate([jnp.concatenate([w_gk_fwd[0], zr], axis=1),
                           jnp.concatenate([zr, w_gk_bwd[0]], axis=1)], axis=0).astype(BF16)
    bgk = jnp.concatenate([b_gk_fwd[0], b_gk_bwd[0]])[None, :]
    return (wa, wlr, wgk, bgk, wb, wvt) + _rotary_tables()


def kernel(x, c, w_ada, b_ada, g_pre_mix, g_post_mix, w_in, w_gk_fwd, b_gk_fwd, w_gk_bwd, b_gk_bwd, g_gla_out,
           attn_sink, w_out, g_pre_ffn, g_post_ffn, w_router, b_router, w_gate_up, b_gate_up, w_down, b_down):
    assert x.shape == (BATCH, SEQ, D_MODEL) and w_ada.shape[0] == 1
    d = D_MODEL
    x2 = x.reshape(TOKENS, d)

    c_pad = jnp.pad(c, ((0, SUBLANES - BATCH), (0, 0)))
    mod = _ada(c_pad, w_ada[0], b_ada)[:BATCH].reshape(BATCH, 6, d)

    mixer_in = _mixer_inputs(w_in, w_gk_fwd, b_gk_fwd, w_gk_bwd, b_gk_bwd)
    q, k, v, gg, laf, lab, aq, ak2, avt = _inproj(x2, mod, g_pre_mix, *mixer_in)
    o_f, o_b = _gla(q, k, v, laf, lab)
    o_att = _attn(attn_sink[0], aq, ak2, avt)

    wr_t = w_router[0].T
    wrh = wr_t.astype(BF16)
    wrl = (wr_t - wrh.astype(F32)).astype(BF16)
    x1, h2_tiles, top_i, gates, rank, counts = _post(
        o_f, o_b, gg, o_att, x2, mod, g_gla_out, g_post_mix, g_pre_ffn, w_out[0].astype(BF16), wrh, wrl,
        b_router[0][:, None])

    pos, blocks = _route(top_i, rank, counts)

    src = _sc_source_rows(pos.reshape(TOP_K * TOKENS), MOE_ROWS)
    h2_rows = h2_tiles.reshape(TOKENS, PACK_SUB, LANES)
    ys = None
    first = 0
    for part_blocks, ring in zip(MOE_PART_BLOCKS, MOE_PART_RINGS):
        row0, n_rows = first * MOE_BM, part_blocks * MOE_BM
        xs_p = _sc_gather_rows(h2_rows, src[row0:row0 + n_rows], ring)
        ys = _experts(first, part_blocks, blocks, xs_p.reshape(n_rows * PACK_SUB, LANES),
                      w_gate_up[0], b_gate_up[0], w_down[0], b_down[0], ys)
        first += part_blocks
    assert first == MOE_NB

    y4 = _sc_gather_rows(ys.reshape(MOE_ROWS, PACK_SUB, LANES), pos.reshape(TOP_K * TOKENS))
    out = _combine(gates, x1, mod, g_post_ffn, y4.reshape(TOP_K * TOKENS * PACK_SUB, LANES))
    return out.reshape(BATCH, SEQ, d)
```

```python
import functools

import jax
import jax.numpy as jnp
import numpy as np
from jax import lax
from jax.experimental import pallas as pl
from jax.experimental.pallas import tpu as pltpu
from jax.experimental.pallas import tpu_sc as plsc

F32 = jnp.float32
BF16 = jnp.bfloat16
I32 = jnp.int32

D_MODEL = 1024
BATCH = 2
SEQ = 8192
TOKENS = BATCH * SEQ
GLA_HEADS = 4
GLA_DV = 128
GLA_DK = 64
GLA_RANK = 16
GLA_GATE_NORMALIZER = 16.0
GLA_CHUNK = 64
ATT_Q_HEADS = 8
ATT_KV_HEADS = 2
ATT_HEAD_DIM = 64
ATT_WINDOW = 128
ATT_BLOCK = 128
ROT_DIM = 16
ROPE_THETA = 500000.0
N_EXPERTS = 32
TOP_K = 4
D_FF = 1024
SWIGLU_LIMIT = 7.0
SWIGLU_ALPHA = 1.702
NORM_EPS = 1e-6
NEG_INF = -1e30
LOG2_E = 1.4426950408889634

LANES = 128
SUBLANES = 8
PACK_COLS = D_MODEL // 2
PACK_SUB = PACK_COLS // LANES

TM_IN = 1024
IN_SUB = 4
GLA_GROUP = 16
ATT_GROUP = 16
TM_POST = 1024
POST_SUB = 4
MOE_BM = 256
MOE_ROWS = TOKENS * TOP_K + N_EXPERTS * MOE_BM
MOE_NB = MOE_ROWS // MOE_BM
MOE_SUB = 4
MOE_PART_BLOCKS = (48, 112, 128)
MOE_NB_PAD = ((MOE_NB + LANES - 1) // LANES) * LANES
SC_SCAN_CHUNK = 4096
SC_SCAN_UNROLL = 8
TM_COMB = 1024
SC_GATHER_WINDOW = 32
SC_GATHER_RING = 4
MOE_PART_RINGS = (SC_GATHER_RING, 1, 1)

NT_DIMS = (((1,), (1,)), ((), ()))
TN_DIMS = (((0,), (0,)), ((), ()))


def _params(semantics, vmem_mib):
    return pltpu.CompilerParams(dimension_semantics=semantics, vmem_limit_bytes=vmem_mib * 1024 * 1024)


def _rms(x, g):
    return x * lax.rsqrt(jnp.mean(x * x, axis=-1, keepdims=True) + NORM_EPS) * g


def _silu(x):
    return x * jax.nn.sigmoid(x)


def _pack_rows(ref, v):
    m = v.shape[0]
    bits = lax.bitcast_convert_type(v, jnp.uint32)
    word = lax.bitcast_convert_type(bits[:, :PACK_COLS] | (bits[:, PACK_COLS:] >> 16), I32)
    for s in range(PACK_SUB):
        ref[pl.ds(s, m, stride=PACK_SUB), :] = word[:, s * LANES:(s + 1) * LANES]


def _unpack_rows(ref, m):
    word = jnp.concatenate([ref[pl.ds(s, m, stride=PACK_SUB), :] for s in range(PACK_SUB)], axis=1)
    bits = lax.bitcast_convert_type(word, jnp.uint32)
    hi = lax.bitcast_convert_type(bits & jnp.uint32(0xFFFF0000), F32)
    lo = lax.bitcast_convert_type(bits << 16, F32)
    return hi, lo


def _ada_body(c_ref, w_ref, b_ref, o_ref):
    ca = _silu(c_ref[...]).astype(BF16)
    o_ref[...] = jnp.dot(ca, w_ref[...].astype(BF16), preferred_element_type=F32) + b_ref[...]


def _ada(c_pad, w_ada, b_ada):
    d = D_MODEL
    return pl.pallas_call(
        _ada_body,
        grid=(6,),
        in_specs=[
            pl.BlockSpec((SUBLANES, d), lambda j: (0, 0)),
            pl.BlockSpec((d, d), lambda j: (0, j)),
            pl.BlockSpec((1, d), lambda j: (0, j)),
        ],
        out_specs=pl.BlockSpec((SUBLANES, d), lambda j: (0, j)),
        out_shape=jax.ShapeDtypeStruct((SUBLANES, 6 * d), F32),
        compiler_params=_params(("arbitrary",), 32),
        name="ada",
    )(c_pad, w_ada, b_ada)


def _rotary(x, cos_t, msin_t, psin_t):
    width = x.shape[1]
    reps = width // LANES
    c = jnp.concatenate([cos_t] * reps, axis=1)
    m = jnp.concatenate([msin_t] * reps, axis=1)
    p = jnp.concatenate([psin_t] * reps, axis=1)
    half = ROT_DIM // 2
    return x * c + pltpu.roll(x, width - half, 1) * m + pltpu.roll(x, half, 1) * p


def _inproj_body(x_ref, mod_ref, g_ref, wa_ref, wlr_ref, wgk_ref, bgk_ref, wb_ref, wvt_ref, rc_ref, rm_ref, rp_ref,
                 q_ref, k_ref, v_ref, gg_ref, laf_ref, lab_ref, aq_ref, ak_ref, avt_ref):
    shift = mod_ref[0:1, :]
    scale = mod_ref[1:2, :]
    hk = GLA_HEADS * GLA_DK
    hv = GLA_HEADS * GLA_DV
    nq = ATT_Q_HEADS * ATT_HEAD_DIM
    nk = 2 * ATT_KV_HEADS * ATT_HEAD_DIM
    sub = x_ref.shape[0] // IN_SUB
    subs = [slice(s * sub, (s + 1) * sub) for s in range(IN_SUB)]

    gain = g_ref[...] * (1.0 + scale)

    def hidden(rows):
        return (_rms(x_ref[rows, :], gain) + shift).astype(BF16)

    def project(h):
        return (jnp.dot(h, wa_ref[...], preferred_element_type=F32),
                jnp.dot(h, wlr_ref[...], preferred_element_type=F32),
                jnp.dot(h, wb_ref[...], preferred_element_type=F32),
                lax.dot_general(wvt_ref[...], h, NT_DIMS, preferred_element_type=F32))

    def finish(rows, pa, plr, pb, pvt):
        q_ref[rows, :] = pa[:, 0:hk] * (GLA_DK ** -0.5)
        k_ref[rows, :] = pa[:, hk:2 * hk]
        v_ref[rows, :] = pa[:, 2 * hk:2 * hk + hv].astype(BF16)
        gg_ref[rows, :] = pa[:, 2 * hk + hv:2 * hk + 2 * hv]
        gk = jnp.dot(plr.astype(BF16), wgk_ref[...], preferred_element_type=F32) + bgk_ref[...]
        la = (jnp.minimum(gk, 0.0) - jnp.log1p(jnp.exp(-jnp.abs(gk)))) * (1.0 / GLA_GATE_NORMALIZER)
        laf_ref[rows, :] = la[:, 0:hk]
        lab_ref[rows, :] = la[:, hk:2 * hk]
        rc, rm, rp = rc_ref[rows, :], rm_ref[rows, :], rp_ref[rows, :]
        aq_ref[rows, :] = (_rotary(pb[:, 0:nq], rc, rm, rp) * (ATT_HEAD_DIM ** -0.5 * LOG2_E)).astype(BF16)
        ak_ref[rows, :] = _rotary(pb[:, nq:nq + nk], rc, rm, rp).astype(BF16)
        avt_ref[:, rows] = pvt.astype(BF16)

    hs = [hidden(rows) for rows in subs]
    ps = [project(h) for h in hs]
    for rows, p in zip(subs, ps):
        finish(rows, *p)


def _inproj(x2, mod, g_pre, wa, wlr, wgk, bgk, wb, wvt, rc, rm, rp):
    t, d = x2.shape
    tm = TM_IN
    tiles_per_seq = SEQ // tm
    hk = GLA_HEADS * GLA_DK
    hv = GLA_HEADS * GLA_DV
    nq = ATT_Q_HEADS * ATT_HEAD_DIM
    nk = 2 * ATT_KV_HEADS * ATT_HEAD_DIM

    def full(a):
        return pl.BlockSpec(a.shape, lambda i: (0,) * a.ndim)

    def rows(w):
        return pl.BlockSpec((tm, w), lambda i: (i, 0))

    def table():
        return pl.BlockSpec((tm, LANES), lambda i: (i % tiles_per_seq, 0))

    out_widths = [(hk, F32), (hk, F32), (hv, BF16), (hv, F32), (hk, F32), (hk, F32), (nq, BF16), (nk, BF16)]
    return pl.pallas_call(
        _inproj_body,
        grid=(t // tm,),
        in_specs=[
            rows(d),
            pl.BlockSpec((None, 6, d), lambda i: (i // tiles_per_seq, 0, 0)),
            full(g_pre), full(wa), full(wlr), full(wgk), full(bgk), full(wb), full(wvt),
            table(), table(), table(),
        ],
        out_specs=[rows(w) for w, _ in out_widths] + [pl.BlockSpec((nk, tm), lambda i: (0, i))],
        out_shape=[jax.ShapeDtypeStruct((t, w), dt) for w, dt in out_widths] + [jax.ShapeDtypeStruct((nk, t), BF16)],
        compiler_params=_params(("arbitrary",), 56),
        name="inproj",
    )(x2, mod, g_pre, wa, wlr, wgk, bgk, wb, wvt, rc, rm, rp)


def _gla_body(qf_ref, kf_ref, vf_ref, laf_ref, qb_ref, kb_ref, vb_ref, lab_ref, of_ref, ob_ref, sf_ref, sb_ref):
    @pl.when(pl.program_id(1) == 0)
    def _():
        sf_ref[...] = jnp.zeros_like(sf_ref)
        sb_ref[...] = jnp.zeros_like(sb_ref)

    c = GLA_CHUNK
    r_i = lax.broadcasted_iota(I32, (c, c), 0)
    c_i = lax.broadcasted_iota(I32, (c, c), 1)
    lower = c_i <= r_i
    upper = c_i >= r_i
    cum_f = jnp.where(lower, 1.0, 0.0).astype(BF16)
    cum_b = jnp.where(upper, 1.0, 0.0).astype(BF16)
    lane = lax.broadcasted_iota(I32, (1, LANES), 1)
    head_masks = (lane < GLA_DK, lane >= GLA_DK)

    fwd = [(qf_ref, kf_ref, laf_ref, vf_ref, of_ref, slice(g * c, (g + 1) * c), cum_f, lower, c - 1, c // 2 - 1)
           for g in range(GLA_GROUP)]
    bwd = [(qb_ref, kb_ref, lab_ref, vb_ref, ob_ref, slice(g * c, (g + 1) * c), cum_b, upper, 0, c // 2)
           for g in reversed(range(GLA_GROUP))]
    heads = range(GLA_HEADS)
    pair = [slice((h // 2) * LANES, (h // 2 + 1) * LANES) for h in heads]
    vcols = [slice(h * GLA_DV, (h + 1) * GLA_DV) for h in heads]

    def stage1(item):
        q_ref, k_ref, la_ref, v_ref, o_ref, rows, cum, tri, i_last, i_mid = item
        la = la_ref[rows, :]
        hi = la.astype(BF16)
        lo = (la - hi.astype(F32)).astype(BF16)
        b = jnp.dot(cum, hi, preferred_element_type=F32) + jnp.dot(cum, lo, preferred_element_type=F32)
        b_last = b[i_last:i_last + 1, :]
        b_mid = b[i_mid:i_mid + 1, :]
        q, k = q_ref[rows, :], k_ref[rows, :]
        return (q * jnp.exp(b - b_mid), (k * jnp.exp(b_mid - b)).astype(BF16), q * jnp.exp(b),
                (k * jnp.exp(b_last - b)).astype(BF16), jnp.exp(b_last))

    def stage2(item, pre):
        v_ref, rows, tri = item[3], item[5], item[7]
        qs, ks, qi, kst, decay = pre
        out = []
        for h in heads:
            mask = head_masks[h % 2]
            qs_h = jnp.where(mask, qs[:, pair[h]], 0.0).astype(BF16)
            sc = lax.dot_general(qs_h, ks[:, pair[h]], NT_DIMS, preferred_element_type=F32)
            v_h = v_ref[rows, vcols[h]]
            kv = lax.dot_general(v_h, kst[:, pair[h]], TN_DIMS, preferred_element_type=F32)
            out.append((jnp.where(tri, sc, 0.0).astype(BF16), kv,
                        jnp.where(mask, qi[:, pair[h]], 0.0).astype(BF16), v_h))
        return out

    def run(items, s_ref):
        pre = [stage1(it) for it in items]
        mid = [stage2(it, p) for it, p in zip(items, pre)]
        states = [s_ref[h] for h in heads]
        for it, p, m in zip(items, pre, mid):
            o_ref, rows, decay = it[4], it[5], p[4]
            for h in heads:
                sc, kv, qi_h, v_h = m[h]
                o = jnp.dot(sc, v_h, preferred_element_type=F32)
                o = o + lax.dot_general(qi_h, states[h].astype(BF16), NT_DIMS, preferred_element_type=F32)
                o_ref[rows, vcols[h]] = o
                states[h] = states[h] * decay[:, pair[h]] + kv
        for h in heads:
            s_ref[h] = states[h]

    run(fwd, sf_ref)
    run(bwd, sb_ref)


def _gla(q, k, v, laf, lab):
    t = q.shape[0]
    rows = GLA_GROUP * GLA_CHUNK
    ng = SEQ // rows
    hk = GLA_HEADS * GLA_DK
    hv = GLA_HEADS * GLA_DV

    def fwd(w):
        return pl.BlockSpec((rows, w), lambda b, n: (b * ng + n, 0))

    def bwd(w):
        return pl.BlockSpec((rows, w), lambda b, n: (b * ng + ng - 1 - n, 0))

    return pl.pallas_call(
        _gla_body,
        grid=(BATCH, ng),
        in_specs=[fwd(hk), fwd(hk), fwd(hv), fwd(hk), bwd(hk), bwd(hk), bwd(hv), bwd(hk)],
        out_specs=[fwd(hv), bwd(hv)],
        out_shape=[jax.ShapeDtypeStruct((t, hv), F32)] * 2,
        scratch_shapes=[pltpu.VMEM((GLA_HEADS, GLA_DV, 2 * GLA_DK), F32)] * 2,
        compiler_params=_params(("arbitrary", "arbitrary"), 32),
        name="gla",
    )(q, k, v, laf, q, k, v, lab)


def _attn_body(sink_ref, q_ref, kp_ref, kc_ref, kn_ref, vp_ref, vc_ref, vn_ref, o_ref):
    step = pl.program_id(1)
    last = pl.num_programs(1) - 1
    qb = ATT_BLOCK
    hd = ATT_HEAD_DIM
    k_all = jnp.concatenate([kp_ref[...], kc_ref[...], kn_ref[...]], axis=0)
    vt_all = jnp.concatenate([vp_ref[...], vc_ref[...], vn_ref[...]], axis=1)
    lane = lax.broadcasted_iota(I32, (1, LANES), 1)
    lo = lane < hd
    j_k = lax.broadcasted_iota(I32, (3 * qb, qb), 0)
    i_q = lax.broadcasted_iota(I32, (3 * qb, qb), 1)
    band = jnp.abs(j_k - qb - i_q) <= ATT_WINDOW
    sinks = [jnp.concatenate([jnp.full((1, qb), sink_ref[4 * g + r] * LOG2_E, F32) for r in range(4)], axis=1)
             for g in range(ATT_KV_HEADS)]
    ones_rows = jnp.ones((SUBLANES, 3 * qb), BF16)
    work = [(j, g) for j in range(ATT_GROUP) for g in range(ATT_KV_HEADS)]

    def scores(j, g):
        valid = band
        if j == 0:
            valid = valid & ((j_k >= qb) | (step > 0))
        if j == ATT_GROUP - 1:
            valid = valid & ((j_k < 2 * qb) | (step < last))
        valid4 = jnp.concatenate([valid] * 4, axis=1)
        rows = slice(j * qb, (j + 1) * qb)
        kg = k_all[j * qb:(j + 3) * qb, g * LANES:(g + 1) * LANES]
        qa = q_ref[rows, (2 * g) * LANES:(2 * g + 1) * LANES]
        qc = q_ref[rows, (2 * g + 1) * LANES:(2 * g + 2) * LANES]
        zero = jnp.zeros_like(qa)
        lhs = jnp.concatenate([jnp.where(lo, qa, zero), jnp.where(lo, zero, qa),
                               jnp.where(lo, qc, zero), jnp.where(lo, zero, qc)], axis=0)
        st = lax.dot_general(kg, lhs, NT_DIMS, preferred_element_type=F32)
        return jnp.concatenate([jnp.where(valid4[0:qb], st[0:qb], NEG_INF), st[qb:2 * qb],
                                jnp.where(valid4[2 * qb:3 * qb], st[2 * qb:3 * qb], NEG_INF)], axis=0)

    def softmax(st, g):
        sink = sinks[g]
        m = jnp.maximum(jnp.max(st, axis=0, keepdims=True), sink)
        return jnp.exp2(st - m).astype(BF16), jnp.exp2(sink - m)

    def output(j, g, p, p_sink):
        rows = slice(j * qb, (j + 1) * qb)
        vgt = vt_all[g * LANES:(g + 1) * LANES, j * qb:(j + 3) * qb]
        res = jnp.dot(jnp.concatenate([vgt, ones_rows], axis=0), p, preferred_element_type=F32)
        ot = res[0:LANES] * (1.0 / (res[LANES:LANES + 1] + p_sink))
        pair_a = jnp.concatenate([ot[0:hd, 0:qb], ot[hd:2 * hd, qb:2 * qb]], axis=0)
        pair_c = jnp.concatenate([ot[0:hd, 2 * qb:3 * qb], ot[hd:2 * hd, 3 * qb:4 * qb]], axis=0)
        o_ref[rows, (2 * g) * LANES:(2 * g + 1) * LANES] = pair_a.T.astype(o_ref.dtype)
        o_ref[rows, (2 * g + 1) * LANES:(2 * g + 2) * LANES] = pair_c.T.astype(o_ref.dtype)

    s_all = [scores(j, g) for j, g in work]
    p_all = [softmax(st, g) for st, (j, g) in zip(s_all, work)]
    for (j, g), (p, p_sink) in zip(work, p_all):
        output(j, g, p, p_sink)


def _attn(sink, aq, ak2, avt):
    t = aq.shape[0]
    qb = ATT_BLOCK
    nb = SEQ // qb
    steps = nb // ATT_GROUP
    nq = ATT_Q_HEADS * ATT_HEAD_DIM
    nk = 2 * ATT_KV_HEADS * ATT_HEAD_DIM

    def edge_block(b, n, shift):
        return b * nb + jnp.clip(n * ATT_GROUP + shift, 0, nb - 1)

    def k_edge(shift):
        return pl.BlockSpec((qb, nk), lambda b, n: (edge_block(b, n, shift), 0))

    def v_edge(shift):
        return pl.BlockSpec((nk, qb), lambda b, n: (0, edge_block(b, n, shift)))

    def group(w):
        return pl.BlockSpec((ATT_GROUP * qb, w), lambda b, n: (b * steps + n, 0))

    v_group = pl.BlockSpec((nk, ATT_GROUP * qb), lambda b, n: (0, b * steps + n))
    return pl.pallas_call(
        _attn_body,
        grid=(BATCH, steps),
        in_specs=[
            pl.BlockSpec(memory_space=pltpu.SMEM),
            group(nq),
            k_edge(-1), group(nk), k_edge(ATT_GROUP), v_edge(-1), v_group, v_edge(ATT_GROUP),
        ],
        out_specs=group(nq),
        out_shape=jax.ShapeDtypeStruct((t, nq), BF16),
        compiler_params=_params(("arbitrary", "arbitrary"), 48),
        name="attn",
    )(sink, aq, ak2, ak2, ak2, avt, avt, avt)


def _post_body(of_ref, ob_ref, gg_ref, oa_ref, x_ref, mod_ref, ggla_ref, gpm_ref, gpf_ref, wout_ref,
               wrh_ref, wrl_ref, br_ref,
               x1_ref, h2_ref, ti_ref, gt_ref, rk_ref, cnt_ref, base_ref):
    tm = TM_POST

    @pl.when(pl.program_id(0) == 0)
    def _():
        base_ref[...] = jnp.zeros_like(base_ref)

    gain1 = mod_ref[2:3, :] * gpm_ref[...]
    shift2 = mod_ref[3:4, :]
    gain2 = gpf_ref[...] * (1.0 + mod_ref[4:5, :])
    sub = tm // POST_SUB
    subs = [slice(s * sub, (s + 1) * sub) for s in range(POST_SUB)]

    def mixer_out(rows):
        og = of_ref[rows, :] + ob_ref[rows, :]
        gg = gg_ref[rows, :]
        parts = []
        for h in range(GLA_HEADS):
            cols = slice(h * GLA_DV, (h + 1) * GLA_DV)
            parts.append((_rms(og[:, cols], ggla_ref[...]) * _silu(gg[:, cols])).astype(BF16))
        return jnp.concatenate(parts + [oa_ref[rows, :]], axis=1)

    def ffn_in(s, y):
        rows = subs[s]
        x1 = x_ref[rows, :] + _rms(y, gain1)
        x1_ref[rows, :] = x1
        h2 = _rms(x1, gain2) + shift2
        hi = h2.astype(BF16)
        hi32 = hi.astype(F32)
        _pack_rows(h2_ref.at[pl.ds(s * sub * PACK_SUB, sub * PACK_SUB), :], hi32)
        return hi, (h2 - hi32).astype(BF16)

    o_subs = [mixer_out(rows) for rows in subs]
    y_subs = [jnp.dot(o, wout_ref[...], preferred_element_type=F32) for o in o_subs]
    split = [ffn_in(s, y) for s, y in enumerate(y_subs)]
    h2_hi = jnp.concatenate([hi for hi, _ in split], axis=0)
    h2_lo = jnp.concatenate([lo for _, lo in split], axis=0)

    wrh = wrh_ref[...]
    logits = (lax.dot_general(wrh, h2_hi, NT_DIMS, preferred_element_type=F32)
              + lax.dot_general(wrh, h2_lo, NT_DIMS, preferred_element_type=F32)
              + lax.dot_general(wrl_ref[...], h2_hi, NT_DIMS, preferred_element_type=F32)
              + br_ref[...])
    e_iota = lax.broadcasted_iota(I32, (N_EXPERTS, tm), 0)
    idxs, vals = [], []
    work = logits
    for _ in range(TOP_K):
        m = jnp.max(work, axis=0, keepdims=True)
        idx = jnp.min(jnp.where(work == m, e_iota, N_EXPERTS), axis=0, keepdims=True)
        idxs.append(idx)
        vals.append(m)
        work = jnp.where(e_iota == idx, -jnp.inf, work)
    exps = [jnp.exp(v - vals[0]) for v in vals]
    inv = 1.0 / (exps[0] + exps[1] + exps[2] + exps[3])
    gt_ref[...] = jnp.concatenate([e * inv for e in exps], axis=0)
    ti_ref[...] = jnp.concatenate(idxs, axis=0)

    onehots = [e_iota == idx for idx in idxs]
    member = jnp.where(onehots[0] | onehots[1] | onehots[2] | onehots[3], 1.0, 0.0)
    t_row = lax.broadcasted_iota(I32, (tm, tm), 0)
    t_col = lax.broadcasted_iota(I32, (tm, tm), 1)
    strict = jnp.where(t_row < t_col, 1.0, 0.0).astype(BF16)
    before = base_ref[...] + jnp.dot(member.astype(BF16), strict, preferred_element_type=F32)
    rk_ref[...] = jnp.concatenate(
        [jnp.sum(jnp.where(oh, before, 0.0), axis=0, keepdims=True) for oh in onehots], axis=0).astype(I32)
    new_base = base_ref[...] + jnp.sum(member, axis=1, keepdims=True)
    base_ref[...] = new_base
    cnt_ref[...] = jnp.broadcast_to(new_base, cnt_ref.shape)


def _post(o_f, o_b, gg, o_att, x2, mod, g_gla, g_pm, g_pf, wout, wrh, wrl, br):
    t, d = x2.shape
    tm = TM_POST
    tiles_per_seq = SEQ // tm
    hv = GLA_HEADS * GLA_DV

    def full(a):
        return pl.BlockSpec(a.shape, lambda i: (0,) * a.ndim)

    def rows(w):
        return pl.BlockSpec((tm, w), lambda i: (i, 0))

    def lanes():
        return pl.BlockSpec((TOP_K, tm), lambda i: (0, i))

    return pl.pallas_call(
        _post_body,
        grid=(t // tm,),
        in_specs=[
            rows(hv), rows(hv), rows(hv), rows(hv), rows(d),
            pl.BlockSpec((None, 6, d), lambda i: (i // tiles_per_seq, 0, 0)),
            full(g_gla), full(g_pm), full(g_pf), full(wout), full(wrh), full(wrl), full(br),
        ],
        out_specs=[
            rows(d),
            pl.BlockSpec((tm * PACK_SUB, LANES), lambda i: (i, 0)),
            lanes(), lanes(), lanes(),
            pl.BlockSpec((N_EXPERTS, LANES), lambda i: (0, 0)),
        ],
        out_shape=[
            jax.ShapeDtypeStruct((t, d), F32),
            jax.ShapeDtypeStruct((t * PACK_SUB, LANES), I32),
            jax.ShapeDtypeStruct((TOP_K, t), I32),
            jax.ShapeDtypeStruct((TOP_K, t), F32),
            jax.ShapeDtypeStruct((TOP_K, t), I32),
            jax.ShapeDtypeStruct((N_EXPERTS, LANES), F32),
        ],
        scratch_shapes=[pltpu.VMEM((N_EXPERTS, 1), F32)],
        compiler_params=_params(("arbitrary",), 48),
        name="post",
    )(o_f, o_b, gg, o_att, x2, mod, g_gla, g_pm, g_pf, wout, wrh, wrl, br)


def _route_body(ti_ref, rk_ref, cnt_ref, pos_ref, blk_ref):
    cnt = cnt_ref[...]
    padded = jnp.floor((cnt + (MOE_BM - 1)) * (1.0 / MOE_BM)) * MOE_BM
    starts, ends = [], []
    acc = jnp.zeros((1, LANES), F32)
    for e in range(N_EXPERTS):
        starts.append(acc)
        acc = acc + padded[e:e + 1, :]
        ends.append(acc)
    ti = ti_ref[...]
    off = jnp.zeros(ti.shape, F32)
    for e in range(N_EXPERTS):
        off = jnp.where(ti == e, starts[e][:, 0:1], off)
    pos_ref[...] = rk_ref[...] + off.astype(I32)

    def owner_of(row):
        n_le = jnp.zeros(row.shape, I32)
        for e in range(N_EXPERTS):
            n_le = n_le + jnp.where(ends[e][:, 0:1] <= row, 1, 0)
        return jnp.minimum(n_le, N_EXPERTS - 1)

    block_start = lax.broadcasted_iota(I32, (1, MOE_NB_PAD), 1).astype(F32) * MOE_BM
    owner = owner_of(block_start)
    nxt = jnp.zeros((1, MOE_NB_PAD), I32)
    nxt_blk = jnp.zeros((1, MOE_NB_PAD), I32)
    slot = jnp.zeros((1, MOE_NB_PAD), I32)
    ordinal = jnp.zeros((1, 1), I32)
    for e in range(N_EXPERTS):
        end_e = ends[e][:, 0:1]
        nxt = jnp.where(owner == e, jnp.where(end_e < acc[:, 0:1], owner_of(end_e), -1), nxt)
        nxt_blk = jnp.where(owner == e, (end_e * (1.0 / MOE_BM)).astype(I32), nxt_blk)
        slot = jnp.where(owner == e, ordinal & 1, slot)
        ordinal = ordinal + jnp.where(padded[e:e + 1, 0:1] > 0, 1, 0)
    used = jnp.broadcast_to((acc[:, 0:1] * (1.0 / MOE_BM)).astype(I32), (1, MOE_NB_PAD))
    blk_ref[...] = jnp.concatenate(
        [owner, nxt, used, nxt_blk, slot, jnp.zeros((SUBLANES - 5, MOE_NB_PAD), I32)], axis=0)


def _route(top_i, rank, counts):
    return pl.pallas_call(
        _route_body,
        out_shape=[
            jax.ShapeDtypeStruct(top_i.shape, I32),
            jax.ShapeDtypeStruct((SUBLANES, MOE_NB_PAD), I32),
        ],
        compiler_params=pltpu.CompilerParams(vmem_limit_bytes=32 * 1024 * 1024),
        name="route",
    )(top_i, rank, counts)


def _sc_workers():
    info = plsc.get_sparse_core_info()
    return info.num_cores, info.num_subcores, info.num_lanes


def _sc_gather_loop(table_hbm, out_hbm, idx_v, base, chunks, bufs, gather_sems, write_sems):
    window = SC_GATHER_WINDOW
    ring = len(bufs)

    def fetch(c, b):
        return pltpu.make_async_copy(table_hbm.at[idx_v.at[pl.ds(c * window, window)]], bufs[b], gather_sems[b])

    def flush(c, b):
        return pltpu.make_async_copy(bufs[b], out_hbm.at[pl.ds(base + c * window, window)], write_sems[b])

    for b in range(ring):
        fetch(b, b).start()

    @pl.loop(0, chunks, step=ring)
    def _(c0):
        for b in range(ring):
            c = c0 + b
            fetch(c, b).wait()
            flush(c, b).start()

            @pl.when(c + ring < chunks)
            def _():
                flush(c, b).wait()
                fetch(c + ring, b).start()

    for b in range(ring):
        flush(chunks - ring + b, b).wait()


def _sc_source_rows(pos_flat, n_rows):
    cores, subcores, lanes = _sc_workers()
    workers = cores * subcores
    per_worker = n_rows // workers
    n_assign = pos_flat.shape[0]
    scan = SC_SCAN_CHUNK
    assert per_worker * workers == n_rows and per_worker % lanes == 0
    assert n_assign % scan == 0 and scan % lanes == 0 and TOKENS & (TOKENS - 1) == 0
    mesh = plsc.VectorSubcoreMesh(core_axis_name="core", subcore_axis_name="subcore")

    @functools.partial(
        pl.kernel,
        out_type=jax.ShapeDtypeStruct((n_rows,), I32),
        mesh=mesh,
        scratch_types=[pltpu.VMEM((per_worker,), I32), pltpu.VMEM((scan,), I32)],
        compiler_params=pltpu.CompilerParams(needs_layout_passes=False),
        name="sc_source_rows",
    )
    def invert(pos_hbm, out_hbm, src_v, pos_v):
        wid = lax.axis_index("subcore") * cores + lax.axis_index("core")
        base = wid * per_worker
        lane = lax.iota(I32, lanes)

        @pl.loop(0, per_worker, step=lanes)
        def _(j):
            src_v[pl.ds(j, lanes)] = (base + j + lane) & (TOKENS - 1)

        @pl.loop(0, n_assign, step=scan)
        def _(a0):
            pltpu.sync_copy(pos_hbm.at[pl.ds(a0, scan)], pos_v)

            @plsc.parallel_loop(0, scan, step=lanes, unroll=SC_SCAN_UNROLL)
            def _(j):
                rel = pos_v[pl.ds(j, lanes)] - base
                mine = (rel >= 0) & (rel < per_worker)
                tok = (a0 + j + lane) & (TOKENS - 1)
                plsc.store_scatter(src_v, [jnp.where(mine, rel, 0)], tok, mask=mine)

        pltpu.sync_copy(src_v, out_hbm.at[pl.ds(base, per_worker)])

    return invert(pos_flat)


def _sc_gather_rows(table, idx, ring=SC_GATHER_RING):
    cores, subcores, _ = _sc_workers()
    workers = cores * subcores
    n = idx.shape[0]
    window = SC_GATHER_WINDOW
    per_worker = n // workers
    chunks = per_worker // window
    assert per_worker * workers == n and chunks * window == per_worker and chunks % ring == 0
    row_shape = table.shape[1:]
    mesh = plsc.VectorSubcoreMesh(core_axis_name="core", subcore_axis_name="subcore")

    @functools.partial(
        pl.kernel,
        out_type=jax.ShapeDtypeStruct((n,) + row_shape, table.dtype),
        mesh=mesh,
        scratch_types=[pltpu.VMEM((per_worker,), I32)]
        + [pltpu.VMEM((window,) + row_shape, table.dtype)] * ring
        + [pltpu.SemaphoreType.DMA] * (2 * ring),
        name="sc_gather_rows",
    )
    def gather(table_hbm, idx_hbm, out_hbm, idx_v, *scratch):
        wid = lax.axis_index("subcore") * cores + lax.axis_index("core")
        base = wid * per_worker
        pltpu.sync_copy(idx_hbm.at[pl.ds(base, per_worker)], idx_v)
        _sc_gather_loop(table_hbm, out_hbm, idx_v, base, chunks,
                        scratch[:ring], scratch[ring:2 * ring], scratch[2 * ring:])

    return gather(table, idx)


def _experts_body(first, be_ref, nxt_ref, nxtblk_ref, slot_ref, nu_ref, xs_ref, bgu_ref, bd_ref, wgu_hbm, wd_hbm,
                  *rest):
    ys_ref, wgu_f32, wd_f32, sems = rest[-4:]
    step = pl.program_id(0)
    end = first + pl.num_programs(0) * MOE_SUB
    bm = MOE_BM
    n_used = nu_ref[0]

    def fetch(e, slot):
        return (pltpu.make_async_copy(wgu_hbm.at[e], wgu_f32.at[slot], sems.at[0, slot]),
                pltpu.make_async_copy(wd_hbm.at[e], wd_f32.at[slot], sems.at[1, slot]))

    @pl.when((step == 0) & (first < n_used))
    def _():
        for cp in fetch(be_ref[first], slot_ref[first]):
            cp.start()

    for sub in range(MOE_SUB):
        i = first + step * MOE_SUB + sub
        e = be_ref[i]
        slot = slot_ref[i]
        used = i < n_used
        fresh = (i == first) | (e != be_ref[jnp.maximum(i - 1, 0)])
        rows = pl.ds(sub * bm * PACK_SUB, bm * PACK_SUB)
        xs_sub = xs_ref.at[rows, :]
        ys_sub = ys_ref.at[rows, :]

        @pl.when(used & fresh)
        def _():
            for cp in fetch(e, slot):
                cp.wait()

            @pl.when((nxt_ref[i] >= 0) & (nxtblk_ref[i] < end))
            def _():
                for cp in fetch(nxt_ref[i], 1 - slot):
                    cp.start()

        @pl.when(used)
        def _():
            x = jnp.concatenate(_unpack_rows(xs_sub, bm), axis=1).astype(BF16)
            gu = jnp.dot(x, wgu_f32[slot].astype(BF16), preferred_element_type=F32) + bgu_ref[pl.ds(e, 1), :]
            gate = jnp.minimum(gu[:, 0:D_FF], SWIGLU_LIMIT)
            up = jnp.clip(gu[:, D_FF:2 * D_FF], -SWIGLU_LIMIT, SWIGLU_LIMIT)
            act = ((up + 1.0) * gate * jax.nn.sigmoid(SWIGLU_ALPHA * gate)).astype(BF16)
            y = jnp.dot(act, wd_f32[slot].astype(BF16), preferred_element_type=F32) + bd_ref[pl.ds(e, 1), :]
            _pack_rows(ys_sub, y.astype(BF16).astype(F32))

        @pl.when(jnp.logical_not(used))
        def _():
            ys_sub[...] = jnp.zeros((bm * PACK_SUB, LANES), I32)


def _experts(first, part_blocks, blocks, xs_part, w_gate_up, b_gate_up, w_down, b_down, ys_prev):
    rows = MOE_SUB * MOE_BM * PACK_SUB
    d = D_MODEL
    steps = part_blocks // MOE_SUB
    assert steps * MOE_SUB == part_blocks and first % MOE_SUB == 0

    def x_block(i, be, nx, nb, sl, nu):
        last = jnp.maximum((jnp.minimum(nu[0], first + part_blocks) - 1 - first) // MOE_SUB, 0)
        return jnp.minimum(i, last)

    in_specs = [
        pl.BlockSpec((rows, LANES), lambda i, be, nx, nb, sl, nu: (x_block(i, be, nx, nb, sl, nu), 0)),
        pl.BlockSpec(b_gate_up.shape, lambda i, be, nx, nb, sl, nu: (0, 0)),
        pl.BlockSpec(b_down.shape, lambda i, be, nx, nb, sl, nu: (0, 0)),
        pl.BlockSpec(memory_space=pl.ANY),
        pl.BlockSpec(memory_space=pl.ANY),
    ]
    operands = [blocks[0, :MOE_NB], blocks[1, :MOE_NB], blocks[3, :MOE_NB], blocks[4, :MOE_NB], blocks[2, :1],
                xs_part, b_gate_up, b_down, w_gate_up, w_down]
    aliases = {}
    if ys_prev is not None:
        in_specs.append(pl.BlockSpec(memory_space=pl.ANY))
        aliases = {len(operands): 0}
        operands.append(ys_prev)
    grid_spec = pltpu.PrefetchScalarGridSpec(
        num_scalar_prefetch=5,
        grid=(steps,),
        in_specs=in_specs,
        out_specs=pl.BlockSpec((rows, LANES), lambda i, be, nx, nb, sl, nu: (first // MOE_SUB + i, 0)),
        scratch_shapes=[
            pltpu.VMEM((2, d, 2 * D_FF), F32), pltpu.VMEM((2, D_FF, d), F32),
            pltpu.SemaphoreType.DMA((2, 2)),
        ],
    )
    return pl.pallas_call(
        functools.partial(_experts_body, first),
        grid_spec=grid_spec,
        out_shape=jax.ShapeDtypeStruct((MOE_ROWS * PACK_SUB, LANES), I32),
        input_output_aliases=aliases,
        compiler_params=_params(("arbitrary",), 48),
        name="experts",
    )(*operands)


def _combine_body(gates_ref, x1_ref, mod_ref, gpost_ref, y0_ref, y1_ref, y2_ref, y3_ref, o_ref):
    tm = TM_COMB
    gates = jnp.concatenate([gates_ref[...], jnp.zeros((SUBLANES - TOP_K, tm), F32)], axis=0).T
    y_hi = jnp.zeros((tm, PACK_COLS), F32)
    y_lo = jnp.zeros((tm, PACK_COLS), F32)
    for k, yk_ref in enumerate((y0_ref, y1_ref, y2_ref, y3_ref)):
        hi, lo = _unpack_rows(yk_ref, tm)
        y_hi = y_hi + hi * gates[:, k:k + 1]
        y_lo = y_lo + lo * gates[:, k:k + 1]
    y = jnp.concatenate([y_hi, y_lo], axis=1)
    gain = mod_ref[5:6, :] * gpost_ref[...]
    o_ref[...] = x1_ref[...] + _rms(y, gain)


def _combine(gates, x1, mod, g_post, y4):
    t, d = x1.shape
    tm = TM_COMB
    tiles = t // tm
    tiles_per_seq = SEQ // tm

    def slab(k):
        return pl.BlockSpec((tm * PACK_SUB, LANES), lambda i: (k * tiles + i, 0))

    return pl.pallas_call(
        _combine_body,
        grid=(tiles,),
        in_specs=[
            pl.BlockSpec((TOP_K, tm), lambda i: (0, i)),
            pl.BlockSpec((tm, d), lambda i: (i, 0)),
            pl.BlockSpec((None, 6, d), lambda i: (i // tiles_per_seq, 0, 0)),
            pl.BlockSpec(g_post.shape, lambda i: (0, 0)),
            slab(0), slab(1), slab(2), slab(3),
        ],
        out_specs=pl.BlockSpec((tm, d), lambda i: (i, 0)),
        out_shape=jax.ShapeDtypeStruct((t, d), F32),
        compiler_params=_params(("arbitrary",), 48),
        name="combine",
    )(gates, x1, mod, g_post, y4, y4, y4, y4)


def _rotary_tables():
    half = ROT_DIM // 2
    inv_freq = ROPE_THETA ** (-2.0 * np.arange(half, dtype=np.float32) / ROT_DIM)
    ang = np.arange(SEQ, dtype=np.float32)[:, None] * inv_freq[None, :].astype(np.float32)
    cos, sin = np.cos(ang), np.sin(ang)
    ones = np.ones((SEQ, ATT_HEAD_DIM - ROT_DIM), np.float32)
    zeros = np.zeros((SEQ, ATT_HEAD_DIM - ROT_DIM), np.float32)
    zh = np.zeros((SEQ, half), np.float32)
    reps = LANES // ATT_HEAD_DIM
    rc = np.tile(np.concatenate([cos, cos, ones], axis=1), (1, reps))
    rm = np.tile(np.concatenate([-sin, zh, zeros], axis=1), (1, reps))
    rp = np.tile(np.concatenate([zh, sin, zeros], axis=1), (1, reps))
    return tuple(jnp.asarray(t, F32) for t in (rc, rm, rp))


def _mixer_inputs(w_in, w_gk_fwd, b_gk_fwd, w_gk_bwd, b_gk_bwd):
    hk = GLA_HEADS * GLA_DK
    hv = GLA_HEADS * GLA_DV
    w = w_in[0]
    o_lr = 2 * hk + 2 * hv
    o_aq = o_lr + 2 * GLA_RANK
    o_ak = o_aq + ATT_Q_HEADS * ATT_HEAD_DIM
    o_av = o_ak + ATT_KV_HEADS * ATT_HEAD_DIM
    hd = ATT_HEAD_DIM
    wa = w[:, :o_lr].astype(BF16)
    wlr = w[:, o_lr:o_aq].astype(BF16)
    dup = lambda m: jnp.concatenate([m[:, g * hd:(g + 1) * hd] for g in range(ATT_KV_HEADS) for _ in range(2)], axis=1)
    wb = jnp.concatenate([w[:, o_aq:o_ak], dup(w[:, o_ak:o_av])], axis=1).astype(BF16)
    wvt = dup(w[:, o_av:o_av + ATT_KV_HEADS * hd]).T.astype(BF16)
    zr = jnp.zeros((GLA_RANK, hk), F32)
    wgk = jnp.concatenate([jnp.concatenate([w_gk_fwd[0], zr], axis=1),
                           jnp.concatenate([zr, w_gk_bwd[0]], axis=1)], axis=0).astype(BF16)
    bgk = jnp.concatenate([b_gk_fwd[0], b_gk_bwd[0]])[None, :]
    return (wa, wlr, wgk, bgk, wb, wvt) + _rotary_tables()


def kernel(x, c, w_ada, b_ada, g_pre_mix, g_post_mix, w_in, w_gk_fwd, b_gk_fwd, w_gk_bwd, b_gk_bwd, g_gla_out,
           attn_sink, w_out, g_pre_ffn, g_post_ffn, w_router, b_router, w_gate_up, b_gate_up, w_down, b_down):
    assert x.shape == (BATCH, SEQ, D_MODEL) and w_ada.shape[0] == 1
    d = D_MODEL
    x2 = x.reshape(TOKENS, d)

    c_pad = jnp.pad(c, ((0, SUBLANES - BATCH), (0, 0)))
    mod = _ada(c_pad, w_ada[0], b_ada)[:BATCH].reshape(BATCH, 6, d)

    mixer_in = _mixer_inputs(w_in, w_gk_fwd, b_gk_fwd, w_gk_bwd, b_gk_bwd)
    q, k, v, gg, laf, lab, aq, ak2, avt = _inproj(x2, mod, g_pre_mix, *mixer_in)
    o_f, o_b = _gla(q, k, v, laf, lab)
    o_att = _attn(attn_sink[0], aq, ak2, avt)

    wr_t = w_router[0].T
    wrh = wr_t.astype(BF16)
    wrl = (wr_t - wrh.astype(F32)).astype(BF16)
    x1, h2_tiles, top_i, gates, rank, counts = _post(
        o_f, o_b, gg, o_att, x2, mod, g_gla_out, g_post_mix, g_pre_ffn, w_out[0].astype(BF16), wrh, wrl,
        b_router[0][:, None])

    pos, blocks = _route(top_i, rank, counts)

    src = _sc_source_rows(pos.reshape(TOP_K * TOKENS), MOE_ROWS)
    h2_rows = h2_tiles.reshape(TOKENS, PACK_SUB, LANES)
    ys = None
    first = 0
    for part_blocks, ring in zip(MOE_PART_BLOCKS, MOE_PART_RINGS):
        row0, n_rows = first * MOE_BM, part_blocks * MOE_BM
        xs_p = _sc_gather_rows(h2_rows, src[row0:row0 + n_rows], ring)
        ys = _experts(first, part_blocks, blocks, xs_p.reshape(n_rows * PACK_SUB, LANES),
                      w_gate_up[0], b_gate_up[0], w_down[0], b_down[0], ys)
        first += part_blocks
    assert first == MOE_NB

    y4 = _sc_gather_rows(ys.reshape(MOE_ROWS, PACK_SUB, LANES), pos.reshape(TOP_K * TOKENS))
    out = _combine(gates, x1, mod, g_post_ffn, y4.reshape(TOP_K * TOKENS * PACK_SUB, LANES))
    return out.reshape(BATCH, SEQ, d)
```

```python
import functools

import jax
import jax.numpy as jnp
import numpy as np
from jax import lax
from jax.experimental import pallas as pl
from jax.experimental.pallas import tpu as pltpu
from jax.experimental.pallas import tpu_sc as plsc

F32 = jnp.float32
BF16 = jnp.bfloat16
I32 = jnp.int32

D_MODEL = 1024
BATCH = 2
SEQ = 8192
TOKENS = BATCH * SEQ
GLA_HEADS = 4
GLA_DV = 128
GLA_DK = 64
GLA_RANK = 16
GLA_GATE_NORMALIZER = 16.0
GLA_CHUNK = 64
ATT_Q_HEADS = 8
ATT_KV_HEADS = 2
ATT_HEAD_DIM = 64
ATT_WINDOW = 128
ATT_BLOCK = 128
ROT_DIM = 16
ROPE_THETA = 500000.0
N_EXPERTS = 32
TOP_K = 4
D_FF = 1024
SWIGLU_LIMIT = 7.0
SWIGLU_ALPHA = 1.702
NORM_EPS = 1e-6
NEG_INF = -1e30
LOG2_E = 1.4426950408889634

LANES = 128
SUBLANES = 8
PACK_COLS = D_MODEL // 2
PACK_SUB = PACK_COLS // LANES

TM_IN = 1024
IN_SUB = 4
GLA_GROUP = 16
ATT_GROUP = 16
TM_POST = 1024
POST_SUB = 4
MOE_BM = 256
MOE_ROWS = TOKENS * TOP_K + N_EXPERTS * MOE_BM
MOE_NB = MOE_ROWS // MOE_BM
MOE_SUB = 4
MOE_PART_BLOCKS = (32, 96, 160)
MOE_NB_PAD = ((MOE_NB + LANES - 1) // LANES) * LANES
SC_SCAN_CHUNK = 4096
SC_SCAN_UNROLL = 8
TM_COMB = 1024
SC_GATHER_WINDOW = 32
SC_GATHER_RING = 4
MOE_PART_RINGS = (SC_GATHER_RING, 1, 1)

NT_DIMS = (((1,), (1,)), ((), ()))
TN_DIMS = (((0,), (0,)), ((), ()))


def _params(semantics, vmem_mib):
    return pltpu.CompilerParams(dimension_semantics=semantics, vmem_limit_bytes=vmem_mib * 1024 * 1024)


def _rms(x, g):
    return x * lax.rsqrt(jnp.mean(x * x, axis=-1, keepdims=True) + NORM_EPS) * g


def _silu(x):
    return x * jax.nn.sigmoid(x)


def _pack_rows(ref, v):
    m = v.shape[0]
    bits = lax.bitcast_convert_type(v, jnp.uint32)
    word = lax.bitcast_convert_type(bits[:, :PACK_COLS] | (bits[:, PACK_COLS:] >> 16), I32)
    for s in range(PACK_SUB):
        ref[pl.ds(s, m, stride=PACK_SUB), :] = word[:, s * LANES:(s + 1) * LANES]


def _unpack_rows(ref, m):
    word = jnp.concatenate([ref[pl.ds(s, m, stride=PACK_SUB), :] for s in range(PACK_SUB)], axis=1)
    bits = lax.bitcast_convert_type(word, jnp.uint32)
    hi = lax.bitcast_convert_type(bits & jnp.uint32(0xFFFF0000), F32)
    lo = lax.bitcast_convert_type(bits << 16, F32)
    return hi, lo


def _ada_body(c_ref, w_ref, b_ref, o_ref):
    ca = _silu(c_ref[...]).astype(BF16)
    o_ref[...] = jnp.dot(ca, w_ref[...].astype(BF16), preferred_element_type=F32) + b_ref[...]


def _ada(c_pad, w_ada, b_ada):
    d = D_MODEL
    return pl.pallas_call(
        _ada_body,
        grid=(6,),
        in_specs=[
            pl.BlockSpec((SUBLANES, d), lambda j: (0, 0)),
            pl.BlockSpec((d, d), lambda j: (0, j)),
            pl.BlockSpec((1, d), lambda j: (0, j)),
        ],
        out_specs=pl.BlockSpec((SUBLANES, d), lambda j: (0, j)),
        out_shape=jax.ShapeDtypeStruct((SUBLANES, 6 * d), F32),
        compiler_params=_params(("arbitrary",), 32),
        name="ada",
    )(c_pad, w_ada, b_ada)


def _rotary(x, cos_t, msin_t, psin_t):
    width = x.shape[1]
    reps = width // LANES
    c = jnp.concatenate([cos_t] * reps, axis=1)
    m = jnp.concatenate([msin_t] * reps, axis=1)
    p = jnp.concatenate([psin_t] * reps, axis=1)
    half = ROT_DIM // 2
    return x * c + pltpu.roll(x, width - half, 1) * m + pltpu.roll(x, half, 1) * p


def _inproj_body(x_ref, mod_ref, g_ref, wa_ref, wlr_ref, wgk_ref, bgk_ref, wb_ref, wvt_ref, rc_ref, rm_ref, rp_ref,
                 q_ref, k_ref, v_ref, gg_ref, laf_ref, lab_ref, aq_ref, ak_ref, avt_ref):
    shift = mod_ref[0:1, :]
    scale = mod_ref[1:2, :]
    hk = GLA_HEADS * GLA_DK
    hv = GLA_HEADS * GLA_DV
    nq = ATT_Q_HEADS * ATT_HEAD_DIM
    nk = 2 * ATT_KV_HEADS * ATT_HEAD_DIM
    sub = x_ref.shape[0] // IN_SUB
    subs = [slice(s * sub, (s + 1) * sub) for s in range(IN_SUB)]

    gain = g_ref[...] * (1.0 + scale)

    def hidden(rows):
        return (_rms(x_ref[rows, :], gain) + shift).astype(BF16)

    def project(h):
        return (jnp.dot(h, wa_ref[...], preferred_element_type=F32),
                jnp.dot(h, wlr_ref[...], preferred_element_type=F32),
                jnp.dot(h, wb_ref[...], preferred_element_type=F32),
                lax.dot_general(wvt_ref[...], h, NT_DIMS, preferred_element_type=F32))

    def finish(rows, pa, plr, pb, pvt):
        q_ref[rows, :] = pa[:, 0:hk] * (GLA_DK ** -0.5)
        k_ref[rows, :] = pa[:, hk:2 * hk]
        v_ref[rows, :] = pa[:, 2 * hk:2 * hk + hv].astype(BF16)
        gg_ref[rows, :] = pa[:, 2 * hk + hv:2 * hk + 2 * hv]
        gk = jnp.dot(plr.astype(BF16), wgk_ref[...], preferred_element_type=F32) + bgk_ref[...]
        la = (jnp.minimum(gk, 0.0) - jnp.log1p(jnp.exp(-jnp.abs(gk)))) * (1.0 / GLA_GATE_NORMALIZER)
        laf_ref[rows, :] = la[:, 0:hk]
        lab_ref[rows, :] = la[:, hk:2 * hk]
        rc, rm, rp = rc_ref[rows, :], rm_ref[rows, :], rp_ref[rows, :]
        aq_ref[rows, :] = (_rotary(pb[:, 0:nq], rc, rm, rp) * (ATT_HEAD_DIM ** -0.5 * LOG2_E)).astype(BF16)
        ak_ref[rows, :] = _rotary(pb[:, nq:nq + nk], rc, rm, rp).astype(BF16)
        avt_ref[:, rows] = pvt.astype(BF16)

    hs = [hidden(rows) for rows in subs]
    ps = [project(h) for h in hs]
    for rows, p in zip(subs, ps):
        finish(rows, *p)


def _inproj(x2, mod, g_pre, wa, wlr, wgk, bgk, wb, wvt, rc, rm, rp):
    t, d = x2.shape
    tm = TM_IN
    tiles_per_seq = SEQ // tm
    hk = GLA_HEADS * GLA_DK
    hv = GLA_HEADS * GLA_DV
    nq = ATT_Q_HEADS * ATT_HEAD_DIM
    nk = 2 * ATT_KV_HEADS * ATT_HEAD_DIM

    def full(a):
        return pl.BlockSpec(a.shape, lambda i: (0,) * a.ndim)

    def rows(w):
        return pl.BlockSpec((tm, w), lambda i: (i, 0))

    def table():
        return pl.BlockSpec((tm, LANES), lambda i: (i % tiles_per_seq, 0))

    out_widths = [(hk, F32), (hk, F32), (hv, BF16), (hv, F32), (hk, F32), (hk, F32), (nq, BF16), (nk, BF16)]
    return pl.pallas_call(
        _inproj_body,
        grid=(t // tm,),
        in_specs=[
            rows(d),
            pl.BlockSpec((None, 6, d), lambda i: (i // tiles_per_seq, 0, 0)),
            full(g_pre), full(wa), full(wlr), full(wgk), full(bgk), full(wb), full(wvt),
            table(), table(), table(),
        ],
        out_specs=[rows(w) for w, _ in out_widths] + [pl.BlockSpec((nk, tm), lambda i: (0, i))],
        out_shape=[jax.ShapeDtypeStruct((t, w), dt) for w, dt in out_widths] + [jax.ShapeDtypeStruct((nk, t), BF16)],
        compiler_params=_params(("arbitrary",), 56),
        name="inproj",
    )(x2, mod, g_pre, wa, wlr, wgk, bgk, wb, wvt, rc, rm, rp)


def _gla_body(qf_ref, kf_ref, vf_ref, laf_ref, qb_ref, kb_ref, vb_ref, lab_ref, of_ref, ob_ref, sf_ref, sb_ref):
    @pl.when(pl.program_id(1) == 0)
    def _():
        sf_ref[...] = jnp.zeros_like(sf_ref)
        sb_ref[...] = jnp.zeros_like(sb_ref)

    c = GLA_CHUNK
    r_i = lax.broadcasted_iota(I32, (c, c), 0)
    c_i = lax.broadcasted_iota(I32, (c, c), 1)
    lower = c_i <= r_i
    upper = c_i >= r_i
    cum_f = jnp.where(lower, 1.0, 0.0).astype(BF16)
    cum_b = jnp.where(upper, 1.0, 0.0).astype(BF16)
    lane = lax.broadcasted_iota(I32, (1, LANES), 1)
    head_masks = (lane < GLA_DK, lane >= GLA_DK)

    fwd = [(qf_ref, kf_ref, laf_ref, vf_ref, of_ref, slice(g * c, (g + 1) * c), cum_f, lower, c - 1, c // 2 - 1)
           for g in range(GLA_GROUP)]
    bwd = [(qb_ref, kb_ref, lab_ref, vb_ref, ob_ref, slice(g * c, (g + 1) * c), cum_b, upper, 0, c // 2)
           for g in reversed(range(GLA_GROUP))]
    heads = range(GLA_HEADS)
    pair = [slice((h // 2) * LANES, (h // 2 + 1) * LANES) for h in heads]
    vcols = [slice(h * GLA_DV, (h + 1) * GLA_DV) for h in heads]

    def stage1(item):
        q_ref, k_ref, la_ref, v_ref, o_ref, rows, cum, tri, i_last, i_mid = item
        la = la_ref[rows, :]
        hi = la.astype(BF16)
        lo = (la - hi.astype(F32)).astype(BF16)
        b = jnp.dot(cum, hi, preferred_element_type=F32) + jnp.dot(cum, lo, preferred_element_type=F32)
        b_last = b[i_last:i_last + 1, :]
        b_mid = b[i_mid:i_mid + 1, :]
        q, k = q_ref[rows, :], k_ref[rows, :]
        return (q * jnp.exp(b - b_mid), (k * jnp.exp(b_mid - b)).astype(BF16), q * jnp.exp(b),
                (k * jnp.exp(b_last - b)).astype(BF16), jnp.exp(b_last))

    def stage2(item, pre):
        v_ref, rows, tri = item[3], item[5], item[7]
        qs, ks, qi, kst, decay = pre
        out = []
        for h in heads:
            mask = head_masks[h % 2]
            qs_h = jnp.where(mask, qs[:, pair[h]], 0.0).astype(BF16)
            sc = lax.dot_general(qs_h, ks[:, pair[h]], NT_DIMS, preferred_element_type=F32)
            v_h = v_ref[rows, vcols[h]]
            kv = lax.dot_general(v_h, kst[:, pair[h]], TN_DIMS, preferred_element_type=F32)
            out.append((jnp.where(tri, sc, 0.0).astype(BF16), kv,
                        jnp.where(mask, qi[:, pair[h]], 0.0).astype(BF16), v_h))
        return out

    def run(items, s_ref):
        pre = [stage1(it) for it in items]
        mid = [stage2(it, p) for it, p in zip(items, pre)]
        states = [s_ref[h] for h in heads]
        for it, p, m in zip(items, pre, mid):
            o_ref, rows, decay = it[4], it[5], p[4]
            for h in heads:
                sc, kv, qi_h, v_h = m[h]
                o = jnp.dot(sc, v_h, preferred_element_type=F32)
                o = o + lax.dot_general(qi_h, states[h].astype(BF16), NT_DIMS, preferred_element_type=F32)
                o_ref[rows, vcols[h]] = o
                states[h] = states[h] * decay[:, pair[h]] + kv
        for h in heads:
            s_ref[h] = states[h]

    run(fwd, sf_ref)
    run(bwd, sb_ref)


def _gla(q, k, v, laf, lab):
    t = q.shape[0]
    rows = GLA_GROUP * GLA_CHUNK
    ng = SEQ // rows
    hk = GLA_HEADS * GLA_DK
    hv = GLA_HEADS * GLA_DV

    def fwd(w):
        return pl.BlockSpec((rows, w), lambda b, n: (b * ng + n, 0))

    def bwd(w):
        return pl.BlockSpec((rows, w), lambda b, n: (b * ng + ng - 1 - n, 0))

    return pl.pallas_call(
        _gla_body,
        grid=(BATCH, ng),
        in_specs=[fwd(hk), fwd(hk), fwd(hv), fwd(hk), bwd(hk), bwd(hk), bwd(hv), bwd(hk)],
        out_specs=[fwd(hv), bwd(hv)],
        out_shape=[jax.ShapeDtypeStruct((t, hv), F32)] * 2,
        scratch_shapes=[pltpu.VMEM((GLA_HEADS, GLA_DV, 2 * GLA_DK), F32)] * 2,
        compiler_params=_params(("arbitrary", "arbitrary"), 32),
        name="gla",
    )(q, k, v, laf, q, k, v, lab)


def _attn_body(sink_ref, q_ref, kp_ref, kc_ref, kn_ref, vp_ref, vc_ref, vn_ref, o_ref):
    step = pl.program_id(1)
    last = pl.num_programs(1) - 1
    qb = ATT_BLOCK
    hd = ATT_HEAD_DIM
    k_all = jnp.concatenate([kp_ref[...], kc_ref[...], kn_ref[...]], axis=0)
    vt_all = jnp.concatenate([vp_ref[...], vc_ref[...], vn_ref[...]], axis=1)
    lane = lax.broadcasted_iota(I32, (1, LANES), 1)
    lo = lane < hd
    j_k = lax.broadcasted_iota(I32, (3 * qb, qb), 0)
    i_q = lax.broadcasted_iota(I32, (3 * qb, qb), 1)
    band = jnp.abs(j_k - qb - i_q) <= ATT_WINDOW
    sinks = [jnp.concatenate([jnp.full((1, qb), sink_ref[4 * g + r] * LOG2_E, F32) for r in range(4)], axis=1)
             for g in range(ATT_KV_HEADS)]
    ones_rows = jnp.ones((SUBLANES, 3 * qb), BF16)
    work = [(j, g) for j in range(ATT_GROUP) for g in range(ATT_KV_HEADS)]

    def scores(j, g):
        valid = band
        if j == 0:
            valid = valid & ((j_k >= qb) | (step > 0))
        if j == ATT_GROUP - 1:
            valid = valid & ((j_k < 2 * qb) | (step < last))
        valid4 = jnp.concatenate([valid] * 4, axis=1)
        rows = slice(j * qb, (j + 1) * qb)
        kg = k_all[j * qb:(j + 3) * qb, g * LANES:(g + 1) * LANES]
        qa = q_ref[rows, (2 * g) * LANES:(2 * g + 1) * LANES]
        qc = q_ref[rows, (2 * g + 1) * LANES:(2 * g + 2) * LANES]
        zero = jnp.zeros_like(qa)
        lhs = jnp.concatenate([jnp.where(lo, qa, zero), jnp.where(lo, zero, qa),
                               jnp.where(lo, qc, zero), jnp.where(lo, zero, qc)], axis=0)
        st = lax.dot_general(kg, lhs, NT_DIMS, preferred_element_type=F32)
        return jnp.concatenate([jnp.where(valid4[0:qb], st[0:qb], NEG_INF), st[qb:2 * qb],
                                jnp.where(valid4[2 * qb:3 * qb], st[2 * qb:3 * qb], NEG_INF)], axis=0)

    def softmax(st, g):
        sink = sinks[g]
        m = jnp.maximum(jnp.max(st, axis=0, keepdims=True), sink)
        return jnp.exp2(st - m).astype(BF16), jnp.exp2(sink - m)

    def output(j, g, p, p_sink):
        rows = slice(j * qb, (j + 1) * qb)
        vgt = vt_all[g * LANES:(g + 1) * LANES, j * qb:(j + 3) * qb]
        res = jnp.dot(jnp.concatenate([vgt, ones_rows], axis=0), p, preferred_element_type=F32)
        ot = res[0:LANES] * (1.0 / (res[LANES:LANES + 1] + p_sink))
        pair_a = jnp.concatenate([ot[0:hd, 0:qb], ot[hd:2 * hd, qb:2 * qb]], axis=0)
        pair_c = jnp.concatenate([ot[0:hd, 2 * qb:3 * qb], ot[hd:2 * hd, 3 * qb:4 * qb]], axis=0)
        o_ref[rows, (2 * g) * LANES:(2 * g + 1) * LANES] = pair_a.T.astype(o_ref.dtype)
        o_ref[rows, (2 * g + 1) * LANES:(2 * g + 2) * LANES] = pair_c.T.astype(o_ref.dtype)

    s_all = [scores(j, g) for j, g in work]
    p_all = [softmax(st, g) for st, (j, g) in zip(s_all, work)]
    for (j, g), (p, p_sink) in zip(work, p_all):
        output(j, g, p, p_sink)


def _attn(sink, aq, ak2, avt):
    t = aq.shape[0]
    qb = ATT_BLOCK
    nb = SEQ // qb
    steps = nb // ATT_GROUP
    nq = ATT_Q_HEADS * ATT_HEAD_DIM
    nk = 2 * ATT_KV_HEADS * ATT_HEAD_DIM

    def edge_block(b, n, shift):
        return b * nb + jnp.clip(n * ATT_GROUP + shift, 0, nb - 1)

    def k_edge(shift):
        return pl.BlockSpec((qb, nk), lambda b, n: (edge_block(b, n, shift), 0))

    def v_edge(shift):
        return pl.BlockSpec((nk, qb), lambda b, n: (0, edge_block(b, n, shift)))

    def group(w):
        return pl.BlockSpec((ATT_GROUP * qb, w), lambda b, n: (b * steps + n, 0))

    v_group = pl.BlockSpec((nk, ATT_GROUP * qb), lambda b, n: (0, b * steps + n))
    return pl.pallas_call(
        _attn_body,
        grid=(BATCH, steps),
        in_specs=[
            pl.BlockSpec(memory_space=pltpu.SMEM),
            group(nq),
            k_edge(-1), group(nk), k_edge(ATT_GROUP), v_edge(-1), v_group, v_edge(ATT_GROUP),
        ],
        out_specs=group(nq),
        out_shape=jax.ShapeDtypeStruct((t, nq), BF16),
        compiler_params=_params(("arbitrary", "arbitrary"), 48),
        name="attn",
    )(sink, aq, ak2, ak2, ak2, avt, avt, avt)


def _post_body(of_ref, ob_ref, gg_ref, oa_ref, x_ref, mod_ref, ggla_ref, gpm_ref, gpf_ref, wout_ref,
               wrh_ref, wrl_ref, br_ref,
               x1_ref, h2_ref, ti_ref, gt_ref, rk_ref, cnt_ref, base_ref):
    tm = TM_POST

    @pl.when(pl.program_id(0) == 0)
    def _():
        base_ref[...] = jnp.zeros_like(base_ref)

    gain1 = mod_ref[2:3, :] * gpm_ref[...]
    shift2 = mod_ref[3:4, :]
    gain2 = gpf_ref[...] * (1.0 + mod_ref[4:5, :])
    sub = tm // POST_SUB
    subs = [slice(s * sub, (s + 1) * sub) for s in range(POST_SUB)]

    def mixer_out(rows):
        og = of_ref[rows, :] + ob_ref[rows, :]
        gg = gg_ref[rows, :]
        parts = []
        for h in range(GLA_HEADS):
            cols = slice(h * GLA_DV, (h + 1) * GLA_DV)
            parts.append((_rms(og[:, cols], ggla_ref[...]) * _silu(gg[:, cols])).astype(BF16))
        return jnp.concatenate(parts + [oa_ref[rows, :]], axis=1)

    def ffn_in(s, y):
        rows = subs[s]
        x1 = x_ref[rows, :] + _rms(y, gain1)
        x1_ref[rows, :] = x1
        h2 = _rms(x1, gain2) + shift2
        hi = h2.astype(BF16)
        hi32 = hi.astype(F32)
        _pack_rows(h2_ref.at[pl.ds(s * sub * PACK_SUB, sub * PACK_SUB), :], hi32)
        return hi, (h2 - hi32).astype(BF16)

    o_subs = [mixer_out(rows) for rows in subs]
    y_subs = [jnp.dot(o, wout_ref[...], preferred_element_type=F32) for o in o_subs]
    split = [ffn_in(s, y) for s, y in enumerate(y_subs)]
    h2_hi = jnp.concatenate([hi for hi, _ in split], axis=0)
    h2_lo = jnp.concatenate([lo for _, lo in split], axis=0)

    wrh = wrh_ref[...]
    logits = (lax.dot_general(wrh, h2_hi, NT_DIMS, preferred_element_type=F32)
              + lax.dot_general(wrh, h2_lo, NT_DIMS, preferred_element_type=F32)
              + lax.dot_general(wrl_ref[...], h2_hi, NT_DIMS, preferred_element_type=F32)
              + br_ref[...])
    e_iota = lax.broadcasted_iota(I32, (N_EXPERTS, tm), 0)
    idxs, vals = [], []
    work = logits
    for _ in range(TOP_K):
        m = jnp.max(work, axis=0, keepdims=True)
        idx = jnp.min(jnp.where(work == m, e_iota, N_EXPERTS), axis=0, keepdims=True)
        idxs.append(idx)
        vals.append(m)
        work = jnp.where(e_iota == idx, -jnp.inf, work)
    exps = [jnp.exp(v - vals[0]) for v in vals]
    inv = 1.0 / (exps[0] + exps[1] + exps[2] + exps[3])
    gt_ref[...] = jnp.concatenate([e * inv for e in exps], axis=0)
    ti_ref[...] = jnp.concatenate(idxs, axis=0)

    onehots = [e_iota == idx for idx in idxs]
    member = jnp.where(onehots[0] | onehots[1] | onehots[2] | onehots[3], 1.0, 0.0)
    t_row = lax.broadcasted_iota(I32, (tm, tm), 0)
    t_col = lax.broadcasted_iota(I32, (tm, tm), 1)
    strict = jnp.where(t_row < t_col, 1.0, 0.0).astype(BF16)
    before = base_ref[...] + jnp.dot(member.astype(BF16), strict, preferred_element_type=F32)
    rk_ref[...] = jnp.concatenate(
        [jnp.sum(jnp.where(oh, before, 0.0), axis=0, keepdims=True) for oh in onehots], axis=0).astype(I32)
    new_base = base_ref[...] + jnp.sum(member, axis=1, keepdims=True)
    base_ref[...] = new_base
    cnt_ref[...] = jnp.broadcast_to(new_base, cnt_ref.shape)


def _post(o_f, o_b, gg, o_att, x2, mod, g_gla, g_pm, g_pf, wout, wrh, wrl, br):
    t, d = x2.shape
    tm = TM_POST
    tiles_per_seq = SEQ // tm
    hv = GLA_HEADS * GLA_DV

    def full(a):
        return pl.BlockSpec(a.shape, lambda i: (0,) * a.ndim)

    def rows(w):
        return pl.BlockSpec((tm, w), lambda i: (i, 0))

    def lanes():
        return pl.BlockSpec((TOP_K, tm), lambda i: (0, i))

    return pl.pallas_call(
        _post_body,
        grid=(t // tm,),
        in_specs=[
            rows(hv), rows(hv), rows(hv), rows(hv), rows(d),
            pl.BlockSpec((None, 6, d), lambda i: (i // tiles_per_seq, 0, 0)),
            full(g_gla), full(g_pm), full(g_pf), full(wout), full(wrh), full(wrl), full(br),
        ],
        out_specs=[
            rows(d),
            pl.BlockSpec((tm * PACK_SUB, LANES), lambda i: (i, 0)),
            lanes(), lanes(), lanes(),
            pl.BlockSpec((N_EXPERTS, LANES), lambda i: (0, 0)),
        ],
        out_shape=[
            jax.ShapeDtypeStruct((t, d), F32),
            jax.ShapeDtypeStruct((t * PACK_SUB, LANES), I32),
            jax.ShapeDtypeStruct((TOP_K, t), I32),
            jax.ShapeDtypeStruct((TOP_K, t), F32),
            jax.ShapeDtypeStruct((TOP_K, t), I32),
            jax.ShapeDtypeStruct((N_EXPERTS, LANES), F32),
        ],
        scratch_shapes=[pltpu.VMEM((N_EXPERTS, 1), F32)],
        compiler_params=_params(("arbitrary",), 48),
        name="post",
    )(o_f, o_b, gg, o_att, x2, mod, g_gla, g_pm, g_pf, wout, wrh, wrl, br)


def _route_body(ti_ref, rk_ref, cnt_ref, pos_ref, blk_ref):
    cnt = cnt_ref[...]
    padded = jnp.floor((cnt + (MOE_BM - 1)) * (1.0 / MOE_BM)) * MOE_BM
    starts, ends = [], []
    acc = jnp.zeros((1, LANES), F32)
    for e in range(N_EXPERTS):
        starts.append(acc)
        acc = acc + padded[e:e + 1, :]
        ends.append(acc)
    ti = ti_ref[...]
    off = jnp.zeros(ti.shape, F32)
    for e in range(N_EXPERTS):
        off = jnp.where(ti == e, starts[e][:, 0:1], off)
    pos_ref[...] = rk_ref[...] + off.astype(I32)

    def owner_of(row):
        n_le = jnp.zeros(row.shape, I32)
        for e in range(N_EXPERTS):
            n_le = n_le + jnp.where(ends[e][:, 0:1] <= row, 1, 0)
        return jnp.minimum(n_le, N_EXPERTS - 1)

    block_start = lax.broadcasted_iota(I32, (1, MOE_NB_PAD), 1).astype(F32) * MOE_BM
    owner = owner_of(block_start)
    nxt = jnp.zeros((1, MOE_NB_PAD), I32)
    nxt_blk = jnp.zeros((1, MOE_NB_PAD), I32)
    slot = jnp.zeros((1, MOE_NB_PAD), I32)
    ordinal = jnp.zeros((1, 1), I32)
    for e in range(N_EXPERTS):
        end_e = ends[e][:, 0:1]
        nxt = jnp.where(owner == e, jnp.where(end_e < acc[:, 0:1], owner_of(end_e), -1), nxt)
        nxt_blk = jnp.where(owner == e, (end_e * (1.0 / MOE_BM)).astype(I32), nxt_blk)
        slot = jnp.where(owner == e, ordinal & 1, slot)
        ordinal = ordinal + jnp.where(padded[e:e + 1, 0:1] > 0, 1, 0)
    used = jnp.broadcast_to((acc[:, 0:1] * (1.0 / MOE_BM)).astype(I32), (1, MOE_NB_PAD))
    blk_ref[...] = jnp.concatenate(
        [owner, nxt, used, nxt_blk, slot, jnp.zeros((SUBLANES - 5, MOE_NB_PAD), I32)], axis=0)


def _route(top_i, rank, counts):
    return pl.pallas_call(
        _route_body,
        out_shape=[
            jax.ShapeDtypeStruct(top_i.shape, I32),
            jax.ShapeDtypeStruct((SUBLANES, MOE_NB_PAD), I32),
        ],
        compiler_params=pltpu.CompilerParams(vmem_limit_bytes=32 * 1024 * 1024),
        name="route",
    )(top_i, rank, counts)


def _sc_workers():
    info = plsc.get_sparse_core_info()
    return info.num_cores, info.num_subcores, info.num_lanes


def _sc_gather_loop(table_hbm, out_hbm, idx_v, base, chunks, bufs, gather_sems, write_sems):
    window = SC_GATHER_WINDOW
    ring = len(bufs)

    def fetch(c, b):
        return pltpu.make_async_copy(table_hbm.at[idx_v.at[pl.ds(c * window, window)]], bufs[b], gather_sems[b])

    def flush(c, b):
        return pltpu.make_async_copy(bufs[b], out_hbm.at[pl.ds(base + c * window, window)], write_sems[b])

    for b in range(ring):
        fetch(b, b).start()

    @pl.loop(0, chunks, step=ring)
    def _(c0):
        for b in range(ring):
            c = c0 + b
            fetch(c, b).wait()
            flush(c, b).start()

            @pl.when(c + ring < chunks)
            def _():
                flush(c, b).wait()
                fetch(c + ring, b).start()

    for b in range(ring):
        flush(chunks - ring + b, b).wait()


def _sc_source_rows(pos_flat, n_rows):
    cores, subcores, lanes = _sc_workers()
    workers = cores * subcores
    per_worker = n_rows // workers
    n_assign = pos_flat.shape[0]
    scan = SC_SCAN_CHUNK
    assert per_worker * workers == n_rows and per_worker % lanes == 0
    assert n_assign % scan == 0 and scan % lanes == 0 and TOKENS & (TOKENS - 1) == 0
    mesh = plsc.VectorSubcoreMesh(core_axis_name="core", subcore_axis_name="subcore")

    @functools.partial(
        pl.kernel,
        out_type=jax.ShapeDtypeStruct((n_rows,), I32),
        mesh=mesh,
        scratch_types=[pltpu.VMEM((per_worker,), I32), pltpu.VMEM((scan,), I32)],
        compiler_params=pltpu.CompilerParams(needs_layout_passes=False),
        name="sc_source_rows",
    )
    def invert(pos_hbm, out_hbm, src_v, pos_v):
        wid = lax.axis_index("subcore") * cores + lax.axis_index("core")
        base = wid * per_worker
        lane = lax.iota(I32, lanes)

        @pl.loop(0, per_worker, step=lanes)
        def _(j):
            src_v[pl.ds(j, lanes)] = (base + j + lane) & (TOKENS - 1)

        @pl.loop(0, n_assign, step=scan)
        def _(a0):
            pltpu.sync_copy(pos_hbm.at[pl.ds(a0, scan)], pos_v)

            @plsc.parallel_loop(0, scan, step=lanes, unroll=SC_SCAN_UNROLL)
            def _(j):
                rel = pos_v[pl.ds(j, lanes)] - base
                mine = (rel >= 0) & (rel < per_worker)
                tok = (a0 + j + lane) & (TOKENS - 1)
                plsc.store_scatter(src_v, [jnp.where(mine, rel, 0)], tok, mask=mine)

        pltpu.sync_copy(src_v, out_hbm.at[pl.ds(base, per_worker)])

    return invert(pos_flat)


def _sc_gather_rows(table, idx, ring=SC_GATHER_RING):
    cores, subcores, _ = _sc_workers()
    workers = cores * subcores
    n = idx.shape[0]
    window = SC_GATHER_WINDOW
    per_worker = n // workers
    chunks = per_worker // window
    assert per_worker * workers == n and chunks * window == per_worker and chunks % ring == 0
    row_shape = table.shape[1:]
    mesh = plsc.VectorSubcoreMesh(core_axis_name="core", subcore_axis_name="subcore")

    @functools.partial(
        pl.kernel,
        out_type=jax.ShapeDtypeStruct((n,) + row_shape, table.dtype),
        mesh=mesh,
        scratch_types=[pltpu.VMEM((per_worker,), I32)]
        + [pltpu.VMEM((window,) + row_shape, table.dtype)] * ring
        + [pltpu.SemaphoreType.DMA] * (2 * ring),
        name="sc_gather_rows",
    )
    def gather(table_hbm, idx_hbm, out_hbm, idx_v, *scratch):
        wid = lax.axis_index("subcore") * cores + lax.axis_index("core")
        base = wid * per_worker
        pltpu.sync_copy(idx_hbm.at[pl.ds(base, per_worker)], idx_v)
        _sc_gather_loop(table_hbm, out_hbm, idx_v, base, chunks,
                        scratch[:ring], scratch[ring:2 * ring], scratch[2 * ring:])

    return gather(table, idx)


def _experts_body(first, be_ref, nxt_ref, nxtblk_ref, slot_ref, nu_ref, xs_ref, bgu_ref, bd_ref, wgu_hbm, wd_hbm,
                  *rest):
    ys_ref, wgu_f32, wd_f32, sems = rest[-4:]
    step = pl.program_id(0)
    end = first + pl.num_programs(0) * MOE_SUB
    bm = MOE_BM
    n_used = nu_ref[0]

    def fetch(e, slot):
        return (pltpu.make_async_copy(wgu_hbm.at[e], wgu_f32.at[slot], sems.at[0, slot]),
                pltpu.make_async_copy(wd_hbm.at[e], wd_f32.at[slot], sems.at[1, slot]))

    @pl.when((step == 0) & (first < n_used))
    def _():
        for cp in fetch(be_ref[first], slot_ref[first]):
            cp.start()

    for sub in range(MOE_SUB):
        i = first + step * MOE_SUB + sub
        e = be_ref[i]
        slot = slot_ref[i]
        used = i < n_used
        fresh = (i == first) | (e != be_ref[jnp.maximum(i - 1, 0)])
        rows = pl.ds(sub * bm * PACK_SUB, bm * PACK_SUB)
        xs_sub = xs_ref.at[rows, :]
        ys_sub = ys_ref.at[rows, :]

        @pl.when(used & fresh)
        def _():
            for cp in fetch(e, slot):
                cp.wait()

            @pl.when((nxt_ref[i] >= 0) & (nxtblk_ref[i] < end))
            def _():
                for cp in fetch(nxt_ref[i], 1 - slot):
                    cp.start()

        @pl.when(used)
        def _():
            x = jnp.concatenate(_unpack_rows(xs_sub, bm), axis=1).astype(BF16)
            gu = jnp.dot(x, wgu_f32[slot].astype(BF16), preferred_element_type=F32) + bgu_ref[pl.ds(e, 1), :]
            gate = jnp.minimum(gu[:, 0:D_FF], SWIGLU_LIMIT)
            up = jnp.clip(gu[:, D_FF:2 * D_FF], -SWIGLU_LIMIT, SWIGLU_LIMIT)
            act = ((up + 1.0) * gate * jax.nn.sigmoid(SWIGLU_ALPHA * gate)).astype(BF16)
            y = jnp.dot(act, wd_f32[slot].astype(BF16), preferred_element_type=F32) + bd_ref[pl.ds(e, 1), :]
            _pack_rows(ys_sub, y.astype(BF16).astype(F32))

        @pl.when(jnp.logical_not(used))
        def _():
            ys_sub[...] = jnp.zeros((bm * PACK_SUB, LANES), I32)


def _experts(first, part_blocks, blocks, xs_part, w_gate_up, b_gate_up, w_down, b_down, ys_prev):
    rows = MOE_SUB * MOE_BM * PACK_SUB
    d = D_MODEL
    steps = part_blocks // MOE_SUB
    assert steps * MOE_SUB == part_blocks and first % MOE_SUB == 0

    def x_block(i, be, nx, nb, sl, nu):
        last = jnp.maximum((jnp.minimum(nu[0], first + part_blocks) - 1 - first) // MOE_SUB, 0)
        return jnp.minimum(i, last)

    in_specs = [
        pl.BlockSpec((rows, LANES), lambda i, be, nx, nb, sl, nu: (x_block(i, be, nx, nb, sl, nu), 0)),
        pl.BlockSpec(b_gate_up.shape, lambda i, be, nx, nb, sl, nu: (0, 0)),
        pl.BlockSpec(b_down.shape, lambda i, be, nx, nb, sl, nu: (0, 0)),
        pl.BlockSpec(memory_space=pl.ANY),
        pl.BlockSpec(memory_space=pl.ANY),
    ]
    operands = [blocks[0, :MOE_NB], blocks[1, :MOE_NB], blocks[3, :MOE_NB], blocks[4, :MOE_NB], blocks[2, :1],
                xs_part, b_gate_up, b_down, w_gate_up, w_down]
    aliases = {}
    if ys_prev is not None:
        in_specs.append(pl.BlockSpec(memory_space=pl.ANY))
        aliases = {len(operands): 0}
        operands.append(ys_prev)
    grid_spec = pltpu.PrefetchScalarGridSpec(
        num_scalar_prefetch=5,
        grid=(steps,),
        in_specs=in_specs,
        out_specs=pl.BlockSpec((rows, LANES), lambda i, be, nx, nb, sl, nu: (first // MOE_SUB + i, 0)),
        scratch_shapes=[
            pltpu.VMEM((2, d, 2 * D_FF), F32), pltpu.VMEM((2, D_FF, d), F32),
            pltpu.SemaphoreType.DMA((2, 2)),
        ],
    )
    return pl.pallas_call(
        functools.partial(_experts_body, first),
        grid_spec=grid_spec,
        out_shape=jax.ShapeDtypeStruct((MOE_ROWS * PACK_SUB, LANES), I32),
        input_output_aliases=aliases,
        compiler_params=_params(("arbitrary",), 48),
        name="experts",
    )(*operands)


def _combine_body(gates_ref, x1_ref, mod_ref, gpost_ref, y0_ref, y1_ref, y2_ref, y3_ref, o_ref):
    tm = TM_COMB
    gates = jnp.concatenate([gates_ref[...], jnp.zeros((SUBLANES - TOP_K, tm), F32)], axis=0).T
    y_hi = jnp.zeros((tm, PACK_COLS), F32)
    y_lo = jnp.zeros((tm, PACK_COLS), F32)
    for k, yk_ref in enumerate((y0_ref, y1_ref, y2_ref, y3_ref)):
        hi, lo = _unpack_rows(yk_ref, tm)
        y_hi = y_hi + hi * gates[:, k:k + 1]
        y_lo = y_lo + lo * gates[:, k:k + 1]
    y = jnp.concatenate([y_hi, y_lo], axis=1)
    gain = mod_ref[5:6, :] * gpost_ref[...]
    o_ref[...] = x1_ref[...] + _rms(y, gain)


def _combine(gates, x1, mod, g_post, y4):
    t, d = x1.shape
    tm = TM_COMB
    tiles = t // tm
    tiles_per_seq = SEQ // tm

    def slab(k):
        return pl.BlockSpec((tm * PACK_SUB, LANES), lambda i: (k * tiles + i, 0))

    return pl.pallas_call(
        _combine_body,
        grid=(tiles,),
        in_specs=[
            pl.BlockSpec((TOP_K, tm), lambda i: (0, i)),
            pl.BlockSpec((tm, d), lambda i: (i, 0)),
            pl.BlockSpec((None, 6, d), lambda i: (i // tiles_per_seq, 0, 0)),
            pl.BlockSpec(g_post.shape, lambda i: (0, 0)),
            slab(0), slab(1), slab(2), slab(3),
        ],
        out_specs=pl.BlockSpec((tm, d), lambda i: (i, 0)),
        out_shape=jax.ShapeDtypeStruct((t, d), F32),
        compiler_params=_params(("arbitrary",), 48),
        name="combine",
    )(gates, x1, mod, g_post, y4, y4, y4, y4)


def _rotary_tables():
    half = ROT_DIM // 2
    inv_freq = ROPE_THETA ** (-2.0 * np.arange(half, dtype=np.float32) / ROT_DIM)
    ang = np.arange(SEQ, dtype=np.float32)[:, None] * inv_freq[None, :].astype(np.float32)
    cos, sin = np.cos(ang), np.sin(ang)
    ones = np.ones((SEQ, ATT_HEAD_DIM - ROT_DIM), np.float32)
    zeros = np.zeros((SEQ, ATT_HEAD_DIM - ROT_DIM), np.float32)
    zh = np.zeros((SEQ, half), np.float32)
    reps = LANES // ATT_HEAD_DIM
    rc = np.tile(np.concatenate([cos, cos, ones], axis=1), (1, reps))
    rm = np.tile(np.concatenate([-sin, zh, zeros], axis=1), (1, reps))
    rp = np.tile(np.concatenate([zh, sin, zeros], axis=1), (1, reps))
    return tuple(jnp.asarray(t, F32) for t in (rc, rm, rp))


def _mixer_inputs(w_in, w_gk_fwd, b_gk_fwd, w_gk_bwd, b_gk_bwd):
    hk = GLA_HEADS * GLA_DK
    hv = GLA_HEADS * GLA_DV
    w = w_in[0]
    o_lr = 2 * hk + 2 * hv
    o_aq = o_lr + 2 * GLA_RANK
    o_ak = o_aq + ATT_Q_HEADS * ATT_HEAD_DIM
    o_av = o_ak + ATT_KV_HEADS * ATT_HEAD_DIM
    hd = ATT_HEAD_DIM
    wa = w[:, :o_lr].astype(BF16)
    wlr = w[:, o_lr:o_aq].astype(BF16)
    dup = lambda m: jnp.concatenate([m[:, g * hd:(g + 1) * hd] for g in range(ATT_KV_HEADS) for _ in range(2)], axis=1)
    wb = jnp.concatenate([w[:, o_aq:o_ak], dup(w[:, o_ak:o_av])], axis=1).astype(BF16)
    wvt = dup(w[:, o_av:o_av + ATT_KV_HEADS * hd]).T.astype(BF16)
    zr = jnp.zeros((GLA_RANK, hk), F32)
    wgk = jnp.concatenate([jnp.concatenate([w_gk_fwd[0], zr], axis=1),
                           jnp.concatenate([zr, w_gk_bwd[0]], axis=1)], axis=0).astype(BF16)
    bgk = jnp.concatenate([b_gk_fwd[0], b_gk_bwd[0]])[None, :]
    return (wa, wlr, wgk, bgk, wb, wvt) + _rotary_tables()


def kernel(x, c, w_ada, b_ada, g_pre_mix, g_post_mix, w_in, w_gk_fwd, b_gk_fwd, w_gk_bwd, b_gk_bwd, g_gla_out,
           attn_sink, w_out, g_pre_ffn, g_post_ffn, w_router, b_router, w_gate_up, b_gate_up, w_down, b_down):
    assert x.shape == (BATCH, SEQ, D_MODEL) and w_ada.shape[0] == 1
    d = D_MODEL
    x2 = x.reshape(TOKENS, d)

    c_pad = jnp.pad(c, ((0, SUBLANES - BATCH), (0, 0)))
    mod = _ada(c_pad, w_ada[0], b_ada)[:BATCH].reshape(BATCH, 6, d)

    mixer_in = _mixer_inputs(w_in, w_gk_fwd, b_gk_fwd, w_gk_bwd, b_gk_bwd)
    q, k, v, gg, laf, lab, aq, ak2, avt = _inproj(x2, mod, g_pre_mix, *mixer_in)
    o_f, o_b = _gla(q, k, v, laf, lab)
    o_att = _attn(attn_sink[0], aq, ak2, avt)

    wr_t = w_router[0].T
    wrh = wr_t.astype(BF16)
    wrl = (wr_t - wrh.astype(F32)).astype(BF16)
    x1, h2_tiles, top_i, gates, rank, counts = _post(
        o_f, o_b, gg, o_att, x2, mod, g_gla_out, g_post_mix, g_pre_ffn, w_out[0].astype(BF16), wrh, wrl,
        b_router[0][:, None])

    pos, blocks = _route(top_i, rank, counts)

    src = _sc_source_rows(pos.reshape(TOP_K * TOKENS), MOE_ROWS)
    h2_rows = h2_tiles.reshape(TOKENS, PACK_SUB, LANES)
    ys = None
    first = 0
    for part_blocks, ring in zip(MOE_PART_BLOCKS, MOE_PART_RINGS):
        row0, n_rows = first * MOE_BM, part_blocks * MOE_BM
        xs_p = _sc_gather_rows(h2_rows, src[row0:row0 + n_rows], ring)
        ys = _experts(first, part_blocks, blocks, xs_p.reshape(n_rows * PACK_SUB, LANES),
                      w_gate_up[0], b_gate_up[0], w_down[0], b_down[0], ys)
        first += part_blocks
    assert first == MOE_NB

    y4 = _sc_gather_rows(ys.reshape(MOE_ROWS, PACK_SUB, LANES), pos.reshape(TOP_K * TOKENS))
    out = _combine(gates, x1, mod, g_post_ffn, y4.reshape(TOP_K * TOKENS * PACK_SUB, LANES))
    return out.reshape(BATCH, SEQ, d)
```

```python
import functools

import jax
import jax.numpy as jnp
import numpy as np
from jax import lax
from jax.experimental import pallas as pl
from jax.experimental.pallas import tpu as pltpu
from jax.experimental.pallas import tpu_sc as plsc

F32 = jnp.float32
BF16 = jnp.bfloat16
I32 = jnp.int32

D_MODEL = 1024
BATCH = 2
SEQ = 8192
TOKENS = BATCH * SEQ
GLA_HEADS = 4
GLA_DV = 128
GLA_DK = 64
GLA_RANK = 16
GLA_GATE_NORMALIZER = 16.0
GLA_CHUNK = 64
ATT_Q_HEADS = 8
ATT_KV_HEADS = 2
ATT_HEAD_DIM = 64
ATT_WINDOW = 128
ATT_BLOCK = 128
ROT_DIM = 16
ROPE_THETA = 500000.0
N_EXPERTS = 32
TOP_K = 4
D_FF = 1024
SWIGLU_LIMIT = 7.0
SWIGLU_ALPHA = 1.702
NORM_EPS = 1e-6
NEG_INF = -1e30
LOG2_E = 1.4426950408889634

LANES = 128
SUBLANES = 8
PACK_COLS = D_MODEL // 2
PACK_SUB = PACK_COLS // LANES

TM_IN = 1024
IN_SUB = 4
GLA_GROUP = 16
ATT_GROUP = 16
TM_POST = 1024
POST_SUB = 4
MOE_BM = 256
MOE_ROWS = TOKENS * TOP_K + N_EXPERTS * MOE_BM
MOE_NB = MOE_ROWS // MOE_BM
MOE_SUB = 4
MOE_PART_BLOCKS = (32, 128, 128)
MOE_NB_PAD = ((MOE_NB + LANES - 1) // LANES) * LANES
SC_SCAN_CHUNK = 4096
SC_SCAN_UNROLL = 8
TM_COMB = 1024
SC_GATHER_WINDOW = 32
SC_GATHER_RING = 4
SC_BACKGROUND_RING = 1

NT_DIMS = (((1,), (1,)), ((), ()))
TN_DIMS = (((0,), (0,)), ((), ()))


def _params(semantics, vmem_mib):
    return pltpu.CompilerParams(dimension_semantics=semantics, vmem_limit_bytes=vmem_mib * 1024 * 1024)


def _rms(x, g):
    return x * lax.rsqrt(jnp.mean(x * x, axis=-1, keepdims=True) + NORM_EPS) * g


def _silu(x):
    return x * jax.nn.sigmoid(x)


def _pack_rows(ref, v):
    m = v.shape[0]
    bits = lax.bitcast_convert_type(v, jnp.uint32)
    word = lax.bitcast_convert_type(bits[:, :PACK_COLS] | (bits[:, PACK_COLS:] >> 16), I32)
    for s in range(PACK_SUB):
        ref[pl.ds(s, m, stride=PACK_SUB), :] = word[:, s * LANES:(s + 1) * LANES]


def _unpack_rows(ref, m):
    word = jnp.concatenate([ref[pl.ds(s, m, stride=PACK_SUB), :] for s in range(PACK_SUB)], axis=1)
    bits = lax.bitcast_convert_type(word, jnp.uint32)
    hi = lax.bitcast_convert_type(bits & jnp.uint32(0xFFFF0000), F32)
    lo = lax.bitcast_convert_type(bits << 16, F32)
    return hi, lo


def _ada_body(c_ref, w_ref, b_ref, o_ref):
    ca = _silu(c_ref[...]).astype(BF16)
    o_ref[...] = jnp.dot(ca, w_ref[...].astype(BF16), preferred_element_type=F32) + b_ref[...]


def _ada(c_pad, w_ada, b_ada):
    d = D_MODEL
    return pl.pallas_call(
        _ada_body,
        grid=(6,),
        in_specs=[
            pl.BlockSpec((SUBLANES, d), lambda j: (0, 0)),
            pl.BlockSpec((d, d), lambda j: (0, j)),
            pl.BlockSpec((1, d), lambda j: (0, j)),
        ],
        out_specs=pl.BlockSpec((SUBLANES, d), lambda j: (0, j)),
        out_shape=jax.ShapeDtypeStruct((SUBLANES, 6 * d), F32),
        compiler_params=_params(("arbitrary",), 32),
        name="ada",
    )(c_pad, w_ada, b_ada)


def _rotary(x, cos_t, msin_t, psin_t):
    width = x.shape[1]
    reps = width // LANES
    c = jnp.concatenate([cos_t] * reps, axis=1)
    m = jnp.concatenate([msin_t] * reps, axis=1)
    p = jnp.concatenate([psin_t] * reps, axis=1)
    half = ROT_DIM // 2
    return x * c + pltpu.roll(x, width - half, 1) * m + pltpu.roll(x, half, 1) * p


def _inproj_body(x_ref, mod_ref, g_ref, wa_ref, wlr_ref, wgk_ref, bgk_ref, wb_ref, wvt_ref, rc_ref, rm_ref, rp_ref,
                 q_ref, k_ref, v_ref, gg_ref, laf_ref, lab_ref, aq_ref, ak_ref, avt_ref):
    shift = mod_ref[0:1, :]
    scale = mod_ref[1:2, :]
    hk = GLA_HEADS * GLA_DK
    hv = GLA_HEADS * GLA_DV
    nq = ATT_Q_HEADS * ATT_HEAD_DIM
    nk = 2 * ATT_KV_HEADS * ATT_HEAD_DIM
    sub = x_ref.shape[0] // IN_SUB
    subs = [slice(s * sub, (s + 1) * sub) for s in range(IN_SUB)]

    gain = g_ref[...] * (1.0 + scale)

    def hidden(rows):
        return (_rms(x_ref[rows, :], gain) + shift).astype(BF16)

    def project(h):
        return (jnp.dot(h, wa_ref[...], preferred_element_type=F32),
                jnp.dot(h, wlr_ref[...], preferred_element_type=F32),
                jnp.dot(h, wb_ref[...], preferred_element_type=F32),
                lax.dot_general(wvt_ref[...], h, NT_DIMS, preferred_element_type=F32))

    def finish(rows, pa, plr, pb, pvt):
        q_ref[rows, :] = pa[:, 0:hk] * (GLA_DK ** -0.5)
        k_ref[rows, :] = pa[:, hk:2 * hk]
        v_ref[rows, :] = pa[:, 2 * hk:2 * hk + hv].astype(BF16)
        gg_ref[rows, :] = pa[:, 2 * hk + hv:2 * hk + 2 * hv]
        gk = jnp.dot(plr.astype(BF16), wgk_ref[...], preferred_element_type=F32) + bgk_ref[...]
        la = (jnp.minimum(gk, 0.0) - jnp.log1p(jnp.exp(-jnp.abs(gk)))) * (1.0 / GLA_GATE_NORMALIZER)
        laf_ref[rows, :] = la[:, 0:hk]
        lab_ref[rows, :] = la[:, hk:2 * hk]
        rc, rm, rp = rc_ref[rows, :], rm_ref[rows, :], rp_ref[rows, :]
        aq_ref[rows, :] = (_rotary(pb[:, 0:nq], rc, rm, rp) * (ATT_HEAD_DIM ** -0.5 * LOG2_E)).astype(BF16)
        ak_ref[rows, :] = _rotary(pb[:, nq:nq + nk], rc, rm, rp).astype(BF16)
        avt_ref[:, rows] = pvt.astype(BF16)

    hs = [hidden(rows) for rows in subs]
    ps = [project(h) for h in hs]
    for rows, p in zip(subs, ps):
        finish(rows, *p)


def _inproj(x2, mod, g_pre, wa, wlr, wgk, bgk, wb, wvt, rc, rm, rp):
    t, d = x2.shape
    tm = TM_IN
    tiles_per_seq = SEQ // tm
    hk = GLA_HEADS * GLA_DK
    hv = GLA_HEADS * GLA_DV
    nq = ATT_Q_HEADS * ATT_HEAD_DIM
    nk = 2 * ATT_KV_HEADS * ATT_HEAD_DIM

    def full(a):
        return pl.BlockSpec(a.shape, lambda i: (0,) * a.ndim)

    def rows(w):
        return pl.BlockSpec((tm, w), lambda i: (i, 0))

    def table():
        return pl.BlockSpec((tm, LANES), lambda i: (i % tiles_per_seq, 0))

    out_widths = [(hk, F32), (hk, F32), (hv, BF16), (hv, F32), (hk, F32), (hk, F32), (nq, BF16), (nk, BF16)]
    return pl.pallas_call(
        _inproj_body,
        grid=(t // tm,),
        in_specs=[
            rows(d),
            pl.BlockSpec((None, 6, d), lambda i: (i // tiles_per_seq, 0, 0)),
            full(g_pre), full(wa), full(wlr), full(wgk), full(bgk), full(wb), full(wvt),
            table(), table(), table(),
        ],
        out_specs=[rows(w) for w, _ in out_widths] + [pl.BlockSpec((nk, tm), lambda i: (0, i))],
        out_shape=[jax.ShapeDtypeStruct((t, w), dt) for w, dt in out_widths] + [jax.ShapeDtypeStruct((nk, t), BF16)],
        compiler_params=_params(("arbitrary",), 56),
        name="inproj",
    )(x2, mod, g_pre, wa, wlr, wgk, bgk, wb, wvt, rc, rm, rp)


def _gla_body(qf_ref, kf_ref, vf_ref, laf_ref, qb_ref, kb_ref, vb_ref, lab_ref, of_ref, ob_ref, sf_ref, sb_ref):
    @pl.when(pl.program_id(1) == 0)
    def _():
        sf_ref[...] = jnp.zeros_like(sf_ref)
        sb_ref[...] = jnp.zeros_like(sb_ref)

    c = GLA_CHUNK
    r_i = lax.broadcasted_iota(I32, (c, c), 0)
    c_i = lax.broadcasted_iota(I32, (c, c), 1)
    lower = c_i <= r_i
    upper = c_i >= r_i
    cum_f = jnp.where(lower, 1.0, 0.0).astype(BF16)
    cum_b = jnp.where(upper, 1.0, 0.0).astype(BF16)
    lane = lax.broadcasted_iota(I32, (1, LANES), 1)
    head_masks = (lane < GLA_DK, lane >= GLA_DK)

    fwd = [(qf_ref, kf_ref, laf_ref, vf_ref, of_ref, slice(g * c, (g + 1) * c), cum_f, lower, c - 1, c // 2 - 1)
           for g in range(GLA_GROUP)]
    bwd = [(qb_ref, kb_ref, lab_ref, vb_ref, ob_ref, slice(g * c, (g + 1) * c), cum_b, upper, 0, c // 2)
           for g in reversed(range(GLA_GROUP))]
    heads = range(GLA_HEADS)
    pair = [slice((h // 2) * LANES, (h // 2 + 1) * LANES) for h in heads]
    vcols = [slice(h * GLA_DV, (h + 1) * GLA_DV) for h in heads]

    def stage1(item):
        q_ref, k_ref, la_ref, v_ref, o_ref, rows, cum, tri, i_last, i_mid = item
        la = la_ref[rows, :]
        hi = la.astype(BF16)
        lo = (la - hi.astype(F32)).astype(BF16)
        b = jnp.dot(cum, hi, preferred_element_type=F32) + jnp.dot(cum, lo, preferred_element_type=F32)
        b_last = b[i_last:i_last + 1, :]
        b_mid = b[i_mid:i_mid + 1, :]
        q, k = q_ref[rows, :], k_ref[rows, :]
        return (q * jnp.exp(b - b_mid), (k * jnp.exp(b_mid - b)).astype(BF16), q * jnp.exp(b),
                (k * jnp.exp(b_last - b)).astype(BF16), jnp.exp(b_last))

    def stage2(item, pre):
        v_ref, rows, tri = item[3], item[5], item[7]
        qs, ks, qi, kst, decay = pre
        out = []
        for h in heads:
            mask = head_masks[h % 2]
            qs_h = jnp.where(mask, qs[:, pair[h]], 0.0).astype(BF16)
            sc = lax.dot_general(qs_h, ks[:, pair[h]], NT_DIMS, preferred_element_type=F32)
            v_h = v_ref[rows, vcols[h]]
            kv = lax.dot_general(v_h, kst[:, pair[h]], TN_DIMS, preferred_element_type=F32)
            out.append((jnp.where(tri, sc, 0.0).astype(BF16), kv,
                        jnp.where(mask, qi[:, pair[h]], 0.0).astype(BF16), v_h))
        return out

    def run(items, s_ref):
        pre = [stage1(it) for it in items]
        mid = [stage2(it, p) for it, p in zip(items, pre)]
        states = [s_ref[h] for h in heads]
        for it, p, m in zip(items, pre, mid):
            o_ref, rows, decay = it[4], it[5], p[4]
            for h in heads:
                sc, kv, qi_h, v_h = m[h]
                o = jnp.dot(sc, v_h, preferred_element_type=F32)
                o = o + lax.dot_general(qi_h, states[h].astype(BF16), NT_DIMS, preferred_element_type=F32)
                o_ref[rows, vcols[h]] = o.astype(o_ref.dtype)
                states[h] = states[h] * decay[:, pair[h]] + kv
        for h in heads:
            s_ref[h] = states[h]

    run(fwd, sf_ref)
    run(bwd, sb_ref)


def _gla(q, k, v, laf, lab):
    t = q.shape[0]
    rows = GLA_GROUP * GLA_CHUNK
    ng = SEQ // rows
    hk = GLA_HEADS * GLA_DK
    hv = GLA_HEADS * GLA_DV

    def fwd(w):
        return pl.BlockSpec((rows, w), lambda b, n: (b * ng + n, 0))

    def bwd(w):
        return pl.BlockSpec((rows, w), lambda b, n: (b * ng + ng - 1 - n, 0))

    return pl.pallas_call(
        _gla_body,
        grid=(BATCH, ng),
        in_specs=[fwd(hk), fwd(hk), fwd(hv), fwd(hk), bwd(hk), bwd(hk), bwd(hv), bwd(hk)],
        out_specs=[fwd(hv), bwd(hv)],
        out_shape=[jax.ShapeDtypeStruct((t, hv), BF16)] * 2,
        scratch_shapes=[pltpu.VMEM((GLA_HEADS, GLA_DV, 2 * GLA_DK), F32)] * 2,
        compiler_params=_params(("arbitrary", "arbitrary"), 32),
        name="gla",
    )(q, k, v, laf, q, k, v, lab)


def _attn_body(sink_ref, q_ref, kp_ref, kc_ref, kn_ref, vp_ref, vc_ref, vn_ref, o_ref):
    step = pl.program_id(1)
    last = pl.num_programs(1) - 1
    qb = ATT_BLOCK
    hd = ATT_HEAD_DIM
    k_all = jnp.concatenate([kp_ref[...], kc_ref[...], kn_ref[...]], axis=0)
    vt_all = jnp.concatenate([vp_ref[...], vc_ref[...], vn_ref[...]], axis=1)
    lane = lax.broadcasted_iota(I32, (1, LANES), 1)
    lo = lane < hd
    j_k = lax.broadcasted_iota(I32, (3 * qb, qb), 0)
    i_q = lax.broadcasted_iota(I32, (3 * qb, qb), 1)
    band = jnp.abs(j_k - qb - i_q) <= ATT_WINDOW
    sinks = [jnp.concatenate([jnp.full((1, qb), sink_ref[4 * g + r] * LOG2_E, F32) for r in range(4)], axis=1)
             for g in range(ATT_KV_HEADS)]
    ones_rows = jnp.ones((SUBLANES, 3 * qb), BF16)
    work = [(j, g) for j in range(ATT_GROUP) for g in range(ATT_KV_HEADS)]

    def scores(j, g):
        valid = band
        if j == 0:
            valid = valid & ((j_k >= qb) | (step > 0))
        if j == ATT_GROUP - 1:
            valid = valid & ((j_k < 2 * qb) | (step < last))
        valid4 = jnp.concatenate([valid] * 4, axis=1)
        rows = slice(j * qb, (j + 1) * qb)
        kg = k_all[j * qb:(j + 3) * qb, g * LANES:(g + 1) * LANES]
        qa = q_ref[rows, (2 * g) * LANES:(2 * g + 1) * LANES]
        qc = q_ref[rows, (2 * g + 1) * LANES:(2 * g + 2) * LANES]
        zero = jnp.zeros_like(qa)
        lhs = jnp.concatenate([jnp.where(lo, qa, zero), jnp.where(lo, zero, qa),
                               jnp.where(lo, qc, zero), jnp.where(lo, zero, qc)], axis=0)
        st = lax.dot_general(kg, lhs, NT_DIMS, preferred_element_type=F32)
        return jnp.concatenate([jnp.where(valid4[0:qb], st[0:qb], NEG_INF), st[qb:2 * qb],
                                jnp.where(valid4[2 * qb:3 * qb], st[2 * qb:3 * qb], NEG_INF)], axis=0)

    def softmax(st, g):
        sink = sinks[g]
        m = jnp.maximum(jnp.max(st, axis=0, keepdims=True), sink)
        return jnp.exp2(st - m).astype(BF16), jnp.exp2(sink - m)

    def output(j, g, p, p_sink):
        rows = slice(j * qb, (j + 1) * qb)
        vgt = vt_all[g * LANES:(g + 1) * LANES, j * qb:(j + 3) * qb]
        res = jnp.dot(jnp.concatenate([vgt, ones_rows], axis=0), p, preferred_element_type=F32)
        ot = res[0:LANES] * (1.0 / (res[LANES:LANES + 1] + p_sink))
        pair_a = jnp.concatenate([ot[0:hd, 0:qb], ot[hd:2 * hd, qb:2 * qb]], axis=0)
        pair_c = jnp.concatenate([ot[0:hd, 2 * qb:3 * qb], ot[hd:2 * hd, 3 * qb:4 * qb]], axis=0)
        o_ref[rows, (2 * g) * LANES:(2 * g + 1) * LANES] = pair_a.T.astype(o_ref.dtype)
        o_ref[rows, (2 * g + 1) * LANES:(2 * g + 2) * LANES] = pair_c.T.astype(o_ref.dtype)

    s_all = [scores(j, g) for j, g in work]
    p_all = [softmax(st, g) for st, (j, g) in zip(s_all, work)]
    for (j, g), (p, p_sink) in zip(work, p_all):
        output(j, g, p, p_sink)


def _attn(sink, aq, ak2, avt):
    t = aq.shape[0]
    qb = ATT_BLOCK
    nb = SEQ // qb
    steps = nb // ATT_GROUP
    nq = ATT_Q_HEADS * ATT_HEAD_DIM
    nk = 2 * ATT_KV_HEADS * ATT_HEAD_DIM

    def edge_block(b, n, shift):
        return b * nb + jnp.clip(n * ATT_GROUP + shift, 0, nb - 1)

    def k_edge(shift):
        return pl.BlockSpec((qb, nk), lambda b, n: (edge_block(b, n, shift), 0))

    def v_edge(shift):
        return pl.BlockSpec((nk, qb), lambda b, n: (0, edge_block(b, n, shift)))

    def group(w):
        return pl.BlockSpec((ATT_GROUP * qb, w), lambda b, n: (b * steps + n, 0))

    v_group = pl.BlockSpec((nk, ATT_GROUP * qb), lambda b, n: (0, b * steps + n))
    return pl.pallas_call(
        _attn_body,
        grid=(BATCH, steps),
        in_specs=[
            pl.BlockSpec(memory_space=pltpu.SMEM),
            group(nq),
            k_edge(-1), group(nk), k_edge(ATT_GROUP), v_edge(-1), v_group, v_edge(ATT_GROUP),
        ],
        out_specs=group(nq),
        out_shape=jax.ShapeDtypeStruct((t, nq), BF16),
        compiler_params=_params(("arbitrary", "arbitrary"), 48),
        name="attn",
    )(sink, aq, ak2, ak2, ak2, avt, avt, avt)


def _post_body(of_ref, ob_ref, gg_ref, oa_ref, x_ref, mod_ref, ggla_ref, gpm_ref, gpf_ref, wout_ref,
               wrh_ref, wrl_ref, br_ref,
               x1_ref, h2_ref, ti_ref, gt_ref, rk_ref, cnt_ref, base_ref):
    tm = TM_POST

    @pl.when(pl.program_id(0) == 0)
    def _():
        base_ref[...] = jnp.zeros_like(base_ref)

    gain1 = mod_ref[2:3, :] * gpm_ref[...]
    shift2 = mod_ref[3:4, :]
    gain2 = gpf_ref[...] * (1.0 + mod_ref[4:5, :])
    sub = tm // POST_SUB
    subs = [slice(s * sub, (s + 1) * sub) for s in range(POST_SUB)]

    def mixer_out(rows):
        og = of_ref[rows, :].astype(F32) + ob_ref[rows, :].astype(F32)
        gg = gg_ref[rows, :]
        parts = []
        for h in range(GLA_HEADS):
            cols = slice(h * GLA_DV, (h + 1) * GLA_DV)
            parts.append((_rms(og[:, cols], ggla_ref[...]) * _silu(gg[:, cols])).astype(BF16))
        return jnp.concatenate(parts + [oa_ref[rows, :]], axis=1)

    def ffn_in(s, y):
        rows = subs[s]
        x1 = x_ref[rows, :] + _rms(y, gain1)
        x1_ref[rows, :] = x1
        h2 = _rms(x1, gain2) + shift2
        hi = h2.astype(BF16)
        hi32 = hi.astype(F32)
        _pack_rows(h2_ref.at[pl.ds(s * sub * PACK_SUB, sub * PACK_SUB), :], hi32)
        return hi, (h2 - hi32).astype(BF16)

    o_subs = [mixer_out(rows) for rows in subs]
    y_subs = [jnp.dot(o, wout_ref[...], preferred_element_type=F32) for o in o_subs]
    split = [ffn_in(s, y) for s, y in enumerate(y_subs)]
    h2_hi = jnp.concatenate([hi for hi, _ in split], axis=0)
    h2_lo = jnp.concatenate([lo for _, lo in split], axis=0)

    wrh = wrh_ref[...]
    logits = (lax.dot_general(wrh, h2_hi, NT_DIMS, preferred_element_type=F32)
              + lax.dot_general(wrh, h2_lo, NT_DIMS, preferred_element_type=F32)
              + lax.dot_general(wrl_ref[...], h2_hi, NT_DIMS, preferred_element_type=F32)
              + br_ref[...])
    e_iota = lax.broadcasted_iota(I32, (N_EXPERTS, tm), 0)
    idxs, vals = [], []
    work = logits
    for _ in range(TOP_K):
        m = jnp.max(work, axis=0, keepdims=True)
        idx = jnp.min(jnp.where(work == m, e_iota, N_EXPERTS), axis=0, keepdims=True)
        idxs.append(idx)
        vals.append(m)
        work = jnp.where(e_iota == idx, -jnp.inf, work)
    exps = [jnp.exp(v - vals[0]) for v in vals]
    inv = 1.0 / (exps[0] + exps[1] + exps[2] + exps[3])
    gt_ref[...] = jnp.concatenate([e * inv for e in exps], axis=0)
    ti_ref[...] = jnp.concatenate(idxs, axis=0)

    onehots = [e_iota == idx for idx in idxs]
    member = jnp.where(onehots[0] | onehots[1] | onehots[2] | onehots[3], 1.0, 0.0)
    t_row = lax.broadcasted_iota(I32, (tm, tm), 0)
    t_col = lax.broadcasted_iota(I32, (tm, tm), 1)
    strict = jnp.where(t_row < t_col, 1.0, 0.0).astype(BF16)
    before = base_ref[...] + jnp.dot(member.astype(BF16), strict, preferred_element_type=F32)
    rk_ref[...] = jnp.concatenate(
        [jnp.sum(jnp.where(oh, before, 0.0), axis=0, keepdims=True) for oh in onehots], axis=0).astype(I32)
    new_base = base_ref[...] + jnp.sum(member, axis=1, keepdims=True)
    base_ref[...] = new_base
    cnt_ref[...] = jnp.broadcast_to(new_base, cnt_ref.shape)


def _post(o_f, o_b, gg, o_att, x2, mod, g_gla, g_pm, g_pf, wout, wrh, wrl, br):
    t, d = x2.shape
    tm = TM_POST
    tiles_per_seq = SEQ // tm
    hv = GLA_HEADS * GLA_DV

    def full(a):
        return pl.BlockSpec(a.shape, lambda i: (0,) * a.ndim)

    def rows(w):
        return pl.BlockSpec((tm, w), lambda i: (i, 0))

    def lanes():
        return pl.BlockSpec((TOP_K, tm), lambda i: (0, i))

    return pl.pallas_call(
        _post_body,
        grid=(t // tm,),
        in_specs=[
            rows(hv), rows(hv), rows(hv), rows(hv), rows(d),
            pl.BlockSpec((None, 6, d), lambda i: (i // tiles_per_seq, 0, 0)),
            full(g_gla), full(g_pm), full(g_pf), full(wout), full(wrh), full(wrl), full(br),
        ],
        out_specs=[
            rows(d),
            pl.BlockSpec((tm * PACK_SUB, LANES), lambda i: (i, 0)),
            lanes(), lanes(), lanes(),
            pl.BlockSpec((N_EXPERTS, LANES), lambda i: (0, 0)),
        ],
        out_shape=[
            jax.ShapeDtypeStruct((t, d), F32),
            jax.ShapeDtypeStruct((t * PACK_SUB, LANES), I32),
            jax.ShapeDtypeStruct((TOP_K, t), I32),
            jax.ShapeDtypeStruct((TOP_K, t), F32),
            jax.ShapeDtypeStruct((TOP_K, t), I32),
            jax.ShapeDtypeStruct((N_EXPERTS, LANES), F32),
        ],
        scratch_shapes=[pltpu.VMEM((N_EXPERTS, 1), F32)],
        compiler_params=_params(("arbitrary",), 48),
        name="post",
    )(o_f, o_b, gg, o_att, x2, mod, g_gla, g_pm, g_pf, wout, wrh, wrl, br)


def _route_body(ti_ref, rk_ref, cnt_ref, pos_ref, blk_ref):
    cnt = cnt_ref[...]
    padded = jnp.floor((cnt + (MOE_BM - 1)) * (1.0 / MOE_BM)) * MOE_BM
    starts, ends = [], []
    acc = jnp.zeros((1, LANES), F32)
    for e in range(N_EXPERTS):
        starts.append(acc)
        acc = acc + padded[e:e + 1, :]
        ends.append(acc)
    ti = ti_ref[...]
    off = jnp.zeros(ti.shape, F32)
    for e in range(N_EXPERTS):
        off = jnp.where(ti == e, starts[e][:, 0:1], off)
    pos_ref[...] = rk_ref[...] + off.astype(I32)

    def owner_of(row):
        n_le = jnp.zeros(row.shape, I32)
        for e in range(N_EXPERTS):
            n_le = n_le + jnp.where(ends[e][:, 0:1] <= row, 1, 0)
        return jnp.minimum(n_le, N_EXPERTS - 1)

    block_start = lax.broadcasted_iota(I32, (1, MOE_NB_PAD), 1).astype(F32) * MOE_BM
    owner = owner_of(block_start)
    nxt = jnp.zeros((1, MOE_NB_PAD), I32)
    nxt_blk = jnp.zeros((1, MOE_NB_PAD), I32)
    slot = jnp.zeros((1, MOE_NB_PAD), I32)
    ordinal = jnp.zeros((1, 1), I32)
    for e in range(N_EXPERTS):
        end_e = ends[e][:, 0:1]
        nxt = jnp.where(owner == e, jnp.where(end_e < acc[:, 0:1], owner_of(end_e), -1), nxt)
        nxt_blk = jnp.where(owner == e, (end_e * (1.0 / MOE_BM)).astype(I32), nxt_blk)
        slot = jnp.where(owner == e, ordinal & 1, slot)
        ordinal = ordinal + jnp.where(padded[e:e + 1, 0:1] > 0, 1, 0)
    used = jnp.broadcast_to((acc[:, 0:1] * (1.0 / MOE_BM)).astype(I32), (1, MOE_NB_PAD))
    blk_ref[...] = jnp.concatenate(
        [owner, nxt, used, nxt_blk, slot, jnp.zeros((SUBLANES - 5, MOE_NB_PAD), I32)], axis=0)


def _route(top_i, rank, counts):
    return pl.pallas_call(
        _route_body,
        out_shape=[
            jax.ShapeDtypeStruct(top_i.shape, I32),
            jax.ShapeDtypeStruct((SUBLANES, MOE_NB_PAD), I32),
        ],
        compiler_params=pltpu.CompilerParams(vmem_limit_bytes=32 * 1024 * 1024),
        name="route",
    )(top_i, rank, counts)


def _sc_workers():
    info = plsc.get_sparse_core_info()
    return info.num_cores, info.num_subcores, info.num_lanes


def _sc_gather_loop(table_hbm, out_hbm, idx_v, base, chunks, bufs, gather_sems, write_sems):
    window = SC_GATHER_WINDOW
    ring = len(bufs)

    def fetch(c, b):
        return pltpu.make_async_copy(table_hbm.at[idx_v.at[pl.ds(c * window, window)]], bufs[b], gather_sems[b])

    def flush(c, b):
        return pltpu.make_async_copy(bufs[b], out_hbm.at[pl.ds(base + c * window, window)], write_sems[b])

    for b in range(ring):
        fetch(b, b).start()

    @pl.loop(0, chunks, step=ring)
    def _(c0):
        for b in range(ring):
            c = c0 + b
            fetch(c, b).wait()
            flush(c, b).start()

            @pl.when(c + ring < chunks)
            def _():
                flush(c, b).wait()
                fetch(c + ring, b).start()

    for b in range(ring):
        flush(chunks - ring + b, b).wait()


def _sc_source_rows(pos_flat, n_rows):
    cores, subcores, lanes = _sc_workers()
    workers = cores * subcores
    per_worker = n_rows // workers
    n_assign = pos_flat.shape[0]
    scan = SC_SCAN_CHUNK
    assert per_worker * workers == n_rows and per_worker % lanes == 0
    assert n_assign % scan == 0 and scan % lanes == 0 and TOKENS & (TOKENS - 1) == 0
    mesh = plsc.VectorSubcoreMesh(core_axis_name="core", subcore_axis_name="subcore")

    @functools.partial(
        pl.kernel,
        out_type=jax.ShapeDtypeStruct((n_rows,), I32),
        mesh=mesh,
        scratch_types=[pltpu.VMEM((per_worker,), I32), pltpu.VMEM((scan,), I32)],
        compiler_params=pltpu.CompilerParams(needs_layout_passes=False),
        name="sc_source_rows",
    )
    def invert(pos_hbm, out_hbm, src_v, pos_v):
        wid = lax.axis_index("subcore") * cores + lax.axis_index("core")
        base = wid * per_worker
        lane = lax.iota(I32, lanes)

        @pl.loop(0, per_worker, step=lanes)
        def _(j):
            src_v[pl.ds(j, lanes)] = (base + j + lane) & (TOKENS - 1)

        @pl.loop(0, n_assign, step=scan)
        def _(a0):
            pltpu.sync_copy(pos_hbm.at[pl.ds(a0, scan)], pos_v)

            @plsc.parallel_loop(0, scan, step=lanes, unroll=SC_SCAN_UNROLL)
            def _(j):
                rel = pos_v[pl.ds(j, lanes)] - base
                mine = (rel >= 0) & (rel < per_worker)
                tok = (a0 + j + lane) & (TOKENS - 1)
                plsc.store_scatter(src_v, [jnp.where(mine, rel, 0)], tok, mask=mine)

        pltpu.sync_copy(src_v, out_hbm.at[pl.ds(base, per_worker)])

    return invert(pos_flat)


def _sc_gather_rows(table, idx, ring=SC_GATHER_RING):
    cores, subcores, _ = _sc_workers()
    workers = cores * subcores
    n = idx.shape[0]
    window = SC_GATHER_WINDOW
    per_worker = n // workers
    chunks = per_worker // window
    assert per_worker * workers == n and chunks * window == per_worker and chunks % ring == 0
    row_shape = table.shape[1:]
    mesh = plsc.VectorSubcoreMesh(core_axis_name="core", subcore_axis_name="subcore")

    @functools.partial(
        pl.kernel,
        out_type=jax.ShapeDtypeStruct((n,) + row_shape, table.dtype),
        mesh=mesh,
        scratch_types=[pltpu.VMEM((per_worker,), I32)]
        + [pltpu.VMEM((window,) + row_shape, table.dtype)] * ring
        + [pltpu.SemaphoreType.DMA] * (2 * ring),
        name="sc_gather_rows",
    )
    def gather(table_hbm, idx_hbm, out_hbm, idx_v, *scratch):
        wid = lax.axis_index("subcore") * cores + lax.axis_index("core")
        base = wid * per_worker
        pltpu.sync_copy(idx_hbm.at[pl.ds(base, per_worker)], idx_v)
        _sc_gather_loop(table_hbm, out_hbm, idx_v, base, chunks,
                        scratch[:ring], scratch[ring:2 * ring], scratch[2 * ring:])

    return gather(table, idx)


def _experts_body(first, be_ref, nxt_ref, nxtblk_ref, slot_ref, nu_ref, xs_ref, bgu_ref, bd_ref, wgu_hbm, wd_hbm,
                  *rest):
    ys_ref, wgu_f32, wd_f32, sems = rest[-4:]
    step = pl.program_id(0)
    end = first + pl.num_programs(0) * MOE_SUB
    bm = MOE_BM
    n_used = nu_ref[0]

    def fetch(e, slot):
        return (pltpu.make_async_copy(wgu_hbm.at[e], wgu_f32.at[slot], sems.at[0, slot]),
                pltpu.make_async_copy(wd_hbm.at[e], wd_f32.at[slot], sems.at[1, slot]))

    @pl.when((step == 0) & (first < n_used))
    def _():
        for cp in fetch(be_ref[first], slot_ref[first]):
            cp.start()

    for sub in range(MOE_SUB):
        i = first + step * MOE_SUB + sub
        e = be_ref[i]
        slot = slot_ref[i]
        used = i < n_used
        fresh = (i == first) | (e != be_ref[jnp.maximum(i - 1, 0)])
        rows = pl.ds(sub * bm * PACK_SUB, bm * PACK_SUB)
        xs_sub = xs_ref.at[rows, :]
        ys_sub = ys_ref.at[rows, :]

        @pl.when(used & fresh)
        def _():
            for cp in fetch(e, slot):
                cp.wait()

            @pl.when((nxt_ref[i] >= 0) & (nxtblk_ref[i] < end))
            def _():
                for cp in fetch(nxt_ref[i], 1 - slot):
                    cp.start()

        @pl.when(used)
        def _():
            x = jnp.concatenate(_unpack_rows(xs_sub, bm), axis=1).astype(BF16)
            gu = jnp.dot(x, wgu_f32[slot].astype(BF16), preferred_element_type=F32) + bgu_ref[pl.ds(e, 1), :]
            gate = jnp.minimum(gu[:, 0:D_FF], SWIGLU_LIMIT)
            up = jnp.clip(gu[:, D_FF:2 * D_FF], -SWIGLU_LIMIT, SWIGLU_LIMIT)
            act = ((up + 1.0) * gate * jax.nn.sigmoid(SWIGLU_ALPHA * gate)).astype(BF16)
            y = jnp.dot(act, wd_f32[slot].astype(BF16), preferred_element_type=F32) + bd_ref[pl.ds(e, 1), :]
            _pack_rows(ys_sub, y.astype(BF16).astype(F32))

        @pl.when(jnp.logical_not(used))
        def _():
            ys_sub[...] = jnp.zeros((bm * PACK_SUB, LANES), I32)


def _experts(first, part_blocks, blocks, xs_part, w_gate_up, b_gate_up, w_down, b_down, ys_prev):
    rows = MOE_SUB * MOE_BM * PACK_SUB
    d = D_MODEL
    steps = part_blocks // MOE_SUB
    assert steps * MOE_SUB == part_blocks and first % MOE_SUB == 0

    def x_block(i, be, nx, nb, sl, nu):
        last = jnp.maximum((jnp.minimum(nu[0], first + part_blocks) - 1 - first) // MOE_SUB, 0)
        return jnp.minimum(i, last)

    in_specs = [
        pl.BlockSpec((rows, LANES), lambda i, be, nx, nb, sl, nu: (x_block(i, be, nx, nb, sl, nu), 0)),
        pl.BlockSpec(b_gate_up.shape, lambda i, be, nx, nb, sl, nu: (0, 0)),
        pl.BlockSpec(b_down.shape, lambda i, be, nx, nb, sl, nu: (0, 0)),
        pl.BlockSpec(memory_space=pl.ANY),
        pl.BlockSpec(memory_space=pl.ANY),
    ]
    operands = [blocks[0, :MOE_NB], blocks[1, :MOE_NB], blocks[3, :MOE_NB], blocks[4, :MOE_NB], blocks[2, :1],
                xs_part, b_gate_up, b_down, w_gate_up, w_down]
    aliases = {}
    if ys_prev is not None:
        in_specs.append(pl.BlockSpec(memory_space=pl.ANY))
        aliases = {len(operands): 0}
        operands.append(ys_prev)
    grid_spec = pltpu.PrefetchScalarGridSpec(
        num_scalar_prefetch=5,
        grid=(steps,),
        in_specs=in_specs,
        out_specs=pl.BlockSpec((rows, LANES), lambda i, be, nx, nb, sl, nu: (first // MOE_SUB + i, 0)),
        scratch_shapes=[
            pltpu.VMEM((2, d, 2 * D_FF), F32), pltpu.VMEM((2, D_FF, d), F32),
            pltpu.SemaphoreType.DMA((2, 2)),
        ],
    )
    return pl.pallas_call(
        functools.partial(_experts_body, first),
        grid_spec=grid_spec,
        out_shape=jax.ShapeDtypeStruct((MOE_ROWS * PACK_SUB, LANES), I32),
        input_output_aliases=aliases,
        compiler_params=_params(("arbitrary",), 48),
        name="experts",
    )(*operands)


def _combine_body(gates_ref, x1_ref, mod_ref, gpost_ref, y0_ref, y1_ref, y2_ref, y3_ref, o_ref):
    tm = TM_COMB
    gates = jnp.concatenate([gates_ref[...], jnp.zeros((SUBLANES - TOP_K, tm), F32)], axis=0).T
    y_hi = jnp.zeros((tm, PACK_COLS), F32)
    y_lo = jnp.zeros((tm, PACK_COLS), F32)
    for k, yk_ref in enumerate((y0_ref, y1_ref, y2_ref, y3_ref)):
        hi, lo = _unpack_rows(yk_ref, tm)
        y_hi = y_hi + hi * gates[:, k:k + 1]
        y_lo = y_lo + lo * gates[:, k:k + 1]
    y = jnp.concatenate([y_hi, y_lo], axis=1)
    gain = mod_ref[5:6, :] * gpost_ref[...]
    o_ref[...] = x1_ref[...] + _rms(y, gain)


def _combine(gates, x1, mod, g_post, y4):
    t, d = x1.shape
    tm = TM_COMB
    tiles = t // tm
    tiles_per_seq = SEQ // tm

    def slab(k):
        return pl.BlockSpec((tm * PACK_SUB, LANES), lambda i: (k * tiles + i, 0))

    return pl.pallas_call(
        _combine_body,
        grid=(tiles,),
        in_specs=[
            pl.BlockSpec((TOP_K, tm), lambda i: (0, i)),
            pl.BlockSpec((tm, d), lambda i: (i, 0)),
            pl.BlockSpec((None, 6, d), lambda i: (i // tiles_per_seq, 0, 0)),
            pl.BlockSpec(g_post.shape, lambda i: (0, 0)),
            slab(0), slab(1), slab(2), slab(3),
        ],
        out_specs=pl.BlockSpec((tm, d), lambda i: (i, 0)),
        out_shape=jax.ShapeDtypeStruct((t, d), F32),
        compiler_params=_params(("arbitrary",), 48),
        name="combine",
    )(gates, x1, mod, g_post, y4, y4, y4, y4)


def _rotary_tables():
    half = ROT_DIM // 2
    inv_freq = ROPE_THETA ** (-2.0 * np.arange(half, dtype=np.float32) / ROT_DIM)
    ang = np.arange(SEQ, dtype=np.float32)[:, None] * inv_freq[None, :].astype(np.float32)
    cos, sin = np.cos(ang), np.sin(ang)
    ones = np.ones((SEQ, ATT_HEAD_DIM - ROT_DIM), np.float32)
    zeros = np.zeros((SEQ, ATT_HEAD_DIM - ROT_DIM), np.float32)
    zh = np.zeros((SEQ, half), np.float32)
    reps = LANES // ATT_HEAD_DIM
    rc = np.tile(np.concatenate([cos, cos, ones], axis=1), (1, reps))
    rm = np.tile(np.concatenate([-sin, zh, zeros], axis=1), (1, reps))
    rp = np.tile(np.concatenate([zh, sin, zeros], axis=1), (1, reps))
    return tuple(jnp.asarray(t, F32) for t in (rc, rm, rp))


def _mixer_inputs(w_in, w_gk_fwd, b_gk_fwd, w_gk_bwd, b_gk_bwd):
    hk = GLA_HEADS * GLA_DK
    hv = GLA_HEADS * GLA_DV
    w = w_in[0]
    o_lr = 2 * hk + 2 * hv
    o_aq = o_lr + 2 * GLA_RANK
    o_ak = o_aq + ATT_Q_HEADS * ATT_HEAD_DIM
    o_av = o_ak + ATT_KV_HEADS * ATT_HEAD_DIM
    hd = ATT_HEAD_DIM
    wa = w[:, :o_lr].astype(BF16)
    wlr = w[:, o_lr:o_aq].astype(BF16)
    dup = lambda m: jnp.concatenate([m[:, g * hd:(g + 1) * hd] for g in range(ATT_KV_HEADS) for _ in range(2)], axis=1)
    wb = jnp.concatenate([w[:, o_aq:o_ak], dup(w[:, o_ak:o_av])], axis=1).astype(BF16)
    wvt = dup(w[:, o_av:o_av + ATT_KV_HEADS * hd]).T.astype(BF16)
    zr = jnp.zeros((GLA_RANK, hk), F32)
    wgk = jnp.concatenate([jnp.concatenate([w_gk_fwd[0], zr], axis=1),
                           jnp.concatenate([zr, w_gk_bwd[0]], axis=1)], axis=0).astype(BF16)
    bgk = jnp.concatenate([b_gk_fwd[0], b_gk_bwd[0]])[None, :]
    return (wa, wlr, wgk, bgk, wb, wvt) + _rotary_tables()


def kernel(x, c, w_ada, b_ada, g_pre_mix, g_post_mix, w_in, w_gk_fwd, b_gk_fwd, w_gk_bwd, b_gk_bwd, g_gla_out,
           attn_sink, w_out, g_pre_ffn, g_post_ffn, w_router, b_router, w_gate_up, b_gate_up, w_down, b_down):
    assert x.shape == (BATCH, SEQ, D_MODEL) and w_ada.shape[0] == 1
    d = D_MODEL
    x2 = x.reshape(TOKENS, d)

    c_pad = jnp.pad(c, ((0, SUBLANES - BATCH), (0, 0)))
    mod = _ada(c_pad, w_ada[0], b_ada)[:BATCH].reshape(BATCH, 6, d)

    mixer_in = _mixer_inputs(w_in, w_gk_fwd, b_gk_fwd, w_gk_bwd, b_gk_bwd)
    q, k, v, gg, laf, lab, aq, ak2, avt = _inproj(x2, mod, g_pre_mix, *mixer_in)
    o_f, o_b = _gla(q, k, v, laf, lab)
    o_att = _attn(attn_sink[0], aq, ak2, avt)

    wr_t = w_router[0].T
    wrh = wr_t.astype(BF16)
    wrl = (wr_t - wrh.astype(F32)).astype(BF16)
    x1, h2_tiles, top_i, gates, rank, counts = _post(
        o_f, o_b, gg, o_att, x2, mod, g_gla_out, g_post_mix, g_pre_ffn, w_out[0].astype(BF16), wrh, wrl,
        b_router[0][:, None])

    pos, blocks = _route(top_i, rank, counts)

    src = _sc_source_rows(pos.reshape(TOP_K * TOKENS), MOE_ROWS)
    h2_rows = h2_tiles.reshape(TOKENS, PACK_SUB, LANES)
    ys = None
    first = 0
    for part_blocks in MOE_PART_BLOCKS:
        row0, n_rows = first * MOE_BM, part_blocks * MOE_BM
        xs_p = _sc_gather_rows(h2_rows, src[row0:row0 + n_rows], SC_GATHER_RING if first == 0 else SC_BACKGROUND_RING)
        ys = _experts(first, part_blocks, blocks, xs_p.reshape(n_rows * PACK_SUB, LANES),
                      w_gate_up[0], b_gate_up[0], w_down[0], b_down[0], ys)
        first += part_blocks
    assert first == MOE_NB

    y4 = _sc_gather_rows(ys.reshape(MOE_ROWS, PACK_SUB, LANES), pos.reshape(TOP_K * TOKENS))
    out = _combine(gates, x1, mod, g_post_ffn, y4.reshape(TOP_K * TOKENS * PACK_SUB, LANES))
    return out.reshape(BATCH, SEQ, d)
```

```python
import functools

import jax
import jax.numpy as jnp
import numpy as np
from jax import lax
from jax.experimental import pallas as pl
from jax.experimental.pallas import tpu as pltpu
from jax.experimental.pallas import tpu_sc as plsc

F32 = jnp.float32
BF16 = jnp.bfloat16
I32 = jnp.int32

D_MODEL = 1024
BATCH = 2
SEQ = 8192
TOKENS = BATCH * SEQ
GLA_HEADS = 4
GLA_DV = 128
GLA_DK = 64
GLA_RANK = 16
GLA_GATE_NORMALIZER = 16.0
GLA_CHUNK = 64
ATT_Q_HEADS = 8
ATT_KV_HEADS = 2
ATT_HEAD_DIM = 64
ATT_WINDOW = 128
ATT_BLOCK = 128
ROT_DIM = 16
ROPE_THETA = 500000.0
N_EXPERTS = 32
TOP_K = 4
D_FF = 1024
SWIGLU_LIMIT = 7.0
SWIGLU_ALPHA = 1.702
NORM_EPS = 1e-6
NEG_INF = -1e30
LOG2_E = 1.4426950408889634

LANES = 128
SUBLANES = 8
PACK_COLS = D_MODEL // 2
PACK_SUB = PACK_COLS // LANES

TM_IN = 1024
IN_SUB = 4
GLA_GROUP = 16
ATT_GROUP = 16
TM_POST = 1024
POST_SUB = 4
POST_X_RING = 3
MOE_BM = 256
MOE_ROWS = TOKENS * TOP_K + N_EXPERTS * MOE_BM
MOE_NB = MOE_ROWS // MOE_BM
MOE_SUB = 4
MOE_PART_BLOCKS = (32, 128, 128)
MOE_NB_PAD = ((MOE_NB + LANES - 1) // LANES) * LANES
SC_SCAN_CHUNK = 4096
SC_SCAN_UNROLL = 8
TM_COMB = 1024
SC_GATHER_WINDOW = 32
SC_GATHER_RING = 4
SC_BACKGROUND_RING = 1

NT_DIMS = (((1,), (1,)), ((), ()))
TN_DIMS = (((0,), (0,)), ((), ()))


def _params(semantics, vmem_mib):
    return pltpu.CompilerParams(dimension_semantics=semantics, vmem_limit_bytes=vmem_mib * 1024 * 1024)


def _rms(x, g):
    return x * lax.rsqrt(jnp.mean(x * x, axis=-1, keepdims=True) + NORM_EPS) * g


def _silu(x):
    return x * jax.nn.sigmoid(x)


def _pack_rows(ref, v):
    m = v.shape[0]
    bits = lax.bitcast_convert_type(v, jnp.uint32)
    word = lax.bitcast_convert_type(bits[:, :PACK_COLS] | (bits[:, PACK_COLS:] >> 16), I32)
    for s in range(PACK_SUB):
        ref[pl.ds(s, m, stride=PACK_SUB), :] = word[:, s * LANES:(s + 1) * LANES]


def _unpack_rows(ref, m):
    word = jnp.concatenate([ref[pl.ds(s, m, stride=PACK_SUB), :] for s in range(PACK_SUB)], axis=1)
    bits = lax.bitcast_convert_type(word, jnp.uint32)
    hi = lax.bitcast_convert_type(bits & jnp.uint32(0xFFFF0000), F32)
    lo = lax.bitcast_convert_type(bits << 16, F32)
    return hi, lo


def _ada_body(c_ref, w_ref, b_ref, o_ref):
    ca = _silu(c_ref[...]).astype(BF16)
    o_ref[...] = jnp.dot(ca, w_ref[...].astype(BF16), preferred_element_type=F32) + b_ref[...]


def _ada(c_pad, w_ada, b_ada):
    d = D_MODEL
    return pl.pallas_call(
        _ada_body,
        grid=(6,),
        in_specs=[
            pl.BlockSpec((SUBLANES, d), lambda j: (0, 0)),
            pl.BlockSpec((d, d), lambda j: (0, j)),
            pl.BlockSpec((1, d), lambda j: (0, j)),
        ],
        out_specs=pl.BlockSpec((SUBLANES, d), lambda j: (0, j)),
        out_shape=jax.ShapeDtypeStruct((SUBLANES, 6 * d), F32),
        compiler_params=_params(("arbitrary",), 32),
        name="ada",
    )(c_pad, w_ada, b_ada)


def _rotary(x, cos_t, msin_t, psin_t):
    width = x.shape[1]
    reps = width // LANES
    c = jnp.concatenate([cos_t] * reps, axis=1)
    m = jnp.concatenate([msin_t] * reps, axis=1)
    p = jnp.concatenate([psin_t] * reps, axis=1)
    half = ROT_DIM // 2
    return x * c + pltpu.roll(x, width - half, 1) * m + pltpu.roll(x, half, 1) * p


def _inproj_body(x_ref, mod_ref, g_ref, wa_ref, wlr_ref, wgk_ref, bgk_ref, wb_ref, wvt_ref, rc_ref, rm_ref, rp_ref,
                 q_ref, k_ref, v_ref, gg_ref, laf_ref, lab_ref, aq_ref, ak_ref, avt_ref):
    shift = mod_ref[0:1, :]
    scale = mod_ref[1:2, :]
    hk = GLA_HEADS * GLA_DK
    hv = GLA_HEADS * GLA_DV
    nq = ATT_Q_HEADS * ATT_HEAD_DIM
    nk = 2 * ATT_KV_HEADS * ATT_HEAD_DIM
    sub = x_ref.shape[0] // IN_SUB
    subs = [slice(s * sub, (s + 1) * sub) for s in range(IN_SUB)]

    gain = g_ref[...] * (1.0 + scale)

    def hidden(rows):
        return (_rms(x_ref[rows, :], gain) + shift).astype(BF16)

    def project(h):
        return (jnp.dot(h, wa_ref[...], preferred_element_type=F32),
                jnp.dot(h, wlr_ref[...], preferred_element_type=F32),
                jnp.dot(h, wb_ref[...], preferred_element_type=F32),
                lax.dot_general(wvt_ref[...], h, NT_DIMS, preferred_element_type=F32))

    def finish(rows, pa, plr, pb, pvt):
        q_ref[rows, :] = pa[:, 0:hk] * (GLA_DK ** -0.5)
        k_ref[rows, :] = pa[:, hk:2 * hk]
        v_ref[rows, :] = pa[:, 2 * hk:2 * hk + hv].astype(BF16)
        gg_ref[rows, :] = pa[:, 2 * hk + hv:2 * hk + 2 * hv]
        gk = jnp.dot(plr.astype(BF16), wgk_ref[...], preferred_element_type=F32) + bgk_ref[...]
        la = (jnp.minimum(gk, 0.0) - jnp.log1p(jnp.exp(-jnp.abs(gk)))) * (1.0 / GLA_GATE_NORMALIZER)
        laf_ref[rows, :] = la[:, 0:hk]
        lab_ref[rows, :] = la[:, hk:2 * hk]
        rc, rm, rp = rc_ref[rows, :], rm_ref[rows, :], rp_ref[rows, :]
        aq_ref[rows, :] = (_rotary(pb[:, 0:nq], rc, rm, rp) * (ATT_HEAD_DIM ** -0.5 * LOG2_E)).astype(BF16)
        ak_ref[rows, :] = _rotary(pb[:, nq:nq + nk], rc, rm, rp).astype(BF16)
        avt_ref[:, rows] = pvt.astype(BF16)

    hs = [hidden(rows) for rows in subs]
    ps = [project(h) for h in hs]
    for rows, p in zip(subs, ps):
        finish(rows, *p)


def _inproj(x2, mod, g_pre, wa, wlr, wgk, bgk, wb, wvt, rc, rm, rp):
    t, d = x2.shape
    tm = TM_IN
    tiles_per_seq = SEQ // tm
    hk = GLA_HEADS * GLA_DK
    hv = GLA_HEADS * GLA_DV
    nq = ATT_Q_HEADS * ATT_HEAD_DIM
    nk = 2 * ATT_KV_HEADS * ATT_HEAD_DIM

    def full(a):
        return pl.BlockSpec(a.shape, lambda i: (0,) * a.ndim)

    def rows(w):
        return pl.BlockSpec((tm, w), lambda i: (i, 0))

    def table():
        return pl.BlockSpec((tm, LANES), lambda i: (i % tiles_per_seq, 0))

    out_widths = [(hk, F32), (hk, F32), (hv, BF16), (hv, F32), (hk, F32), (hk, F32), (nq, BF16), (nk, BF16)]
    return pl.pallas_call(
        _inproj_body,
        grid=(t // tm,),
        in_specs=[
            rows(d),
            pl.BlockSpec((None, 6, d), lambda i: (i // tiles_per_seq, 0, 0)),
            full(g_pre), full(wa), full(wlr), full(wgk), full(bgk), full(wb), full(wvt),
            table(), table(), table(),
        ],
        out_specs=[rows(w) for w, _ in out_widths] + [pl.BlockSpec((nk, tm), lambda i: (0, i))],
        out_shape=[jax.ShapeDtypeStruct((t, w), dt) for w, dt in out_widths] + [jax.ShapeDtypeStruct((nk, t), BF16)],
        compiler_params=_params(("arbitrary",), 56),
        name="inproj",
    )(x2, mod, g_pre, wa, wlr, wgk, bgk, wb, wvt, rc, rm, rp)


def _gla_body(qf_ref, kf_ref, vf_ref, laf_ref, qb_ref, kb_ref, vb_ref, lab_ref, of_ref, ob_ref, sf_ref, sb_ref):
    @pl.when(pl.program_id(1) == 0)
    def _():
        sf_ref[...] = jnp.zeros_like(sf_ref)
        sb_ref[...] = jnp.zeros_like(sb_ref)

    c = GLA_CHUNK
    r_i = lax.broadcasted_iota(I32, (c, c), 0)
    c_i = lax.broadcasted_iota(I32, (c, c), 1)
    lower = c_i <= r_i
    upper = c_i >= r_i
    cum_f = jnp.where(lower, 1.0, 0.0).astype(BF16)
    cum_b = jnp.where(upper, 1.0, 0.0).astype(BF16)
    lane = lax.broadcasted_iota(I32, (1, LANES), 1)
    head_masks = (lane < GLA_DK, lane >= GLA_DK)

    fwd = [(qf_ref, kf_ref, laf_ref, vf_ref, of_ref, slice(g * c, (g + 1) * c), cum_f, lower, c - 1, c // 2 - 1)
           for g in range(GLA_GROUP)]
    bwd = [(qb_ref, kb_ref, lab_ref, vb_ref, ob_ref, slice(g * c, (g + 1) * c), cum_b, upper, 0, c // 2)
           for g in reversed(range(GLA_GROUP))]
    heads = range(GLA_HEADS)
    pair = [slice((h // 2) * LANES, (h // 2 + 1) * LANES) for h in heads]
    vcols = [slice(h * GLA_DV, (h + 1) * GLA_DV) for h in heads]

    def stage1(item):
        q_ref, k_ref, la_ref, v_ref, o_ref, rows, cum, tri, i_last, i_mid = item
        la = la_ref[rows, :]
        hi = la.astype(BF16)
        lo = (la - hi.astype(F32)).astype(BF16)
        b = jnp.dot(cum, hi, preferred_element_type=F32) + jnp.dot(cum, lo, preferred_element_type=F32)
        b_last = b[i_last:i_last + 1, :]
        b_mid = b[i_mid:i_mid + 1, :]
        q, k = q_ref[rows, :], k_ref[rows, :]
        return (q * jnp.exp(b - b_mid), (k * jnp.exp(b_mid - b)).astype(BF16), q * jnp.exp(b),
                (k * jnp.exp(b_last - b)).astype(BF16), jnp.exp(b_last))

    def stage2(item, pre):
        v_ref, rows, tri = item[3], item[5], item[7]
        qs, ks, qi, kst, decay = pre
        out = []
        for h in heads:
            mask = head_masks[h % 2]
            qs_h = jnp.where(mask, qs[:, pair[h]], 0.0).astype(BF16)
            sc = lax.dot_general(qs_h, ks[:, pair[h]], NT_DIMS, preferred_element_type=F32)
            v_h = v_ref[rows, vcols[h]]
            kv = lax.dot_general(v_h, kst[:, pair[h]], TN_DIMS, preferred_element_type=F32)
            out.append((jnp.where(tri, sc, 0.0).astype(BF16), kv,
                        jnp.where(mask, qi[:, pair[h]], 0.0).astype(BF16), v_h))
        return out

    def run(items, s_ref):
        pre = [stage1(it) for it in items]
        mid = [stage2(it, p) for it, p in zip(items, pre)]
        states = [s_ref[h] for h in heads]
        for it, p, m in zip(items, pre, mid):
            o_ref, rows, decay = it[4], it[5], p[4]
            for h in heads:
                sc, kv, qi_h, v_h = m[h]
                o = jnp.dot(sc, v_h, preferred_element_type=F32)
                o = o + lax.dot_general(qi_h, states[h].astype(BF16), NT_DIMS, preferred_element_type=F32)
                o_ref[rows, vcols[h]] = o
                states[h] = states[h] * decay[:, pair[h]] + kv
        for h in heads:
            s_ref[h] = states[h]

    run(fwd, sf_ref)
    run(bwd, sb_ref)


def _gla(q, k, v, laf, lab):
    t = q.shape[0]
    rows = GLA_GROUP * GLA_CHUNK
    ng = SEQ // rows
    hk = GLA_HEADS * GLA_DK
    hv = GLA_HEADS * GLA_DV

    def fwd(w):
        return pl.BlockSpec((rows, w), lambda b, n: (b * ng + n, 0))

    def bwd(w):
        return pl.BlockSpec((rows, w), lambda b, n: (b * ng + ng - 1 - n, 0))

    return pl.pallas_call(
        _gla_body,
        grid=(BATCH, ng),
        in_specs=[fwd(hk), fwd(hk), fwd(hv), fwd(hk), bwd(hk), bwd(hk), bwd(hv), bwd(hk)],
        out_specs=[fwd(hv), bwd(hv)],
        out_shape=[jax.ShapeDtypeStruct((t, hv), F32)] * 2,
        scratch_shapes=[pltpu.VMEM((GLA_HEADS, GLA_DV, 2 * GLA_DK), F32)] * 2,
        compiler_params=_params(("arbitrary", "arbitrary"), 32),
        name="gla",
    )(q, k, v, laf, q, k, v, lab)


def _attn_body(sink_ref, q_ref, kp_ref, kc_ref, kn_ref, vp_ref, vc_ref, vn_ref, o_ref):
    step = pl.program_id(1)
    last = pl.num_programs(1) - 1
    qb = ATT_BLOCK
    hd = ATT_HEAD_DIM
    k_all = jnp.concatenate([kp_ref[...], kc_ref[...], kn_ref[...]], axis=0)
    vt_all = jnp.concatenate([vp_ref[...], vc_ref[...], vn_ref[...]], axis=1)
    lane = lax.broadcasted_iota(I32, (1, LANES), 1)
    lo = lane < hd
    j_k = lax.broadcasted_iota(I32, (3 * qb, qb), 0)
    i_q = lax.broadcasted_iota(I32, (3 * qb, qb), 1)
    band = jnp.abs(j_k - qb - i_q) <= ATT_WINDOW
    sinks = [jnp.concatenate([jnp.full((1, qb), sink_ref[4 * g + r] * LOG2_E, F32) for r in range(4)], axis=1)
             for g in range(ATT_KV_HEADS)]
    ones_rows = jnp.ones((SUBLANES, 3 * qb), BF16)
    work = [(j, g) for j in range(ATT_GROUP) for g in range(ATT_KV_HEADS)]

    def scores(j, g):
        valid = band
        if j == 0:
            valid = valid & ((j_k >= qb) | (step > 0))
        if j == ATT_GROUP - 1:
            valid = valid & ((j_k < 2 * qb) | (step < last))
        valid4 = jnp.concatenate([valid] * 4, axis=1)
        rows = slice(j * qb, (j + 1) * qb)
        kg = k_all[j * qb:(j + 3) * qb, g * LANES:(g + 1) * LANES]
        qa = q_ref[rows, (2 * g) * LANES:(2 * g + 1) * LANES]
        qc = q_ref[rows, (2 * g + 1) * LANES:(2 * g + 2) * LANES]
        zero = jnp.zeros_like(qa)
        lhs = jnp.concatenate([jnp.where(lo, qa, zero), jnp.where(lo, zero, qa),
                               jnp.where(lo, qc, zero), jnp.where(lo, zero, qc)], axis=0)
        st = lax.dot_general(kg, lhs, NT_DIMS, preferred_element_type=F32)
        return jnp.concatenate([jnp.where(valid4[0:qb], st[0:qb], NEG_INF), st[qb:2 * qb],
                                jnp.where(valid4[2 * qb:3 * qb], st[2 * qb:3 * qb], NEG_INF)], axis=0)

    def softmax(st, g):
        sink = sinks[g]
        m = jnp.maximum(jnp.max(st, axis=0, keepdims=True), sink)
        return jnp.exp2(st - m).astype(BF16), jnp.exp2(sink - m)

    def output(j, g, p, p_sink):
        rows = slice(j * qb, (j + 1) * qb)
        vgt = vt_all[g * LANES:(g + 1) * LANES, j * qb:(j + 3) * qb]
        res = jnp.dot(jnp.concatenate([vgt, ones_rows], axis=0), p, preferred_element_type=F32)
        ot = res[0:LANES] * (1.0 / (res[LANES:LANES + 1] + p_sink))
        pair_a = jnp.concatenate([ot[0:hd, 0:qb], ot[hd:2 * hd, qb:2 * qb]], axis=0)
        pair_c = jnp.concatenate([ot[0:hd, 2 * qb:3 * qb], ot[hd:2 * hd, 3 * qb:4 * qb]], axis=0)
        o_ref[rows, (2 * g) * LANES:(2 * g + 1) * LANES] = pair_a.T.astype(o_ref.dtype)
        o_ref[rows, (2 * g + 1) * LANES:(2 * g + 2) * LANES] = pair_c.T.astype(o_ref.dtype)

    s_all = [scores(j, g) for j, g in work]
    p_all = [softmax(st, g) for st, (j, g) in zip(s_all, work)]
    for (j, g), (p, p_sink) in zip(work, p_all):
        output(j, g, p, p_sink)


def _attn(sink, aq, ak2, avt):
    t = aq.shape[0]
    qb = ATT_BLOCK
    nb = SEQ // qb
    steps = nb // ATT_GROUP
    nq = ATT_Q_HEADS * ATT_HEAD_DIM
    nk = 2 * ATT_KV_HEADS * ATT_HEAD_DIM

    def edge_block(b, n, shift):
        return b * nb + jnp.clip(n * ATT_GROUP + shift, 0, nb - 1)

    def k_edge(shift):
        return pl.BlockSpec((qb, nk), lambda b, n: (edge_block(b, n, shift), 0))

    def v_edge(shift):
        return pl.BlockSpec((nk, qb), lambda b, n: (0, edge_block(b, n, shift)))

    def group(w):
        return pl.BlockSpec((ATT_GROUP * qb, w), lambda b, n: (b * steps + n, 0))

    v_group = pl.BlockSpec((nk, ATT_GROUP * qb), lambda b, n: (0, b * steps + n))
    return pl.pallas_call(
        _attn_body,
        grid=(BATCH, steps),
        in_specs=[
            pl.BlockSpec(memory_space=pltpu.SMEM),
            group(nq),
            k_edge(-1), group(nk), k_edge(ATT_GROUP), v_edge(-1), v_group, v_edge(ATT_GROUP),
        ],
        out_specs=group(nq),
        out_shape=jax.ShapeDtypeStruct((t, nq), BF16),
        compiler_params=_params(("arbitrary", "arbitrary"), 48),
        name="attn",
    )(sink, aq, ak2, ak2, ak2, avt, avt, avt)


def _post_body(of_ref, ob_ref, gg_ref, oa_ref, x_ref, mod_ref, ggla_ref, gpm_ref, gpf_ref, wout_ref,
               wrh_ref, wrl_ref, br_ref,
               x1_ref, h2_ref, ti_ref, gt_ref, rk_ref, cnt_ref, base_ref, x_ring, x_sems):
    tm = TM_POST
    step = pl.program_id(0)
    steps = pl.num_programs(0)

    def x_fetch(s):
        slot = s % POST_X_RING
        return pltpu.make_async_copy(x_ref.at[pl.ds(pl.multiple_of(s * tm, tm), tm), :], x_ring.at[slot],
                                     x_sems.at[slot])

    @pl.when(step == 0)
    def _():
        base_ref[...] = jnp.zeros_like(base_ref)
        for s in range(POST_X_RING - 1):
            x_fetch(s).start()

    @pl.when(step + POST_X_RING - 1 < steps)
    def _():
        x_fetch(step + POST_X_RING - 1).start()

    x_fetch(step).wait()
    x_tile = x_ring.at[step % POST_X_RING]

    gain1 = mod_ref[2:3, :] * gpm_ref[...]
    shift2 = mod_ref[3:4, :]
    gain2 = gpf_ref[...] * (1.0 + mod_ref[4:5, :])
    sub = tm // POST_SUB
    subs = [slice(s * sub, (s + 1) * sub) for s in range(POST_SUB)]

    def mixer_out(rows):
        og = of_ref[rows, :] + ob_ref[rows, :]
        gg = gg_ref[rows, :]
        parts = []
        for h in range(GLA_HEADS):
            cols = slice(h * GLA_DV, (h + 1) * GLA_DV)
            parts.append((_rms(og[:, cols], ggla_ref[...]) * _silu(gg[:, cols])).astype(BF16))
        return jnp.concatenate(parts + [oa_ref[rows, :]], axis=1)

    def ffn_in(s, y):
        rows = subs[s]
        x1 = x_tile[rows, :] + _rms(y, gain1)
        x1_ref[rows, :] = x1
        h2 = _rms(x1, gain2) + shift2
        hi = h2.astype(BF16)
        hi32 = hi.astype(F32)
        _pack_rows(h2_ref.at[pl.ds(s * sub * PACK_SUB, sub * PACK_SUB), :], hi32)
        return hi, (h2 - hi32).astype(BF16)

    o_subs = [mixer_out(rows) for rows in subs]
    y_subs = [jnp.dot(o, wout_ref[...], preferred_element_type=F32) for o in o_subs]
    split = [ffn_in(s, y) for s, y in enumerate(y_subs)]
    h2_hi = jnp.concatenate([hi for hi, _ in split], axis=0)
    h2_lo = jnp.concatenate([lo for _, lo in split], axis=0)

    wrh = wrh_ref[...]
    logits = (lax.dot_general(wrh, h2_hi, NT_DIMS, preferred_element_type=F32)
              + lax.dot_general(wrh, h2_lo, NT_DIMS, preferred_element_type=F32)
              + lax.dot_general(wrl_ref[...], h2_hi, NT_DIMS, preferred_element_type=F32)
              + br_ref[...])
    e_iota = lax.broadcasted_iota(I32, (N_EXPERTS, tm), 0)
    idxs, vals = [], []
    work = logits
    for _ in range(TOP_K):
        m = jnp.max(work, axis=0, keepdims=True)
        idx = jnp.min(jnp.where(work == m, e_iota, N_EXPERTS), axis=0, keepdims=True)
        idxs.append(idx)
        vals.append(m)
        work = jnp.where(e_iota == idx, -jnp.inf, work)
    exps = [jnp.exp(v - vals[0]) for v in vals]
    inv = 1.0 / (exps[0] + exps[1] + exps[2] + exps[3])
    gt_ref[...] = jnp.concatenate([e * inv for e in exps], axis=0)
    ti_ref[...] = jnp.concatenate(idxs, axis=0)

    onehots = [e_iota == idx for idx in idxs]
    member = jnp.where(onehots[0] | onehots[1] | onehots[2] | onehots[3], 1.0, 0.0)
    t_row = lax.broadcasted_iota(I32, (tm, tm), 0)
    t_col = lax.broadcasted_iota(I32, (tm, tm), 1)
    strict = jnp.where(t_row < t_col, 1.0, 0.0).astype(BF16)
    before = base_ref[...] + jnp.dot(member.astype(BF16), strict, preferred_element_type=F32)
    rk_ref[...] = jnp.concatenate(
        [jnp.sum(jnp.where(oh, before, 0.0), axis=0, keepdims=True) for oh in onehots], axis=0).astype(I32)
    new_base = base_ref[...] + jnp.sum(member, axis=1, keepdims=True)
    base_ref[...] = new_base
    cnt_ref[...] = jnp.broadcast_to(new_base, cnt_ref.shape)


def _post(o_f, o_b, gg, o_att, x2, mod, g_gla, g_pm, g_pf, wout, wrh, wrl, br):
    t, d = x2.shape
    tm = TM_POST
    tiles_per_seq = SEQ // tm
    hv = GLA_HEADS * GLA_DV

    def full(a):
        return pl.BlockSpec(a.shape, lambda i: (0,) * a.ndim)

    def rows(w):
        return pl.BlockSpec((tm, w), lambda i: (i, 0))

    def lanes():
        return pl.BlockSpec((TOP_K, tm), lambda i: (0, i))

    return pl.pallas_call(
        _post_body,
        grid=(t // tm,),
        in_specs=[
            rows(hv), rows(hv), rows(hv), rows(hv), pl.BlockSpec(memory_space=pl.ANY),
            pl.BlockSpec((None, 6, d), lambda i: (i // tiles_per_seq, 0, 0)),
            full(g_gla), full(g_pm), full(g_pf), full(wout), full(wrh), full(wrl), full(br),
        ],
        out_specs=[
            rows(d),
            pl.BlockSpec((tm * PACK_SUB, LANES), lambda i: (i, 0)),
            lanes(), lanes(), lanes(),
            pl.BlockSpec((N_EXPERTS, LANES), lambda i: (0, 0)),
        ],
        out_shape=[
            jax.ShapeDtypeStruct((t, d), F32),
            jax.ShapeDtypeStruct((t * PACK_SUB, LANES), I32),
            jax.ShapeDtypeStruct((TOP_K, t), I32),
            jax.ShapeDtypeStruct((TOP_K, t), F32),
            jax.ShapeDtypeStruct((TOP_K, t), I32),
            jax.ShapeDtypeStruct((N_EXPERTS, LANES), F32),
        ],
        scratch_shapes=[pltpu.VMEM((N_EXPERTS, 1), F32), pltpu.VMEM((POST_X_RING, tm, d), F32),
                        pltpu.SemaphoreType.DMA((POST_X_RING,))],
        compiler_params=_params(("arbitrary",), 56),
        name="post",
    )(o_f, o_b, gg, o_att, x2, mod, g_gla, g_pm, g_pf, wout, wrh, wrl, br)


def _route_body(ti_ref, rk_ref, cnt_ref, pos_ref, blk_ref):
    cnt = cnt_ref[...]
    padded = jnp.floor((cnt + (MOE_BM - 1)) * (1.0 / MOE_BM)) * MOE_BM
    starts, ends = [], []
    acc = jnp.zeros((1, LANES), F32)
    for e in range(N_EXPERTS):
        starts.append(acc)
        acc = acc + padded[e:e + 1, :]
        ends.append(acc)
    ti = ti_ref[...]
    off = jnp.zeros(ti.shape, F32)
    for e in range(N_EXPERTS):
        off = jnp.where(ti == e, starts[e][:, 0:1], off)
    pos_ref[...] = rk_ref[...] + off.astype(I32)

    def owner_of(row):
        n_le = jnp.zeros(row.shape, I32)
        for e in range(N_EXPERTS):
            n_le = n_le + jnp.where(ends[e][:, 0:1] <= row, 1, 0)
        return jnp.minimum(n_le, N_EXPERTS - 1)

    block_start = lax.broadcasted_iota(I32, (1, MOE_NB_PAD), 1).astype(F32) * MOE_BM
    owner = owner_of(block_start)
    nxt = jnp.zeros((1, MOE_NB_PAD), I32)
    nxt_blk = jnp.zeros((1, MOE_NB_PAD), I32)
    slot = jnp.zeros((1, MOE_NB_PAD), I32)
    ordinal = jnp.zeros((1, 1), I32)
    for e in range(N_EXPERTS):
        end_e = ends[e][:, 0:1]
        nxt = jnp.where(owner == e, jnp.where(end_e < acc[:, 0:1], owner_of(end_e), -1), nxt)
        nxt_blk = jnp.where(owner == e, (end_e * (1.0 / MOE_BM)).astype(I32), nxt_blk)
        slot = jnp.where(owner == e, ordinal & 1, slot)
        ordinal = ordinal + jnp.where(padded[e:e + 1, 0:1] > 0, 1, 0)
    used = jnp.broadcast_to((acc[:, 0:1] * (1.0 / MOE_BM)).astype(I32), (1, MOE_NB_PAD))
    blk_ref[...] = jnp.concatenate(
        [owner, nxt, used, nxt_blk, slot, jnp.zeros((SUBLANES - 5, MOE_NB_PAD), I32)], axis=0)


def _route(top_i, rank, counts):
    return pl.pallas_call(
        _route_body,
        out_shape=[
            jax.ShapeDtypeStruct(top_i.shape, I32),
            jax.ShapeDtypeStruct((SUBLANES, MOE_NB_PAD), I32),
        ],
        compiler_params=pltpu.CompilerParams(vmem_limit_bytes=32 * 1024 * 1024),
        name="route",
    )(top_i, rank, counts)


def _sc_workers():
    info = plsc.get_sparse_core_info()
    return info.num_cores, info.num_subcores, info.num_lanes


def _sc_gather_loop(table_hbm, out_hbm, idx_v, base, chunks, bufs, gather_sems, write_sems):
    window = SC_GATHER_WINDOW
    ring = len(bufs)

    def fetch(c, b):
        return pltpu.make_async_copy(table_hbm.at[idx_v.at[pl.ds(c * window, window)]], bufs[b], gather_sems[b])

    def flush(c, b):
        return pltpu.make_async_copy(bufs[b], out_hbm.at[pl.ds(base + c * window, window)], write_sems[b])

    for b in range(ring):
        fetch(b, b).start()

    @pl.loop(0, chunks, step=ring)
    def _(c0):
        for b in range(ring):
            c = c0 + b
            fetch(c, b).wait()
            flush(c, b).start()

            @pl.when(c + ring < chunks)
            def _():
                flush(c, b).wait()
                fetch(c + ring, b).start()

    for b in range(ring):
        flush(chunks - ring + b, b).wait()


def _sc_source_rows(pos_flat, n_rows):
    cores, subcores, lanes = _sc_workers()
    workers = cores * subcores
    per_worker = n_rows // workers
    n_assign = pos_flat.shape[0]
    scan = SC_SCAN_CHUNK
    assert per_worker * workers == n_rows and per_worker % lanes == 0
    assert n_assign % scan == 0 and scan % lanes == 0 and TOKENS & (TOKENS - 1) == 0
    mesh = plsc.VectorSubcoreMesh(core_axis_name="core", subcore_axis_name="subcore")

    @functools.partial(
        pl.kernel,
        out_type=jax.ShapeDtypeStruct((n_rows,), I32),
        mesh=mesh,
        scratch_types=[pltpu.VMEM((per_worker,), I32), pltpu.VMEM((scan,), I32)],
        compiler_params=pltpu.CompilerParams(needs_layout_passes=False),
        name="sc_source_rows",
    )
    def invert(pos_hbm, out_hbm, src_v, pos_v):
        wid = lax.axis_index("subcore") * cores + lax.axis_index("core")
        base = wid * per_worker
        lane = lax.iota(I32, lanes)

        @pl.loop(0, per_worker, step=lanes)
        def _(j):
            src_v[pl.ds(j, lanes)] = (base + j + lane) & (TOKENS - 1)

        @pl.loop(0, n_assign, step=scan)
        def _(a0):
            pltpu.sync_copy(pos_hbm.at[pl.ds(a0, scan)], pos_v)

            @plsc.parallel_loop(0, scan, step=lanes, unroll=SC_SCAN_UNROLL)
            def _(j):
                rel = pos_v[pl.ds(j, lanes)] - base
                mine = (rel >= 0) & (rel < per_worker)
                tok = (a0 + j + lane) & (TOKENS - 1)
                plsc.store_scatter(src_v, [jnp.where(mine, rel, 0)], tok, mask=mine)

        pltpu.sync_copy(src_v, out_hbm.at[pl.ds(base, per_worker)])

    return invert(pos_flat)


def _sc_gather_rows(table, idx, ring=SC_GATHER_RING):
    cores, subcores, _ = _sc_workers()
    workers = cores * subcores
    n = idx.shape[0]
    window = SC_GATHER_WINDOW
    per_worker = n // workers
    chunks = per_worker // window
    assert per_worker * workers == n and chunks * window == per_worker and chunks % ring == 0
    row_shape = table.shape[1:]
    mesh = plsc.VectorSubcoreMesh(core_axis_name="core", subcore_axis_name="subcore")

    @functools.partial(
        pl.kernel,
        out_type=jax.ShapeDtypeStruct((n,) + row_shape, table.dtype),
        mesh=mesh,
        scratch_types=[pltpu.VMEM((per_worker,), I32)]
        + [pltpu.VMEM((window,) + row_shape, table.dtype)] * ring
        + [pltpu.SemaphoreType.DMA] * (2 * ring),
        name="sc_gather_rows",
    )
    def gather(table_hbm, idx_hbm, out_hbm, idx_v, *scratch):
        wid = lax.axis_index("subcore") * cores + lax.axis_index("core")
        base = wid * per_worker
        pltpu.sync_copy(idx_hbm.at[pl.ds(base, per_worker)], idx_v)
        _sc_gather_loop(table_hbm, out_hbm, idx_v, base, chunks,
                        scratch[:ring], scratch[ring:2 * ring], scratch[2 * ring:])

    return gather(table, idx)


def _experts_body(first, be_ref, nxt_ref, nxtblk_ref, slot_ref, nu_ref, xs_ref, bgu_ref, bd_ref, wgu_hbm, wd_hbm,
                  *rest):
    ys_ref, wgu_f32, wd_f32, sems = rest[-4:]
    step = pl.program_id(0)
    end = first + pl.num_programs(0) * MOE_SUB
    bm = MOE_BM
    n_used = nu_ref[0]

    def fetch(e, slot):
        return (pltpu.make_async_copy(wgu_hbm.at[e], wgu_f32.at[slot], sems.at[0, slot]),
                pltpu.make_async_copy(wd_hbm.at[e], wd_f32.at[slot], sems.at[1, slot]))

    @pl.when((step == 0) & (first < n_used))
    def _():
        for cp in fetch(be_ref[first], slot_ref[first]):
            cp.start()

    for sub in range(MOE_SUB):
        i = first + step * MOE_SUB + sub
        e = be_ref[i]
        slot = slot_ref[i]
        used = i < n_used
        fresh = (i == first) | (e != be_ref[jnp.maximum(i - 1, 0)])
        rows = pl.ds(sub * bm * PACK_SUB, bm * PACK_SUB)
        xs_sub = xs_ref.at[rows, :]
        ys_sub = ys_ref.at[rows, :]

        @pl.when(used & fresh)
        def _():
            for cp in fetch(e, slot):
                cp.wait()

            @pl.when((nxt_ref[i] >= 0) & (nxtblk_ref[i] < end))
            def _():
                for cp in fetch(nxt_ref[i], 1 - slot):
                    cp.start()

        @pl.when(used)
        def _():
            x = jnp.concatenate(_unpack_rows(xs_sub, bm), axis=1).astype(BF16)
            gu = jnp.dot(x, wgu_f32[slot].astype(BF16), preferred_element_type=F32) + bgu_ref[pl.ds(e, 1), :]
            gate = jnp.minimum(gu[:, 0:D_FF], SWIGLU_LIMIT)
            up = jnp.clip(gu[:, D_FF:2 * D_FF], -SWIGLU_LIMIT, SWIGLU_LIMIT)
            act = ((up + 1.0) * gate * jax.nn.sigmoid(SWIGLU_ALPHA * gate)).astype(BF16)
            y = jnp.dot(act, wd_f32[slot].astype(BF16), preferred_element_type=F32) + bd_ref[pl.ds(e, 1), :]
            _pack_rows(ys_sub, y.astype(BF16).astype(F32))

        @pl.when(jnp.logical_not(used))
        def _():
            ys_sub[...] = jnp.zeros((bm * PACK_SUB, LANES), I32)


def _experts(first, part_blocks, blocks, xs_part, w_gate_up, b_gate_up, w_down, b_down, ys_prev):
    rows = MOE_SUB * MOE_BM * PACK_SUB
    d = D_MODEL
    steps = part_blocks // MOE_SUB
    assert steps * MOE_SUB == part_blocks and first % MOE_SUB == 0

    def x_block(i, be, nx, nb, sl, nu):
        last = jnp.maximum((jnp.minimum(nu[0], first + part_blocks) - 1 - first) // MOE_SUB, 0)
        return jnp.minimum(i, last)

    in_specs = [
        pl.BlockSpec((rows, LANES), lambda i, be, nx, nb, sl, nu: (x_block(i, be, nx, nb, sl, nu), 0)),
        pl.BlockSpec(b_gate_up.shape, lambda i, be, nx, nb, sl, nu: (0, 0)),
        pl.BlockSpec(b_down.shape, lambda i, be, nx, nb, sl, nu: (0, 0)),
        pl.BlockSpec(memory_space=pl.ANY),
        pl.BlockSpec(memory_space=pl.ANY),
    ]
    operands = [blocks[0, :MOE_NB], blocks[1, :MOE_NB], blocks[3, :MOE_NB], blocks[4, :MOE_NB], blocks[2, :1],
                xs_part, b_gate_up, b_down, w_gate_up, w_down]
    aliases = {}
    if ys_prev is not None:
        in_specs.append(pl.BlockSpec(memory_space=pl.ANY))
        aliases = {len(operands): 0}
        operands.append(ys_prev)
    grid_spec = pltpu.PrefetchScalarGridSpec(
        num_scalar_prefetch=5,
        grid=(steps,),
        in_specs=in_specs,
        out_specs=pl.BlockSpec((rows, LANES), lambda i, be, nx, nb, sl, nu: (first // MOE_SUB + i, 0)),
        scratch_shapes=[
            pltpu.VMEM((2, d, 2 * D_FF), F32), pltpu.VMEM((2, D_FF, d), F32),
            pltpu.SemaphoreType.DMA((2, 2)),
        ],
    )
    return pl.pallas_call(
        functools.partial(_experts_body, first),
        grid_spec=grid_spec,
        out_shape=jax.ShapeDtypeStruct((MOE_ROWS * PACK_SUB, LANES), I32),
        input_output_aliases=aliases,
        compiler_params=_params(("arbitrary",), 48),
        name="experts",
    )(*operands)


def _combine_body(gates_ref, x1_ref, mod_ref, gpost_ref, y0_ref, y1_ref, y2_ref, y3_ref, o_ref):
    tm = TM_COMB
    gates = jnp.concatenate([gates_ref[...], jnp.zeros((SUBLANES - TOP_K, tm), F32)], axis=0).T
    y_hi = jnp.zeros((tm, PACK_COLS), F32)
    y_lo = jnp.zeros((tm, PACK_COLS), F32)
    for k, yk_ref in enumerate((y0_ref, y1_ref, y2_ref, y3_ref)):
        hi, lo = _unpack_rows(yk_ref, tm)
        y_hi = y_hi + hi * gates[:, k:k + 1]
        y_lo = y_lo + lo * gates[:, k:k + 1]
    y = jnp.concatenate([y_hi, y_lo], axis=1)
    gain = mod_ref[5:6, :] * gpost_ref[...]
    o_ref[...] = x1_ref[...] + _rms(y, gain)


def _combine(gates, x1, mod, g_post, y4):
    t, d = x1.shape
    tm = TM_COMB
    tiles = t // tm
    tiles_per_seq = SEQ // tm

    def slab(k):
        return pl.BlockSpec((tm * PACK_SUB, LANES), lambda i: (k * tiles + i, 0))

    return pl.pallas_call(
        _combine_body,
        grid=(tiles,),
        in_specs=[
            pl.BlockSpec((TOP_K, tm), lambda i: (0, i)),
            pl.BlockSpec((tm, d), lambda i: (i, 0)),
            pl.BlockSpec((None, 6, d), lambda i: (i // tiles_per_seq, 0, 0)),
            pl.BlockSpec(g_post.shape, lambda i: (0, 0)),
            slab(0), slab(1), slab(2), slab(3),
        ],
        out_specs=pl.BlockSpec((tm, d), lambda i: (i, 0)),
        out_shape=jax.ShapeDtypeStruct((t, d), F32),
        compiler_params=_params(("arbitrary",), 48),
        name="combine",
    )(gates, x1, mod, g_post, y4, y4, y4, y4)


def _rotary_tables():
    half = ROT_DIM // 2
    inv_freq = ROPE_THETA ** (-2.0 * np.arange(half, dtype=np.float32) / ROT_DIM)
    ang = np.arange(SEQ, dtype=np.float32)[:, None] * inv_freq[None, :].astype(np.float32)
    cos, sin = np.cos(ang), np.sin(ang)
    ones = np.ones((SEQ, ATT_HEAD_DIM - ROT_DIM), np.float32)
    zeros = np.zeros((SEQ, ATT_HEAD_DIM - ROT_DIM), np.float32)
    zh = np.zeros((SEQ, half), np.float32)
    reps = LANES // ATT_HEAD_DIM
    rc = np.tile(np.concatenate([cos, cos, ones], axis=1), (1, reps))
    rm = np.tile(np.concatenate([-sin, zh, zeros], axis=1), (1, reps))
    rp = np.tile(np.concatenate([zh, sin, zeros], axis=1), (1, reps))
    return tuple(jnp.asarray(t, F32) for t in (rc, rm, rp))


def _mixer_inputs(w_in, w_gk_fwd, b_gk_fwd, w_gk_bwd, b_gk_bwd):
    hk = GLA_HEADS * GLA_DK
    hv = GLA_HEADS * GLA_DV
    w = w_in[0]
    o_lr = 2 * hk + 2 * hv
    o_aq = o_lr + 2 * GLA_RANK
    o_ak = o_aq + ATT_Q_HEADS * ATT_HEAD_DIM
    o_av = o_ak + ATT_KV_HEADS * ATT_HEAD_DIM
    hd = ATT_HEAD_DIM
    wa = w[:, :o_lr].astype(BF16)
    wlr = w[:, o_lr:o_aq].astype(BF16)
    dup = lambda m: jnp.concatenate([m[:, g * hd:(g + 1) * hd] for g in range(ATT_KV_HEADS) for _ in range(2)], axis=1)
    wb = jnp.concatenate([w[:, o_aq:o_ak], dup(w[:, o_ak:o_av])], axis=1).astype(BF16)
    wvt = dup(w[:, o_av:o_av + ATT_KV_HEADS * hd]).T.astype(BF16)
    zr = jnp.zeros((GLA_RANK, hk), F32)
    wgk = jnp.concatenate([jnp.concatenate([w_gk_fwd[0], zr], axis=1),
                           jnp.concatenate([zr, w_gk_bwd[0]], axis=1)], axis=0).astype(BF16)
    bgk = jnp.concatenate([b_gk_fwd[0], b_gk_bwd[0]])[None, :]
    return (wa, wlr, wgk, bgk, wb, wvt) + _rotary_tables()


def kernel(x, c, w_ada, b_ada, g_pre_mix, g_post_mix, w_in, w_gk_fwd, b_gk_fwd, w_gk_bwd, b_gk_bwd, g_gla_out,
           attn_sink, w_out, g_pre_ffn, g_post_ffn, w_router, b_router, w_gate_up, b_gate_up, w_down, b_down):
    assert x.shape == (BATCH, SEQ, D_MODEL) and w_ada.shape[0] == 1
    d = D_MODEL
    x2 = x.reshape(TOKENS, d)

    c_pad = jnp.pad(c, ((0, SUBLANES - BATCH), (0, 0)))
    mod = _ada(c_pad, w_ada[0], b_ada)[:BATCH].reshape(BATCH, 6, d)

    mixer_in = _mixer_inputs(w_in, w_gk_fwd, b_gk_fwd, w_gk_bwd, b_gk_bwd)
    q, k, v, gg, laf, lab, aq, ak2, avt = _inproj(x2, mod, g_pre_mix, *mixer_in)
    o_f, o_b = _gla(q, k, v, laf, lab)
    o_att = _attn(attn_sink[0], aq, ak2, avt)

    wr_t = w_router[0].T
    wrh = wr_t.astype(BF16)
    wrl = (wr_t - wrh.astype(F32)).astype(BF16)
    x1, h2_tiles, top_i, gates, rank, counts = _post(
        o_f, o_b, gg, o_att, x2, mod, g_gla_out, g_post_mix, g_pre_ffn, w_out[0].astype(BF16), wrh, wrl,
        b_router[0][:, None])

    pos, blocks = _route(top_i, rank, counts)

    src = _sc_source_rows(pos.reshape(TOP_K * TOKENS), MOE_ROWS)
    h2_rows = h2_tiles.reshape(TOKENS, PACK_SUB, LANES)
    ys = None
    first = 0
    for part_blocks in MOE_PART_BLOCKS:
        row0, n_rows = first * MOE_BM, part_blocks * MOE_BM
        xs_p = _sc_gather_rows(h2_rows, src[row0:row0 + n_rows], SC_GATHER_RING if first == 0 else SC_BACKGROUND_RING)
        ys = _experts(first, part_blocks, blocks, xs_p.reshape(n_rows * PACK_SUB, LANES),
                      w_gate_up[0], b_gate_up[0], w_down[0], b_down[0], ys)
        first += part_blocks
    assert first == MOE_NB

    y4 = _sc_gather_rows(ys.reshape(MOE_ROWS, PACK_SUB, LANES), pos.reshape(TOP_K * TOKENS))
    out = _combine(gates, x1, mod, g_post_ffn, y4.reshape(TOP_K * TOKENS * PACK_SUB, LANES))
    return out.reshape(BATCH, SEQ, d)
```

```python
import functools

import jax
import jax.numpy as jnp
import numpy as np
from jax import lax
from jax.experimental import pallas as pl
from jax.experimental.pallas import tpu as pltpu
from jax.experimental.pallas import tpu_sc as plsc

F32 = jnp.float32
BF16 = jnp.bfloat16
I32 = jnp.int32

D_MODEL = 1024
BATCH = 2
SEQ = 8192
TOKENS = BATCH * SEQ
GLA_HEADS = 4
GLA_DV = 128
GLA_DK = 64
GLA_RANK = 16
GLA_GATE_NORMALIZER = 16.0
GLA_CHUNK = 64
ATT_Q_HEADS = 8
ATT_KV_HEADS = 2
ATT_HEAD_DIM = 64
ATT_WINDOW = 128
ATT_BLOCK = 128
ROT_DIM = 16
ROPE_THETA = 500000.0
N_EXPERTS = 32
TOP_K = 4
D_FF = 1024
SWIGLU_LIMIT = 7.0
SWIGLU_ALPHA = 1.702
NORM_EPS = 1e-6
NEG_INF = -1e30
LOG2_E = 1.4426950408889634

LANES = 128
SUBLANES = 8
PACK_COLS = D_MODEL // 2
PACK_SUB = PACK_COLS // LANES

TM_IN = 1024
IN_SUB = 4
GLA_GROUP = 16
ATT_GROUP = 16
TM_POST = 1024
POST_SUB = 4
MOE_BM = 256
MOE_ROWS = TOKENS * TOP_K + N_EXPERTS * MOE_BM
MOE_NB = MOE_ROWS // MOE_BM
MOE_SUB = 4
MOE_PART_BLOCKS = (32, 80, 176)
MOE_NB_PAD = ((MOE_NB + LANES - 1) // LANES) * LANES
SC_SCAN_CHUNK = TOKENS * TOP_K
SC_SCAN_UNROLL = 8
TM_COMB = 1024
SC_GATHER_WINDOW = 32
SC_GATHER_RING = 4
SC_BACKGROUND_RING = 1

NT_DIMS = (((1,), (1,)), ((), ()))
TN_DIMS = (((0,), (0,)), ((), ()))


def _params(semantics, vmem_mib):
    return pltpu.CompilerParams(dimension_semantics=semantics, vmem_limit_bytes=vmem_mib * 1024 * 1024)


def _rms(x, g):
    return x * lax.rsqrt(jnp.mean(x * x, axis=-1, keepdims=True) + NORM_EPS) * g


def _silu(x):
    return x * jax.nn.sigmoid(x)


def _pack_rows(ref, v):
    m = v.shape[0]
    bits = lax.bitcast_convert_type(v, jnp.uint32)
    word = lax.bitcast_convert_type(bits[:, :PACK_COLS] | (bits[:, PACK_COLS:] >> 16), I32)
    for s in range(PACK_SUB):
        ref[pl.ds(s, m, stride=PACK_SUB), :] = word[:, s * LANES:(s + 1) * LANES]


def _unpack_rows(ref, m):
    word = jnp.concatenate([ref[pl.ds(s, m, stride=PACK_SUB), :] for s in range(PACK_SUB)], axis=1)
    bits = lax.bitcast_convert_type(word, jnp.uint32)
    hi = lax.bitcast_convert_type(bits & jnp.uint32(0xFFFF0000), F32)
    lo = lax.bitcast_convert_type(bits << 16, F32)
    return hi, lo


def _ada_body(c_ref, w_ref, b_ref, o_ref):
    ca = _silu(c_ref[...]).astype(BF16)
    o_ref[...] = jnp.dot(ca, w_ref[...].astype(BF16), preferred_element_type=F32) + b_ref[...]


def _ada(c_pad, w_ada, b_ada):
    d = D_MODEL
    return pl.pallas_call(
        _ada_body,
        grid=(6,),
        in_specs=[
            pl.BlockSpec((SUBLANES, d), lambda j: (0, 0)),
            pl.BlockSpec((d, d), lambda j: (0, j)),
            pl.BlockSpec((1, d), lambda j: (0, j)),
        ],
        out_specs=pl.BlockSpec((SUBLANES, d), lambda j: (0, j)),
        out_shape=jax.ShapeDtypeStruct((SUBLANES, 6 * d), F32),
        compiler_params=_params(("arbitrary",), 32),
        name="ada",
    )(c_pad, w_ada, b_ada)


def _rotary(x, cos_t, msin_t, psin_t):
    width = x.shape[1]
    reps = width // LANES
    c = jnp.concatenate([cos_t] * reps, axis=1)
    m = jnp.concatenate([msin_t] * reps, axis=1)
    p = jnp.concatenate([psin_t] * reps, axis=1)
    half = ROT_DIM // 2
    return x * c + pltpu.roll(x, width - half, 1) * m + pltpu.roll(x, half, 1) * p


def _inproj_body(x_ref, mod_ref, g_ref, wa_ref, wlr_ref, wgk_ref, bgk_ref, wb_ref, wvt_ref, rc_ref, rm_ref, rp_ref,
                 q_ref, k_ref, v_ref, gg_ref, laf_ref, lab_ref, aq_ref, ak_ref, avt_ref):
    shift = mod_ref[0:1, :]
    scale = mod_ref[1:2, :]
    hk = GLA_HEADS * GLA_DK
    hv = GLA_HEADS * GLA_DV
    nq = ATT_Q_HEADS * ATT_HEAD_DIM
    nk = 2 * ATT_KV_HEADS * ATT_HEAD_DIM
    sub = x_ref.shape[0] // IN_SUB
    subs = [slice(s * sub, (s + 1) * sub) for s in range(IN_SUB)]

    gain = g_ref[...] * (1.0 + scale)

    def hidden(rows):
        return (_rms(x_ref[rows, :], gain) + shift).astype(BF16)

    def project(h):
        return (jnp.dot(h, wa_ref[...], preferred_element_type=F32),
                jnp.dot(h, wlr_ref[...], preferred_element_type=F32),
                jnp.dot(h, wb_ref[...], preferred_element_type=F32),
                lax.dot_general(wvt_ref[...], h, NT_DIMS, preferred_element_type=F32))

    def finish(rows, pa, plr, pb, pvt):
        q_ref[rows, :] = pa[:, 0:hk] * (GLA_DK ** -0.5)
        k_ref[rows, :] = pa[:, hk:2 * hk]
        v_ref[rows, :] = pa[:, 2 * hk:2 * hk + hv].astype(BF16)
        gg_ref[rows, :] = pa[:, 2 * hk + hv:2 * hk + 2 * hv]
        gk = jnp.dot(plr.astype(BF16), wgk_ref[...], preferred_element_type=F32) + bgk_ref[...]
        la = (jnp.minimum(gk, 0.0) - jnp.log1p(jnp.exp(-jnp.abs(gk)))) * (1.0 / GLA_GATE_NORMALIZER)
        laf_ref[rows, :] = la[:, 0:hk]
        lab_ref[rows, :] = la[:, hk:2 * hk]
        rc, rm, rp = rc_ref[rows, :], rm_ref[rows, :], rp_ref[rows, :]
        aq_ref[rows, :] = (_rotary(pb[:, 0:nq], rc, rm, rp) * (ATT_HEAD_DIM ** -0.5 * LOG2_E)).astype(BF16)
        ak_ref[rows, :] = _rotary(pb[:, nq:nq + nk], rc, rm, rp).astype(BF16)
        avt_ref[:, rows] = pvt.astype(BF16)

    hs = [hidden(rows) for rows in subs]
    ps = [project(h) for h in hs]
    for rows, p in zip(subs, ps):
        finish(rows, *p)


def _inproj(x2, mod, g_pre, wa, wlr, wgk, bgk, wb, wvt, rc, rm, rp):
    t, d = x2.shape
    tm = TM_IN
    tiles_per_seq = SEQ // tm
    hk = GLA_HEADS * GLA_DK
    hv = GLA_HEADS * GLA_DV
    nq = ATT_Q_HEADS * ATT_HEAD_DIM
    nk = 2 * ATT_KV_HEADS * ATT_HEAD_DIM

    def full(a):
        return pl.BlockSpec(a.shape, lambda i: (0,) * a.ndim)

    def rows(w):
        return pl.BlockSpec((tm, w), lambda i: (i, 0))

    def table():
        return pl.BlockSpec((tm, LANES), lambda i: (i % tiles_per_seq, 0))

    out_widths = [(hk, F32), (hk, F32), (hv, BF16), (hv, F32), (hk, F32), (hk, F32), (nq, BF16), (nk, BF16)]
    return pl.pallas_call(
        _inproj_body,
        grid=(t // tm,),
        in_specs=[
            rows(d),
            pl.BlockSpec((None, 6, d), lambda i: (i // tiles_per_seq, 0, 0)),
            full(g_pre), full(wa), full(wlr), full(wgk), full(bgk), full(wb), full(wvt),
            table(), table(), table(),
        ],
        out_specs=[rows(w) for w, _ in out_widths] + [pl.BlockSpec((nk, tm), lambda i: (0, i))],
        out_shape=[jax.ShapeDtypeStruct((t, w), dt) for w, dt in out_widths] + [jax.ShapeDtypeStruct((nk, t), BF16)],
        compiler_params=_params(("arbitrary",), 56),
        name="inproj",
    )(x2, mod, g_pre, wa, wlr, wgk, bgk, wb, wvt, rc, rm, rp)


def _gla_body(qf_ref, kf_ref, vf_ref, laf_ref, qb_ref, kb_ref, vb_ref, lab_ref, of_ref, ob_ref, sf_ref, sb_ref):
    @pl.when(pl.program_id(1) == 0)
    def _():
        sf_ref[...] = jnp.zeros_like(sf_ref)
        sb_ref[...] = jnp.zeros_like(sb_ref)

    c = GLA_CHUNK
    r_i = lax.broadcasted_iota(I32, (c, c), 0)
    c_i = lax.broadcasted_iota(I32, (c, c), 1)
    lower = c_i <= r_i
    upper = c_i >= r_i
    cum_f = jnp.where(lower, 1.0, 0.0).astype(BF16)
    cum_b = jnp.where(upper, 1.0, 0.0).astype(BF16)
    lane = lax.broadcasted_iota(I32, (1, LANES), 1)
    head_masks = (lane < GLA_DK, lane >= GLA_DK)

    fwd = [(qf_ref, kf_ref, laf_ref, vf_ref, of_ref, slice(g * c, (g + 1) * c), cum_f, lower, c - 1, c // 2 - 1)
           for g in range(GLA_GROUP)]
    bwd = [(qb_ref, kb_ref, lab_ref, vb_ref, ob_ref, slice(g * c, (g + 1) * c), cum_b, upper, 0, c // 2)
           for g in reversed(range(GLA_GROUP))]
    heads = range(GLA_HEADS)
    pair = [slice((h // 2) * LANES, (h // 2 + 1) * LANES) for h in heads]
    vcols = [slice(h * GLA_DV, (h + 1) * GLA_DV) for h in heads]

    def stage1(item):
        q_ref, k_ref, la_ref, v_ref, o_ref, rows, cum, tri, i_last, i_mid = item
        la = la_ref[rows, :]
        hi = la.astype(BF16)
        lo = (la - hi.astype(F32)).astype(BF16)
        b = jnp.dot(cum, hi, preferred_element_type=F32) + jnp.dot(cum, lo, preferred_element_type=F32)
        b_last = b[i_last:i_last + 1, :]
        b_mid = b[i_mid:i_mid + 1, :]
        q, k = q_ref[rows, :], k_ref[rows, :]
        return (q * jnp.exp(b - b_mid), (k * jnp.exp(b_mid - b)).astype(BF16), q * jnp.exp(b),
                (k * jnp.exp(b_last - b)).astype(BF16), jnp.exp(b_last))

    def stage2(item, pre):
        v_ref, rows, tri = item[3], item[5], item[7]
        qs, ks, qi, kst, decay = pre
        out = []
        for h in heads:
            mask = head_masks[h % 2]
            qs_h = jnp.where(mask, qs[:, pair[h]], 0.0).astype(BF16)
            sc = lax.dot_general(qs_h, ks[:, pair[h]], NT_DIMS, preferred_element_type=F32)
            v_h = v_ref[rows, vcols[h]]
            kv = lax.dot_general(v_h, kst[:, pair[h]], TN_DIMS, preferred_element_type=F32)
            out.append((jnp.where(tri, sc, 0.0).astype(BF16), kv,
                        jnp.where(mask, qi[:, pair[h]], 0.0).astype(BF16), v_h))
        return out

    def run(items, s_ref):
        pre = [stage1(it) for it in items]
        mid = [stage2(it, p) for it, p in zip(items, pre)]
        states = [s_ref[h] for h in heads]
        for it, p, m in zip(items, pre, mid):
            o_ref, rows, decay = it[4], it[5], p[4]
            for h in heads:
                sc, kv, qi_h, v_h = m[h]
                o = jnp.dot(sc, v_h, preferred_element_type=F32)
                o = o + lax.dot_general(qi_h, states[h].astype(BF16), NT_DIMS, preferred_element_type=F32)
                o_ref[rows, vcols[h]] = o
                states[h] = states[h] * decay[:, pair[h]] + kv
        for h in heads:
            s_ref[h] = states[h]

    run(fwd, sf_ref)
    run(bwd, sb_ref)


def _gla(q, k, v, laf, lab):
    t = q.shape[0]
    rows = GLA_GROUP * GLA_CHUNK
    ng = SEQ // rows
    hk = GLA_HEADS * GLA_DK
    hv = GLA_HEADS * GLA_DV

    def fwd(w):
        return pl.BlockSpec((rows, w), lambda b, n: (b * ng + n, 0))

    def bwd(w):
        return pl.BlockSpec((rows, w), lambda b, n: (b * ng + ng - 1 - n, 0))

    return pl.pallas_call(
        _gla_body,
        grid=(BATCH, ng),
        in_specs=[fwd(hk), fwd(hk), fwd(hv), fwd(hk), bwd(hk), bwd(hk), bwd(hv), bwd(hk)],
        out_specs=[fwd(hv), bwd(hv)],
        out_shape=[jax.ShapeDtypeStruct((t, hv), F32)] * 2,
        scratch_shapes=[pltpu.VMEM((GLA_HEADS, GLA_DV, 2 * GLA_DK), F32)] * 2,
        compiler_params=_params(("arbitrary", "arbitrary"), 32),
        name="gla",
    )(q, k, v, laf, q, k, v, lab)


def _attn_body(sink_ref, q_ref, kp_ref, kc_ref, kn_ref, vp_ref, vc_ref, vn_ref, o_ref):
    step = pl.program_id(1)
    last = pl.num_programs(1) - 1
    qb = ATT_BLOCK
    hd = ATT_HEAD_DIM
    k_all = jnp.concatenate([kp_ref[...], kc_ref[...], kn_ref[...]], axis=0)
    vt_all = jnp.concatenate([vp_ref[...], vc_ref[...], vn_ref[...]], axis=1)
    lane = lax.broadcasted_iota(I32, (1, LANES), 1)
    lo = lane < hd
    j_k = lax.broadcasted_iota(I32, (3 * qb, qb), 0)
    i_q = lax.broadcasted_iota(I32, (3 * qb, qb), 1)
    band = jnp.abs(j_k - qb - i_q) <= ATT_WINDOW
    sinks = [jnp.concatenate([jnp.full((1, qb), sink_ref[4 * g + r] * LOG2_E, F32) for r in range(4)], axis=1)
             for g in range(ATT_KV_HEADS)]
    ones_rows = jnp.ones((SUBLANES, 3 * qb), BF16)
    work = [(j, g) for j in range(ATT_GROUP) for g in range(ATT_KV_HEADS)]

    def scores(j, g):
        valid = band
        if j == 0:
            valid = valid & ((j_k >= qb) | (step > 0))
        if j == ATT_GROUP - 1:
            valid = valid & ((j_k < 2 * qb) | (step < last))
        valid4 = jnp.concatenate([valid] * 4, axis=1)
        rows = slice(j * qb, (j + 1) * qb)
        kg = k_all[j * qb:(j + 3) * qb, g * LANES:(g + 1) * LANES]
        qa = q_ref[rows, (2 * g) * LANES:(2 * g + 1) * LANES]
        qc = q_ref[rows, (2 * g + 1) * LANES:(2 * g + 2) * LANES]
        zero = jnp.zeros_like(qa)
        lhs = jnp.concatenate([jnp.where(lo, qa, zero), jnp.where(lo, zero, qa),
                               jnp.where(lo, qc, zero), jnp.where(lo, zero, qc)], axis=0)
        st = lax.dot_general(kg, lhs, NT_DIMS, preferred_element_type=F32)
        return jnp.concatenate([jnp.where(valid4[0:qb], st[0:qb], NEG_INF), st[qb:2 * qb],
                                jnp.where(valid4[2 * qb:3 * qb], st[2 * qb:3 * qb], NEG_INF)], axis=0)

    def softmax(st, g):
        sink = sinks[g]
        m = jnp.maximum(jnp.max(st, axis=0, keepdims=True), sink)
        return jnp.exp2(st - m).astype(BF16), jnp.exp2(sink - m)

    def output(j, g, p, p_sink):
        rows = slice(j * qb, (j + 1) * qb)
        vgt = vt_all[g * LANES:(g + 1) * LANES, j * qb:(j + 3) * qb]
        res = jnp.dot(jnp.concatenate([vgt, ones_rows], axis=0), p, preferred_element_type=F32)
        ot = res[0:LANES] * (1.0 / (res[LANES:LANES + 1] + p_sink))
        pair_a = jnp.concatenate([ot[0:hd, 0:qb], ot[hd:2 * hd, qb:2 * qb]], axis=0)
        pair_c = jnp.concatenate([ot[0:hd, 2 * qb:3 * qb], ot[hd:2 * hd, 3 * qb:4 * qb]], axis=0)
        o_ref[rows, (2 * g) * LANES:(2 * g + 1) * LANES] = pair_a.T.astype(o_ref.dtype)
        o_ref[rows, (2 * g + 1) * LANES:(2 * g + 2) * LANES] = pair_c.T.astype(o_ref.dtype)

    s_all = [scores(j, g) for j, g in work]
    p_all = [softmax(st, g) for st, (j, g) in zip(s_all, work)]
    for (j, g), (p, p_sink) in zip(work, p_all):
        output(j, g, p, p_sink)


def _attn(sink, aq, ak2, avt):
    t = aq.shape[0]
    qb = ATT_BLOCK
    nb = SEQ // qb
    steps = nb // ATT_GROUP
    nq = ATT_Q_HEADS * ATT_HEAD_DIM
    nk = 2 * ATT_KV_HEADS * ATT_HEAD_DIM

    def edge_block(b, n, shift):
        return b * nb + jnp.clip(n * ATT_GROUP + shift, 0, nb - 1)

    def k_edge(shift):
        return pl.BlockSpec((qb, nk), lambda b, n: (edge_block(b, n, shift), 0))

    def v_edge(shift):
        return pl.BlockSpec((nk, qb), lambda b, n: (0, edge_block(b, n, shift)))

    def group(w):
        return pl.BlockSpec((ATT_GROUP * qb, w), lambda b, n: (b * steps + n, 0))

    v_group = pl.BlockSpec((nk, ATT_GROUP * qb), lambda b, n: (0, b * steps + n))
    return pl.pallas_call(
        _attn_body,
        grid=(BATCH, steps),
        in_specs=[
            pl.BlockSpec(memory_space=pltpu.SMEM),
            group(nq),
            k_edge(-1), group(nk), k_edge(ATT_GROUP), v_edge(-1), v_group, v_edge(ATT_GROUP),
        ],
        out_specs=group(nq),
        out_shape=jax.ShapeDtypeStruct((t, nq), BF16),
        compiler_params=_params(("arbitrary", "arbitrary"), 48),
        name="attn",
    )(sink, aq, ak2, ak2, ak2, avt, avt, avt)


def _post_body(of_ref, ob_ref, gg_ref, oa_ref, x_ref, mod_ref, ggla_ref, gpm_ref, gpf_ref, wout_ref,
               wrh_ref, wrl_ref, br_ref,
               x1_ref, h2_ref, ti_ref, gt_ref, rk_ref, cnt_ref, base_ref):
    tm = TM_POST

    @pl.when(pl.program_id(0) == 0)
    def _():
        base_ref[...] = jnp.zeros_like(base_ref)

    gain1 = mod_ref[2:3, :] * gpm_ref[...]
    shift2 = mod_ref[3:4, :]
    gain2 = gpf_ref[...] * (1.0 + mod_ref[4:5, :])
    sub = tm // POST_SUB
    subs = [slice(s * sub, (s + 1) * sub) for s in range(POST_SUB)]

    def mixer_out(rows):
        og = of_ref[rows, :] + ob_ref[rows, :]
        gg = gg_ref[rows, :]
        parts = []
        for h in range(GLA_HEADS):
            cols = slice(h * GLA_DV, (h + 1) * GLA_DV)
            parts.append((_rms(og[:, cols], ggla_ref[...]) * _silu(gg[:, cols])).astype(BF16))
        return jnp.concatenate(parts + [oa_ref[rows, :]], axis=1)

    def ffn_in(s, y):
        rows = subs[s]
        x1 = x_ref[rows, :] + _rms(y, gain1)
        x1_ref[rows, :] = x1
        h2 = _rms(x1, gain2) + shift2
        hi = h2.astype(BF16)
        hi32 = hi.astype(F32)
        _pack_rows(h2_ref.at[pl.ds(s * sub * PACK_SUB, sub * PACK_SUB), :], hi32)
        return hi, (h2 - hi32).astype(BF16)

    o_subs = [mixer_out(rows) for rows in subs]
    y_subs = [jnp.dot(o, wout_ref[...], preferred_element_type=F32) for o in o_subs]
    split = [ffn_in(s, y) for s, y in enumerate(y_subs)]
    h2_hi = jnp.concatenate([hi for hi, _ in split], axis=0)
    h2_lo = jnp.concatenate([lo for _, lo in split], axis=0)

    wrh = wrh_ref[...]
    logits = (lax.dot_general(wrh, h2_hi, NT_DIMS, preferred_element_type=F32)
              + lax.dot_general(wrh, h2_lo, NT_DIMS, preferred_element_type=F32)
              + lax.dot_general(wrl_ref[...], h2_hi, NT_DIMS, preferred_element_type=F32)
              + br_ref[...])
    e_iota = lax.broadcasted_iota(I32, (N_EXPERTS, tm), 0)
    idxs, vals = [], []
    work = logits
    for _ in range(TOP_K):
        m = jnp.max(work, axis=0, keepdims=True)
        idx = jnp.min(jnp.where(work == m, e_iota, N_EXPERTS), axis=0, keepdims=True)
        idxs.append(idx)
        vals.append(m)
        work = jnp.where(e_iota == idx, -jnp.inf, work)
    exps = [jnp.exp(v - vals[0]) for v in vals]
    inv = 1.0 / (exps[0] + exps[1] + exps[2] + exps[3])
    gt_ref[...] = jnp.concatenate([e * inv for e in exps], axis=0)
    ti_ref[...] = jnp.concatenate(idxs, axis=0)

    onehots = [e_iota == idx for idx in idxs]
    member = jnp.where(onehots[0] | onehots[1] | onehots[2] | onehots[3], 1.0, 0.0)
    t_row = lax.broadcasted_iota(I32, (tm, tm), 0)
    t_col = lax.broadcasted_iota(I32, (tm, tm), 1)
    strict = jnp.where(t_row < t_col, 1.0, 0.0).astype(BF16)
    before = base_ref[...] + jnp.dot(member.astype(BF16), strict, preferred_element_type=F32)
    rk_ref[...] = jnp.concatenate(
        [jnp.sum(jnp.where(oh, before, 0.0), axis=0, keepdims=True) for oh in onehots], axis=0).astype(I32)
    new_base = base_ref[...] + jnp.sum(member, axis=1, keepdims=True)
    base_ref[...] = new_base
    cnt_ref[...] = jnp.broadcast_to(new_base, cnt_ref.shape)


def _post(o_f, o_b, gg, o_att, x2, mod, g_gla, g_pm, g_pf, wout, wrh, wrl, br):
    t, d = x2.shape
    tm = TM_POST
    tiles_per_seq = SEQ // tm
    hv = GLA_HEADS * GLA_DV

    def full(a):
        return pl.BlockSpec(a.shape, lambda i: (0,) * a.ndim)

    def rows(w):
        return pl.BlockSpec((tm, w), lambda i: (i, 0))

    def lanes():
        return pl.BlockSpec((TOP_K, tm), lambda i: (0, i))

    return pl.pallas_call(
        _post_body,
        grid=(t // tm,),
        in_specs=[
            rows(hv), rows(hv), rows(hv), rows(hv), rows(d),
            pl.BlockSpec((None, 6, d), lambda i: (i // tiles_per_seq, 0, 0)),
            full(g_gla), full(g_pm), full(g_pf), full(wout), full(wrh), full(wrl), full(br),
        ],
        out_specs=[
            rows(d),
            pl.BlockSpec((tm * PACK_SUB, LANES), lambda i: (i, 0)),
            lanes(), lanes(), lanes(),
            pl.BlockSpec((N_EXPERTS, LANES), lambda i: (0, 0)),
        ],
        out_shape=[
            jax.ShapeDtypeStruct((t, d), F32),
            jax.ShapeDtypeStruct((t * PACK_SUB, LANES), I32),
            jax.ShapeDtypeStruct((TOP_K, t), I32),
            jax.ShapeDtypeStruct((TOP_K, t), F32),
            jax.ShapeDtypeStruct((TOP_K, t), I32),
            jax.ShapeDtypeStruct((N_EXPERTS, LANES), F32),
        ],
        scratch_shapes=[pltpu.VMEM((N_EXPERTS, 1), F32)],
        compiler_params=_params(("arbitrary",), 48),
        name="post",
    )(o_f, o_b, gg, o_att, x2, mod, g_gla, g_pm, g_pf, wout, wrh, wrl, br)


def _route_body(ti_ref, rk_ref, cnt_ref, pos_ref, blk_ref):
    cnt = cnt_ref[...]
    padded = jnp.floor((cnt + (MOE_BM - 1)) * (1.0 / MOE_BM)) * MOE_BM
    starts, ends = [], []
    acc = jnp.zeros((1, LANES), F32)
    for e in range(N_EXPERTS):
        starts.append(acc)
        acc = acc + padded[e:e + 1, :]
        ends.append(acc)
    ti = ti_ref[...]
    off = jnp.zeros(ti.shape, F32)
    for e in range(N_EXPERTS):
        off = jnp.where(ti == e, starts[e][:, 0:1], off)
    pos_ref[...] = rk_ref[...] + off.astype(I32)

    def owner_of(row):
        n_le = jnp.zeros(row.shape, I32)
        for e in range(N_EXPERTS):
            n_le = n_le + jnp.where(ends[e][:, 0:1] <= row, 1, 0)
        return jnp.minimum(n_le, N_EXPERTS - 1)

    block_start = lax.broadcasted_iota(I32, (1, MOE_NB_PAD), 1).astype(F32) * MOE_BM
    owner = owner_of(block_start)
    nxt = jnp.zeros((1, MOE_NB_PAD), I32)
    nxt_blk = jnp.zeros((1, MOE_NB_PAD), I32)
    slot = jnp.zeros((1, MOE_NB_PAD), I32)
    ordinal = jnp.zeros((1, 1), I32)
    for e in range(N_EXPERTS):
        end_e = ends[e][:, 0:1]
        nxt = jnp.where(owner == e, jnp.where(end_e < acc[:, 0:1], owner_of(end_e), -1), nxt)
        nxt_blk = jnp.where(owner == e, (end_e * (1.0 / MOE_BM)).astype(I32), nxt_blk)
        slot = jnp.where(owner == e, ordinal & 1, slot)
        ordinal = ordinal + jnp.where(padded[e:e + 1, 0:1] > 0, 1, 0)
    used = jnp.broadcast_to((acc[:, 0:1] * (1.0 / MOE_BM)).astype(I32), (1, MOE_NB_PAD))
    blk_ref[...] = jnp.concatenate(
        [owner, nxt, used, nxt_blk, slot, jnp.zeros((SUBLANES - 5, MOE_NB_PAD), I32)], axis=0)


def _route(top_i, rank, counts):
    return pl.pallas_call(
        _route_body,
        out_shape=[
            jax.ShapeDtypeStruct(top_i.shape, I32),
            jax.ShapeDtypeStruct((SUBLANES, MOE_NB_PAD), I32),
        ],
        compiler_params=pltpu.CompilerParams(vmem_limit_bytes=32 * 1024 * 1024),
        name="route",
    )(top_i, rank, counts)


def _sc_workers():
    info = plsc.get_sparse_core_info()
    return info.num_cores, info.num_subcores, info.num_lanes


def _sc_gather_loop(table_hbm, out_hbm, idx_v, base, chunks, bufs, gather_sems, write_sems):
    window = SC_GATHER_WINDOW
    ring = len(bufs)

    def fetch(c, b):
        return pltpu.make_async_copy(table_hbm.at[idx_v.at[pl.ds(c * window, window)]], bufs[b], gather_sems[b])

    def flush(c, b):
        return pltpu.make_async_copy(bufs[b], out_hbm.at[pl.ds(base + c * window, window)], write_sems[b])

    for b in range(ring):
        fetch(b, b).start()

    @pl.loop(0, chunks, step=ring)
    def _(c0):
        for b in range(ring):
            c = c0 + b
            fetch(c, b).wait()
            flush(c, b).start()

            @pl.when(c + ring < chunks)
            def _():
                flush(c, b).wait()
                fetch(c + ring, b).start()

    for b in range(ring):
        flush(chunks - ring + b, b).wait()


def _sc_source_rows(pos_flat, n_rows):
    cores, subcores, lanes = _sc_workers()
    workers = cores * subcores
    per_worker = n_rows // workers
    n_assign = pos_flat.shape[0]
    scan = SC_SCAN_CHUNK
    assert per_worker * workers == n_rows and per_worker % lanes == 0
    assert n_assign % scan == 0 and scan % lanes == 0 and TOKENS & (TOKENS - 1) == 0
    mesh = plsc.VectorSubcoreMesh(core_axis_name="core", subcore_axis_name="subcore")

    @functools.partial(
        pl.kernel,
        out_type=jax.ShapeDtypeStruct((n_rows,), I32),
        mesh=mesh,
        scratch_types=[pltpu.VMEM((per_worker,), I32), pltpu.VMEM((scan,), I32)],
        compiler_params=pltpu.CompilerParams(needs_layout_passes=False),
        name="sc_source_rows",
    )
    def invert(pos_hbm, out_hbm, src_v, pos_v):
        wid = lax.axis_index("subcore") * cores + lax.axis_index("core")
        base = wid * per_worker
        lane = lax.iota(I32, lanes)

        @pl.loop(0, per_worker, step=lanes)
        def _(j):
            src_v[pl.ds(j, lanes)] = (base + j + lane) & (TOKENS - 1)

        @pl.loop(0, n_assign, step=scan)
        def _(a0):
            pltpu.sync_copy(pos_hbm.at[pl.ds(a0, scan)], pos_v)

            @plsc.parallel_loop(0, scan, step=lanes, unroll=SC_SCAN_UNROLL)
            def _(j):
                rel = pos_v[pl.ds(j, lanes)] - base
                mine = (rel >= 0) & (rel < per_worker)
                tok = (a0 + j + lane) & (TOKENS - 1)
                plsc.store_scatter(src_v, [jnp.where(mine, rel, 0)], tok, mask=mine)

        pltpu.sync_copy(src_v, out_hbm.at[pl.ds(base, per_worker)])

    return invert(pos_flat)


def _sc_gather_rows(table, idx, ring=SC_GATHER_RING):
    cores, subcores, _ = _sc_workers()
    workers = cores * subcores
    n = idx.shape[0]
    window = SC_GATHER_WINDOW
    per_worker = n // workers
    chunks = per_worker // window
    assert per_worker * workers == n and chunks * window == per_worker and chunks % ring == 0
    row_shape = table.shape[1:]
    mesh = plsc.VectorSubcoreMesh(core_axis_name="core", subcore_axis_name="subcore")

    @functools.partial(
        pl.kernel,
        out_type=jax.ShapeDtypeStruct((n,) + row_shape, table.dtype),
        mesh=mesh,
        scratch_types=[pltpu.VMEM((per_worker,), I32)]
        + [pltpu.VMEM((window,) + row_shape, table.dtype)] * ring
        + [pltpu.SemaphoreType.DMA] * (2 * ring),
        name="sc_gather_rows",
    )
    def gather(table_hbm, idx_hbm, out_hbm, idx_v, *scratch):
        wid = lax.axis_index("subcore") * cores + lax.axis_index("core")
        base = wid * per_worker
        pltpu.sync_copy(idx_hbm.at[pl.ds(base, per_worker)], idx_v)
        _sc_gather_loop(table_hbm, out_hbm, idx_v, base, chunks,
                        scratch[:ring], scratch[ring:2 * ring], scratch[2 * ring:])

    return gather(table, idx)


def _experts_body(first, be_ref, nxt_ref, nxtblk_ref, slot_ref, nu_ref, xs_ref, bgu_ref, bd_ref, wgu_hbm, wd_hbm,
                  *rest):
    ys_ref, wgu_f32, wd_f32, sems = rest[-4:]
    step = pl.program_id(0)
    end = first + pl.num_programs(0) * MOE_SUB
    bm = MOE_BM
    n_used = nu_ref[0]

    def fetch(e, slot):
        return (pltpu.make_async_copy(wgu_hbm.at[e], wgu_f32.at[slot], sems.at[0, slot]),
                pltpu.make_async_copy(wd_hbm.at[e], wd_f32.at[slot], sems.at[1, slot]))

    @pl.when((step == 0) & (first < n_used))
    def _():
        for cp in fetch(be_ref[first], slot_ref[first]):
            cp.start()

    for sub in range(MOE_SUB):
        i = first + step * MOE_SUB + sub
        e = be_ref[i]
        slot = slot_ref[i]
        used = i < n_used
        fresh = (i == first) | (e != be_ref[jnp.maximum(i - 1, 0)])
        rows = pl.ds(sub * bm * PACK_SUB, bm * PACK_SUB)
        xs_sub = xs_ref.at[rows, :]
        ys_sub = ys_ref.at[rows, :]

        @pl.when(used & fresh)
        def _():
            for cp in fetch(e, slot):
                cp.wait()

            @pl.when((nxt_ref[i] >= 0) & (nxtblk_ref[i] < end))
            def _():
                for cp in fetch(nxt_ref[i], 1 - slot):
                    cp.start()

        @pl.when(used)
        def _():
            x = jnp.concatenate(_unpack_rows(xs_sub, bm), axis=1).astype(BF16)
            gu = jnp.dot(x, wgu_f32[slot].astype(BF16), preferred_element_type=F32) + bgu_ref[pl.ds(e, 1), :]
            gate = jnp.minimum(gu[:, 0:D_FF], SWIGLU_LIMIT)
            up = jnp.clip(gu[:, D_FF:2 * D_FF], -SWIGLU_LIMIT, SWIGLU_LIMIT)
            act = ((up + 1.0) * gate * jax.nn.sigmoid(SWIGLU_ALPHA * gate)).astype(BF16)
            y = jnp.dot(act, wd_f32[slot].astype(BF16), preferred_element_type=F32) + bd_ref[pl.ds(e, 1), :]
            _pack_rows(ys_sub, y.astype(BF16).astype(F32))

        @pl.when(jnp.logical_not(used))
        def _():
            ys_sub[...] = jnp.zeros((bm * PACK_SUB, LANES), I32)


def _experts(first, part_blocks, blocks, xs_part, w_gate_up, b_gate_up, w_down, b_down, ys_prev):
    rows = MOE_SUB * MOE_BM * PACK_SUB
    d = D_MODEL
    steps = part_blocks // MOE_SUB
    assert steps * MOE_SUB == part_blocks and first % MOE_SUB == 0

    def x_block(i, be, nx, nb, sl, nu):
        last = jnp.maximum((jnp.minimum(nu[0], first + part_blocks) - 1 - first) // MOE_SUB, 0)
        return jnp.minimum(i, last)

    in_specs = [
        pl.BlockSpec((rows, LANES), lambda i, be, nx, nb, sl, nu: (x_block(i, be, nx, nb, sl, nu), 0)),
        pl.BlockSpec(b_gate_up.shape, lambda i, be, nx, nb, sl, nu: (0, 0)),
        pl.BlockSpec(b_down.shape, lambda i, be, nx, nb, sl, nu: (0, 0)),
        pl.BlockSpec(memory_space=pl.ANY),
        pl.BlockSpec(memory_space=pl.ANY),
    ]
    operands = [blocks[0, :MOE_NB], blocks[1, :MOE_NB], blocks[3, :MOE_NB], blocks[4, :MOE_NB], blocks[2, :1],
                xs_part, b_gate_up, b_down, w_gate_up, w_down]
    aliases = {}
    if ys_prev is not None:
        in_specs.append(pl.BlockSpec(memory_space=pl.ANY))
        aliases = {len(operands): 0}
        operands.append(ys_prev)
    grid_spec = pltpu.PrefetchScalarGridSpec(
        num_scalar_prefetch=5,
        grid=(steps,),
        in_specs=in_specs,
        out_specs=pl.BlockSpec((rows, LANES), lambda i, be, nx, nb, sl, nu: (first // MOE_SUB + i, 0)),
        scratch_shapes=[
            pltpu.VMEM((2, d, 2 * D_FF), F32), pltpu.VMEM((2, D_FF, d), F32),
            pltpu.SemaphoreType.DMA((2, 2)),
        ],
    )
    return pl.pallas_call(
        functools.partial(_experts_body, first),
        grid_spec=grid_spec,
        out_shape=jax.ShapeDtypeStruct((MOE_ROWS * PACK_SUB, LANES), I32),
        input_output_aliases=aliases,
        compiler_params=_params(("arbitrary",), 48),
        name="experts",
    )(*operands)


def _combine_body(gates_ref, x1_ref, mod_ref, gpost_ref, y0_ref, y1_ref, y2_ref, y3_ref, o_ref):
    tm = TM_COMB
    gates = jnp.concatenate([gates_ref[...], jnp.zeros((SUBLANES - TOP_K, tm), F32)], axis=0).T
    y_hi = jnp.zeros((tm, PACK_COLS), F32)
    y_lo = jnp.zeros((tm, PACK_COLS), F32)
    for k, yk_ref in enumerate((y0_ref, y1_ref, y2_ref, y3_ref)):
        hi, lo = _unpack_rows(yk_ref, tm)
        y_hi = y_hi + hi * gates[:, k:k + 1]
        y_lo = y_lo + lo * gates[:, k:k + 1]
    y = jnp.concatenate([y_hi, y_lo], axis=1)
    gain = mod_ref[5:6, :] * gpost_ref[...]
    o_ref[...] = x1_ref[...] + _rms(y, gain)


def _combine(gates, x1, mod, g_post, y4):
    t, d = x1.shape
    tm = TM_COMB
    tiles = t // tm
    tiles_per_seq = SEQ // tm

    def slab(k):
        return pl.BlockSpec((tm * PACK_SUB, LANES), lambda i: (k * tiles + i, 0))

    return pl.pallas_call(
        _combine_body,
        grid=(tiles,),
        in_specs=[
            pl.BlockSpec((TOP_K, tm), lambda i: (0, i)),
            pl.BlockSpec((tm, d), lambda i: (i, 0)),
            pl.BlockSpec((None, 6, d), lambda i: (i // tiles_per_seq, 0, 0)),
            pl.BlockSpec(g_post.shape, lambda i: (0, 0)),
            slab(0), slab(1), slab(2), slab(3),
        ],
        out_specs=pl.BlockSpec((tm, d), lambda i: (i, 0)),
        out_shape=jax.ShapeDtypeStruct((t, d), F32),
        compiler_params=_params(("arbitrary",), 48),
        name="combine",
    )(gates, x1, mod, g_post, y4, y4, y4, y4)


def _rotary_tables():
    half = ROT_DIM // 2
    inv_freq = ROPE_THETA ** (-2.0 * np.arange(half, dtype=np.float32) / ROT_DIM)
    ang = np.arange(SEQ, dtype=np.float32)[:, None] * inv_freq[None, :].astype(np.float32)
    cos, sin = np.cos(ang), np.sin(ang)
    ones = np.ones((SEQ, ATT_HEAD_DIM - ROT_DIM), np.float32)
    zeros = np.zeros((SEQ, ATT_HEAD_DIM - ROT_DIM), np.float32)
    zh = np.zeros((SEQ, half), np.float32)
    reps = LANES // ATT_HEAD_DIM
    rc = np.tile(np.concatenate([cos, cos, ones], axis=1), (1, reps))
    rm = np.tile(np.concatenate([-sin, zh, zeros], axis=1), (1, reps))
    rp = np.tile(np.concatenate([zh, sin, zeros], axis=1), (1, reps))
    return tuple(jnp.asarray(t, F32) for t in (rc, rm, rp))


def _mixer_inputs(w_in, w_gk_fwd, b_gk_fwd, w_gk_bwd, b_gk_bwd):
    hk = GLA_HEADS * GLA_DK
    hv = GLA_HEADS * GLA_DV
    w = w_in[0]
    o_lr = 2 * hk + 2 * hv
    o_aq = o_lr + 2 * GLA_RANK
    o_ak = o_aq + ATT_Q_HEADS * ATT_HEAD_DIM
    o_av = o_ak + ATT_KV_HEADS * ATT_HEAD_DIM
    hd = ATT_HEAD_DIM
    wa = w[:, :o_lr].astype(BF16)
    wlr = w[:, o_lr:o_aq].astype(BF16)
    dup = lambda m: jnp.concatenate([m[:, g * hd:(g + 1) * hd] for g in range(ATT_KV_HEADS) for _ in range(2)], axis=1)
    wb = jnp.concatenate([w[:, o_aq:o_ak], dup(w[:, o_ak:o_av])], axis=1).astype(BF16)
    wvt = dup(w[:, o_av:o_av + ATT_KV_HEADS * hd]).T.astype(BF16)
    zr = jnp.zeros((GLA_RANK, hk), F32)
    wgk = jnp.concatenate([jnp.concatenate([w_gk_fwd[0], zr], axis=1),
                           jnp.concatenate([zr, w_gk_bwd[0]], axis=1)], axis=0).astype(BF16)
    bgk = jnp.concatenate([b_gk_fwd[0], b_gk_bwd[0]])[None, :]
    return (wa, wlr, wgk, bgk, wb, wvt) + _rotary_tables()


def kernel(x, c, w_ada, b_ada, g_pre_mix, g_post_mix, w_in, w_gk_fwd, b_gk_fwd, w_gk_bwd, b_gk_bwd, g_gla_out,
           attn_sink, w_out, g_pre_ffn, g_post_ffn, w_router, b_router, w_gate_up, b_gate_up, w_down, b_down):
    assert x.shape == (BATCH, SEQ, D_MODEL) and w_ada.shape[0] == 1
    d = D_MODEL
    x2 = x.reshape(TOKENS, d)

    c_pad = jnp.pad(c, ((0, SUBLANES - BATCH), (0, 0)))
    mod = _ada(c_pad, w_ada[0], b_ada)[:BATCH].reshape(BATCH, 6, d)

    mixer_in = _mixer_inputs(w_in, w_gk_fwd, b_gk_fwd, w_gk_bwd, b_gk_bwd)
    q, k, v, gg, laf, lab, aq, ak2, avt = _inproj(x2, mod, g_pre_mix, *mixer_in)
    o_f, o_b = _gla(q, k, v, laf, lab)
    o_att = _attn(attn_sink[0], aq, ak2, avt)

    wr_t = w_router[0].T
    wrh = wr_t.astype(BF16)
    wrl = (wr_t - wrh.astype(F32)).astype(BF16)
    x1, h2_tiles, top_i, gates, rank, counts = _post(
        o_f, o_b, gg, o_att, x2, mod, g_gla_out, g_post_mix, g_pre_ffn, w_out[0].astype(BF16), wrh, wrl,
        b_router[0][:, None])

    pos, blocks = _route(top_i, rank, counts)

    src = _sc_source_rows(pos.reshape(TOP_K * TOKENS), MOE_ROWS)
    h2_rows = h2_tiles.reshape(TOKENS, PACK_SUB, LANES)
    ys = None
    first = 0
    for part_blocks in MOE_PART_BLOCKS:
        row0, n_rows = first * MOE_BM, part_blocks * MOE_BM
        xs_p = _sc_gather_rows(h2_rows, src[row0:row0 + n_rows], SC_GATHER_RING if first == 0 else SC_BACKGROUND_RING)
        ys = _experts(first, part_blocks, blocks, xs_p.reshape(n_rows * PACK_SUB, LANES),
                      w_gate_up[0], b_gate_up[0], w_down[0], b_down[0], ys)
        first += part_blocks
    assert first == MOE_NB

    y4 = _sc_gather_rows(ys.reshape(MOE_ROWS, PACK_SUB, LANES), pos.reshape(TOP_K * TOKENS))
    out = _combine(gates, x1, mod, g_post_ffn, y4.reshape(TOP_K * TOKENS * PACK_SUB, LANES))
    return out.reshape(BATCH, SEQ, d)
```
